```python
import math
import jax, jax.numpy as jnp
from jax import lax
import numpy as np

D_MODEL = 2048
BATCH = 8
SEQ = 4096
DEPTH = 1

S5_WIDTH = D_MODEL // 2
S5_GROUP = 16
S5_GROUPS = S5_WIDTH // S5_GROUP
S5_STATE = 64
DT_MIN = 1e-3
DT_MAX = 1e-1
S5_MAX_RE = -1e-4
HGRN_WIDTH = D_MODEL // 2
HGRN_EXPAND = 128
HGRN_HEADS = HGRN_WIDTH // HGRN_EXPAND
HGRN_HEAD_DIM = HGRN_EXPAND
HGRN_CHUNK = 64
D_FF = 5632
CONV_WIDTH = 3
RMS_EPS = 1e-6
N_IN = S5_WIDTH + 4 * HGRN_WIDTH + 2 * D_MODEL

kernel_name = 'hybrid_s5_hgrn2_gated_merge_block'


def _rmsnorm(x, g):
    xf = x.astype(jnp.float32)
    y = xf * lax.rsqrt(jnp.mean(xf * xf, axis=-1, keepdims=True) + RMS_EPS)
    return (y * g.astype(jnp.float32)).astype(x.dtype)


def _in_splits():
    sizes = [S5_WIDTH, HGRN_WIDTH, HGRN_WIDTH, HGRN_WIDTH, HGRN_WIDTH, D_MODEL, D_MODEL]
    offs, acc = [], 0
    for s in sizes[:-1]:
        acc += s
        offs.append(acc)
    return offs


def _complex_affine_combine(first, second):
    a1r, a1i, b1r, b1i = first
    a2r, a2i, b2r, b2i = second
    return (a2r * a1r - a2i * a1i,
            a2r * a1i + a2i * a1r,
            a2r * b1r - a2i * b1i + b2r,
            a2r * b1i + a2i * b1r + b2i)


def _s5_branch(u, a_re, a_im, log_dt, b_re, b_im, c_re, c_im, d, w_glu, b_glu):
    bsz, L, _ = u.shape
    f32 = jnp.float32
    uf = u.astype(f32).reshape(bsz, L, S5_GROUPS, S5_GROUP)
    lam_re = jnp.minimum(a_re.astype(f32), S5_MAX_RE)
    lam_im = a_im.astype(f32)
    dt = jnp.exp(log_dt.astype(f32))[:, None]
    mag = jnp.exp(lam_re * dt)
    abar_re = mag * jnp.cos(lam_im * dt)
    abar_im = mag * jnp.sin(lam_im * dt)
    den = lam_re * lam_re + lam_im * lam_im
    nr = abar_re - 1.0
    ni = abar_im
    coef_re = (nr * lam_re + ni * lam_im) / den
    coef_im = (ni * lam_re - nr * lam_im) / den
    bu_re = jnp.einsum('blgc,gpc->blgp', uf, b_re.astype(f32))
    bu_im = jnp.einsum('blgc,gpc->blgp', uf, b_im.astype(f32))
    bb_re = coef_re * bu_re - coef_im * bu_im
    bb_im = coef_re * bu_im + coef_im * bu_re
    a_seq_re = jnp.broadcast_to(abar_re, (1, L, S5_GROUPS, S5_STATE))
    a_seq_im = jnp.broadcast_to(abar_im, (1, L, S5_GROUPS, S5_STATE))
    _, _, s_re, s_im = lax.associative_scan(
        _complex_affine_combine, (a_seq_re, a_seq_im, bb_re, bb_im), axis=1)
    y = (jnp.einsum('blgp,gcp->blgc', s_re, c_re.astype(f32))
         - jnp.einsum('blgp,gcp->blgc', s_im, c_im.astype(f32))
         + d.astype(f32) * uf)
    y = y.reshape(bsz, L, S5_WIDTH)
    z = jax.nn.gelu(y)
    z = z * jax.nn.sigmoid(z @ w_glu.astype(f32) + b_glu.astype(f32))
    return z.astype(u.dtype)


def _hgrn2_branch(q_in, f_in, i_in, g_in, lb, norm_g):
    bsz, L, _ = q_in.shape
    f32 = jnp.float32
    n_chunks = L // HGRN_CHUNK

    def heads(t):
        return t.reshape(bsz, n_chunks, HGRN_CHUNK, HGRN_HEADS, HGRN_HEAD_DIM).transpose(1, 0, 3, 2, 4)

    q = jax.nn.silu(q_in.astype(f32))
    zf = f_in.astype(f32)
    lbf = lb.astype(f32)
    log_f = jnp.logaddexp(jnp.log(lbf), jnp.log1p(-lbf) + jax.nn.log_sigmoid(zf))
    k = (1.0 - lbf) * jax.nn.sigmoid(-zf)
    v = i_in.astype(f32)
    causal = jnp.tril(jnp.ones((HGRN_CHUNK, HGRN_CHUNK), dtype=bool))

    def step(S, blk):
        qc, lfc, kc, vc = blk
        b = jnp.cumsum(lfc, axis=2)
        b_last = b[:, :, -1:, :]
        inter = jnp.einsum('bhtk,bhkv->bhtv', qc * jnp.exp(b), S)
        diff = b[:, :, :, None, :] - b[:, :, None, :, :]
        decay = jnp.exp(jnp.where(causal[None, None, :, :, None], diff, -jnp.inf))
        scores = jnp.einsum('bhtk,bhsk,bhtsk->bhts', qc, kc, decay)
        intra = jnp.einsum('bhts,bhsv->bhtv', scores, vc)
        S_new = (jnp.exp(b_last[:, :, 0, :])[..., None] * S
                 + jnp.einsum('bhsk,bhsv->bhkv', kc * jnp.exp(b_last - b), vc))
        return S_new, inter + intra

    S0 = jnp.zeros((bsz, HGRN_HEADS, HGRN_HEAD_DIM, HGRN_HEAD_DIM), f32)
    _, o = lax.scan(step, S0, (heads(q), heads(log_f), heads(k), heads(v)))
    o = o.transpose(1, 0, 3, 2, 4).reshape(bsz, L, HGRN_HEADS, HGRN_HEAD_DIM)
    o = o * lax.rsqrt(jnp.mean(o * o, axis=-1, keepdims=True) + RMS_EPS)
    o = o * norm_g.astype(f32).reshape(HGRN_HEADS, HGRN_HEAD_DIM)
    o = o.reshape(bsz, L, HGRN_WIDTH) * jax.nn.silu(g_in.astype(f32))
    return o.astype(q_in.dtype)


def _conv_glu_ffn(h, w_up, conv_w, conv_b, w_down):
    L = h.shape[1]
    up = h @ w_up
    up_pad = jnp.pad(up, ((0, 0), (CONV_WIDTH - 1, 0), (0, 0)))
    conv = conv_b
    for j in range(CONV_WIDTH):
        conv = conv + conv_w[j] * up_pad[:, j:j + L, :]
    gate, val = jnp.split(conv, 2, axis=-1)
    return (jax.nn.silu(gate) * val) @ w_down


def _fwd_setup_inputs(seed: int = 0) -> dict:
    key = jax.random.key(seed)
    ks = jax.random.split(key, 24)
    f32 = jnp.float32

    def nrm(k, shape, scale):
        return jax.random.normal(k, shape, f32) * scale

    G, P = S5_GROUPS, S5_STATE
    return {
        'x': nrm(ks[0], (BATCH, SEQ, D_MODEL), 1.0),
        'ln_mix_g': 1.0 + nrm(ks[1], (DEPTH, D_MODEL), 0.01),
        'w_in': nrm(ks[2], (DEPTH, D_MODEL, N_IN), D_MODEL ** -0.5),
        's5_a_re': -0.5 + nrm(ks[3], (DEPTH, G, P), 0.01),
        's5_a_im': math.pi * jnp.arange(P, dtype=f32) + nrm(ks[4], (DEPTH, G, P), 0.01),
        's5_log_dt': jax.random.uniform(ks[5], (DEPTH, G), f32, math.log(DT_MIN), math.log(DT_MAX)),
        's5_b_re': nrm(ks[6], (DEPTH, G, P, S5_GROUP), (2 * S5_GROUP) ** -0.5),
        's5_b_im': nrm(ks[7], (DEPTH, G, P, S5_GROUP), (2 * S5_GROUP) ** -0.5),
        's5_c_re': nrm(ks[8], (DEPTH, G, S5_GROUP, P), S5_STATE ** -0.5),
        's5_c_im': nrm(ks[9], (DEPTH, G, S5_GROUP, P), S5_STATE ** -0.5),
        's5_d': nrm(ks[10], (DEPTH, G, S5_GROUP), 1.0),
        's5_w_glu': nrm(ks[11], (DEPTH, S5_WIDTH, S5_WIDTH), S5_WIDTH ** -0.5),
        's5_b_glu': nrm(ks[12], (DEPTH, S5_WIDTH), 0.01),
        'w_proj_s5': nrm(ks[13], (DEPTH, S5_WIDTH, D_MODEL), S5_WIDTH ** -0.5),
        'hgrn_lb_logits': nrm(ks[14], (DEPTH + 1, HGRN_WIDTH), 0.1),
        'hgrn_norm_g': 1.0 + nrm(ks[15], (DEPTH, HGRN_WIDTH), 0.01),
        'w_proj_hgrn': nrm(ks[16], (DEPTH, HGRN_WIDTH, D_MODEL), HGRN_WIDTH ** -0.5),
        'w_out': nrm(ks[17], (DEPTH, D_MODEL, D_MODEL), D_MODEL ** -0.5),
        'ln_ffn_g': 1.0 + nrm(ks[18], (DEPTH, D_MODEL), 0.01),
        'w_up': nrm(ks[19], (DEPTH, D_MODEL, 2 * D_FF), D_MODEL ** -0.5),
        'conv_w': nrm(ks[20], (DEPTH, CONV_WIDTH, 2 * D_FF), CONV_WIDTH ** -0.5),
        'conv_b': nrm(ks[21], (DEPTH, 2 * D_FF), 0.01),
        'w_down': nrm(ks[22], (DEPTH, D_FF, D_MODEL), D_FF ** -0.5),
        'ln_final_g': 1.0 + nrm(ks[23], (D_MODEL,), 0.01),
    }


def _fwd_reference(x, ln_mix_g, w_in, s5_a_re, s5_a_im, s5_log_dt, s5_b_re, s5_b_im, s5_c_re, s5_c_im,
              s5_d, s5_w_glu, s5_b_glu, w_proj_s5, hgrn_lb_logits, hgrn_norm_g, w_proj_hgrn,
              w_out, ln_ffn_g, w_up, conv_w, conv_b, w_down, ln_final_g):
    lb_all = jnp.cumsum(jax.nn.softmax(hgrn_lb_logits.astype(jnp.float32), axis=0), axis=0)
    for l in range(DEPTH):
        h = _rmsnorm(x, ln_mix_g[l])
        proj = h @ w_in[l]
        u_s5, q_h, f_h, i_h, g_h, gate_s5, gate_hgrn = jnp.split(proj, _in_splits(), axis=-1)
        y_s5 = _s5_branch(u_s5, s5_a_re[l], s5_a_im[l], s5_log_dt[l], s5_b_re[l], s5_b_im[l],
                          s5_c_re[l], s5_c_im[l], s5_d[l], s5_w_glu[l], s5_b_glu[l]) @ w_proj_s5[l]
        y_hgrn = _hgrn2_branch(q_h, f_h, i_h, g_h, lb_all[l], hgrn_norm_g[l]) @ w_proj_hgrn[l]
        merged = jax.nn.sigmoid(gate_s5) * y_s5 + jax.nn.sigmoid(gate_hgrn) * y_hgrn
        x = x + merged @ w_out[l]
        x = x + _conv_glu_ffn(_rmsnorm(x, ln_ffn_g[l]), w_up[l], conv_w[l], conv_b[l], w_down[l])
    return _rmsnorm(x, ln_final_g)


import jax as _jax
import jax.numpy as _jnp

TWIN_FORMAT = 'train_step'
FWD_PARAMS = ['x', 'ln_mix_g', 'w_in', 's5_a_re', 's5_a_im', 's5_log_dt', 's5_b_re', 's5_b_im', 's5_c_re', 's5_c_im', 's5_d', 's5_w_glu', 's5_b_glu', 'w_proj_s5', 'hgrn_lb_logits', 'hgrn_norm_g', 'w_proj_hgrn', 'w_out', 'ln_ffn_g', 'w_up', 'conv_w', 'conv_b', 'w_down', 'ln_final_g']
TWIN_WEIGHTS = ['ln_mix_g', 'w_in', 's5_a_re', 's5_a_im', 's5_log_dt', 's5_b_re', 's5_b_im', 's5_c_re', 's5_c_im', 's5_d', 's5_w_glu', 's5_b_glu', 'w_proj_s5', 'hgrn_lb_logits', 'hgrn_norm_g', 'w_proj_hgrn', 'w_out', 'ln_ffn_g', 'w_up', 'conv_w', 'conv_b', 'w_down', 'ln_final_g']
TWIN_DIFF_INPUT = 'x'
TWIN_INPUTS = ['x', 'ln_mix_g', 'w_in', 's5_a_re', 's5_a_im', 's5_log_dt', 's5_b_re', 's5_b_im', 's5_c_re', 's5_c_im', 's5_d', 's5_w_glu', 's5_b_glu', 'w_proj_s5', 'hgrn_lb_logits', 'hgrn_norm_g', 'w_proj_hgrn', 'w_out', 'ln_ffn_g', 'w_up', 'conv_w', 'conv_b', 'w_down', 'ln_final_g', 'loss_target', 'm_ln_mix_g', 'm_w_in', 'm_s5_a_re', 'm_s5_a_im', 'm_s5_log_dt', 'm_s5_b_re', 'm_s5_b_im', 'm_s5_c_re', 'm_s5_c_im', 'm_s5_d', 'm_s5_w_glu', 'm_s5_b_glu', 'm_w_proj_s5', 'm_hgrn_lb_logits', 'm_hgrn_norm_g', 'm_w_proj_hgrn', 'm_w_out', 'm_ln_ffn_g', 'm_w_up', 'm_conv_w', 'm_conv_b', 'm_w_down', 'm_ln_final_g', 'v_ln_mix_g', 'v_w_in', 'v_s5_a_re', 'v_s5_a_im', 'v_s5_log_dt', 'v_s5_b_re', 'v_s5_b_im', 'v_s5_c_re', 'v_s5_c_im', 'v_s5_d', 'v_s5_w_glu', 'v_s5_b_glu', 'v_w_proj_s5', 'v_hgrn_lb_logits', 'v_hgrn_norm_g', 'v_w_proj_hgrn', 'v_w_out', 'v_ln_ffn_g', 'v_w_up', 'v_conv_w', 'v_conv_b', 'v_w_down', 'v_ln_final_g']
TWIN_OUTPUTS = ['loss', 'grad_x', 'grad_ln_mix_g', 'grad_w_in', 'grad_s5_a_re', 'grad_s5_a_im', 'grad_s5_log_dt', 'grad_s5_b_re', 'grad_s5_b_im', 'grad_s5_c_re', 'grad_s5_c_im', 'grad_s5_d', 'grad_s5_w_glu', 'grad_s5_b_glu', 'grad_w_proj_s5', 'grad_hgrn_lb_logits', 'grad_hgrn_norm_g', 'grad_w_proj_hgrn', 'grad_w_out', 'grad_ln_ffn_g', 'grad_w_up', 'grad_conv_w', 'grad_conv_b', 'grad_w_down', 'grad_ln_final_g', 'delta_ln_mix_g', 'delta_w_in', 'delta_s5_a_re', 'delta_s5_a_im', 'delta_s5_log_dt', 'delta_s5_b_re', 'delta_s5_b_im', 'delta_s5_c_re', 'delta_s5_c_im', 'delta_s5_d', 'delta_s5_w_glu', 'delta_s5_b_glu', 'delta_w_proj_s5', 'delta_hgrn_lb_logits', 'delta_hgrn_norm_g', 'delta_w_proj_hgrn', 'delta_w_out', 'delta_ln_ffn_g', 'delta_w_up', 'delta_conv_w', 'delta_conv_b', 'delta_w_down', 'delta_ln_final_g', 'new_m_ln_mix_g', 'new_m_w_in', 'new_m_s5_a_re', 'new_m_s5_a_im', 'new_m_s5_log_dt', 'new_m_s5_b_re', 'new_m_s5_b_im', 'new_m_s5_c_re', 'new_m_s5_c_im', 'new_m_s5_d', 'new_m_s5_w_glu', 'new_m_s5_b_glu', 'new_m_w_proj_s5', 'new_m_hgrn_lb_logits', 'new_m_hgrn_norm_g', 'new_m_w_proj_hgrn', 'new_m_w_out', 'new_m_ln_ffn_g', 'new_m_w_up', 'new_m_conv_w', 'new_m_conv_b', 'new_m_w_down', 'new_m_ln_final_g', 'new_v_ln_mix_g', 'new_v_w_in', 'new_v_s5_a_re', 'new_v_s5_a_im', 'new_v_s5_log_dt', 'new_v_s5_b_re', 'new_v_s5_b_im', 'new_v_s5_c_re', 'new_v_s5_c_im', 'new_v_s5_d', 'new_v_s5_w_glu', 'new_v_s5_b_glu', 'new_v_w_proj_s5', 'new_v_hgrn_lb_logits', 'new_v_hgrn_norm_g', 'new_v_w_proj_hgrn', 'new_v_w_out', 'new_v_ln_ffn_g', 'new_v_w_up', 'new_v_conv_w', 'new_v_conv_b', 'new_v_w_down', 'new_v_ln_final_g']
TWIN_LEAF_KINDS = {'loss': 'loss', 'grad_x': 'grad_x', 'grad_ln_mix_g': 'grad_w', 'grad_w_in': 'grad_w', 'grad_s5_a_re': 'grad_w', 'grad_s5_a_im': 'grad_w', 'grad_s5_log_dt': 'grad_w', 'grad_s5_b_re': 'grad_w', 'grad_s5_b_im': 'grad_w', 'grad_s5_c_re': 'grad_w', 'grad_s5_c_im': 'grad_w', 'grad_s5_d': 'grad_w', 'grad_s5_w_glu': 'grad_w', 'grad_s5_b_glu': 'grad_w', 'grad_w_proj_s5': 'grad_w', 'grad_hgrn_lb_logits': 'grad_w', 'grad_hgrn_norm_g': 'grad_w', 'grad_w_proj_hgrn': 'grad_w', 'grad_w_out': 'grad_w', 'grad_ln_ffn_g': 'grad_w', 'grad_w_up': 'grad_w', 'grad_conv_w': 'grad_w', 'grad_conv_b': 'grad_w', 'grad_w_down': 'grad_w', 'grad_ln_final_g': 'grad_w', 'delta_ln_mix_g': 'delta_w', 'delta_w_in': 'delta_w', 'delta_s5_a_re': 'delta_w', 'delta_s5_a_im': 'delta_w', 'delta_s5_log_dt': 'delta_w', 'delta_s5_b_re': 'delta_w', 'delta_s5_b_im': 'delta_w', 'delta_s5_c_re': 'delta_w', 'delta_s5_c_im': 'delta_w', 'delta_s5_d': 'delta_w', 'delta_s5_w_glu': 'delta_w', 'delta_s5_b_glu': 'delta_w', 'delta_w_proj_s5': 'delta_w', 'delta_hgrn_lb_logits': 'delta_w', 'delta_hgrn_norm_g': 'delta_w', 'delta_w_proj_hgrn': 'delta_w', 'delta_w_out': 'delta_w', 'delta_ln_ffn_g': 'delta_w', 'delta_w_up': 'delta_w', 'delta_conv_w': 'delta_w', 'delta_conv_b': 'delta_w', 'delta_w_down': 'delta_w', 'delta_ln_final_g': 'delta_w', 'new_m_ln_mix_g': 'new_m', 'new_m_w_in': 'new_m', 'new_m_s5_a_re': 'new_m', 'new_m_s5_a_im': 'new_m', 'new_m_s5_log_dt': 'new_m', 'new_m_s5_b_re': 'new_m', 'new_m_s5_b_im': 'new_m', 'new_m_s5_c_re': 'new_m', 'new_m_s5_c_im': 'new_m', 'new_m_s5_d': 'new_m', 'new_m_s5_w_glu': 'new_m', 'new_m_s5_b_glu': 'new_m', 'new_m_w_proj_s5': 'new_m', 'new_m_hgrn_lb_logits': 'new_m', 'new_m_hgrn_norm_g': 'new_m', 'new_m_w_proj_hgrn': 'new_m', 'new_m_w_out': 'new_m', 'new_m_ln_ffn_g': 'new_m', 'new_m_w_up': 'new_m', 'new_m_conv_w': 'new_m', 'new_m_conv_b': 'new_m', 'new_m_w_down': 'new_m', 'new_m_ln_final_g': 'new_m', 'new_v_ln_mix_g': 'new_v', 'new_v_w_in': 'new_v', 'new_v_s5_a_re': 'new_v', 'new_v_s5_a_im': 'new_v', 'new_v_s5_log_dt': 'new_v', 'new_v_s5_b_re': 'new_v', 'new_v_s5_b_im': 'new_v', 'new_v_s5_c_re': 'new_v', 'new_v_s5_c_im': 'new_v', 'new_v_s5_d': 'new_v', 'new_v_s5_w_glu': 'new_v', 'new_v_s5_b_glu': 'new_v', 'new_v_w_proj_s5': 'new_v', 'new_v_hgrn_lb_logits': 'new_v', 'new_v_hgrn_norm_g': 'new_v', 'new_v_w_proj_hgrn': 'new_v', 'new_v_w_out': 'new_v', 'new_v_ln_ffn_g': 'new_v', 'new_v_w_up': 'new_v', 'new_v_conv_w': 'new_v', 'new_v_conv_b': 'new_v', 'new_v_w_down': 'new_v', 'new_v_ln_final_g': 'new_v'}


def _forward(args):
    return _fwd_reference(*[args[k] for k in FWD_PARAMS])


def _output_shape():
    def fwd():
        inp = _fwd_setup_inputs(0)
        return _fwd_reference(*[inp[k] for k in FWD_PARAMS])
    out = _jax.eval_shape(fwd)
    return out.shape, out.dtype

N_MICROBATCH = 1
ADAM_LR = 0.001
ADAM_B1 = 0.9
ADAM_B2 = 0.999
ADAM_EPS = 1e-08
ADAM_WD = 0.01
ADAM_STEP = 10
PER_EXAMPLE_BATCH_AXIS = {'x': 0, 'loss_target': 0}
SHARED_INPUTS = []
_WEIGHT_DTYPES = {'ln_mix_g': _jnp.float32, 'w_in': _jnp.float32, 's5_a_re': _jnp.float32, 's5_a_im': _jnp.float32, 's5_log_dt': _jnp.float32, 's5_b_re': _jnp.float32, 's5_b_im': _jnp.float32, 's5_c_re': _jnp.float32, 's5_c_im': _jnp.float32, 's5_d': _jnp.float32, 's5_w_glu': _jnp.float32, 's5_b_glu': _jnp.float32, 'w_proj_s5': _jnp.float32, 'hgrn_lb_logits': _jnp.float32, 'hgrn_norm_g': _jnp.float32, 'w_proj_hgrn': _jnp.float32, 'w_out': _jnp.float32, 'ln_ffn_g': _jnp.float32, 'w_up': _jnp.float32, 'conv_w': _jnp.float32, 'conv_b': _jnp.float32, 'w_down': _jnp.float32, 'ln_final_g': _jnp.float32}
MOMENT_SCALE = {'ln_mix_g': 4.963729e-02, 'w_in': 2.339816e-02, 's5_a_re': 1.923701e-03, 's5_a_im': 1.992774e-03, 's5_log_dt': 1.297431e+00, 's5_b_re': 1.265154e-03, 's5_b_im': 1.266823e-03, 's5_c_re': 1.774026e-03, 's5_c_im': 1.792189e-03, 's5_d': 2.912608e-02, 's5_w_glu': 7.538014e-03, 's5_b_glu': 1.152464e-02, 'w_proj_s5': 1.763865e-02, 'hgrn_lb_logits': 3.915959e-03, 'hgrn_norm_g': 4.276771e-02, 'w_proj_hgrn': 3.007825e-02, 'w_out': 3.493326e-02, 'ln_ffn_g': 6.409234e-02, 'w_up': 2.709341e-02, 'conv_w': 2.737385e-02, 'conv_b': 2.610307e-02, 'w_down': 4.413927e-02, 'ln_final_g': 1.598668e+01}


def _to_microbatches(a, axis):
    t = _jnp.moveaxis(a, axis, 0)
    t = t.reshape((N_MICROBATCH, t.shape[0] // N_MICROBATCH) + t.shape[1:])
    return _jnp.moveaxis(t, 1, axis + 1)


def setup_inputs(seed: int = 0) -> dict:
    inp = _fwd_setup_inputs(seed)
    key = _jax.random.fold_in(_jax.random.key(seed), 7919)
    shape, _ = _output_shape()
    out = dict(inp)
    out["loss_target"] = _jax.random.normal(_jax.random.fold_in(key, 0), shape, _jnp.float32)
    for i, name in enumerate(TWIN_WEIGHTS):
        w = inp[name].astype(_jnp.float32)
        if MOMENT_SCALE is None:
            s = _jnp.sqrt(_jnp.mean(_jnp.square(w)) + 1e-30)
        else:
            s = MOMENT_SCALE[name]
        km, kv = _jax.random.split(_jax.random.fold_in(key, i + 1))
        out[name] = w
        out["m_" + name] = s * _jax.random.normal(km, w.shape, _jnp.float32)
        out["v_" + name] = (s * s) * _jax.random.uniform(kv, w.shape, _jnp.float32, 0.5, 1.5)
    if N_MICROBATCH > 1:
        for name, axis in PER_EXAMPLE_BATCH_AXIS.items():
            out[name] = _to_microbatches(out[name], axis)
    return {'x': out['x'], 'ln_mix_g': out['ln_mix_g'], 'w_in': out['w_in'], 's5_a_re': out['s5_a_re'], 's5_a_im': out['s5_a_im'], 's5_log_dt': out['s5_log_dt'], 's5_b_re': out['s5_b_re'], 's5_b_im': out['s5_b_im'], 's5_c_re': out['s5_c_re'], 's5_c_im': out['s5_c_im'], 's5_d': out['s5_d'], 's5_w_glu': out['s5_w_glu'], 's5_b_glu': out['s5_b_glu'], 'w_proj_s5': out['w_proj_s5'], 'hgrn_lb_logits': out['hgrn_lb_logits'], 'hgrn_norm_g': out['hgrn_norm_g'], 'w_proj_hgrn': out['w_proj_hgrn'], 'w_out': out['w_out'], 'ln_ffn_g': out['ln_ffn_g'], 'w_up': out['w_up'], 'conv_w': out['conv_w'], 'conv_b': out['conv_b'], 'w_down': out['w_down'], 'ln_final_g': out['ln_final_g'], 'loss_target': out['loss_target'], 'm_ln_mix_g': out['m_ln_mix_g'], 'm_w_in': out['m_w_in'], 'm_s5_a_re': out['m_s5_a_re'], 'm_s5_a_im': out['m_s5_a_im'], 'm_s5_log_dt': out['m_s5_log_dt'], 'm_s5_b_re': out['m_s5_b_re'], 'm_s5_b_im': out['m_s5_b_im'], 'm_s5_c_re': out['m_s5_c_re'], 'm_s5_c_im': out['m_s5_c_im'], 'm_s5_d': out['m_s5_d'], 'm_s5_w_glu': out['m_s5_w_glu'], 'm_s5_b_glu': out['m_s5_b_glu'], 'm_w_proj_s5': out['m_w_proj_s5'], 'm_hgrn_lb_logits': out['m_hgrn_lb_logits'], 'm_hgrn_norm_g': out['m_hgrn_norm_g'], 'm_w_proj_hgrn': out['m_w_proj_hgrn'], 'm_w_out': out['m_w_out'], 'm_ln_ffn_g': out['m_ln_ffn_g'], 'm_w_up': out['m_w_up'], 'm_conv_w': out['m_conv_w'], 'm_conv_b': out['m_conv_b'], 'm_w_down': out['m_w_down'], 'm_ln_final_g': out['m_ln_final_g'], 'v_ln_mix_g': out['v_ln_mix_g'], 'v_w_in': out['v_w_in'], 'v_s5_a_re': out['v_s5_a_re'], 'v_s5_a_im': out['v_s5_a_im'], 'v_s5_log_dt': out['v_s5_log_dt'], 'v_s5_b_re': out['v_s5_b_re'], 'v_s5_b_im': out['v_s5_b_im'], 'v_s5_c_re': out['v_s5_c_re'], 'v_s5_c_im': out['v_s5_c_im'], 'v_s5_d': out['v_s5_d'], 'v_s5_w_glu': out['v_s5_w_glu'], 'v_s5_b_glu': out['v_s5_b_glu'], 'v_w_proj_s5': out['v_w_proj_s5'], 'v_hgrn_lb_logits': out['v_hgrn_lb_logits'], 'v_hgrn_norm_g': out['v_hgrn_norm_g'], 'v_w_proj_hgrn': out['v_w_proj_hgrn'], 'v_w_out': out['v_w_out'], 'v_ln_ffn_g': out['v_ln_ffn_g'], 'v_w_up': out['v_w_up'], 'v_conv_w': out['v_conv_w'], 'v_conv_b': out['v_conv_b'], 'v_w_down': out['v_w_down'], 'v_ln_final_g': out['v_ln_final_g']}


def _loss(weights, diff, rest, loss_target):
    with _jax.named_scope("forward"):
        args = {**rest, TWIN_DIFF_INPUT: diff, **{k: w.astype(_WEIGHT_DTYPES[k]) for k, w in weights.items()}}
        y = _forward(args)
    with _jax.named_scope("loss_head"):
        err = _jnp.square(y.astype(_jnp.float32) - loss_target)
        return 0.5 * _jnp.sum(_jnp.mean(err, axis=-1)) if err.ndim else 0.5 * err


def _adamw(w, g, m, v):
    m = ADAM_B1 * m + (1.0 - ADAM_B1) * g
    v = ADAM_B2 * v + (1.0 - ADAM_B2) * _jnp.square(g)
    m_hat = m / (1.0 - ADAM_B1 ** ADAM_STEP)
    v_hat = v / (1.0 - ADAM_B2 ** ADAM_STEP)
    delta = -ADAM_LR * (m_hat / (_jnp.sqrt(v_hat) + ADAM_EPS) + ADAM_WD * w)
    return delta, m, v


def reference(x, ln_mix_g, w_in, s5_a_re, s5_a_im, s5_log_dt, s5_b_re, s5_b_im, s5_c_re, s5_c_im, s5_d, s5_w_glu, s5_b_glu, w_proj_s5, hgrn_lb_logits, hgrn_norm_g, w_proj_hgrn, w_out, ln_ffn_g, w_up, conv_w, conv_b, w_down, ln_final_g, loss_target, m_ln_mix_g, m_w_in, m_s5_a_re, m_s5_a_im, m_s5_log_dt, m_s5_b_re, m_s5_b_im, m_s5_c_re, m_s5_c_im, m_s5_d, m_s5_w_glu, m_s5_b_glu, m_w_proj_s5, m_hgrn_lb_logits, m_hgrn_norm_g, m_w_proj_hgrn, m_w_out, m_ln_ffn_g, m_w_up, m_conv_w, m_conv_b, m_w_down, m_ln_final_g, v_ln_mix_g, v_w_in, v_s5_a_re, v_s5_a_im, v_s5_log_dt, v_s5_b_re, v_s5_b_im, v_s5_c_re, v_s5_c_im, v_s5_d, v_s5_w_glu, v_s5_b_glu, v_w_proj_s5, v_hgrn_lb_logits, v_hgrn_norm_g, v_w_proj_hgrn, v_w_out, v_ln_ffn_g, v_w_up, v_conv_w, v_conv_b, v_w_down, v_ln_final_g):
    given = dict(x=x, ln_mix_g=ln_mix_g, w_in=w_in, s5_a_re=s5_a_re, s5_a_im=s5_a_im, s5_log_dt=s5_log_dt, s5_b_re=s5_b_re, s5_b_im=s5_b_im, s5_c_re=s5_c_re, s5_c_im=s5_c_im, s5_d=s5_d, s5_w_glu=s5_w_glu, s5_b_glu=s5_b_glu, w_proj_s5=w_proj_s5, hgrn_lb_logits=hgrn_lb_logits, hgrn_norm_g=hgrn_norm_g, w_proj_hgrn=w_proj_hgrn, w_out=w_out, ln_ffn_g=ln_ffn_g, w_up=w_up, conv_w=conv_w, conv_b=conv_b, w_down=w_down, ln_final_g=ln_final_g, loss_target=loss_target, m_ln_mix_g=m_ln_mix_g, m_w_in=m_w_in, m_s5_a_re=m_s5_a_re, m_s5_a_im=m_s5_a_im, m_s5_log_dt=m_s5_log_dt, m_s5_b_re=m_s5_b_re, m_s5_b_im=m_s5_b_im, m_s5_c_re=m_s5_c_re, m_s5_c_im=m_s5_c_im, m_s5_d=m_s5_d, m_s5_w_glu=m_s5_w_glu, m_s5_b_glu=m_s5_b_glu, m_w_proj_s5=m_w_proj_s5, m_hgrn_lb_logits=m_hgrn_lb_logits, m_hgrn_norm_g=m_hgrn_norm_g, m_w_proj_hgrn=m_w_proj_hgrn, m_w_out=m_w_out, m_ln_ffn_g=m_ln_ffn_g, m_w_up=m_w_up, m_conv_w=m_conv_w, m_conv_b=m_conv_b, m_w_down=m_w_down, m_ln_final_g=m_ln_final_g, v_ln_mix_g=v_ln_mix_g, v_w_in=v_w_in, v_s5_a_re=v_s5_a_re, v_s5_a_im=v_s5_a_im, v_s5_log_dt=v_s5_log_dt, v_s5_b_re=v_s5_b_re, v_s5_b_im=v_s5_b_im, v_s5_c_re=v_s5_c_re, v_s5_c_im=v_s5_c_im, v_s5_d=v_s5_d, v_s5_w_glu=v_s5_w_glu, v_s5_b_glu=v_s5_b_glu, v_w_proj_s5=v_w_proj_s5, v_hgrn_lb_logits=v_hgrn_lb_logits, v_hgrn_norm_g=v_hgrn_norm_g, v_w_proj_hgrn=v_w_proj_hgrn, v_w_out=v_w_out, v_ln_ffn_g=v_ln_ffn_g, v_w_up=v_w_up, v_conv_w=v_conv_w, v_conv_b=v_conv_b, v_w_down=v_w_down, v_ln_final_g=v_ln_final_g)
    weights = {n: given[n] for n in TWIN_WEIGHTS}
    shared = {n: given[n] for n in SHARED_INPUTS}
    per_example = {n: given[n] for n in ['x']}
    grad_fn = _jax.value_and_grad(_loss, argnums=(0, 1))

    def one_microbatch(ex, loss_target):
        ex = dict(ex)
        diff = ex.pop(TWIN_DIFF_INPUT)
        return grad_fn(weights, diff, {**shared, **ex}, loss_target)

    if N_MICROBATCH == 1:
        loss, (grad_w, grad_x) = one_microbatch(per_example, given["loss_target"])
    else:
        def body(carry, xs):
            loss_sum, grad_sum = carry
            l_k, (gw_k, gx_k) = one_microbatch(xs[0], xs[1])
            with _jax.named_scope("update"):
                return (loss_sum + l_k, _jax.tree.map(_jnp.add, grad_sum, gw_k)), gx_k

        init = (_jnp.zeros((), _jnp.float32), _jax.tree.map(_jnp.zeros_like, weights))
        (loss, grad_w), grad_x = _jax.lax.scan(body, init, (per_example, given["loss_target"]))
    with _jax.named_scope("update"):
        delta_w, new_m, new_v = {}, {}, {}
        for n in TWIN_WEIGHTS:
            delta_w[n], new_m[n], new_v[n] = _adamw(weights[n], grad_w[n], given["m_" + n], given["v_" + n])
    return (loss, grad_x, *[grad_w[n] for n in TWIN_WEIGHTS], *[delta_w[n] for n in TWIN_WEIGHTS],
            *[new_m[n] for n in TWIN_WEIGHTS], *[new_v[n] for n in TWIN_WEIGHTS])
```

```python
import functools

import jax
import jax.numpy as jnp
from jax import lax
from jax.experimental import pallas as pl
from jax.experimental.pallas import tpu as pltpu

_F32 = jnp.float32
_BF = jnp.bfloat16
_RMS_EPS = 1e-6
_S5_MAX_RE = -1e-4
_LR, _B1, _B2, _ADAM_EPS, _WD, _STEP = 0.001, 0.9, 0.999, 1e-08, 0.01, 10
_MESH = pl.DeviceIdType.MESH
_ANY = pl.BlockSpec(memory_space=pl.ANY)
_LANE = 128
_VMEM_LIMIT = 56 * 1024 * 1024
_CHUNK = 64
_S5_TB = 128
_S5_SET = 8
_NCHIP = 4
_NDEV = 8


def _call(body, **kw):
    return pl.pallas_call(body, **kw)


def _cparams(*sem):
    return pltpu.CompilerParams(dimension_semantics=sem, vmem_limit_bytes=_VMEM_LIMIT)


def _tile(n, pref, unit=_LANE):
    if n <= pref:
        return n
    t = (pref // unit) * unit
    while t >= unit:
        if n % t == 0:
            return t
        t -= unit
    raise ValueError(f"no tile for {n}")


_NN = ((1,), (0,))
_NT = ((1,), (1,))
_TN = ((0,), (0,))


def _dg(a, b, dims):
    return lax.dot_general(a.astype(_BF), b.astype(_BF), (dims, ((), ())), preferred_element_type=_F32)


@jax.custom_vjp
def _bdot(a, b):
    return _dg(a, b, _NN)


def _bdot_f(a, b):
    return _dg(a, b, _NN), (a, b)


def _bdot_b(res, g):
    a, b = res
    return _dg(g, b, _NT).astype(a.dtype), _dg(a, g, _TN).astype(b.dtype)


_bdot.defvjp(_bdot_f, _bdot_b)


@jax.custom_vjp
def _bdot_nt(a, b):
    return _dg(a, b, _NT)


def _bdot_nt_f(a, b):
    return _dg(a, b, _NT), (a, b)


def _bdot_nt_b(res, g):
    a, b = res
    return _dg(g, b, _NN).astype(a.dtype), _dg(g, a, _TN).astype(b.dtype)


_bdot_nt.defvjp(_bdot_nt_f, _bdot_nt_b)


@jax.custom_vjp
def _bdot_tn(a, b):
    return _dg(a, b, _TN)


def _bdot_tn_f(a, b):
    return _dg(a, b, _TN), (a, b)


def _bdot_tn_b(res, g):
    a, b = res
    return _dg(b, g, _NT).astype(a.dtype), _dg(a, g, _NN).astype(b.dtype)


_bdot_tn.defvjp(_bdot_tn_f, _bdot_tn_b)


def _shift_up(x, n):
    r = x.shape[0]
    row = lax.broadcasted_iota(jnp.int32, x.shape, 0)
    return jnp.where(row < r - n, pltpu.roll(x, r - n, 0), 0.0)


@functools.partial(jax.custom_vjp, nondiff_argnums=(1,))
def _shift_down(x, n):
    row = lax.broadcasted_iota(jnp.int32, x.shape, 0)
    return jnp.where(row >= n, pltpu.roll(x, n, 0), 0.0)


def _shift_down_f(x, n):
    return _shift_down(x, n), None


def _shift_down_b(n, _, g):
    return (_shift_up(g, n),)


_shift_down.defvjp(_shift_down_f, _shift_down_b)


def _sigmoid(x):
    return 1.0 / (1.0 + jnp.exp(-x))


def _silu(x):
    return x * _sigmoid(x)


def _gelu(x):
    return 0.5 * x * (1.0 + jnp.tanh(0.7978845608028654 * (x + 0.044715 * (x * x * x))))


def _rms_core(x, g):
    return x * lax.rsqrt(jnp.mean(x * x, axis=-1, keepdims=True) + _RMS_EPS) * g


def _mm(a, b, mode, out_dtype, *, tm, tn, tk, res=None, halves=None, name):
    if mode == "nn":
        m, k = a.shape
        a_spec = pl.BlockSpec((tm, tk), lambda i, j, kk: (i, kk))
        if b.ndim == 3:
            s, _, ns = b.shape
            n = s * ns
            npb = ns // tn
            b_spec = pl.BlockSpec((None, tk, tn), lambda i, j, kk: (j // npb, kk, j % npb))
        else:
            n = b.shape[1]
            b_spec = pl.BlockSpec((tk, tn), lambda i, j, kk: (kk, j))
        dims = _NN
    elif mode == "nt":
        m, k = a.shape
        a_spec = pl.BlockSpec((tm, tk), lambda i, j, kk: (i, kk))
        if b.ndim == 3:
            s, n, ks = b.shape
            kpb = ks // tk
            b_spec = pl.BlockSpec((None, tn, tk), lambda i, j, kk: (kk // kpb, j, kk % kpb))
        else:
            n = b.shape[0]
            b_spec = pl.BlockSpec((tn, tk), lambda i, j, kk: (j, kk))
        dims = _NT
    else:
        k, m = a.shape
        n = b.shape[1]
        a_spec = pl.BlockSpec((tk, tm), lambda i, j, kk: (kk, i))
        b_spec = pl.BlockSpec((tk, tn), lambda i, j, kk: (kk, j))
        dims = _TN
    nk = k // tk
    if halves is None:
        out_shape = jax.ShapeDtypeStruct((m, n), out_dtype)
        out_spec = pl.BlockSpec((tm, tn), lambda i, j, kk: (i, j))
    elif halves == "cols":
        c2 = n // (2 * _NCHIP)
        tpc = c2 // tn
        out_shape = jax.ShapeDtypeStruct((2, _NCHIP, m, c2), out_dtype)
        out_spec = pl.BlockSpec((None, None, tm, tn),
                                lambda i, j, kk: ((j // tpc) % 2, j // (2 * tpc), i, j % tpc))
    else:
        c2 = n // 2
        tpc = c2 // tn
        r = m // _NCHIP
        tpr = r // tm
        out_shape = jax.ShapeDtypeStruct((2, _NCHIP, r, c2), out_dtype)
        out_spec = pl.BlockSpec((None, None, tm, tn),
                                lambda i, j, kk: (j // tpc, i // tpr, i % tpr, j % tpc))
    has_res = res is not None

    def body(*refs):
        if has_res:
            a_ref, b_ref, r_ref, o_ref, acc_ref = refs
        else:
            a_ref, b_ref, o_ref, acc_ref = refs
        kk = pl.program_id(2)

        @pl.when(kk == 0)
        def _():
            acc_ref[...] = jnp.zeros_like(acc_ref)

        acc_ref[...] += _dg(a_ref[...], b_ref[...], dims)

        @pl.when(kk == nk - 1)
        def _():
            out = acc_ref[...]
            if has_res:
                out = out + r_ref[...]
            o_ref[...] = out.astype(out_dtype)

    in_specs = [a_spec, b_spec]
    args = [a, b]
    if has_res:
        in_specs.append(pl.BlockSpec((tm, tn), lambda i, j, kk: (i, j)))
        args.append(res)
    return _call(body, name=name, grid=(m // tm, n // tn, nk), in_specs=in_specs, out_specs=out_spec,
                 out_shape=out_shape, scratch_shapes=[pltpu.VMEM((tm, tn), _F32)],
                 compiler_params=_cparams("parallel", "parallel", "arbitrary"))(*args)


def _rowcall(fn, ins, in_specs, outs, out_specs, acc, *, grid, name):
    nin = len(ins)

    def body(*refs):
        vals = fn(*[r[...] for r in refs[:nin]])
        first = pl.program_id(1) == 0
        for k, (o_ref, v) in enumerate(zip(refs[nin:], vals)):
            if acc[k]:
                @pl.when(first)
                def _(o_ref=o_ref):
                    o_ref[...] = jnp.zeros_like(o_ref)
                o_ref[...] += v.astype(o_ref.dtype)
            else:
                o_ref[...] = v.astype(o_ref.dtype)

    return _call(body, name=name, grid=grid, in_specs=in_specs, out_specs=out_specs, out_shape=outs,
                 compiler_params=_cparams("parallel", "arbitrary"))(*ins)


def _rb(tm, w, cb=0):
    return pl.BlockSpec((tm, w), lambda j, i: (i, cb + j))


def _cb(w, cb=0):
    return pl.BlockSpec((1, w), lambda j, i: (0, cb + j))


def _full(shape):
    nd = len(shape)
    return pl.BlockSpec(shape, lambda j, i: (0,) * nd)


def _sds(shape, dtype):
    return jax.ShapeDtypeStruct(shape, dtype)


def _rms_fwd(x, g, name):
    t, d = x.shape
    tm = _tile(t, 512, 8)
    return _rowcall(lambda xb, gb: (_rms_core(xb, gb),), [x, g], [_rb(tm, d), _full((1, d))],
                    [_sds((t, d), _BF)], [_rb(tm, d)], [False], grid=(1, t // tm), name=name)[0]


def _rms_bwd(x, g, dh, dres, name):
    t, d = x.shape
    tm = _tile(t, 256, 8)

    def fn(xb, gb, dhb, drb):
        _, vjp = jax.vjp(_rms_core, xb, gb)
        dx, dg = vjp(dhb)
        dx = dx + drb
        return dx, dx, dg

    return _rowcall(fn, [x, g, dh, dres], [_rb(tm, d), _full((1, d)), _rb(tm, d), _rb(tm, d)],
                    [_sds((t, d), _F32), _sds((t, d), _BF), _sds((1, d), _F32)],
                    [_rb(tm, d), _rb(tm, d), _full((1, d))], [False, False, True],
                    grid=(1, t // tm), name=name)


def _loss_head(x3, tgt, g):
    t, d = x3.shape
    tm = _tile(t, 256, 8)

    def fn(xb, tb, gb):
        y, vjp = jax.vjp(_rms_core, xb, gb)
        e = y - tb
        part = 0.5 * jnp.sum(jnp.mean(e * e, axis=-1, keepdims=True), axis=0, keepdims=True)
        dx, dg = vjp(e * (1.0 / d))
        return jnp.broadcast_to(part, (1, _LANE)), dx, dx, dg

    return _rowcall(fn, [x3, tgt, g], [_rb(tm, d), _rb(tm, d), _full((1, d))],
                    [_sds((1, _LANE), _F32), _sds((t, d), _F32), _sds((t, d), _BF), _sds((1, d), _F32)],
                    [_full((1, _LANE)), _rb(tm, d), _rb(tm, d), _full((1, d))], [True, False, False, True],
                    grid=(1, t // tm), name="loss_head")


def _s5_disc(a_re, a_im, log_dt):
    lam_re = jnp.minimum(a_re, _S5_MAX_RE)
    lam_im = a_im
    dt = jnp.exp(log_dt)
    mag = jnp.exp(lam_re * dt)
    abar_re = mag * jnp.cos(lam_im * dt)
    abar_im = mag * jnp.sin(lam_im * dt)
    den = lam_re * lam_re + lam_im * lam_im
    nr = abar_re - 1.0
    ni = abar_im
    coef_re = (nr * lam_re + ni * lam_im) / den
    coef_im = (ni * lam_re - nr * lam_im) / den
    return abar_re, abar_im, coef_re, coef_im


def _s5_prep(a_re, a_im, log_dt):
    g, p = a_re.shape

    def body(ar, ai, ld, o0, o1, o2, o3):
        outs = _s5_disc(ar[...], ai[...], ld[...])
        for o, v in zip((o0, o1, o2, o3), outs):
            o[...] = v

    return _call(body, name="s5_prep", out_shape=[_sds((g, p), _F32)] * 4)(a_re, a_im, log_dt)


def _s5_prep_bwd(a_re, a_im, log_dt, cts):
    g, p = a_re.shape

    def body(ar, ai, ld, c0, c1, c2, c3, d0, d1, d2):
        _, vjp = jax.vjp(_s5_disc, ar[...], ai[...], ld[...])
        outs = vjp((c0[...], c1[...], c2[...], c3[...]))
        for o, v in zip((d0, d1, d2), outs):
            o[...] = v

    return _call(body, name="s5_prep_bwd",
                 out_shape=[_sds((g, p), _F32), _sds((g, p), _F32), _sds((g, 1), _F32)])(a_re, a_im, log_dt, *cts)


def _s5_block(u, car, cai, ar, ai, cr, ci, b_re, b_im, c_re, c_imn, dvec):
    tb = u.shape[0]
    bur = _bdot(u, b_re)
    bui = _bdot(u, b_im)
    sr = cr * bur - ci * bui
    si = cr * bui + ci * bur
    row = lax.broadcasted_iota(jnp.int32, sr.shape, 0)
    sr = sr + jnp.where(row == 0, ar * car - ai * cai, 0.0)
    si = si + jnp.where(row == 0, ar * cai + ai * car, 0.0)
    pr, pi = ar, ai
    sh = 1
    while sh < tb:
        dr = _shift_down(sr, sh)
        di = _shift_down(si, sh)
        sr, si = sr + pr * dr - pi * di, si + pr * di + pi * dr
        pr, pi = pr * pr - pi * pi, 2.0 * pr * pi
        sh *= 2
    y = _bdot(sr, c_re) + _bdot(si, c_imn) + dvec * u
    last = row == tb - 1
    ncr = jnp.sum(jnp.where(last, sr, 0.0), axis=0, keepdims=True)
    nci = jnp.sum(jnp.where(last, si, 0.0), axis=0, keepdims=True)
    return y, ncr, nci


def _s5_specs(tb, lw, nt, rev):
    tmap = (lambda t: nt - 1 - t) if rev else (lambda t: t)
    vec = pl.BlockSpec((1, lw), lambda s, t: (0, s))
    return dict(
        u=pl.BlockSpec((tb, _LANE), lambda s, t: (tmap(t), s)),
        car=pl.BlockSpec((None, 1, lw), lambda s, t: (tmap(t), 0, s)),
        vec=vec,
        bmat=pl.BlockSpec((None, _LANE, lw), lambda s, t: (s, 0, 0)),
        cmat=pl.BlockSpec((None, lw, _LANE), lambda s, t: (s, 0, 0)),
        dvec=pl.BlockSpec((1, _LANE), lambda s, t: (0, s)),
    )


def _s5_fwd(proj, par, t, w5):
    ar, ai, cr, ci, b_re, b_im, c_re, c_imn, dvec = par
    ns = w5 // _LANE
    lw = ar.shape[1] // ns
    tb = min(_S5_TB, t)
    nt = t // tb
    sp = _s5_specs(tb, lw, nt, False)

    def body(u_ref, ar_r, ai_r, cr_r, ci_r, bre_r, bim_r, cre_r, cim_r, d_r, y_ref, car_ref, cai_ref, s_r, s_i):
        @pl.when(pl.program_id(1) == 0)
        def _():
            s_r[...] = jnp.zeros_like(s_r)
            s_i[...] = jnp.zeros_like(s_i)

        car_ref[...] = s_r[...]
        cai_ref[...] = s_i[...]
        y, ncr, nci = _s5_block(u_ref[...], s_r[...], s_i[...], ar_r[...], ai_r[...], cr_r[...], ci_r[...],
                                bre_r[...], bim_r[...], cre_r[...], cim_r[...], d_r[...])
        y_ref[...] = y
        s_r[...] = ncr
        s_i[...] = nci

    return _call(body, name="s5_fwd", grid=(ns, nt),
                 in_specs=[sp["u"], sp["vec"], sp["vec"], sp["vec"], sp["vec"], sp["bmat"], sp["bmat"],
                           sp["cmat"], sp["cmat"], sp["dvec"]],
                 out_specs=[sp["u"], sp["car"], sp["car"]],
                 out_shape=[_sds((t, w5), _F32), _sds((nt, 1, ns * lw), _F32), _sds((nt, 1, ns * lw), _F32)],
                 scratch_shapes=[pltpu.VMEM((1, lw), _F32), pltpu.VMEM((1, lw), _F32)],
                 compiler_params=_cparams("parallel", "arbitrary"))(proj, ar, ai, cr, ci, b_re, b_im, c_re, c_imn, dvec)


def _s5_bwd(proj, dy, car, cai, par, t, w5):
    ar, ai, cr, ci, b_re, b_im, c_re, c_imn, dvec = par
    ns = w5 // _LANE
    lw = ar.shape[1] // ns
    tb = min(_S5_TB, t)
    nt = t // tb
    sp = _s5_specs(tb, lw, nt, True)

    def body(u_ref, dy_ref, car_ref, cai_ref, ar_r, ai_r, cr_r, ci_r, bre_r, bim_r, cre_r, cim_r, d_r,
             du_ref, g_ar, g_ai, g_cr, g_ci, g_bre, g_bim, g_cre, g_cim, g_d, ds_r, ds_i):
        accs = (g_ar, g_ai, g_cr, g_ci, g_bre, g_bim, g_cre, g_cim, g_d)

        @pl.when(pl.program_id(1) == 0)
        def _():
            ds_r[...] = jnp.zeros_like(ds_r)
            ds_i[...] = jnp.zeros_like(ds_i)
            for o in accs:
                o[...] = jnp.zeros_like(o)

        _, vjp = jax.vjp(_s5_block, u_ref[...], car_ref[...], cai_ref[...], ar_r[...], ai_r[...], cr_r[...],
                         ci_r[...], bre_r[...], bim_r[...], cre_r[...], cim_r[...], d_r[...])
        grads = vjp((dy_ref[...], ds_r[...], ds_i[...]))
        du_ref[...] = grads[0].astype(_BF)
        ds_r[...] = grads[1]
        ds_i[...] = grads[2]
        for o, gval in zip(accs, grads[3:]):
            o[...] += gval

    vec_o = _sds((1, ns * lw), _F32)
    return _call(body, name="s5_bwd", grid=(ns, nt),
                 in_specs=[sp["u"], sp["u"], sp["car"], sp["car"], sp["vec"], sp["vec"], sp["vec"], sp["vec"],
                           sp["bmat"], sp["bmat"], sp["cmat"], sp["cmat"], sp["dvec"]],
                 out_specs=[sp["u"], sp["vec"], sp["vec"], sp["vec"], sp["vec"], sp["bmat"], sp["bmat"],
                            sp["cmat"], sp["cmat"], sp["dvec"]],
                 out_shape=[_sds((t, w5), _BF), vec_o, vec_o, vec_o, vec_o,
                            _sds(b_re.shape, _F32), _sds(b_re.shape, _F32), _sds(c_re.shape, _F32),
                            _sds(c_re.shape, _F32), _sds((1, w5), _F32)],
                 scratch_shapes=[pltpu.VMEM((1, lw), _F32), pltpu.VMEM((1, lw), _F32)],
                 compiler_params=_cparams("parallel", "arbitrary"))(
                     proj, dy, car, cai, ar, ai, cr, ci, b_re, b_im, c_re, c_imn, dvec)


def _bd_in(b):
    g, p, c = b.shape
    s = g // _S5_SET
    b4 = b.reshape(s, _S5_SET, p, c).transpose(0, 1, 3, 2)
    eye = jnp.eye(_S5_SET, dtype=b.dtype)
    return (b4[:, :, :, None, :] * eye[None, :, None, :, None]).reshape(s, _S5_SET * c, _S5_SET * p)


def _bd_in_grad(d, p, c):
    s = d.shape[0]
    eye = jnp.eye(_S5_SET, dtype=d.dtype)
    d5 = d.reshape(s, _S5_SET, c, _S5_SET, p) * eye[None, :, None, :, None]
    return d5.sum(axis=3).transpose(0, 1, 3, 2).reshape(s * _S5_SET, p, c)


def _bd_out(cm):
    g, c, p = cm.shape
    s = g // _S5_SET
    c4 = cm.reshape(s, _S5_SET, c, p).transpose(0, 1, 3, 2)
    eye = jnp.eye(_S5_SET, dtype=cm.dtype)
    return (c4[:, :, :, None, :] * eye[None, :, None, :, None]).reshape(s, _S5_SET * p, _S5_SET * c)


def _bd_out_grad(d, p, c):
    s = d.shape[0]
    eye = jnp.eye(_S5_SET, dtype=d.dtype)
    d5 = d.reshape(s, _S5_SET, p, _S5_SET, c) * eye[None, :, None, :, None]
    return d5.sum(axis=3).transpose(0, 1, 3, 2).reshape(s * _S5_SET, c, p)


def _s5glu_fwd(y, wglu, bglu):
    t, w5 = y.shape
    tm = _tile(t, 256, 8)

    def fn(yb, wb, bb):
        z1 = _gelu(yb)
        a = _dg(z1, wb, _NN) + bb
        return (z1 * _sigmoid(a),)

    return _rowcall(fn, [y, wglu, bglu], [_rb(tm, w5), _full(wglu.shape), _full((1, w5))],
                    [_sds((t, w5), _BF)], [_rb(tm, w5)], [False], grid=(1, t // tm), name="s5glu_fwd")[0]


def _s5glu_bwd(y, dz, wglu, bglu):
    t, w5 = y.shape
    tm = _tile(t, 256, 8)

    def fn(yb, dzb, wb, bb):
        z1, gelu_vjp = jax.vjp(_gelu, yb)
        sig = _sigmoid(_dg(z1, wb, _NN) + bb)
        da = dzb * z1 * sig * (1.0 - sig)
        dz1 = dzb * sig + _dg(da, wb, _NT)
        (dy,) = gelu_vjp(dz1)
        return dy, _dg(z1, da, _TN), jnp.sum(da, axis=0, keepdims=True)

    return _rowcall(fn, [y, dz, wglu, bglu], [_rb(tm, w5), _rb(tm, w5), _full(wglu.shape), _full((1, w5))],
                    [_sds((t, w5), _F32), _sds((w5, w5), _F32), _sds((1, w5), _F32)],
                    [_rb(tm, w5), _full((w5, w5)), _full((1, w5))], [False, True, True],
                    grid=(1, t // tm), name="s5glu_bwd")


def _tri_mats(c, transposed):
    d0, d1 = (1, 0) if transposed else (0, 1)
    t = lax.broadcasted_iota(jnp.int32, (c, c), d0)
    j = lax.broadcasted_iota(jnp.int32, (c, c), d1)
    low = (j <= t).astype(_F32)
    mats = [low]
    for sh in (6, 5, 4):
        m = 1 << (sh - 1)
        r = ((t >> sh) << sh) + (m - 1)
        mats.append(low - (j <= r).astype(_F32))
    return [mm.astype(_BF) for mm in mats]


def _split_dot(mat, x):
    hi = x.astype(_BF)
    lo = (x - hi.astype(_F32)).astype(_BF)
    return (jnp.dot(mat, hi, preferred_element_type=_F32) + jnp.dot(mat, lo, preferred_element_type=_F32))


@jax.custom_vjp
def _decay_sums(lf):
    return tuple(_split_dot(mm, lf) for mm in _tri_mats(lf.shape[0], False))


def _decay_sums_f(lf):
    return _decay_sums(lf), None


def _decay_sums_b(_, gs):
    mats = _tri_mats(gs[0].shape[0], True)
    out = _split_dot(mats[0], gs[0])
    for mm, gval in zip(mats[1:], gs[1:]):
        out = out + _split_dot(mm, gval)
    return (out,)


_decay_sums.defvjp(_decay_sums_f, _decay_sums_b)


def _hgrn_chunk(qi, fi, vi, gi, st, lb, ng):
    c = qi.shape[0]
    row = lax.broadcasted_iota(jnp.int32, (c, 1), 0)
    q = _silu(qi)
    lf = jnp.log(lb + (1.0 - lb) * _sigmoid(fi))
    k = (1.0 - lb) * _sigmoid(-fi)
    b, p32, p16, p8 = _decay_sums(lf)
    btot = jnp.sum(lf, axis=0, keepdims=True)
    inter = _bdot_nt(q * jnp.exp(b), st)
    tt = lax.broadcasted_iota(jnp.int32, (c, c), 0)
    ss = lax.broadcasted_iota(jnp.int32, (c, c), 1)
    sc = jnp.zeros((c, c), _F32)
    for sh, p in ((6, p32), (5, p16), (4, p8)):
        upper = ((row >> (sh - 1)) & 1) == 1
        qm = jnp.where(upper, q * jnp.exp(jnp.where(upper, p, 0.0)), 0.0)
        km = jnp.where(upper, 0.0, k * jnp.exp(jnp.where(upper, 0.0, -p)))
        sc = sc + jnp.where((tt >> sh) == (ss >> sh), _bdot_nt(qm, km), 0.0)
    o = inter + _bdot(sc, vi)
    o = o + jnp.sum(q * k, axis=1, keepdims=True) * vi
    for d in range(1, 8):
        valid = (row & 7) >= d
        bs = _shift_down(b, d)
        ks = _shift_down(k, d)
        vs = _shift_down(vi, d)
        w = jnp.where(valid, q * jnp.exp(jnp.where(valid, b - bs, 0.0)) * ks, 0.0)
        o = o + jnp.sum(w, axis=1, keepdims=True) * vs
    st_new = st * jnp.exp(btot) + _bdot_tn(vi, k * jnp.exp(btot - b))
    on = o * lax.rsqrt(jnp.mean(o * o, axis=1, keepdims=True) + _RMS_EPS) * ng
    return on * _silu(gi), st_new


def _hgrn_in_specs(t, w5, hw):
    nh = hw // _LANE
    qb = w5 // _LANE
    return [pl.BlockSpec((t, _LANE), (lambda h, k=k: (0, qb + k * nh + h))) for k in range(4)]


def _hgrn_fwd(proj, lb, ng, t, w5, hw):
    assert _CHUNK == 64
    nh = hw // _LANE
    nc = t // _CHUNK
    vec = pl.BlockSpec((1, _LANE), lambda h: (0, h))

    def body(q_ref, f_ref, i_ref, g_ref, lb_ref, ng_ref, og_ref, st_ref, s_scr):
        s_scr[...] = jnp.zeros_like(s_scr)
        lbv = lb_ref[...]
        ngv = ng_ref[...]

        def step(ci, carry):
            r = pl.ds(pl.multiple_of(ci * _CHUNK, _CHUNK), _CHUNK)
            st_ref[ci] = s_scr[...]
            og, sn = _hgrn_chunk(q_ref[r, :], f_ref[r, :], i_ref[r, :], g_ref[r, :], s_scr[...], lbv, ngv)
            og_ref[r, :] = og.astype(_BF)
            s_scr[...] = sn
            return carry

        lax.fori_loop(0, nc, step, 0)

    return _call(body, name="hgrn_fwd", grid=(nh,), in_specs=_hgrn_in_specs(t, w5, hw) + [vec, vec],
                 out_specs=[pl.BlockSpec((t, _LANE), lambda h: (0, h)),
                            pl.BlockSpec((None, nc, _LANE, _LANE), lambda h: (h, 0, 0, 0))],
                 out_shape=[_sds((t, hw), _BF), _sds((nh, nc, _LANE, _LANE), _F32)],
                 scratch_shapes=[pltpu.VMEM((_LANE, _LANE), _F32)],
                 compiler_params=_cparams("parallel"))(proj, proj, proj, proj, lb, ng)


def _hgrn_bwd(proj, lb, ng, st_all, dog, t, w5, hw):
    nh = hw // _LANE
    nc = t // _CHUNK
    vec = pl.BlockSpec((1, _LANE), lambda h: (0, h))
    col = pl.BlockSpec((t, _LANE), lambda h: (0, h))

    def body(q_ref, f_ref, i_ref, g_ref, lb_ref, ng_ref, st_ref, dog_ref,
             dq_ref, df_ref, di_ref, dg_ref, dlb_ref, dng_ref, ds_scr):
        ds_scr[...] = jnp.zeros_like(ds_scr)
        dlb_ref[...] = jnp.zeros_like(dlb_ref)
        dng_ref[...] = jnp.zeros_like(dng_ref)
        lbv = lb_ref[...]
        ngv = ng_ref[...]

        def step(kk, carry):
            ci = nc - 1 - kk
            r = pl.ds(pl.multiple_of(ci * _CHUNK, _CHUNK), _CHUNK)
            _, vjp = jax.vjp(_hgrn_chunk, q_ref[r, :], f_ref[r, :], i_ref[r, :], g_ref[r, :], st_ref[ci], lbv, ngv)
            dq, df, di, dg, ds, dlb, dng = vjp((dog_ref[r, :], ds_scr[...]))
            dq_ref[r, :] = dq.astype(_BF)
            df_ref[r, :] = df.astype(_BF)
            di_ref[r, :] = di.astype(_BF)
            dg_ref[r, :] = dg.astype(_BF)
            ds_scr[...] = ds
            dlb_ref[...] += dlb
            dng_ref[...] += dng
            return carry

        lax.fori_loop(0, nc, step, 0)

    return _call(body, name="hgrn_bwd", grid=(nh,),
                 in_specs=_hgrn_in_specs(t, w5, hw) + [vec, vec,
                                                      pl.BlockSpec((None, nc, _LANE, _LANE), lambda h: (h, 0, 0, 0)), col],
                 out_specs=[col, col, col, col, vec, vec],
                 out_shape=[_sds((t, hw), _BF)] * 4 + [_sds((1, hw), _F32)] * 2,
                 scratch_shapes=[pltpu.VMEM((_LANE, _LANE), _F32)],
                 compiler_params=_cparams("parallel"))(proj, proj, proj, proj, lb, ng, st_all, dog)


def _lb_of(logits):
    mx = jnp.max(logits, axis=0, keepdims=True)
    e = jnp.exp(logits - mx)
    sm = e / jnp.sum(e, axis=0, keepdims=True)
    row = lax.broadcasted_iota(jnp.int32, logits.shape, 0)
    return jnp.sum(jnp.where(row == 0, sm, 0.0), axis=0, keepdims=True)


def _lb_prep(logits):
    def body(l_ref, o_ref):
        o_ref[...] = _lb_of(l_ref[...])

    return _call(body, name="lb_prep", out_shape=_sds((1, logits.shape[1]), _F32))(logits)


def _lb_prep_bwd(logits, dlb):
    def body(l_ref, d_ref, o_ref):
        _, vjp = jax.vjp(_lb_of, l_ref[...])
        o_ref[...] = vjp(d_ref[...])[0]

    return _call(body, name="lb_prep_bwd", out_shape=_sds(logits.shape, _F32))(logits, dlb)


def _merge_fwd(proj, ys, yh, t, d, gs_off):
    tm = _tile(t, 256, 8)
    w = _tile(d, 1024)
    nb = d // w

    def fn(gs, gh, a, b):
        return (_sigmoid(gs) * a + _sigmoid(gh) * b,)

    return _rowcall(fn, [proj, proj, ys, yh], [_rb(tm, w, gs_off // w), _rb(tm, w, gs_off // w + nb), _rb(tm, w), _rb(tm, w)],
                    [_sds((t, d), _BF)], [_rb(tm, w)], [False], grid=(nb, t // tm), name="merge_fwd")[0]


def _merge_bwd(proj, ys, yh, dm, t, d, gs_off):
    tm = _tile(t, 256, 8)
    w = _tile(d, 1024)
    nb = d // w

    def fn(gs, gh, a, b, g):
        s1 = _sigmoid(gs)
        s2 = _sigmoid(gh)
        return g * s1, g * s2, g * a * s1 * (1.0 - s1), g * b * s2 * (1.0 - s2)

    return _rowcall(fn, [proj, proj, ys, yh, dm],
                    [_rb(tm, w, gs_off // w), _rb(tm, w, gs_off // w + nb), _rb(tm, w), _rb(tm, w), _rb(tm, w)],
                    [_sds((t, d), _BF)] * 4, [_rb(tm, w)] * 4, [False] * 4, grid=(nb, t // tm), name="merge_bwd")


def _prev_rows(up_prev8, is_first):
    p1 = jnp.where(is_first, 0.0, up_prev8[7:8, :])
    p2 = jnp.where(is_first, 0.0, up_prev8[6:7, :])
    return p1, p2


def _causal_taps(cur, p1, p2):
    row = lax.broadcasted_iota(jnp.int32, cur.shape, 0)
    s1 = jnp.where(row == 0, p1, _shift_down(cur, 1))
    s2 = jnp.where(row == 0, p2, jnp.where(row == 1, p1, _shift_down(cur, 2)))
    return s1, s2


def _conv_specs(tm, w, nb_half, t):
    r8 = tm // 8
    cur_g = pl.BlockSpec((tm, w), lambda j, i: (i, j))
    cur_v = pl.BlockSpec((tm, w), lambda j, i: (i, nb_half + j))
    prev_g = pl.BlockSpec((8, w), lambda j, i: (jnp.maximum(i * r8 - 1, 0), j))
    prev_v = pl.BlockSpec((8, w), lambda j, i: (jnp.maximum(i * r8 - 1, 0), nb_half + j))
    w_g = pl.BlockSpec((3, w), lambda j, i: (0, j))
    w_v = pl.BlockSpec((3, w), lambda j, i: (0, nb_half + j))
    b_g = pl.BlockSpec((1, w), lambda j, i: (0, j))
    b_v = pl.BlockSpec((1, w), lambda j, i: (0, nb_half + j))
    return cur_g, cur_v, prev_g, prev_v, w_g, w_v, b_g, b_v


def _conv_of(cur, prev8, wt, bias, is_first):
    p1, p2 = _prev_rows(prev8, is_first)
    s1, s2 = _causal_taps(cur, p1, p2)
    return bias + wt[0:1, :] * s2 + wt[1:2, :] * s1 + wt[2:3, :] * cur


def _convact_fwd(up, cw, cb, t, dff):
    tm = _tile(t, 512, 8)
    w = _tile(dff, 1408)
    nbh = dff // w
    sp = _conv_specs(tm, w, nbh, t)

    def body(ug, uv, pg, pv, wg, wv, bg, bv, o_ref):
        first = pl.program_id(1) == 0
        gate = _conv_of(ug[...], pg[...], wg[...], bg[...], first)
        val = _conv_of(uv[...], pv[...], wv[...], bv[...], first)
        o_ref[...] = (_silu(gate) * val).astype(_BF)

    return _call(body, name="convact_fwd", grid=(nbh, t // tm), in_specs=list(sp),
                 out_specs=pl.BlockSpec((tm, w), lambda j, i: (i, j)), out_shape=_sds((t, dff), _BF),
                 compiler_params=_cparams("parallel", "arbitrary"))(up, up, up, up, cw, cw, cb, cb)


def _convact_bwd(up, dact, cw, cb, t, dff):
    tm = _tile(t, 256, 8)
    w = _tile(dff, 1408)
    nbh = dff // w
    r8 = tm // 8
    nt = t // tm
    last8 = t // 8 - 1

    def own(j):
        return j

    def partner(j):
        return (j + nbh) % (2 * nbh)

    def triple(col):
        return [pl.BlockSpec((tm, w), lambda j, i: (i, col(j))),
                pl.BlockSpec((8, w), lambda j, i: (jnp.minimum((i + 1) * r8, last8), col(j))),
                pl.BlockSpec((8, w), lambda j, i: (jnp.maximum(i * r8 - 1, 0), col(j)))]

    def body(uo, uon, uop, upp, upn, upv, wo_ref, wp_ref, bo_ref, bp_ref, da_ref, dan_ref, du_ref, dw_ref, db_ref):
        j = pl.program_id(0)
        i = pl.program_id(1)
        is_gate = j < nbh
        first = i == 0
        is_last = i == nt - 1
        cur = uo[...]
        conv_o = _conv_of(jnp.concatenate([cur, uon[...]], axis=0), uop[...], wo_ref[...], bo_ref[...], first)
        conv_p = _conv_of(jnp.concatenate([upp[...], upn[...]], axis=0), upv[...], wp_ref[...], bp_ref[...], first)
        da = jnp.concatenate([da_ref[...], dan_ref[...]], axis=0)
        gate = jnp.where(is_gate, conv_o, conv_p)
        val = jnp.where(is_gate, conv_p, conv_o)
        sg = _sigmoid(gate)
        dc = jnp.where(is_gate, da * val * sg * (1.0 + gate * (1.0 - sg)), da * gate * sg)
        row = lax.broadcasted_iota(jnp.int32, dc.shape, 0)
        dc = jnp.where(jnp.logical_and(row >= tm, is_last), 0.0, dc)
        wt = wo_ref[...]
        du = wt[2:3, :] * dc + wt[1:2, :] * _shift_up(dc, 1) + wt[0:1, :] * _shift_up(dc, 2)
        du_ref[...] = du[0:tm, :].astype(_BF)
        dcm = dc[0:tm, :]
        p1, p2 = _prev_rows(uop[...], first)
        s1, s2 = _causal_taps(cur, p1, p2)

        @pl.when(first)
        def _():
            dw_ref[...] = jnp.zeros_like(dw_ref)
            db_ref[...] = jnp.zeros_like(db_ref)

        dw_ref[0:1, :] += jnp.sum(dcm * s2, axis=0, keepdims=True)
        dw_ref[1:2, :] += jnp.sum(dcm * s1, axis=0, keepdims=True)
        dw_ref[2:3, :] += jnp.sum(dcm * cur, axis=0, keepdims=True)
        db_ref[...] += jnp.sum(dcm, axis=0, keepdims=True)

    def dcol(j):
        return j % nbh

    in_specs = (triple(own) + triple(partner)
                + [pl.BlockSpec((3, w), lambda j, i: (0, own(j))), pl.BlockSpec((3, w), lambda j, i: (0, partner(j))),
                   pl.BlockSpec((1, w), lambda j, i: (0, own(j))), pl.BlockSpec((1, w), lambda j, i: (0, partner(j))),
                   pl.BlockSpec((tm, w), lambda j, i: (i, dcol(j))),
                   pl.BlockSpec((8, w), lambda j, i: (jnp.minimum((i + 1) * r8, last8), dcol(j)))])
    return _call(body, name="convact_bwd", grid=(2 * nbh, nt), in_specs=in_specs,
                 out_specs=[pl.BlockSpec((tm, w), lambda j, i: (i, j)), pl.BlockSpec((3, w), lambda j, i: (0, j)),
                            pl.BlockSpec((1, w), lambda j, i: (0, j))],
                 out_shape=[_sds((t, 2 * dff), _BF), _sds((3, 2 * dff), _F32), _sds((1, 2 * dff), _F32)],
                 compiler_params=_cparams("parallel", "arbitrary"))(
                     up, up, up, up, up, up, cw, cw, cb, cb, dact, dact)


def _me():
    return lax.axis_index("x"), lax.axis_index("y"), lax.axis_index("c")


def _other_chips(x, y):
    return [(1 - x, y), (x, 1 - y), (1 - x, 1 - y)]


def _rcopy(src, dst, ssem, rsem, dev):
    return pltpu.make_async_remote_copy(src_ref=src, dst_ref=dst, send_sem=ssem, recv_sem=rsem,
                                        device_id=dev, device_id_type=_MESH)


def _gather_weights(shards, direct):
    n, nd = len(shards), len(direct)

    def body(*refs):
        ins, dins = refs[:n], refs[n:n + nd]
        outs, douts = refs[n + nd:2 * n + nd], refs[2 * n + nd:2 * n + 2 * nd]
        ssem, rsem, fssem, frsem, lsem, dssem, drsem, dlsem = refs[2 * n + 2 * nd:]
        x, y, c = _me()
        me = 2 * x + y
        chips = _other_chips(x, y)
        sib = (x, y, 1 - c)

        def piece(a, chip, h):
            r2 = shards[a].shape[0] // 2
            return outs[a].at[chip, pl.ds(h * r2, r2)]

        def mine(a):
            r2 = shards[a].shape[0] // 2
            return ins[a].at[pl.ds(c * r2, r2)]

        waits = []
        for a in range(n):
            lc = pltpu.make_async_copy(ins[a], outs[a].at[me], lsem.at[a])
            lc.start()
            waits.append(lc.wait)
            for j, (cx, cy) in enumerate(chips):
                cp = _rcopy(mine(a), piece(a, me, c), ssem.at[3 * a + j], rsem.at[3 * a + j], (cx, cy, c))
                cp.start()
                waits.append(cp.wait_send)
        for a in range(nd):
            lc = pltpu.make_async_copy(dins[a], douts[a].at[me], dlsem.at[a])
            lc.start()
            waits.append(lc.wait)
            for j, (cx, cy) in enumerate(chips):
                cp = _rcopy(dins[a], douts[a].at[me], dssem.at[3 * a + j], drsem.at[3 * a + j], (cx, cy, c))
                cp.start()
                waits.append(cp.wait_send)
        for a in range(n):
            for j, (cx, cy) in enumerate(chips):
                pc = 2 * cx + cy
                _rcopy(mine(a), piece(a, pc, c), ssem.at[3 * a + j], rsem.at[3 * a + j], (cx, cy, c)).wait_recv()
                fw = _rcopy(piece(a, pc, c), piece(a, pc, c), fssem.at[3 * a + j], frsem.at[3 * a + j], sib)
                fw.start()
                waits.append(fw.wait_send)
        for a in range(n):
            for j, (cx, cy) in enumerate(chips):
                pc = 2 * cx + cy
                _rcopy(piece(a, pc, 1 - c), piece(a, pc, 1 - c), fssem.at[3 * a + j], frsem.at[3 * a + j], sib).wait_recv()
        for a in range(nd):
            for j, (cx, cy) in enumerate(chips):
                pc = 2 * cx + cy
                _rcopy(dins[a], douts[a].at[pc], dssem.at[3 * a + j], drsem.at[3 * a + j], (cx, cy, c)).wait_recv()
        for w in waits:
            w()

    dma = pltpu.SemaphoreType.DMA
    return _call(body, name="gather_weights", in_specs=[_ANY] * (n + nd), out_specs=[_ANY] * (n + nd),
                 out_shape=[_sds((_NCHIP,) + s.shape, s.dtype) for s in list(shards) + list(direct)],
                 scratch_shapes=[dma((3 * n,)), dma((3 * n,)), dma((3 * n,)), dma((3 * n,)), dma((n,)),
                                 dma((3 * nd,)), dma((3 * nd,)), dma((nd,))])(*shards, *direct)


def _swap_halves(grads):
    n = len(grads)

    def body(*refs):
        ins, outs = refs[:n], refs[n:2 * n]
        ssem, rsem = refs[2 * n:]
        x, y, c = _me()
        sib = (x, y, 1 - c)
        cps = []
        for a in range(n):
            cp = _rcopy(ins[a].at[1 - c], outs[a], ssem.at[a], rsem.at[a], sib)
            cp.start()
            cps.append(cp)
        for cp in cps:
            cp.wait_recv()
        for cp in cps:
            cp.wait_send()

    dma = pltpu.SemaphoreType.DMA
    return _call(body, name="swap_halves", in_specs=[_ANY] * n, out_specs=[_ANY] * n,
                 out_shape=[_sds(g.shape[1:], g.dtype) for g in grads],
                 scratch_shapes=[dma((n,)), dma((n,))])(*grads)


def _add_pairs(mine, theirs):
    s, r, c2 = mine.shape
    a2 = mine.reshape(s * r, c2)
    b2 = theirs.reshape(s * r, c2)
    tm = _tile(s * r, 512, 16)
    out = _rowcall(lambda p, q: (p.astype(_F32) + q.astype(_F32),), [a2, b2], [_rb(tm, c2), _rb(tm, c2)],
                   [_sds((s * r, c2), _BF)], [_rb(tm, c2)], [False], grid=(1, s * r // tm), name="chip_sum")[0]
    return out.reshape(s, r, c2)


def _exchange_shards(sums, small):
    n = len(sums)

    def body(*refs):
        ins, sm = refs[:n], refs[n]
        outs, smo = refs[n + 1:2 * n + 1], refs[2 * n + 1]
        ssem, rsem, lsem, sssem, srsem, slsem = refs[2 * n + 2:]
        x, y, c = _me()
        me = 2 * x + y
        dev = 4 * x + 2 * y + c
        chips = _other_chips(x, y)
        waits = []
        for a in range(n):
            lc = pltpu.make_async_copy(ins[a].at[me], outs[a].at[me], lsem.at[a])
            lc.start()
            waits.append(lc.wait)
            for j, (cx, cy) in enumerate(chips):
                pc = 2 * cx + cy
                cp = _rcopy(ins[a].at[pc], outs[a].at[me], ssem.at[3 * a + j], rsem.at[3 * a + j], (cx, cy, c))
                cp.start()
                waits.append(cp.wait_send)
        lc = pltpu.make_async_copy(sm, smo.at[dev], slsem.at[0])
        lc.start()
        waits.append(lc.wait)
        peers = []
        for k in range(1, _NDEV):
            px = 1 - x if k & 4 else x
            py = 1 - y if k & 2 else y
            pc_ = 1 - c if k & 1 else c
            peers.append((px, py, pc_))
            cp = _rcopy(sm, smo.at[dev], sssem.at[k - 1], srsem.at[k - 1], (px, py, pc_))
            cp.start()
            waits.append(cp.wait_send)
        for a in range(n):
            for j, (cx, cy) in enumerate(chips):
                pc = 2 * cx + cy
                _rcopy(ins[a].at[pc], outs[a].at[pc], ssem.at[3 * a + j], rsem.at[3 * a + j], (cx, cy, c)).wait_recv()
        for k, (px, py, pc_) in enumerate(peers):
            _rcopy(sm, smo.at[4 * px + 2 * py + pc_], sssem.at[k], srsem.at[k], (px, py, pc_)).wait_recv()
        for w in waits:
            w()

    dma = pltpu.SemaphoreType.DMA
    return _call(body, name="exchange_shards", in_specs=[_ANY] * (n + 1), out_specs=[_ANY] * (n + 1),
                 out_shape=[_sds(s.shape, s.dtype) for s in sums] + [_sds((_NDEV,) + small.shape, small.dtype)],
                 scratch_shapes=[dma((3 * n,)), dma((3 * n,)), dma((n,)), dma((_NDEV - 1,)), dma((_NDEV - 1,)),
                                 dma((1,))])(*sums, small)


def _sum_slots(stack, name):
    k, r, c = stack.shape
    tm = _tile(r, 256, 16 if stack.dtype == _BF else 8)

    def fn(v):
        out = v[0].astype(_F32)
        for i in range(1, k):
            out = out + v[i].astype(_F32)
        return (out,)

    return _rowcall(fn, [stack], [pl.BlockSpec((k, tm, c), lambda j, i: (0, i, 0))], [_sds((r, c), _F32)],
                    [_rb(tm, c)], [False], grid=(1, r // tm), name=name)[0]


def _share_halves(halves):
    n = len(halves)

    def body(*refs):
        ins, outs = refs[:n], refs[n:2 * n]
        ssem, rsem, lsem = refs[2 * n:]
        x, y, c = _me()
        sib = (x, y, 1 - c)
        waits = []
        for a in range(n):
            lc = pltpu.make_async_copy(ins[a], outs[a].at[c], lsem.at[a])
            lc.start()
            waits.append(lc.wait)
            cp = _rcopy(ins[a], outs[a].at[c], ssem.at[a], rsem.at[a], sib)
            cp.start()
            waits.append(cp.wait_send)
        for a in range(n):
            _rcopy(ins[a], outs[a].at[1 - c], ssem.at[a], rsem.at[a], sib).wait_recv()
        for w in waits:
            w()

    dma = pltpu.SemaphoreType.DMA
    return _call(body, name="share_halves", in_specs=[_ANY] * n, out_specs=[_ANY] * n,
                 out_shape=[_sds((2,) + h.shape, h.dtype) for h in halves],
                 scratch_shapes=[dma((n,)), dma((n,)), dma((n,))])(*halves)


def _adamw_math(w, g, m, v):
    m = _B1 * m + (1.0 - _B1) * g
    v = _B2 * v + (1.0 - _B2) * jnp.square(g)
    m_hat = m / (1.0 - _B1 ** _STEP)
    v_hat = v / (1.0 - _B2 ** _STEP)
    delta = -_LR * (m_hat / (jnp.sqrt(v_hat) + _ADAM_EPS) + _WD * w)
    return delta, m, v


def _adamw_shard(w, g2, m, v, name):
    r, c = w.shape
    c2 = c // 2
    tm = _tile(r, 256, 8)
    blk = pl.BlockSpec((tm, c2), lambda h, i: (i, h))

    def fn(wb, gb, mb, vb):
        return (gb,) + _adamw_math(wb, gb, mb, vb)

    return _rowcall(fn, [w, g2, m, v], [blk, pl.BlockSpec((None, tm, c2), lambda h, i: (h, i, 0)), blk, blk],
                    [_sds((r, c), _F32)] * 4, [blk] * 4, [False] * 4, grid=(2, r // tm), name=name)


def _adamw_whole(w, g, m, v, name):
    r, c = w.shape
    blk = _full((r, c))
    return _rowcall(lambda *a: _adamw_math(*a), [w, g, m, v], [blk] * 4, [_sds((r, c), _F32)] * 3, [blk] * 3,
                    [False] * 3, grid=(1, 1), name=name)


def _pack(arrs):
    parts = []
    for a in arrs:
        f = a.reshape(-1).astype(_F32)
        pad = (-f.shape[0]) % (8 * _LANE)
        if pad:
            f = jnp.concatenate([f, jnp.zeros((pad,), _F32)])
        parts.append(f)
    return jnp.concatenate(parts).reshape(-1, _LANE)


def _unpack(buf, like):
    flat = buf.reshape(-1)
    outs, off = [], 0
    for a in like:
        nel = a.size
        outs.append(flat[off:off + nel].reshape(a.shape))
        off += nel + ((-nel) % (8 * _LANE))
    return outs


def kernel(x, ln_mix_g, w_in, s5_a_re, s5_a_im, s5_log_dt, s5_b_re, s5_b_im, s5_c_re, s5_c_im, s5_d, s5_w_glu, s5_b_glu, w_proj_s5, hgrn_lb_logits, hgrn_norm_g, w_proj_hgrn, w_out, ln_ffn_g, w_up, conv_w, conv_b, w_down, ln_final_g, loss_target, m_ln_mix_g, m_w_in, m_s5_a_re, m_s5_a_im, m_s5_log_dt, m_s5_b_re, m_s5_b_im, m_s5_c_re, m_s5_c_im, m_s5_d, m_s5_w_glu, m_s5_b_glu, m_w_proj_s5, m_hgrn_lb_logits, m_hgrn_norm_g, m_w_proj_hgrn, m_w_out, m_ln_ffn_g, m_w_up, m_conv_w, m_conv_b, m_w_down, m_ln_final_g, v_ln_mix_g, v_w_in, v_s5_a_re, v_s5_a_im, v_s5_log_dt, v_s5_b_re, v_s5_b_im, v_s5_c_re, v_s5_c_im, v_s5_d, v_s5_w_glu, v_s5_b_glu, v_w_proj_s5, v_hgrn_lb_logits, v_hgrn_norm_g, v_w_proj_hgrn, v_w_out, v_ln_ffn_g, v_w_up, v_conv_w, v_conv_b, v_w_down, v_ln_final_g):
    assert x.shape[0] == 1 and w_in.shape[0] == 1, "one example per device, one layer"
    t, d = x.shape[1], x.shape[2]
    w5 = s5_w_glu.shape[2]
    hw = hgrn_norm_g.shape[1]
    ng_, np_, gc = s5_b_re.shape[1], s5_b_re.shape[2], s5_b_re.shape[3]
    dff = w_down.shape[1] * _NCHIP
    assert gc * _S5_SET == _LANE and ng_ * gc == w5 and hw % _LANE == 0
    gs_off = w5 + 4 * hw
    ci = lax.axis_index("c")
    xt = x.reshape(t, d)
    tgt = loss_target.reshape(t, d)

    big_names = ["w_in", "s5_w_glu", "w_proj_s5", "w_proj_hgrn", "w_out", "w_up", "w_down"]
    big_w = dict(w_in=w_in[0], s5_w_glu=s5_w_glu[0], w_proj_s5=w_proj_s5[0], w_proj_hgrn=w_proj_hgrn[0],
                 w_out=w_out[0], w_up=w_up[0], w_down=w_down[0])
    gathered = _gather_weights([big_w[k].astype(_BF) for k in big_names], [conv_w[0]])
    g_in, g_glu, g_ps5, g_ph, g_out, g_up, g_down, g_cw = gathered
    wglu = g_glu.reshape(w5, w5)
    wout = g_out.reshape(d, d)
    wdown = g_down.reshape(dff, d)
    cw = g_cw.transpose(1, 0, 2).reshape(3, 2 * dff)
    cb = conv_b

    tm_big = _tile(t, 1024, 8)

    h1 = _rms_fwd(xt, ln_mix_g, "rms1_fwd")
    nin_s = g_in.shape[2]
    proj = _mm(h1, g_in, "nn", _F32, tm=tm_big, tn=_tile(nin_s, 1152), tk=d, name="mm_proj")

    abar_r, abar_i, coef_r, coef_i = _s5_prep(s5_a_re[0], s5_a_im[0], s5_log_dt.reshape(ng_, 1))
    lanes = ng_ * np_
    par = (abar_r.reshape(1, lanes), abar_i.reshape(1, lanes), coef_r.reshape(1, lanes), coef_i.reshape(1, lanes),
           _bd_in(s5_b_re[0]), _bd_in(s5_b_im[0]), _bd_out(s5_c_re[0]), -_bd_out(s5_c_im[0]), s5_d.reshape(1, w5))
    y_s5, car, cai = _s5_fwd(proj, par, t, w5)
    z = _s5glu_fwd(y_s5, wglu, s5_b_glu)
    ys = _mm(z, g_ps5, "nn", _F32, tm=tm_big, tn=g_ps5.shape[2], tk=w5, name="mm_proj_s5")

    lb = _lb_prep(hgrn_lb_logits)
    og, st_all = _hgrn_fwd(proj, lb, hgrn_norm_g, t, w5, hw)
    yh = _mm(og, g_ph, "nn", _F32, tm=tm_big, tn=g_ph.shape[2], tk=hw, name="mm_proj_hgrn")

    merged = _merge_fwd(proj, ys, yh, t, d, gs_off)
    x2 = _mm(merged, wout, "nn", _F32, tm=tm_big, tn=_tile(d, 1024), tk=d, res=xt, name="mm_out")
    h2 = _rms_fwd(x2, ln_ffn_g, "rms2_fwd")
    up_s = g_up.shape[2]
    up = _mm(h2, g_up, "nn", _F32, tm=tm_big, tn=_tile(up_s, 1408), tk=d, name="mm_up")
    act = _convact_fwd(up, cw, cb, t, dff)
    x3 = _mm(act, wdown, "nn", _F32, tm=tm_big, tn=_tile(d, 1024), tk=_tile(dff, 1408), res=x2, name="mm_down")
    loss_part, dx3, dx3b, d_gfin = _loss_head(x3, tgt, ln_final_g.reshape(1, d))

    tk_t = _tile(t, 512, 8)
    dact = _mm(dx3b, wdown, "nt", _F32, tm=tm_big, tn=_tile(dff, 1408), tk=d, name="mm_dact")
    r_down = dff // _NCHIP
    gw_down = _mm(act, dx3b, "tn", _BF, tm=_tile(r_down, 1408), tn=_tile(d // 2, 1024), tk=tk_t, halves="rows",
                  name="mm_gw_down")
    dup, d_cw, d_cb = _convact_bwd(up, dact, cw, cb, t, dff)
    dh2 = _mm(dup, g_up, "nt", _F32, tm=tm_big, tn=_tile(d, 1024), tk=_tile(up_s, 1408), name="mm_dh2")
    gw_up = _mm(h2, dup, "tn", _BF, tm=_tile(d, 1024), tn=_tile(up_s // 2, 1408), tk=tk_t, halves="cols", name="mm_gw_up")
    dx2, dx2b, d_gffn = _rms_bwd(x2, ln_ffn_g, dh2, dx3, "rms2_bwd")
    dmerged = _mm(dx2b, wout, "nt", _F32, tm=tm_big, tn=_tile(d, 1024), tk=d, name="mm_dmerged")
    gw_out = _mm(merged, dx2b, "tn", _BF, tm=_tile(d // _NCHIP, 1024), tn=_tile(d // 2, 1024), tk=tk_t, halves="rows",
                 name="mm_gw_out")
    dys, dyh, dgs, dgh = _merge_bwd(proj, ys, yh, dmerged, t, d, gs_off)
    ps_s = g_ps5.shape[2]
    dz = _mm(dys, g_ps5, "nt", _F32, tm=tm_big, tn=_tile(w5, 1024), tk=ps_s, name="mm_dz")
    gw_ps5 = _mm(z, dys, "tn", _BF, tm=_tile(w5, 1024), tn=ps_s // 2, tk=tk_t, halves="cols", name="mm_gw_ps5")
    dog = _mm(dyh, g_ph, "nt", _F32, tm=tm_big, tn=_tile(hw, 1024), tk=ps_s, name="mm_dog")
    gw_ph = _mm(og, dyh, "tn", _BF, tm=_tile(hw, 1024), tn=ps_s // 2, tk=tk_t, halves="cols", name="mm_gw_ph")
    dy_s5, gw_glu_full, d_bglu = _s5glu_bwd(y_s5, dz, wglu, s5_b_glu)
    s5g = _s5_bwd(proj, dy_s5, car, cai, par, t, w5)
    du = s5g[0]
    dq, df, di, dg, d_lb, d_ng = _hgrn_bwd(proj, lb, hgrn_norm_g, st_all, dog, t, w5, hw)
    dproj = jnp.concatenate([du, dq, df, di, dg, dgs, dgh], axis=1)
    dh1 = _mm(dproj, g_in, "nt", _F32, tm=tm_big, tn=_tile(d, 1024), tk=_tile(nin_s, 1152), name="mm_dh1")
    gw_in = _mm(h1, dproj, "tn", _BF, tm=_tile(d, 1024), tn=_tile(nin_s // 2, 1152), tk=tk_t, halves="cols", name="mm_gw_in")
    dx, _, d_gmix = _rms_bwd(xt, ln_mix_g, dh1, dx2, "rms1_bwd")

    d_are, d_aim, d_ldt = _s5_prep_bwd(s5_a_re[0], s5_a_im[0], s5_log_dt.reshape(ng_, 1),
                                       [s5g[k].reshape(ng_, np_) for k in (1, 2, 3, 4)])
    d_bre = _bd_in_grad(s5g[5], np_, gc)
    d_bim = _bd_in_grad(s5g[6], np_, gc)
    d_cre = _bd_out_grad(s5g[7], np_, gc)
    d_cim = -_bd_out_grad(s5g[8], np_, gc)
    d_logits = _lb_prep_bwd(hgrn_lb_logits, d_lb)
    r_glu = w5 // _NCHIP
    gw_glu = gw_glu_full.astype(_BF).reshape(_NCHIP, r_glu, 2, w5 // 2).transpose(2, 0, 1, 3)

    big_g = [gw_in, gw_glu, gw_ps5, gw_ph, gw_out, gw_up, gw_down]
    theirs = _swap_halves(big_g)
    sums = [_add_pairs(lax.dynamic_index_in_dim(g, ci, 0, keepdims=False), th) for g, th in zip(big_g, theirs)]
    small_names = ["ln_mix_g", "s5_a_re", "s5_a_im", "s5_log_dt", "s5_b_re", "s5_b_im", "s5_c_re", "s5_c_im", "s5_d",
                   "s5_b_glu", "hgrn_lb_logits", "hgrn_norm_g", "ln_ffn_g", "conv_b", "ln_final_g"]
    small_w = dict(ln_mix_g=ln_mix_g, s5_a_re=s5_a_re, s5_a_im=s5_a_im, s5_log_dt=s5_log_dt, s5_b_re=s5_b_re,
                   s5_b_im=s5_b_im, s5_c_re=s5_c_re, s5_c_im=s5_c_im, s5_d=s5_d, s5_b_glu=s5_b_glu,
                   hgrn_lb_logits=hgrn_lb_logits, hgrn_norm_g=hgrn_norm_g, ln_ffn_g=ln_ffn_g, conv_b=conv_b,
                   ln_final_g=ln_final_g)
    small_m = dict(ln_mix_g=m_ln_mix_g, s5_a_re=m_s5_a_re, s5_a_im=m_s5_a_im, s5_log_dt=m_s5_log_dt, s5_b_re=m_s5_b_re,
                   s5_b_im=m_s5_b_im, s5_c_re=m_s5_c_re, s5_c_im=m_s5_c_im, s5_d=m_s5_d, s5_b_glu=m_s5_b_glu,
                   hgrn_lb_logits=m_hgrn_lb_logits, hgrn_norm_g=m_hgrn_norm_g, ln_ffn_g=m_ln_ffn_g, conv_b=m_conv_b,
                   ln_final_g=m_ln_final_g)
    small_v = dict(ln_mix_g=v_ln_mix_g, s5_a_re=v_s5_a_re, s5_a_im=v_s5_a_im, s5_log_dt=v_s5_log_dt, s5_b_re=v_s5_b_re,
                   s5_b_im=v_s5_b_im, s5_c_re=v_s5_c_re, s5_c_im=v_s5_c_im, s5_d=v_s5_d, s5_b_glu=v_s5_b_glu,
                   hgrn_lb_logits=v_hgrn_lb_logits, hgrn_norm_g=v_hgrn_norm_g, ln_ffn_g=v_ln_ffn_g, conv_b=v_conv_b,
                   ln_final_g=v_ln_final_g)
    small_g = dict(ln_mix_g=d_gmix, s5_a_re=d_are, s5_a_im=d_aim, s5_log_dt=d_ldt, s5_b_re=d_bre, s5_b_im=d_bim,
                   s5_c_re=d_cre, s5_c_im=d_cim, s5_d=s5g[9], s5_b_glu=d_bglu, hgrn_lb_logits=d_logits,
                   hgrn_norm_g=d_ng, ln_ffn_g=d_gffn, conv_b=d_cb, ln_final_g=d_gfin)
    like = [small_w[k] for k in small_names]
    g_pack = _pack([small_g[k] for k in small_names] + [d_cw])
    received = _exchange_shards(sums, g_pack)
    halves = [_sum_slots(rc, "shard_sum") for rc in received[:-1]]
    g_all = _sum_slots(received[-1], "small_sum")
    full = _share_halves(halves)
    w_pack = _pack(like)
    g_small = g_all[:w_pack.shape[0]]
    cs = conv_w.shape[2]
    g_cw_full = g_all[w_pack.shape[0]:].reshape(-1)[:3 * 2 * dff].reshape(3, 2 * dff)
    g_cw = lax.dynamic_slice_in_dim(g_cw_full, (2 * lax.axis_index("x") + lax.axis_index("y")) * cs, cs, axis=1)

    big_m = dict(w_in=m_w_in, s5_w_glu=m_s5_w_glu, w_proj_s5=m_w_proj_s5, w_proj_hgrn=m_w_proj_hgrn, w_out=m_w_out,
                 w_up=m_w_up, w_down=m_w_down)
    big_v = dict(w_in=v_w_in, s5_w_glu=v_s5_w_glu, w_proj_s5=v_w_proj_s5, w_proj_hgrn=v_w_proj_hgrn, w_out=v_w_out,
                 w_up=v_w_up, w_down=v_w_down)
    res = {}
    for k, g2 in zip(big_names, full):
        w2 = big_w[k]
        shp = (1,) + w2.shape
        outs = _adamw_shard(w2, g2, big_m[k][0], big_v[k][0], "adamw_" + k)
        res[k] = [o.reshape(shp) for o in outs]
    sm_outs = _adamw_whole(w_pack, g_small, _pack([small_m[k] for k in small_names]),
                           _pack([small_v[k] for k in small_names]), "adamw_small")
    sm_g = _unpack(g_small, like)
    sm_d, sm_m, sm_v = (_unpack(o, like) for o in sm_outs)
    for i, k in enumerate(small_names):
        res[k] = [sm_g[i], sm_d[i], sm_m[i], sm_v[i]]

    cw_outs = _adamw_whole(conv_w[0], g_cw, m_conv_w[0], v_conv_w[0], "adamw_conv_w")
    res["conv_w"] = [g_cw.reshape(conv_w.shape)] + [o.reshape(conv_w.shape) for o in cw_outs]

    loss = lax.psum(loss_part[0, 0], ("x", "y", "c"))
    order = ["ln_mix_g", "w_in", "s5_a_re", "s5_a_im", "s5_log_dt", "s5_b_re", "s5_b_im", "s5_c_re", "s5_c_im", "s5_d",
             "s5_w_glu", "s5_b_glu", "w_proj_s5", "hgrn_lb_logits", "hgrn_norm_g", "w_proj_hgrn", "w_out", "ln_ffn_g",
             "w_up", "conv_w", "conv_b", "w_down", "ln_final_g"]
    return (loss, dx.reshape(x.shape), *[res[k][0] for k in order], *[res[k][1] for k in order],
            *[res[k][2] for k in order], *[res[k][3] for k in order])
```

```python
import functools

import jax
import jax.numpy as jnp
from jax import lax
from jax.experimental import pallas as pl
from jax.experimental.pallas import tpu as pltpu

_F32 = jnp.float32
_BF = jnp.bfloat16
_RMS_EPS = 1e-6
_S5_MAX_RE = -1e-4
_LR, _B1, _B2, _ADAM_EPS, _WD, _STEP = 0.001, 0.9, 0.999, 1e-08, 0.01, 10
_MESH = pl.DeviceIdType.MESH
_ANY = pl.BlockSpec(memory_space=pl.ANY)
_LANE = 128
_VMEM_LIMIT = 56 * 1024 * 1024
_CHUNK = 64
_S5_TB = 128
_S5_SET = 8
_HGRN_HP = 2
_HGRN_SEG = 1024
_NCHIP = 4
_NDEV = 8


def _call(body, **kw):
    return pl.pallas_call(body, **kw)


def _cparams(*sem):
    return pltpu.CompilerParams(dimension_semantics=sem, vmem_limit_bytes=_VMEM_LIMIT)


def _tile(n, pref, unit=_LANE):
    if n <= pref:
        return n
    t = (pref // unit) * unit
    while t >= unit:
        if n % t == 0:
            return t
        t -= unit
    raise ValueError(f"no tile for {n}")


_NN = ((1,), (0,))
_NT = ((1,), (1,))
_TN = ((0,), (0,))


def _dg(a, b, dims):
    return lax.dot_general(a.astype(_BF), b.astype(_BF), (dims, ((), ())), preferred_element_type=_F32)


@jax.custom_vjp
def _bdot(a, b):
    return _dg(a, b, _NN)


def _bdot_f(a, b):
    return _dg(a, b, _NN), (a, b)


def _bdot_b(res, g):
    a, b = res
    return _dg(g, b, _NT).astype(a.dtype), _dg(a, g, _TN).astype(b.dtype)


_bdot.defvjp(_bdot_f, _bdot_b)


@jax.custom_vjp
def _bdot_nt(a, b):
    return _dg(a, b, _NT)


def _bdot_nt_f(a, b):
    return _dg(a, b, _NT), (a, b)


def _bdot_nt_b(res, g):
    a, b = res
    return _dg(g, b, _NN).astype(a.dtype), _dg(g, a, _TN).astype(b.dtype)


_bdot_nt.defvjp(_bdot_nt_f, _bdot_nt_b)


@jax.custom_vjp
def _bdot_tn(a, b):
    return _dg(a, b, _TN)


def _bdot_tn_f(a, b):
    return _dg(a, b, _TN), (a, b)


def _bdot_tn_b(res, g):
    a, b = res
    return _dg(b, g, _NT).astype(a.dtype), _dg(a, g, _NN).astype(b.dtype)


_bdot_tn.defvjp(_bdot_tn_f, _bdot_tn_b)


def _shift_up(x, n):
    r = x.shape[0]
    row = lax.broadcasted_iota(jnp.int32, x.shape, 0)
    return jnp.where(row < r - n, pltpu.roll(x, r - n, 0), 0.0)


@functools.partial(jax.custom_vjp, nondiff_argnums=(1,))
def _shift_down(x, n):
    row = lax.broadcasted_iota(jnp.int32, x.shape, 0)
    return jnp.where(row >= n, pltpu.roll(x, n, 0), 0.0)


def _shift_down_f(x, n):
    return _shift_down(x, n), None


def _shift_down_b(n, _, g):
    return (_shift_up(g, n),)


_shift_down.defvjp(_shift_down_f, _shift_down_b)


def _sigmoid(x):
    return 1.0 / (1.0 + jnp.exp(-x))


def _silu(x):
    return x * _sigmoid(x)


def _gelu(x):
    return 0.5 * x * (1.0 + jnp.tanh(0.7978845608028654 * (x + 0.044715 * (x * x * x))))


def _rms_core(x, g):
    return x * lax.rsqrt(jnp.mean(x * x, axis=-1, keepdims=True) + _RMS_EPS) * g


def _mm(a, b, mode, out_dtype, *, tm, tn, tk, res=None, halves=None, name):
    if mode == "nn":
        m, k = a.shape
        a_spec = pl.BlockSpec((tm, tk), lambda i, j, kk: (i, kk))
        if b.ndim == 3:
            s, _, ns = b.shape
            n = s * ns
            npb = ns // tn
            b_spec = pl.BlockSpec((None, tk, tn), lambda i, j, kk: (j // npb, kk, j % npb))
        else:
            n = b.shape[1]
            b_spec = pl.BlockSpec((tk, tn), lambda i, j, kk: (kk, j))
        dims = _NN
    elif mode == "nt":
        m, k = a.shape
        a_spec = pl.BlockSpec((tm, tk), lambda i, j, kk: (i, kk))
        if b.ndim == 3:
            s, n, ks = b.shape
            kpb = ks // tk
            b_spec = pl.BlockSpec((None, tn, tk), lambda i, j, kk: (kk // kpb, j, kk % kpb))
        else:
            n = b.shape[0]
            b_spec = pl.BlockSpec((tn, tk), lambda i, j, kk: (j, kk))
        dims = _NT
    else:
        k, m = a.shape
        n = b.shape[1]
        a_spec = pl.BlockSpec((tk, tm), lambda i, j, kk: (kk, i))
        b_spec = pl.BlockSpec((tk, tn), lambda i, j, kk: (kk, j))
        dims = _TN
    nk = k // tk
    if halves is None:
        out_shape = jax.ShapeDtypeStruct((m, n), out_dtype)
        out_spec = pl.BlockSpec((tm, tn), lambda i, j, kk: (i, j))
    elif halves == "cols":
        c2 = n // (2 * _NCHIP)
        tpc = c2 // tn
        out_shape = jax.ShapeDtypeStruct((2, _NCHIP, m, c2), out_dtype)
        out_spec = pl.BlockSpec((None, None, tm, tn),
                                lambda i, j, kk: ((j // tpc) % 2, j // (2 * tpc), i, j % tpc))
    else:
        c2 = n // 2
        tpc = c2 // tn
        r = m // _NCHIP
        tpr = r // tm
        out_shape = jax.ShapeDtypeStruct((2, _NCHIP, r, c2), out_dtype)
        out_spec = pl.BlockSpec((None, None, tm, tn),
                                lambda i, j, kk: (j // tpc, i // tpr, i % tpr, j % tpc))
    has_res = res is not None

    def body(*refs):
        if has_res:
            a_ref, b_ref, r_ref, o_ref, acc_ref = refs
        else:
            a_ref, b_ref, o_ref, acc_ref = refs
        kk = pl.program_id(2)

        @pl.when(kk == 0)
        def _():
            acc_ref[...] = jnp.zeros_like(acc_ref)

        acc_ref[...] += _dg(a_ref[...], b_ref[...], dims)

        @pl.when(kk == nk - 1)
        def _():
            out = acc_ref[...]
            if has_res:
                out = out + r_ref[...]
            o_ref[...] = out.astype(out_dtype)

    in_specs = [a_spec, b_spec]
    args = [a, b]
    if has_res:
        in_specs.append(pl.BlockSpec((tm, tn), lambda i, j, kk: (i, j)))
        args.append(res)
    return _call(body, name=name, grid=(m // tm, n // tn, nk), in_specs=in_specs, out_specs=out_spec,
                 out_shape=out_shape, scratch_shapes=[pltpu.VMEM((tm, tn), _F32)],
                 compiler_params=_cparams("parallel", "parallel", "arbitrary"))(*args)


def _rowcall(fn, ins, in_specs, outs, out_specs, acc, *, grid, name):
    nin = len(ins)

    def body(*refs):
        vals = fn(*[r[...] for r in refs[:nin]])
        first = pl.program_id(1) == 0
        for k, (o_ref, v) in enumerate(zip(refs[nin:], vals)):
            if acc[k]:
                @pl.when(first)
                def _(o_ref=o_ref):
                    o_ref[...] = jnp.zeros_like(o_ref)
                o_ref[...] += v.astype(o_ref.dtype)
            else:
                o_ref[...] = v.astype(o_ref.dtype)

    return _call(body, name=name, grid=grid, in_specs=in_specs, out_specs=out_specs, out_shape=outs,
                 compiler_params=_cparams("parallel", "arbitrary"))(*ins)


def _rb(tm, w, cb=0):
    return pl.BlockSpec((tm, w), lambda j, i: (i, cb + j))


def _cb(w, cb=0):
    return pl.BlockSpec((1, w), lambda j, i: (0, cb + j))


def _full(shape):
    nd = len(shape)
    return pl.BlockSpec(shape, lambda j, i: (0,) * nd)


def _sds(shape, dtype):
    return jax.ShapeDtypeStruct(shape, dtype)


def _rms_fwd(x, g, name):
    t, d = x.shape
    tm = _tile(t, 512, 8)
    return _rowcall(lambda xb, gb: (_rms_core(xb, gb),), [x, g], [_rb(tm, d), _full((1, d))],
                    [_sds((t, d), _BF)], [_rb(tm, d)], [False], grid=(1, t // tm), name=name)[0]


def _rms_bwd(x, g, dh, dres, name):
    t, d = x.shape
    tm = _tile(t, 256, 8)

    def fn(xb, gb, dhb, drb):
        _, vjp = jax.vjp(_rms_core, xb, gb)
        dx, dg = vjp(dhb)
        dx = dx + drb
        return dx, dx, dg

    return _rowcall(fn, [x, g, dh, dres], [_rb(tm, d), _full((1, d)), _rb(tm, d), _rb(tm, d)],
                    [_sds((t, d), _F32), _sds((t, d), _BF), _sds((1, d), _F32)],
                    [_rb(tm, d), _rb(tm, d), _full((1, d))], [False, False, True],
                    grid=(1, t // tm), name=name)


def _loss_head(x3, tgt, g):
    t, d = x3.shape
    tm = _tile(t, 256, 8)

    def fn(xb, tb, gb):
        y, vjp = jax.vjp(_rms_core, xb, gb)
        e = y - tb
        part = 0.5 * jnp.sum(jnp.mean(e * e, axis=-1, keepdims=True), axis=0, keepdims=True)
        dx, dg = vjp(e * (1.0 / d))
        return jnp.broadcast_to(part, (1, _LANE)), dx, dx, dg

    return _rowcall(fn, [x3, tgt, g], [_rb(tm, d), _rb(tm, d), _full((1, d))],
                    [_sds((1, _LANE), _F32), _sds((t, d), _F32), _sds((t, d), _BF), _sds((1, d), _F32)],
                    [_full((1, _LANE)), _rb(tm, d), _rb(tm, d), _full((1, d))], [True, False, False, True],
                    grid=(1, t // tm), name="loss_head")


def _s5_disc(a_re, a_im, log_dt):
    lam_re = jnp.minimum(a_re, _S5_MAX_RE)
    lam_im = a_im
    dt = jnp.exp(log_dt)
    mag = jnp.exp(lam_re * dt)
    abar_re = mag * jnp.cos(lam_im * dt)
    abar_im = mag * jnp.sin(lam_im * dt)
    den = lam_re * lam_re + lam_im * lam_im
    nr = abar_re - 1.0
    ni = abar_im
    coef_re = (nr * lam_re + ni * lam_im) / den
    coef_im = (ni * lam_re - nr * lam_im) / den
    return abar_re, abar_im, coef_re, coef_im


def _s5_prep(a_re, a_im, log_dt):
    g, p = a_re.shape

    def body(ar, ai, ld, o0, o1, o2, o3):
        outs = _s5_disc(ar[...], ai[...], ld[...])
        for o, v in zip((o0, o1, o2, o3), outs):
            o[...] = v

    return _call(body, name="s5_prep", out_shape=[_sds((g, p), _F32)] * 4)(a_re, a_im, log_dt)


def _s5_prep_bwd(a_re, a_im, log_dt, cts):
    g, p = a_re.shape

    def body(ar, ai, ld, c0, c1, c2, c3, d0, d1, d2):
        _, vjp = jax.vjp(_s5_disc, ar[...], ai[...], ld[...])
        outs = vjp((c0[...], c1[...], c2[...], c3[...]))
        for o, v in zip((d0, d1, d2), outs):
            o[...] = v

    return _call(body, name="s5_prep_bwd",
                 out_shape=[_sds((g, p), _F32), _sds((g, p), _F32), _sds((g, 1), _F32)])(a_re, a_im, log_dt, *cts)


def _s5_block(u, car, cai, ar, ai, cr, ci, b_re, b_im, c_re, c_imn, dvec):
    tb = u.shape[0]
    bur = _bdot(u, b_re)
    bui = _bdot(u, b_im)
    sr = cr * bur - ci * bui
    si = cr * bui + ci * bur
    row = lax.broadcasted_iota(jnp.int32, sr.shape, 0)
    sr = sr + jnp.where(row == 0, ar * car - ai * cai, 0.0)
    si = si + jnp.where(row == 0, ar * cai + ai * car, 0.0)
    pr, pi = ar, ai
    sh = 1
    while sh < tb:
        dr = _shift_down(sr, sh)
        di = _shift_down(si, sh)
        sr, si = sr + pr * dr - pi * di, si + pr * di + pi * dr
        pr, pi = pr * pr - pi * pi, 2.0 * pr * pi
        sh *= 2
    y = _bdot(sr, c_re) + _bdot(si, c_imn) + dvec * u
    last = row == tb - 1
    ncr = jnp.sum(jnp.where(last, sr, 0.0), axis=0, keepdims=True)
    nci = jnp.sum(jnp.where(last, si, 0.0), axis=0, keepdims=True)
    return y, ncr, nci


def _s5_specs(tb, lw, nt, rev):
    tmap = (lambda t: nt - 1 - t) if rev else (lambda t: t)
    vec = pl.BlockSpec((1, lw), lambda s, t: (0, s))
    return dict(
        u=pl.BlockSpec((tb, _LANE), lambda s, t: (tmap(t), s)),
        car=pl.BlockSpec((None, 1, lw), lambda s, t: (tmap(t), 0, s)),
        vec=vec,
        bmat=pl.BlockSpec((None, _LANE, lw), lambda s, t: (s, 0, 0)),
        cmat=pl.BlockSpec((None, lw, _LANE), lambda s, t: (s, 0, 0)),
        dvec=pl.BlockSpec((1, _LANE), lambda s, t: (0, s)),
    )


def _s5_fwd(proj, par, t, w5):
    ar, ai, cr, ci, b_re, b_im, c_re, c_imn, dvec = par
    ns = w5 // _LANE
    lw = ar.shape[1] // ns
    tb = min(_S5_TB, t)
    nt = t // tb
    sp = _s5_specs(tb, lw, nt, False)

    def body(u_ref, ar_r, ai_r, cr_r, ci_r, bre_r, bim_r, cre_r, cim_r, d_r, y_ref, car_ref, cai_ref, s_r, s_i):
        @pl.when(pl.program_id(1) == 0)
        def _():
            s_r[...] = jnp.zeros_like(s_r)
            s_i[...] = jnp.zeros_like(s_i)

        car_ref[...] = s_r[...]
        cai_ref[...] = s_i[...]
        y, ncr, nci = _s5_block(u_ref[...], s_r[...], s_i[...], ar_r[...], ai_r[...], cr_r[...], ci_r[...],
                                bre_r[...], bim_r[...], cre_r[...], cim_r[...], d_r[...])
        y_ref[...] = y
        s_r[...] = ncr
        s_i[...] = nci

    return _call(body, name="s5_fwd", grid=(ns, nt),
                 in_specs=[sp["u"], sp["vec"], sp["vec"], sp["vec"], sp["vec"], sp["bmat"], sp["bmat"],
                           sp["cmat"], sp["cmat"], sp["dvec"]],
                 out_specs=[sp["u"], sp["car"], sp["car"]],
                 out_shape=[_sds((t, w5), _F32), _sds((nt, 1, ns * lw), _F32), _sds((nt, 1, ns * lw), _F32)],
                 scratch_shapes=[pltpu.VMEM((1, lw), _F32), pltpu.VMEM((1, lw), _F32)],
                 compiler_params=_cparams("parallel", "arbitrary"))(proj, ar, ai, cr, ci, b_re, b_im, c_re, c_imn, dvec)


def _s5_bwd(proj, dy, car, cai, par, t, w5):
    ar, ai, cr, ci, b_re, b_im, c_re, c_imn, dvec = par
    ns = w5 // _LANE
    lw = ar.shape[1] // ns
    tb = min(_S5_TB, t)
    nt = t // tb
    sp = _s5_specs(tb, lw, nt, True)

    def body(u_ref, dy_ref, car_ref, cai_ref, ar_r, ai_r, cr_r, ci_r, bre_r, bim_r, cre_r, cim_r, d_r,
             du_ref, g_ar, g_ai, g_cr, g_ci, g_bre, g_bim, g_cre, g_cim, g_d, ds_r, ds_i):
        accs = (g_ar, g_ai, g_cr, g_ci, g_bre, g_bim, g_cre, g_cim, g_d)

        @pl.when(pl.program_id(1) == 0)
        def _():
            ds_r[...] = jnp.zeros_like(ds_r)
            ds_i[...] = jnp.zeros_like(ds_i)
            for o in accs:
                o[...] = jnp.zeros_like(o)

        _, vjp = jax.vjp(_s5_block, u_ref[...], car_ref[...], cai_ref[...], ar_r[...], ai_r[...], cr_r[...],
                         ci_r[...], bre_r[...], bim_r[...], cre_r[...], cim_r[...], d_r[...])
        grads = vjp((dy_ref[...], ds_r[...], ds_i[...]))
        du_ref[...] = grads[0].astype(_BF)
        ds_r[...] = grads[1]
        ds_i[...] = grads[2]
        for o, gval in zip(accs, grads[3:]):
            o[...] += gval

    vec_o = _sds((1, ns * lw), _F32)
    return _call(body, name="s5_bwd", grid=(ns, nt),
                 in_specs=[sp["u"], sp["u"], sp["car"], sp["car"], sp["vec"], sp["vec"], sp["vec"], sp["vec"],
                           sp["bmat"], sp["bmat"], sp["cmat"], sp["cmat"], sp["dvec"]],
                 out_specs=[sp["u"], sp["vec"], sp["vec"], sp["vec"], sp["vec"], sp["bmat"], sp["bmat"],
                            sp["cmat"], sp["cmat"], sp["dvec"]],
                 out_shape=[_sds((t, w5), _BF), vec_o, vec_o, vec_o, vec_o,
                            _sds(b_re.shape, _F32), _sds(b_re.shape, _F32), _sds(c_re.shape, _F32),
                            _sds(c_re.shape, _F32), _sds((1, w5), _F32)],
                 scratch_shapes=[pltpu.VMEM((1, lw), _F32), pltpu.VMEM((1, lw), _F32)],
                 compiler_params=_cparams("parallel", "arbitrary"))(
                     proj, dy, car, cai, ar, ai, cr, ci, b_re, b_im, c_re, c_imn, dvec)


def _bd_in(b):
    g, p, c = b.shape
    s = g // _S5_SET
    b4 = b.reshape(s, _S5_SET, p, c).transpose(0, 1, 3, 2)
    eye = jnp.eye(_S5_SET, dtype=b.dtype)
    return (b4[:, :, :, None, :] * eye[None, :, None, :, None]).reshape(s, _S5_SET * c, _S5_SET * p)


def _bd_in_grad(d, p, c):
    s = d.shape[0]
    eye = jnp.eye(_S5_SET, dtype=d.dtype)
    d5 = d.reshape(s, _S5_SET, c, _S5_SET, p) * eye[None, :, None, :, None]
    return d5.sum(axis=3).transpose(0, 1, 3, 2).reshape(s * _S5_SET, p, c)


def _bd_out(cm):
    g, c, p = cm.shape
    s = g // _S5_SET
    c4 = cm.reshape(s, _S5_SET, c, p).transpose(0, 1, 3, 2)
    eye = jnp.eye(_S5_SET, dtype=cm.dtype)
    return (c4[:, :, :, None, :] * eye[None, :, None, :, None]).reshape(s, _S5_SET * p, _S5_SET * c)


def _bd_out_grad(d, p, c):
    s = d.shape[0]
    eye = jnp.eye(_S5_SET, dtype=d.dtype)
    d5 = d.reshape(s, _S5_SET, p, _S5_SET, c) * eye[None, :, None, :, None]
    return d5.sum(axis=3).transpose(0, 1, 3, 2).reshape(s * _S5_SET, c, p)


def _s5glu_fwd(y, wglu, bglu):
    t, w5 = y.shape
    tm = _tile(t, 256, 8)

    def fn(yb, wb, bb):
        z1 = _gelu(yb)
        a = _dg(z1, wb, _NN) + bb
        return (z1 * _sigmoid(a),)

    return _rowcall(fn, [y, wglu, bglu], [_rb(tm, w5), _full(wglu.shape), _full((1, w5))],
                    [_sds((t, w5), _BF)], [_rb(tm, w5)], [False], grid=(1, t // tm), name="s5glu_fwd")[0]


def _s5glu_bwd(y, dz, wglu, bglu):
    t, w5 = y.shape
    tm = _tile(t, 256, 8)

    def fn(yb, dzb, wb, bb):
        z1, gelu_vjp = jax.vjp(_gelu, yb)
        sig = _sigmoid(_dg(z1, wb, _NN) + bb)
        da = dzb * z1 * sig * (1.0 - sig)
        dz1 = dzb * sig + _dg(da, wb, _NT)
        (dy,) = gelu_vjp(dz1)
        return dy, _dg(z1, da, _TN), jnp.sum(da, axis=0, keepdims=True)

    return _rowcall(fn, [y, dz, wglu, bglu], [_rb(tm, w5), _rb(tm, w5), _full(wglu.shape), _full((1, w5))],
                    [_sds((t, w5), _F32), _sds((w5, w5), _F32), _sds((1, w5), _F32)],
                    [_rb(tm, w5), _full((w5, w5)), _full((1, w5))], [False, True, True],
                    grid=(1, t // tm), name="s5glu_bwd")


def _tri_mats(c, transposed):
    d0, d1 = (1, 0) if transposed else (0, 1)
    t = lax.broadcasted_iota(jnp.int32, (c, c), d0)
    j = lax.broadcasted_iota(jnp.int32, (c, c), d1)
    low = (j <= t).astype(_F32)
    mats = [low]
    for sh in (6, 5, 4):
        m = 1 << (sh - 1)
        r = ((t >> sh) << sh) + (m - 1)
        mats.append(low - (j <= r).astype(_F32))
    return [mm.astype(_BF) for mm in mats]


def _split_dot(mat, x):
    hi = x.astype(_BF)
    lo = (x - hi.astype(_F32)).astype(_BF)
    return (jnp.dot(mat, hi, preferred_element_type=_F32) + jnp.dot(mat, lo, preferred_element_type=_F32))


@jax.custom_vjp
def _decay_sums(lf):
    return tuple(_split_dot(mm, lf) for mm in _tri_mats(lf.shape[0], False))


def _decay_sums_f(lf):
    return _decay_sums(lf), None


def _decay_sums_b(_, gs):
    mats = _tri_mats(gs[0].shape[0], True)
    out = _split_dot(mats[0], gs[0])
    for mm, gval in zip(mats[1:], gs[1:]):
        out = out + _split_dot(mm, gval)
    return (out,)


_decay_sums.defvjp(_decay_sums_f, _decay_sums_b)


def _hgrn_chunk(qi, fi, vi, gi, st, lb, ng):
    c = qi.shape[0]
    row = lax.broadcasted_iota(jnp.int32, (c, 1), 0)
    q = _silu(qi)
    lf = jnp.log(lb + (1.0 - lb) * _sigmoid(fi))
    k = (1.0 - lb) * _sigmoid(-fi)
    b, p32, p16, p8 = _decay_sums(lf)
    btot = jnp.sum(lf, axis=0, keepdims=True)
    inter = _bdot_nt(q * jnp.exp(b), st)
    tt = lax.broadcasted_iota(jnp.int32, (c, c), 0)
    ss = lax.broadcasted_iota(jnp.int32, (c, c), 1)
    sc = jnp.zeros((c, c), _F32)
    for sh, p in ((6, p32), (5, p16), (4, p8)):
        upper = ((row >> (sh - 1)) & 1) == 1
        qm = jnp.where(upper, q * jnp.exp(jnp.where(upper, p, 0.0)), 0.0)
        km = jnp.where(upper, 0.0, k * jnp.exp(jnp.where(upper, 0.0, -p)))
        sc = sc + jnp.where((tt >> sh) == (ss >> sh), _bdot_nt(qm, km), 0.0)
    o = inter + _bdot(sc, vi)
    o = o + jnp.sum(q * k, axis=1, keepdims=True) * vi
    for d in range(1, 8):
        valid = (row & 7) >= d
        bs = _shift_down(b, d)
        ks = _shift_down(k, d)
        vs = _shift_down(vi, d)
        w = jnp.where(valid, q * jnp.exp(jnp.where(valid, b - bs, 0.0)) * ks, 0.0)
        o = o + jnp.sum(w, axis=1, keepdims=True) * vs
    st_new = st * jnp.exp(btot) + _bdot_tn(vi, k * jnp.exp(btot - b))
    on = o * lax.rsqrt(jnp.mean(o * o, axis=1, keepdims=True) + _RMS_EPS) * ng
    return on * _silu(gi), st_new


def _hgrn_geom(t, w5, hw):
    hp = _HGRN_HP if (hw // _LANE) % _HGRN_HP == 0 and w5 % (_LANE * _HGRN_HP) == 0 else 1
    seg = min(t, _HGRN_SEG)
    return hp, hp * _LANE, seg, t // seg


def _hgrn_in_specs(seg, wd, w5, hw, tmap):
    nhp = hw // wd
    qb = w5 // wd
    return [pl.BlockSpec((seg, wd), (lambda h, s, k=k: (tmap(s), qb + k * nhp + h))) for k in range(4)]


def _hgrn_fwd(proj, lb, ng, t, w5, hw):
    assert _CHUNK == 64
    hp, wd, seg, nseg = _hgrn_geom(t, w5, hw)
    ncs = seg // _CHUNK
    vec = pl.BlockSpec((1, wd), lambda h, s: (0, h))

    def body(q_ref, f_ref, i_ref, g_ref, lb_ref, ng_ref, og_ref, st_ref, s_scr):
        @pl.when(pl.program_id(1) == 0)
        def _():
            s_scr[...] = jnp.zeros_like(s_scr)

        def step(ci, carry):
            r = pl.ds(pl.multiple_of(ci * _CHUNK, _CHUNK), _CHUNK)
            for hh in range(hp):
                ln = slice(hh * _LANE, (hh + 1) * _LANE)
                st_ref[hh, ci] = s_scr[hh]
                og, sn = _hgrn_chunk(q_ref[r, ln], f_ref[r, ln], i_ref[r, ln], g_ref[r, ln], s_scr[hh],
                                     lb_ref[:, ln], ng_ref[:, ln])
                og_ref[r, ln] = og.astype(_BF)
                s_scr[hh] = sn
            return carry

        lax.fori_loop(0, ncs, step, 0)

    return _call(body, name="hgrn_fwd", grid=(hw // wd, nseg),
                 in_specs=_hgrn_in_specs(seg, wd, w5, hw, lambda s: s) + [vec, vec],
                 out_specs=[pl.BlockSpec((seg, wd), lambda h, s: (s, h)),
                            pl.BlockSpec((hp, ncs, _LANE, _LANE), lambda h, s: (h, s, 0, 0))],
                 out_shape=[_sds((t, hw), _BF), _sds((hw // _LANE, t // _CHUNK, _LANE, _LANE), _F32)],
                 scratch_shapes=[pltpu.VMEM((hp, _LANE, _LANE), _F32)],
                 compiler_params=_cparams("parallel", "arbitrary"))(proj, proj, proj, proj, lb, ng)


def _hgrn_bwd(proj, lb, ng, st_all, dog, t, w5, hw):
    hp, wd, seg, nseg = _hgrn_geom(t, w5, hw)
    ncs = seg // _CHUNK

    def rev(s):
        return nseg - 1 - s

    vec = pl.BlockSpec((1, wd), lambda h, s: (0, h))
    col = pl.BlockSpec((seg, wd), lambda h, s: (rev(s), h))

    def body(q_ref, f_ref, i_ref, g_ref, lb_ref, ng_ref, st_ref, dog_ref,
             dq_ref, df_ref, di_ref, dg_ref, dlb_ref, dng_ref, ds_scr):
        @pl.when(pl.program_id(1) == 0)
        def _():
            ds_scr[...] = jnp.zeros_like(ds_scr)
            dlb_ref[...] = jnp.zeros_like(dlb_ref)
            dng_ref[...] = jnp.zeros_like(dng_ref)

        def step(kk, carry):
            ci = ncs - 1 - kk
            r = pl.ds(pl.multiple_of(ci * _CHUNK, _CHUNK), _CHUNK)
            for hh in range(hp):
                ln = slice(hh * _LANE, (hh + 1) * _LANE)
                _, vjp = jax.vjp(_hgrn_chunk, q_ref[r, ln], f_ref[r, ln], i_ref[r, ln], g_ref[r, ln], st_ref[hh, ci],
                                 lb_ref[:, ln], ng_ref[:, ln])
                dq, df, di, dg, ds, dlb, dng = vjp((dog_ref[r, ln], ds_scr[hh]))
                dq_ref[r, ln] = dq.astype(_BF)
                df_ref[r, ln] = df.astype(_BF)
                di_ref[r, ln] = di.astype(_BF)
                dg_ref[r, ln] = dg.astype(_BF)
                ds_scr[hh] = ds
                dlb_ref[:, ln] += dlb
                dng_ref[:, ln] += dng
            return carry

        lax.fori_loop(0, ncs, step, 0)

    return _call(body, name="hgrn_bwd", grid=(hw // wd, nseg),
                 in_specs=_hgrn_in_specs(seg, wd, w5, hw, rev) + [
                     vec, vec, pl.BlockSpec((hp, ncs, _LANE, _LANE), lambda h, s: (h, rev(s), 0, 0)), col],
                 out_specs=[col, col, col, col, vec, vec],
                 out_shape=[_sds((t, hw), _BF)] * 4 + [_sds((1, hw), _F32)] * 2,
                 scratch_shapes=[pltpu.VMEM((hp, _LANE, _LANE), _F32)],
                 compiler_params=_cparams("parallel", "arbitrary"))(proj, proj, proj, proj, lb, ng, st_all, dog)


def _lb_of(logits):
    mx = jnp.max(logits, axis=0, keepdims=True)
    e = jnp.exp(logits - mx)
    sm = e / jnp.sum(e, axis=0, keepdims=True)
    row = lax.broadcasted_iota(jnp.int32, logits.shape, 0)
    return jnp.sum(jnp.where(row == 0, sm, 0.0), axis=0, keepdims=True)


def _lb_prep(logits):
    def body(l_ref, o_ref):
        o_ref[...] = _lb_of(l_ref[...])

    return _call(body, name="lb_prep", out_shape=_sds((1, logits.shape[1]), _F32))(logits)


def _lb_prep_bwd(logits, dlb):
    def body(l_ref, d_ref, o_ref):
        _, vjp = jax.vjp(_lb_of, l_ref[...])
        o_ref[...] = vjp(d_ref[...])[0]

    return _call(body, name="lb_prep_bwd", out_shape=_sds(logits.shape, _F32))(logits, dlb)


def _merge_fwd(proj, ys, yh, t, d, gs_off):
    tm = _tile(t, 256, 8)
    w = _tile(d, 1024)
    nb = d // w

    def fn(gs, gh, a, b):
        return (_sigmoid(gs) * a + _sigmoid(gh) * b,)

    return _rowcall(fn, [proj, proj, ys, yh], [_rb(tm, w, gs_off // w), _rb(tm, w, gs_off // w + nb), _rb(tm, w), _rb(tm, w)],
                    [_sds((t, d), _BF)], [_rb(tm, w)], [False], grid=(nb, t // tm), name="merge_fwd")[0]


def _merge_bwd(proj, ys, yh, dm, t, d, gs_off):
    tm = _tile(t, 256, 8)
    w = _tile(d, 1024)
    nb = d // w

    def fn(gs, gh, a, b, g):
        s1 = _sigmoid(gs)
        s2 = _sigmoid(gh)
        return g * s1, g * s2, g * a * s1 * (1.0 - s1), g * b * s2 * (1.0 - s2)

    return _rowcall(fn, [proj, proj, ys, yh, dm],
                    [_rb(tm, w, gs_off // w), _rb(tm, w, gs_off // w + nb), _rb(tm, w), _rb(tm, w), _rb(tm, w)],
                    [_sds((t, d), _BF)] * 4, [_rb(tm, w)] * 4, [False] * 4, grid=(nb, t // tm), name="merge_bwd")


def _prev_rows(up_prev8, is_first):
    p1 = jnp.where(is_first, 0.0, up_prev8[7:8, :])
    p2 = jnp.where(is_first, 0.0, up_prev8[6:7, :])
    return p1, p2


def _causal_taps(cur, p1, p2):
    row = lax.broadcasted_iota(jnp.int32, cur.shape, 0)
    s1 = jnp.where(row == 0, p1, _shift_down(cur, 1))
    s2 = jnp.where(row == 0, p2, jnp.where(row == 1, p1, _shift_down(cur, 2)))
    return s1, s2


def _conv_specs(tm, w, nb_half, t):
    r8 = tm // 8
    cur_g = pl.BlockSpec((tm, w), lambda j, i: (i, j))
    cur_v = pl.BlockSpec((tm, w), lambda j, i: (i, nb_half + j))
    prev_g = pl.BlockSpec((8, w), lambda j, i: (jnp.maximum(i * r8 - 1, 0), j))
    prev_v = pl.BlockSpec((8, w), lambda j, i: (jnp.maximum(i * r8 - 1, 0), nb_half + j))
    w_g = pl.BlockSpec((3, w), lambda j, i: (0, j))
    w_v = pl.BlockSpec((3, w), lambda j, i: (0, nb_half + j))
    b_g = pl.BlockSpec((1, w), lambda j, i: (0, j))
    b_v = pl.BlockSpec((1, w), lambda j, i: (0, nb_half + j))
    return cur_g, cur_v, prev_g, prev_v, w_g, w_v, b_g, b_v


def _conv_of(cur, prev8, wt, bias, is_first):
    p1, p2 = _prev_rows(prev8, is_first)
    s1, s2 = _causal_taps(cur, p1, p2)
    return bias + wt[0:1, :] * s2 + wt[1:2, :] * s1 + wt[2:3, :] * cur


def _convact_fwd(up, cw, cb, t, dff):
    tm = _tile(t, 512, 8)
    w = _tile(dff, 1408)
    nbh = dff // w
    sp = _conv_specs(tm, w, nbh, t)

    def body(ug, uv, pg, pv, wg, wv, bg, bv, o_ref):
        first = pl.program_id(1) == 0
        gate = _conv_of(ug[...], pg[...], wg[...], bg[...], first)
        val = _conv_of(uv[...], pv[...], wv[...], bv[...], first)
        o_ref[...] = (_silu(gate) * val).astype(_BF)

    return _call(body, name="convact_fwd", grid=(nbh, t // tm), in_specs=list(sp),
                 out_specs=pl.BlockSpec((tm, w), lambda j, i: (i, j)), out_shape=_sds((t, dff), _BF),
                 compiler_params=_cparams("parallel", "arbitrary"))(up, up, up, up, cw, cw, cb, cb)


def _convact_bwd(up, dact, cw, cb, t, dff):
    tm = _tile(t, 256, 8)
    w = _tile(dff, 1408)
    nbh = dff // w
    r8 = tm // 8
    nt = t // tm
    last8 = t // 8 - 1

    def own(j):
        return j

    def partner(j):
        return (j + nbh) % (2 * nbh)

    def triple(col):
        return [pl.BlockSpec((tm, w), lambda j, i: (i, col(j))),
                pl.BlockSpec((8, w), lambda j, i: (jnp.minimum((i + 1) * r8, last8), col(j))),
                pl.BlockSpec((8, w), lambda j, i: (jnp.maximum(i * r8 - 1, 0), col(j)))]

    def body(uo, uon, uop, upp, upn, upv, wo_ref, wp_ref, bo_ref, bp_ref, da_ref, dan_ref, du_ref, dw_ref, db_ref):
        j = pl.program_id(0)
        i = pl.program_id(1)
        is_gate = j < nbh
        first = i == 0
        is_last = i == nt - 1
        cur = uo[...]
        conv_o = _conv_of(jnp.concatenate([cur, uon[...]], axis=0), uop[...], wo_ref[...], bo_ref[...], first)
        conv_p = _conv_of(jnp.concatenate([upp[...], upn[...]], axis=0), upv[...], wp_ref[...], bp_ref[...], first)
        da = jnp.concatenate([da_ref[...], dan_ref[...]], axis=0)
        gate = jnp.where(is_gate, conv_o, conv_p)
        val = jnp.where(is_gate, conv_p, conv_o)
        sg = _sigmoid(gate)
        dc = jnp.where(is_gate, da * val * sg * (1.0 + gate * (1.0 - sg)), da * gate * sg)
        row = lax.broadcasted_iota(jnp.int32, dc.shape, 0)
        dc = jnp.where(jnp.logical_and(row >= tm, is_last), 0.0, dc)
        wt = wo_ref[...]
        du = wt[2:3, :] * dc + wt[1:2, :] * _shift_up(dc, 1) + wt[0:1, :] * _shift_up(dc, 2)
        du_ref[...] = du[0:tm, :].astype(_BF)
        dcm = dc[0:tm, :]
        p1, p2 = _prev_rows(uop[...], first)
        s1, s2 = _causal_taps(cur, p1, p2)

        @pl.when(first)
        def _():
            dw_ref[...] = jnp.zeros_like(dw_ref)
            db_ref[...] = jnp.zeros_like(db_ref)

        dw_ref[0:1, :] += jnp.sum(dcm * s2, axis=0, keepdims=True)
        dw_ref[1:2, :] += jnp.sum(dcm * s1, axis=0, keepdims=True)
        dw_ref[2:3, :] += jnp.sum(dcm * cur, axis=0, keepdims=True)
        db_ref[...] += jnp.sum(dcm, axis=0, keepdims=True)

    def dcol(j):
        return j % nbh

    in_specs = (triple(own) + triple(partner)
                + [pl.BlockSpec((3, w), lambda j, i: (0, own(j))), pl.BlockSpec((3, w), lambda j, i: (0, partner(j))),
                   pl.BlockSpec((1, w), lambda j, i: (0, own(j))), pl.BlockSpec((1, w), lambda j, i: (0, partner(j))),
                   pl.BlockSpec((tm, w), lambda j, i: (i, dcol(j))),
                   pl.BlockSpec((8, w), lambda j, i: (jnp.minimum((i + 1) * r8, last8), dcol(j)))])
    return _call(body, name="convact_bwd", grid=(2 * nbh, nt), in_specs=in_specs,
                 out_specs=[pl.BlockSpec((tm, w), lambda j, i: (i, j)), pl.BlockSpec((3, w), lambda j, i: (0, j)),
                            pl.BlockSpec((1, w), lambda j, i: (0, j))],
                 out_shape=[_sds((t, 2 * dff), _BF), _sds((3, 2 * dff), _F32), _sds((1, 2 * dff), _F32)],
                 compiler_params=_cparams("parallel", "arbitrary"))(
                     up, up, up, up, up, up, cw, cw, cb, cb, dact, dact)


def _me():
    return lax.axis_index("x"), lax.axis_index("y"), lax.axis_index("c")


def _other_chips(x, y):
    return [(1 - x, y), (x, 1 - y), (1 - x, 1 - y)]


def _rcopy(src, dst, ssem, rsem, dev):
    return pltpu.make_async_remote_copy(src_ref=src, dst_ref=dst, send_sem=ssem, recv_sem=rsem,
                                        device_id=dev, device_id_type=_MESH)


def _cast_into_slot(w, sel):
    r, c = w.shape
    tm = _tile(r, 256, 16)

    def body(sel_ref, w_ref, o_ref):
        o_ref[...] = w_ref[...].astype(_BF)

    gs = pltpu.PrefetchScalarGridSpec(
        num_scalar_prefetch=1, grid=(r // tm,),
        in_specs=[pl.BlockSpec((tm, c), lambda i, s: (i, 0))],
        out_specs=pl.BlockSpec((None, tm, c), lambda i, s: (s[0], i, 0)))
    return _call(body, name="cast_into_slot", grid_spec=gs, out_shape=_sds((_NCHIP, r, c), _BF),
                 compiler_params=_cparams("parallel"))(sel, w)


def _gather_weights(bufs, direct):
    n, nd = len(bufs), len(direct)

    def body(*refs):
        dins = refs[n:n + nd]
        outs, douts = refs[n + nd:2 * n + nd], refs[2 * n + nd:2 * n + 2 * nd]
        ssem, rsem, fssem, frsem, dssem, drsem, dlsem = refs[2 * n + 2 * nd:]
        x, y, c = _me()
        me = 2 * x + y
        chips = _other_chips(x, y)
        sib = (x, y, 1 - c)

        def piece(a, chip, h):
            r2 = bufs[a].shape[1] // 2
            return outs[a].at[chip, pl.ds(h * r2, r2)]

        def mine(a):
            return piece(a, me, c)

        waits = []
        for a in range(n):
            for j, (cx, cy) in enumerate(chips):
                cp = _rcopy(mine(a), piece(a, me, c), ssem.at[3 * a + j], rsem.at[3 * a + j], (cx, cy, c))
                cp.start()
                waits.append(cp.wait_send)
        for a in range(nd):
            lc = pltpu.make_async_copy(dins[a], douts[a].at[me], dlsem.at[a])
            lc.start()
            waits.append(lc.wait)
            for j, (cx, cy) in enumerate(chips):
                cp = _rcopy(dins[a], douts[a].at[me], dssem.at[3 * a + j], drsem.at[3 * a + j], (cx, cy, c))
                cp.start()
                waits.append(cp.wait_send)
        for a in range(n):
            for j, (cx, cy) in enumerate(chips):
                pc = 2 * cx + cy
                _rcopy(mine(a), piece(a, pc, c), ssem.at[3 * a + j], rsem.at[3 * a + j], (cx, cy, c)).wait_recv()
                fw = _rcopy(piece(a, pc, c), piece(a, pc, c), fssem.at[3 * a + j], frsem.at[3 * a + j], sib)
                fw.start()
                waits.append(fw.wait_send)
        for a in range(n):
            for j, (cx, cy) in enumerate(chips):
                pc = 2 * cx + cy
                _rcopy(piece(a, pc, 1 - c), piece(a, pc, 1 - c), fssem.at[3 * a + j], frsem.at[3 * a + j], sib).wait_recv()
        for a in range(nd):
            for j, (cx, cy) in enumerate(chips):
                pc = 2 * cx + cy
                _rcopy(dins[a], douts[a].at[pc], dssem.at[3 * a + j], drsem.at[3 * a + j], (cx, cy, c)).wait_recv()
        for w in waits:
            w()

    dma = pltpu.SemaphoreType.DMA
    return _call(body, name="gather_weights", in_specs=[_ANY] * (n + nd), out_specs=[_ANY] * (n + nd),
                 out_shape=[_sds(b.shape, b.dtype) for b in bufs] + [_sds((_NCHIP,) + s.shape, s.dtype) for s in direct],
                 input_output_aliases={a: a for a in range(n)},
                 scratch_shapes=[dma((3 * n,)), dma((3 * n,)), dma((3 * n,)), dma((3 * n,)),
                                 dma((3 * nd,)), dma((3 * nd,)), dma((nd,))])(*bufs, *direct)


def _swap_halves(grads):
    n = len(grads)

    def body(*refs):
        ins, outs = refs[:n], refs[n:2 * n]
        ssem, rsem = refs[2 * n:]
        x, y, c = _me()
        sib = (x, y, 1 - c)
        cps = []
        for a in range(n):
            cp = _rcopy(ins[a].at[1 - c], outs[a], ssem.at[a], rsem.at[a], sib)
            cp.start()
            cps.append(cp)
        for cp in cps:
            cp.wait_recv()
        for cp in cps:
            cp.wait_send()

    dma = pltpu.SemaphoreType.DMA
    return _call(body, name="swap_halves", in_specs=[_ANY] * n, out_specs=[_ANY] * n,
                 out_shape=[_sds(g.shape[1:], g.dtype) for g in grads],
                 scratch_shapes=[dma((n,)), dma((n,))])(*grads)


def _add_pairs(mine, theirs):
    s, r, c2 = mine.shape
    a2 = mine.reshape(s * r, c2)
    b2 = theirs.reshape(s * r, c2)
    tm = _tile(s * r, 512, 16)
    out = _rowcall(lambda p, q: (p.astype(_F32) + q.astype(_F32),), [a2, b2], [_rb(tm, c2), _rb(tm, c2)],
                   [_sds((s * r, c2), _BF)], [_rb(tm, c2)], [False], grid=(1, s * r // tm), name="chip_sum")[0]
    return out.reshape(s, r, c2)


def _exchange_shards(sums, small):
    n = len(sums)

    def body(*refs):
        ins, sm = refs[:n], refs[n]
        outs, smo = refs[n + 1:2 * n + 1], refs[2 * n + 1]
        ssem, rsem, sssem, srsem, slsem = refs[2 * n + 2:]
        x, y, c = _me()
        dev = 4 * x + 2 * y + c
        chips = _other_chips(x, y)
        waits = []
        for a in range(n):
            for j, (cx, cy) in enumerate(chips):
                pc = 2 * cx + cy
                cp = _rcopy(ins[a].at[pc], outs[a].at[j], ssem.at[3 * a + j], rsem.at[3 * a + j], (cx, cy, c))
                cp.start()
                waits.append(cp.wait_send)
        lc = pltpu.make_async_copy(sm, smo.at[dev], slsem.at[0])
        lc.start()
        waits.append(lc.wait)
        peers = []
        for k in range(1, _NDEV):
            px = 1 - x if k & 4 else x
            py = 1 - y if k & 2 else y
            pc_ = 1 - c if k & 1 else c
            peers.append((px, py, pc_))
            cp = _rcopy(sm, smo.at[dev], sssem.at[k - 1], srsem.at[k - 1], (px, py, pc_))
            cp.start()
            waits.append(cp.wait_send)
        for a in range(n):
            for j, (cx, cy) in enumerate(chips):
                pc = 2 * cx + cy
                _rcopy(ins[a].at[pc], outs[a].at[j], ssem.at[3 * a + j], rsem.at[3 * a + j], (cx, cy, c)).wait_recv()
        for k, (px, py, pc_) in enumerate(peers):
            _rcopy(sm, smo.at[4 * px + 2 * py + pc_], sssem.at[k], srsem.at[k], (px, py, pc_)).wait_recv()
        for w in waits:
            w()

    dma = pltpu.SemaphoreType.DMA
    return _call(body, name="exchange_shards", in_specs=[_ANY] * (n + 1), out_specs=[_ANY] * (n + 1),
                 out_shape=[_sds((3,) + s.shape[1:], s.dtype) for s in sums] + [_sds((_NDEV,) + small.shape, small.dtype)],
                 scratch_shapes=[dma((3 * n,)), dma((3 * n,)), dma((_NDEV - 1,)), dma((_NDEV - 1,)),
                                 dma((1,))])(*sums, small)


def _shard_sum(sums, recv, sel):
    s, r, c2 = sums.shape
    tm = _tile(r, 256, 16)

    def body(sel_ref, own_ref, rc_ref, o_ref):
        rc = rc_ref[...]
        o_ref[...] = ((own_ref[...].astype(_F32) + rc[0].astype(_F32)) + rc[1].astype(_F32)) + rc[2].astype(_F32)

    gs = pltpu.PrefetchScalarGridSpec(
        num_scalar_prefetch=1, grid=(r // tm,),
        in_specs=[pl.BlockSpec((None, tm, c2), lambda i, q: (q[0], i, 0)),
                  pl.BlockSpec((3, tm, c2), lambda i, q: (0, i, 0))],
        out_specs=pl.BlockSpec((None, tm, c2), lambda i, q: (q[1], i, 0)))
    return _call(body, name="shard_sum", grid_spec=gs, out_shape=_sds((2, r, c2), _F32),
                 compiler_params=_cparams("parallel"))(sel, sums, recv)


def _sum_slots(stack, name):
    k, r, c = stack.shape
    tm = _tile(r, 256, 16 if stack.dtype == _BF else 8)

    def fn(v):
        out = v[0].astype(_F32)
        for i in range(1, k):
            out = out + v[i].astype(_F32)
        return (out,)

    return _rowcall(fn, [stack], [pl.BlockSpec((k, tm, c), lambda j, i: (0, i, 0))], [_sds((r, c), _F32)],
                    [_rb(tm, c)], [False], grid=(1, r // tm), name=name)[0]


def _share_halves(bufs):
    n = len(bufs)

    def body(*refs):
        outs = refs[n:2 * n]
        ssem, rsem = refs[2 * n:]
        x, y, c = _me()
        sib = (x, y, 1 - c)
        cps = []
        for a in range(n):
            cp = _rcopy(outs[a].at[c], outs[a].at[c], ssem.at[a], rsem.at[a], sib)
            cp.start()
            cps.append(cp)
        for a in range(n):
            _rcopy(outs[a].at[c], outs[a].at[1 - c], ssem.at[a], rsem.at[a], sib).wait_recv()
        for cp in cps:
            cp.wait_send()

    dma = pltpu.SemaphoreType.DMA
    return _call(body, name="share_halves", in_specs=[_ANY] * n, out_specs=[_ANY] * n,
                 out_shape=[_sds(b.shape, b.dtype) for b in bufs], input_output_aliases={a: a for a in range(n)},
                 scratch_shapes=[dma((n,)), dma((n,))])(*bufs)


def _adamw_math(w, g, m, v):
    m = _B1 * m + (1.0 - _B1) * g
    v = _B2 * v + (1.0 - _B2) * jnp.square(g)
    m_hat = m / (1.0 - _B1 ** _STEP)
    v_hat = v / (1.0 - _B2 ** _STEP)
    delta = -_LR * (m_hat / (jnp.sqrt(v_hat) + _ADAM_EPS) + _WD * w)
    return delta, m, v


def _adamw_shard(w, g2, m, v, name):
    r, c = w.shape
    c2 = c // 2
    tm = _tile(r, 256, 8)
    blk = pl.BlockSpec((tm, c2), lambda h, i: (i, h))

    def fn(wb, gb, mb, vb):
        return (gb,) + _adamw_math(wb, gb, mb, vb)

    return _rowcall(fn, [w, g2, m, v], [blk, pl.BlockSpec((None, tm, c2), lambda h, i: (h, i, 0)), blk, blk],
                    [_sds((r, c), _F32)] * 4, [blk] * 4, [False] * 4, grid=(2, r // tm), name=name)


def _adamw_whole(w, g, m, v, name):
    r, c = w.shape
    blk = _full((r, c))
    return _rowcall(lambda *a: _adamw_math(*a), [w, g, m, v], [blk] * 4, [_sds((r, c), _F32)] * 3, [blk] * 3,
                    [False] * 3, grid=(1, 1), name=name)


def _pack(arrs):
    parts = []
    for a in arrs:
        f = a.reshape(-1).astype(_F32)
        pad = (-f.shape[0]) % (8 * _LANE)
        if pad:
            f = jnp.concatenate([f, jnp.zeros((pad,), _F32)])
        parts.append(f)
    return jnp.concatenate(parts).reshape(-1, _LANE)


def _unpack(buf, like):
    flat = buf.reshape(-1)
    outs, off = [], 0
    for a in like:
        nel = a.size
        outs.append(flat[off:off + nel].reshape(a.shape))
        off += nel + ((-nel) % (8 * _LANE))
    return outs


def kernel(x, ln_mix_g, w_in, s5_a_re, s5_a_im, s5_log_dt, s5_b_re, s5_b_im, s5_c_re, s5_c_im, s5_d, s5_w_glu, s5_b_glu, w_proj_s5, hgrn_lb_logits, hgrn_norm_g, w_proj_hgrn, w_out, ln_ffn_g, w_up, conv_w, conv_b, w_down, ln_final_g, loss_target, m_ln_mix_g, m_w_in, m_s5_a_re, m_s5_a_im, m_s5_log_dt, m_s5_b_re, m_s5_b_im, m_s5_c_re, m_s5_c_im, m_s5_d, m_s5_w_glu, m_s5_b_glu, m_w_proj_s5, m_hgrn_lb_logits, m_hgrn_norm_g, m_w_proj_hgrn, m_w_out, m_ln_ffn_g, m_w_up, m_conv_w, m_conv_b, m_w_down, m_ln_final_g, v_ln_mix_g, v_w_in, v_s5_a_re, v_s5_a_im, v_s5_log_dt, v_s5_b_re, v_s5_b_im, v_s5_c_re, v_s5_c_im, v_s5_d, v_s5_w_glu, v_s5_b_glu, v_w_proj_s5, v_hgrn_lb_logits, v_hgrn_norm_g, v_w_proj_hgrn, v_w_out, v_ln_ffn_g, v_w_up, v_conv_w, v_conv_b, v_w_down, v_ln_final_g):
    assert x.shape[0] == 1 and w_in.shape[0] == 1, "one example per device, one layer"
    t, d = x.shape[1], x.shape[2]
    w5 = s5_w_glu.shape[2]
    hw = hgrn_norm_g.shape[1]
    ng_, np_, gc = s5_b_re.shape[1], s5_b_re.shape[2], s5_b_re.shape[3]
    dff = w_down.shape[1] * _NCHIP
    assert gc * _S5_SET == _LANE and ng_ * gc == w5 and hw % _LANE == 0
    gs_off = w5 + 4 * hw
    ci = lax.axis_index("c")
    xt = x.reshape(t, d)
    tgt = loss_target.reshape(t, d)

    big_names = ["w_in", "s5_w_glu", "w_proj_s5", "w_proj_hgrn", "w_out", "w_up", "w_down"]
    big_w = dict(w_in=w_in[0], s5_w_glu=s5_w_glu[0], w_proj_s5=w_proj_s5[0], w_proj_hgrn=w_proj_hgrn[0],
                 w_out=w_out[0], w_up=w_up[0], w_down=w_down[0])
    chip = 2 * lax.axis_index("x") + lax.axis_index("y")
    sel_chip = jnp.stack([chip, ci]).astype(jnp.int32)
    gathered = _gather_weights([_cast_into_slot(big_w[k], sel_chip) for k in big_names], [conv_w[0]])
    g_in, g_glu, g_ps5, g_ph, g_out, g_up, g_down, g_cw = gathered
    wglu = g_glu.reshape(w5, w5)
    wout = g_out.reshape(d, d)
    wdown = g_down.reshape(dff, d)
    cw = g_cw.transpose(1, 0, 2).reshape(3, 2 * dff)
    cb = conv_b

    tm_big = _tile(t, 1024, 8)

    h1 = _rms_fwd(xt, ln_mix_g, "rms1_fwd")
    nin_s = g_in.shape[2]
    proj = _mm(h1, g_in, "nn", _F32, tm=tm_big, tn=_tile(nin_s, 1152), tk=d, name="mm_proj")

    abar_r, abar_i, coef_r, coef_i = _s5_prep(s5_a_re[0], s5_a_im[0], s5_log_dt.reshape(ng_, 1))
    lanes = ng_ * np_
    par = (abar_r.reshape(1, lanes), abar_i.reshape(1, lanes), coef_r.reshape(1, lanes), coef_i.reshape(1, lanes),
           _bd_in(s5_b_re[0]), _bd_in(s5_b_im[0]), _bd_out(s5_c_re[0]), -_bd_out(s5_c_im[0]), s5_d.reshape(1, w5))
    y_s5, car, cai = _s5_fwd(proj, par, t, w5)
    z = _s5glu_fwd(y_s5, wglu, s5_b_glu)
    ys = _mm(z, g_ps5, "nn", _F32, tm=tm_big, tn=g_ps5.shape[2], tk=w5, name="mm_proj_s5")

    lb = _lb_prep(hgrn_lb_logits)
    og, st_all = _hgrn_fwd(proj, lb, hgrn_norm_g, t, w5, hw)
    yh = _mm(og, g_ph, "nn", _F32, tm=tm_big, tn=g_ph.shape[2], tk=hw, name="mm_proj_hgrn")

    merged = _merge_fwd(proj, ys, yh, t, d, gs_off)
    x2 = _mm(merged, wout, "nn", _F32, tm=tm_big, tn=_tile(d, 1024), tk=d, res=xt, name="mm_out")
    h2 = _rms_fwd(x2, ln_ffn_g, "rms2_fwd")
    up_s = g_up.shape[2]
    up = _mm(h2, g_up, "nn", _F32, tm=tm_big, tn=_tile(up_s, 1408), tk=d, name="mm_up")
    act = _convact_fwd(up, cw, cb, t, dff)
    x3 = _mm(act, wdown, "nn", _F32, tm=tm_big, tn=_tile(d, 1024), tk=_tile(dff, 1408), res=x2, name="mm_down")
    loss_part, dx3, dx3b, d_gfin = _loss_head(x3, tgt, ln_final_g.reshape(1, d))

    tk_t = _tile(t, 512, 8)
    dact = _mm(dx3b, wdown, "nt", _F32, tm=tm_big, tn=_tile(dff, 1408), tk=d, name="mm_dact")
    r_down = dff // _NCHIP
    gw_down = _mm(act, dx3b, "tn", _BF, tm=_tile(r_down, 1408), tn=_tile(d // 2, 1024), tk=tk_t, halves="rows",
                  name="mm_gw_down")
    dup, d_cw, d_cb = _convact_bwd(up, dact, cw, cb, t, dff)
    dh2 = _mm(dup, g_up, "nt", _F32, tm=tm_big, tn=_tile(d, 1024), tk=_tile(up_s, 1408), name="mm_dh2")
    gw_up = _mm(h2, dup, "tn", _BF, tm=_tile(d, 1024), tn=_tile(up_s // 2, 1408), tk=tk_t, halves="cols", name="mm_gw_up")
    dx2, dx2b, d_gffn = _rms_bwd(x2, ln_ffn_g, dh2, dx3, "rms2_bwd")
    dmerged = _mm(dx2b, wout, "nt", _F32, tm=tm_big, tn=_tile(d, 1024), tk=d, name="mm_dmerged")
    gw_out = _mm(merged, dx2b, "tn", _BF, tm=_tile(d // _NCHIP, 1024), tn=_tile(d // 2, 1024), tk=tk_t, halves="rows",
                 name="mm_gw_out")
    dys, dyh, dgs, dgh = _merge_bwd(proj, ys, yh, dmerged, t, d, gs_off)
    ps_s = g_ps5.shape[2]
    dz = _mm(dys, g_ps5, "nt", _F32, tm=tm_big, tn=_tile(w5, 1024), tk=ps_s, name="mm_dz")
    gw_ps5 = _mm(z, dys, "tn", _BF, tm=_tile(w5, 1024), tn=ps_s // 2, tk=tk_t, halves="cols", name="mm_gw_ps5")
    dog = _mm(dyh, g_ph, "nt", _F32, tm=tm_big, tn=_tile(hw, 1024), tk=ps_s, name="mm_dog")
    gw_ph = _mm(og, dyh, "tn", _BF, tm=_tile(hw, 1024), tn=ps_s // 2, tk=tk_t, halves="cols", name="mm_gw_ph")
    dy_s5, gw_glu_full, d_bglu = _s5glu_bwd(y_s5, dz, wglu, s5_b_glu)
    s5g = _s5_bwd(proj, dy_s5, car, cai, par, t, w5)
    du = s5g[0]
    dq, df, di, dg, d_lb, d_ng = _hgrn_bwd(proj, lb, hgrn_norm_g, st_all, dog, t, w5, hw)
    dproj = jnp.concatenate([du, dq, df, di, dg, dgs, dgh], axis=1)
    dh1 = _mm(dproj, g_in, "nt", _F32, tm=tm_big, tn=_tile(d, 1024), tk=_tile(nin_s, 1152), name="mm_dh1")
    gw_in = _mm(h1, dproj, "tn", _BF, tm=_tile(d, 1024), tn=_tile(nin_s // 2, 1152), tk=tk_t, halves="cols", name="mm_gw_in")
    dx, _, d_gmix = _rms_bwd(xt, ln_mix_g, dh1, dx2, "rms1_bwd")

    d_are, d_aim, d_ldt = _s5_prep_bwd(s5_a_re[0], s5_a_im[0], s5_log_dt.reshape(ng_, 1),
                                       [s5g[k].reshape(ng_, np_) for k in (1, 2, 3, 4)])
    d_bre = _bd_in_grad(s5g[5], np_, gc)
    d_bim = _bd_in_grad(s5g[6], np_, gc)
    d_cre = _bd_out_grad(s5g[7], np_, gc)
    d_cim = -_bd_out_grad(s5g[8], np_, gc)
    d_logits = _lb_prep_bwd(hgrn_lb_logits, d_lb)
    r_glu = w5 // _NCHIP
    gw_glu = gw_glu_full.astype(_BF).reshape(_NCHIP, r_glu, 2, w5 // 2).transpose(2, 0, 1, 3)

    big_g = [gw_in, gw_glu, gw_ps5, gw_ph, gw_out, gw_up, gw_down]
    theirs = _swap_halves(big_g)
    sums = [_add_pairs(lax.dynamic_index_in_dim(g, ci, 0, keepdims=False), th) for g, th in zip(big_g, theirs)]
    small_names = ["ln_mix_g", "s5_a_re", "s5_a_im", "s5_log_dt", "s5_b_re", "s5_b_im", "s5_c_re", "s5_c_im", "s5_d",
                   "s5_b_glu", "hgrn_lb_logits", "hgrn_norm_g", "ln_ffn_g", "conv_b", "ln_final_g"]
    small_w = dict(ln_mix_g=ln_mix_g, s5_a_re=s5_a_re, s5_a_im=s5_a_im, s5_log_dt=s5_log_dt, s5_b_re=s5_b_re,
                   s5_b_im=s5_b_im, s5_c_re=s5_c_re, s5_c_im=s5_c_im, s5_d=s5_d, s5_b_glu=s5_b_glu,
                   hgrn_lb_logits=hgrn_lb_logits, hgrn_norm_g=hgrn_norm_g, ln_ffn_g=ln_ffn_g, conv_b=conv_b,
                   ln_final_g=ln_final_g)
    small_m = dict(ln_mix_g=m_ln_mix_g, s5_a_re=m_s5_a_re, s5_a_im=m_s5_a_im, s5_log_dt=m_s5_log_dt, s5_b_re=m_s5_b_re,
                   s5_b_im=m_s5_b_im, s5_c_re=m_s5_c_re, s5_c_im=m_s5_c_im, s5_d=m_s5_d, s5_b_glu=m_s5_b_glu,
                   hgrn_lb_logits=m_hgrn_lb_logits, hgrn_norm_g=m_hgrn_norm_g, ln_ffn_g=m_ln_ffn_g, conv_b=m_conv_b,
                   ln_final_g=m_ln_final_g)
    small_v = dict(ln_mix_g=v_ln_mix_g, s5_a_re=v_s5_a_re, s5_a_im=v_s5_a_im, s5_log_dt=v_s5_log_dt, s5_b_re=v_s5_b_re,
                   s5_b_im=v_s5_b_im, s5_c_re=v_s5_c_re, s5_c_im=v_s5_c_im, s5_d=v_s5_d, s5_b_glu=v_s5_b_glu,
                   hgrn_lb_logits=v_hgrn_lb_logits, hgrn_norm_g=v_hgrn_norm_g, ln_ffn_g=v_ln_ffn_g, conv_b=v_conv_b,
                   ln_final_g=v_ln_final_g)
    small_g = dict(ln_mix_g=d_gmix, s5_a_re=d_are, s5_a_im=d_aim, s5_log_dt=d_ldt, s5_b_re=d_bre, s5_b_im=d_bim,
                   s5_c_re=d_cre, s5_c_im=d_cim, s5_d=s5g[9], s5_b_glu=d_bglu, hgrn_lb_logits=d_logits,
                   hgrn_norm_g=d_ng, ln_ffn_g=d_gffn, conv_b=d_cb, ln_final_g=d_gfin)
    like = [small_w[k] for k in small_names]
    g_pack = _pack([small_g[k] for k in small_names] + [d_cw])
    received = _exchange_shards(sums, g_pack)
    halves = [_shard_sum(sm, rc, sel_chip) for sm, rc in zip(sums, received[:-1])]
    g_all = _sum_slots(received[-1], "small_sum")
    full = _share_halves(halves)
    w_pack = _pack(like)
    g_small = g_all[:w_pack.shape[0]]
    cs = conv_w.shape[2]
    g_cw_full = g_all[w_pack.shape[0]:].reshape(-1)[:3 * 2 * dff].reshape(3, 2 * dff)
    g_cw = lax.dynamic_slice_in_dim(g_cw_full, chip * cs, cs, axis=1)

    big_m = dict(w_in=m_w_in, s5_w_glu=m_s5_w_glu, w_proj_s5=m_w_proj_s5, w_proj_hgrn=m_w_proj_hgrn, w_out=m_w_out,
                 w_up=m_w_up, w_down=m_w_down)
    big_v = dict(w_in=v_w_in, s5_w_glu=v_s5_w_glu, w_proj_s5=v_w_proj_s5, w_proj_hgrn=v_w_proj_hgrn, w_out=v_w_out,
                 w_up=v_w_up, w_down=v_w_down)
    res = {}
    for k, g2 in zip(big_names, full):
        w2 = big_w[k]
        shp = (1,) + w2.shape
        outs = _adamw_shard(w2, g2, big_m[k][0], big_v[k][0], "adamw_" + k)
        res[k] = [o.reshape(shp) for o in outs]
    sm_outs = _adamw_whole(w_pack, g_small, _pack([small_m[k] for k in small_names]),
                           _pack([small_v[k] for k in small_names]), "adamw_small")
    sm_g = _unpack(g_small, like)
    sm_d, sm_m, sm_v = (_unpack(o, like) for o in sm_outs)
    for i, k in enumerate(small_names):
        res[k] = [sm_g[i], sm_d[i], sm_m[i], sm_v[i]]

    cw_outs = _adamw_whole(conv_w[0], g_cw, m_conv_w[0], v_conv_w[0], "adamw_conv_w")
    res["conv_w"] = [g_cw.reshape(conv_w.shape)] + [o.reshape(conv_w.shape) for o in cw_outs]

    loss = lax.psum(loss_part[0, 0], ("x", "y", "c"))
    order = ["ln_mix_g", "w_in", "s5_a_re", "s5_a_im", "s5_log_dt", "s5_b_re", "s5_b_im", "s5_c_re", "s5_c_im", "s5_d",
             "s5_w_glu", "s5_b_glu", "w_proj_s5", "hgrn_lb_logits", "hgrn_norm_g", "w_proj_hgrn", "w_out", "ln_ffn_g",
             "w_up", "conv_w", "conv_b", "w_down", "ln_final_g"]
    return (loss, dx.reshape(x.shape), *[res[k][0] for k in order], *[res[k][1] for k in order],
            *[res[k][2] for k in order], *[res[k][3] for k in order])
```

```python
import functools

import jax
import jax.numpy as jnp
from jax import lax
from jax.experimental import pallas as pl
from jax.experimental.pallas import tpu as pltpu

_F32 = jnp.float32
_BF = jnp.bfloat16
_RMS_EPS = 1e-6
_S5_MAX_RE = -1e-4
_LR, _B1, _B2, _ADAM_EPS, _WD, _STEP = 0.001, 0.9, 0.999, 1e-08, 0.01, 10
_MESH = pl.DeviceIdType.MESH
_ANY = pl.BlockSpec(memory_space=pl.ANY)
_LANE = 128
_VMEM_LIMIT = 56 * 1024 * 1024
_CHUNK = 64
_S5_TB = 128
_S5_SET = 8
_HGRN_HP = 2
_HGRN_SEG = 1024
_NCHIP = 4
_NDEV = 8


def _call(body, **kw):
    return pl.pallas_call(body, **kw)


def _cparams(*sem):
    return pltpu.CompilerParams(dimension_semantics=sem, vmem_limit_bytes=_VMEM_LIMIT)


def _tile(n, pref, unit=_LANE):
    if n <= pref:
        return n
    t = (pref // unit) * unit
    while t >= unit:
        if n % t == 0:
            return t
        t -= unit
    raise ValueError(f"no tile for {n}")


_NN = ((1,), (0,))
_NT = ((1,), (1,))
_TN = ((0,), (0,))


def _dg(a, b, dims):
    return lax.dot_general(a.astype(_BF), b.astype(_BF), (dims, ((), ())), preferred_element_type=_F32)


@jax.custom_vjp
def _bdot(a, b):
    return _dg(a, b, _NN)


def _bdot_f(a, b):
    return _dg(a, b, _NN), (a, b)


def _bdot_b(res, g):
    a, b = res
    return _dg(g, b, _NT).astype(a.dtype), _dg(a, g, _TN).astype(b.dtype)


_bdot.defvjp(_bdot_f, _bdot_b)


@jax.custom_vjp
def _bdot_nt(a, b):
    return _dg(a, b, _NT)


def _bdot_nt_f(a, b):
    return _dg(a, b, _NT), (a, b)


def _bdot_nt_b(res, g):
    a, b = res
    return _dg(g, b, _NN).astype(a.dtype), _dg(g, a, _TN).astype(b.dtype)


_bdot_nt.defvjp(_bdot_nt_f, _bdot_nt_b)


@jax.custom_vjp
def _bdot_tn(a, b):
    return _dg(a, b, _TN)


def _bdot_tn_f(a, b):
    return _dg(a, b, _TN), (a, b)


def _bdot_tn_b(res, g):
    a, b = res
    return _dg(b, g, _NT).astype(a.dtype), _dg(a, g, _NN).astype(b.dtype)


_bdot_tn.defvjp(_bdot_tn_f, _bdot_tn_b)


def _shift_up(x, n):
    r = x.shape[0]
    row = lax.broadcasted_iota(jnp.int32, x.shape, 0)
    return jnp.where(row < r - n, pltpu.roll(x, r - n, 0), 0.0)


@functools.partial(jax.custom_vjp, nondiff_argnums=(1,))
def _shift_down(x, n):
    row = lax.broadcasted_iota(jnp.int32, x.shape, 0)
    return jnp.where(row >= n, pltpu.roll(x, n, 0), 0.0)


def _shift_down_f(x, n):
    return _shift_down(x, n), None


def _shift_down_b(n, _, g):
    return (_shift_up(g, n),)


_shift_down.defvjp(_shift_down_f, _shift_down_b)


def _sigmoid(x):
    return 1.0 / (1.0 + jnp.exp(-x))


def _silu(x):
    return x * _sigmoid(x)


def _gelu(x):
    return 0.5 * x * (1.0 + jnp.tanh(0.7978845608028654 * (x + 0.044715 * (x * x * x))))


def _rms_core(x, g):
    return x * lax.rsqrt(jnp.mean(x * x, axis=-1, keepdims=True) + _RMS_EPS) * g


def _mm(a, b, mode, out_dtype, *, tm, tn, tk, res=None, halves=None, rider=None, name):
    if mode == "nn":
        m, k = a.shape
        a_spec = pl.BlockSpec((tm, tk), lambda i, j, kk: (i, kk))
        if b.ndim == 3:
            s, _, ns = b.shape
            n = s * ns
            npb = ns // tn
            b_spec = pl.BlockSpec((None, tk, tn), lambda i, j, kk: (j // npb, kk, j % npb))
        else:
            n = b.shape[1]
            b_spec = pl.BlockSpec((tk, tn), lambda i, j, kk: (kk, j))
        dims = _NN
    elif mode == "nt":
        m, k = a.shape
        a_spec = pl.BlockSpec((tm, tk), lambda i, j, kk: (i, kk))
        if b.ndim == 3:
            s, n, ks = b.shape
            kpb = ks // tk
            b_spec = pl.BlockSpec((None, tn, tk), lambda i, j, kk: (kk // kpb, j, kk % kpb))
        else:
            n = b.shape[0]
            b_spec = pl.BlockSpec((tn, tk), lambda i, j, kk: (j, kk))
        dims = _NT
    else:
        k, m = a.shape
        n = b.shape[1]
        a_spec = pl.BlockSpec((tk, tm), lambda i, j, kk: (kk, i))
        b_spec = pl.BlockSpec((tk, tn), lambda i, j, kk: (kk, j))
        dims = _TN
    nk = k // tk
    if halves is None:
        out_shape = jax.ShapeDtypeStruct((m, n), out_dtype)
        out_spec = pl.BlockSpec((tm, tn), lambda i, j, kk: (i, j))
    elif halves == "cols":
        c2 = n // (2 * _NCHIP)
        tpc = c2 // tn
        out_shape = jax.ShapeDtypeStruct((2, _NCHIP, m, c2), out_dtype)
        out_spec = pl.BlockSpec((None, None, tm, tn),
                                lambda i, j, kk: ((j // tpc) % 2, j // (2 * tpc), i, j % tpc))
    else:
        c2 = n // 2
        tpc = c2 // tn
        r = m // _NCHIP
        tpr = r // tm
        out_shape = jax.ShapeDtypeStruct((2, _NCHIP, r, c2), out_dtype)
        out_spec = pl.BlockSpec((None, None, tm, tn),
                                lambda i, j, kk: (j // tpc, i // tpr, i % tpr, j % tpc))
    has_res = res is not None
    nreg = 3 if has_res else 2
    ni, nj = m // tm, n // tn
    r_ins = list(rider.ins) if rider else []
    r_outs = list(rider.outs) if rider else []

    def body(*refs):
        a_ref, b_ref = refs[0], refs[1]
        r_ref = refs[2] if has_res else None
        rin = refs[nreg:nreg + len(r_ins)]
        o_ref = refs[nreg + len(r_ins)]
        rout = refs[nreg + len(r_ins) + 1:nreg + len(r_ins) + 1 + len(r_outs)]
        acc_ref = refs[nreg + len(r_ins) + 1 + len(r_outs)]
        sems = refs[nreg + len(r_ins) + 2 + len(r_outs):]
        i, j, kk = pl.program_id(0), pl.program_id(1), pl.program_id(2)

        if rider:
            @pl.when(jnp.logical_and(jnp.logical_and(i == 0, j == 0), kk == 0))
            def _():
                rider.start(rin, rout, sems)

        @pl.when(kk == 0)
        def _():
            acc_ref[...] = jnp.zeros_like(acc_ref)

        acc_ref[...] += _dg(a_ref[...], b_ref[...], dims)

        @pl.when(kk == nk - 1)
        def _():
            out = acc_ref[...]
            if has_res:
                out = out + r_ref[...]
            o_ref[...] = out.astype(out_dtype)

        if rider:
            @pl.when(jnp.logical_and(jnp.logical_and(i == ni - 1, j == nj - 1), kk == nk - 1))
            def _():
                rider.finish(rin, rout, sems)

    in_specs = [a_spec, b_spec]
    args = [a, b]
    if has_res:
        in_specs.append(pl.BlockSpec((tm, tn), lambda i, j, kk: (i, j)))
        args.append(res)
    if not rider:
        return _call(body, name=name, grid=(ni, nj, nk), in_specs=in_specs, out_specs=out_spec,
                     out_shape=out_shape, scratch_shapes=[pltpu.VMEM((tm, tn), _F32)],
                     compiler_params=_cparams("parallel", "parallel", "arbitrary"))(*args)
    res_all = _call(body, name=name, grid=(ni, nj, nk), in_specs=in_specs + [_ANY] * len(r_ins),
                    out_specs=[out_spec] + [_ANY] * len(r_outs), out_shape=[out_shape] + r_outs,
                    input_output_aliases={nreg + k: 1 + v for k, v in rider.aliases.items()},
                    scratch_shapes=[pltpu.VMEM((tm, tn), _F32)] + list(rider.sems),
                    compiler_params=_cparams("arbitrary", "arbitrary", "arbitrary"))(*args, *r_ins)
    return res_all[0], list(res_all[1:])


def _rowcall(fn, ins, in_specs, outs, out_specs, acc, *, grid, name):
    nin = len(ins)

    def body(*refs):
        vals = fn(*[r[...] for r in refs[:nin]])
        first = pl.program_id(1) == 0
        for k, (o_ref, v) in enumerate(zip(refs[nin:], vals)):
            if acc[k]:
                @pl.when(first)
                def _(o_ref=o_ref):
                    o_ref[...] = jnp.zeros_like(o_ref)
                o_ref[...] += v.astype(o_ref.dtype)
            else:
                o_ref[...] = v.astype(o_ref.dtype)

    return _call(body, name=name, grid=grid, in_specs=in_specs, out_specs=out_specs, out_shape=outs,
                 compiler_params=_cparams("parallel", "arbitrary"))(*ins)


def _rb(tm, w, cb=0):
    return pl.BlockSpec((tm, w), lambda j, i: (i, cb + j))


def _cb(w, cb=0):
    return pl.BlockSpec((1, w), lambda j, i: (0, cb + j))


def _full(shape):
    nd = len(shape)
    return pl.BlockSpec(shape, lambda j, i: (0,) * nd)


def _sds(shape, dtype):
    return jax.ShapeDtypeStruct(shape, dtype)


def _rms_fwd(x, g, name):
    t, d = x.shape
    tm = _tile(t, 512, 8)
    return _rowcall(lambda xb, gb: (_rms_core(xb, gb),), [x, g], [_rb(tm, d), _full((1, d))],
                    [_sds((t, d), _BF)], [_rb(tm, d)], [False], grid=(1, t // tm), name=name)[0]


def _rms_bwd(x, g, dh, dres, name):
    t, d = x.shape
    tm = _tile(t, 256, 8)

    def fn(xb, gb, dhb, drb):
        _, vjp = jax.vjp(_rms_core, xb, gb)
        dx, dg = vjp(dhb)
        dx = dx + drb
        return dx, dx, dg

    return _rowcall(fn, [x, g, dh, dres], [_rb(tm, d), _full((1, d)), _rb(tm, d), _rb(tm, d)],
                    [_sds((t, d), _F32), _sds((t, d), _BF), _sds((1, d), _F32)],
                    [_rb(tm, d), _rb(tm, d), _full((1, d))], [False, False, True],
                    grid=(1, t // tm), name=name)


def _loss_head(x3, tgt, g):
    t, d = x3.shape
    tm = _tile(t, 256, 8)

    def fn(xb, tb, gb):
        y, vjp = jax.vjp(_rms_core, xb, gb)
        e = y - tb
        part = 0.5 * jnp.sum(jnp.mean(e * e, axis=-1, keepdims=True), axis=0, keepdims=True)
        dx, dg = vjp(e * (1.0 / d))
        return jnp.broadcast_to(part, (1, _LANE)), dx, dx, dg

    return _rowcall(fn, [x3, tgt, g], [_rb(tm, d), _rb(tm, d), _full((1, d))],
                    [_sds((1, _LANE), _F32), _sds((t, d), _F32), _sds((t, d), _BF), _sds((1, d), _F32)],
                    [_full((1, _LANE)), _rb(tm, d), _rb(tm, d), _full((1, d))], [True, False, False, True],
                    grid=(1, t // tm), name="loss_head")


def _s5_disc(a_re, a_im, log_dt):
    lam_re = jnp.minimum(a_re, _S5_MAX_RE)
    lam_im = a_im
    dt = jnp.exp(log_dt)
    mag = jnp.exp(lam_re * dt)
    abar_re = mag * jnp.cos(lam_im * dt)
    abar_im = mag * jnp.sin(lam_im * dt)
    den = lam_re * lam_re + lam_im * lam_im
    nr = abar_re - 1.0
    ni = abar_im
    coef_re = (nr * lam_re + ni * lam_im) / den
    coef_im = (ni * lam_re - nr * lam_im) / den
    return abar_re, abar_im, coef_re, coef_im


def _s5_prep(a_re, a_im, log_dt):
    g, p = a_re.shape

    def body(ar, ai, ld, o0, o1, o2, o3):
        outs = _s5_disc(ar[...], ai[...], ld[...])
        for o, v in zip((o0, o1, o2, o3), outs):
            o[...] = v

    return _call(body, name="s5_prep", out_shape=[_sds((g, p), _F32)] * 4)(a_re, a_im, log_dt)


def _s5_prep_bwd(a_re, a_im, log_dt, cts):
    g, p = a_re.shape

    def body(ar, ai, ld, c0, c1, c2, c3, d0, d1, d2):
        _, vjp = jax.vjp(_s5_disc, ar[...], ai[...], ld[...])
        outs = vjp((c0[...], c1[...], c2[...], c3[...]))
        for o, v in zip((d0, d1, d2), outs):
            o[...] = v

    return _call(body, name="s5_prep_bwd",
                 out_shape=[_sds((g, p), _F32), _sds((g, p), _F32), _sds((g, 1), _F32)])(a_re, a_im, log_dt, *cts)


def _s5_block(u, car, cai, ar, ai, cr, ci, b_re, b_im, c_re, c_imn, dvec):
    tb = u.shape[0]
    bur = _bdot(u, b_re)
    bui = _bdot(u, b_im)
    sr = cr * bur - ci * bui
    si = cr * bui + ci * bur
    row = lax.broadcasted_iota(jnp.int32, sr.shape, 0)
    sr = sr + jnp.where(row == 0, ar * car - ai * cai, 0.0)
    si = si + jnp.where(row == 0, ar * cai + ai * car, 0.0)
    pr, pi = ar, ai
    sh = 1
    while sh < tb:
        dr = _shift_down(sr, sh)
        di = _shift_down(si, sh)
        sr, si = sr + pr * dr - pi * di, si + pr * di + pi * dr
        pr, pi = pr * pr - pi * pi, 2.0 * pr * pi
        sh *= 2
    y = _bdot(sr, c_re) + _bdot(si, c_imn) + dvec * u
    last = row == tb - 1
    ncr = jnp.sum(jnp.where(last, sr, 0.0), axis=0, keepdims=True)
    nci = jnp.sum(jnp.where(last, si, 0.0), axis=0, keepdims=True)
    return y, ncr, nci


def _s5_specs(tb, lw, nt, rev):
    tmap = (lambda t: nt - 1 - t) if rev else (lambda t: t)
    vec = pl.BlockSpec((1, lw), lambda s, t: (0, s))
    return dict(
        u=pl.BlockSpec((tb, _LANE), lambda s, t: (tmap(t), s)),
        car=pl.BlockSpec((None, 1, lw), lambda s, t: (tmap(t), 0, s)),
        vec=vec,
        bmat=pl.BlockSpec((None, _LANE, lw), lambda s, t: (s, 0, 0)),
        cmat=pl.BlockSpec((None, lw, _LANE), lambda s, t: (s, 0, 0)),
        dvec=pl.BlockSpec((1, _LANE), lambda s, t: (0, s)),
    )


def _s5_fwd(proj, par, t, w5):
    ar, ai, cr, ci, b_re, b_im, c_re, c_imn, dvec = par
    ns = w5 // _LANE
    lw = ar.shape[1] // ns
    tb = min(_S5_TB, t)
    nt = t // tb
    sp = _s5_specs(tb, lw, nt, False)

    def body(u_ref, ar_r, ai_r, cr_r, ci_r, bre_r, bim_r, cre_r, cim_r, d_r, y_ref, car_ref, cai_ref, s_r, s_i):
        @pl.when(pl.program_id(1) == 0)
        def _():
            s_r[...] = jnp.zeros_like(s_r)
            s_i[...] = jnp.zeros_like(s_i)

        car_ref[...] = s_r[...]
        cai_ref[...] = s_i[...]
        y, ncr, nci = _s5_block(u_ref[...], s_r[...], s_i[...], ar_r[...], ai_r[...], cr_r[...], ci_r[...],
                                bre_r[...], bim_r[...], cre_r[...], cim_r[...], d_r[...])
        y_ref[...] = y
        s_r[...] = ncr
        s_i[...] = nci

    return _call(body, name="s5_fwd", grid=(ns, nt),
                 in_specs=[sp["u"], sp["vec"], sp["vec"], sp["vec"], sp["vec"], sp["bmat"], sp["bmat"],
                           sp["cmat"], sp["cmat"], sp["dvec"]],
                 out_specs=[sp["u"], sp["car"], sp["car"]],
                 out_shape=[_sds((t, w5), _F32), _sds((nt, 1, ns * lw), _F32), _sds((nt, 1, ns * lw), _F32)],
                 scratch_shapes=[pltpu.VMEM((1, lw), _F32), pltpu.VMEM((1, lw), _F32)],
                 compiler_params=_cparams("parallel", "arbitrary"))(proj, ar, ai, cr, ci, b_re, b_im, c_re, c_imn, dvec)


def _s5_bwd(proj, dy, car, cai, par, t, w5):
    ar, ai, cr, ci, b_re, b_im, c_re, c_imn, dvec = par
    ns = w5 // _LANE
    lw = ar.shape[1] // ns
    tb = min(_S5_TB, t)
    nt = t // tb
    sp = _s5_specs(tb, lw, nt, True)

    def body(u_ref, dy_ref, car_ref, cai_ref, ar_r, ai_r, cr_r, ci_r, bre_r, bim_r, cre_r, cim_r, d_r,
             du_ref, g_ar, g_ai, g_cr, g_ci, g_bre, g_bim, g_cre, g_cim, g_d, ds_r, ds_i):
        accs = (g_ar, g_ai, g_cr, g_ci, g_bre, g_bim, g_cre, g_cim, g_d)

        @pl.when(pl.program_id(1) == 0)
        def _():
            ds_r[...] = jnp.zeros_like(ds_r)
            ds_i[...] = jnp.zeros_like(ds_i)
            for o in accs:
                o[...] = jnp.zeros_like(o)

        _, vjp = jax.vjp(_s5_block, u_ref[...], car_ref[...], cai_ref[...], ar_r[...], ai_r[...], cr_r[...],
                         ci_r[...], bre_r[...], bim_r[...], cre_r[...], cim_r[...], d_r[...])
        grads = vjp((dy_ref[...], ds_r[...], ds_i[...]))
        du_ref[...] = grads[0].astype(_BF)
        ds_r[...] = grads[1]
        ds_i[...] = grads[2]
        for o, gval in zip(accs, grads[3:]):
            o[...] += gval

    vec_o = _sds((1, ns * lw), _F32)
    return _call(body, name="s5_bwd", grid=(ns, nt),
                 in_specs=[sp["u"], sp["u"], sp["car"], sp["car"], sp["vec"], sp["vec"], sp["vec"], sp["vec"],
                           sp["bmat"], sp["bmat"], sp["cmat"], sp["cmat"], sp["dvec"]],
                 out_specs=[sp["u"], sp["vec"], sp["vec"], sp["vec"], sp["vec"], sp["bmat"], sp["bmat"],
                            sp["cmat"], sp["cmat"], sp["dvec"]],
                 out_shape=[_sds((t, w5), _BF), vec_o, vec_o, vec_o, vec_o,
                            _sds(b_re.shape, _F32), _sds(b_re.shape, _F32), _sds(c_re.shape, _F32),
                            _sds(c_re.shape, _F32), _sds((1, w5), _F32)],
                 scratch_shapes=[pltpu.VMEM((1, lw), _F32), pltpu.VMEM((1, lw), _F32)],
                 compiler_params=_cparams("parallel", "arbitrary"))(
                     proj, dy, car, cai, ar, ai, cr, ci, b_re, b_im, c_re, c_imn, dvec)


def _bd_in(b):
    g, p, c = b.shape
    s = g // _S5_SET
    b4 = b.reshape(s, _S5_SET, p, c).transpose(0, 1, 3, 2)
    eye = jnp.eye(_S5_SET, dtype=b.dtype)
    return (b4[:, :, :, None, :] * eye[None, :, None, :, None]).reshape(s, _S5_SET * c, _S5_SET * p)


def _bd_in_grad(d, p, c):
    s = d.shape[0]
    eye = jnp.eye(_S5_SET, dtype=d.dtype)
    d5 = d.reshape(s, _S5_SET, c, _S5_SET, p) * eye[None, :, None, :, None]
    return d5.sum(axis=3).transpose(0, 1, 3, 2).reshape(s * _S5_SET, p, c)


def _bd_out(cm):
    g, c, p = cm.shape
    s = g // _S5_SET
    c4 = cm.reshape(s, _S5_SET, c, p).transpose(0, 1, 3, 2)
    eye = jnp.eye(_S5_SET, dtype=cm.dtype)
    return (c4[:, :, :, None, :] * eye[None, :, None, :, None]).reshape(s, _S5_SET * p, _S5_SET * c)


def _bd_out_grad(d, p, c):
    s = d.shape[0]
    eye = jnp.eye(_S5_SET, dtype=d.dtype)
    d5 = d.reshape(s, _S5_SET, p, _S5_SET, c) * eye[None, :, None, :, None]
    return d5.sum(axis=3).transpose(0, 1, 3, 2).reshape(s * _S5_SET, c, p)


def _s5glu_fwd(y, wglu, bglu):
    t, w5 = y.shape
    tm = _tile(t, 256, 8)

    def fn(yb, wb, bb):
        z1 = _gelu(yb)
        a = _dg(z1, wb, _NN) + bb
        return (z1 * _sigmoid(a),)

    return _rowcall(fn, [y, wglu, bglu], [_rb(tm, w5), _full(wglu.shape), _full((1, w5))],
                    [_sds((t, w5), _BF)], [_rb(tm, w5)], [False], grid=(1, t // tm), name="s5glu_fwd")[0]


def _s5glu_bwd(y, dz, wglu, bglu):
    t, w5 = y.shape
    tm = _tile(t, 256, 8)

    def fn(yb, dzb, wb, bb):
        z1, gelu_vjp = jax.vjp(_gelu, yb)
        sig = _sigmoid(_dg(z1, wb, _NN) + bb)
        da = dzb * z1 * sig * (1.0 - sig)
        dz1 = dzb * sig + _dg(da, wb, _NT)
        (dy,) = gelu_vjp(dz1)
        return dy, _dg(z1, da, _TN), jnp.sum(da, axis=0, keepdims=True)

    return _rowcall(fn, [y, dz, wglu, bglu], [_rb(tm, w5), _rb(tm, w5), _full(wglu.shape), _full((1, w5))],
                    [_sds((t, w5), _F32), _sds((w5, w5), _F32), _sds((1, w5), _F32)],
                    [_rb(tm, w5), _full((w5, w5)), _full((1, w5))], [False, True, True],
                    grid=(1, t // tm), name="s5glu_bwd")


def _tri_mats(c, transposed):
    d0, d1 = (1, 0) if transposed else (0, 1)
    t = lax.broadcasted_iota(jnp.int32, (c, c), d0)
    j = lax.broadcasted_iota(jnp.int32, (c, c), d1)
    low = (j <= t).astype(_F32)
    mats = [low]
    for sh in (6, 5, 4):
        m = 1 << (sh - 1)
        r = ((t >> sh) << sh) + (m - 1)
        mats.append(low - (j <= r).astype(_F32))
    return [mm.astype(_BF) for mm in mats]


def _split_dot(mat, x):
    hi = x.astype(_BF)
    lo = (x - hi.astype(_F32)).astype(_BF)
    return (jnp.dot(mat, hi, preferred_element_type=_F32) + jnp.dot(mat, lo, preferred_element_type=_F32))


@jax.custom_vjp
def _decay_sums(lf):
    return tuple(_split_dot(mm, lf) for mm in _tri_mats(lf.shape[0], False))


def _decay_sums_f(lf):
    return _decay_sums(lf), None


def _decay_sums_b(_, gs):
    mats = _tri_mats(gs[0].shape[0], True)
    out = _split_dot(mats[0], gs[0])
    for mm, gval in zip(mats[1:], gs[1:]):
        out = out + _split_dot(mm, gval)
    return (out,)


_decay_sums.defvjp(_decay_sums_f, _decay_sums_b)


def _hgrn_chunk(qi, fi, vi, gi, st, lb, ng):
    c = qi.shape[0]
    row = lax.broadcasted_iota(jnp.int32, (c, 1), 0)
    q = _silu(qi)
    lf = jnp.log(lb + (1.0 - lb) * _sigmoid(fi))
    k = (1.0 - lb) * _sigmoid(-fi)
    b, p32, p16, p8 = _decay_sums(lf)
    btot = jnp.sum(lf, axis=0, keepdims=True)
    inter = _bdot_nt(q * jnp.exp(b), st)
    tt = lax.broadcasted_iota(jnp.int32, (c, c), 0)
    ss = lax.broadcasted_iota(jnp.int32, (c, c), 1)
    sc = jnp.zeros((c, c), _F32)
    for sh, p in ((6, p32), (5, p16), (4, p8)):
        upper = ((row >> (sh - 1)) & 1) == 1
        qm = jnp.where(upper, q * jnp.exp(jnp.where(upper, p, 0.0)), 0.0)
        km = jnp.where(upper, 0.0, k * jnp.exp(jnp.where(upper, 0.0, -p)))
        sc = sc + jnp.where((tt >> sh) == (ss >> sh), _bdot_nt(qm, km), 0.0)
    o = inter + _bdot(sc, vi)
    o = o + jnp.sum(q * k, axis=1, keepdims=True) * vi
    for d in range(1, 8):
        valid = (row & 7) >= d
        bs = _shift_down(b, d)
        ks = _shift_down(k, d)
        vs = _shift_down(vi, d)
        w = jnp.where(valid, q * jnp.exp(jnp.where(valid, b - bs, 0.0)) * ks, 0.0)
        o = o + jnp.sum(w, axis=1, keepdims=True) * vs
    st_new = st * jnp.exp(btot) + _bdot_tn(vi, k * jnp.exp(btot - b))
    on = o * lax.rsqrt(jnp.mean(o * o, axis=1, keepdims=True) + _RMS_EPS) * ng
    return on * _silu(gi), st_new


def _hgrn_geom(t, w5, hw):
    hp = _HGRN_HP if (hw // _LANE) % _HGRN_HP == 0 and w5 % (_LANE * _HGRN_HP) == 0 else 1
    seg = min(t, _HGRN_SEG)
    return hp, hp * _LANE, seg, t // seg


def _hgrn_in_specs(seg, wd, w5, hw, tmap):
    nhp = hw // wd
    qb = w5 // wd
    return [pl.BlockSpec((seg, wd), (lambda h, s, k=k: (tmap(s), qb + k * nhp + h))) for k in range(4)]


def _hgrn_fwd(proj, lb, ng, t, w5, hw):
    assert _CHUNK == 64
    hp, wd, seg, nseg = _hgrn_geom(t, w5, hw)
    ncs = seg // _CHUNK
    vec = pl.BlockSpec((1, wd), lambda h, s: (0, h))

    def body(q_ref, f_ref, i_ref, g_ref, lb_ref, ng_ref, og_ref, st_ref, s_scr):
        @pl.when(pl.program_id(1) == 0)
        def _():
            s_scr[...] = jnp.zeros_like(s_scr)

        def step(ci, carry):
            r = pl.ds(pl.multiple_of(ci * _CHUNK, _CHUNK), _CHUNK)
            for hh in range(hp):
                ln = slice(hh * _LANE, (hh + 1) * _LANE)
                st_ref[hh, ci] = s_scr[hh]
                og, sn = _hgrn_chunk(q_ref[r, ln], f_ref[r, ln], i_ref[r, ln], g_ref[r, ln], s_scr[hh],
                                     lb_ref[:, ln], ng_ref[:, ln])
                og_ref[r, ln] = og.astype(_BF)
                s_scr[hh] = sn
            return carry

        lax.fori_loop(0, ncs, step, 0)

    return _call(body, name="hgrn_fwd", grid=(hw // wd, nseg),
                 in_specs=_hgrn_in_specs(seg, wd, w5, hw, lambda s: s) + [vec, vec],
                 out_specs=[pl.BlockSpec((seg, wd), lambda h, s: (s, h)),
                            pl.BlockSpec((hp, ncs, _LANE, _LANE), lambda h, s: (h, s, 0, 0))],
                 out_shape=[_sds((t, hw), _BF), _sds((hw // _LANE, t // _CHUNK, _LANE, _LANE), _F32)],
                 scratch_shapes=[pltpu.VMEM((hp, _LANE, _LANE), _F32)],
                 compiler_params=_cparams("parallel", "arbitrary"))(proj, proj, proj, proj, lb, ng)


def _hgrn_bwd(proj, lb, ng, st_all, dog, t, w5, hw):
    hp, wd, seg, nseg = _hgrn_geom(t, w5, hw)
    ncs = seg // _CHUNK

    def rev(s):
        return nseg - 1 - s

    vec = pl.BlockSpec((1, wd), lambda h, s: (0, h))
    col = pl.BlockSpec((seg, wd), lambda h, s: (rev(s), h))

    def body(q_ref, f_ref, i_ref, g_ref, lb_ref, ng_ref, st_ref, dog_ref,
             dq_ref, df_ref, di_ref, dg_ref, dlb_ref, dng_ref, ds_scr):
        @pl.when(pl.program_id(1) == 0)
        def _():
            ds_scr[...] = jnp.zeros_like(ds_scr)
            dlb_ref[...] = jnp.zeros_like(dlb_ref)
            dng_ref[...] = jnp.zeros_like(dng_ref)

        def step(kk, carry):
            ci = ncs - 1 - kk
            r = pl.ds(pl.multiple_of(ci * _CHUNK, _CHUNK), _CHUNK)
            for hh in range(hp):
                ln = slice(hh * _LANE, (hh + 1) * _LANE)
                _, vjp = jax.vjp(_hgrn_chunk, q_ref[r, ln], f_ref[r, ln], i_ref[r, ln], g_ref[r, ln], st_ref[hh, ci],
                                 lb_ref[:, ln], ng_ref[:, ln])
                dq, df, di, dg, ds, dlb, dng = vjp((dog_ref[r, ln], ds_scr[hh]))
                dq_ref[r, ln] = dq.astype(_BF)
                df_ref[r, ln] = df.astype(_BF)
                di_ref[r, ln] = di.astype(_BF)
                dg_ref[r, ln] = dg.astype(_BF)
                ds_scr[hh] = ds
                dlb_ref[:, ln] += dlb
                dng_ref[:, ln] += dng
            return carry

        lax.fori_loop(0, ncs, step, 0)

    return _call(body, name="hgrn_bwd", grid=(hw // wd, nseg),
                 in_specs=_hgrn_in_specs(seg, wd, w5, hw, rev) + [
                     vec, vec, pl.BlockSpec((hp, ncs, _LANE, _LANE), lambda h, s: (h, rev(s), 0, 0)), col],
                 out_specs=[col, col, col, col, vec, vec],
                 out_shape=[_sds((t, hw), _BF)] * 4 + [_sds((1, hw), _F32)] * 2,
                 scratch_shapes=[pltpu.VMEM((hp, _LANE, _LANE), _F32)],
                 compiler_params=_cparams("parallel", "arbitrary"))(proj, proj, proj, proj, lb, ng, st_all, dog)


def _lb_of(logits):
    mx = jnp.max(logits, axis=0, keepdims=True)
    e = jnp.exp(logits - mx)
    sm = e / jnp.sum(e, axis=0, keepdims=True)
    row = lax.broadcasted_iota(jnp.int32, logits.shape, 0)
    return jnp.sum(jnp.where(row == 0, sm, 0.0), axis=0, keepdims=True)


def _lb_prep(logits):
    def body(l_ref, o_ref):
        o_ref[...] = _lb_of(l_ref[...])

    return _call(body, name="lb_prep", out_shape=_sds((1, logits.shape[1]), _F32))(logits)


def _lb_prep_bwd(logits, dlb):
    def body(l_ref, d_ref, o_ref):
        _, vjp = jax.vjp(_lb_of, l_ref[...])
        o_ref[...] = vjp(d_ref[...])[0]

    return _call(body, name="lb_prep_bwd", out_shape=_sds(logits.shape, _F32))(logits, dlb)


def _merge_fwd(proj, ys, yh, t, d, gs_off):
    tm = _tile(t, 256, 8)
    w = _tile(d, 1024)
    nb = d // w

    def fn(gs, gh, a, b):
        return (_sigmoid(gs) * a + _sigmoid(gh) * b,)

    return _rowcall(fn, [proj, proj, ys, yh], [_rb(tm, w, gs_off // w), _rb(tm, w, gs_off // w + nb), _rb(tm, w), _rb(tm, w)],
                    [_sds((t, d), _BF)], [_rb(tm, w)], [False], grid=(nb, t // tm), name="merge_fwd")[0]


def _merge_bwd(proj, ys, yh, dm, t, d, gs_off):
    tm = _tile(t, 256, 8)
    w = _tile(d, 1024)
    nb = d // w

    def fn(gs, gh, a, b, g):
        s1 = _sigmoid(gs)
        s2 = _sigmoid(gh)
        return g * s1, g * s2, g * a * s1 * (1.0 - s1), g * b * s2 * (1.0 - s2)

    return _rowcall(fn, [proj, proj, ys, yh, dm],
                    [_rb(tm, w, gs_off // w), _rb(tm, w, gs_off // w + nb), _rb(tm, w), _rb(tm, w), _rb(tm, w)],
                    [_sds((t, d), _BF)] * 4, [_rb(tm, w)] * 4, [False] * 4, grid=(nb, t // tm), name="merge_bwd")


def _prev_rows(up_prev8, is_first):
    p1 = jnp.where(is_first, 0.0, up_prev8[7:8, :])
    p2 = jnp.where(is_first, 0.0, up_prev8[6:7, :])
    return p1, p2


def _causal_taps(cur, p1, p2):
    row = lax.broadcasted_iota(jnp.int32, cur.shape, 0)
    s1 = jnp.where(row == 0, p1, _shift_down(cur, 1))
    s2 = jnp.where(row == 0, p2, jnp.where(row == 1, p1, _shift_down(cur, 2)))
    return s1, s2


def _conv_specs(tm, w, nb_half, t):
    r8 = tm // 8
    cur_g = pl.BlockSpec((tm, w), lambda j, i: (i, j))
    cur_v = pl.BlockSpec((tm, w), lambda j, i: (i, nb_half + j))
    prev_g = pl.BlockSpec((8, w), lambda j, i: (jnp.maximum(i * r8 - 1, 0), j))
    prev_v = pl.BlockSpec((8, w), lambda j, i: (jnp.maximum(i * r8 - 1, 0), nb_half + j))
    w_g = pl.BlockSpec((3, w), lambda j, i: (0, j))
    w_v = pl.BlockSpec((3, w), lambda j, i: (0, nb_half + j))
    b_g = pl.BlockSpec((1, w), lambda j, i: (0, j))
    b_v = pl.BlockSpec((1, w), lambda j, i: (0, nb_half + j))
    return cur_g, cur_v, prev_g, prev_v, w_g, w_v, b_g, b_v


def _conv_of(cur, prev8, wt, bias, is_first):
    p1, p2 = _prev_rows(prev8, is_first)
    s1, s2 = _causal_taps(cur, p1, p2)
    return bias + wt[0:1, :] * s2 + wt[1:2, :] * s1 + wt[2:3, :] * cur


def _convact_fwd(up, cw, cb, t, dff):
    tm = _tile(t, 512, 8)
    w = _tile(dff, 1408)
    nbh = dff // w
    sp = _conv_specs(tm, w, nbh, t)

    def body(ug, uv, pg, pv, wg, wv, bg, bv, o_ref):
        first = pl.program_id(1) == 0
        gate = _conv_of(ug[...], pg[...], wg[...], bg[...], first)
        val = _conv_of(uv[...], pv[...], wv[...], bv[...], first)
        o_ref[...] = (_silu(gate) * val).astype(_BF)

    return _call(body, name="convact_fwd", grid=(nbh, t // tm), in_specs=list(sp),
                 out_specs=pl.BlockSpec((tm, w), lambda j, i: (i, j)), out_shape=_sds((t, dff), _BF),
                 compiler_params=_cparams("parallel", "arbitrary"))(up, up, up, up, cw, cw, cb, cb)


def _convact_bwd(up, dact, cw, cb, t, dff):
    tm = _tile(t, 256, 8)
    w = _tile(dff, 1408)
    nbh = dff // w
    r8 = tm // 8
    nt = t // tm
    last8 = t // 8 - 1

    def own(j):
        return j

    def partner(j):
        return (j + nbh) % (2 * nbh)

    def triple(col):
        return [pl.BlockSpec((tm, w), lambda j, i: (i, col(j))),
                pl.BlockSpec((8, w), lambda j, i: (jnp.minimum((i + 1) * r8, last8), col(j))),
                pl.BlockSpec((8, w), lambda j, i: (jnp.maximum(i * r8 - 1, 0), col(j)))]

    def body(uo, uon, uop, upp, upn, upv, wo_ref, wp_ref, bo_ref, bp_ref, da_ref, dan_ref, du_ref, dw_ref, db_ref):
        j = pl.program_id(0)
        i = pl.program_id(1)
        is_gate = j < nbh
        first = i == 0
        is_last = i == nt - 1
        cur = uo[...]
        conv_o = _conv_of(jnp.concatenate([cur, uon[...]], axis=0), uop[...], wo_ref[...], bo_ref[...], first)
        conv_p = _conv_of(jnp.concatenate([upp[...], upn[...]], axis=0), upv[...], wp_ref[...], bp_ref[...], first)
        da = jnp.concatenate([da_ref[...], dan_ref[...]], axis=0)
        gate = jnp.where(is_gate, conv_o, conv_p)
        val = jnp.where(is_gate, conv_p, conv_o)
        sg = _sigmoid(gate)
        dc = jnp.where(is_gate, da * val * sg * (1.0 + gate * (1.0 - sg)), da * gate * sg)
        row = lax.broadcasted_iota(jnp.int32, dc.shape, 0)
        dc = jnp.where(jnp.logical_and(row >= tm, is_last), 0.0, dc)
        wt = wo_ref[...]
        du = wt[2:3, :] * dc + wt[1:2, :] * _shift_up(dc, 1) + wt[0:1, :] * _shift_up(dc, 2)
        du_ref[...] = du[0:tm, :].astype(_BF)
        dcm = dc[0:tm, :]
        p1, p2 = _prev_rows(uop[...], first)
        s1, s2 = _causal_taps(cur, p1, p2)

        @pl.when(first)
        def _():
            dw_ref[...] = jnp.zeros_like(dw_ref)
            db_ref[...] = jnp.zeros_like(db_ref)

        dw_ref[0:1, :] += jnp.sum(dcm * s2, axis=0, keepdims=True)
        dw_ref[1:2, :] += jnp.sum(dcm * s1, axis=0, keepdims=True)
        dw_ref[2:3, :] += jnp.sum(dcm * cur, axis=0, keepdims=True)
        db_ref[...] += jnp.sum(dcm, axis=0, keepdims=True)

    def dcol(j):
        return j % nbh

    in_specs = (triple(own) + triple(partner)
                + [pl.BlockSpec((3, w), lambda j, i: (0, own(j))), pl.BlockSpec((3, w), lambda j, i: (0, partner(j))),
                   pl.BlockSpec((1, w), lambda j, i: (0, own(j))), pl.BlockSpec((1, w), lambda j, i: (0, partner(j))),
                   pl.BlockSpec((tm, w), lambda j, i: (i, dcol(j))),
                   pl.BlockSpec((8, w), lambda j, i: (jnp.minimum((i + 1) * r8, last8), dcol(j)))])
    return _call(body, name="convact_bwd", grid=(2 * nbh, nt), in_specs=in_specs,
                 out_specs=[pl.BlockSpec((tm, w), lambda j, i: (i, j)), pl.BlockSpec((3, w), lambda j, i: (0, j)),
                            pl.BlockSpec((1, w), lambda j, i: (0, j))],
                 out_shape=[_sds((t, 2 * dff), _BF), _sds((3, 2 * dff), _F32), _sds((1, 2 * dff), _F32)],
                 compiler_params=_cparams("parallel", "arbitrary"))(
                     up, up, up, up, up, up, cw, cw, cb, cb, dact, dact)


def _me():
    return lax.axis_index("x"), lax.axis_index("y"), lax.axis_index("c")


def _other_chips(x, y):
    return [(1 - x, y), (x, 1 - y), (1 - x, 1 - y)]


def _rcopy(src, dst, ssem, rsem, dev):
    return pltpu.make_async_remote_copy(src_ref=src, dst_ref=dst, send_sem=ssem, recv_sem=rsem,
                                        device_id=dev, device_id_type=_MESH)


def _cast_into_slot(w, sel):
    r, c = w.shape
    tm = _tile(r, 256, 16)

    def body(sel_ref, w_ref, o_ref):
        o_ref[...] = w_ref[...].astype(_BF)

    gs = pltpu.PrefetchScalarGridSpec(
        num_scalar_prefetch=1, grid=(r // tm,),
        in_specs=[pl.BlockSpec((tm, c), lambda i, s: (i, 0))],
        out_specs=pl.BlockSpec((None, tm, c), lambda i, s: (s[0], i, 0)))
    return _call(body, name="cast_into_slot", grid_spec=gs, out_shape=_sds((_NCHIP, r, c), _BF),
                 compiler_params=_cparams("parallel"))(sel, w)


class _Plan:
    def __init__(self, ins, outs, aliases, sems, start, finish):
        self.ins, self.outs, self.aliases, self.sems, self.start, self.finish = ins, outs, aliases, sems, start, finish


def _run_plan(plan, name):
    ni, no = len(plan.ins), len(plan.outs)

    def body(*refs):
        rin, rout, sems = refs[:ni], refs[ni:ni + no], refs[ni + no:]
        plan.start(rin, rout, sems)
        plan.finish(rin, rout, sems)

    return _call(body, name=name, in_specs=[_ANY] * ni, out_specs=[_ANY] * no, out_shape=list(plan.outs),
                 input_output_aliases=dict(plan.aliases), scratch_shapes=list(plan.sems))(*plan.ins)


def _gather_plan(bufs, direct):
    n, nd = len(bufs), len(direct)

    def where():
        x, y, c = _me()
        return c, 2 * x + y, _other_chips(x, y), (x, y, 1 - c)

    def piece(outs, a, chip, h):
        r2 = bufs[a].shape[1] // 2
        return outs[a].at[chip, pl.ds(h * r2, r2)]

    def send(outs, sems, a, j, me, c, chip):
        return _rcopy(piece(outs, a, me, c), piece(outs, a, me, c), sems[0].at[3 * a + j], sems[1].at[3 * a + j],
                      (chip[0], chip[1], c))

    def forward(outs, sems, a, j, pc, c, sib):
        return _rcopy(piece(outs, a, pc, c), piece(outs, a, pc, c), sems[2].at[3 * a + j], sems[3].at[3 * a + j], sib)

    def dsend(dins, douts, sems, a, j, me, c, chip):
        return _rcopy(dins[a], douts[a].at[me], sems[4].at[3 * a + j], sems[5].at[3 * a + j], (chip[0], chip[1], c))

    def start(rin, rout, sems):
        outs, dins, douts = rout[:n], rin[n:], rout[n:]
        c, me, chips, _ = where()
        for a in range(n):
            for j, chip in enumerate(chips):
                send(outs, sems, a, j, me, c, chip).start()
        for a in range(nd):
            pltpu.make_async_copy(dins[a], douts[a].at[me], sems[6].at[a]).start()
            for j, chip in enumerate(chips):
                dsend(dins, douts, sems, a, j, me, c, chip).start()

    def finish(rin, rout, sems):
        outs, dins, douts = rout[:n], rin[n:], rout[n:]
        c, me, chips, sib = where()
        for a in range(n):
            for j, (cx, cy) in enumerate(chips):
                pc = 2 * cx + cy
                _rcopy(piece(outs, a, me, c), piece(outs, a, pc, c), sems[0].at[3 * a + j], sems[1].at[3 * a + j],
                       (cx, cy, c)).wait_recv()
                forward(outs, sems, a, j, pc, c, sib).start()
        for a in range(n):
            for j, (cx, cy) in enumerate(chips):
                pc = 2 * cx + cy
                _rcopy(piece(outs, a, pc, 1 - c), piece(outs, a, pc, 1 - c), sems[2].at[3 * a + j],
                       sems[3].at[3 * a + j], sib).wait_recv()
        for a in range(nd):
            for j, (cx, cy) in enumerate(chips):
                _rcopy(dins[a], douts[a].at[2 * cx + cy], sems[4].at[3 * a + j], sems[5].at[3 * a + j],
                       (cx, cy, c)).wait_recv()
        for a in range(n):
            for j, (cx, cy) in enumerate(chips):
                send(outs, sems, a, j, me, c, (cx, cy)).wait_send()
                forward(outs, sems, a, j, 2 * cx + cy, c, sib).wait_send()
        for a in range(nd):
            pltpu.make_async_copy(dins[a], douts[a].at[me], sems[6].at[a]).wait()
            for j, chip in enumerate(chips):
                dsend(dins, douts, sems, a, j, me, c, chip).wait_send()

    dma = pltpu.SemaphoreType.DMA
    return _Plan(list(bufs) + list(direct),
                 [_sds(b.shape, b.dtype) for b in bufs] + [_sds((_NCHIP,) + s.shape, s.dtype) for s in direct],
                 {a: a for a in range(n)},
                 [dma((3 * max(n, 1),)), dma((3 * max(n, 1),)), dma((3 * max(n, 1),)), dma((3 * max(n, 1),)),
                  dma((3 * max(nd, 1),)), dma((3 * max(nd, 1),)), dma((max(nd, 1),))], start, finish)


def _swap_halves(grads, name):
    n = len(grads)

    def body(*refs):
        ins, outs = refs[:n], refs[n:2 * n]
        ssem, rsem = refs[2 * n:]
        x, y, c = _me()
        sib = (x, y, 1 - c)
        cps = []
        for a in range(n):
            cp = _rcopy(ins[a].at[1 - c], outs[a], ssem.at[a], rsem.at[a], sib)
            cp.start()
            cps.append(cp)
        for cp in cps:
            cp.wait_recv()
        for cp in cps:
            cp.wait_send()

    dma = pltpu.SemaphoreType.DMA
    return _call(body, name=name, in_specs=[_ANY] * n, out_specs=[_ANY] * n,
                 out_shape=[_sds(g.shape[1:], g.dtype) for g in grads],
                 scratch_shapes=[dma((n,)), dma((n,))])(*grads)


def _add_pairs(mine, theirs):
    s, r, c2 = mine.shape
    a2 = mine.reshape(s * r, c2)
    b2 = theirs.reshape(s * r, c2)
    tm = _tile(s * r, 512, 16)
    out = _rowcall(lambda p, q: (p.astype(_F32) + q.astype(_F32),), [a2, b2], [_rb(tm, c2), _rb(tm, c2)],
                   [_sds((s * r, c2), _BF)], [_rb(tm, c2)], [False], grid=(1, s * r // tm), name="chip_sum")[0]
    return out.reshape(s, r, c2)


def _exchange_plan(sums, small):
    n = len(sums)
    has_small = small is not None

    def where():
        x, y, c = _me()
        peers = [(1 - x if k & 4 else x, 1 - y if k & 2 else y, 1 - c if k & 1 else c) for k in range(1, _NDEV)]
        return c, 4 * x + 2 * y + c, _other_chips(x, y), peers

    def send(rin, rout, sems, a, j, c, chip):
        return _rcopy(rin[a].at[2 * chip[0] + chip[1]], rout[a].at[j], sems[0].at[3 * a + j], sems[1].at[3 * a + j],
                      (chip[0], chip[1], c))

    def small_send(rin, rout, sems, k, dev, peer):
        return _rcopy(rin[n], rout[n].at[dev], sems[2].at[k], sems[3].at[k], peer)

    def start(rin, rout, sems):
        c, dev, chips, peers = where()
        for a in range(n):
            for j, chip in enumerate(chips):
                send(rin, rout, sems, a, j, c, chip).start()
        if has_small:
            pltpu.make_async_copy(rin[n], rout[n].at[dev], sems[4].at[0]).start()
            for k, peer in enumerate(peers):
                small_send(rin, rout, sems, k, dev, peer).start()

    def finish(rin, rout, sems):
        c, dev, chips, peers = where()
        for a in range(n):
            for j, chip in enumerate(chips):
                send(rin, rout, sems, a, j, c, chip).wait_recv()
        if has_small:
            for k, (px, py, pc_) in enumerate(peers):
                _rcopy(rin[n], rout[n].at[4 * px + 2 * py + pc_], sems[2].at[k], sems[3].at[k], (px, py, pc_)).wait_recv()
        for a in range(n):
            for j, chip in enumerate(chips):
                send(rin, rout, sems, a, j, c, chip).wait_send()
        if has_small:
            pltpu.make_async_copy(rin[n], rout[n].at[dev], sems[4].at[0]).wait()
            for k, peer in enumerate(peers):
                small_send(rin, rout, sems, k, dev, peer).wait_send()

    dma = pltpu.SemaphoreType.DMA
    outs = [_sds((3,) + s.shape[1:], s.dtype) for s in sums]
    if has_small:
        outs.append(_sds((_NDEV,) + small.shape, small.dtype))
    return _Plan(list(sums) + ([small] if has_small else []), outs, {},
                 [dma((3 * max(n, 1),)), dma((3 * max(n, 1),)), dma((_NDEV - 1,)), dma((_NDEV - 1,)), dma((1,))],
                 start, finish)


def _shard_sum(sums, recv, sel):
    s, r, c2 = sums.shape
    tm = _tile(r, 256, 16)

    def body(sel_ref, own_ref, rc_ref, o_ref):
        rc = rc_ref[...]
        o_ref[...] = ((own_ref[...].astype(_F32) + rc[0].astype(_F32)) + rc[1].astype(_F32)) + rc[2].astype(_F32)

    gs = pltpu.PrefetchScalarGridSpec(
        num_scalar_prefetch=1, grid=(r // tm,),
        in_specs=[pl.BlockSpec((None, tm, c2), lambda i, q: (q[0], i, 0)),
                  pl.BlockSpec((3, tm, c2), lambda i, q: (0, i, 0))],
        out_specs=pl.BlockSpec((None, tm, c2), lambda i, q: (q[1], i, 0)))
    return _call(body, name="shard_sum", grid_spec=gs, out_shape=_sds((2, r, c2), _F32),
                 compiler_params=_cparams("parallel"))(sel, sums, recv)


def _sum_slots(stack, name):
    k, r, c = stack.shape
    tm = _tile(r, 256, 16 if stack.dtype == _BF else 8)

    def fn(v):
        out = v[0].astype(_F32)
        for i in range(1, k):
            out = out + v[i].astype(_F32)
        return (out,)

    return _rowcall(fn, [stack], [pl.BlockSpec((k, tm, c), lambda j, i: (0, i, 0))], [_sds((r, c), _F32)],
                    [_rb(tm, c)], [False], grid=(1, r // tm), name=name)[0]


def _share_halves(bufs):
    n = len(bufs)

    def body(*refs):
        outs = refs[n:2 * n]
        ssem, rsem = refs[2 * n:]
        x, y, c = _me()
        sib = (x, y, 1 - c)
        cps = []
        for a in range(n):
            cp = _rcopy(outs[a].at[c], outs[a].at[c], ssem.at[a], rsem.at[a], sib)
            cp.start()
            cps.append(cp)
        for a in range(n):
            _rcopy(outs[a].at[c], outs[a].at[1 - c], ssem.at[a], rsem.at[a], sib).wait_recv()
        for cp in cps:
            cp.wait_send()

    dma = pltpu.SemaphoreType.DMA
    return _call(body, name="share_halves", in_specs=[_ANY] * n, out_specs=[_ANY] * n,
                 out_shape=[_sds(b.shape, b.dtype) for b in bufs], input_output_aliases={a: a for a in range(n)},
                 scratch_shapes=[dma((n,)), dma((n,))])(*bufs)


def _adamw_math(w, g, m, v):
    m = _B1 * m + (1.0 - _B1) * g
    v = _B2 * v + (1.0 - _B2) * jnp.square(g)
    m_hat = m / (1.0 - _B1 ** _STEP)
    v_hat = v / (1.0 - _B2 ** _STEP)
    delta = -_LR * (m_hat / (jnp.sqrt(v_hat) + _ADAM_EPS) + _WD * w)
    return delta, m, v


def _adamw_shard(w, g2, m, v, name):
    r, c = w.shape
    c2 = c // 2
    tm = _tile(r, 256, 8)
    blk = pl.BlockSpec((tm, c2), lambda h, i: (i, h))

    def fn(wb, gb, mb, vb):
        return (gb,) + _adamw_math(wb, gb, mb, vb)

    return _rowcall(fn, [w, g2, m, v], [blk, pl.BlockSpec((None, tm, c2), lambda h, i: (h, i, 0)), blk, blk],
                    [_sds((r, c), _F32)] * 4, [blk] * 4, [False] * 4, grid=(2, r // tm), name=name)


def _adamw_whole(w, g, m, v, name):
    r, c = w.shape
    blk = _full((r, c))
    return _rowcall(lambda *a: _adamw_math(*a), [w, g, m, v], [blk] * 4, [_sds((r, c), _F32)] * 3, [blk] * 3,
                    [False] * 3, grid=(1, 1), name=name)


def _pack(arrs):
    parts = []
    for a in arrs:
        f = a.reshape(-1).astype(_F32)
        pad = (-f.shape[0]) % (8 * _LANE)
        if pad:
            f = jnp.concatenate([f, jnp.zeros((pad,), _F32)])
        parts.append(f)
    return jnp.concatenate(parts).reshape(-1, _LANE)


def _unpack(buf, like):
    flat = buf.reshape(-1)
    outs, off = [], 0
    for a in like:
        nel = a.size
        outs.append(flat[off:off + nel].reshape(a.shape))
        off += nel + ((-nel) % (8 * _LANE))
    return outs


def kernel(x, ln_mix_g, w_in, s5_a_re, s5_a_im, s5_log_dt, s5_b_re, s5_b_im, s5_c_re, s5_c_im, s5_d, s5_w_glu, s5_b_glu, w_proj_s5, hgrn_lb_logits, hgrn_norm_g, w_proj_hgrn, w_out, ln_ffn_g, w_up, conv_w, conv_b, w_down, ln_final_g, loss_target, m_ln_mix_g, m_w_in, m_s5_a_re, m_s5_a_im, m_s5_log_dt, m_s5_b_re, m_s5_b_im, m_s5_c_re, m_s5_c_im, m_s5_d, m_s5_w_glu, m_s5_b_glu, m_w_proj_s5, m_hgrn_lb_logits, m_hgrn_norm_g, m_w_proj_hgrn, m_w_out, m_ln_ffn_g, m_w_up, m_conv_w, m_conv_b, m_w_down, m_ln_final_g, v_ln_mix_g, v_w_in, v_s5_a_re, v_s5_a_im, v_s5_log_dt, v_s5_b_re, v_s5_b_im, v_s5_c_re, v_s5_c_im, v_s5_d, v_s5_w_glu, v_s5_b_glu, v_w_proj_s5, v_hgrn_lb_logits, v_hgrn_norm_g, v_w_proj_hgrn, v_w_out, v_ln_ffn_g, v_w_up, v_conv_w, v_conv_b, v_w_down, v_ln_final_g):
    assert x.shape[0] == 1 and w_in.shape[0] == 1, "one example per device, one layer"
    t, d = x.shape[1], x.shape[2]
    w5 = s5_w_glu.shape[2]
    hw = hgrn_norm_g.shape[1]
    ng_, np_, gc = s5_b_re.shape[1], s5_b_re.shape[2], s5_b_re.shape[3]
    dff = w_down.shape[1] * _NCHIP
    assert gc * _S5_SET == _LANE and ng_ * gc == w5 and hw % _LANE == 0
    gs_off = w5 + 4 * hw
    ci = lax.axis_index("c")
    xt = x.reshape(t, d)
    tgt = loss_target.reshape(t, d)

    big_names = ["w_in", "s5_w_glu", "w_proj_s5", "w_proj_hgrn", "w_out", "w_up", "w_down"]
    big_w = dict(w_in=w_in[0], s5_w_glu=s5_w_glu[0], w_proj_s5=w_proj_s5[0], w_proj_hgrn=w_proj_hgrn[0],
                 w_out=w_out[0], w_up=w_up[0], w_down=w_down[0])
    chip = 2 * lax.axis_index("x") + lax.axis_index("y")
    sel_chip = jnp.stack([chip, ci]).astype(jnp.int32)
    slots = {k: _cast_into_slot(big_w[k], sel_chip) for k in big_names}
    g_in, g_cw = _run_plan(_gather_plan([slots["w_in"]], [conv_w[0]]), "gather_w_in")
    cw = g_cw.transpose(1, 0, 2).reshape(3, 2 * dff)
    cb = conv_b

    tm_big = _tile(t, 1024, 8)

    h1 = _rms_fwd(xt, ln_mix_g, "rms1_fwd")
    nin_s = g_in.shape[2]
    proj, (g_glu, g_ps5, g_ph, g_out, g_up) = _mm(
        h1, g_in, "nn", _F32, tm=tm_big, tn=_tile(nin_s, 1152), tk=d, name="mm_proj",
        rider=_gather_plan([slots[k] for k in ("s5_w_glu", "w_proj_s5", "w_proj_hgrn", "w_out", "w_up")], []))
    wglu = g_glu.reshape(w5, w5)
    wout = g_out.reshape(d, d)

    abar_r, abar_i, coef_r, coef_i = _s5_prep(s5_a_re[0], s5_a_im[0], s5_log_dt.reshape(ng_, 1))
    lanes = ng_ * np_
    par = (abar_r.reshape(1, lanes), abar_i.reshape(1, lanes), coef_r.reshape(1, lanes), coef_i.reshape(1, lanes),
           _bd_in(s5_b_re[0]), _bd_in(s5_b_im[0]), _bd_out(s5_c_re[0]), -_bd_out(s5_c_im[0]), s5_d.reshape(1, w5))
    y_s5, car, cai = _s5_fwd(proj, par, t, w5)
    z = _s5glu_fwd(y_s5, wglu, s5_b_glu)
    ys = _mm(z, g_ps5, "nn", _F32, tm=tm_big, tn=g_ps5.shape[2], tk=w5, name="mm_proj_s5")

    lb = _lb_prep(hgrn_lb_logits)
    og, st_all = _hgrn_fwd(proj, lb, hgrn_norm_g, t, w5, hw)
    yh = _mm(og, g_ph, "nn", _F32, tm=tm_big, tn=g_ph.shape[2], tk=hw, name="mm_proj_hgrn")

    merged = _merge_fwd(proj, ys, yh, t, d, gs_off)
    x2 = _mm(merged, wout, "nn", _F32, tm=tm_big, tn=_tile(d, 1024), tk=d, res=xt, name="mm_out")
    h2 = _rms_fwd(x2, ln_ffn_g, "rms2_fwd")
    up_s = g_up.shape[2]
    up, (g_down,) = _mm(h2, g_up, "nn", _F32, tm=tm_big, tn=_tile(up_s, 1408), tk=d, name="mm_up",
                        rider=_gather_plan([slots["w_down"]], []))
    wdown = g_down.reshape(dff, d)
    act = _convact_fwd(up, cw, cb, t, dff)
    x3 = _mm(act, wdown, "nn", _F32, tm=tm_big, tn=_tile(d, 1024), tk=_tile(dff, 1408), res=x2, name="mm_down")
    loss_part, dx3, dx3b, d_gfin = _loss_head(x3, tgt, ln_final_g.reshape(1, d))

    tk_t = _tile(t, 512, 8)
    dact = _mm(dx3b, wdown, "nt", _F32, tm=tm_big, tn=_tile(dff, 1408), tk=d, name="mm_dact")
    r_down = dff // _NCHIP
    gw_down = _mm(act, dx3b, "tn", _BF, tm=_tile(r_down, 1408), tn=_tile(d // 2, 1024), tk=tk_t, halves="rows",
                  name="mm_gw_down")
    dup, d_cw, d_cb = _convact_bwd(up, dact, cw, cb, t, dff)
    dh2 = _mm(dup, g_up, "nt", _F32, tm=tm_big, tn=_tile(d, 1024), tk=_tile(up_s, 1408), name="mm_dh2")
    gw_up = _mm(h2, dup, "tn", _BF, tm=_tile(d, 1024), tn=_tile(up_s // 2, 1408), tk=tk_t, halves="cols", name="mm_gw_up")
    dx2, dx2b, d_gffn = _rms_bwd(x2, ln_ffn_g, dh2, dx3, "rms2_bwd")
    dmerged = _mm(dx2b, wout, "nt", _F32, tm=tm_big, tn=_tile(d, 1024), tk=d, name="mm_dmerged")
    gw_out = _mm(merged, dx2b, "tn", _BF, tm=_tile(d // _NCHIP, 1024), tn=_tile(d // 2, 1024), tk=tk_t, halves="rows",
                 name="mm_gw_out")
    dys, dyh, dgs, dgh = _merge_bwd(proj, ys, yh, dmerged, t, d, gs_off)
    ps_s = g_ps5.shape[2]
    dz = _mm(dys, g_ps5, "nt", _F32, tm=tm_big, tn=_tile(w5, 1024), tk=ps_s, name="mm_dz")
    gw_ps5 = _mm(z, dys, "tn", _BF, tm=_tile(w5, 1024), tn=ps_s // 2, tk=tk_t, halves="cols", name="mm_gw_ps5")
    dog = _mm(dyh, g_ph, "nt", _F32, tm=tm_big, tn=_tile(hw, 1024), tk=ps_s, name="mm_dog")
    gw_ph = _mm(og, dyh, "tn", _BF, tm=_tile(hw, 1024), tn=ps_s // 2, tk=tk_t, halves="cols", name="mm_gw_ph")
    dy_s5, gw_glu_full, d_bglu = _s5glu_bwd(y_s5, dz, wglu, s5_b_glu)
    r_glu = w5 // _NCHIP
    gw_glu = gw_glu_full.astype(_BF).reshape(_NCHIP, r_glu, 2, w5 // 2).transpose(2, 0, 1, 3)

    def chip_sums(grads, name):
        theirs = _swap_halves(grads, name)
        return [_add_pairs(lax.dynamic_index_in_dim(g, ci, 0, keepdims=False), th) for g, th in zip(grads, theirs)]

    s_glu, s_ps5, s_ph, s_out, s_up, s_down = chip_sums([gw_glu, gw_ps5, gw_ph, gw_out, gw_up, gw_down], "swap_halves_a")

    s5g = _s5_bwd(proj, dy_s5, car, cai, par, t, w5)
    du = s5g[0]
    dq, df, di, dg, d_lb, d_ng = _hgrn_bwd(proj, lb, hgrn_norm_g, st_all, dog, t, w5, hw)
    dproj = jnp.concatenate([du, dq, df, di, dg, dgs, dgh], axis=1)
    dh1, (r_up,) = _mm(dproj, g_in, "nt", _F32, tm=tm_big, tn=_tile(d, 1024), tk=_tile(nin_s, 1152), name="mm_dh1",
                       rider=_exchange_plan([s_up], None))
    dx, _, d_gmix = _rms_bwd(xt, ln_mix_g, dh1, dx2, "rms1_bwd")

    d_are, d_aim, d_ldt = _s5_prep_bwd(s5_a_re[0], s5_a_im[0], s5_log_dt.reshape(ng_, 1),
                                       [s5g[k].reshape(ng_, np_) for k in (1, 2, 3, 4)])
    d_bre = _bd_in_grad(s5g[5], np_, gc)
    d_bim = _bd_in_grad(s5g[6], np_, gc)
    d_cre = _bd_out_grad(s5g[7], np_, gc)
    d_cim = -_bd_out_grad(s5g[8], np_, gc)
    d_logits = _lb_prep_bwd(hgrn_lb_logits, d_lb)

    small_names = ["ln_mix_g", "s5_a_re", "s5_a_im", "s5_log_dt", "s5_b_re", "s5_b_im", "s5_c_re", "s5_c_im", "s5_d",
                   "s5_b_glu", "hgrn_lb_logits", "hgrn_norm_g", "ln_ffn_g", "conv_b", "ln_final_g"]
    small_w = dict(ln_mix_g=ln_mix_g, s5_a_re=s5_a_re, s5_a_im=s5_a_im, s5_log_dt=s5_log_dt, s5_b_re=s5_b_re,
                   s5_b_im=s5_b_im, s5_c_re=s5_c_re, s5_c_im=s5_c_im, s5_d=s5_d, s5_b_glu=s5_b_glu,
                   hgrn_lb_logits=hgrn_lb_logits, hgrn_norm_g=hgrn_norm_g, ln_ffn_g=ln_ffn_g, conv_b=conv_b,
                   ln_final_g=ln_final_g)
    small_m = dict(ln_mix_g=m_ln_mix_g, s5_a_re=m_s5_a_re, s5_a_im=m_s5_a_im, s5_log_dt=m_s5_log_dt, s5_b_re=m_s5_b_re,
                   s5_b_im=m_s5_b_im, s5_c_re=m_s5_c_re, s5_c_im=m_s5_c_im, s5_d=m_s5_d, s5_b_glu=m_s5_b_glu,
                   hgrn_lb_logits=m_hgrn_lb_logits, hgrn_norm_g=m_hgrn_norm_g, ln_ffn_g=m_ln_ffn_g, conv_b=m_conv_b,
                   ln_final_g=m_ln_final_g)
    small_v = dict(ln_mix_g=v_ln_mix_g, s5_a_re=v_s5_a_re, s5_a_im=v_s5_a_im, s5_log_dt=v_s5_log_dt, s5_b_re=v_s5_b_re,
                   s5_b_im=v_s5_b_im, s5_c_re=v_s5_c_re, s5_c_im=v_s5_c_im, s5_d=v_s5_d, s5_b_glu=v_s5_b_glu,
                   hgrn_lb_logits=v_hgrn_lb_logits, hgrn_norm_g=v_hgrn_norm_g, ln_ffn_g=v_ln_ffn_g, conv_b=v_conv_b,
                   ln_final_g=v_ln_final_g)
    small_g = dict(ln_mix_g=d_gmix, s5_a_re=d_are, s5_a_im=d_aim, s5_log_dt=d_ldt, s5_b_re=d_bre, s5_b_im=d_bim,
                   s5_c_re=d_cre, s5_c_im=d_cim, s5_d=s5g[9], s5_b_glu=d_bglu, hgrn_lb_logits=d_logits,
                   hgrn_norm_g=d_ng, ln_ffn_g=d_gffn, conv_b=d_cb, ln_final_g=d_gfin)
    like = [small_w[k] for k in small_names]
    g_pack = _pack([small_g[k] for k in small_names] + [d_cw])
    gw_in, (r_glu_, r_ps5, r_ph, r_out, r_down, r_small) = _mm(
        h1, dproj, "tn", _BF, tm=_tile(d, 1024), tn=_tile(nin_s // 2, 1152), tk=tk_t, halves="cols", name="mm_gw_in",
        rider=_exchange_plan([s_glu, s_ps5, s_ph, s_out, s_down], g_pack))
    (s_in,) = chip_sums([gw_in], "swap_halves_b")
    (r_in,) = _run_plan(_exchange_plan([s_in], None), "exchange_w_in")
    sums = [s_in, s_glu, s_ps5, s_ph, s_out, s_up, s_down]
    received = [r_in, r_glu_, r_ps5, r_ph, r_out, r_up, r_down]
    halves = [_shard_sum(sm, rc, sel_chip) for sm, rc in zip(sums, received)]
    g_all = _sum_slots(r_small, "small_sum")
    full = _share_halves(halves)
    w_pack = _pack(like)
    g_small = g_all[:w_pack.shape[0]]
    cs = conv_w.shape[2]
    g_cw_full = g_all[w_pack.shape[0]:].reshape(-1)[:3 * 2 * dff].reshape(3, 2 * dff)
    g_cw = lax.dynamic_slice_in_dim(g_cw_full, chip * cs, cs, axis=1)

    big_m = dict(w_in=m_w_in, s5_w_glu=m_s5_w_glu, w_proj_s5=m_w_proj_s5, w_proj_hgrn=m_w_proj_hgrn, w_out=m_w_out,
                 w_up=m_w_up, w_down=m_w_down)
    big_v = dict(w_in=v_w_in, s5_w_glu=v_s5_w_glu, w_proj_s5=v_w_proj_s5, w_proj_hgrn=v_w_proj_hgrn, w_out=v_w_out,
                 w_up=v_w_up, w_down=v_w_down)
    res = {}
    for k, g2 in zip(big_names, full):
        w2 = big_w[k]
        shp = (1,) + w2.shape
        outs = _adamw_shard(w2, g2, big_m[k][0], big_v[k][0], "adamw_" + k)
        res[k] = [o.reshape(shp) for o in outs]
    sm_outs = _adamw_whole(w_pack, g_small, _pack([small_m[k] for k in small_names]),
                           _pack([small_v[k] for k in small_names]), "adamw_small")
    sm_g = _unpack(g_small, like)
    sm_d, sm_m, sm_v = (_unpack(o, like) for o in sm_outs)
    for i, k in enumerate(small_names):
        res[k] = [sm_g[i], sm_d[i], sm_m[i], sm_v[i]]

    cw_outs = _adamw_whole(conv_w[0], g_cw, m_conv_w[0], v_conv_w[0], "adamw_conv_w")
    res["conv_w"] = [g_cw.reshape(conv_w.shape)] + [o.reshape(conv_w.shape) for o in cw_outs]

    loss = lax.psum(loss_part[0, 0], ("x", "y", "c"))
    order = ["ln_mix_g", "w_in", "s5_a_re", "s5_a_im", "s5_log_dt", "s5_b_re", "s5_b_im", "s5_c_re", "s5_c_im", "s5_d",
             "s5_w_glu", "s5_b_glu", "w_proj_s5", "hgrn_lb_logits", "hgrn_norm_g", "w_proj_hgrn", "w_out", "ln_ffn_g",
             "w_up", "conv_w", "conv_b", "w_down", "ln_final_g"]
    return (loss, dx.reshape(x.shape), *[res[k][0] for k in order], *[res[k][1] for k in order],
            *[res[k][2] for k in order], *[res[k][3] for k in order])
```

```python
import functools

import jax
import jax.numpy as jnp
from jax import lax
from jax.experimental import pallas as pl
from jax.experimental.pallas import tpu as pltpu

_F32 = jnp.float32
_BF = jnp.bfloat16
_RMS_EPS = 1e-6
_S5_MAX_RE = -1e-4
_LR, _B1, _B2, _ADAM_EPS, _WD, _STEP = 0.001, 0.9, 0.999, 1e-08, 0.01, 10
_MESH = pl.DeviceIdType.MESH
_ANY = pl.BlockSpec(memory_space=pl.ANY)
_LANE = 128
_VMEM_LIMIT = 56 * 1024 * 1024
_CHUNK = 64
_S5_TB = 128
_S5_SET = 8
_HGRN_HP = 2
_HGRN_SEG = 1024
_NCHIP = 4
_NDEV = 8


def _call(body, **kw):
    return pl.pallas_call(body, **kw)


def _cparams(*sem):
    return pltpu.CompilerParams(dimension_semantics=sem, vmem_limit_bytes=_VMEM_LIMIT)


def _tile(n, pref, unit=_LANE):
    if n <= pref:
        return n
    t = (pref // unit) * unit
    while t >= unit:
        if n % t == 0:
            return t
        t -= unit
    raise ValueError(f"no tile for {n}")


_NN = ((1,), (0,))
_NT = ((1,), (1,))
_TN = ((0,), (0,))


def _dg(a, b, dims):
    return lax.dot_general(a.astype(_BF), b.astype(_BF), (dims, ((), ())), preferred_element_type=_F32)


@jax.custom_vjp
def _bdot(a, b):
    return _dg(a, b, _NN)


def _bdot_f(a, b):
    return _dg(a, b, _NN), (a, b)


def _bdot_b(res, g):
    a, b = res
    return _dg(g, b, _NT).astype(a.dtype), _dg(a, g, _TN).astype(b.dtype)


_bdot.defvjp(_bdot_f, _bdot_b)


@jax.custom_vjp
def _bdot_nt(a, b):
    return _dg(a, b, _NT)


def _bdot_nt_f(a, b):
    return _dg(a, b, _NT), (a, b)


def _bdot_nt_b(res, g):
    a, b = res
    return _dg(g, b, _NN).astype(a.dtype), _dg(g, a, _TN).astype(b.dtype)


_bdot_nt.defvjp(_bdot_nt_f, _bdot_nt_b)


@jax.custom_vjp
def _bdot_tn(a, b):
    return _dg(a, b, _TN)


def _bdot_tn_f(a, b):
    return _dg(a, b, _TN), (a, b)


def _bdot_tn_b(res, g):
    a, b = res
    return _dg(b, g, _NT).astype(a.dtype), _dg(a, g, _NN).astype(b.dtype)


_bdot_tn.defvjp(_bdot_tn_f, _bdot_tn_b)


def _shift_up(x, n):
    r = x.shape[0]
    row = lax.broadcasted_iota(jnp.int32, x.shape, 0)
    return jnp.where(row < r - n, pltpu.roll(x, r - n, 0), 0.0)


@functools.partial(jax.custom_vjp, nondiff_argnums=(1,))
def _shift_down(x, n):
    row = lax.broadcasted_iota(jnp.int32, x.shape, 0)
    return jnp.where(row >= n, pltpu.roll(x, n, 0), 0.0)


def _shift_down_f(x, n):
    return _shift_down(x, n), None


def _shift_down_b(n, _, g):
    return (_shift_up(g, n),)


_shift_down.defvjp(_shift_down_f, _shift_down_b)


def _sigmoid(x):
    return 1.0 / (1.0 + jnp.exp(-x))


def _silu(x):
    return x * _sigmoid(x)


def _gelu(x):
    return 0.5 * x * (1.0 + jnp.tanh(0.7978845608028654 * (x + 0.044715 * (x * x * x))))


def _rms_core(x, g):
    return x * lax.rsqrt(jnp.mean(x * x, axis=-1, keepdims=True) + _RMS_EPS) * g


def _mm(a, b, mode, out_dtype, *, tm, tn, tk, res=None, halves=None, rider=None, name):
    if mode == "nn":
        m, k = a.shape
        a_spec = pl.BlockSpec((tm, tk), lambda i, j, kk: (i, kk))
        if b.ndim == 3:
            s, _, ns = b.shape
            n = s * ns
            npb = ns // tn
            b_spec = pl.BlockSpec((None, tk, tn), lambda i, j, kk: (j // npb, kk, j % npb))
        else:
            n = b.shape[1]
            b_spec = pl.BlockSpec((tk, tn), lambda i, j, kk: (kk, j))
        dims = _NN
    elif mode == "nt":
        m, k = a.shape
        a_spec = pl.BlockSpec((tm, tk), lambda i, j, kk: (i, kk))
        if b.ndim == 3:
            s, n, ks = b.shape
            kpb = ks // tk
            b_spec = pl.BlockSpec((None, tn, tk), lambda i, j, kk: (kk // kpb, j, kk % kpb))
        else:
            n = b.shape[0]
            b_spec = pl.BlockSpec((tn, tk), lambda i, j, kk: (j, kk))
        dims = _NT
    else:
        k, m = a.shape
        n = b.shape[1]
        a_spec = pl.BlockSpec((tk, tm), lambda i, j, kk: (kk, i))
        b_spec = pl.BlockSpec((tk, tn), lambda i, j, kk: (kk, j))
        dims = _TN
    nk = k // tk
    if halves is None:
        out_shape = jax.ShapeDtypeStruct((m, n), out_dtype)
        out_spec = pl.BlockSpec((tm, tn), lambda i, j, kk: (i, j))
    elif halves == "cols":
        c2 = n // (2 * _NCHIP)
        tpc = c2 // tn
        out_shape = jax.ShapeDtypeStruct((2, _NCHIP, m, c2), out_dtype)
        out_spec = pl.BlockSpec((None, None, tm, tn),
                                lambda i, j, kk: ((j // tpc) % 2, j // (2 * tpc), i, j % tpc))
    else:
        c2 = n // 2
        tpc = c2 // tn
        r = m // _NCHIP
        tpr = r // tm
        out_shape = jax.ShapeDtypeStruct((2, _NCHIP, r, c2), out_dtype)
        out_spec = pl.BlockSpec((None, None, tm, tn),
                                lambda i, j, kk: (j // tpc, i // tpr, i % tpr, j % tpc))
    has_res = res is not None
    nreg = 3 if has_res else 2
    ni, nj = m // tm, n // tn
    r_ins = list(rider.ins) if rider else []
    r_outs = list(rider.outs) if rider else []

    def body(*refs):
        a_ref, b_ref = refs[0], refs[1]
        r_ref = refs[2] if has_res else None
        rin = refs[nreg:nreg + len(r_ins)]
        o_ref = refs[nreg + len(r_ins)]
        rout = refs[nreg + len(r_ins) + 1:nreg + len(r_ins) + 1 + len(r_outs)]
        acc_ref = refs[nreg + len(r_ins) + 1 + len(r_outs)]
        sems = refs[nreg + len(r_ins) + 2 + len(r_outs):]
        i, j, kk = pl.program_id(0), pl.program_id(1), pl.program_id(2)

        if rider:
            @pl.when(jnp.logical_and(jnp.logical_and(i == 0, j == 0), kk == 0))
            def _():
                rider.start(rin, rout, sems)

        @pl.when(kk == 0)
        def _():
            acc_ref[...] = jnp.zeros_like(acc_ref)

        acc_ref[...] += _dg(a_ref[...], b_ref[...], dims)

        @pl.when(kk == nk - 1)
        def _():
            out = acc_ref[...]
            if has_res:
                out = out + r_ref[...]
            o_ref[...] = out.astype(out_dtype)

        if rider:
            @pl.when(jnp.logical_and(jnp.logical_and(i == ni - 1, j == nj - 1), kk == nk - 1))
            def _():
                rider.finish(rin, rout, sems)

    in_specs = [a_spec, b_spec]
    args = [a, b]
    if has_res:
        in_specs.append(pl.BlockSpec((tm, tn), lambda i, j, kk: (i, j)))
        args.append(res)
    if not rider:
        return _call(body, name=name, grid=(ni, nj, nk), in_specs=in_specs, out_specs=out_spec,
                     out_shape=out_shape, scratch_shapes=[pltpu.VMEM((tm, tn), _F32)],
                     compiler_params=_cparams("parallel", "parallel", "arbitrary"))(*args)
    res_all = _call(body, name=name, grid=(ni, nj, nk), in_specs=in_specs + [_ANY] * len(r_ins),
                    out_specs=[out_spec] + [_ANY] * len(r_outs), out_shape=[out_shape] + r_outs,
                    input_output_aliases={nreg + k: 1 + v for k, v in rider.aliases.items()},
                    scratch_shapes=[pltpu.VMEM((tm, tn), _F32)] + list(rider.sems),
                    compiler_params=_cparams("arbitrary", "arbitrary", "arbitrary"))(*args, *r_ins)
    return res_all[0], list(res_all[1:])


def _rowcall(fn, ins, in_specs, outs, out_specs, acc, *, grid, name):
    nin = len(ins)

    def body(*refs):
        vals = fn(*[r[...] for r in refs[:nin]])
        first = pl.program_id(1) == 0
        for k, (o_ref, v) in enumerate(zip(refs[nin:], vals)):
            if acc[k]:
                @pl.when(first)
                def _(o_ref=o_ref):
                    o_ref[...] = jnp.zeros_like(o_ref)
                o_ref[...] += v.astype(o_ref.dtype)
            else:
                o_ref[...] = v.astype(o_ref.dtype)

    return _call(body, name=name, grid=grid, in_specs=in_specs, out_specs=out_specs, out_shape=outs,
                 compiler_params=_cparams("parallel", "arbitrary"))(*ins)


def _rb(tm, w, cb=0):
    return pl.BlockSpec((tm, w), lambda j, i: (i, cb + j))


def _cb(w, cb=0):
    return pl.BlockSpec((1, w), lambda j, i: (0, cb + j))


def _full(shape):
    nd = len(shape)
    return pl.BlockSpec(shape, lambda j, i: (0,) * nd)


def _sds(shape, dtype):
    return jax.ShapeDtypeStruct(shape, dtype)


def _rms_fwd(x, g, name):
    t, d = x.shape
    tm = _tile(t, 512, 8)
    return _rowcall(lambda xb, gb: (_rms_core(xb, gb),), [x, g], [_rb(tm, d), _full((1, d))],
                    [_sds((t, d), _BF)], [_rb(tm, d)], [False], grid=(1, t // tm), name=name)[0]


def _rms_bwd(x, g, dh, dres, name):
    t, d = x.shape
    tm = _tile(t, 256, 8)

    def fn(xb, gb, dhb, drb):
        _, vjp = jax.vjp(_rms_core, xb, gb)
        dx, dg = vjp(dhb)
        dx = dx + drb
        return dx, dx, dg

    return _rowcall(fn, [x, g, dh, dres], [_rb(tm, d), _full((1, d)), _rb(tm, d), _rb(tm, d)],
                    [_sds((t, d), _F32), _sds((t, d), _BF), _sds((1, d), _F32)],
                    [_rb(tm, d), _rb(tm, d), _full((1, d))], [False, False, True],
                    grid=(1, t // tm), name=name)


def _loss_head(x3, tgt, g):
    t, d = x3.shape
    tm = _tile(t, 256, 8)

    def fn(xb, tb, gb):
        y, vjp = jax.vjp(_rms_core, xb, gb)
        e = y - tb
        part = 0.5 * jnp.sum(jnp.mean(e * e, axis=-1, keepdims=True), axis=0, keepdims=True)
        dx, dg = vjp(e * (1.0 / d))
        return jnp.broadcast_to(part, (1, _LANE)), dx, dx, dg

    return _rowcall(fn, [x3, tgt, g], [_rb(tm, d), _rb(tm, d), _full((1, d))],
                    [_sds((1, _LANE), _F32), _sds((t, d), _F32), _sds((t, d), _BF), _sds((1, d), _F32)],
                    [_full((1, _LANE)), _rb(tm, d), _rb(tm, d), _full((1, d))], [True, False, False, True],
                    grid=(1, t // tm), name="loss_head")


def _s5_disc(a_re, a_im, log_dt):
    lam_re = jnp.minimum(a_re, _S5_MAX_RE)
    lam_im = a_im
    dt = jnp.exp(log_dt)
    mag = jnp.exp(lam_re * dt)
    abar_re = mag * jnp.cos(lam_im * dt)
    abar_im = mag * jnp.sin(lam_im * dt)
    den = lam_re * lam_re + lam_im * lam_im
    nr = abar_re - 1.0
    ni = abar_im
    coef_re = (nr * lam_re + ni * lam_im) / den
    coef_im = (ni * lam_re - nr * lam_im) / den
    return abar_re, abar_im, coef_re, coef_im


def _s5_prep(a_re, a_im, log_dt):
    g, p = a_re.shape

    def body(ar, ai, ld, o0, o1, o2, o3):
        outs = _s5_disc(ar[...], ai[...], ld[...])
        for o, v in zip((o0, o1, o2, o3), outs):
            o[...] = v

    return _call(body, name="s5_prep", out_shape=[_sds((g, p), _F32)] * 4)(a_re, a_im, log_dt)


def _s5_prep_bwd(a_re, a_im, log_dt, cts):
    g, p = a_re.shape

    def body(ar, ai, ld, c0, c1, c2, c3, d0, d1, d2):
        _, vjp = jax.vjp(_s5_disc, ar[...], ai[...], ld[...])
        outs = vjp((c0[...], c1[...], c2[...], c3[...]))
        for o, v in zip((d0, d1, d2), outs):
            o[...] = v

    return _call(body, name="s5_prep_bwd",
                 out_shape=[_sds((g, p), _F32), _sds((g, p), _F32), _sds((g, 1), _F32)])(a_re, a_im, log_dt, *cts)


def _s5_block(u, car, cai, ar, ai, cr, ci, b_re, b_im, c_re, c_imn, dvec):
    tb = u.shape[0]
    bur = _bdot(u, b_re)
    bui = _bdot(u, b_im)
    sr = cr * bur - ci * bui
    si = cr * bui + ci * bur
    row = lax.broadcasted_iota(jnp.int32, sr.shape, 0)
    sr = sr + jnp.where(row == 0, ar * car - ai * cai, 0.0)
    si = si + jnp.where(row == 0, ar * cai + ai * car, 0.0)
    pr, pi = ar, ai
    sh = 1
    while sh < tb:
        dr = _shift_down(sr, sh)
        di = _shift_down(si, sh)
        sr, si = sr + pr * dr - pi * di, si + pr * di + pi * dr
        pr, pi = pr * pr - pi * pi, 2.0 * pr * pi
        sh *= 2
    y = _bdot(sr, c_re) + _bdot(si, c_imn) + dvec * u
    last = row == tb - 1
    ncr = jnp.sum(jnp.where(last, sr, 0.0), axis=0, keepdims=True)
    nci = jnp.sum(jnp.where(last, si, 0.0), axis=0, keepdims=True)
    return y, ncr, nci


def _s5_specs(tb, lw, nt, rev):
    tmap = (lambda t: nt - 1 - t) if rev else (lambda t: t)
    vec = pl.BlockSpec((1, lw), lambda s, t: (0, s))
    return dict(
        u=pl.BlockSpec((tb, _LANE), lambda s, t: (tmap(t), s)),
        car=pl.BlockSpec((None, 1, lw), lambda s, t: (tmap(t), 0, s)),
        vec=vec,
        bmat=pl.BlockSpec((None, _LANE, lw), lambda s, t: (s, 0, 0)),
        cmat=pl.BlockSpec((None, lw, _LANE), lambda s, t: (s, 0, 0)),
        dvec=pl.BlockSpec((1, _LANE), lambda s, t: (0, s)),
    )


def _s5_fwd(proj, par, t, w5, rider=None):
    ar, ai, cr, ci, b_re, b_im, c_re, c_imn, dvec = par
    ns = w5 // _LANE
    lw = ar.shape[1] // ns
    tb = min(_S5_TB, t)
    nt = t // tb
    sp = _s5_specs(tb, lw, nt, False)

    def body(u_ref, ar_r, ai_r, cr_r, ci_r, bre_r, bim_r, cre_r, cim_r, d_r, y_ref, car_ref, cai_ref, s_r, s_i):
        @pl.when(pl.program_id(1) == 0)
        def _():
            s_r[...] = jnp.zeros_like(s_r)
            s_i[...] = jnp.zeros_like(s_i)

        car_ref[...] = s_r[...]
        cai_ref[...] = s_i[...]
        y, ncr, nci = _s5_block(u_ref[...], s_r[...], s_i[...], ar_r[...], ai_r[...], cr_r[...], ci_r[...],
                                bre_r[...], bim_r[...], cre_r[...], cim_r[...], d_r[...])
        y_ref[...] = y
        s_r[...] = ncr
        s_i[...] = nci

    kw = dict(name="s5_fwd", grid=(ns, nt),
              in_specs=[sp["u"], sp["vec"], sp["vec"], sp["vec"], sp["vec"], sp["bmat"], sp["bmat"],
                        sp["cmat"], sp["cmat"], sp["dvec"]],
              out_specs=[sp["u"], sp["car"], sp["car"]],
              out_shape=[_sds((t, w5), _F32), _sds((nt, 1, ns * lw), _F32), _sds((nt, 1, ns * lw), _F32)],
              scratch_shapes=[pltpu.VMEM((1, lw), _F32), pltpu.VMEM((1, lw), _F32)])
    args = (proj, ar, ai, cr, ci, b_re, b_im, c_re, c_imn, dvec)
    if rider is None:
        return _call(body, compiler_params=_cparams("parallel", "arbitrary"), **kw)(*args), []
    return _call_riding(body, rider, args=args, **kw)


def _s5_bwd(proj, dy, car, cai, par, t, w5, rider=None):
    ar, ai, cr, ci, b_re, b_im, c_re, c_imn, dvec = par
    ns = w5 // _LANE
    lw = ar.shape[1] // ns
    tb = min(_S5_TB, t)
    nt = t // tb
    sp = _s5_specs(tb, lw, nt, True)

    def body(u_ref, dy_ref, car_ref, cai_ref, ar_r, ai_r, cr_r, ci_r, bre_r, bim_r, cre_r, cim_r, d_r,
             du_ref, g_ar, g_ai, g_cr, g_ci, g_bre, g_bim, g_cre, g_cim, g_d, ds_r, ds_i):
        accs = (g_ar, g_ai, g_cr, g_ci, g_bre, g_bim, g_cre, g_cim, g_d)

        @pl.when(pl.program_id(1) == 0)
        def _():
            ds_r[...] = jnp.zeros_like(ds_r)
            ds_i[...] = jnp.zeros_like(ds_i)
            for o in accs:
                o[...] = jnp.zeros_like(o)

        _, vjp = jax.vjp(_s5_block, u_ref[...], car_ref[...], cai_ref[...], ar_r[...], ai_r[...], cr_r[...],
                         ci_r[...], bre_r[...], bim_r[...], cre_r[...], cim_r[...], d_r[...])
        grads = vjp((dy_ref[...], ds_r[...], ds_i[...]))
        du_ref[...] = grads[0].astype(_BF)
        ds_r[...] = grads[1]
        ds_i[...] = grads[2]
        for o, gval in zip(accs, grads[3:]):
            o[...] += gval

    vec_o = _sds((1, ns * lw), _F32)
    kw = dict(name="s5_bwd", grid=(ns, nt),
              in_specs=[sp["u"], sp["u"], sp["car"], sp["car"], sp["vec"], sp["vec"], sp["vec"], sp["vec"],
                        sp["bmat"], sp["bmat"], sp["cmat"], sp["cmat"], sp["dvec"]],
              out_specs=[sp["u"], sp["vec"], sp["vec"], sp["vec"], sp["vec"], sp["bmat"], sp["bmat"],
                         sp["cmat"], sp["cmat"], sp["dvec"]],
              out_shape=[_sds((t, w5), _BF), vec_o, vec_o, vec_o, vec_o,
                         _sds(b_re.shape, _F32), _sds(b_re.shape, _F32), _sds(c_re.shape, _F32),
                         _sds(c_re.shape, _F32), _sds((1, w5), _F32)],
              scratch_shapes=[pltpu.VMEM((1, lw), _F32), pltpu.VMEM((1, lw), _F32)])
    args = (proj, dy, car, cai, ar, ai, cr, ci, b_re, b_im, c_re, c_imn, dvec)
    if rider is None:
        return _call(body, compiler_params=_cparams("parallel", "arbitrary"), **kw)(*args), []
    return _call_riding(body, rider, args=args, **kw)


def _bd_in(b):
    g, p, c = b.shape
    s = g // _S5_SET
    b4 = b.reshape(s, _S5_SET, p, c).transpose(0, 1, 3, 2)
    eye = jnp.eye(_S5_SET, dtype=b.dtype)
    return (b4[:, :, :, None, :] * eye[None, :, None, :, None]).reshape(s, _S5_SET * c, _S5_SET * p)


def _bd_in_grad(d, p, c):
    s = d.shape[0]
    eye = jnp.eye(_S5_SET, dtype=d.dtype)
    d5 = d.reshape(s, _S5_SET, c, _S5_SET, p) * eye[None, :, None, :, None]
    return d5.sum(axis=3).transpose(0, 1, 3, 2).reshape(s * _S5_SET, p, c)


def _bd_out(cm):
    g, c, p = cm.shape
    s = g // _S5_SET
    c4 = cm.reshape(s, _S5_SET, c, p).transpose(0, 1, 3, 2)
    eye = jnp.eye(_S5_SET, dtype=cm.dtype)
    return (c4[:, :, :, None, :] * eye[None, :, None, :, None]).reshape(s, _S5_SET * p, _S5_SET * c)


def _bd_out_grad(d, p, c):
    s = d.shape[0]
    eye = jnp.eye(_S5_SET, dtype=d.dtype)
    d5 = d.reshape(s, _S5_SET, p, _S5_SET, c) * eye[None, :, None, :, None]
    return d5.sum(axis=3).transpose(0, 1, 3, 2).reshape(s * _S5_SET, c, p)


def _s5glu_fwd(y, wglu, bglu):
    t, w5 = y.shape
    tm = _tile(t, 256, 8)

    def fn(yb, wb, bb):
        z1 = _gelu(yb)
        a = _dg(z1, wb, _NN) + bb
        return (z1 * _sigmoid(a),)

    return _rowcall(fn, [y, wglu, bglu], [_rb(tm, w5), _full(wglu.shape), _full((1, w5))],
                    [_sds((t, w5), _BF)], [_rb(tm, w5)], [False], grid=(1, t // tm), name="s5glu_fwd")[0]


def _s5glu_bwd(y, dz, wglu, bglu):
    t, w5 = y.shape
    tm = _tile(t, 256, 8)

    def fn(yb, dzb, wb, bb):
        z1, gelu_vjp = jax.vjp(_gelu, yb)
        sig = _sigmoid(_dg(z1, wb, _NN) + bb)
        da = dzb * z1 * sig * (1.0 - sig)
        dz1 = dzb * sig + _dg(da, wb, _NT)
        (dy,) = gelu_vjp(dz1)
        return dy, _dg(z1, da, _TN), jnp.sum(da, axis=0, keepdims=True)

    return _rowcall(fn, [y, dz, wglu, bglu], [_rb(tm, w5), _rb(tm, w5), _full(wglu.shape), _full((1, w5))],
                    [_sds((t, w5), _F32), _sds((w5, w5), _F32), _sds((1, w5), _F32)],
                    [_rb(tm, w5), _full((w5, w5)), _full((1, w5))], [False, True, True],
                    grid=(1, t // tm), name="s5glu_bwd")


def _tri_mats(c, transposed):
    d0, d1 = (1, 0) if transposed else (0, 1)
    t = lax.broadcasted_iota(jnp.int32, (c, c), d0)
    j = lax.broadcasted_iota(jnp.int32, (c, c), d1)
    low = (j <= t).astype(_F32)
    mats = [low]
    for sh in (6, 5, 4):
        m = 1 << (sh - 1)
        r = ((t >> sh) << sh) + (m - 1)
        mats.append(low - (j <= r).astype(_F32))
    return [mm.astype(_BF) for mm in mats]


def _split_dot(mat, x):
    hi = x.astype(_BF)
    lo = (x - hi.astype(_F32)).astype(_BF)
    return (jnp.dot(mat, hi, preferred_element_type=_F32) + jnp.dot(mat, lo, preferred_element_type=_F32))


@jax.custom_vjp
def _decay_sums(lf):
    return tuple(_split_dot(mm, lf) for mm in _tri_mats(lf.shape[0], False))


def _decay_sums_f(lf):
    return _decay_sums(lf), None


def _decay_sums_b(_, gs):
    mats = _tri_mats(gs[0].shape[0], True)
    out = _split_dot(mats[0], gs[0])
    for mm, gval in zip(mats[1:], gs[1:]):
        out = out + _split_dot(mm, gval)
    return (out,)


_decay_sums.defvjp(_decay_sums_f, _decay_sums_b)


def _hgrn_chunk(qi, fi, vi, gi, st, lb, ng):
    c = qi.shape[0]
    row = lax.broadcasted_iota(jnp.int32, (c, 1), 0)
    q = _silu(qi)
    lf = jnp.log(lb + (1.0 - lb) * _sigmoid(fi))
    k = (1.0 - lb) * _sigmoid(-fi)
    b, p32, p16, p8 = _decay_sums(lf)
    btot = jnp.sum(lf, axis=0, keepdims=True)
    inter = _bdot_nt(q * jnp.exp(b), st)
    tt = lax.broadcasted_iota(jnp.int32, (c, c), 0)
    ss = lax.broadcasted_iota(jnp.int32, (c, c), 1)
    sc = jnp.zeros((c, c), _F32)
    for sh, p in ((6, p32), (5, p16), (4, p8)):
        upper = ((row >> (sh - 1)) & 1) == 1
        qm = jnp.where(upper, q * jnp.exp(jnp.where(upper, p, 0.0)), 0.0)
        km = jnp.where(upper, 0.0, k * jnp.exp(jnp.where(upper, 0.0, -p)))
        sc = sc + jnp.where((tt >> sh) == (ss >> sh), _bdot_nt(qm, km), 0.0)
    o = inter + _bdot(sc, vi)
    o = o + jnp.sum(q * k, axis=1, keepdims=True) * vi
    for d in range(1, 8):
        valid = (row & 7) >= d
        bs = _shift_down(b, d)
        ks = _shift_down(k, d)
        vs = _shift_down(vi, d)
        w = jnp.where(valid, q * jnp.exp(jnp.where(valid, b - bs, 0.0)) * ks, 0.0)
        o = o + jnp.sum(w, axis=1, keepdims=True) * vs
    st_new = st * jnp.exp(btot) + _bdot_tn(vi, k * jnp.exp(btot - b))
    on = o * lax.rsqrt(jnp.mean(o * o, axis=1, keepdims=True) + _RMS_EPS) * ng
    return on * _silu(gi), st_new


def _hgrn_geom(t, w5, hw):
    hp = _HGRN_HP if (hw // _LANE) % _HGRN_HP == 0 and w5 % (_LANE * _HGRN_HP) == 0 else 1
    seg = min(t, _HGRN_SEG)
    return hp, hp * _LANE, seg, t // seg


def _hgrn_in_specs(seg, wd, w5, hw, tmap):
    nhp = hw // wd
    qb = w5 // wd
    return [pl.BlockSpec((seg, wd), (lambda h, s, k=k: (tmap(s), qb + k * nhp + h))) for k in range(4)]


def _hgrn_fwd(proj, lb, ng, t, w5, hw):
    assert _CHUNK == 64
    hp, wd, seg, nseg = _hgrn_geom(t, w5, hw)
    ncs = seg // _CHUNK
    vec = pl.BlockSpec((1, wd), lambda h, s: (0, h))

    def body(q_ref, f_ref, i_ref, g_ref, lb_ref, ng_ref, og_ref, st_ref, s_scr):
        @pl.when(pl.program_id(1) == 0)
        def _():
            s_scr[...] = jnp.zeros_like(s_scr)

        def step(ci, carry):
            r = pl.ds(pl.multiple_of(ci * _CHUNK, _CHUNK), _CHUNK)
            for hh in range(hp):
                ln = slice(hh * _LANE, (hh + 1) * _LANE)
                st_ref[hh, ci] = s_scr[hh]
                og, sn = _hgrn_chunk(q_ref[r, ln], f_ref[r, ln], i_ref[r, ln], g_ref[r, ln], s_scr[hh],
                                     lb_ref[:, ln], ng_ref[:, ln])
                og_ref[r, ln] = og.astype(_BF)
                s_scr[hh] = sn
            return carry

        lax.fori_loop(0, ncs, step, 0)

    return _call(body, name="hgrn_fwd", grid=(hw // wd, nseg),
                 in_specs=_hgrn_in_specs(seg, wd, w5, hw, lambda s: s) + [vec, vec],
                 out_specs=[pl.BlockSpec((seg, wd), lambda h, s: (s, h)),
                            pl.BlockSpec((hp, ncs, _LANE, _LANE), lambda h, s: (h, s, 0, 0))],
                 out_shape=[_sds((t, hw), _BF), _sds((hw // _LANE, t // _CHUNK, _LANE, _LANE), _F32)],
                 scratch_shapes=[pltpu.VMEM((hp, _LANE, _LANE), _F32)],
                 compiler_params=_cparams("parallel", "arbitrary"))(proj, proj, proj, proj, lb, ng)


def _hgrn_bwd(proj, lb, ng, st_all, dog, t, w5, hw):
    hp, wd, seg, nseg = _hgrn_geom(t, w5, hw)
    ncs = seg // _CHUNK

    def rev(s):
        return nseg - 1 - s

    vec = pl.BlockSpec((1, wd), lambda h, s: (0, h))
    col = pl.BlockSpec((seg, wd), lambda h, s: (rev(s), h))

    def body(q_ref, f_ref, i_ref, g_ref, lb_ref, ng_ref, st_ref, dog_ref,
             dq_ref, df_ref, di_ref, dg_ref, dlb_ref, dng_ref, ds_scr):
        @pl.when(pl.program_id(1) == 0)
        def _():
            ds_scr[...] = jnp.zeros_like(ds_scr)
            dlb_ref[...] = jnp.zeros_like(dlb_ref)
            dng_ref[...] = jnp.zeros_like(dng_ref)

        def step(kk, carry):
            ci = ncs - 1 - kk
            r = pl.ds(pl.multiple_of(ci * _CHUNK, _CHUNK), _CHUNK)
            for hh in range(hp):
                ln = slice(hh * _LANE, (hh + 1) * _LANE)
                _, vjp = jax.vjp(_hgrn_chunk, q_ref[r, ln], f_ref[r, ln], i_ref[r, ln], g_ref[r, ln], st_ref[hh, ci],
                                 lb_ref[:, ln], ng_ref[:, ln])
                dq, df, di, dg, ds, dlb, dng = vjp((dog_ref[r, ln], ds_scr[hh]))
                dq_ref[r, ln] = dq.astype(_BF)
                df_ref[r, ln] = df.astype(_BF)
                di_ref[r, ln] = di.astype(_BF)
                dg_ref[r, ln] = dg.astype(_BF)
                ds_scr[hh] = ds
                dlb_ref[:, ln] += dlb
                dng_ref[:, ln] += dng
            return carry

        lax.fori_loop(0, ncs, step, 0)

    return _call(body, name="hgrn_bwd", grid=(hw // wd, nseg),
                 in_specs=_hgrn_in_specs(seg, wd, w5, hw, rev) + [
                     vec, vec, pl.BlockSpec((hp, ncs, _LANE, _LANE), lambda h, s: (h, rev(s), 0, 0)), col],
                 out_specs=[col, col, col, col, vec, vec],
                 out_shape=[_sds((t, hw), _BF)] * 4 + [_sds((1, hw), _F32)] * 2,
                 scratch_shapes=[pltpu.VMEM((hp, _LANE, _LANE), _F32)],
                 compiler_params=_cparams("parallel", "arbitrary"))(proj, proj, proj, proj, lb, ng, st_all, dog)


def _lb_of(logits):
    mx = jnp.max(logits, axis=0, keepdims=True)
    e = jnp.exp(logits - mx)
    sm = e / jnp.sum(e, axis=0, keepdims=True)
    row = lax.broadcasted_iota(jnp.int32, logits.shape, 0)
    return jnp.sum(jnp.where(row == 0, sm, 0.0), axis=0, keepdims=True)


def _lb_prep(logits):
    def body(l_ref, o_ref):
        o_ref[...] = _lb_of(l_ref[...])

    return _call(body, name="lb_prep", out_shape=_sds((1, logits.shape[1]), _F32))(logits)


def _lb_prep_bwd(logits, dlb):
    def body(l_ref, d_ref, o_ref):
        _, vjp = jax.vjp(_lb_of, l_ref[...])
        o_ref[...] = vjp(d_ref[...])[0]

    return _call(body, name="lb_prep_bwd", out_shape=_sds(logits.shape, _F32))(logits, dlb)


def _merge_fwd(proj, ys, yh, t, d, gs_off):
    tm = _tile(t, 256, 8)
    w = _tile(d, 1024)
    nb = d // w

    def fn(gs, gh, a, b):
        return (_sigmoid(gs) * a + _sigmoid(gh) * b,)

    return _rowcall(fn, [proj, proj, ys, yh], [_rb(tm, w, gs_off // w), _rb(tm, w, gs_off // w + nb), _rb(tm, w), _rb(tm, w)],
                    [_sds((t, d), _BF)], [_rb(tm, w)], [False], grid=(nb, t // tm), name="merge_fwd")[0]


def _merge_bwd(proj, ys, yh, dm, t, d, gs_off):
    tm = _tile(t, 256, 8)
    w = _tile(d, 1024)
    nb = d // w

    def fn(gs, gh, a, b, g):
        s1 = _sigmoid(gs)
        s2 = _sigmoid(gh)
        return g * s1, g * s2, g * a * s1 * (1.0 - s1), g * b * s2 * (1.0 - s2)

    return _rowcall(fn, [proj, proj, ys, yh, dm],
                    [_rb(tm, w, gs_off // w), _rb(tm, w, gs_off // w + nb), _rb(tm, w), _rb(tm, w), _rb(tm, w)],
                    [_sds((t, d), _BF)] * 4, [_rb(tm, w)] * 4, [False] * 4, grid=(nb, t // tm), name="merge_bwd")


def _prev_rows(up_prev8, is_first):
    p1 = jnp.where(is_first, 0.0, up_prev8[7:8, :])
    p2 = jnp.where(is_first, 0.0, up_prev8[6:7, :])
    return p1, p2


def _causal_taps(cur, p1, p2):
    row = lax.broadcasted_iota(jnp.int32, cur.shape, 0)
    s1 = jnp.where(row == 0, p1, _shift_down(cur, 1))
    s2 = jnp.where(row == 0, p2, jnp.where(row == 1, p1, _shift_down(cur, 2)))
    return s1, s2


def _conv_specs(tm, w, nb_half, t):
    r8 = tm // 8
    cur_g = pl.BlockSpec((tm, w), lambda j, i: (i, j))
    cur_v = pl.BlockSpec((tm, w), lambda j, i: (i, nb_half + j))
    prev_g = pl.BlockSpec((8, w), lambda j, i: (jnp.maximum(i * r8 - 1, 0), j))
    prev_v = pl.BlockSpec((8, w), lambda j, i: (jnp.maximum(i * r8 - 1, 0), nb_half + j))
    w_g = pl.BlockSpec((3, w), lambda j, i: (0, j))
    w_v = pl.BlockSpec((3, w), lambda j, i: (0, nb_half + j))
    b_g = pl.BlockSpec((1, w), lambda j, i: (0, j))
    b_v = pl.BlockSpec((1, w), lambda j, i: (0, nb_half + j))
    return cur_g, cur_v, prev_g, prev_v, w_g, w_v, b_g, b_v


def _conv_of(cur, prev8, wt, bias, is_first):
    p1, p2 = _prev_rows(prev8, is_first)
    s1, s2 = _causal_taps(cur, p1, p2)
    return bias + wt[0:1, :] * s2 + wt[1:2, :] * s1 + wt[2:3, :] * cur


def _convact_fwd(up, cw, cb, t, dff):
    tm = _tile(t, 512, 8)
    w = _tile(dff, 1408)
    nbh = dff // w
    sp = _conv_specs(tm, w, nbh, t)

    def body(ug, uv, pg, pv, wg, wv, bg, bv, o_ref):
        first = pl.program_id(1) == 0
        gate = _conv_of(ug[...], pg[...], wg[...], bg[...], first)
        val = _conv_of(uv[...], pv[...], wv[...], bv[...], first)
        o_ref[...] = (_silu(gate) * val).astype(_BF)

    return _call(body, name="convact_fwd", grid=(nbh, t // tm), in_specs=list(sp),
                 out_specs=pl.BlockSpec((tm, w), lambda j, i: (i, j)), out_shape=_sds((t, dff), _BF),
                 compiler_params=_cparams("parallel", "arbitrary"))(up, up, up, up, cw, cw, cb, cb)


def _convact_bwd(up, dact, cw, cb, t, dff):
    tm = _tile(t, 256, 8)
    w = _tile(dff, 1408)
    nbh = dff // w
    r8 = tm // 8
    nt = t // tm
    last8 = t // 8 - 1

    def own(j):
        return j

    def partner(j):
        return (j + nbh) % (2 * nbh)

    def triple(col):
        return [pl.BlockSpec((tm, w), lambda j, i: (i, col(j))),
                pl.BlockSpec((8, w), lambda j, i: (jnp.minimum((i + 1) * r8, last8), col(j))),
                pl.BlockSpec((8, w), lambda j, i: (jnp.maximum(i * r8 - 1, 0), col(j)))]

    def body(uo, uon, uop, upp, upn, upv, wo_ref, wp_ref, bo_ref, bp_ref, da_ref, dan_ref, du_ref, dw_ref, db_ref):
        j = pl.program_id(0)
        i = pl.program_id(1)
        is_gate = j < nbh
        first = i == 0
        is_last = i == nt - 1
        cur = uo[...]
        conv_o = _conv_of(jnp.concatenate([cur, uon[...]], axis=0), uop[...], wo_ref[...], bo_ref[...], first)
        conv_p = _conv_of(jnp.concatenate([upp[...], upn[...]], axis=0), upv[...], wp_ref[...], bp_ref[...], first)
        da = jnp.concatenate([da_ref[...], dan_ref[...]], axis=0)
        gate = jnp.where(is_gate, conv_o, conv_p)
        val = jnp.where(is_gate, conv_p, conv_o)
        sg = _sigmoid(gate)
        dc = jnp.where(is_gate, da * val * sg * (1.0 + gate * (1.0 - sg)), da * gate * sg)
        row = lax.broadcasted_iota(jnp.int32, dc.shape, 0)
        dc = jnp.where(jnp.logical_and(row >= tm, is_last), 0.0, dc)
        wt = wo_ref[...]
        du = wt[2:3, :] * dc + wt[1:2, :] * _shift_up(dc, 1) + wt[0:1, :] * _shift_up(dc, 2)
        du_ref[...] = du[0:tm, :].astype(_BF)
        dcm = dc[0:tm, :]
        p1, p2 = _prev_rows(uop[...], first)
        s1, s2 = _causal_taps(cur, p1, p2)

        @pl.when(first)
        def _():
            dw_ref[...] = jnp.zeros_like(dw_ref)
            db_ref[...] = jnp.zeros_like(db_ref)

        dw_ref[0:1, :] += jnp.sum(dcm * s2, axis=0, keepdims=True)
        dw_ref[1:2, :] += jnp.sum(dcm * s1, axis=0, keepdims=True)
        dw_ref[2:3, :] += jnp.sum(dcm * cur, axis=0, keepdims=True)
        db_ref[...] += jnp.sum(dcm, axis=0, keepdims=True)

    def dcol(j):
        return j % nbh

    in_specs = (triple(own) + triple(partner)
                + [pl.BlockSpec((3, w), lambda j, i: (0, own(j))), pl.BlockSpec((3, w), lambda j, i: (0, partner(j))),
                   pl.BlockSpec((1, w), lambda j, i: (0, own(j))), pl.BlockSpec((1, w), lambda j, i: (0, partner(j))),
                   pl.BlockSpec((tm, w), lambda j, i: (i, dcol(j))),
                   pl.BlockSpec((8, w), lambda j, i: (jnp.minimum((i + 1) * r8, last8), dcol(j)))])
    return _call(body, name="convact_bwd", grid=(2 * nbh, nt), in_specs=in_specs,
                 out_specs=[pl.BlockSpec((tm, w), lambda j, i: (i, j)), pl.BlockSpec((3, w), lambda j, i: (0, j)),
                            pl.BlockSpec((1, w), lambda j, i: (0, j))],
                 out_shape=[_sds((t, 2 * dff), _BF), _sds((3, 2 * dff), _F32), _sds((1, 2 * dff), _F32)],
                 compiler_params=_cparams("parallel", "arbitrary"))(
                     up, up, up, up, up, up, cw, cw, cb, cb, dact, dact)


def _me():
    return lax.axis_index("x"), lax.axis_index("y"), lax.axis_index("c")


def _other_chips(x, y):
    return [(1 - x, y), (x, 1 - y), (1 - x, 1 - y)]


def _rcopy(src, dst, ssem, rsem, dev):
    return pltpu.make_async_remote_copy(src_ref=src, dst_ref=dst, send_sem=ssem, recv_sem=rsem,
                                        device_id=dev, device_id_type=_MESH)


def _cast_into_slot(w, sel):
    r, c = w.shape
    tm = _tile(r, 256, 16)

    def body(sel_ref, w_ref, o_ref):
        o_ref[...] = w_ref[...].astype(_BF)

    gs = pltpu.PrefetchScalarGridSpec(
        num_scalar_prefetch=1, grid=(r // tm,),
        in_specs=[pl.BlockSpec((tm, c), lambda i, s: (i, 0))],
        out_specs=pl.BlockSpec((None, tm, c), lambda i, s: (s[0], i, 0)))
    return _call(body, name="cast_into_slot", grid_spec=gs, out_shape=_sds((_NCHIP, r, c), _BF),
                 compiler_params=_cparams("parallel"))(sel, w)


class _Plan:
    def __init__(self, ins, outs, aliases, sems, start, finish):
        self.ins, self.outs, self.aliases, self.sems, self.start, self.finish = ins, outs, aliases, sems, start, finish


def _run_plan(plan, name):
    ni, no = len(plan.ins), len(plan.outs)

    def body(*refs):
        rin, rout, sems = refs[:ni], refs[ni:ni + no], refs[ni + no:]
        plan.start(rin, rout, sems)
        plan.finish(rin, rout, sems)

    return _call(body, name=name, in_specs=[_ANY] * ni, out_specs=[_ANY] * no, out_shape=list(plan.outs),
                 input_output_aliases=dict(plan.aliases), scratch_shapes=list(plan.sems))(*plan.ins)


def _call_riding(body, rider, *, name, grid, in_specs, out_specs, out_shape, scratch_shapes, args):
    n_in, n_out, n_scr = len(in_specs), len(out_specs), len(scratch_shapes)
    n_rin, n_rout = len(rider.ins), len(rider.outs)

    def wrapped(*refs):
        ins, rin = refs[:n_in], refs[n_in:n_in + n_rin]
        o0 = n_in + n_rin
        outs, rout = refs[o0:o0 + n_out], refs[o0 + n_out:o0 + n_out + n_rout]
        s0 = o0 + n_out + n_rout
        scratch, sems = refs[s0:s0 + n_scr], refs[s0 + n_scr:]
        first = functools.reduce(jnp.logical_and, [pl.program_id(k) == 0 for k in range(len(grid))])
        last = functools.reduce(jnp.logical_and, [pl.program_id(k) == grid[k] - 1 for k in range(len(grid))])

        @pl.when(first)
        def _():
            rider.start(rin, rout, sems)

        body(*ins, *outs, *scratch)

        @pl.when(last)
        def _():
            rider.finish(rin, rout, sems)

    res = _call(wrapped, name=name, grid=grid, in_specs=list(in_specs) + [_ANY] * n_rin,
                out_specs=list(out_specs) + [_ANY] * n_rout, out_shape=list(out_shape) + list(rider.outs),
                input_output_aliases={n_in + k: n_out + v for k, v in rider.aliases.items()},
                scratch_shapes=list(scratch_shapes) + list(rider.sems),
                compiler_params=_cparams(*(["arbitrary"] * len(grid))))(*args, *rider.ins)
    return list(res[:n_out]), list(res[n_out:])


def _gather_plan(bufs, direct):
    n, nd = len(bufs), len(direct)

    def where():
        x, y, c = _me()
        return c, 2 * x + y, _other_chips(x, y), (x, y, 1 - c)

    def piece(outs, a, chip, h):
        r2 = bufs[a].shape[1] // 2
        return outs[a].at[chip, pl.ds(h * r2, r2)]

    def send(outs, sems, a, j, me, c, chip):
        return _rcopy(piece(outs, a, me, c), piece(outs, a, me, c), sems[0].at[3 * a + j], sems[1].at[3 * a + j],
                      (chip[0], chip[1], c))

    def forward(outs, sems, a, j, pc, c, sib):
        return _rcopy(piece(outs, a, pc, c), piece(outs, a, pc, c), sems[2].at[3 * a + j], sems[3].at[3 * a + j], sib)

    def dsend(dins, douts, sems, a, j, me, c, chip):
        return _rcopy(dins[a], douts[a].at[me], sems[4].at[3 * a + j], sems[5].at[3 * a + j], (chip[0], chip[1], c))

    def start(rin, rout, sems):
        outs, dins, douts = rout[:n], rin[n:], rout[n:]
        c, me, chips, _ = where()
        for a in range(n):
            for j, chip in enumerate(chips):
                send(outs, sems, a, j, me, c, chip).start()
        for a in range(nd):
            pltpu.make_async_copy(dins[a], douts[a].at[me], sems[6].at[a]).start()
            for j, chip in enumerate(chips):
                dsend(dins, douts, sems, a, j, me, c, chip).start()

    def finish(rin, rout, sems):
        outs, dins, douts = rout[:n], rin[n:], rout[n:]
        c, me, chips, sib = where()
        for a in range(n):
            for j, (cx, cy) in enumerate(chips):
                pc = 2 * cx + cy
                _rcopy(piece(outs, a, me, c), piece(outs, a, pc, c), sems[0].at[3 * a + j], sems[1].at[3 * a + j],
                       (cx, cy, c)).wait_recv()
                forward(outs, sems, a, j, pc, c, sib).start()
        for a in range(n):
            for j, (cx, cy) in enumerate(chips):
                pc = 2 * cx + cy
                _rcopy(piece(outs, a, pc, 1 - c), piece(outs, a, pc, 1 - c), sems[2].at[3 * a + j],
                       sems[3].at[3 * a + j], sib).wait_recv()
        for a in range(nd):
            for j, (cx, cy) in enumerate(chips):
                _rcopy(dins[a], douts[a].at[2 * cx + cy], sems[4].at[3 * a + j], sems[5].at[3 * a + j],
                       (cx, cy, c)).wait_recv()
        for a in range(n):
            for j, (cx, cy) in enumerate(chips):
                send(outs, sems, a, j, me, c, (cx, cy)).wait_send()
                forward(outs, sems, a, j, 2 * cx + cy, c, sib).wait_send()
        for a in range(nd):
            pltpu.make_async_copy(dins[a], douts[a].at[me], sems[6].at[a]).wait()
            for j, chip in enumerate(chips):
                dsend(dins, douts, sems, a, j, me, c, chip).wait_send()

    dma = pltpu.SemaphoreType.DMA
    return _Plan(list(bufs) + list(direct),
                 [_sds(b.shape, b.dtype) for b in bufs] + [_sds((_NCHIP,) + s.shape, s.dtype) for s in direct],
                 {a: a for a in range(n)},
                 [dma((3 * max(n, 1),)), dma((3 * max(n, 1),)), dma((3 * max(n, 1),)), dma((3 * max(n, 1),)),
                  dma((3 * max(nd, 1),)), dma((3 * max(nd, 1),)), dma((max(nd, 1),))], start, finish)


def _swap_halves(grads, name):
    n = len(grads)

    def body(*refs):
        ins, outs = refs[:n], refs[n:2 * n]
        ssem, rsem = refs[2 * n:]
        x, y, c = _me()
        sib = (x, y, 1 - c)
        cps = []
        for a in range(n):
            cp = _rcopy(ins[a].at[1 - c], outs[a], ssem.at[a], rsem.at[a], sib)
            cp.start()
            cps.append(cp)
        for cp in cps:
            cp.wait_recv()
        for cp in cps:
            cp.wait_send()

    dma = pltpu.SemaphoreType.DMA
    return _call(body, name=name, in_specs=[_ANY] * n, out_specs=[_ANY] * n,
                 out_shape=[_sds(g.shape[1:], g.dtype) for g in grads],
                 scratch_shapes=[dma((n,)), dma((n,))])(*grads)


def _add_pairs(mine, theirs):
    s, r, c2 = mine.shape
    a2 = mine.reshape(s * r, c2)
    b2 = theirs.reshape(s * r, c2)
    tm = _tile(s * r, 512, 16)
    out = _rowcall(lambda p, q: (p.astype(_F32) + q.astype(_F32),), [a2, b2], [_rb(tm, c2), _rb(tm, c2)],
                   [_sds((s * r, c2), _BF)], [_rb(tm, c2)], [False], grid=(1, s * r // tm), name="chip_sum")[0]
    return out.reshape(s, r, c2)


def _exchange_plan(sums, small):
    n = len(sums)
    has_small = small is not None

    def where():
        x, y, c = _me()
        peers = [(1 - x if k & 4 else x, 1 - y if k & 2 else y, 1 - c if k & 1 else c) for k in range(1, _NDEV)]
        return c, 4 * x + 2 * y + c, _other_chips(x, y), peers

    def send(rin, rout, sems, a, j, c, chip):
        return _rcopy(rin[a].at[2 * chip[0] + chip[1]], rout[a].at[j], sems[0].at[3 * a + j], sems[1].at[3 * a + j],
                      (chip[0], chip[1], c))

    def small_send(rin, rout, sems, k, dev, peer):
        return _rcopy(rin[n], rout[n].at[dev], sems[2].at[k], sems[3].at[k], peer)

    def start(rin, rout, sems):
        c, dev, chips, peers = where()
        for a in range(n):
            for j, chip in enumerate(chips):
                send(rin, rout, sems, a, j, c, chip).start()
        if has_small:
            pltpu.make_async_copy(rin[n], rout[n].at[dev], sems[4].at[0]).start()
            for k, peer in enumerate(peers):
                small_send(rin, rout, sems, k, dev, peer).start()

    def finish(rin, rout, sems):
        c, dev, chips, peers = where()
        for a in range(n):
            for j, chip in enumerate(chips):
                send(rin, rout, sems, a, j, c, chip).wait_recv()
        if has_small:
            for k, (px, py, pc_) in enumerate(peers):
                _rcopy(rin[n], rout[n].at[4 * px + 2 * py + pc_], sems[2].at[k], sems[3].at[k], (px, py, pc_)).wait_recv()
        for a in range(n):
            for j, chip in enumerate(chips):
                send(rin, rout, sems, a, j, c, chip).wait_send()
        if has_small:
            pltpu.make_async_copy(rin[n], rout[n].at[dev], sems[4].at[0]).wait()
            for k, peer in enumerate(peers):
                small_send(rin, rout, sems, k, dev, peer).wait_send()

    dma = pltpu.SemaphoreType.DMA
    outs = [_sds((3,) + s.shape[1:], s.dtype) for s in sums]
    if has_small:
        outs.append(_sds((_NDEV,) + small.shape, small.dtype))
    return _Plan(list(sums) + ([small] if has_small else []), outs, {},
                 [dma((3 * max(n, 1),)), dma((3 * max(n, 1),)), dma((_NDEV - 1,)), dma((_NDEV - 1,)), dma((1,))],
                 start, finish)


def _shard_sum(sums, recv, sel):
    s, r, c2 = sums.shape
    tm = _tile(r, 256, 16)

    def body(sel_ref, own_ref, rc_ref, o_ref):
        rc = rc_ref[...]
        o_ref[...] = ((own_ref[...].astype(_F32) + rc[0].astype(_F32)) + rc[1].astype(_F32)) + rc[2].astype(_F32)

    gs = pltpu.PrefetchScalarGridSpec(
        num_scalar_prefetch=1, grid=(r // tm,),
        in_specs=[pl.BlockSpec((None, tm, c2), lambda i, q: (q[0], i, 0)),
                  pl.BlockSpec((3, tm, c2), lambda i, q: (0, i, 0))],
        out_specs=pl.BlockSpec((None, tm, c2), lambda i, q: (q[1], i, 0)))
    return _call(body, name="shard_sum", grid_spec=gs, out_shape=_sds((2, r, c2), _F32),
                 compiler_params=_cparams("parallel"))(sel, sums, recv)


def _sum_slots(stack, name):
    k, r, c = stack.shape
    tm = _tile(r, 256, 16 if stack.dtype == _BF else 8)

    def fn(v):
        out = v[0].astype(_F32)
        for i in range(1, k):
            out = out + v[i].astype(_F32)
        return (out,)

    return _rowcall(fn, [stack], [pl.BlockSpec((k, tm, c), lambda j, i: (0, i, 0))], [_sds((r, c), _F32)],
                    [_rb(tm, c)], [False], grid=(1, r // tm), name=name)[0]


def _share_halves(bufs):
    n = len(bufs)

    def body(*refs):
        outs = refs[n:2 * n]
        ssem, rsem = refs[2 * n:]
        x, y, c = _me()
        sib = (x, y, 1 - c)
        cps = []
        for a in range(n):
            cp = _rcopy(outs[a].at[c], outs[a].at[c], ssem.at[a], rsem.at[a], sib)
            cp.start()
            cps.append(cp)
        for a in range(n):
            _rcopy(outs[a].at[c], outs[a].at[1 - c], ssem.at[a], rsem.at[a], sib).wait_recv()
        for cp in cps:
            cp.wait_send()

    dma = pltpu.SemaphoreType.DMA
    return _call(body, name="share_halves", in_specs=[_ANY] * n, out_specs=[_ANY] * n,
                 out_shape=[_sds(b.shape, b.dtype) for b in bufs], input_output_aliases={a: a for a in range(n)},
                 scratch_shapes=[dma((n,)), dma((n,))])(*bufs)


def _adamw_math(w, g, m, v):
    m = _B1 * m + (1.0 - _B1) * g
    v = _B2 * v + (1.0 - _B2) * jnp.square(g)
    m_hat = m / (1.0 - _B1 ** _STEP)
    v_hat = v / (1.0 - _B2 ** _STEP)
    delta = -_LR * (m_hat / (jnp.sqrt(v_hat) + _ADAM_EPS) + _WD * w)
    return delta, m, v


def _adamw_shard(w, g2, m, v, name):
    r, c = w.shape
    c2 = c // 2
    tm = _tile(r, 256, 8)
    blk = pl.BlockSpec((tm, c2), lambda h, i: (i, h))

    def fn(wb, gb, mb, vb):
        return (gb,) + _adamw_math(wb, gb, mb, vb)

    return _rowcall(fn, [w, g2, m, v], [blk, pl.BlockSpec((None, tm, c2), lambda h, i: (h, i, 0)), blk, blk],
                    [_sds((r, c), _F32)] * 4, [blk] * 4, [False] * 4, grid=(2, r // tm), name=name)


def _adamw_whole(w, g, m, v, name):
    r, c = w.shape
    blk = _full((r, c))
    return _rowcall(lambda *a: _adamw_math(*a), [w, g, m, v], [blk] * 4, [_sds((r, c), _F32)] * 3, [blk] * 3,
                    [False] * 3, grid=(1, 1), name=name)


def _pack(arrs):
    parts = []
    for a in arrs:
        f = a.reshape(-1).astype(_F32)
        pad = (-f.shape[0]) % (8 * _LANE)
        if pad:
            f = jnp.concatenate([f, jnp.zeros((pad,), _F32)])
        parts.append(f)
    return jnp.concatenate(parts).reshape(-1, _LANE)


def _unpack(buf, like):
    flat = buf.reshape(-1)
    outs, off = [], 0
    for a in like:
        nel = a.size
        outs.append(flat[off:off + nel].reshape(a.shape))
        off += nel + ((-nel) % (8 * _LANE))
    return outs


def kernel(x, ln_mix_g, w_in, s5_a_re, s5_a_im, s5_log_dt, s5_b_re, s5_b_im, s5_c_re, s5_c_im, s5_d, s5_w_glu, s5_b_glu, w_proj_s5, hgrn_lb_logits, hgrn_norm_g, w_proj_hgrn, w_out, ln_ffn_g, w_up, conv_w, conv_b, w_down, ln_final_g, loss_target, m_ln_mix_g, m_w_in, m_s5_a_re, m_s5_a_im, m_s5_log_dt, m_s5_b_re, m_s5_b_im, m_s5_c_re, m_s5_c_im, m_s5_d, m_s5_w_glu, m_s5_b_glu, m_w_proj_s5, m_hgrn_lb_logits, m_hgrn_norm_g, m_w_proj_hgrn, m_w_out, m_ln_ffn_g, m_w_up, m_conv_w, m_conv_b, m_w_down, m_ln_final_g, v_ln_mix_g, v_w_in, v_s5_a_re, v_s5_a_im, v_s5_log_dt, v_s5_b_re, v_s5_b_im, v_s5_c_re, v_s5_c_im, v_s5_d, v_s5_w_glu, v_s5_b_glu, v_w_proj_s5, v_hgrn_lb_logits, v_hgrn_norm_g, v_w_proj_hgrn, v_w_out, v_ln_ffn_g, v_w_up, v_conv_w, v_conv_b, v_w_down, v_ln_final_g):
    assert x.shape[0] == 1 and w_in.shape[0] == 1, "one example per device, one layer"
    t, d = x.shape[1], x.shape[2]
    w5 = s5_w_glu.shape[2]
    hw = hgrn_norm_g.shape[1]
    ng_, np_, gc = s5_b_re.shape[1], s5_b_re.shape[2], s5_b_re.shape[3]
    dff = w_down.shape[1] * _NCHIP
    assert gc * _S5_SET == _LANE and ng_ * gc == w5 and hw % _LANE == 0
    gs_off = w5 + 4 * hw
    ci = lax.axis_index("c")
    xt = x.reshape(t, d)
    tgt = loss_target.reshape(t, d)

    big_names = ["w_in", "s5_w_glu", "w_proj_s5", "w_proj_hgrn", "w_out", "w_up", "w_down"]
    big_w = dict(w_in=w_in[0], s5_w_glu=s5_w_glu[0], w_proj_s5=w_proj_s5[0], w_proj_hgrn=w_proj_hgrn[0],
                 w_out=w_out[0], w_up=w_up[0], w_down=w_down[0])
    chip = 2 * lax.axis_index("x") + lax.axis_index("y")
    sel_chip = jnp.stack([chip, ci]).astype(jnp.int32)
    slots = {k: _cast_into_slot(big_w[k], sel_chip) for k in big_names}
    g_in, g_cw = _run_plan(_gather_plan([slots["w_in"]], [conv_w[0]]), "gather_w_in")
    cw = g_cw.transpose(1, 0, 2).reshape(3, 2 * dff)
    cb = conv_b

    tm_big = _tile(t, 1024, 8)

    h1 = _rms_fwd(xt, ln_mix_g, "rms1_fwd")
    nin_s = g_in.shape[2]
    proj, (g_glu, g_ps5, g_ph, g_out) = _mm(
        h1, g_in, "nn", _F32, tm=tm_big, tn=_tile(nin_s, 1152), tk=d, name="mm_proj",
        rider=_gather_plan([slots[k] for k in ("s5_w_glu", "w_proj_s5", "w_proj_hgrn", "w_out")], []))
    wglu = g_glu.reshape(w5, w5)
    wout = g_out.reshape(d, d)

    abar_r, abar_i, coef_r, coef_i = _s5_prep(s5_a_re[0], s5_a_im[0], s5_log_dt.reshape(ng_, 1))
    lanes = ng_ * np_
    par = (abar_r.reshape(1, lanes), abar_i.reshape(1, lanes), coef_r.reshape(1, lanes), coef_i.reshape(1, lanes),
           _bd_in(s5_b_re[0]), _bd_in(s5_b_im[0]), _bd_out(s5_c_re[0]), -_bd_out(s5_c_im[0]), s5_d.reshape(1, w5))
    (y_s5, car, cai), (g_up,) = _s5_fwd(proj, par, t, w5, rider=_gather_plan([slots["w_up"]], []))
    z = _s5glu_fwd(y_s5, wglu, s5_b_glu)
    ys = _mm(z, g_ps5, "nn", _F32, tm=tm_big, tn=g_ps5.shape[2], tk=w5, name="mm_proj_s5")

    lb = _lb_prep(hgrn_lb_logits)
    og, st_all = _hgrn_fwd(proj, lb, hgrn_norm_g, t, w5, hw)
    yh = _mm(og, g_ph, "nn", _F32, tm=tm_big, tn=g_ph.shape[2], tk=hw, name="mm_proj_hgrn")

    merged = _merge_fwd(proj, ys, yh, t, d, gs_off)
    x2 = _mm(merged, wout, "nn", _F32, tm=tm_big, tn=_tile(d, 1024), tk=d, res=xt, name="mm_out")
    h2 = _rms_fwd(x2, ln_ffn_g, "rms2_fwd")
    up_s = g_up.shape[2]
    up, (g_down,) = _mm(h2, g_up, "nn", _F32, tm=tm_big, tn=_tile(up_s, 1408), tk=d, name="mm_up",
                        rider=_gather_plan([slots["w_down"]], []))
    wdown = g_down.reshape(dff, d)
    act = _convact_fwd(up, cw, cb, t, dff)
    x3 = _mm(act, wdown, "nn", _F32, tm=tm_big, tn=_tile(d, 1024), tk=_tile(dff, 1408), res=x2, name="mm_down")
    loss_part, dx3, dx3b, d_gfin = _loss_head(x3, tgt, ln_final_g.reshape(1, d))

    tk_t = _tile(t, 512, 8)
    dact = _mm(dx3b, wdown, "nt", _F32, tm=tm_big, tn=_tile(dff, 1408), tk=d, name="mm_dact")
    r_down = dff // _NCHIP
    gw_down = _mm(act, dx3b, "tn", _BF, tm=_tile(r_down, 1408), tn=_tile(d // 2, 1024), tk=tk_t, halves="rows",
                  name="mm_gw_down")
    def chip_sums(grads, name):
        theirs = _swap_halves(grads, name)
        return [_add_pairs(lax.dynamic_index_in_dim(g, ci, 0, keepdims=False), th) for g, th in zip(grads, theirs)]

    (s_down,) = chip_sums([gw_down], "swap_halves_d")
    dup, d_cw, d_cb = _convact_bwd(up, dact, cw, cb, t, dff)
    dh2, (r_down,) = _mm(dup, g_up, "nt", _F32, tm=tm_big, tn=_tile(d, 1024), tk=_tile(up_s, 1408), name="mm_dh2",
                         rider=_exchange_plan([s_down], None))
    gw_up = _mm(h2, dup, "tn", _BF, tm=_tile(d, 1024), tn=_tile(up_s // 2, 1408), tk=tk_t, halves="cols", name="mm_gw_up")
    dx2, dx2b, d_gffn = _rms_bwd(x2, ln_ffn_g, dh2, dx3, "rms2_bwd")
    dmerged = _mm(dx2b, wout, "nt", _F32, tm=tm_big, tn=_tile(d, 1024), tk=d, name="mm_dmerged")
    gw_out = _mm(merged, dx2b, "tn", _BF, tm=_tile(d // _NCHIP, 1024), tn=_tile(d // 2, 1024), tk=tk_t, halves="rows",
                 name="mm_gw_out")
    dys, dyh, dgs, dgh = _merge_bwd(proj, ys, yh, dmerged, t, d, gs_off)
    ps_s = g_ps5.shape[2]
    dz = _mm(dys, g_ps5, "nt", _F32, tm=tm_big, tn=_tile(w5, 1024), tk=ps_s, name="mm_dz")
    gw_ps5 = _mm(z, dys, "tn", _BF, tm=_tile(w5, 1024), tn=ps_s // 2, tk=tk_t, halves="cols", name="mm_gw_ps5")
    dog = _mm(dyh, g_ph, "nt", _F32, tm=tm_big, tn=_tile(hw, 1024), tk=ps_s, name="mm_dog")
    gw_ph = _mm(og, dyh, "tn", _BF, tm=_tile(hw, 1024), tn=ps_s // 2, tk=tk_t, halves="cols", name="mm_gw_ph")
    dy_s5, gw_glu_full, d_bglu = _s5glu_bwd(y_s5, dz, wglu, s5_b_glu)
    r_glu = w5 // _NCHIP
    gw_glu = gw_glu_full.astype(_BF).reshape(_NCHIP, r_glu, 2, w5 // 2).transpose(2, 0, 1, 3)

    s_glu, s_ps5, s_ph, s_out, s_up = chip_sums([gw_glu, gw_ps5, gw_ph, gw_out, gw_up], "swap_halves_a")
    s5g, (r_glu_, r_ps5, r_ph, r_out, r_up) = _s5_bwd(
        proj, dy_s5, car, cai, par, t, w5, rider=_exchange_plan([s_glu, s_ps5, s_ph, s_out, s_up], None))
    du = s5g[0]
    dq, df, di, dg, d_lb, d_ng = _hgrn_bwd(proj, lb, hgrn_norm_g, st_all, dog, t, w5, hw)
    dproj = jnp.concatenate([du, dq, df, di, dg, dgs, dgh], axis=1)

    d_are, d_aim, d_ldt = _s5_prep_bwd(s5_a_re[0], s5_a_im[0], s5_log_dt.reshape(ng_, 1),
                                       [s5g[k].reshape(ng_, np_) for k in (1, 2, 3, 4)])
    d_bre = _bd_in_grad(s5g[5], np_, gc)
    d_bim = _bd_in_grad(s5g[6], np_, gc)
    d_cre = _bd_out_grad(s5g[7], np_, gc)
    d_cim = -_bd_out_grad(s5g[8], np_, gc)
    d_logits = _lb_prep_bwd(hgrn_lb_logits, d_lb)

    small_names = ["s5_a_re", "s5_a_im", "s5_log_dt", "s5_b_re", "s5_b_im", "s5_c_re", "s5_c_im", "s5_d",
                   "s5_b_glu", "hgrn_lb_logits", "hgrn_norm_g", "ln_ffn_g", "conv_b", "ln_final_g", "ln_mix_g"]
    small_w = dict(ln_mix_g=ln_mix_g, s5_a_re=s5_a_re, s5_a_im=s5_a_im, s5_log_dt=s5_log_dt, s5_b_re=s5_b_re,
                   s5_b_im=s5_b_im, s5_c_re=s5_c_re, s5_c_im=s5_c_im, s5_d=s5_d, s5_b_glu=s5_b_glu,
                   hgrn_lb_logits=hgrn_lb_logits, hgrn_norm_g=hgrn_norm_g, ln_ffn_g=ln_ffn_g, conv_b=conv_b,
                   ln_final_g=ln_final_g)
    small_m = dict(ln_mix_g=m_ln_mix_g, s5_a_re=m_s5_a_re, s5_a_im=m_s5_a_im, s5_log_dt=m_s5_log_dt, s5_b_re=m_s5_b_re,
                   s5_b_im=m_s5_b_im, s5_c_re=m_s5_c_re, s5_c_im=m_s5_c_im, s5_d=m_s5_d, s5_b_glu=m_s5_b_glu,
                   hgrn_lb_logits=m_hgrn_lb_logits, hgrn_norm_g=m_hgrn_norm_g, ln_ffn_g=m_ln_ffn_g, conv_b=m_conv_b,
                   ln_final_g=m_ln_final_g)
    small_v = dict(ln_mix_g=v_ln_mix_g, s5_a_re=v_s5_a_re, s5_a_im=v_s5_a_im, s5_log_dt=v_s5_log_dt, s5_b_re=v_s5_b_re,
                   s5_b_im=v_s5_b_im, s5_c_re=v_s5_c_re, s5_c_im=v_s5_c_im, s5_d=v_s5_d, s5_b_glu=v_s5_b_glu,
                   hgrn_lb_logits=v_hgrn_lb_logits, hgrn_norm_g=v_hgrn_norm_g, ln_ffn_g=v_ln_ffn_g, conv_b=v_conv_b,
                   ln_final_g=v_ln_final_g)
    small_g = dict(s5_a_re=d_are, s5_a_im=d_aim, s5_log_dt=d_ldt, s5_b_re=d_bre, s5_b_im=d_bim,
                   s5_c_re=d_cre, s5_c_im=d_cim, s5_d=s5g[9], s5_b_glu=d_bglu, hgrn_lb_logits=d_logits,
                   hgrn_norm_g=d_ng, ln_ffn_g=d_gffn, conv_b=d_cb, ln_final_g=d_gfin)
    like = [small_w[k] for k in small_names]
    assert small_names[-1] == "ln_mix_g"
    pack_a = _pack([small_g[k] for k in small_names[:-1]] + [d_cw])
    gw_in, (r_small_a,) = _mm(h1, dproj, "tn", _BF, tm=_tile(d, 1024), tn=_tile(nin_s // 2, 1152), tk=tk_t, halves="cols",
                              name="mm_gw_in", rider=_exchange_plan([], pack_a))
    (s_in,) = chip_sums([gw_in], "swap_halves_b")
    dh1, (r_in,) = _mm(dproj, g_in, "nt", _F32, tm=tm_big, tn=_tile(d, 1024), tk=_tile(nin_s, 1152), name="mm_dh1",
                       rider=_exchange_plan([s_in], None))
    dx, _, d_gmix = _rms_bwd(xt, ln_mix_g, dh1, dx2, "rms1_bwd")
    (r_small_b,) = _run_plan(_exchange_plan([], _pack([d_gmix])), "exchange_gmix")
    sums = [s_in, s_glu, s_ps5, s_ph, s_out, s_up, s_down]
    received = [r_in, r_glu_, r_ps5, r_ph, r_out, r_up, r_down]
    halves = [_shard_sum(sm, rc, sel_chip) for sm, rc in zip(sums, received)]
    g_a = _sum_slots(r_small_a, "small_sum_a")
    g_b = _sum_slots(r_small_b, "small_sum_b")
    full = _share_halves(halves)
    w_pack = _pack(like)
    rows_a = w_pack.shape[0] - g_b.shape[0]
    g_small = jnp.concatenate([g_a[:rows_a], g_b], axis=0)
    cs = conv_w.shape[2]
    g_cw_full = g_a[rows_a:].reshape(-1)[:3 * 2 * dff].reshape(3, 2 * dff)
    g_cw = lax.dynamic_slice_in_dim(g_cw_full, chip * cs, cs, axis=1)

    big_m = dict(w_in=m_w_in, s5_w_glu=m_s5_w_glu, w_proj_s5=m_w_proj_s5, w_proj_hgrn=m_w_proj_hgrn, w_out=m_w_out,
                 w_up=m_w_up, w_down=m_w_down)
    big_v = dict(w_in=v_w_in, s5_w_glu=v_s5_w_glu, w_proj_s5=v_w_proj_s5, w_proj_hgrn=v_w_proj_hgrn, w_out=v_w_out,
                 w_up=v_w_up, w_down=v_w_down)
    res = {}
    for k, g2 in zip(big_names, full):
        w2 = big_w[k]
        shp = (1,) + w2.shape
        outs = _adamw_shard(w2, g2, big_m[k][0], big_v[k][0], "adamw_" + k)
        res[k] = [o.reshape(shp) for o in outs]
    sm_outs = _adamw_whole(w_pack, g_small, _pack([small_m[k] for k in small_names]),
                           _pack([small_v[k] for k in small_names]), "adamw_small")
    sm_g = _unpack(g_small, like)
    sm_d, sm_m, sm_v = (_unpack(o, like) for o in sm_outs)
    for i, k in enumerate(small_names):
        res[k] = [sm_g[i], sm_d[i], sm_m[i], sm_v[i]]

    cw_outs = _adamw_whole(conv_w[0], g_cw, m_conv_w[0], v_conv_w[0], "adamw_conv_w")
    res["conv_w"] = [g_cw.reshape(conv_w.shape)] + [o.reshape(conv_w.shape) for o in cw_outs]

    loss = lax.psum(loss_part[0, 0], ("x", "y", "c"))
    order = ["ln_mix_g", "w_in", "s5_a_re", "s5_a_im", "s5_log_dt", "s5_b_re", "s5_b_im", "s5_c_re", "s5_c_im", "s5_d",
             "s5_w_glu", "s5_b_glu", "w_proj_s5", "hgrn_lb_logits", "hgrn_norm_g", "w_proj_hgrn", "w_out", "ln_ffn_g",
             "w_up", "conv_w", "conv_b", "w_down", "ln_final_g"]
    return (loss, dx.reshape(x.shape), *[res[k][0] for k in order], *[res[k][1] for k in order],
            *[res[k][2] for k in order], *[res[k][3] for k in order])
```

```python
import functools

import jax
import jax.numpy as jnp
from jax import lax
from jax.experimental import pallas as pl
from jax.experimental.pallas import tpu as pltpu

_F32 = jnp.float32
_BF = jnp.bfloat16
_RMS_EPS = 1e-6
_S5_MAX_RE = -1e-4
_LR, _B1, _B2, _ADAM_EPS, _WD, _STEP = 0.001, 0.9, 0.999, 1e-08, 0.01, 10
_MESH = pl.DeviceIdType.MESH
_ANY = pl.BlockSpec(memory_space=pl.ANY)
_LANE = 128
_VMEM_LIMIT = 56 * 1024 * 1024
_CHUNK = 64
_S5_TB = 128
_S5_SET = 8
_HGRN_HP = 4
_HGRN_SEG = 1024
_NCHIP = 4
_NDEV = 8


def _call(body, **kw):
    return pl.pallas_call(body, **kw)


def _cparams(*sem):
    return pltpu.CompilerParams(dimension_semantics=sem, vmem_limit_bytes=_VMEM_LIMIT)


def _tile(n, pref, unit=_LANE):
    if n <= pref:
        return n
    t = (pref // unit) * unit
    while t >= unit:
        if n % t == 0:
            return t
        t -= unit
    raise ValueError(f"no tile for {n}")


_OPERAND_BYTES = 12 * 1024 * 1024


def _tk_fit(k, tm, tn):
    best = _LANE
    for tk in range(_LANE, k + 1, _LANE):
        if k % tk == 0 and (tm + tn) * tk * 2 <= _OPERAND_BYTES:
            best = tk
    return best if k % _LANE == 0 else k


_NN = ((1,), (0,))
_NT = ((1,), (1,))
_TN = ((0,), (0,))


def _dg(a, b, dims):
    return lax.dot_general(a.astype(_BF), b.astype(_BF), (dims, ((), ())), preferred_element_type=_F32)


@jax.custom_vjp
def _bdot(a, b):
    return _dg(a, b, _NN)


def _bdot_f(a, b):
    return _dg(a, b, _NN), (a, b)


def _bdot_b(res, g):
    a, b = res
    return _dg(g, b, _NT).astype(a.dtype), _dg(a, g, _TN).astype(b.dtype)


_bdot.defvjp(_bdot_f, _bdot_b)


@jax.custom_vjp
def _bdot_nt(a, b):
    return _dg(a, b, _NT)


def _bdot_nt_f(a, b):
    return _dg(a, b, _NT), (a, b)


def _bdot_nt_b(res, g):
    a, b = res
    return _dg(g, b, _NN).astype(a.dtype), _dg(g, a, _TN).astype(b.dtype)


_bdot_nt.defvjp(_bdot_nt_f, _bdot_nt_b)


@jax.custom_vjp
def _bdot_tn(a, b):
    return _dg(a, b, _TN)


def _bdot_tn_f(a, b):
    return _dg(a, b, _TN), (a, b)


def _bdot_tn_b(res, g):
    a, b = res
    return _dg(b, g, _NT).astype(a.dtype), _dg(a, g, _NN).astype(b.dtype)


_bdot_tn.defvjp(_bdot_tn_f, _bdot_tn_b)


_SUBLANES = 8


def _shift_up(x, n):
    r = x.shape[0]
    if n % _SUBLANES == 0:
        return jnp.concatenate([x[n:], jnp.zeros((n,) + x.shape[1:], x.dtype)], axis=0)
    row = lax.broadcasted_iota(jnp.int32, x.shape, 0)
    return jnp.where(row < r - n, pltpu.roll(x, r - n, 0), 0.0)


@functools.partial(jax.custom_vjp, nondiff_argnums=(1,))
def _shift_down(x, n):
    if n % _SUBLANES == 0:
        return jnp.concatenate([jnp.zeros((n,) + x.shape[1:], x.dtype), x[:x.shape[0] - n]], axis=0)
    row = lax.broadcasted_iota(jnp.int32, x.shape, 0)
    return jnp.where(row >= n, pltpu.roll(x, n, 0), 0.0)


def _shift_down_f(x, n):
    return _shift_down(x, n), None


def _shift_down_b(n, _, g):
    return (_shift_up(g, n),)


_shift_down.defvjp(_shift_down_f, _shift_down_b)


def _sigmoid(x):
    return 1.0 / (1.0 + jnp.exp(-x))


def _silu(x):
    return x * _sigmoid(x)


def _gelu(x):
    return 0.5 * x * (1.0 + jnp.tanh(0.7978845608028654 * (x + 0.044715 * (x * x * x))))


def _rms_core(x, g):
    return x * lax.rsqrt(jnp.mean(x * x, axis=-1, keepdims=True) + _RMS_EPS) * g


def _mm(a, b, mode, out_dtype, *, tm, tn, tk, res=None, halves=None, rider=None, colperm=None, name):
    if colperm is None:
        def colperm(n_):
            return n_
    if tk is None:
        kdim = a.shape[0] if mode == "tn" else (b.shape[2] if (mode == "nt" and b.ndim == 3) else a.shape[1])
        tk = _tk_fit(kdim, tm, tn)
    if mode == "nn":
        m, k = a.shape
        a_spec = pl.BlockSpec((tm, tk), lambda i, j, kk: (i, kk))
        if b.ndim == 3:
            s, _, ns = b.shape
            n = s * ns
            npb = ns // tn
            b_spec = pl.BlockSpec((None, tk, tn), lambda i, j, kk: (j // npb, kk, j % npb))
        else:
            n = b.shape[1]
            b_spec = pl.BlockSpec((tk, tn), lambda i, j, kk: (kk, j))
        dims = _NN
    elif mode == "nt":
        m, k = a.shape
        a_spec = pl.BlockSpec((tm, tk), lambda i, j, kk: (i, colperm(kk)))
        if b.ndim == 3:
            s, n, ks = b.shape
            kpb = ks // tk
            b_spec = pl.BlockSpec((None, tn, tk), lambda i, j, kk: (kk // kpb, j, kk % kpb))
        else:
            n = b.shape[0]
            b_spec = pl.BlockSpec((tn, tk), lambda i, j, kk: (j, kk))
        dims = _NT
    else:
        k, m = a.shape
        n = b.shape[1]
        a_spec = pl.BlockSpec((tk, tm), lambda i, j, kk: (kk, i))
        b_spec = pl.BlockSpec((tk, tn), lambda i, j, kk: (kk, colperm(j)))
        dims = _TN
    nk = k // tk
    if halves is None:
        out_shape = jax.ShapeDtypeStruct((m, n), out_dtype)
        out_spec = pl.BlockSpec((tm, tn), lambda i, j, kk: (i, colperm(j) if mode == "nn" else j))
    elif halves == "cols":
        c2 = n // (2 * _NCHIP)
        tpc = c2 // tn
        out_shape = jax.ShapeDtypeStruct((2, _NCHIP, m, c2), out_dtype)
        out_spec = pl.BlockSpec((None, None, tm, tn),
                                lambda i, j, kk: ((j // tpc) % 2, j // (2 * tpc), i, j % tpc))
    else:
        c2 = n // 2
        tpc = c2 // tn
        r = m // _NCHIP
        tpr = r // tm
        out_shape = jax.ShapeDtypeStruct((2, _NCHIP, r, c2), out_dtype)
        out_spec = pl.BlockSpec((None, None, tm, tn),
                                lambda i, j, kk: (j // tpc, i // tpr, i % tpr, j % tpc))
    has_res = res is not None
    nreg = 3 if has_res else 2
    ni, nj = m // tm, n // tn
    r_ins = list(rider.ins) if rider else []
    r_outs = list(rider.outs) if rider else []

    def body(*refs):
        a_ref, b_ref = refs[0], refs[1]
        r_ref = refs[2] if has_res else None
        rin = refs[nreg:nreg + len(r_ins)]
        o_ref = refs[nreg + len(r_ins)]
        rout = refs[nreg + len(r_ins) + 1:nreg + len(r_ins) + 1 + len(r_outs)]
        acc_ref = refs[nreg + len(r_ins) + 1 + len(r_outs)]
        sems = refs[nreg + len(r_ins) + 2 + len(r_outs):]
        i, j, kk = pl.program_id(0), pl.program_id(1), pl.program_id(2)

        if rider:
            @pl.when(jnp.logical_and(jnp.logical_and(i == 0, j == 0), kk == 0))
            def _():
                rider.start(rin, rout, sems)

        @pl.when(kk == 0)
        def _():
            acc_ref[...] = jnp.zeros_like(acc_ref)

        acc_ref[...] += _dg(a_ref[...], b_ref[...], dims)

        @pl.when(kk == nk - 1)
        def _():
            out = acc_ref[...]
            if has_res:
                out = out + r_ref[...]
            o_ref[...] = out.astype(out_dtype)

        if rider:
            @pl.when(jnp.logical_and(jnp.logical_and(i == ni - 1, j == nj - 1), kk == nk - 1))
            def _():
                rider.finish(rin, rout, sems)

    in_specs = [a_spec, b_spec]
    args = [a, b]
    if has_res:
        in_specs.append(pl.BlockSpec((tm, tn), lambda i, j, kk: (i, j)))
        args.append(res)
    if not rider:
        return _call(body, name=name, grid=(ni, nj, nk), in_specs=in_specs, out_specs=out_spec,
                     out_shape=out_shape, scratch_shapes=[pltpu.VMEM((tm, tn), _F32)],
                     compiler_params=_cparams("parallel", "parallel", "arbitrary"))(*args)
    res_all = _call(body, name=name, grid=(ni, nj, nk), in_specs=in_specs + [_ANY] * len(r_ins),
                    out_specs=[out_spec] + [_ANY] * len(r_outs), out_shape=[out_shape] + r_outs,
                    input_output_aliases={nreg + k: 1 + v for k, v in rider.aliases.items()},
                    scratch_shapes=[pltpu.VMEM((tm, tn), _F32)] + list(rider.sems),
                    compiler_params=_cparams("arbitrary", "arbitrary", "arbitrary"))(*args, *r_ins)
    return res_all[0], list(res_all[1:])


def _rowcall(fn, ins, in_specs, outs, out_specs, acc, *, grid, name):
    nin = len(ins)

    def body(*refs):
        vals = fn(*[r[...] for r in refs[:nin]])
        first = pl.program_id(1) == 0
        for k, (o_ref, v) in enumerate(zip(refs[nin:], vals)):
            if acc[k]:
                @pl.when(first)
                def _(o_ref=o_ref):
                    o_ref[...] = jnp.zeros_like(o_ref)
                o_ref[...] += v.astype(o_ref.dtype)
            else:
                o_ref[...] = v.astype(o_ref.dtype)

    return _call(body, name=name, grid=grid, in_specs=in_specs, out_specs=out_specs, out_shape=outs,
                 compiler_params=_cparams("parallel", "arbitrary"))(*ins)


def _rb(tm, w, cb=0):
    return pl.BlockSpec((tm, w), lambda j, i: (i, cb + j))


def _cb(w, cb=0):
    return pl.BlockSpec((1, w), lambda j, i: (0, cb + j))


def _full(shape):
    nd = len(shape)
    return pl.BlockSpec(shape, lambda j, i: (0,) * nd)


def _sds(shape, dtype):
    return jax.ShapeDtypeStruct(shape, dtype)


def _rms_fwd(x, g, name):
    t, d = x.shape
    tm = _tile(t, 512, 8)
    return _rowcall(lambda xb, gb: (_rms_core(xb, gb),), [x, g], [_rb(tm, d), _full((1, d))],
                    [_sds((t, d), _BF)], [_rb(tm, d)], [False], grid=(1, t // tm), name=name)[0]


def _rms_bwd(x, g, dh, dres, name):
    t, d = x.shape
    tm = _tile(t, 256, 8)

    def fn(xb, gb, dhb, drb):
        _, vjp = jax.vjp(_rms_core, xb, gb)
        dx, dg = vjp(dhb)
        dx = dx + drb
        return dx, dx, dg

    return _rowcall(fn, [x, g, dh, dres], [_rb(tm, d), _full((1, d)), _rb(tm, d), _rb(tm, d)],
                    [_sds((t, d), _F32), _sds((t, d), _BF), _sds((1, d), _F32)],
                    [_rb(tm, d), _rb(tm, d), _full((1, d))], [False, False, True],
                    grid=(1, t // tm), name=name)


def _loss_head(x3, tgt, g):
    t, d = x3.shape
    tm = _tile(t, 256, 8)

    def fn(xb, tb, gb):
        y, vjp = jax.vjp(_rms_core, xb, gb)
        e = y - tb
        part = 0.5 * jnp.sum(jnp.mean(e * e, axis=-1, keepdims=True), axis=0, keepdims=True)
        dx, dg = vjp(e * (1.0 / d))
        return jnp.broadcast_to(part, (1, _LANE)), dx, dx, dg

    return _rowcall(fn, [x3, tgt, g], [_rb(tm, d), _rb(tm, d), _full((1, d))],
                    [_sds((1, _LANE), _F32), _sds((t, d), _F32), _sds((t, d), _BF), _sds((1, d), _F32)],
                    [_full((1, _LANE)), _rb(tm, d), _rb(tm, d), _full((1, d))], [True, False, False, True],
                    grid=(1, t // tm), name="loss_head")


def _s5_disc(a_re, a_im, log_dt):
    lam_re = jnp.minimum(a_re, _S5_MAX_RE)
    lam_im = a_im
    dt = jnp.exp(log_dt)
    mag = jnp.exp(lam_re * dt)
    abar_re = mag * jnp.cos(lam_im * dt)
    abar_im = mag * jnp.sin(lam_im * dt)
    den = lam_re * lam_re + lam_im * lam_im
    nr = abar_re - 1.0
    ni = abar_im
    coef_re = (nr * lam_re + ni * lam_im) / den
    coef_im = (ni * lam_re - nr * lam_im) / den
    return abar_re, abar_im, coef_re, coef_im


def _s5_prep(a_re, a_im, log_dt):
    g, p = a_re.shape

    def body(ar, ai, ld, o0, o1, o2, o3):
        outs = _s5_disc(ar[...], ai[...], ld[...])
        for o, v in zip((o0, o1, o2, o3), outs):
            o[...] = v

    return _call(body, name="s5_prep", out_shape=[_sds((g, p), _F32)] * 4)(a_re, a_im, log_dt)


def _s5_prep_bwd(a_re, a_im, log_dt, cts):
    g, p = a_re.shape

    def body(ar, ai, ld, c0, c1, c2, c3, d0, d1, d2):
        _, vjp = jax.vjp(_s5_disc, ar[...], ai[...], ld[...])
        outs = vjp((c0[...], c1[...], c2[...], c3[...]))
        for o, v in zip((d0, d1, d2), outs):
            o[...] = v

    return _call(body, name="s5_prep_bwd",
                 out_shape=[_sds((g, p), _F32), _sds((g, p), _F32), _sds((g, 1), _F32)])(a_re, a_im, log_dt, *cts)


def _s5_block(u, car, cai, ar, ai, cr, ci, b_re, b_im, c_re, c_imn, dvec):
    tb = u.shape[0]
    bur = _bdot(u, b_re)
    bui = _bdot(u, b_im)
    sr = cr * bur - ci * bui
    si = cr * bui + ci * bur
    row = lax.broadcasted_iota(jnp.int32, sr.shape, 0)
    sr = sr + jnp.where(row == 0, ar * car - ai * cai, 0.0)
    si = si + jnp.where(row == 0, ar * cai + ai * car, 0.0)
    pr, pi = ar, ai
    sh = 1
    while sh < tb:
        dr = _shift_down(sr, sh)
        di = _shift_down(si, sh)
        sr, si = sr + pr * dr - pi * di, si + pr * di + pi * dr
        pr, pi = pr * pr - pi * pi, 2.0 * pr * pi
        sh *= 2
    y = _bdot(sr, c_re) + _bdot(si, c_imn) + dvec * u
    last = row == tb - 1
    ncr = jnp.sum(jnp.where(last, sr, 0.0), axis=0, keepdims=True)
    nci = jnp.sum(jnp.where(last, si, 0.0), axis=0, keepdims=True)
    return y, ncr, nci


def _s5_specs(tb, lw, nt, rev):
    tmap = (lambda t: nt - 1 - t) if rev else (lambda t: t)
    vec = pl.BlockSpec((1, lw), lambda s, t: (0, s))
    return dict(
        u=pl.BlockSpec((tb, _LANE), lambda s, t: (tmap(t), s)),
        car=pl.BlockSpec((None, 1, lw), lambda s, t: (tmap(t), 0, s)),
        vec=vec,
        bmat=pl.BlockSpec((None, _LANE, lw), lambda s, t: (s, 0, 0)),
        cmat=pl.BlockSpec((None, lw, _LANE), lambda s, t: (s, 0, 0)),
        dvec=pl.BlockSpec((1, _LANE), lambda s, t: (0, s)),
    )


def _s5_fwd(proj, par, t, w5, rider=None):
    ar, ai, cr, ci, b_re, b_im, c_re, c_imn, dvec = par
    ns = w5 // _LANE
    lw = ar.shape[1] // ns
    tb = min(_S5_TB, t)
    nt = t // tb
    sp = _s5_specs(tb, lw, nt, False)

    def body(u_ref, ar_r, ai_r, cr_r, ci_r, bre_r, bim_r, cre_r, cim_r, d_r, y_ref, car_ref, cai_ref, s_r, s_i):
        @pl.when(pl.program_id(1) == 0)
        def _():
            s_r[...] = jnp.zeros_like(s_r)
            s_i[...] = jnp.zeros_like(s_i)

        car_ref[...] = s_r[...]
        cai_ref[...] = s_i[...]
        y, ncr, nci = _s5_block(u_ref[...], s_r[...], s_i[...], ar_r[...], ai_r[...], cr_r[...], ci_r[...],
                                bre_r[...], bim_r[...], cre_r[...], cim_r[...], d_r[...])
        y_ref[...] = y
        s_r[...] = ncr
        s_i[...] = nci

    kw = dict(name="s5_fwd", grid=(ns, nt),
              in_specs=[sp["u"], sp["vec"], sp["vec"], sp["vec"], sp["vec"], sp["bmat"], sp["bmat"],
                        sp["cmat"], sp["cmat"], sp["dvec"]],
              out_specs=[sp["u"], sp["car"], sp["car"]],
              out_shape=[_sds((t, w5), _F32), _sds((nt, 1, ns * lw), _F32), _sds((nt, 1, ns * lw), _F32)],
              scratch_shapes=[pltpu.VMEM((1, lw), _F32), pltpu.VMEM((1, lw), _F32)])
    args = (proj, ar, ai, cr, ci, b_re, b_im, c_re, c_imn, dvec)
    if rider is None:
        return _call(body, compiler_params=_cparams("parallel", "arbitrary"), **kw)(*args), []
    return _call_riding(body, rider, args=args, **kw)


def _s5_bwd(proj, dy, car, cai, par, t, w5, rider=None):
    ar, ai, cr, ci, b_re, b_im, c_re, c_imn, dvec = par
    ns = w5 // _LANE
    lw = ar.shape[1] // ns
    tb = min(_S5_TB, t)
    nt = t // tb
    sp = _s5_specs(tb, lw, nt, True)

    def body(u_ref, dy_ref, car_ref, cai_ref, ar_r, ai_r, cr_r, ci_r, bre_r, bim_r, cre_r, cim_r, d_r,
             du_ref, g_ar, g_ai, g_cr, g_ci, g_bre, g_bim, g_cre, g_cim, g_d, ds_r, ds_i):
        accs = (g_ar, g_ai, g_cr, g_ci, g_bre, g_bim, g_cre, g_cim, g_d)

        @pl.when(pl.program_id(1) == 0)
        def _():
            ds_r[...] = jnp.zeros_like(ds_r)
            ds_i[...] = jnp.zeros_like(ds_i)
            for o in accs:
                o[...] = jnp.zeros_like(o)

        _, vjp = jax.vjp(_s5_block, u_ref[...], car_ref[...], cai_ref[...], ar_r[...], ai_r[...], cr_r[...],
                         ci_r[...], bre_r[...], bim_r[...], cre_r[...], cim_r[...], d_r[...])
        grads = vjp((dy_ref[...], ds_r[...], ds_i[...]))
        du_ref[...] = grads[0].astype(_BF)
        ds_r[...] = grads[1]
        ds_i[...] = grads[2]
        for o, gval in zip(accs, grads[3:]):
            o[...] += gval

    vec_o = _sds((1, ns * lw), _F32)
    kw = dict(name="s5_bwd", grid=(ns, nt),
              in_specs=[sp["u"], sp["u"], sp["car"], sp["car"], sp["vec"], sp["vec"], sp["vec"], sp["vec"],
                        sp["bmat"], sp["bmat"], sp["cmat"], sp["cmat"], sp["dvec"]],
              out_specs=[sp["u"], sp["vec"], sp["vec"], sp["vec"], sp["vec"], sp["bmat"], sp["bmat"],
                         sp["cmat"], sp["cmat"], sp["dvec"]],
              out_shape=[_sds((t, w5), _BF), vec_o, vec_o, vec_o, vec_o,
                         _sds(b_re.shape, _F32), _sds(b_re.shape, _F32), _sds(c_re.shape, _F32),
                         _sds(c_re.shape, _F32), _sds((1, w5), _F32)],
              scratch_shapes=[pltpu.VMEM((1, lw), _F32), pltpu.VMEM((1, lw), _F32)])
    args = (proj, dy, car, cai, ar, ai, cr, ci, b_re, b_im, c_re, c_imn, dvec)
    if rider is None:
        return _call(body, compiler_params=_cparams("parallel", "arbitrary"), **kw)(*args), []
    return _call_riding(body, rider, args=args, **kw)


def _bd_in(b):
    g, p, c = b.shape
    s = g // _S5_SET
    b4 = b.reshape(s, _S5_SET, p, c).transpose(0, 1, 3, 2)
    eye = jnp.eye(_S5_SET, dtype=b.dtype)
    return (b4[:, :, :, None, :] * eye[None, :, None, :, None]).reshape(s, _S5_SET * c, _S5_SET * p)


def _bd_in_grad(d, p, c):
    s = d.shape[0]
    eye = jnp.eye(_S5_SET, dtype=d.dtype)
    d5 = d.reshape(s, _S5_SET, c, _S5_SET, p) * eye[None, :, None, :, None]
    return d5.sum(axis=3).transpose(0, 1, 3, 2).reshape(s * _S5_SET, p, c)


def _bd_out(cm):
    g, c, p = cm.shape
    s = g // _S5_SET
    c4 = cm.reshape(s, _S5_SET, c, p).transpose(0, 1, 3, 2)
    eye = jnp.eye(_S5_SET, dtype=cm.dtype)
    return (c4[:, :, :, None, :] * eye[None, :, None, :, None]).reshape(s, _S5_SET * p, _S5_SET * c)


def _bd_out_grad(d, p, c):
    s = d.shape[0]
    eye = jnp.eye(_S5_SET, dtype=d.dtype)
    d5 = d.reshape(s, _S5_SET, p, _S5_SET, c) * eye[None, :, None, :, None]
    return d5.sum(axis=3).transpose(0, 1, 3, 2).reshape(s * _S5_SET, c, p)


def _s5glu_fwd(y, wglu, bglu):
    t, w5 = y.shape
    tm = _tile(t, 256, 8)

    def fn(yb, wb, bb):
        z1 = _gelu(yb)
        a = _dg(z1, wb, _NN) + bb
        return (z1 * _sigmoid(a),)

    return _rowcall(fn, [y, wglu, bglu], [_rb(tm, w5), _full(wglu.shape), _full((1, w5))],
                    [_sds((t, w5), _BF)], [_rb(tm, w5)], [False], grid=(1, t // tm), name="s5glu_fwd")[0]


def _s5glu_bwd(y, dz, wglu, bglu):
    t, w5 = y.shape
    tm = _tile(t, 256, 8)

    def fn(yb, dzb, wb, bb):
        z1, gelu_vjp = jax.vjp(_gelu, yb)
        sig = _sigmoid(_dg(z1, wb, _NN) + bb)
        da = dzb * z1 * sig * (1.0 - sig)
        dz1 = dzb * sig + _dg(da, wb, _NT)
        (dy,) = gelu_vjp(dz1)
        return dy, _dg(z1, da, _TN), jnp.sum(da, axis=0, keepdims=True)

    return _rowcall(fn, [y, dz, wglu, bglu], [_rb(tm, w5), _rb(tm, w5), _full(wglu.shape), _full((1, w5))],
                    [_sds((t, w5), _F32), _sds((w5, w5), _F32), _sds((1, w5), _F32)],
                    [_rb(tm, w5), _full((w5, w5)), _full((1, w5))], [False, True, True],
                    grid=(1, t // tm), name="s5glu_bwd")


def _tri_mats(c, transposed):
    d0, d1 = (1, 0) if transposed else (0, 1)
    t = lax.broadcasted_iota(jnp.int32, (c, c), d0)
    j = lax.broadcasted_iota(jnp.int32, (c, c), d1)
    low = (j <= t).astype(_F32)
    mats = [low]
    for sh in (6, 5, 4):
        m = 1 << (sh - 1)
        r = ((t >> sh) << sh) + (m - 1)
        mats.append(low - (j <= r).astype(_F32))
    return [mm.astype(_BF) for mm in mats]


def _split_dot(mat, x):
    hi = x.astype(_BF)
    lo = (x - hi.astype(_F32)).astype(_BF)
    return (jnp.dot(mat, hi, preferred_element_type=_F32) + jnp.dot(mat, lo, preferred_element_type=_F32))


@jax.custom_vjp
def _decay_sums(lf):
    return tuple(_split_dot(mm, lf) for mm in _tri_mats(lf.shape[0], False))


def _decay_sums_f(lf):
    return _decay_sums(lf), None


def _decay_sums_b(_, gs):
    mats = _tri_mats(gs[0].shape[0], True)
    out = _split_dot(mats[0], gs[0])
    for mm, gval in zip(mats[1:], gs[1:]):
        out = out + _split_dot(mm, gval)
    return (out,)


_decay_sums.defvjp(_decay_sums_f, _decay_sums_b)


def _hgrn_chunk(qi, fi, vi, gi, st, lb, ng):
    c = qi.shape[0]
    row = lax.broadcasted_iota(jnp.int32, (c, 1), 0)
    q = _silu(qi)
    lf = jnp.log(lb + (1.0 - lb) * _sigmoid(fi))
    k = (1.0 - lb) * _sigmoid(-fi)
    b, p32, p16, p8 = _decay_sums(lf)
    btot = jnp.sum(lf, axis=0, keepdims=True)
    inter = _bdot_nt(q * jnp.exp(b), st)
    tt = lax.broadcasted_iota(jnp.int32, (c, c), 0)
    ss = lax.broadcasted_iota(jnp.int32, (c, c), 1)
    sc = jnp.zeros((c, c), _F32)
    for sh, p in ((6, p32), (5, p16), (4, p8)):
        upper = ((row >> (sh - 1)) & 1) == 1
        qm = jnp.where(upper, q * jnp.exp(jnp.where(upper, p, 0.0)), 0.0)
        km = jnp.where(upper, 0.0, k * jnp.exp(jnp.where(upper, 0.0, -p)))
        sc = sc + jnp.where((tt >> sh) == (ss >> sh), _bdot_nt(qm, km), 0.0)
    o = inter + _bdot(sc, vi)
    o = o + jnp.sum(q * k, axis=1, keepdims=True) * vi
    for d in range(1, 8):
        valid = (row & 7) >= d
        bs = _shift_down(b, d)
        ks = _shift_down(k, d)
        vs = _shift_down(vi, d)
        w = jnp.where(valid, q * jnp.exp(jnp.where(valid, b - bs, 0.0)) * ks, 0.0)
        o = o + jnp.sum(w, axis=1, keepdims=True) * vs
    st_new = st * jnp.exp(btot) + _bdot_tn(vi, k * jnp.exp(btot - b))
    on = o * lax.rsqrt(jnp.mean(o * o, axis=1, keepdims=True) + _RMS_EPS) * ng
    return on * _silu(gi), st_new


def _hgrn_geom(t, w5, hw):
    hp = _HGRN_HP if (hw // _LANE) % _HGRN_HP == 0 and w5 % (_LANE * _HGRN_HP) == 0 else 1
    seg = min(t, _HGRN_SEG)
    return hp, hp * _LANE, seg, t // seg


def _hgrn_in_specs(seg, wd, w5, hw, tmap):
    nhp = hw // wd
    qb = w5 // wd
    return [pl.BlockSpec((seg, wd), (lambda h, s, k=k: (tmap(s), qb + k * nhp + h))) for k in range(4)]


def _hgrn_fwd(proj, lb, ng, t, w5, hw):
    assert _CHUNK == 64
    hp, wd, seg, nseg = _hgrn_geom(t, w5, hw)
    ncs = seg // _CHUNK
    vec = pl.BlockSpec((1, wd), lambda h, s: (0, h))

    def body(q_ref, f_ref, i_ref, g_ref, lb_ref, ng_ref, og_ref, st_ref, s_scr):
        @pl.when(pl.program_id(1) == 0)
        def _():
            s_scr[...] = jnp.zeros_like(s_scr)

        def step(ci, carry):
            r = pl.ds(pl.multiple_of(ci * _CHUNK, _CHUNK), _CHUNK)
            for hh in range(hp):
                ln = slice(hh * _LANE, (hh + 1) * _LANE)
                st_ref[hh, ci] = s_scr[hh]
                og, sn = _hgrn_chunk(q_ref[r, ln], f_ref[r, ln], i_ref[r, ln], g_ref[r, ln], s_scr[hh],
                                     lb_ref[:, ln], ng_ref[:, ln])
                og_ref[r, ln] = og.astype(_BF)
                s_scr[hh] = sn
            return carry

        lax.fori_loop(0, ncs, step, 0)

    return _call(body, name="hgrn_fwd", grid=(hw // wd, nseg),
                 in_specs=_hgrn_in_specs(seg, wd, w5, hw, lambda s: s) + [vec, vec],
                 out_specs=[pl.BlockSpec((seg, wd), lambda h, s: (s, h)),
                            pl.BlockSpec((hp, ncs, _LANE, _LANE), lambda h, s: (h, s, 0, 0))],
                 out_shape=[_sds((t, hw), _BF), _sds((hw // _LANE, t // _CHUNK, _LANE, _LANE), _F32)],
                 scratch_shapes=[pltpu.VMEM((hp, _LANE, _LANE), _F32)],
                 compiler_params=_cparams("parallel", "arbitrary"))(proj, proj, proj, proj, lb, ng)


def _hgrn_bwd(proj, lb, ng, st_all, dog, t, w5, hw):
    hp, wd, seg, nseg = _hgrn_geom(t, w5, hw)
    ncs = seg // _CHUNK

    def rev(s):
        return nseg - 1 - s

    vec = pl.BlockSpec((1, wd), lambda h, s: (0, h))
    col = pl.BlockSpec((seg, wd), lambda h, s: (rev(s), h))

    def body(q_ref, f_ref, i_ref, g_ref, lb_ref, ng_ref, st_ref, dog_ref,
             dq_ref, df_ref, di_ref, dg_ref, dlb_ref, dng_ref, ds_scr):
        @pl.when(pl.program_id(1) == 0)
        def _():
            ds_scr[...] = jnp.zeros_like(ds_scr)
            dlb_ref[...] = jnp.zeros_like(dlb_ref)
            dng_ref[...] = jnp.zeros_like(dng_ref)

        def step(kk, carry):
            ci = ncs - 1 - kk
            r = pl.ds(pl.multiple_of(ci * _CHUNK, _CHUNK), _CHUNK)
            for hh in range(hp):
                ln = slice(hh * _LANE, (hh + 1) * _LANE)
                _, vjp = jax.vjp(_hgrn_chunk, q_ref[r, ln], f_ref[r, ln], i_ref[r, ln], g_ref[r, ln], st_ref[hh, ci],
                                 lb_ref[:, ln], ng_ref[:, ln])
                dq, df, di, dg, ds, dlb, dng = vjp((dog_ref[r, ln], ds_scr[hh]))
                dq_ref[r, ln] = dq.astype(_BF)
                df_ref[r, ln] = df.astype(_BF)
                di_ref[r, ln] = di.astype(_BF)
                dg_ref[r, ln] = dg.astype(_BF)
                ds_scr[hh] = ds
                dlb_ref[:, ln] += dlb
                dng_ref[:, ln] += dng
            return carry

        lax.fori_loop(0, ncs, step, 0)

    return _call(body, name="hgrn_bwd", grid=(hw // wd, nseg),
                 in_specs=_hgrn_in_specs(seg, wd, w5, hw, rev) + [
                     vec, vec, pl.BlockSpec((hp, ncs, _LANE, _LANE), lambda h, s: (h, rev(s), 0, 0)), col],
                 out_specs=[col, col, col, col, vec, vec],
                 out_shape=[_sds((t, hw), _BF)] * 4 + [_sds((1, hw), _F32)] * 2,
                 scratch_shapes=[pltpu.VMEM((hp, _LANE, _LANE), _F32)],
                 compiler_params=_cparams("parallel", "arbitrary"))(proj, proj, proj, proj, lb, ng, st_all, dog)


def _lb_of(logits):
    mx = jnp.max(logits, axis=0, keepdims=True)
    e = jnp.exp(logits - mx)
    sm = e / jnp.sum(e, axis=0, keepdims=True)
    row = lax.broadcasted_iota(jnp.int32, logits.shape, 0)
    return jnp.sum(jnp.where(row == 0, sm, 0.0), axis=0, keepdims=True)


def _lb_prep(logits):
    def body(l_ref, o_ref):
        o_ref[...] = _lb_of(l_ref[...])

    return _call(body, name="lb_prep", out_shape=_sds((1, logits.shape[1]), _F32))(logits)


def _lb_prep_bwd(logits, dlb):
    def body(l_ref, d_ref, o_ref):
        _, vjp = jax.vjp(_lb_of, l_ref[...])
        o_ref[...] = vjp(d_ref[...])[0]

    return _call(body, name="lb_prep_bwd", out_shape=_sds(logits.shape, _F32))(logits, dlb)


def _merge_fwd(proj, ys, yh, t, d, gs_off):
    tm = _tile(t, 256, 8)
    w = _tile(d, 1024)
    nb = d // w

    def fn(gs, gh, a, b):
        return (_sigmoid(gs) * a + _sigmoid(gh) * b,)

    return _rowcall(fn, [proj, proj, ys, yh], [_rb(tm, w, gs_off // w), _rb(tm, w, gs_off // w + nb), _rb(tm, w), _rb(tm, w)],
                    [_sds((t, d), _BF)], [_rb(tm, w)], [False], grid=(nb, t // tm), name="merge_fwd")[0]


def _merge_bwd(proj, ys, yh, dm, t, d, gs_off):
    tm = _tile(t, 256, 8)
    w = _tile(d, 1024)
    nb = d // w

    def fn(gs, gh, a, b, g):
        s1 = _sigmoid(gs)
        s2 = _sigmoid(gh)
        return g * s1, g * s2, g * a * s1 * (1.0 - s1), g * b * s2 * (1.0 - s2)

    return _rowcall(fn, [proj, proj, ys, yh, dm],
                    [_rb(tm, w, gs_off // w), _rb(tm, w, gs_off // w + nb), _rb(tm, w), _rb(tm, w), _rb(tm, w)],
                    [_sds((t, d), _BF)] * 4, [_rb(tm, w)] * 4, [False] * 4, grid=(nb, t // tm), name="merge_bwd")


def _prev_rows(up_prev8, is_first):
    p1 = jnp.where(is_first, 0.0, up_prev8[7:8, :])
    p2 = jnp.where(is_first, 0.0, up_prev8[6:7, :])
    return p1, p2


def _causal_taps(cur, p1, p2):
    row = lax.broadcasted_iota(jnp.int32, cur.shape, 0)
    s1 = jnp.where(row == 0, p1, _shift_down(cur, 1))
    s2 = jnp.where(row == 0, p2, jnp.where(row == 1, p1, _shift_down(cur, 2)))
    return s1, s2


def _pair_perm(nbh, per):
    def perm(n_):
        big = n_ // per
        return (2 * (big % nbh) + big // nbh) * per + n_ % per
    return perm


def _conv_specs(tm, w, nb_half, t):
    r8 = tm // 8
    cur_g = pl.BlockSpec((tm, w), lambda j, i: (i, 2 * j))
    cur_v = pl.BlockSpec((tm, w), lambda j, i: (i, 2 * j + 1))
    prev_g = pl.BlockSpec((8, w), lambda j, i: (jnp.maximum(i * r8 - 1, 0), 2 * j))
    prev_v = pl.BlockSpec((8, w), lambda j, i: (jnp.maximum(i * r8 - 1, 0), 2 * j + 1))
    w_g = pl.BlockSpec((3, w), lambda j, i: (0, j))
    w_v = pl.BlockSpec((3, w), lambda j, i: (0, nb_half + j))
    b_g = pl.BlockSpec((1, w), lambda j, i: (0, j))
    b_v = pl.BlockSpec((1, w), lambda j, i: (0, nb_half + j))
    return cur_g, cur_v, prev_g, prev_v, w_g, w_v, b_g, b_v


def _conv_of(cur, prev8, wt, bias, is_first):
    p1, p2 = _prev_rows(prev8, is_first)
    s1, s2 = _causal_taps(cur, p1, p2)
    return bias + wt[0:1, :] * s2 + wt[1:2, :] * s1 + wt[2:3, :] * cur


def _convact_fwd(up, cw, cb, t, dff):
    tm = _tile(t, 512, 8)
    w = _tile(dff, 1408)
    nbh = dff // w
    sp = _conv_specs(tm, w, nbh, t)

    def body(ug, uv, pg, pv, wg, wv, bg, bv, o_ref):
        first = pl.program_id(1) == 0
        gate = _conv_of(ug[...], pg[...], wg[...], bg[...], first)
        val = _conv_of(uv[...], pv[...], wv[...], bv[...], first)
        o_ref[...] = (_silu(gate) * val).astype(_BF)

    return _call(body, name="convact_fwd", grid=(nbh, t // tm), in_specs=list(sp),
                 out_specs=pl.BlockSpec((tm, w), lambda j, i: (i, j)), out_shape=_sds((t, dff), _BF),
                 compiler_params=_cparams("parallel", "arbitrary"))(up, up, up, up, cw, cw, cb, cb)


def _convact_bwd(up, dact, cw, cb, t, dff):
    tm = _tile(t, 256, 8)
    w = _tile(dff, 1408)
    nbh = dff // w
    r8 = tm // 8
    nt = t // tm
    last8 = t // 8 - 1

    def triple(off):
        return [pl.BlockSpec((tm, w), lambda j, i: (i, 2 * j + off)),
                pl.BlockSpec((8, w), lambda j, i: (jnp.minimum((i + 1) * r8, last8), 2 * j + off)),
                pl.BlockSpec((8, w), lambda j, i: (jnp.maximum(i * r8 - 1, 0), 2 * j + off))]

    def body(ug, ugn, ugp, uv, uvn, uvp, wg_ref, wv_ref, bg_ref, bv_ref, da_ref, dan_ref, du_ref, dw_ref, db_ref):
        i = pl.program_id(1)
        first = i == 0
        is_last = i == nt - 1
        gate = _conv_of(jnp.concatenate([ug[...], ugn[...]], axis=0), ugp[...], wg_ref[...], bg_ref[...], first)
        val = _conv_of(jnp.concatenate([uv[...], uvn[...]], axis=0), uvp[...], wv_ref[...], bv_ref[...], first)
        da = jnp.concatenate([da_ref[...], dan_ref[...]], axis=0)
        row = lax.broadcasted_iota(jnp.int32, da.shape, 0)
        da = jnp.where(jnp.logical_and(row >= tm, is_last), 0.0, da)
        sg = _sigmoid(gate)
        halves = ((da * val * sg * (1.0 + gate * (1.0 - sg)), wg_ref, ug, ugp),
                  (da * gate * sg, wv_ref, uv, uvp))

        @pl.when(first)
        def _():
            dw_ref[...] = jnp.zeros_like(dw_ref)
            db_ref[...] = jnp.zeros_like(db_ref)

        for h, (dc, w_ref, u_ref, p_ref) in enumerate(halves):
            ln = slice(h * w, (h + 1) * w)
            wt = w_ref[...]
            du = wt[2:3, :] * dc + wt[1:2, :] * _shift_up(dc, 1) + wt[0:1, :] * _shift_up(dc, 2)
            du_ref[:, ln] = du[0:tm, :].astype(_BF)
            dcm = dc[0:tm, :]
            cur = u_ref[...]
            p1, p2 = _prev_rows(p_ref[...], first)
            s1, s2 = _causal_taps(cur, p1, p2)
            dw_ref[0:1, ln] += jnp.sum(dcm * s2, axis=0, keepdims=True)
            dw_ref[1:2, ln] += jnp.sum(dcm * s1, axis=0, keepdims=True)
            dw_ref[2:3, ln] += jnp.sum(dcm * cur, axis=0, keepdims=True)
            db_ref[:, ln] += jnp.sum(dcm, axis=0, keepdims=True)

    in_specs = (triple(0) + triple(1)
                + [pl.BlockSpec((3, w), lambda j, i: (0, j)), pl.BlockSpec((3, w), lambda j, i: (0, nbh + j)),
                   pl.BlockSpec((1, w), lambda j, i: (0, j)), pl.BlockSpec((1, w), lambda j, i: (0, nbh + j)),
                   pl.BlockSpec((tm, w), lambda j, i: (i, j)),
                   pl.BlockSpec((8, w), lambda j, i: (jnp.minimum((i + 1) * r8, last8), j))])
    dup, dw_p, db_p = _call(
        body, name="convact_bwd", grid=(nbh, nt), in_specs=in_specs,
        out_specs=[pl.BlockSpec((tm, 2 * w), lambda j, i: (i, j)), pl.BlockSpec((3, 2 * w), lambda j, i: (0, j)),
                   pl.BlockSpec((1, 2 * w), lambda j, i: (0, j))],
        out_shape=[_sds((t, 2 * dff), _BF), _sds((3, 2 * dff), _F32), _sds((1, 2 * dff), _F32)],
        compiler_params=_cparams("parallel", "arbitrary"))(up, up, up, up, up, up, cw, cw, cb, cb, dact, dact)

    def natural(v):
        k = v.shape[0]
        return v.reshape(k, nbh, 2, w).transpose(0, 2, 1, 3).reshape(k, 2 * dff)

    return dup, natural(dw_p), natural(db_p)


def _me():
    return lax.axis_index("x"), lax.axis_index("y"), lax.axis_index("c")


def _other_chips(x, y):
    return [(1 - x, y), (x, 1 - y), (1 - x, 1 - y)]


def _rcopy(src, dst, ssem, rsem, dev):
    return pltpu.make_async_remote_copy(src_ref=src, dst_ref=dst, send_sem=ssem, recv_sem=rsem,
                                        device_id=dev, device_id_type=_MESH)


def _cast_into_slot(w, sel):
    r, c = w.shape
    tm = _tile(r, 256, 16)

    def body(sel_ref, w_ref, o_ref):
        o_ref[...] = w_ref[...].astype(_BF)

    gs = pltpu.PrefetchScalarGridSpec(
        num_scalar_prefetch=1, grid=(r // tm,),
        in_specs=[pl.BlockSpec((tm, c), lambda i, s: (i, 0))],
        out_specs=pl.BlockSpec((None, tm, c), lambda i, s: (s[0], i, 0)))
    return _call(body, name="cast_into_slot", grid_spec=gs, out_shape=_sds((_NCHIP, r, c), _BF),
                 compiler_params=_cparams("parallel"))(sel, w)


class _Plan:
    def __init__(self, ins, outs, aliases, sems, start, finish):
        self.ins, self.outs, self.aliases, self.sems, self.start, self.finish = ins, outs, aliases, sems, start, finish


def _run_plan(plan, name):
    ni, no = len(plan.ins), len(plan.outs)

    def body(*refs):
        rin, rout, sems = refs[:ni], refs[ni:ni + no], refs[ni + no:]
        plan.start(rin, rout, sems)
        plan.finish(rin, rout, sems)

    return _call(body, name=name, in_specs=[_ANY] * ni, out_specs=[_ANY] * no, out_shape=list(plan.outs),
                 input_output_aliases=dict(plan.aliases), scratch_shapes=list(plan.sems))(*plan.ins)


def _call_riding(body, rider, *, name, grid, in_specs, out_specs, out_shape, scratch_shapes, args):
    n_in, n_out, n_scr = len(in_specs), len(out_specs), len(scratch_shapes)
    n_rin, n_rout = len(rider.ins), len(rider.outs)

    def wrapped(*refs):
        ins, rin = refs[:n_in], refs[n_in:n_in + n_rin]
        o0 = n_in + n_rin
        outs, rout = refs[o0:o0 + n_out], refs[o0 + n_out:o0 + n_out + n_rout]
        s0 = o0 + n_out + n_rout
        scratch, sems = refs[s0:s0 + n_scr], refs[s0 + n_scr:]
        first = functools.reduce(jnp.logical_and, [pl.program_id(k) == 0 for k in range(len(grid))])
        last = functools.reduce(jnp.logical_and, [pl.program_id(k) == grid[k] - 1 for k in range(len(grid))])

        @pl.when(first)
        def _():
            rider.start(rin, rout, sems)

        body(*ins, *outs, *scratch)

        @pl.when(last)
        def _():
            rider.finish(rin, rout, sems)

    res = _call(wrapped, name=name, grid=grid, in_specs=list(in_specs) + [_ANY] * n_rin,
                out_specs=list(out_specs) + [_ANY] * n_rout, out_shape=list(out_shape) + list(rider.outs),
                input_output_aliases={n_in + k: n_out + v for k, v in rider.aliases.items()},
                scratch_shapes=list(scratch_shapes) + list(rider.sems),
                compiler_params=_cparams(*(["arbitrary"] * len(grid))))(*args, *rider.ins)
    return list(res[:n_out]), list(res[n_out:])


def _gather_plan(bufs, direct):
    n, nd = len(bufs), len(direct)

    def where():
        x, y, c = _me()
        return c, 2 * x + y, _other_chips(x, y), (x, y, 1 - c)

    def piece(outs, a, chip, h):
        r2 = bufs[a].shape[1] // 2
        return outs[a].at[chip, pl.ds(h * r2, r2)]

    def send(outs, sems, a, j, me, c, chip):
        return _rcopy(piece(outs, a, me, c), piece(outs, a, me, c), sems[0].at[3 * a + j], sems[1].at[3 * a + j],
                      (chip[0], chip[1], c))

    def forward(outs, sems, a, j, pc, c, sib):
        return _rcopy(piece(outs, a, pc, c), piece(outs, a, pc, c), sems[2].at[3 * a + j], sems[3].at[3 * a + j], sib)

    def dsend(dins, douts, sems, a, j, me, c, chip):
        return _rcopy(dins[a], douts[a].at[me], sems[4].at[3 * a + j], sems[5].at[3 * a + j], (chip[0], chip[1], c))

    def start(rin, rout, sems):
        outs, dins, douts = rout[:n], rin[n:], rout[n:]
        c, me, chips, _ = where()
        for a in range(n):
            for j, chip in enumerate(chips):
                send(outs, sems, a, j, me, c, chip).start()
        for a in range(nd):
            pltpu.make_async_copy(dins[a], douts[a].at[me], sems[6].at[a]).start()
            for j, chip in enumerate(chips):
                dsend(dins, douts, sems, a, j, me, c, chip).start()

    def finish(rin, rout, sems):
        outs, dins, douts = rout[:n], rin[n:], rout[n:]
        c, me, chips, sib = where()
        for a in range(n):
            for j, (cx, cy) in enumerate(chips):
                pc = 2 * cx + cy
                _rcopy(piece(outs, a, me, c), piece(outs, a, pc, c), sems[0].at[3 * a + j], sems[1].at[3 * a + j],
                       (cx, cy, c)).wait_recv()
                forward(outs, sems, a, j, pc, c, sib).start()
        for a in range(n):
            for j, (cx, cy) in enumerate(chips):
                pc = 2 * cx + cy
                _rcopy(piece(outs, a, pc, 1 - c), piece(outs, a, pc, 1 - c), sems[2].at[3 * a + j],
                       sems[3].at[3 * a + j], sib).wait_recv()
        for a in range(nd):
            for j, (cx, cy) in enumerate(chips):
                _rcopy(dins[a], douts[a].at[2 * cx + cy], sems[4].at[3 * a + j], sems[5].at[3 * a + j],
                       (cx, cy, c)).wait_recv()
        for a in range(n):
            for j, (cx, cy) in enumerate(chips):
                send(outs, sems, a, j, me, c, (cx, cy)).wait_send()
                forward(outs, sems, a, j, 2 * cx + cy, c, sib).wait_send()
        for a in range(nd):
            pltpu.make_async_copy(dins[a], douts[a].at[me], sems[6].at[a]).wait()
            for j, chip in enumerate(chips):
                dsend(dins, douts, sems, a, j, me, c, chip).wait_send()

    dma = pltpu.SemaphoreType.DMA
    return _Plan(list(bufs) + list(direct),
                 [_sds(b.shape, b.dtype) for b in bufs] + [_sds((_NCHIP,) + s.shape, s.dtype) for s in direct],
                 {a: a for a in range(n)},
                 [dma((3 * max(n, 1),)), dma((3 * max(n, 1),)), dma((3 * max(n, 1),)), dma((3 * max(n, 1),)),
                  dma((3 * max(nd, 1),)), dma((3 * max(nd, 1),)), dma((max(nd, 1),))], start, finish)


def _swap_halves(grads, name):
    n = len(grads)

    def body(*refs):
        ins, outs = refs[:n], refs[n:2 * n]
        ssem, rsem = refs[2 * n:]
        x, y, c = _me()
        sib = (x, y, 1 - c)
        cps = []
        for a in range(n):
            cp = _rcopy(ins[a].at[1 - c], outs[a], ssem.at[a], rsem.at[a], sib)
            cp.start()
            cps.append(cp)
        for cp in cps:
            cp.wait_recv()
        for cp in cps:
            cp.wait_send()

    dma = pltpu.SemaphoreType.DMA
    return _call(body, name=name, in_specs=[_ANY] * n, out_specs=[_ANY] * n,
                 out_shape=[_sds(g.shape[1:], g.dtype) for g in grads],
                 scratch_shapes=[dma((n,)), dma((n,))])(*grads)


def _add_pairs(grads, theirs, sel):
    _, s, r, c2 = grads.shape
    a3 = grads.reshape(2, s * r, c2)
    b2 = theirs.reshape(s * r, c2)
    tm = _tile(s * r, 512, 16)

    def body(sel_ref, a_ref, b_ref, o_ref):
        o_ref[...] = (a_ref[...].astype(_F32) + b_ref[...].astype(_F32)).astype(_BF)

    gs = pltpu.PrefetchScalarGridSpec(
        num_scalar_prefetch=1, grid=(s * r // tm,),
        in_specs=[pl.BlockSpec((None, tm, c2), lambda i, q: (q[1], i, 0)), pl.BlockSpec((tm, c2), lambda i, q: (i, 0))],
        out_specs=pl.BlockSpec((tm, c2), lambda i, q: (i, 0)))
    out = _call(body, name="chip_sum", grid_spec=gs, out_shape=_sds((s * r, c2), _BF),
                compiler_params=_cparams("parallel"))(sel, a3, b2)
    return out.reshape(s, r, c2)


def _exchange_plan(sums, small):
    n = len(sums)
    has_small = small is not None

    def where():
        x, y, c = _me()
        peers = [(1 - x if k & 4 else x, 1 - y if k & 2 else y, 1 - c if k & 1 else c) for k in range(1, _NDEV)]
        return c, 4 * x + 2 * y + c, _other_chips(x, y), peers

    def send(rin, rout, sems, a, j, c, chip):
        return _rcopy(rin[a].at[2 * chip[0] + chip[1]], rout[a].at[j], sems[0].at[3 * a + j], sems[1].at[3 * a + j],
                      (chip[0], chip[1], c))

    def small_send(rin, rout, sems, k, dev, peer):
        return _rcopy(rin[n], rout[n].at[dev], sems[2].at[k], sems[3].at[k], peer)

    def start(rin, rout, sems):
        c, dev, chips, peers = where()
        for a in range(n):
            for j, chip in enumerate(chips):
                send(rin, rout, sems, a, j, c, chip).start()
        if has_small:
            pltpu.make_async_copy(rin[n], rout[n].at[dev], sems[4].at[0]).start()
            for k, peer in enumerate(peers):
                small_send(rin, rout, sems, k, dev, peer).start()

    def finish(rin, rout, sems):
        c, dev, chips, peers = where()
        for a in range(n):
            for j, chip in enumerate(chips):
                send(rin, rout, sems, a, j, c, chip).wait_recv()
        if has_small:
            for k, (px, py, pc_) in enumerate(peers):
                _rcopy(rin[n], rout[n].at[4 * px + 2 * py + pc_], sems[2].at[k], sems[3].at[k], (px, py, pc_)).wait_recv()
        for a in range(n):
            for j, chip in enumerate(chips):
                send(rin, rout, sems, a, j, c, chip).wait_send()
        if has_small:
            pltpu.make_async_copy(rin[n], rout[n].at[dev], sems[4].at[0]).wait()
            for k, peer in enumerate(peers):
                small_send(rin, rout, sems, k, dev, peer).wait_send()

    dma = pltpu.SemaphoreType.DMA
    outs = [_sds((3,) + s.shape[1:], s.dtype) for s in sums]
    if has_small:
        outs.append(_sds((_NDEV,) + small.shape, small.dtype))
    return _Plan(list(sums) + ([small] if has_small else []), outs, {},
                 [dma((3 * max(n, 1),)), dma((3 * max(n, 1),)), dma((_NDEV - 1,)), dma((_NDEV - 1,)), dma((1,))],
                 start, finish)


def _shard_sum(sums, recv, sel):
    s, r, c2 = sums.shape
    tm = _tile(r, 256, 16)

    def body(sel_ref, own_ref, rc_ref, o_ref):
        rc = rc_ref[...]
        o_ref[...] = ((own_ref[...].astype(_F32) + rc[0].astype(_F32)) + rc[1].astype(_F32)) + rc[2].astype(_F32)

    gs = pltpu.PrefetchScalarGridSpec(
        num_scalar_prefetch=1, grid=(r // tm,),
        in_specs=[pl.BlockSpec((None, tm, c2), lambda i, q: (q[0], i, 0)),
                  pl.BlockSpec((3, tm, c2), lambda i, q: (0, i, 0))],
        out_specs=pl.BlockSpec((None, tm, c2), lambda i, q: (q[1], i, 0)))
    return _call(body, name="shard_sum", grid_spec=gs, out_shape=_sds((2, r, c2), _F32),
                 compiler_params=_cparams("parallel"))(sel, sums, recv)


def _sum_slots(stack, name):
    k, r, c = stack.shape
    tm = _tile(r, 256, 16 if stack.dtype == _BF else 8)

    def fn(v):
        out = v[0].astype(_F32)
        for i in range(1, k):
            out = out + v[i].astype(_F32)
        return (out,)

    return _rowcall(fn, [stack], [pl.BlockSpec((k, tm, c), lambda j, i: (0, i, 0))], [_sds((r, c), _F32)],
                    [_rb(tm, c)], [False], grid=(1, r // tm), name=name)[0]


def _share_halves(bufs):
    n = len(bufs)

    def body(*refs):
        outs = refs[n:2 * n]
        ssem, rsem = refs[2 * n:]
        x, y, c = _me()
        sib = (x, y, 1 - c)
        cps = []
        for a in range(n):
            cp = _rcopy(outs[a].at[c], outs[a].at[c], ssem.at[a], rsem.at[a], sib)
            cp.start()
            cps.append(cp)
        for a in range(n):
            _rcopy(outs[a].at[c], outs[a].at[1 - c], ssem.at[a], rsem.at[a], sib).wait_recv()
        for cp in cps:
            cp.wait_send()

    dma = pltpu.SemaphoreType.DMA
    return _call(body, name="share_halves", in_specs=[_ANY] * n, out_specs=[_ANY] * n,
                 out_shape=[_sds(b.shape, b.dtype) for b in bufs], input_output_aliases={a: a for a in range(n)},
                 scratch_shapes=[dma((n,)), dma((n,))])(*bufs)


def _adamw_math(w, g, m, v):
    m = _B1 * m + (1.0 - _B1) * g
    v = _B2 * v + (1.0 - _B2) * jnp.square(g)
    m_hat = m / (1.0 - _B1 ** _STEP)
    v_hat = v / (1.0 - _B2 ** _STEP)
    delta = -_LR * (m_hat / (jnp.sqrt(v_hat) + _ADAM_EPS) + _WD * w)
    return delta, m, v


def _adamw_shard(w, g2, m, v, name):
    r, c = w.shape
    c2 = c // 2
    tm = _tile(r, 256, 8)
    blk = pl.BlockSpec((tm, c2), lambda h, i: (i, h))

    def fn(wb, gb, mb, vb):
        return (gb,) + _adamw_math(wb, gb, mb, vb)

    return _rowcall(fn, [w, g2, m, v], [blk, pl.BlockSpec((None, tm, c2), lambda h, i: (h, i, 0)), blk, blk],
                    [_sds((r, c), _F32)] * 4, [blk] * 4, [False] * 4, grid=(2, r // tm), name=name)


def _adamw_whole(w, g, m, v, name):
    r, c = w.shape
    blk = _full((r, c))
    return _rowcall(lambda *a: _adamw_math(*a), [w, g, m, v], [blk] * 4, [_sds((r, c), _F32)] * 3, [blk] * 3,
                    [False] * 3, grid=(1, 1), name=name)


def _pack(arrs):
    parts = []
    for a in arrs:
        f = a.reshape(-1).astype(_F32)
        pad = (-f.shape[0]) % (8 * _LANE)
        if pad:
            f = jnp.concatenate([f, jnp.zeros((pad,), _F32)])
        parts.append(f)
    return jnp.concatenate(parts).reshape(-1, _LANE)


def _unpack(buf, like):
    flat = buf.reshape(-1)
    outs, off = [], 0
    for a in like:
        nel = a.size
        outs.append(flat[off:off + nel].reshape(a.shape))
        off += nel + ((-nel) % (8 * _LANE))
    return outs


def kernel(x, ln_mix_g, w_in, s5_a_re, s5_a_im, s5_log_dt, s5_b_re, s5_b_im, s5_c_re, s5_c_im, s5_d, s5_w_glu, s5_b_glu, w_proj_s5, hgrn_lb_logits, hgrn_norm_g, w_proj_hgrn, w_out, ln_ffn_g, w_up, conv_w, conv_b, w_down, ln_final_g, loss_target, m_ln_mix_g, m_w_in, m_s5_a_re, m_s5_a_im, m_s5_log_dt, m_s5_b_re, m_s5_b_im, m_s5_c_re, m_s5_c_im, m_s5_d, m_s5_w_glu, m_s5_b_glu, m_w_proj_s5, m_hgrn_lb_logits, m_hgrn_norm_g, m_w_proj_hgrn, m_w_out, m_ln_ffn_g, m_w_up, m_conv_w, m_conv_b, m_w_down, m_ln_final_g, v_ln_mix_g, v_w_in, v_s5_a_re, v_s5_a_im, v_s5_log_dt, v_s5_b_re, v_s5_b_im, v_s5_c_re, v_s5_c_im, v_s5_d, v_s5_w_glu, v_s5_b_glu, v_w_proj_s5, v_hgrn_lb_logits, v_hgrn_norm_g, v_w_proj_hgrn, v_w_out, v_ln_ffn_g, v_w_up, v_conv_w, v_conv_b, v_w_down, v_ln_final_g):
    assert x.shape[0] == 1 and w_in.shape[0] == 1, "one example per device, one layer"
    t, d = x.shape[1], x.shape[2]
    w5 = s5_w_glu.shape[2]
    hw = hgrn_norm_g.shape[1]
    ng_, np_, gc = s5_b_re.shape[1], s5_b_re.shape[2], s5_b_re.shape[3]
    dff = w_down.shape[1] * _NCHIP
    assert gc * _S5_SET == _LANE and ng_ * gc == w5 and hw % _LANE == 0
    gs_off = w5 + 4 * hw
    ci = lax.axis_index("c")
    xt = x.reshape(t, d)
    tgt = loss_target.reshape(t, d)

    big_names = ["w_in", "s5_w_glu", "w_proj_s5", "w_proj_hgrn", "w_out", "w_up", "w_down"]
    big_w = dict(w_in=w_in[0], s5_w_glu=s5_w_glu[0], w_proj_s5=w_proj_s5[0], w_proj_hgrn=w_proj_hgrn[0],
                 w_out=w_out[0], w_up=w_up[0], w_down=w_down[0])
    chip = 2 * lax.axis_index("x") + lax.axis_index("y")
    sel_chip = jnp.stack([chip, ci]).astype(jnp.int32)
    slots = {k: _cast_into_slot(big_w[k], sel_chip) for k in big_names}
    g_in, g_cw = _run_plan(_gather_plan([slots["w_in"]], [conv_w[0]]), "gather_w_in")
    cw = g_cw.transpose(1, 0, 2).reshape(3, 2 * dff)
    cb = conv_b

    tm_big = _tile(t, 1024, 8)

    h1 = _rms_fwd(xt, ln_mix_g, "rms1_fwd")
    nin_s = g_in.shape[2]
    proj, (g_glu, g_ps5, g_ph, g_out) = _mm(
        h1, g_in, "nn", _F32, tm=tm_big, tn=_tile(nin_s, 1152), tk=d, name="mm_proj",
        rider=_gather_plan([slots[k] for k in ("s5_w_glu", "w_proj_s5", "w_proj_hgrn", "w_out")], []))
    wglu = g_glu.reshape(w5, w5)
    wout = g_out.reshape(d, d)

    abar_r, abar_i, coef_r, coef_i = _s5_prep(s5_a_re[0], s5_a_im[0], s5_log_dt.reshape(ng_, 1))
    lanes = ng_ * np_
    par = (abar_r.reshape(1, lanes), abar_i.reshape(1, lanes), coef_r.reshape(1, lanes), coef_i.reshape(1, lanes),
           _bd_in(s5_b_re[0]), _bd_in(s5_b_im[0]), _bd_out(s5_c_re[0]), -_bd_out(s5_c_im[0]), s5_d.reshape(1, w5))
    (y_s5, car, cai), (g_up,) = _s5_fwd(proj, par, t, w5, rider=_gather_plan([slots["w_up"]], []))
    z = _s5glu_fwd(y_s5, wglu, s5_b_glu)
    ys = _mm(z, g_ps5, "nn", _F32, tm=tm_big, tn=g_ps5.shape[2], tk=w5, name="mm_proj_s5")

    lb = _lb_prep(hgrn_lb_logits)
    og, st_all = _hgrn_fwd(proj, lb, hgrn_norm_g, t, w5, hw)
    yh = _mm(og, g_ph, "nn", _F32, tm=tm_big, tn=g_ph.shape[2], tk=hw, name="mm_proj_hgrn")

    merged = _merge_fwd(proj, ys, yh, t, d, gs_off)
    x2 = _mm(merged, wout, "nn", _F32, tm=tm_big, tn=_tile(d, 1024), tk=d, res=xt, name="mm_out")
    h2 = _rms_fwd(x2, ln_ffn_g, "rms2_fwd")
    up_s = g_up.shape[2]
    w_conv = _tile(dff, 1408)
    nbh = dff // w_conv
    tn_up = _tile(up_s, 1408)
    tk_dh2 = _tile(up_s, w_conv)
    tn_gwup = _tile(up_s // 2, 1408)
    assert w_conv % tn_up == 0 and w_conv % tk_dh2 == 0 and w_conv % tn_gwup == 0
    up, (g_down,) = _mm(h2, g_up, "nn", _F32, tm=tm_big, tn=tn_up, tk=d, name="mm_up",
                        colperm=_pair_perm(nbh, w_conv // tn_up), rider=_gather_plan([slots["w_down"]], []))
    wdown = g_down.reshape(dff, d)
    act = _convact_fwd(up, cw, cb, t, dff)
    x3 = _mm(act, wdown, "nn", _F32, tm=tm_big, tn=_tile(d, 1024), tk=None, res=x2, name="mm_down")
    loss_part, dx3, dx3b, d_gfin = _loss_head(x3, tgt, ln_final_g.reshape(1, d))

    dact = _mm(dx3b, wdown, "nt", _F32, tm=tm_big, tn=_tile(dff, 1408), tk=d, name="mm_dact")
    r_down = dff // _NCHIP
    gw_down = _mm(act, dx3b, "tn", _BF, tm=_tile(r_down, 1408), tn=_tile(d // 2, 1024), tk=None, halves="rows",
                  name="mm_gw_down")
    def chip_sums(grads, name):
        theirs = _swap_halves(grads, name)
        return [_add_pairs(g, th, sel_chip) for g, th in zip(grads, theirs)]

    (s_down,) = chip_sums([gw_down], "swap_halves_d")
    dup, d_cw, d_cb = _convact_bwd(up, dact, cw, cb, t, dff)
    dh2, (r_down,) = _mm(dup, g_up, "nt", _F32, tm=tm_big, tn=_tile(d, 1024), tk=tk_dh2, name="mm_dh2",
                         colperm=_pair_perm(nbh, w_conv // tk_dh2), rider=_exchange_plan([s_down], None))
    gw_up = _mm(h2, dup, "tn", _BF, tm=_tile(d, 1024), tn=tn_gwup, tk=None, halves="cols", name="mm_gw_up",
                colperm=_pair_perm(nbh, w_conv // tn_gwup))
    dx2, dx2b, d_gffn = _rms_bwd(x2, ln_ffn_g, dh2, dx3, "rms2_bwd")
    dmerged = _mm(dx2b, wout, "nt", _F32, tm=tm_big, tn=_tile(d, 1024), tk=d, name="mm_dmerged")
    gw_out = _mm(merged, dx2b, "tn", _BF, tm=_tile(d // _NCHIP, 1024), tn=_tile(d // 2, 1024), tk=None, halves="rows",
                 name="mm_gw_out")
    dys, dyh, dgs, dgh = _merge_bwd(proj, ys, yh, dmerged, t, d, gs_off)
    ps_s = g_ps5.shape[2]
    dz = _mm(dys, g_ps5, "nt", _F32, tm=tm_big, tn=_tile(w5, 1024), tk=ps_s, name="mm_dz")
    gw_ps5 = _mm(z, dys, "tn", _BF, tm=_tile(w5, 1024), tn=ps_s // 2, tk=None, halves="cols", name="mm_gw_ps5")
    dog = _mm(dyh, g_ph, "nt", _F32, tm=tm_big, tn=_tile(hw, 1024), tk=ps_s, name="mm_dog")
    gw_ph = _mm(og, dyh, "tn", _BF, tm=_tile(hw, 1024), tn=ps_s // 2, tk=None, halves="cols", name="mm_gw_ph")
    dy_s5, gw_glu_full, d_bglu = _s5glu_bwd(y_s5, dz, wglu, s5_b_glu)
    r_glu = w5 // _NCHIP
    gw_glu = gw_glu_full.astype(_BF).reshape(_NCHIP, r_glu, 2, w5 // 2).transpose(2, 0, 1, 3)

    s_glu, s_ps5, s_ph, s_out, s_up = chip_sums([gw_glu, gw_ps5, gw_ph, gw_out, gw_up], "swap_halves_a")
    s5g, (r_glu_, r_ps5, r_ph, r_out, r_up) = _s5_bwd(
        proj, dy_s5, car, cai, par, t, w5, rider=_exchange_plan([s_glu, s_ps5, s_ph, s_out, s_up], None))
    du = s5g[0]
    dq, df, di, dg, d_lb, d_ng = _hgrn_bwd(proj, lb, hgrn_norm_g, st_all, dog, t, w5, hw)
    dproj = jnp.concatenate([du, dq, df, di, dg, dgs, dgh], axis=1)

    d_are, d_aim, d_ldt = _s5_prep_bwd(s5_a_re[0], s5_a_im[0], s5_log_dt.reshape(ng_, 1),
                                       [s5g[k].reshape(ng_, np_) for k in (1, 2, 3, 4)])
    d_bre = _bd_in_grad(s5g[5], np_, gc)
    d_bim = _bd_in_grad(s5g[6], np_, gc)
    d_cre = _bd_out_grad(s5g[7], np_, gc)
    d_cim = -_bd_out_grad(s5g[8], np_, gc)
    d_logits = _lb_prep_bwd(hgrn_lb_logits, d_lb)

    small_names = ["s5_a_re", "s5_a_im", "s5_log_dt", "s5_b_re", "s5_b_im", "s5_c_re", "s5_c_im", "s5_d",
                   "s5_b_glu", "hgrn_lb_logits", "hgrn_norm_g", "ln_ffn_g", "conv_b", "ln_final_g", "ln_mix_g"]
    small_w = dict(ln_mix_g=ln_mix_g, s5_a_re=s5_a_re, s5_a_im=s5_a_im, s5_log_dt=s5_log_dt, s5_b_re=s5_b_re,
                   s5_b_im=s5_b_im, s5_c_re=s5_c_re, s5_c_im=s5_c_im, s5_d=s5_d, s5_b_glu=s5_b_glu,
                   hgrn_lb_logits=hgrn_lb_logits, hgrn_norm_g=hgrn_norm_g, ln_ffn_g=ln_ffn_g, conv_b=conv_b,
                   ln_final_g=ln_final_g)
    small_m = dict(ln_mix_g=m_ln_mix_g, s5_a_re=m_s5_a_re, s5_a_im=m_s5_a_im, s5_log_dt=m_s5_log_dt, s5_b_re=m_s5_b_re,
                   s5_b_im=m_s5_b_im, s5_c_re=m_s5_c_re, s5_c_im=m_s5_c_im, s5_d=m_s5_d, s5_b_glu=m_s5_b_glu,
                   hgrn_lb_logits=m_hgrn_lb_logits, hgrn_norm_g=m_hgrn_norm_g, ln_ffn_g=m_ln_ffn_g, conv_b=m_conv_b,
                   ln_final_g=m_ln_final_g)
    small_v = dict(ln_mix_g=v_ln_mix_g, s5_a_re=v_s5_a_re, s5_a_im=v_s5_a_im, s5_log_dt=v_s5_log_dt, s5_b_re=v_s5_b_re,
                   s5_b_im=v_s5_b_im, s5_c_re=v_s5_c_re, s5_c_im=v_s5_c_im, s5_d=v_s5_d, s5_b_glu=v_s5_b_glu,
                   hgrn_lb_logits=v_hgrn_lb_logits, hgrn_norm_g=v_hgrn_norm_g, ln_ffn_g=v_ln_ffn_g, conv_b=v_conv_b,
                   ln_final_g=v_ln_final_g)
    small_g = dict(s5_a_re=d_are, s5_a_im=d_aim, s5_log_dt=d_ldt, s5_b_re=d_bre, s5_b_im=d_bim,
                   s5_c_re=d_cre, s5_c_im=d_cim, s5_d=s5g[9], s5_b_glu=d_bglu, hgrn_lb_logits=d_logits,
                   hgrn_norm_g=d_ng, ln_ffn_g=d_gffn, conv_b=d_cb, ln_final_g=d_gfin)
    like = [small_w[k] for k in small_names]
    assert small_names[-1] == "ln_mix_g"
    pack_a = _pack([small_g[k] for k in small_names[:-1]] + [d_cw])
    gw_in, (r_small_a,) = _mm(h1, dproj, "tn", _BF, tm=_tile(d, 1024), tn=_tile(nin_s // 2, 1152), tk=None, halves="cols",
                              name="mm_gw_in", rider=_exchange_plan([], pack_a))
    (s_in,) = chip_sums([gw_in], "swap_halves_b")
    dh1, (r_in,) = _mm(dproj, g_in, "nt", _F32, tm=tm_big, tn=_tile(d, 1024), tk=None, name="mm_dh1",
                       rider=_exchange_plan([s_in], None))
    dx, _, d_gmix = _rms_bwd(xt, ln_mix_g, dh1, dx2, "rms1_bwd")
    (r_small_b,) = _run_plan(_exchange_plan([], _pack([d_gmix])), "exchange_gmix")
    sums = [s_in, s_glu, s_ps5, s_ph, s_out, s_up, s_down]
    received = [r_in, r_glu_, r_ps5, r_ph, r_out, r_up, r_down]
    halves = [_shard_sum(sm, rc, sel_chip) for sm, rc in zip(sums, received)]
    g_a = _sum_slots(r_small_a, "small_sum_a")
    g_b = _sum_slots(r_small_b, "small_sum_b")
    full = _share_halves(halves)
    w_pack = _pack(like)
    rows_a = w_pack.shape[0] - g_b.shape[0]
    g_small = jnp.concatenate([g_a[:rows_a], g_b], axis=0)
    cs = conv_w.shape[2]
    g_cw_full = g_a[rows_a:].reshape(-1)[:3 * 2 * dff].reshape(3, 2 * dff)
    g_cw = lax.dynamic_slice_in_dim(g_cw_full, chip * cs, cs, axis=1)

    big_m = dict(w_in=m_w_in, s5_w_glu=m_s5_w_glu, w_proj_s5=m_w_proj_s5, w_proj_hgrn=m_w_proj_hgrn, w_out=m_w_out,
                 w_up=m_w_up, w_down=m_w_down)
    big_v = dict(w_in=v_w_in, s5_w_glu=v_s5_w_glu, w_proj_s5=v_w_proj_s5, w_proj_hgrn=v_w_proj_hgrn, w_out=v_w_out,
                 w_up=v_w_up, w_down=v_w_down)
    res = {}
    for k, g2 in zip(big_names, full):
        w2 = big_w[k]
        shp = (1,) + w2.shape
        outs = _adamw_shard(w2, g2, big_m[k][0], big_v[k][0], "adamw_" + k)
        res[k] = [o.reshape(shp) for o in outs]
    sm_outs = _adamw_whole(w_pack, g_small, _pack([small_m[k] for k in small_names]),
                           _pack([small_v[k] for k in small_names]), "adamw_small")
    sm_g = _unpack(g_small, like)
    sm_d, sm_m, sm_v = (_unpack(o, like) for o in sm_outs)
    for i, k in enumerate(small_names):
        res[k] = [sm_g[i], sm_d[i], sm_m[i], sm_v[i]]

    cw_outs = _adamw_whole(conv_w[0], g_cw, m_conv_w[0], v_conv_w[0], "adamw_conv_w")
    res["conv_w"] = [g_cw.reshape(conv_w.shape)] + [o.reshape(conv_w.shape) for o in cw_outs]

    loss = lax.psum(loss_part[0, 0], ("x", "y", "c"))
    order = ["ln_mix_g", "w_in", "s5_a_re", "s5_a_im", "s5_log_dt", "s5_b_re", "s5_b_im", "s5_c_re", "s5_c_im", "s5_d",
             "s5_w_glu", "s5_b_glu", "w_proj_s5", "hgrn_lb_logits", "hgrn_norm_g", "w_proj_hgrn", "w_out", "ln_ffn_g",
             "w_up", "conv_w", "conv_b", "w_down", "ln_final_g"]
    return (loss, dx.reshape(x.shape), *[res[k][0] for k in order], *[res[k][1] for k in order],
            *[res[k][2] for k in order], *[res[k][3] for k in order])
```

```python
import functools

import jax
import jax.numpy as jnp
from jax import lax
from jax.experimental import pallas as pl
from jax.experimental.pallas import tpu as pltpu

_F32 = jnp.float32
_BF = jnp.bfloat16
_RMS_EPS = 1e-6
_S5_MAX_RE = -1e-4
_LR, _B1, _B2, _ADAM_EPS, _WD, _STEP = 0.001, 0.9, 0.999, 1e-08, 0.01, 10
_MESH = pl.DeviceIdType.MESH
_ANY = pl.BlockSpec(memory_space=pl.ANY)
_LANE = 128
_VMEM_LIMIT = 56 * 1024 * 1024
_CHUNK = 64
_S5_TB = 128
_S5_SET = 8
_HGRN_HP = 8
_HGRN_SEG = 512
_NCHIP = 4
_NDEV = 8


def _call(body, **kw):
    return pl.pallas_call(body, **kw)


def _cparams(*sem):
    return pltpu.CompilerParams(dimension_semantics=sem, vmem_limit_bytes=_VMEM_LIMIT)


def _tile(n, pref, unit=_LANE):
    if n <= pref:
        return n
    t = (pref // unit) * unit
    while t >= unit:
        if n % t == 0:
            return t
        t -= unit
    raise ValueError(f"no tile for {n}")


_OPERAND_BYTES = 12 * 1024 * 1024


def _tk_fit(k, tm, tn):
    best = _LANE
    for tk in range(_LANE, k + 1, _LANE):
        if k % tk == 0 and (tm + tn) * tk * 2 <= _OPERAND_BYTES:
            best = tk
    return best if k % _LANE == 0 else k


_NN = ((1,), (0,))
_NT = ((1,), (1,))
_TN = ((0,), (0,))


def _dg(a, b, dims):
    return lax.dot_general(a.astype(_BF), b.astype(_BF), (dims, ((), ())), preferred_element_type=_F32)


@jax.custom_vjp
def _bdot(a, b):
    return _dg(a, b, _NN)


def _bdot_f(a, b):
    return _dg(a, b, _NN), (a, b)


def _bdot_b(res, g):
    a, b = res
    return _dg(g, b, _NT).astype(a.dtype), _dg(a, g, _TN).astype(b.dtype)


_bdot.defvjp(_bdot_f, _bdot_b)


@jax.custom_vjp
def _bdot_nt(a, b):
    return _dg(a, b, _NT)


def _bdot_nt_f(a, b):
    return _dg(a, b, _NT), (a, b)


def _bdot_nt_b(res, g):
    a, b = res
    return _dg(g, b, _NN).astype(a.dtype), _dg(g, a, _TN).astype(b.dtype)


_bdot_nt.defvjp(_bdot_nt_f, _bdot_nt_b)


@jax.custom_vjp
def _bdot_tn(a, b):
    return _dg(a, b, _TN)


def _bdot_tn_f(a, b):
    return _dg(a, b, _TN), (a, b)


def _bdot_tn_b(res, g):
    a, b = res
    return _dg(b, g, _NT).astype(a.dtype), _dg(a, g, _NN).astype(b.dtype)


_bdot_tn.defvjp(_bdot_tn_f, _bdot_tn_b)


_SUBLANES = 8


def _shift_up(x, n):
    r = x.shape[0]
    if n % _SUBLANES == 0:
        return jnp.concatenate([x[n:], jnp.zeros((n,) + x.shape[1:], x.dtype)], axis=0)
    row = lax.broadcasted_iota(jnp.int32, x.shape, 0)
    return jnp.where(row < r - n, pltpu.roll(x, r - n, 0), 0.0)


@functools.partial(jax.custom_vjp, nondiff_argnums=(1,))
def _shift_down(x, n):
    if n % _SUBLANES == 0:
        return jnp.concatenate([jnp.zeros((n,) + x.shape[1:], x.dtype), x[:x.shape[0] - n]], axis=0)
    row = lax.broadcasted_iota(jnp.int32, x.shape, 0)
    return jnp.where(row >= n, pltpu.roll(x, n, 0), 0.0)


def _shift_down_f(x, n):
    return _shift_down(x, n), None


def _shift_down_b(n, _, g):
    return (_shift_up(g, n),)


_shift_down.defvjp(_shift_down_f, _shift_down_b)


def _rows_apart(x):
    return tuple(x[k:k + _SUBLANES] for k in range(0, x.shape[0], _SUBLANES))


@jax.custom_vjp
def _split_rows(x):
    return _rows_apart(x)


_split_rows.defvjp(lambda x: (_rows_apart(x), None), lambda _, gs: (jnp.concatenate(gs, axis=0),))


@jax.custom_vjp
def _join_rows(pieces):
    return jnp.concatenate(pieces, axis=0)


_join_rows.defvjp(lambda pieces: (jnp.concatenate(pieces, axis=0), None), lambda _, g: (_rows_apart(g),))


@jax.custom_vjp
def _last_row(x):
    return x[_SUBLANES - 1:_SUBLANES]


def _last_row_b(_, g):
    row = lax.broadcasted_iota(jnp.int32, (_SUBLANES, g.shape[1]), 0)
    return (jnp.where(row == _SUBLANES - 1, g, 0.0),)


_last_row.defvjp(lambda x: (x[_SUBLANES - 1:_SUBLANES], None), _last_row_b)


def _sigmoid(x):
    return 1.0 / (1.0 + jnp.exp(-x))


def _silu(x):
    return x * _sigmoid(x)


def _gelu(x):
    return 0.5 * x * (1.0 + jnp.tanh(0.7978845608028654 * (x + 0.044715 * (x * x * x))))


def _rms_core(x, g):
    return x * lax.rsqrt(jnp.mean(x * x, axis=-1, keepdims=True) + _RMS_EPS) * g


def _mm(a, b, mode, out_dtype, *, tm, tn, tk, res=None, halves=None, rider=None, colperm=None, name):
    if colperm is None:
        def colperm(n_):
            return n_
    if tk is None:
        kdim = a.shape[0] if mode == "tn" else (b.shape[2] if (mode == "nt" and b.ndim == 3) else a.shape[1])
        tk = _tk_fit(kdim, tm, tn)
    if mode == "nn":
        m, k = a.shape
        a_spec = pl.BlockSpec((tm, tk), lambda i, j, kk: (i, kk))
        if b.ndim == 3:
            s, _, ns = b.shape
            n = s * ns
            npb = ns // tn
            b_spec = pl.BlockSpec((None, tk, tn), lambda i, j, kk: (j // npb, kk, j % npb))
        else:
            n = b.shape[1]
            b_spec = pl.BlockSpec((tk, tn), lambda i, j, kk: (kk, j))
        dims = _NN
    elif mode == "nt":
        m, k = a.shape
        a_spec = pl.BlockSpec((tm, tk), lambda i, j, kk: (i, colperm(kk)))
        if b.ndim == 3:
            s, n, ks = b.shape
            kpb = ks // tk
            b_spec = pl.BlockSpec((None, tn, tk), lambda i, j, kk: (kk // kpb, j, kk % kpb))
        else:
            n = b.shape[0]
            b_spec = pl.BlockSpec((tn, tk), lambda i, j, kk: (j, kk))
        dims = _NT
    else:
        k, m = a.shape
        n = b.shape[1]
        a_spec = pl.BlockSpec((tk, tm), lambda i, j, kk: (kk, i))
        b_spec = pl.BlockSpec((tk, tn), lambda i, j, kk: (kk, colperm(j)))
        dims = _TN
    nk = k // tk
    if halves is None:
        out_shape = jax.ShapeDtypeStruct((m, n), out_dtype)
        out_spec = pl.BlockSpec((tm, tn), lambda i, j, kk: (i, colperm(j) if mode == "nn" else j))
    elif halves == "cols":
        c2 = n // (2 * _NCHIP)
        tpc = c2 // tn
        out_shape = jax.ShapeDtypeStruct((2, _NCHIP, m, c2), out_dtype)
        out_spec = pl.BlockSpec((None, None, tm, tn),
                                lambda i, j, kk: ((j // tpc) % 2, j // (2 * tpc), i, j % tpc))
    else:
        c2 = n // 2
        tpc = c2 // tn
        r = m // _NCHIP
        tpr = r // tm
        out_shape = jax.ShapeDtypeStruct((2, _NCHIP, r, c2), out_dtype)
        out_spec = pl.BlockSpec((None, None, tm, tn),
                                lambda i, j, kk: (j // tpc, i // tpr, i % tpr, j % tpc))
    has_res = res is not None
    nreg = 3 if has_res else 2
    ni, nj = m // tm, n // tn
    r_ins = list(rider.ins) if rider else []
    r_outs = list(rider.outs) if rider else []

    def body(*refs):
        a_ref, b_ref = refs[0], refs[1]
        r_ref = refs[2] if has_res else None
        rin = refs[nreg:nreg + len(r_ins)]
        o_ref = refs[nreg + len(r_ins)]
        rout = refs[nreg + len(r_ins) + 1:nreg + len(r_ins) + 1 + len(r_outs)]
        acc_ref = refs[nreg + len(r_ins) + 1 + len(r_outs)]
        sems = refs[nreg + len(r_ins) + 2 + len(r_outs):]
        i, j, kk = pl.program_id(0), pl.program_id(1), pl.program_id(2)

        if rider:
            @pl.when(jnp.logical_and(jnp.logical_and(i == 0, j == 0), kk == 0))
            def _():
                rider.start(rin, rout, sems)

        @pl.when(kk == 0)
        def _():
            acc_ref[...] = jnp.zeros_like(acc_ref)

        acc_ref[...] += _dg(a_ref[...], b_ref[...], dims)

        @pl.when(kk == nk - 1)
        def _():
            out = acc_ref[...]
            if has_res:
                out = out + r_ref[...]
            o_ref[...] = out.astype(out_dtype)

        if rider:
            @pl.when(jnp.logical_and(jnp.logical_and(i == ni - 1, j == nj - 1), kk == nk - 1))
            def _():
                rider.finish(rin, rout, sems)

    in_specs = [a_spec, b_spec]
    args = [a, b]
    if has_res:
        in_specs.append(pl.BlockSpec((tm, tn), lambda i, j, kk: (i, j)))
        args.append(res)
    if not rider:
        return _call(body, name=name, grid=(ni, nj, nk), in_specs=in_specs, out_specs=out_spec,
                     out_shape=out_shape, scratch_shapes=[pltpu.VMEM((tm, tn), _F32)],
                     compiler_params=_cparams("parallel", "parallel", "arbitrary"))(*args)
    res_all = _call(body, name=name, grid=(ni, nj, nk), in_specs=in_specs + [_ANY] * len(r_ins),
                    out_specs=[out_spec] + [_ANY] * len(r_outs), out_shape=[out_shape] + r_outs,
                    input_output_aliases={nreg + k: 1 + v for k, v in rider.aliases.items()},
                    scratch_shapes=[pltpu.VMEM((tm, tn), _F32)] + list(rider.sems),
                    compiler_params=_cparams("arbitrary", "arbitrary", "arbitrary"))(*args, *r_ins)
    return res_all[0], list(res_all[1:])


def _rowcall(fn, ins, in_specs, outs, out_specs, acc, *, grid, name):
    nin = len(ins)

    def body(*refs):
        vals = fn(*[r[...] for r in refs[:nin]])
        first = pl.program_id(1) == 0
        for k, (o_ref, v) in enumerate(zip(refs[nin:], vals)):
            if acc[k]:
                @pl.when(first)
                def _(o_ref=o_ref):
                    o_ref[...] = jnp.zeros_like(o_ref)
                o_ref[...] += v.astype(o_ref.dtype)
            else:
                o_ref[...] = v.astype(o_ref.dtype)

    return _call(body, name=name, grid=grid, in_specs=in_specs, out_specs=out_specs, out_shape=outs,
                 compiler_params=_cparams("parallel", "arbitrary"))(*ins)


def _rb(tm, w, cb=0):
    return pl.BlockSpec((tm, w), lambda j, i: (i, cb + j))


def _cb(w, cb=0):
    return pl.BlockSpec((1, w), lambda j, i: (0, cb + j))


def _full(shape):
    nd = len(shape)
    return pl.BlockSpec(shape, lambda j, i: (0,) * nd)


def _sds(shape, dtype):
    return jax.ShapeDtypeStruct(shape, dtype)


def _rms_fwd(x, g, name):
    t, d = x.shape
    tm = _tile(t, 512, 8)
    return _rowcall(lambda xb, gb: (_rms_core(xb, gb),), [x, g], [_rb(tm, d), _full((1, d))],
                    [_sds((t, d), _BF)], [_rb(tm, d)], [False], grid=(1, t // tm), name=name)[0]


def _rms_bwd(x, g, dh, dres, name):
    t, d = x.shape
    tm = _tile(t, 256, 8)

    def fn(xb, gb, dhb, drb):
        _, vjp = jax.vjp(_rms_core, xb, gb)
        dx, dg = vjp(dhb.astype(_F32))
        dx = dx + drb
        return dx, dx, dg

    return _rowcall(fn, [x, g, dh, dres], [_rb(tm, d), _full((1, d)), _rb(tm, d), _rb(tm, d)],
                    [_sds((t, d), _F32), _sds((t, d), _BF), _sds((1, d), _F32)],
                    [_rb(tm, d), _rb(tm, d), _full((1, d))], [False, False, True],
                    grid=(1, t // tm), name=name)


def _loss_head(x3, tgt, g):
    t, d = x3.shape
    tm = _tile(t, 256, 8)

    def fn(xb, tb, gb):
        y, vjp = jax.vjp(_rms_core, xb, gb)
        e = y - tb
        part = 0.5 * jnp.sum(jnp.mean(e * e, axis=-1, keepdims=True), axis=0, keepdims=True)
        dx, dg = vjp(e * (1.0 / d))
        return jnp.broadcast_to(part, (1, _LANE)), dx, dx, dg

    return _rowcall(fn, [x3, tgt, g], [_rb(tm, d), _rb(tm, d), _full((1, d))],
                    [_sds((1, _LANE), _F32), _sds((t, d), _F32), _sds((t, d), _BF), _sds((1, d), _F32)],
                    [_full((1, _LANE)), _rb(tm, d), _rb(tm, d), _full((1, d))], [True, False, False, True],
                    grid=(1, t // tm), name="loss_head")


def _s5_disc(a_re, a_im, log_dt):
    lam_re = jnp.minimum(a_re, _S5_MAX_RE)
    lam_im = a_im
    dt = jnp.exp(log_dt)
    mag = jnp.exp(lam_re * dt)
    abar_re = mag * jnp.cos(lam_im * dt)
    abar_im = mag * jnp.sin(lam_im * dt)
    den = lam_re * lam_re + lam_im * lam_im
    nr = abar_re - 1.0
    ni = abar_im
    coef_re = (nr * lam_re + ni * lam_im) / den
    coef_im = (ni * lam_re - nr * lam_im) / den
    return abar_re, abar_im, coef_re, coef_im


def _s5_prep(a_re, a_im, log_dt):
    g, p = a_re.shape

    def body(ar, ai, ld, o0, o1, o2, o3):
        outs = _s5_disc(ar[...], ai[...], ld[...])
        for o, v in zip((o0, o1, o2, o3), outs):
            o[...] = v

    return _call(body, name="s5_prep", out_shape=[_sds((g, p), _F32)] * 4)(a_re, a_im, log_dt)


def _s5_prep_bwd(a_re, a_im, log_dt, cts):
    g, p = a_re.shape

    def body(ar, ai, ld, c0, c1, c2, c3, d0, d1, d2):
        _, vjp = jax.vjp(_s5_disc, ar[...], ai[...], ld[...])
        outs = vjp((c0[...], c1[...], c2[...], c3[...]))
        for o, v in zip((d0, d1, d2), outs):
            o[...] = v

    return _call(body, name="s5_prep_bwd",
                 out_shape=[_sds((g, p), _F32), _sds((g, p), _F32), _sds((g, 1), _F32)])(a_re, a_im, log_dt, *cts)


def _s5_block(u, car, cai, ar, ai, cr, ci, b_re, b_im, c_re, c_imn, dvec):
    bur = _bdot(u, b_re)
    bui = _bdot(u, b_im)
    shape8 = (_SUBLANES, ar.shape[1])
    pows = [(ar, ai)]
    for _ in range(2):
        pr, pi = pows[-1]
        pows.append((pr * pr - pi * pi, 2.0 * pr * pi))
    pows = [(jnp.broadcast_to(pr, shape8), jnp.broadcast_to(pi, shape8)) for pr, pi in pows]
    cr, ci = jnp.broadcast_to(cr, shape8), jnp.broadcast_to(ci, shape8)

    def scan8(xr, xi):
        for k, (pr, pi) in enumerate(pows):
            dr = _shift_down(xr, 1 << k)
            di = _shift_down(xi, 1 << k)
            xr, xi = xr + pr * dr - pi * di, xi + pr * di + pi * dr
        return xr, xi

    row8 = lax.broadcasted_iota(jnp.int32, (_SUBLANES, ar.shape[1]), 0)
    tr, ti = scan8(jnp.where(row8 == 0, ar, 0.0), jnp.where(row8 == 0, ai, 0.0))
    outs_r, outs_i = [], []
    for xr, xi in zip(_split_rows(bur), _split_rows(bui)):
        xr, xi = scan8(cr * xr - ci * xi, cr * xi + ci * xr)
        xr, xi = xr + tr * car - ti * cai, xi + tr * cai + ti * car
        car, cai = _last_row(xr), _last_row(xi)
        outs_r.append(xr)
        outs_i.append(xi)
    y = _bdot(_join_rows(tuple(outs_r)), c_re) + _bdot(_join_rows(tuple(outs_i)), c_imn) + dvec * u
    return y, car, cai


def _s5_specs(tb, lw, nt, rev):
    tmap = (lambda t: nt - 1 - t) if rev else (lambda t: t)
    vec = pl.BlockSpec((1, lw), lambda s, t: (0, s))
    return dict(
        u=pl.BlockSpec((tb, _LANE), lambda s, t: (tmap(t), s)),
        car=pl.BlockSpec((None, 1, lw), lambda s, t: (tmap(t), 0, s)),
        vec=vec,
        bmat=pl.BlockSpec((None, _LANE, lw), lambda s, t: (s, 0, 0)),
        cmat=pl.BlockSpec((None, lw, _LANE), lambda s, t: (s, 0, 0)),
        dvec=pl.BlockSpec((1, _LANE), lambda s, t: (0, s)),
    )


def _s5_fwd(proj, par, t, w5, rider=None):
    ar, ai, cr, ci, b_re, b_im, c_re, c_imn, dvec = par
    ns = w5 // _LANE
    lw = ar.shape[1] // ns
    tb = min(_S5_TB, t)
    nt = t // tb
    sp = _s5_specs(tb, lw, nt, False)

    def body(u_ref, ar_r, ai_r, cr_r, ci_r, bre_r, bim_r, cre_r, cim_r, d_r, y_ref, car_ref, cai_ref, s_r, s_i):
        @pl.when(pl.program_id(1) == 0)
        def _():
            s_r[...] = jnp.zeros_like(s_r)
            s_i[...] = jnp.zeros_like(s_i)

        car_ref[...] = s_r[...]
        cai_ref[...] = s_i[...]
        y, ncr, nci = _s5_block(u_ref[...], s_r[...], s_i[...], ar_r[...], ai_r[...], cr_r[...], ci_r[...],
                                bre_r[...], bim_r[...], cre_r[...], cim_r[...], d_r[...])
        y_ref[...] = y
        s_r[...] = ncr
        s_i[...] = nci

    kw = dict(name="s5_fwd", grid=(ns, nt),
              in_specs=[sp["u"], sp["vec"], sp["vec"], sp["vec"], sp["vec"], sp["bmat"], sp["bmat"],
                        sp["cmat"], sp["cmat"], sp["dvec"]],
              out_specs=[sp["u"], sp["car"], sp["car"]],
              out_shape=[_sds((t, w5), _F32), _sds((nt, 1, ns * lw), _F32), _sds((nt, 1, ns * lw), _F32)],
              scratch_shapes=[pltpu.VMEM((1, lw), _F32), pltpu.VMEM((1, lw), _F32)])
    args = (proj, ar, ai, cr, ci, b_re, b_im, c_re, c_imn, dvec)
    if rider is None:
        return _call(body, compiler_params=_cparams("parallel", "arbitrary"), **kw)(*args), []
    return _call_riding(body, rider, args=args, **kw)


def _s5_bwd(proj, dy, car, cai, par, t, w5, rider=None):
    ar, ai, cr, ci, b_re, b_im, c_re, c_imn, dvec = par
    ns = w5 // _LANE
    lw = ar.shape[1] // ns
    tb = min(_S5_TB, t)
    nt = t // tb
    sp = _s5_specs(tb, lw, nt, True)

    def body(u_ref, dy_ref, car_ref, cai_ref, ar_r, ai_r, cr_r, ci_r, bre_r, bim_r, cre_r, cim_r, d_r,
             du_ref, g_ar, g_ai, g_cr, g_ci, g_bre, g_bim, g_cre, g_cim, g_d, ds_r, ds_i):
        accs = (g_ar, g_ai, g_cr, g_ci, g_bre, g_bim, g_cre, g_cim, g_d)

        @pl.when(pl.program_id(1) == 0)
        def _():
            ds_r[...] = jnp.zeros_like(ds_r)
            ds_i[...] = jnp.zeros_like(ds_i)
            for o in accs:
                o[...] = jnp.zeros_like(o)

        _, vjp = jax.vjp(_s5_block, u_ref[...], car_ref[...], cai_ref[...], ar_r[...], ai_r[...], cr_r[...],
                         ci_r[...], bre_r[...], bim_r[...], cre_r[...], cim_r[...], d_r[...])
        grads = vjp((dy_ref[...], ds_r[...], ds_i[...]))
        du_ref[...] = grads[0].astype(_BF)
        ds_r[...] = grads[1]
        ds_i[...] = grads[2]
        for o, gval in zip(accs, grads[3:]):
            o[...] += gval

    vec_o = _sds((1, ns * lw), _F32)
    kw = dict(name="s5_bwd", grid=(ns, nt),
              in_specs=[sp["u"], sp["u"], sp["car"], sp["car"], sp["vec"], sp["vec"], sp["vec"], sp["vec"],
                        sp["bmat"], sp["bmat"], sp["cmat"], sp["cmat"], sp["dvec"]],
              out_specs=[sp["u"], sp["vec"], sp["vec"], sp["vec"], sp["vec"], sp["bmat"], sp["bmat"],
                         sp["cmat"], sp["cmat"], sp["dvec"]],
              out_shape=[_sds((t, w5), _BF), vec_o, vec_o, vec_o, vec_o,
                         _sds(b_re.shape, _F32), _sds(b_re.shape, _F32), _sds(c_re.shape, _F32),
                         _sds(c_re.shape, _F32), _sds((1, w5), _F32)],
              scratch_shapes=[pltpu.VMEM((1, lw), _F32), pltpu.VMEM((1, lw), _F32)])
    args = (proj, dy, car, cai, ar, ai, cr, ci, b_re, b_im, c_re, c_imn, dvec)
    if rider is None:
        return _call(body, compiler_params=_cparams("parallel", "arbitrary"), **kw)(*args), []
    return _call_riding(body, rider, args=args, **kw)


def _bd_in(b):
    g, p, c = b.shape
    s = g // _S5_SET
    b4 = b.reshape(s, _S5_SET, p, c).transpose(0, 1, 3, 2)
    eye = jnp.eye(_S5_SET, dtype=b.dtype)
    return (b4[:, :, :, None, :] * eye[None, :, None, :, None]).reshape(s, _S5_SET * c, _S5_SET * p)


def _bd_in_grad(d, p, c):
    s = d.shape[0]
    eye = jnp.eye(_S5_SET, dtype=d.dtype)
    d5 = d.reshape(s, _S5_SET, c, _S5_SET, p) * eye[None, :, None, :, None]
    return d5.sum(axis=3).transpose(0, 1, 3, 2).reshape(s * _S5_SET, p, c)


def _bd_out(cm):
    g, c, p = cm.shape
    s = g // _S5_SET
    c4 = cm.reshape(s, _S5_SET, c, p).transpose(0, 1, 3, 2)
    eye = jnp.eye(_S5_SET, dtype=cm.dtype)
    return (c4[:, :, :, None, :] * eye[None, :, None, :, None]).reshape(s, _S5_SET * p, _S5_SET * c)


def _bd_out_grad(d, p, c):
    s = d.shape[0]
    eye = jnp.eye(_S5_SET, dtype=d.dtype)
    d5 = d.reshape(s, _S5_SET, p, _S5_SET, c) * eye[None, :, None, :, None]
    return d5.sum(axis=3).transpose(0, 1, 3, 2).reshape(s * _S5_SET, c, p)


def _s5glu_fwd(y, wglu, bglu):
    t, w5 = y.shape
    tm = _tile(t, 256, 8)

    def fn(yb, wb, bb):
        z1 = _gelu(yb)
        a = _dg(z1, wb, _NN) + bb
        return (z1 * _sigmoid(a),)

    return _rowcall(fn, [y, wglu, bglu], [_rb(tm, w5), _full(wglu.shape), _full((1, w5))],
                    [_sds((t, w5), _BF)], [_rb(tm, w5)], [False], grid=(1, t // tm), name="s5glu_fwd")[0]


def _s5glu_bwd(y, dz, wglu, bglu):
    t, w5 = y.shape
    tm = _tile(t, 256, 8)

    def fn(yb, dzb, wb, bb):
        dzb = dzb.astype(_F32)
        z1, gelu_vjp = jax.vjp(_gelu, yb)
        sig = _sigmoid(_dg(z1, wb, _NN) + bb)
        da = dzb * z1 * sig * (1.0 - sig)
        dz1 = dzb * sig + _dg(da, wb, _NT)
        (dy,) = gelu_vjp(dz1)
        return dy, _dg(z1, da, _TN), jnp.sum(da, axis=0, keepdims=True)

    return _rowcall(fn, [y, dz, wglu, bglu], [_rb(tm, w5), _rb(tm, w5), _full(wglu.shape), _full((1, w5))],
                    [_sds((t, w5), _F32), _sds((w5, w5), _F32), _sds((1, w5), _F32)],
                    [_rb(tm, w5), _full((w5, w5)), _full((1, w5))], [False, True, True],
                    grid=(1, t // tm), name="s5glu_bwd")


_LEVELS = (6, 5, 4, 3, 2, 1)


def _tri_stack(c):
    t = jnp.arange(c, dtype=jnp.int32)[:, None]
    j = jnp.arange(c, dtype=jnp.int32)[None, :]
    low = (j <= t).astype(_F32)
    mats = [low]
    for sh in _LEVELS:
        r = ((t >> sh) << sh) + ((1 << (sh - 1)) - 1)
        mats.append(low - (j <= r).astype(_F32))
    stack = jnp.concatenate(mats, axis=0).astype(_BF)
    return stack, stack.T


def _split_dot(mat, x):
    l = x.shape[1]
    hi = x.astype(_BF)
    lo = (x - hi.astype(_F32)).astype(_BF)
    out = jnp.dot(mat, jnp.concatenate([hi, lo], axis=1), preferred_element_type=_F32)
    return out[:, :l] + out[:, l:]


@jax.custom_vjp
def _decay_sums(lf, tri, tri_t):
    c = lf.shape[0]
    out = _split_dot(tri, lf)
    return tuple(out[k * c:(k + 1) * c] for k in range(len(_LEVELS) + 1))


def _decay_sums_f(lf, tri, tri_t):
    return _decay_sums(lf, tri, tri_t), (tri, tri_t)


def _decay_sums_b(res, gs):
    tri, tri_t = res
    return _split_dot(tri_t, jnp.concatenate(gs, axis=0)), jnp.zeros_like(tri), jnp.zeros_like(tri_t)


_decay_sums.defvjp(_decay_sums_f, _decay_sums_b)


def _hgrn_chunk(qi, fi, vi, gi, st, lb, ng, tri, tri_t):
    c = qi.shape[0]
    row = lax.broadcasted_iota(jnp.int32, (c, 1), 0)
    q = _silu(qi)
    lf = jnp.log(lb + (1.0 - lb) * _sigmoid(fi))
    k = (1.0 - lb) * _sigmoid(-fi)
    sums = _decay_sums(lf, tri, tri_t)
    b = sums[0]
    btot = jnp.sum(lf, axis=0, keepdims=True)
    inter = _bdot_nt(q * jnp.exp(b), st)
    tt = lax.broadcasted_iota(jnp.int32, (c, c), 0)
    ss = lax.broadcasted_iota(jnp.int32, (c, c), 1)
    sc = jnp.where(tt == ss, jnp.sum(q * k, axis=1, keepdims=True), 0.0)
    for sh, p in zip(_LEVELS, sums[1:]):
        upper = ((row >> (sh - 1)) & 1) == 1
        qm = jnp.where(upper, q * jnp.exp(jnp.where(upper, p, 0.0)), 0.0)
        km = jnp.where(upper, 0.0, k * jnp.exp(jnp.where(upper, 0.0, -p)))
        sc = sc + jnp.where((tt >> sh) == (ss >> sh), _bdot_nt(qm, km), 0.0)
    o = inter + _bdot(sc, vi)
    st_new = st * jnp.exp(btot) + _bdot_tn(vi, k * jnp.exp(btot - b))
    on = o * lax.rsqrt(jnp.mean(o * o, axis=1, keepdims=True) + _RMS_EPS) * ng
    return on * _silu(gi), st_new


def _hgrn_geom(t, w5, hw):
    hp = _HGRN_HP if (hw // _LANE) % _HGRN_HP == 0 and w5 % (_LANE * _HGRN_HP) == 0 else 1
    seg = min(t, _HGRN_SEG)
    return hp, hp * _LANE, seg, t // seg


def _hgrn_in_specs(seg, wd, w5, hw, tmap):
    nhp = hw // wd
    qb = w5 // wd
    return [pl.BlockSpec((seg, wd), (lambda h, s, k=k: (tmap(s), qb + k * nhp + h))) for k in range(4)]


def _hgrn_fwd(proj, lb, ng, t, w5, hw):
    assert _CHUNK == 64
    hp, wd, seg, nseg = _hgrn_geom(t, w5, hw)
    ncs = seg // _CHUNK
    vec = pl.BlockSpec((1, wd), lambda h, s: (0, h))

    tri, tri_t = _tri_stack(_CHUNK)

    def body(q_ref, f_ref, i_ref, g_ref, lb_ref, ng_ref, tri_ref, trit_ref, og_ref, st_ref, s_scr):
        @pl.when(pl.program_id(1) == 0)
        def _():
            s_scr[...] = jnp.zeros_like(s_scr)

        tri_v, trit_v = tri_ref[...], trit_ref[...]

        def step(ci, carry):
            r = pl.ds(pl.multiple_of(ci * _CHUNK, _CHUNK), _CHUNK)
            for hh in range(hp):
                ln = slice(hh * _LANE, (hh + 1) * _LANE)
                st_ref[hh, ci] = s_scr[hh]
                og, sn = _hgrn_chunk(q_ref[r, ln], f_ref[r, ln], i_ref[r, ln], g_ref[r, ln], s_scr[hh],
                                     lb_ref[:, ln], ng_ref[:, ln], tri_v, trit_v)
                og_ref[r, ln] = og.astype(_BF)
                s_scr[hh] = sn
            return carry

        lax.fori_loop(0, ncs, step, 0)

    return _call(body, name="hgrn_fwd", grid=(hw // wd, nseg),
                 in_specs=_hgrn_in_specs(seg, wd, w5, hw, lambda s: s) + [
                     vec, vec, pl.BlockSpec(tri.shape, lambda h, s: (0, 0)), pl.BlockSpec(tri_t.shape, lambda h, s: (0, 0))],
                 out_specs=[pl.BlockSpec((seg, wd), lambda h, s: (s, h)),
                            pl.BlockSpec((hp, ncs, _LANE, _LANE), lambda h, s: (h, s, 0, 0))],
                 out_shape=[_sds((t, hw), _BF), _sds((hw // _LANE, t // _CHUNK, _LANE, _LANE), _F32)],
                 scratch_shapes=[pltpu.VMEM((hp, _LANE, _LANE), _F32)],
                 compiler_params=_cparams("parallel", "arbitrary"))(proj, proj, proj, proj, lb, ng, tri, tri_t)


def _hgrn_bwd(proj, lb, ng, st_all, dog, t, w5, hw):
    hp, wd, seg, nseg = _hgrn_geom(t, w5, hw)
    ncs = seg // _CHUNK

    def rev(s):
        return nseg - 1 - s

    vec = pl.BlockSpec((1, wd), lambda h, s: (0, h))
    col = pl.BlockSpec((seg, wd), lambda h, s: (rev(s), h))

    tri, tri_t = _tri_stack(_CHUNK)

    def body(q_ref, f_ref, i_ref, g_ref, lb_ref, ng_ref, tri_ref, trit_ref, st_ref, dog_ref,
             dq_ref, df_ref, di_ref, dg_ref, dlb_ref, dng_ref, ds_scr):
        @pl.when(pl.program_id(1) == 0)
        def _():
            ds_scr[...] = jnp.zeros_like(ds_scr)
            dlb_ref[...] = jnp.zeros_like(dlb_ref)
            dng_ref[...] = jnp.zeros_like(dng_ref)

        tri_v, trit_v = tri_ref[...], trit_ref[...]

        def step(kk, carry):
            ci = ncs - 1 - kk
            r = pl.ds(pl.multiple_of(ci * _CHUNK, _CHUNK), _CHUNK)
            for hh in range(hp):
                ln = slice(hh * _LANE, (hh + 1) * _LANE)
                _, vjp = jax.vjp(_hgrn_chunk, q_ref[r, ln], f_ref[r, ln], i_ref[r, ln], g_ref[r, ln], st_ref[hh, ci],
                                 lb_ref[:, ln], ng_ref[:, ln], tri_v, trit_v)
                dq, df, di, dg, ds, dlb, dng = vjp((dog_ref[r, ln].astype(_F32), ds_scr[hh]))[:7]
                dq_ref[r, ln] = dq.astype(_BF)
                df_ref[r, ln] = df.astype(_BF)
                di_ref[r, ln] = di.astype(_BF)
                dg_ref[r, ln] = dg.astype(_BF)
                ds_scr[hh] = ds
                dlb_ref[:, ln] += dlb
                dng_ref[:, ln] += dng
            return carry

        lax.fori_loop(0, ncs, step, 0)

    return _call(body, name="hgrn_bwd", grid=(hw // wd, nseg),
                 in_specs=_hgrn_in_specs(seg, wd, w5, hw, rev) + [
                     vec, vec, pl.BlockSpec(tri.shape, lambda h, s: (0, 0)), pl.BlockSpec(tri_t.shape, lambda h, s: (0, 0)),
                     pl.BlockSpec((hp, ncs, _LANE, _LANE), lambda h, s: (h, rev(s), 0, 0)), col],
                 out_specs=[col, col, col, col, vec, vec],
                 out_shape=[_sds((t, hw), _BF)] * 4 + [_sds((1, hw), _F32)] * 2,
                 scratch_shapes=[pltpu.VMEM((hp, _LANE, _LANE), _F32)],
                 compiler_params=_cparams("parallel", "arbitrary"))(
                     proj, proj, proj, proj, lb, ng, tri, tri_t, st_all, dog)


def _lb_of(logits):
    mx = jnp.max(logits, axis=0, keepdims=True)
    e = jnp.exp(logits - mx)
    sm = e / jnp.sum(e, axis=0, keepdims=True)
    row = lax.broadcasted_iota(jnp.int32, logits.shape, 0)
    return jnp.sum(jnp.where(row == 0, sm, 0.0), axis=0, keepdims=True)


def _lb_prep(logits):
    def body(l_ref, o_ref):
        o_ref[...] = _lb_of(l_ref[...])

    return _call(body, name="lb_prep", out_shape=_sds((1, logits.shape[1]), _F32))(logits)


def _lb_prep_bwd(logits, dlb):
    def body(l_ref, d_ref, o_ref):
        _, vjp = jax.vjp(_lb_of, l_ref[...])
        o_ref[...] = vjp(d_ref[...])[0]

    return _call(body, name="lb_prep_bwd", out_shape=_sds(logits.shape, _F32))(logits, dlb)


def _merge_fwd(proj, ys, yh, t, d, gs_off):
    tm = _tile(t, 256, 8)
    w = _tile(d, 1024)
    nb = d // w

    def fn(gs, gh, a, b):
        return (_sigmoid(gs) * a + _sigmoid(gh) * b,)

    return _rowcall(fn, [proj, proj, ys, yh], [_rb(tm, w, gs_off // w), _rb(tm, w, gs_off // w + nb), _rb(tm, w), _rb(tm, w)],
                    [_sds((t, d), _BF)], [_rb(tm, w)], [False], grid=(nb, t // tm), name="merge_fwd")[0]


def _merge_bwd(proj, ys, yh, dm, t, d, gs_off):
    tm = _tile(t, 256, 8)
    w = _tile(d, 1024)
    nb = d // w

    def fn(gs, gh, a, b, g):
        g = g.astype(_F32)
        s1 = _sigmoid(gs)
        s2 = _sigmoid(gh)
        return g * s1, g * s2, g * a * s1 * (1.0 - s1), g * b * s2 * (1.0 - s2)

    return _rowcall(fn, [proj, proj, ys, yh, dm],
                    [_rb(tm, w, gs_off // w), _rb(tm, w, gs_off // w + nb), _rb(tm, w), _rb(tm, w), _rb(tm, w)],
                    [_sds((t, d), _BF)] * 4, [_rb(tm, w)] * 4, [False] * 4, grid=(nb, t // tm), name="merge_bwd")


def _prev_rows(up_prev8, is_first):
    p1 = jnp.where(is_first, 0.0, up_prev8[7:8, :])
    p2 = jnp.where(is_first, 0.0, up_prev8[6:7, :])
    return p1, p2


def _causal_taps(cur, p1, p2):
    row = lax.broadcasted_iota(jnp.int32, cur.shape, 0)
    s1 = jnp.where(row == 0, p1, _shift_down(cur, 1))
    s2 = jnp.where(row == 0, p2, jnp.where(row == 1, p1, _shift_down(cur, 2)))
    return s1, s2


def _pair_perm(nbh, per):
    def perm(n_):
        big = n_ // per
        return (2 * (big % nbh) + big // nbh) * per + n_ % per
    return perm


def _conv_specs(tm, w, nb_half, t):
    r8 = tm // 8
    cur_g = pl.BlockSpec((tm, w), lambda j, i: (i, 2 * j))
    cur_v = pl.BlockSpec((tm, w), lambda j, i: (i, 2 * j + 1))
    prev_g = pl.BlockSpec((8, w), lambda j, i: (jnp.maximum(i * r8 - 1, 0), 2 * j))
    prev_v = pl.BlockSpec((8, w), lambda j, i: (jnp.maximum(i * r8 - 1, 0), 2 * j + 1))
    w_g = pl.BlockSpec((3, w), lambda j, i: (0, j))
    w_v = pl.BlockSpec((3, w), lambda j, i: (0, nb_half + j))
    b_g = pl.BlockSpec((1, w), lambda j, i: (0, j))
    b_v = pl.BlockSpec((1, w), lambda j, i: (0, nb_half + j))
    return cur_g, cur_v, prev_g, prev_v, w_g, w_v, b_g, b_v


def _conv_of(cur, prev8, wt, bias, is_first):
    p1, p2 = _prev_rows(prev8, is_first)
    s1, s2 = _causal_taps(cur, p1, p2)
    return bias + wt[0:1, :] * s2 + wt[1:2, :] * s1 + wt[2:3, :] * cur


def _convact_fwd(up, cw, cb, t, dff):
    tm = _tile(t, 512, 8)
    w = _tile(dff, 1408)
    nbh = dff // w
    sp = _conv_specs(tm, w, nbh, t)

    def body(ug, uv, pg, pv, wg, wv, bg, bv, o_ref):
        first = pl.program_id(1) == 0
        gate = _conv_of(ug[...], pg[...], wg[...], bg[...], first)
        val = _conv_of(uv[...], pv[...], wv[...], bv[...], first)
        o_ref[...] = (_silu(gate) * val).astype(_BF)

    return _call(body, name="convact_fwd", grid=(nbh, t // tm), in_specs=list(sp),
                 out_specs=pl.BlockSpec((tm, w), lambda j, i: (i, j)), out_shape=_sds((t, dff), _BF),
                 compiler_params=_cparams("parallel", "arbitrary"))(up, up, up, up, cw, cw, cb, cb)


def _convact_bwd(up, dact, cw, cb, t, dff):
    tm = _tile(t, 256, 8)
    w = _tile(dff, 1408)
    nbh = dff // w
    r8 = tm // 8
    nt = t // tm
    last8 = t // 8 - 1

    def triple(off):
        return [pl.BlockSpec((tm, w), lambda j, i: (i, 2 * j + off)),
                pl.BlockSpec((8, w), lambda j, i: (jnp.minimum((i + 1) * r8, last8), 2 * j + off)),
                pl.BlockSpec((8, w), lambda j, i: (jnp.maximum(i * r8 - 1, 0), 2 * j + off))]

    def body(ug, ugn, ugp, uv, uvn, uvp, wg_ref, wv_ref, bg_ref, bv_ref, da_ref, dan_ref, du_ref, dw_ref, db_ref):
        i = pl.program_id(1)
        first = i == 0
        is_last = i == nt - 1
        gate = _conv_of(jnp.concatenate([ug[...], ugn[...]], axis=0), ugp[...], wg_ref[...], bg_ref[...], first)
        val = _conv_of(jnp.concatenate([uv[...], uvn[...]], axis=0), uvp[...], wv_ref[...], bv_ref[...], first)
        da = jnp.concatenate([da_ref[...], dan_ref[...]], axis=0)
        row = lax.broadcasted_iota(jnp.int32, da.shape, 0)
        da = jnp.where(jnp.logical_and(row >= tm, is_last), 0.0, da)
        sg = _sigmoid(gate)
        halves = ((da * val * sg * (1.0 + gate * (1.0 - sg)), wg_ref, ug, ugp),
                  (da * gate * sg, wv_ref, uv, uvp))

        @pl.when(first)
        def _():
            dw_ref[...] = jnp.zeros_like(dw_ref)
            db_ref[...] = jnp.zeros_like(db_ref)

        for h, (dc, w_ref, u_ref, p_ref) in enumerate(halves):
            ln = slice(h * w, (h + 1) * w)
            wt = w_ref[...]
            du = wt[2:3, :] * dc + wt[1:2, :] * _shift_up(dc, 1) + wt[0:1, :] * _shift_up(dc, 2)
            du_ref[:, ln] = du[0:tm, :].astype(_BF)
            dcm = dc[0:tm, :]
            cur = u_ref[...]
            p1, p2 = _prev_rows(p_ref[...], first)
            s1, s2 = _causal_taps(cur, p1, p2)
            dw_ref[0:1, ln] += jnp.sum(dcm * s2, axis=0, keepdims=True)
            dw_ref[1:2, ln] += jnp.sum(dcm * s1, axis=0, keepdims=True)
            dw_ref[2:3, ln] += jnp.sum(dcm * cur, axis=0, keepdims=True)
            db_ref[:, ln] += jnp.sum(dcm, axis=0, keepdims=True)

    in_specs = (triple(0) + triple(1)
                + [pl.BlockSpec((3, w), lambda j, i: (0, j)), pl.BlockSpec((3, w), lambda j, i: (0, nbh + j)),
                   pl.BlockSpec((1, w), lambda j, i: (0, j)), pl.BlockSpec((1, w), lambda j, i: (0, nbh + j)),
                   pl.BlockSpec((tm, w), lambda j, i: (i, j)),
                   pl.BlockSpec((8, w), lambda j, i: (jnp.minimum((i + 1) * r8, last8), j))])
    dup, dw_p, db_p = _call(
        body, name="convact_bwd", grid=(nbh, nt), in_specs=in_specs,
        out_specs=[pl.BlockSpec((tm, 2 * w), lambda j, i: (i, j)), pl.BlockSpec((3, 2 * w), lambda j, i: (0, j)),
                   pl.BlockSpec((1, 2 * w), lambda j, i: (0, j))],
        out_shape=[_sds((t, 2 * dff), _BF), _sds((3, 2 * dff), _F32), _sds((1, 2 * dff), _F32)],
        compiler_params=_cparams("parallel", "arbitrary"))(up, up, up, up, up, up, cw, cw, cb, cb, dact, dact)

    def natural(v):
        k = v.shape[0]
        return v.reshape(k, nbh, 2, w).transpose(0, 2, 1, 3).reshape(k, 2 * dff)

    return dup, natural(dw_p), natural(db_p)


def _me():
    return lax.axis_index("x"), lax.axis_index("y"), lax.axis_index("c")


def _other_chips(x, y):
    return [(1 - x, y), (x, 1 - y), (1 - x, 1 - y)]


def _rcopy(src, dst, ssem, rsem, dev):
    return pltpu.make_async_remote_copy(src_ref=src, dst_ref=dst, send_sem=ssem, recv_sem=rsem,
                                        device_id=dev, device_id_type=_MESH)


def _cast_into_slot(w, sel):
    r, c = w.shape
    tm = _tile(r, 256, 16)

    def body(sel_ref, w_ref, o_ref):
        o_ref[...] = w_ref[...].astype(_BF)

    gs = pltpu.PrefetchScalarGridSpec(
        num_scalar_prefetch=1, grid=(r // tm,),
        in_specs=[pl.BlockSpec((tm, c), lambda i, s: (i, 0))],
        out_specs=pl.BlockSpec((None, tm, c), lambda i, s: (s[0], i, 0)))
    return _call(body, name="cast_into_slot", grid_spec=gs, out_shape=_sds((_NCHIP, r, c), _BF),
                 compiler_params=_cparams("parallel"))(sel, w)


class _Plan:
    def __init__(self, ins, outs, aliases, sems, start, finish):
        self.ins, self.outs, self.aliases, self.sems, self.start, self.finish = ins, outs, aliases, sems, start, finish


def _run_plan(plan, name):
    ni, no = len(plan.ins), len(plan.outs)

    def body(*refs):
        rin, rout, sems = refs[:ni], refs[ni:ni + no], refs[ni + no:]
        plan.start(rin, rout, sems)
        plan.finish(rin, rout, sems)

    return _call(body, name=name, in_specs=[_ANY] * ni, out_specs=[_ANY] * no, out_shape=list(plan.outs),
                 input_output_aliases=dict(plan.aliases), scratch_shapes=list(plan.sems))(*plan.ins)


def _call_riding(body, rider, *, name, grid, in_specs, out_specs, out_shape, scratch_shapes, args):
    n_in, n_out, n_scr = len(in_specs), len(out_specs), len(scratch_shapes)
    n_rin, n_rout = len(rider.ins), len(rider.outs)

    def wrapped(*refs):
        ins, rin = refs[:n_in], refs[n_in:n_in + n_rin]
        o0 = n_in + n_rin
        outs, rout = refs[o0:o0 + n_out], refs[o0 + n_out:o0 + n_out + n_rout]
        s0 = o0 + n_out + n_rout
        scratch, sems = refs[s0:s0 + n_scr], refs[s0 + n_scr:]
        first = functools.reduce(jnp.logical_and, [pl.program_id(k) == 0 for k in range(len(grid))])
        last = functools.reduce(jnp.logical_and, [pl.program_id(k) == grid[k] - 1 for k in range(len(grid))])

        @pl.when(first)
        def _():
            rider.start(rin, rout, sems)

        body(*ins, *outs, *scratch)

        @pl.when(last)
        def _():
            rider.finish(rin, rout, sems)

    res = _call(wrapped, name=name, grid=grid, in_specs=list(in_specs) + [_ANY] * n_rin,
                out_specs=list(out_specs) + [_ANY] * n_rout, out_shape=list(out_shape) + list(rider.outs),
                input_output_aliases={n_in + k: n_out + v for k, v in rider.aliases.items()},
                scratch_shapes=list(scratch_shapes) + list(rider.sems),
                compiler_params=_cparams(*(["arbitrary"] * len(grid))))(*args, *rider.ins)
    return list(res[:n_out]), list(res[n_out:])


def _gather_plan(bufs, direct):
    n, nd = len(bufs), len(direct)

    def where():
        x, y, c = _me()
        return c, 2 * x + y, _other_chips(x, y), (x, y, 1 - c)

    def piece(outs, a, chip, h):
        r2 = bufs[a].shape[1] // 2
        return outs[a].at[chip, pl.ds(h * r2, r2)]

    def send(outs, sems, a, j, me, c, chip):
        return _rcopy(piece(outs, a, me, c), piece(outs, a, me, c), sems[0].at[3 * a + j], sems[1].at[3 * a + j],
                      (chip[0], chip[1], c))

    def forward(outs, sems, a, j, pc, c, sib):
        return _rcopy(piece(outs, a, pc, c), piece(outs, a, pc, c), sems[2].at[3 * a + j], sems[3].at[3 * a + j], sib)

    def dsend(dins, douts, sems, a, j, me, c, chip):
        return _rcopy(dins[a], douts[a].at[me], sems[4].at[3 * a + j], sems[5].at[3 * a + j], (chip[0], chip[1], c))

    def start(rin, rout, sems):
        outs, dins, douts = rout[:n], rin[n:], rout[n:]
        c, me, chips, _ = where()
        for a in range(n):
            for j, chip in enumerate(chips):
                send(outs, sems, a, j, me, c, chip).start()
        for a in range(nd):
            pltpu.make_async_copy(dins[a], douts[a].at[me], sems[6].at[a]).start()
            for j, chip in enumerate(chips):
                dsend(dins, douts, sems, a, j, me, c, chip).start()

    def finish(rin, rout, sems):
        outs, dins, douts = rout[:n], rin[n:], rout[n:]
        c, me, chips, sib = where()
        for a in range(n):
            for j, (cx, cy) in enumerate(chips):
                pc = 2 * cx + cy
                _rcopy(piece(outs, a, me, c), piece(outs, a, pc, c), sems[0].at[3 * a + j], sems[1].at[3 * a + j],
                       (cx, cy, c)).wait_recv()
                forward(outs, sems, a, j, pc, c, sib).start()
        for a in range(n):
            for j, (cx, cy) in enumerate(chips):
                pc = 2 * cx + cy
                _rcopy(piece(outs, a, pc, 1 - c), piece(outs, a, pc, 1 - c), sems[2].at[3 * a + j],
                       sems[3].at[3 * a + j], sib).wait_recv()
        for a in range(nd):
            for j, (cx, cy) in enumerate(chips):
                _rcopy(dins[a], douts[a].at[2 * cx + cy], sems[4].at[3 * a + j], sems[5].at[3 * a + j],
                       (cx, cy, c)).wait_recv()
        for a in range(n):
            for j, (cx, cy) in enumerate(chips):
                send(outs, sems, a, j, me, c, (cx, cy)).wait_send()
                forward(outs, sems, a, j, 2 * cx + cy, c, sib).wait_send()
        for a in range(nd):
            pltpu.make_async_copy(dins[a], douts[a].at[me], sems[6].at[a]).wait()
            for j, chip in enumerate(chips):
                dsend(dins, douts, sems, a, j, me, c, chip).wait_send()

    dma = pltpu.SemaphoreType.DMA
    return _Plan(list(bufs) + list(direct),
                 [_sds(b.shape, b.dtype) for b in bufs] + [_sds((_NCHIP,) + s.shape, s.dtype) for s in direct],
                 {a: a for a in range(n)},
                 [dma((3 * max(n, 1),)), dma((3 * max(n, 1),)), dma((3 * max(n, 1),)), dma((3 * max(n, 1),)),
                  dma((3 * max(nd, 1),)), dma((3 * max(nd, 1),)), dma((max(nd, 1),))], start, finish)


def _swap_halves(grads, name):
    n = len(grads)

    def body(*refs):
        ins, outs = refs[:n], refs[n:2 * n]
        ssem, rsem = refs[2 * n:]
        x, y, c = _me()
        sib = (x, y, 1 - c)
        cps = []
        for a in range(n):
            cp = _rcopy(ins[a].at[1 - c], outs[a], ssem.at[a], rsem.at[a], sib)
            cp.start()
            cps.append(cp)
        for cp in cps:
            cp.wait_recv()
        for cp in cps:
            cp.wait_send()

    dma = pltpu.SemaphoreType.DMA
    return _call(body, name=name, in_specs=[_ANY] * n, out_specs=[_ANY] * n,
                 out_shape=[_sds(g.shape[1:], g.dtype) for g in grads],
                 scratch_shapes=[dma((n,)), dma((n,))])(*grads)


def _add_pairs(grads, theirs, sel):
    _, s, r, c2 = grads.shape
    a3 = grads.reshape(2, s * r, c2)
    b2 = theirs.reshape(s * r, c2)
    tm = _tile(s * r, 512, 16)

    def body(sel_ref, a_ref, b_ref, o_ref):
        o_ref[...] = (a_ref[...].astype(_F32) + b_ref[...].astype(_F32)).astype(_BF)

    gs = pltpu.PrefetchScalarGridSpec(
        num_scalar_prefetch=1, grid=(s * r // tm,),
        in_specs=[pl.BlockSpec((None, tm, c2), lambda i, q: (q[1], i, 0)), pl.BlockSpec((tm, c2), lambda i, q: (i, 0))],
        out_specs=pl.BlockSpec((tm, c2), lambda i, q: (i, 0)))
    out = _call(body, name="chip_sum", grid_spec=gs, out_shape=_sds((s * r, c2), _BF),
                compiler_params=_cparams("parallel"))(sel, a3, b2)
    return out.reshape(s, r, c2)


def _exchange_plan(sums, small):
    n = len(sums)
    has_small = small is not None

    def where():
        x, y, c = _me()
        peers = [(1 - x if k & 4 else x, 1 - y if k & 2 else y, 1 - c if k & 1 else c) for k in range(1, _NDEV)]
        return c, 4 * x + 2 * y + c, _other_chips(x, y), peers

    def send(rin, rout, sems, a, j, c, chip):
        return _rcopy(rin[a].at[2 * chip[0] + chip[1]], rout[a].at[j], sems[0].at[3 * a + j], sems[1].at[3 * a + j],
                      (chip[0], chip[1], c))

    def small_send(rin, rout, sems, k, dev, peer):
        return _rcopy(rin[n], rout[n].at[dev], sems[2].at[k], sems[3].at[k], peer)

    def start(rin, rout, sems):
        c, dev, chips, peers = where()
        for a in range(n):
            for j, chip in enumerate(chips):
                send(rin, rout, sems, a, j, c, chip).start()
        if has_small:
            pltpu.make_async_copy(rin[n], rout[n].at[dev], sems[4].at[0]).start()
            for k, peer in enumerate(peers):
                small_send(rin, rout, sems, k, dev, peer).start()

    def finish(rin, rout, sems):
        c, dev, chips, peers = where()
        for a in range(n):
            for j, chip in enumerate(chips):
                send(rin, rout, sems, a, j, c, chip).wait_recv()
        if has_small:
            for k, (px, py, pc_) in enumerate(peers):
                _rcopy(rin[n], rout[n].at[4 * px + 2 * py + pc_], sems[2].at[k], sems[3].at[k], (px, py, pc_)).wait_recv()
        for a in range(n):
            for j, chip in enumerate(chips):
                send(rin, rout, sems, a, j, c, chip).wait_send()
        if has_small:
            pltpu.make_async_copy(rin[n], rout[n].at[dev], sems[4].at[0]).wait()
            for k, peer in enumerate(peers):
                small_send(rin, rout, sems, k, dev, peer).wait_send()

    dma = pltpu.SemaphoreType.DMA
    outs = [_sds((3,) + s.shape[1:], s.dtype) for s in sums]
    if has_small:
        outs.append(_sds((_NDEV,) + small.shape, small.dtype))
    return _Plan(list(sums) + ([small] if has_small else []), outs, {},
                 [dma((3 * max(n, 1),)), dma((3 * max(n, 1),)), dma((_NDEV - 1,)), dma((_NDEV - 1,)), dma((1,))],
                 start, finish)


def _shard_sum(sums, recv, sel):
    s, r, c2 = sums.shape
    tm = _tile(r, 256, 16)

    def body(sel_ref, own_ref, rc_ref, o_ref):
        rc = rc_ref[...]
        o_ref[...] = ((own_ref[...].astype(_F32) + rc[0].astype(_F32)) + rc[1].astype(_F32)) + rc[2].astype(_F32)

    gs = pltpu.PrefetchScalarGridSpec(
        num_scalar_prefetch=1, grid=(r // tm,),
        in_specs=[pl.BlockSpec((None, tm, c2), lambda i, q: (q[0], i, 0)),
                  pl.BlockSpec((3, tm, c2), lambda i, q: (0, i, 0))],
        out_specs=pl.BlockSpec((None, tm, c2), lambda i, q: (q[1], i, 0)))
    return _call(body, name="shard_sum", grid_spec=gs, out_shape=_sds((2, r, c2), _F32),
                 compiler_params=_cparams("parallel"))(sel, sums, recv)


def _sum_slots(stack, name):
    k, r, c = stack.shape
    tm = _tile(r, 256, 16 if stack.dtype == _BF else 8)

    def fn(v):
        out = v[0].astype(_F32)
        for i in range(1, k):
            out = out + v[i].astype(_F32)
        return (out,)

    return _rowcall(fn, [stack], [pl.BlockSpec((k, tm, c), lambda j, i: (0, i, 0))], [_sds((r, c), _F32)],
                    [_rb(tm, c)], [False], grid=(1, r // tm), name=name)[0]


def _share_halves(bufs):
    n = len(bufs)

    def body(*refs):
        outs = refs[n:2 * n]
        ssem, rsem = refs[2 * n:]
        x, y, c = _me()
        sib = (x, y, 1 - c)
        cps = []
        for a in range(n):
            cp = _rcopy(outs[a].at[c], outs[a].at[c], ssem.at[a], rsem.at[a], sib)
            cp.start()
            cps.append(cp)
        for a in range(n):
            _rcopy(outs[a].at[c], outs[a].at[1 - c], ssem.at[a], rsem.at[a], sib).wait_recv()
        for cp in cps:
            cp.wait_send()

    dma = pltpu.SemaphoreType.DMA
    return _call(body, name="share_halves", in_specs=[_ANY] * n, out_specs=[_ANY] * n,
                 out_shape=[_sds(b.shape, b.dtype) for b in bufs], input_output_aliases={a: a for a in range(n)},
                 scratch_shapes=[dma((n,)), dma((n,))])(*bufs)


def _adamw_math(w, g, m, v):
    m = _B1 * m + (1.0 - _B1) * g
    v = _B2 * v + (1.0 - _B2) * jnp.square(g)
    m_hat = m / (1.0 - _B1 ** _STEP)
    v_hat = v / (1.0 - _B2 ** _STEP)
    delta = -_LR * (m_hat / (jnp.sqrt(v_hat) + _ADAM_EPS) + _WD * w)
    return delta, m, v


def _adamw_shard(w, g2, m, v, name):
    r, c = w.shape
    c2 = c // 2
    tm = _tile(r, 256, 8)
    blk = pl.BlockSpec((tm, c2), lambda h, i: (i, h))

    def fn(wb, gb, mb, vb):
        return (gb,) + _adamw_math(wb, gb, mb, vb)

    return _rowcall(fn, [w, g2, m, v], [blk, pl.BlockSpec((None, tm, c2), lambda h, i: (h, i, 0)), blk, blk],
                    [_sds((r, c), _F32)] * 4, [blk] * 4, [False] * 4, grid=(2, r // tm), name=name)


def _adamw_whole(w, g, m, v, name):
    r, c = w.shape
    blk = _full((r, c))
    return _rowcall(lambda *a: _adamw_math(*a), [w, g, m, v], [blk] * 4, [_sds((r, c), _F32)] * 3, [blk] * 3,
                    [False] * 3, grid=(1, 1), name=name)


def _pack(arrs):
    parts = []
    for a in arrs:
        f = a.reshape(-1).astype(_F32)
        pad = (-f.shape[0]) % (8 * _LANE)
        if pad:
            f = jnp.concatenate([f, jnp.zeros((pad,), _F32)])
        parts.append(f)
    return jnp.concatenate(parts).reshape(-1, _LANE)


def _unpack(buf, like):
    flat = buf.reshape(-1)
    outs, off = [], 0
    for a in like:
        nel = a.size
        outs.append(flat[off:off + nel].reshape(a.shape))
        off += nel + ((-nel) % (8 * _LANE))
    return outs


def kernel(x, ln_mix_g, w_in, s5_a_re, s5_a_im, s5_log_dt, s5_b_re, s5_b_im, s5_c_re, s5_c_im, s5_d, s5_w_glu, s5_b_glu, w_proj_s5, hgrn_lb_logits, hgrn_norm_g, w_proj_hgrn, w_out, ln_ffn_g, w_up, conv_w, conv_b, w_down, ln_final_g, loss_target, m_ln_mix_g, m_w_in, m_s5_a_re, m_s5_a_im, m_s5_log_dt, m_s5_b_re, m_s5_b_im, m_s5_c_re, m_s5_c_im, m_s5_d, m_s5_w_glu, m_s5_b_glu, m_w_proj_s5, m_hgrn_lb_logits, m_hgrn_norm_g, m_w_proj_hgrn, m_w_out, m_ln_ffn_g, m_w_up, m_conv_w, m_conv_b, m_w_down, m_ln_final_g, v_ln_mix_g, v_w_in, v_s5_a_re, v_s5_a_im, v_s5_log_dt, v_s5_b_re, v_s5_b_im, v_s5_c_re, v_s5_c_im, v_s5_d, v_s5_w_glu, v_s5_b_glu, v_w_proj_s5, v_hgrn_lb_logits, v_hgrn_norm_g, v_w_proj_hgrn, v_w_out, v_ln_ffn_g, v_w_up, v_conv_w, v_conv_b, v_w_down, v_ln_final_g):
    assert x.shape[0] == 1 and w_in.shape[0] == 1, "one example per device, one layer"
    t, d = x.shape[1], x.shape[2]
    w5 = s5_w_glu.shape[2]
    hw = hgrn_norm_g.shape[1]
    ng_, np_, gc = s5_b_re.shape[1], s5_b_re.shape[2], s5_b_re.shape[3]
    dff = w_down.shape[1] * _NCHIP
    assert gc * _S5_SET == _LANE and ng_ * gc == w5 and hw % _LANE == 0
    gs_off = w5 + 4 * hw
    ci = lax.axis_index("c")
    xt = x.reshape(t, d)
    tgt = loss_target.reshape(t, d)

    big_names = ["w_in", "s5_w_glu", "w_proj_s5", "w_proj_hgrn", "w_out", "w_up", "w_down"]
    big_w = dict(w_in=w_in[0], s5_w_glu=s5_w_glu[0], w_proj_s5=w_proj_s5[0], w_proj_hgrn=w_proj_hgrn[0],
                 w_out=w_out[0], w_up=w_up[0], w_down=w_down[0])
    chip = 2 * lax.axis_index("x") + lax.axis_index("y")
    sel_chip = jnp.stack([chip, ci]).astype(jnp.int32)
    slots = {k: _cast_into_slot(big_w[k], sel_chip) for k in big_names}
    g_in, g_cw = _run_plan(_gather_plan([slots["w_in"]], [conv_w[0]]), "gather_w_in")
    cw = g_cw.transpose(1, 0, 2).reshape(3, 2 * dff)
    cb = conv_b

    tm_big = _tile(t, 1024, 8)

    h1 = _rms_fwd(xt, ln_mix_g, "rms1_fwd")
    nin_s = g_in.shape[2]
    proj, (g_glu, g_ps5, g_ph, g_out) = _mm(
        h1, g_in, "nn", _F32, tm=tm_big, tn=_tile(nin_s, 1152), tk=d, name="mm_proj",
        rider=_gather_plan([slots[k] for k in ("s5_w_glu", "w_proj_s5", "w_proj_hgrn", "w_out")], []))
    wglu = g_glu.reshape(w5, w5)
    wout = g_out.reshape(d, d)

    abar_r, abar_i, coef_r, coef_i = _s5_prep(s5_a_re[0], s5_a_im[0], s5_log_dt.reshape(ng_, 1))
    lanes = ng_ * np_
    par = (abar_r.reshape(1, lanes), abar_i.reshape(1, lanes), coef_r.reshape(1, lanes), coef_i.reshape(1, lanes),
           _bd_in(s5_b_re[0]), _bd_in(s5_b_im[0]), _bd_out(s5_c_re[0]), -_bd_out(s5_c_im[0]), s5_d.reshape(1, w5))
    (y_s5, car, cai), (g_up,) = _s5_fwd(proj, par, t, w5, rider=_gather_plan([slots["w_up"]], []))
    z = _s5glu_fwd(y_s5, wglu, s5_b_glu)
    ys = _mm(z, g_ps5, "nn", _F32, tm=tm_big, tn=g_ps5.shape[2], tk=w5, name="mm_proj_s5")

    lb = _lb_prep(hgrn_lb_logits)
    og, st_all = _hgrn_fwd(proj, lb, hgrn_norm_g, t, w5, hw)
    yh = _mm(og, g_ph, "nn", _F32, tm=tm_big, tn=g_ph.shape[2], tk=hw, name="mm_proj_hgrn")

    merged = _merge_fwd(proj, ys, yh, t, d, gs_off)
    x2 = _mm(merged, wout, "nn", _F32, tm=tm_big, tn=_tile(d, 1024), tk=d, res=xt, name="mm_out")
    h2 = _rms_fwd(x2, ln_ffn_g, "rms2_fwd")
    up_s = g_up.shape[2]
    w_conv = _tile(dff, 1408)
    nbh = dff // w_conv
    tn_up = _tile(up_s, 1408)
    tk_dh2 = _tile(up_s, w_conv)
    tn_gwup = _tile(up_s // 2, 1408)
    assert w_conv % tn_up == 0 and w_conv % tk_dh2 == 0 and w_conv % tn_gwup == 0
    up, (g_down,) = _mm(h2, g_up, "nn", _F32, tm=tm_big, tn=tn_up, tk=d, name="mm_up",
                        colperm=_pair_perm(nbh, w_conv // tn_up), rider=_gather_plan([slots["w_down"]], []))
    wdown = g_down.reshape(dff, d)
    act = _convact_fwd(up, cw, cb, t, dff)
    x3 = _mm(act, wdown, "nn", _F32, tm=tm_big, tn=_tile(d, 1024), tk=None, res=x2, name="mm_down")
    loss_part, dx3, dx3b, d_gfin = _loss_head(x3, tgt, ln_final_g.reshape(1, d))

    dact = _mm(dx3b, wdown, "nt", _F32, tm=tm_big, tn=_tile(dff, 1408), tk=d, name="mm_dact")
    r_down = dff // _NCHIP
    gw_down = _mm(act, dx3b, "tn", _BF, tm=_tile(r_down, 1408), tn=_tile(d // 2, 1024), tk=None, halves="rows",
                  name="mm_gw_down")
    def chip_sums(grads, name):
        theirs = _swap_halves(grads, name)
        return [_add_pairs(g, th, sel_chip) for g, th in zip(grads, theirs)]

    (s_down,) = chip_sums([gw_down], "swap_halves_d")
    dup, d_cw, d_cb = _convact_bwd(up, dact, cw, cb, t, dff)
    dh2, (r_down,) = _mm(dup, g_up, "nt", _BF, tm=tm_big, tn=_tile(d, 1024), tk=tk_dh2, name="mm_dh2",
                         colperm=_pair_perm(nbh, w_conv // tk_dh2), rider=_exchange_plan([s_down], None))
    gw_up = _mm(h2, dup, "tn", _BF, tm=_tile(d, 1024), tn=tn_gwup, tk=None, halves="cols", name="mm_gw_up",
                colperm=_pair_perm(nbh, w_conv // tn_gwup))
    dx2, dx2b, d_gffn = _rms_bwd(x2, ln_ffn_g, dh2, dx3, "rms2_bwd")
    dmerged = _mm(dx2b, wout, "nt", _BF, tm=tm_big, tn=_tile(d, 1024), tk=d, name="mm_dmerged")
    gw_out = _mm(merged, dx2b, "tn", _BF, tm=_tile(d // _NCHIP, 1024), tn=_tile(d // 2, 1024), tk=None, halves="rows",
                 name="mm_gw_out")
    dys, dyh, dgs, dgh = _merge_bwd(proj, ys, yh, dmerged, t, d, gs_off)
    ps_s = g_ps5.shape[2]
    dz = _mm(dys, g_ps5, "nt", _BF, tm=tm_big, tn=_tile(w5, 1024), tk=ps_s, name="mm_dz")
    gw_ps5 = _mm(z, dys, "tn", _BF, tm=_tile(w5, 1024), tn=ps_s // 2, tk=None, halves="cols", name="mm_gw_ps5")
    dog = _mm(dyh, g_ph, "nt", _BF, tm=tm_big, tn=_tile(hw, 1024), tk=ps_s, name="mm_dog")
    gw_ph = _mm(og, dyh, "tn", _BF, tm=_tile(hw, 1024), tn=ps_s // 2, tk=None, halves="cols", name="mm_gw_ph")
    dy_s5, gw_glu_full, d_bglu = _s5glu_bwd(y_s5, dz, wglu, s5_b_glu)
    r_glu = w5 // _NCHIP
    gw_glu = gw_glu_full.astype(_BF).reshape(_NCHIP, r_glu, 2, w5 // 2).transpose(2, 0, 1, 3)

    s_glu, s_ps5, s_ph, s_out, s_up = chip_sums([gw_glu, gw_ps5, gw_ph, gw_out, gw_up], "swap_halves_a")
    s5g, (r_glu_, r_ps5, r_ph, r_out, r_up) = _s5_bwd(
        proj, dy_s5, car, cai, par, t, w5, rider=_exchange_plan([s_glu, s_ps5, s_ph, s_out, s_up], None))
    du = s5g[0]
    dq, df, di, dg, d_lb, d_ng = _hgrn_bwd(proj, lb, hgrn_norm_g, st_all, dog, t, w5, hw)
    dproj = jnp.concatenate([du, dq, df, di, dg, dgs, dgh], axis=1)

    d_are, d_aim, d_ldt = _s5_prep_bwd(s5_a_re[0], s5_a_im[0], s5_log_dt.reshape(ng_, 1),
                                       [s5g[k].reshape(ng_, np_) for k in (1, 2, 3, 4)])
    d_bre = _bd_in_grad(s5g[5], np_, gc)
    d_bim = _bd_in_grad(s5g[6], np_, gc)
    d_cre = _bd_out_grad(s5g[7], np_, gc)
    d_cim = -_bd_out_grad(s5g[8], np_, gc)
    d_logits = _lb_prep_bwd(hgrn_lb_logits, d_lb)

    small_names = ["s5_a_re", "s5_a_im", "s5_log_dt", "s5_b_re", "s5_b_im", "s5_c_re", "s5_c_im", "s5_d",
                   "s5_b_glu", "hgrn_lb_logits", "hgrn_norm_g", "ln_ffn_g", "conv_b", "ln_final_g", "ln_mix_g"]
    small_w = dict(ln_mix_g=ln_mix_g, s5_a_re=s5_a_re, s5_a_im=s5_a_im, s5_log_dt=s5_log_dt, s5_b_re=s5_b_re,
                   s5_b_im=s5_b_im, s5_c_re=s5_c_re, s5_c_im=s5_c_im, s5_d=s5_d, s5_b_glu=s5_b_glu,
                   hgrn_lb_logits=hgrn_lb_logits, hgrn_norm_g=hgrn_norm_g, ln_ffn_g=ln_ffn_g, conv_b=conv_b,
                   ln_final_g=ln_final_g)
    small_m = dict(ln_mix_g=m_ln_mix_g, s5_a_re=m_s5_a_re, s5_a_im=m_s5_a_im, s5_log_dt=m_s5_log_dt, s5_b_re=m_s5_b_re,
                   s5_b_im=m_s5_b_im, s5_c_re=m_s5_c_re, s5_c_im=m_s5_c_im, s5_d=m_s5_d, s5_b_glu=m_s5_b_glu,
                   hgrn_lb_logits=m_hgrn_lb_logits, hgrn_norm_g=m_hgrn_norm_g, ln_ffn_g=m_ln_ffn_g, conv_b=m_conv_b,
                   ln_final_g=m_ln_final_g)
    small_v = dict(ln_mix_g=v_ln_mix_g, s5_a_re=v_s5_a_re, s5_a_im=v_s5_a_im, s5_log_dt=v_s5_log_dt, s5_b_re=v_s5_b_re,
                   s5_b_im=v_s5_b_im, s5_c_re=v_s5_c_re, s5_c_im=v_s5_c_im, s5_d=v_s5_d, s5_b_glu=v_s5_b_glu,
                   hgrn_lb_logits=v_hgrn_lb_logits, hgrn_norm_g=v_hgrn_norm_g, ln_ffn_g=v_ln_ffn_g, conv_b=v_conv_b,
                   ln_final_g=v_ln_final_g)
    small_g = dict(s5_a_re=d_are, s5_a_im=d_aim, s5_log_dt=d_ldt, s5_b_re=d_bre, s5_b_im=d_bim,
                   s5_c_re=d_cre, s5_c_im=d_cim, s5_d=s5g[9], s5_b_glu=d_bglu, hgrn_lb_logits=d_logits,
                   hgrn_norm_g=d_ng, ln_ffn_g=d_gffn, conv_b=d_cb, ln_final_g=d_gfin)
    like = [small_w[k] for k in small_names]
    assert small_names[-1] == "ln_mix_g"
    pack_a = _pack([small_g[k] for k in small_names[:-1]] + [d_cw])
    gw_in, (r_small_a,) = _mm(h1, dproj, "tn", _BF, tm=_tile(d, 1024), tn=_tile(nin_s // 2, 1152), tk=None, halves="cols",
                              name="mm_gw_in", rider=_exchange_plan([], pack_a))
    (s_in,) = chip_sums([gw_in], "swap_halves_b")
    dh1, (r_in,) = _mm(dproj, g_in, "nt", _BF, tm=tm_big, tn=_tile(d, 1024), tk=None, name="mm_dh1",
                       rider=_exchange_plan([s_in], None))
    dx, _, d_gmix = _rms_bwd(xt, ln_mix_g, dh1, dx2, "rms1_bwd")
    (r_small_b,) = _run_plan(_exchange_plan([], _pack([d_gmix])), "exchange_gmix")
    sums = [s_in, s_glu, s_ps5, s_ph, s_out, s_up, s_down]
    received = [r_in, r_glu_, r_ps5, r_ph, r_out, r_up, r_down]
    halves = [_shard_sum(sm, rc, sel_chip) for sm, rc in zip(sums, received)]
    g_a = _sum_slots(r_small_a, "small_sum_a")
    g_b = _sum_slots(r_small_b, "small_sum_b")
    full = _share_halves(halves)
    w_pack = _pack(like)
    rows_a = w_pack.shape[0] - g_b.shape[0]
    g_small = jnp.concatenate([g_a[:rows_a], g_b], axis=0)
    cs = conv_w.shape[2]
    g_cw_full = g_a[rows_a:].reshape(-1)[:3 * 2 * dff].reshape(3, 2 * dff)
    g_cw = lax.dynamic_slice_in_dim(g_cw_full, chip * cs, cs, axis=1)

    big_m = dict(w_in=m_w_in, s5_w_glu=m_s5_w_glu, w_proj_s5=m_w_proj_s5, w_proj_hgrn=m_w_proj_hgrn, w_out=m_w_out,
                 w_up=m_w_up, w_down=m_w_down)
    big_v = dict(w_in=v_w_in, s5_w_glu=v_s5_w_glu, w_proj_s5=v_w_proj_s5, w_proj_hgrn=v_w_proj_hgrn, w_out=v_w_out,
                 w_up=v_w_up, w_down=v_w_down)
    res = {}
    for k, g2 in zip(big_names, full):
        w2 = big_w[k]
        shp = (1,) + w2.shape
        outs = _adamw_shard(w2, g2, big_m[k][0], big_v[k][0], "adamw_" + k)
        res[k] = [o.reshape(shp) for o in outs]
    sm_outs = _adamw_whole(w_pack, g_small, _pack([small_m[k] for k in small_names]),
                           _pack([small_v[k] for k in small_names]), "adamw_small")
    sm_g = _unpack(g_small, like)
    sm_d, sm_m, sm_v = (_unpack(o, like) for o in sm_outs)
    for i, k in enumerate(small_names):
        res[k] = [sm_g[i], sm_d[i], sm_m[i], sm_v[i]]

    cw_outs = _adamw_whole(conv_w[0], g_cw, m_conv_w[0], v_conv_w[0], "adamw_conv_w")
    res["conv_w"] = [g_cw.reshape(conv_w.shape)] + [o.reshape(conv_w.shape) for o in cw_outs]

    loss = lax.psum(loss_part[0, 0], ("x", "y", "c"))
    order = ["ln_mix_g", "w_in", "s5_a_re", "s5_a_im", "s5_log_dt", "s5_b_re", "s5_b_im", "s5_c_re", "s5_c_im", "s5_d",
             "s5_w_glu", "s5_b_glu", "w_proj_s5", "hgrn_lb_logits", "hgrn_norm_g", "w_proj_hgrn", "w_out", "ln_ffn_g",
             "w_up", "conv_w", "conv_b", "w_down", "ln_final_g"]
    return (loss, dx.reshape(x.shape), *[res[k][0] for k in order], *[res[k][1] for k in order],
            *[res[k][2] for k in order], *[res[k][3] for k in order])
```

```python
import functools

import jax
import jax.numpy as jnp
from jax import lax
from jax.experimental import pallas as pl
from jax.experimental.pallas import tpu as pltpu

_F32 = jnp.float32
_BF = jnp.bfloat16
_RMS_EPS = 1e-6
_S5_MAX_RE = -1e-4
_LR, _B1, _B2, _ADAM_EPS, _WD, _STEP = 0.001, 0.9, 0.999, 1e-08, 0.01, 10
_MESH = pl.DeviceIdType.MESH
_ANY = pl.BlockSpec(memory_space=pl.ANY)
_LANE = 128
_VMEM_LIMIT = 56 * 1024 * 1024
_CHUNK = 64
_S5_TB = 256
_S5_SET = 8
_HGRN_HP = 8
_HGRN_SEG = 512
_NCHIP = 4
_NDEV = 8


def _call(body, **kw):
    return pl.pallas_call(body, **kw)


def _cparams(*sem):
    return pltpu.CompilerParams(dimension_semantics=sem, vmem_limit_bytes=_VMEM_LIMIT)


def _tile(n, pref, unit=_LANE):
    if n <= pref:
        return n
    t = (pref // unit) * unit
    while t >= unit:
        if n % t == 0:
            return t
        t -= unit
    raise ValueError(f"no tile for {n}")


_OPERAND_BYTES = 12 * 1024 * 1024


def _tk_fit(k, tm, tn):
    best = _LANE
    for tk in range(_LANE, k + 1, _LANE):
        if k % tk == 0 and (tm + tn) * tk * 2 <= _OPERAND_BYTES:
            best = tk
    return best if k % _LANE == 0 else k


_NN = ((1,), (0,))
_NT = ((1,), (1,))
_TN = ((0,), (0,))


def _dg(a, b, dims):
    return lax.dot_general(a.astype(_BF), b.astype(_BF), (dims, ((), ())), preferred_element_type=_F32)


@jax.custom_vjp
def _bdot(a, b):
    return _dg(a, b, _NN)


def _bdot_f(a, b):
    return _dg(a, b, _NN), (a, b)


def _bdot_b(res, g):
    a, b = res
    return _dg(g, b, _NT).astype(a.dtype), _dg(a, g, _TN).astype(b.dtype)


_bdot.defvjp(_bdot_f, _bdot_b)


@jax.custom_vjp
def _bdot_nt(a, b):
    return _dg(a, b, _NT)


def _bdot_nt_f(a, b):
    return _dg(a, b, _NT), (a, b)


def _bdot_nt_b(res, g):
    a, b = res
    return _dg(g, b, _NN).astype(a.dtype), _dg(g, a, _TN).astype(b.dtype)


_bdot_nt.defvjp(_bdot_nt_f, _bdot_nt_b)


@jax.custom_vjp
def _bdot_tn(a, b):
    return _dg(a, b, _TN)


def _bdot_tn_f(a, b):
    return _dg(a, b, _TN), (a, b)


def _bdot_tn_b(res, g):
    a, b = res
    return _dg(b, g, _NT).astype(a.dtype), _dg(a, g, _NN).astype(b.dtype)


_bdot_tn.defvjp(_bdot_tn_f, _bdot_tn_b)


_SUBLANES = 8


def _shift_up(x, n):
    r = x.shape[0]
    if n % _SUBLANES == 0:
        return jnp.concatenate([x[n:], jnp.zeros((n,) + x.shape[1:], x.dtype)], axis=0)
    row = lax.broadcasted_iota(jnp.int32, x.shape, 0)
    return jnp.where(row < r - n, pltpu.roll(x, r - n, 0), 0.0)


@functools.partial(jax.custom_vjp, nondiff_argnums=(1,))
def _shift_down(x, n):
    if n % _SUBLANES == 0:
        return jnp.concatenate([jnp.zeros((n,) + x.shape[1:], x.dtype), x[:x.shape[0] - n]], axis=0)
    row = lax.broadcasted_iota(jnp.int32, x.shape, 0)
    return jnp.where(row >= n, pltpu.roll(x, n, 0), 0.0)


def _shift_down_f(x, n):
    return _shift_down(x, n), None


def _shift_down_b(n, _, g):
    return (_shift_up(g, n),)


_shift_down.defvjp(_shift_down_f, _shift_down_b)


def _rows_apart(x):
    return tuple(x[k:k + _SUBLANES] for k in range(0, x.shape[0], _SUBLANES))


@jax.custom_vjp
def _split_rows(x):
    return _rows_apart(x)


_split_rows.defvjp(lambda x: (_rows_apart(x), None), lambda _, gs: (jnp.concatenate(gs, axis=0),))


@jax.custom_vjp
def _join_rows(pieces):
    return jnp.concatenate(pieces, axis=0)


_join_rows.defvjp(lambda pieces: (jnp.concatenate(pieces, axis=0), None), lambda _, g: (_rows_apart(g),))


@jax.custom_vjp
def _last_row(x):
    return x[_SUBLANES - 1:_SUBLANES]


def _last_row_b(_, g):
    row = lax.broadcasted_iota(jnp.int32, (_SUBLANES, g.shape[1]), 0)
    return (jnp.where(row == _SUBLANES - 1, g, 0.0),)


_last_row.defvjp(lambda x: (x[_SUBLANES - 1:_SUBLANES], None), _last_row_b)


def _sigmoid(x):
    return 1.0 / (1.0 + jnp.exp(-x))


def _silu(x):
    return x * _sigmoid(x)


def _gelu(x):
    return 0.5 * x * (1.0 + jnp.tanh(0.7978845608028654 * (x + 0.044715 * (x * x * x))))


def _rms_core(x, g):
    return x * lax.rsqrt(jnp.mean(x * x, axis=-1, keepdims=True) + _RMS_EPS) * g


def _mm(a, b, mode, out_dtype, *, tm, tn, tk, res=None, halves=None, rider=None, colperm=None, name):
    if colperm is None:
        def colperm(n_):
            return n_
    if tk is None:
        kdim = a.shape[0] if mode == "tn" else (b.shape[2] if (mode == "nt" and b.ndim == 3) else a.shape[1])
        tk = _tk_fit(kdim, tm, tn)
    if mode == "nn":
        m, k = a.shape
        a_spec = pl.BlockSpec((tm, tk), lambda i, j, kk: (i, kk))
        if b.ndim == 3:
            s, _, ns = b.shape
            n = s * ns
            npb = ns // tn
            b_spec = pl.BlockSpec((None, tk, tn), lambda i, j, kk: (j // npb, kk, j % npb))
        else:
            n = b.shape[1]
            b_spec = pl.BlockSpec((tk, tn), lambda i, j, kk: (kk, j))
        dims = _NN
    elif mode == "nt":
        m, k = a.shape
        a_spec = pl.BlockSpec((tm, tk), lambda i, j, kk: (i, colperm(kk)))
        if b.ndim == 3:
            s, n, ks = b.shape
            kpb = ks // tk
            b_spec = pl.BlockSpec((None, tn, tk), lambda i, j, kk: (kk // kpb, j, kk % kpb))
        else:
            n = b.shape[0]
            b_spec = pl.BlockSpec((tn, tk), lambda i, j, kk: (j, kk))
        dims = _NT
    else:
        k, m = a.shape
        n = b.shape[1]
        a_spec = pl.BlockSpec((tk, tm), lambda i, j, kk: (kk, i))
        b_spec = pl.BlockSpec((tk, tn), lambda i, j, kk: (kk, colperm(j)))
        dims = _TN
    nk = k // tk
    if halves is None:
        out_shape = jax.ShapeDtypeStruct((m, n), out_dtype)
        out_spec = pl.BlockSpec((tm, tn), lambda i, j, kk: (i, colperm(j) if mode == "nn" else j))
    elif halves == "cols":
        c2 = n // (2 * _NCHIP)
        tpc = c2 // tn
        out_shape = jax.ShapeDtypeStruct((2, _NCHIP, m, c2), out_dtype)
        out_spec = pl.BlockSpec((None, None, tm, tn),
                                lambda i, j, kk: ((j // tpc) % 2, j // (2 * tpc), i, j % tpc))
    else:
        c2 = n // 2
        tpc = c2 // tn
        r = m // _NCHIP
        tpr = r // tm
        out_shape = jax.ShapeDtypeStruct((2, _NCHIP, r, c2), out_dtype)
        out_spec = pl.BlockSpec((None, None, tm, tn),
                                lambda i, j, kk: (j // tpc, i // tpr, i % tpr, j % tpc))
    has_res = res is not None
    nreg = 3 if has_res else 2
    ni, nj = m // tm, n // tn
    r_ins = list(rider.ins) if rider else []
    r_outs = list(rider.outs) if rider else []

    def body(*refs):
        a_ref, b_ref = refs[0], refs[1]
        r_ref = refs[2] if has_res else None
        rin = refs[nreg:nreg + len(r_ins)]
        o_ref = refs[nreg + len(r_ins)]
        rout = refs[nreg + len(r_ins) + 1:nreg + len(r_ins) + 1 + len(r_outs)]
        acc_ref = refs[nreg + len(r_ins) + 1 + len(r_outs)]
        sems = refs[nreg + len(r_ins) + 2 + len(r_outs):]
        i, j, kk = pl.program_id(0), pl.program_id(1), pl.program_id(2)

        if rider:
            @pl.when(jnp.logical_and(jnp.logical_and(i == 0, j == 0), kk == 0))
            def _():
                rider.start(rin, rout, sems)

        @pl.when(kk == 0)
        def _():
            acc_ref[...] = jnp.zeros_like(acc_ref)

        acc_ref[...] += _dg(a_ref[...], b_ref[...], dims)

        @pl.when(kk == nk - 1)
        def _():
            out = acc_ref[...]
            if has_res:
                out = out + r_ref[...]
            o_ref[...] = out.astype(out_dtype)

        if rider:
            @pl.when(jnp.logical_and(jnp.logical_and(i == ni - 1, j == nj - 1), kk == nk - 1))
            def _():
                rider.finish(rin, rout, sems)

    in_specs = [a_spec, b_spec]
    args = [a, b]
    if has_res:
        in_specs.append(pl.BlockSpec((tm, tn), lambda i, j, kk: (i, j)))
        args.append(res)
    if not rider:
        return _call(body, name=name, grid=(ni, nj, nk), in_specs=in_specs, out_specs=out_spec,
                     out_shape=out_shape, scratch_shapes=[pltpu.VMEM((tm, tn), _F32)],
                     compiler_params=_cparams("parallel", "parallel", "arbitrary"))(*args)
    res_all = _call(body, name=name, grid=(ni, nj, nk), in_specs=in_specs + [_ANY] * len(r_ins),
                    out_specs=[out_spec] + [_ANY] * len(r_outs), out_shape=[out_shape] + r_outs,
                    input_output_aliases={nreg + k: 1 + v for k, v in rider.aliases.items()},
                    scratch_shapes=[pltpu.VMEM((tm, tn), _F32)] + list(rider.sems),
                    compiler_params=_cparams("arbitrary", "arbitrary", "arbitrary"))(*args, *r_ins)
    return res_all[0], list(res_all[1:])


def _rowcall(fn, ins, in_specs, outs, out_specs, acc, *, grid, name):
    nin = len(ins)

    def body(*refs):
        vals = fn(*[r[...] for r in refs[:nin]])
        first = pl.program_id(1) == 0
        for k, (o_ref, v) in enumerate(zip(refs[nin:], vals)):
            if acc[k]:
                @pl.when(first)
                def _(o_ref=o_ref):
                    o_ref[...] = jnp.zeros_like(o_ref)
                o_ref[...] += v.astype(o_ref.dtype)
            else:
                o_ref[...] = v.astype(o_ref.dtype)

    return _call(body, name=name, grid=grid, in_specs=in_specs, out_specs=out_specs, out_shape=outs,
                 compiler_params=_cparams("parallel", "arbitrary"))(*ins)


def _rb(tm, w, cb=0):
    return pl.BlockSpec((tm, w), lambda j, i: (i, cb + j))


def _cb(w, cb=0):
    return pl.BlockSpec((1, w), lambda j, i: (0, cb + j))


def _full(shape):
    nd = len(shape)
    return pl.BlockSpec(shape, lambda j, i: (0,) * nd)


def _sds(shape, dtype):
    return jax.ShapeDtypeStruct(shape, dtype)


def _rms_fwd(x, g, name):
    t, d = x.shape
    tm = _tile(t, 512, 8)
    return _rowcall(lambda xb, gb: (_rms_core(xb, gb),), [x, g], [_rb(tm, d), _full((1, d))],
                    [_sds((t, d), _BF)], [_rb(tm, d)], [False], grid=(1, t // tm), name=name)[0]


def _rms_bwd(x, g, dh, dres, name):
    t, d = x.shape
    tm = _tile(t, 256, 8)

    def fn(xb, gb, dhb, drb):
        _, vjp = jax.vjp(_rms_core, xb, gb)
        dx, dg = vjp(dhb.astype(_F32))
        dx = dx + drb
        return dx, dx, dg

    return _rowcall(fn, [x, g, dh, dres], [_rb(tm, d), _full((1, d)), _rb(tm, d), _rb(tm, d)],
                    [_sds((t, d), _F32), _sds((t, d), _BF), _sds((1, d), _F32)],
                    [_rb(tm, d), _rb(tm, d), _full((1, d))], [False, False, True],
                    grid=(1, t // tm), name=name)


def _loss_head(x3, tgt, g):
    t, d = x3.shape
    tm = _tile(t, 256, 8)

    def fn(xb, tb, gb):
        y, vjp = jax.vjp(_rms_core, xb, gb)
        e = y - tb
        part = 0.5 * jnp.sum(jnp.mean(e * e, axis=-1, keepdims=True), axis=0, keepdims=True)
        dx, dg = vjp(e * (1.0 / d))
        return jnp.broadcast_to(part, (1, _LANE)), dx, dx, dg

    return _rowcall(fn, [x3, tgt, g], [_rb(tm, d), _rb(tm, d), _full((1, d))],
                    [_sds((1, _LANE), _F32), _sds((t, d), _F32), _sds((t, d), _BF), _sds((1, d), _F32)],
                    [_full((1, _LANE)), _rb(tm, d), _rb(tm, d), _full((1, d))], [True, False, False, True],
                    grid=(1, t // tm), name="loss_head")


def _s5_disc(a_re, a_im, log_dt):
    lam_re = jnp.minimum(a_re, _S5_MAX_RE)
    lam_im = a_im
    dt = jnp.exp(log_dt)
    mag = jnp.exp(lam_re * dt)
    abar_re = mag * jnp.cos(lam_im * dt)
    abar_im = mag * jnp.sin(lam_im * dt)
    den = lam_re * lam_re + lam_im * lam_im
    nr = abar_re - 1.0
    ni = abar_im
    coef_re = (nr * lam_re + ni * lam_im) / den
    coef_im = (ni * lam_re - nr * lam_im) / den
    return abar_re, abar_im, coef_re, coef_im


def _s5_prep(a_re, a_im, log_dt):
    g, p = a_re.shape

    def body(ar, ai, ld, o0, o1, o2, o3):
        outs = _s5_disc(ar[...], ai[...], ld[...])
        for o, v in zip((o0, o1, o2, o3), outs):
            o[...] = v

    return _call(body, name="s5_prep", out_shape=[_sds((g, p), _F32)] * 4)(a_re, a_im, log_dt)


def _s5_prep_bwd(a_re, a_im, log_dt, cts):
    g, p = a_re.shape

    def body(ar, ai, ld, c0, c1, c2, c3, d0, d1, d2):
        _, vjp = jax.vjp(_s5_disc, ar[...], ai[...], ld[...])
        outs = vjp((c0[...], c1[...], c2[...], c3[...]))
        for o, v in zip((d0, d1, d2), outs):
            o[...] = v

    return _call(body, name="s5_prep_bwd",
                 out_shape=[_sds((g, p), _F32), _sds((g, p), _F32), _sds((g, 1), _F32)])(a_re, a_im, log_dt, *cts)


def _s5_block(u, car, cai, ar, ai, cr, ci, b_re, b_im, c_re, c_imn, dvec):
    bur = _bdot(u, b_re)
    bui = _bdot(u, b_im)
    shape8 = (_SUBLANES, ar.shape[1])
    pows = [(ar, ai)]
    for _ in range(2):
        pr, pi = pows[-1]
        pows.append((pr * pr - pi * pi, 2.0 * pr * pi))
    pows = [(jnp.broadcast_to(pr, shape8), jnp.broadcast_to(pi, shape8)) for pr, pi in pows]
    cr, ci = jnp.broadcast_to(cr, shape8), jnp.broadcast_to(ci, shape8)

    def scan8(xr, xi):
        for k, (pr, pi) in enumerate(pows):
            dr = _shift_down(xr, 1 << k)
            di = _shift_down(xi, 1 << k)
            xr, xi = xr + pr * dr - pi * di, xi + pr * di + pi * dr
        return xr, xi

    row8 = lax.broadcasted_iota(jnp.int32, (_SUBLANES, ar.shape[1]), 0)
    tr, ti = scan8(jnp.where(row8 == 0, ar, 0.0), jnp.where(row8 == 0, ai, 0.0))
    outs_r, outs_i = [], []
    for xr, xi in zip(_split_rows(bur), _split_rows(bui)):
        xr, xi = scan8(cr * xr - ci * xi, cr * xi + ci * xr)
        xr, xi = xr + tr * car - ti * cai, xi + tr * cai + ti * car
        car, cai = _last_row(xr), _last_row(xi)
        outs_r.append(xr)
        outs_i.append(xi)
    y = _bdot(_join_rows(tuple(outs_r)), c_re) + _bdot(_join_rows(tuple(outs_i)), c_imn) + dvec * u
    return y, car, cai


def _s5_specs(tb, lw, nt, rev):
    tmap = (lambda t: nt - 1 - t) if rev else (lambda t: t)
    vec = pl.BlockSpec((1, lw), lambda s, t: (0, s))
    return dict(
        u=pl.BlockSpec((tb, _LANE), lambda s, t: (tmap(t), s)),
        car=pl.BlockSpec((None, 1, lw), lambda s, t: (tmap(t), 0, s)),
        vec=vec,
        bmat=pl.BlockSpec((None, _LANE, lw), lambda s, t: (s, 0, 0)),
        cmat=pl.BlockSpec((None, lw, _LANE), lambda s, t: (s, 0, 0)),
        dvec=pl.BlockSpec((1, _LANE), lambda s, t: (0, s)),
    )


def _s5_fwd(proj, par, t, w5, rider=None):
    ar, ai, cr, ci, b_re, b_im, c_re, c_imn, dvec = par
    ns = w5 // _LANE
    lw = ar.shape[1] // ns
    tb = min(_S5_TB, t)
    nt = t // tb
    sp = _s5_specs(tb, lw, nt, False)

    def body(u_ref, ar_r, ai_r, cr_r, ci_r, bre_r, bim_r, cre_r, cim_r, d_r, y_ref, car_ref, cai_ref, s_r, s_i):
        @pl.when(pl.program_id(1) == 0)
        def _():
            s_r[...] = jnp.zeros_like(s_r)
            s_i[...] = jnp.zeros_like(s_i)

        car_ref[...] = s_r[...]
        cai_ref[...] = s_i[...]
        y, ncr, nci = _s5_block(u_ref[...], s_r[...], s_i[...], ar_r[...], ai_r[...], cr_r[...], ci_r[...],
                                bre_r[...], bim_r[...], cre_r[...], cim_r[...], d_r[...])
        y_ref[...] = y
        s_r[...] = ncr
        s_i[...] = nci

    kw = dict(name="s5_fwd", grid=(ns, nt),
              in_specs=[sp["u"], sp["vec"], sp["vec"], sp["vec"], sp["vec"], sp["bmat"], sp["bmat"],
                        sp["cmat"], sp["cmat"], sp["dvec"]],
              out_specs=[sp["u"], sp["car"], sp["car"]],
              out_shape=[_sds((t, w5), _F32), _sds((nt, 1, ns * lw), _F32), _sds((nt, 1, ns * lw), _F32)],
              scratch_shapes=[pltpu.VMEM((1, lw), _F32), pltpu.VMEM((1, lw), _F32)])
    args = (proj, ar, ai, cr, ci, b_re, b_im, c_re, c_imn, dvec)
    if rider is None:
        return _call(body, compiler_params=_cparams("parallel", "arbitrary"), **kw)(*args), []
    return _call_riding(body, rider, args=args, **kw)


def _s5_bwd(proj, dy, car, cai, par, t, w5, rider=None):
    ar, ai, cr, ci, b_re, b_im, c_re, c_imn, dvec = par
    ns = w5 // _LANE
    lw = ar.shape[1] // ns
    tb = min(_S5_TB, t)
    nt = t // tb
    sp = _s5_specs(tb, lw, nt, True)

    def body(u_ref, dy_ref, car_ref, cai_ref, ar_r, ai_r, cr_r, ci_r, bre_r, bim_r, cre_r, cim_r, d_r,
             du_ref, g_ar, g_ai, g_cr, g_ci, g_bre, g_bim, g_cre, g_cim, g_d, ds_r, ds_i):
        accs = (g_ar, g_ai, g_cr, g_ci, g_bre, g_bim, g_cre, g_cim, g_d)

        @pl.when(pl.program_id(1) == 0)
        def _():
            ds_r[...] = jnp.zeros_like(ds_r)
            ds_i[...] = jnp.zeros_like(ds_i)
            for o in accs:
                o[...] = jnp.zeros_like(o)

        _, vjp = jax.vjp(_s5_block, u_ref[...], car_ref[...], cai_ref[...], ar_r[...], ai_r[...], cr_r[...],
                         ci_r[...], bre_r[...], bim_r[...], cre_r[...], cim_r[...], d_r[...])
        grads = vjp((dy_ref[...], ds_r[...], ds_i[...]))
        du_ref[...] = grads[0].astype(_BF)
        ds_r[...] = grads[1]
        ds_i[...] = grads[2]
        for o, gval in zip(accs, grads[3:]):
            o[...] += gval

    vec_o = _sds((1, ns * lw), _F32)
    kw = dict(name="s5_bwd", grid=(ns, nt),
              in_specs=[sp["u"], sp["u"], sp["car"], sp["car"], sp["vec"], sp["vec"], sp["vec"], sp["vec"],
                        sp["bmat"], sp["bmat"], sp["cmat"], sp["cmat"], sp["dvec"]],
              out_specs=[sp["u"], sp["vec"], sp["vec"], sp["vec"], sp["vec"], sp["bmat"], sp["bmat"],
                         sp["cmat"], sp["cmat"], sp["dvec"]],
              out_shape=[_sds((t, w5), _BF), vec_o, vec_o, vec_o, vec_o,
                         _sds(b_re.shape, _F32), _sds(b_re.shape, _F32), _sds(c_re.shape, _F32),
                         _sds(c_re.shape, _F32), _sds((1, w5), _F32)],
              scratch_shapes=[pltpu.VMEM((1, lw), _F32), pltpu.VMEM((1, lw), _F32)])
    args = (proj, dy, car, cai, ar, ai, cr, ci, b_re, b_im, c_re, c_imn, dvec)
    if rider is None:
        return _call(body, compiler_params=_cparams("parallel", "arbitrary"), **kw)(*args), []
    return _call_riding(body, rider, args=args, **kw)


def _bd_in(b):
    g, p, c = b.shape
    s = g // _S5_SET
    b4 = b.reshape(s, _S5_SET, p, c).transpose(0, 1, 3, 2)
    eye = jnp.eye(_S5_SET, dtype=b.dtype)
    return (b4[:, :, :, None, :] * eye[None, :, None, :, None]).reshape(s, _S5_SET * c, _S5_SET * p)


def _bd_in_grad(d, p, c):
    s = d.shape[0]
    eye = jnp.eye(_S5_SET, dtype=d.dtype)
    d5 = d.reshape(s, _S5_SET, c, _S5_SET, p) * eye[None, :, None, :, None]
    return d5.sum(axis=3).transpose(0, 1, 3, 2).reshape(s * _S5_SET, p, c)


def _bd_out(cm):
    g, c, p = cm.shape
    s = g // _S5_SET
    c4 = cm.reshape(s, _S5_SET, c, p).transpose(0, 1, 3, 2)
    eye = jnp.eye(_S5_SET, dtype=cm.dtype)
    return (c4[:, :, :, None, :] * eye[None, :, None, :, None]).reshape(s, _S5_SET * p, _S5_SET * c)


def _bd_out_grad(d, p, c):
    s = d.shape[0]
    eye = jnp.eye(_S5_SET, dtype=d.dtype)
    d5 = d.reshape(s, _S5_SET, p, _S5_SET, c) * eye[None, :, None, :, None]
    return d5.sum(axis=3).transpose(0, 1, 3, 2).reshape(s * _S5_SET, c, p)


def _s5glu_fwd(y, wglu, bglu):
    t, w5 = y.shape
    tm = _tile(t, 256, 8)

    def fn(yb, wb, bb):
        z1 = _gelu(yb)
        a = _dg(z1, wb, _NN) + bb
        return (z1 * _sigmoid(a),)

    return _rowcall(fn, [y, wglu, bglu], [_rb(tm, w5), _full(wglu.shape), _full((1, w5))],
                    [_sds((t, w5), _BF)], [_rb(tm, w5)], [False], grid=(1, t // tm), name="s5glu_fwd")[0]


def _s5glu_bwd(y, dz, wglu, bglu):
    t, w5 = y.shape
    tm = _tile(t, 256, 8)

    def fn(yb, dzb, wb, bb):
        dzb = dzb.astype(_F32)
        z1, gelu_vjp = jax.vjp(_gelu, yb)
        sig = _sigmoid(_dg(z1, wb, _NN) + bb)
        da = dzb * z1 * sig * (1.0 - sig)
        dz1 = dzb * sig + _dg(da, wb, _NT)
        (dy,) = gelu_vjp(dz1)
        return dy, _dg(z1, da, _TN), jnp.sum(da, axis=0, keepdims=True)

    return _rowcall(fn, [y, dz, wglu, bglu], [_rb(tm, w5), _rb(tm, w5), _full(wglu.shape), _full((1, w5))],
                    [_sds((t, w5), _F32), _sds((w5, w5), _F32), _sds((1, w5), _F32)],
                    [_rb(tm, w5), _full((w5, w5)), _full((1, w5))], [False, True, True],
                    grid=(1, t // tm), name="s5glu_bwd")


_LEVELS = (6, 5, 4, 3, 2, 1)


def _tri_stack(c):
    t = jnp.arange(c, dtype=jnp.int32)[:, None]
    j = jnp.arange(c, dtype=jnp.int32)[None, :]
    low = (j <= t).astype(_F32)
    mats = [low]
    for sh in _LEVELS:
        r = ((t >> sh) << sh) + ((1 << (sh - 1)) - 1)
        mats.append(low - (j <= r).astype(_F32))
    stack = jnp.concatenate(mats, axis=0).astype(_BF)
    return stack, stack.T


def _split_dot(mat, x):
    l = x.shape[1]
    hi = x.astype(_BF)
    lo = (x - hi.astype(_F32)).astype(_BF)
    out = jnp.dot(mat, jnp.concatenate([hi, lo], axis=1), preferred_element_type=_F32)
    return out[:, :l] + out[:, l:]


@jax.custom_vjp
def _decay_sums(lf, tri, tri_t):
    c = lf.shape[0]
    out = _split_dot(tri, lf)
    return tuple(out[k * c:(k + 1) * c] for k in range(len(_LEVELS) + 1))


def _decay_sums_f(lf, tri, tri_t):
    return _decay_sums(lf, tri, tri_t), (tri, tri_t)


def _decay_sums_b(res, gs):
    tri, tri_t = res
    return _split_dot(tri_t, jnp.concatenate(gs, axis=0)), jnp.zeros_like(tri), jnp.zeros_like(tri_t)


_decay_sums.defvjp(_decay_sums_f, _decay_sums_b)


def _hgrn_chunk(qi, fi, vi, gi, st, lb, ng, tri, tri_t):
    c = qi.shape[0]
    row = lax.broadcasted_iota(jnp.int32, (c, 1), 0)
    q = _silu(qi)
    lf = jnp.log(lb + (1.0 - lb) * _sigmoid(fi))
    k = (1.0 - lb) * _sigmoid(-fi)
    sums = _decay_sums(lf, tri, tri_t)
    b = sums[0]
    btot = jnp.sum(lf, axis=0, keepdims=True)
    inter = _bdot_nt(q * jnp.exp(b), st)
    tt = lax.broadcasted_iota(jnp.int32, (c, c), 0)
    ss = lax.broadcasted_iota(jnp.int32, (c, c), 1)
    sc = jnp.where(tt == ss, jnp.sum(q * k, axis=1, keepdims=True), 0.0)
    for sh, p in zip(_LEVELS, sums[1:]):
        upper = ((row >> (sh - 1)) & 1) == 1
        qm = jnp.where(upper, q * jnp.exp(jnp.where(upper, p, 0.0)), 0.0)
        km = jnp.where(upper, 0.0, k * jnp.exp(jnp.where(upper, 0.0, -p)))
        sc = sc + jnp.where((tt >> sh) == (ss >> sh), _bdot_nt(qm, km), 0.0)
    o = inter + _bdot(sc, vi)
    st_new = st * jnp.exp(btot) + _bdot_tn(vi, k * jnp.exp(btot - b))
    on = o * lax.rsqrt(jnp.mean(o * o, axis=1, keepdims=True) + _RMS_EPS) * ng
    return on * _silu(gi), st_new


def _hgrn_geom(t, w5, hw):
    hp = _HGRN_HP if (hw // _LANE) % _HGRN_HP == 0 and w5 % (_LANE * _HGRN_HP) == 0 else 1
    seg = min(t, _HGRN_SEG)
    return hp, hp * _LANE, seg, t // seg


def _hgrn_in_specs(seg, wd, w5, hw, tmap):
    nhp = hw // wd
    qb = w5 // wd
    return [pl.BlockSpec((seg, wd), (lambda h, s, k=k: (tmap(s), qb + k * nhp + h))) for k in range(4)]


def _hgrn_fwd(proj, lb, ng, t, w5, hw):
    assert _CHUNK == 64
    hp, wd, seg, nseg = _hgrn_geom(t, w5, hw)
    ncs = seg // _CHUNK
    vec = pl.BlockSpec((1, wd), lambda h, s: (0, h))

    tri, tri_t = _tri_stack(_CHUNK)

    def body(q_ref, f_ref, i_ref, g_ref, lb_ref, ng_ref, tri_ref, trit_ref, og_ref, st_ref, s_scr):
        @pl.when(pl.program_id(1) == 0)
        def _():
            s_scr[...] = jnp.zeros_like(s_scr)

        tri_v, trit_v = tri_ref[...], trit_ref[...]

        def step(ci, carry):
            r = pl.ds(pl.multiple_of(ci * _CHUNK, _CHUNK), _CHUNK)
            for hh in range(hp):
                ln = slice(hh * _LANE, (hh + 1) * _LANE)
                st_ref[hh, ci] = s_scr[hh]
                og, sn = _hgrn_chunk(q_ref[r, ln], f_ref[r, ln], i_ref[r, ln], g_ref[r, ln], s_scr[hh],
                                     lb_ref[:, ln], ng_ref[:, ln], tri_v, trit_v)
                og_ref[r, ln] = og.astype(_BF)
                s_scr[hh] = sn
            return carry

        lax.fori_loop(0, ncs, step, 0)

    return _call(body, name="hgrn_fwd", grid=(hw // wd, nseg),
                 in_specs=_hgrn_in_specs(seg, wd, w5, hw, lambda s: s) + [
                     vec, vec, pl.BlockSpec(tri.shape, lambda h, s: (0, 0)), pl.BlockSpec(tri_t.shape, lambda h, s: (0, 0))],
                 out_specs=[pl.BlockSpec((seg, wd), lambda h, s: (s, h)),
                            pl.BlockSpec((hp, ncs, _LANE, _LANE), lambda h, s: (h, s, 0, 0))],
                 out_shape=[_sds((t, hw), _BF), _sds((hw // _LANE, t // _CHUNK, _LANE, _LANE), _F32)],
                 scratch_shapes=[pltpu.VMEM((hp, _LANE, _LANE), _F32)],
                 compiler_params=_cparams("parallel", "arbitrary"))(proj, proj, proj, proj, lb, ng, tri, tri_t)


def _hgrn_bwd(proj, lb, ng, st_all, dog, t, w5, hw):
    hp, wd, seg, nseg = _hgrn_geom(t, w5, hw)
    ncs = seg // _CHUNK

    def rev(s):
        return nseg - 1 - s

    vec = pl.BlockSpec((1, wd), lambda h, s: (0, h))
    col = pl.BlockSpec((seg, wd), lambda h, s: (rev(s), h))

    tri, tri_t = _tri_stack(_CHUNK)

    def body(q_ref, f_ref, i_ref, g_ref, lb_ref, ng_ref, tri_ref, trit_ref, st_ref, dog_ref,
             dq_ref, df_ref, di_ref, dg_ref, dlb_ref, dng_ref, ds_scr):
        @pl.when(pl.program_id(1) == 0)
        def _():
            ds_scr[...] = jnp.zeros_like(ds_scr)
            dlb_ref[...] = jnp.zeros_like(dlb_ref)
            dng_ref[...] = jnp.zeros_like(dng_ref)

        tri_v, trit_v = tri_ref[...], trit_ref[...]

        def step(kk, carry):
            ci = ncs - 1 - kk
            r = pl.ds(pl.multiple_of(ci * _CHUNK, _CHUNK), _CHUNK)
            for hh in range(hp):
                ln = slice(hh * _LANE, (hh + 1) * _LANE)
                _, vjp = jax.vjp(_hgrn_chunk, q_ref[r, ln], f_ref[r, ln], i_ref[r, ln], g_ref[r, ln], st_ref[hh, ci],
                                 lb_ref[:, ln], ng_ref[:, ln], tri_v, trit_v)
                dq, df, di, dg, ds, dlb, dng = vjp((dog_ref[r, ln].astype(_F32), ds_scr[hh]))[:7]
                dq_ref[r, ln] = dq.astype(_BF)
                df_ref[r, ln] = df.astype(_BF)
                di_ref[r, ln] = di.astype(_BF)
                dg_ref[r, ln] = dg.astype(_BF)
                ds_scr[hh] = ds
                dlb_ref[:, ln] += dlb
                dng_ref[:, ln] += dng
            return carry

        lax.fori_loop(0, ncs, step, 0)

    return _call(body, name="hgrn_bwd", grid=(hw // wd, nseg),
                 in_specs=_hgrn_in_specs(seg, wd, w5, hw, rev) + [
                     vec, vec, pl.BlockSpec(tri.shape, lambda h, s: (0, 0)), pl.BlockSpec(tri_t.shape, lambda h, s: (0, 0)),
                     pl.BlockSpec((hp, ncs, _LANE, _LANE), lambda h, s: (h, rev(s), 0, 0)), col],
                 out_specs=[col, col, col, col, vec, vec],
                 out_shape=[_sds((t, hw), _BF)] * 4 + [_sds((1, hw), _F32)] * 2,
                 scratch_shapes=[pltpu.VMEM((hp, _LANE, _LANE), _F32)],
                 compiler_params=_cparams("parallel", "arbitrary"))(
                     proj, proj, proj, proj, lb, ng, tri, tri_t, st_all, dog)


def _lb_of(logits):
    mx = jnp.max(logits, axis=0, keepdims=True)
    e = jnp.exp(logits - mx)
    sm = e / jnp.sum(e, axis=0, keepdims=True)
    row = lax.broadcasted_iota(jnp.int32, logits.shape, 0)
    return jnp.sum(jnp.where(row == 0, sm, 0.0), axis=0, keepdims=True)


def _lb_prep(logits):
    def body(l_ref, o_ref):
        o_ref[...] = _lb_of(l_ref[...])

    return _call(body, name="lb_prep", out_shape=_sds((1, logits.shape[1]), _F32))(logits)


def _lb_prep_bwd(logits, dlb):
    def body(l_ref, d_ref, o_ref):
        _, vjp = jax.vjp(_lb_of, l_ref[...])
        o_ref[...] = vjp(d_ref[...])[0]

    return _call(body, name="lb_prep_bwd", out_shape=_sds(logits.shape, _F32))(logits, dlb)


def _merge_fwd(proj, ys, yh, t, d, gs_off):
    tm = _tile(t, 256, 8)
    w = _tile(d, 1024)
    nb = d // w

    def fn(gs, gh, a, b):
        return (_sigmoid(gs) * a + _sigmoid(gh) * b,)

    return _rowcall(fn, [proj, proj, ys, yh], [_rb(tm, w, gs_off // w), _rb(tm, w, gs_off // w + nb), _rb(tm, w), _rb(tm, w)],
                    [_sds((t, d), _BF)], [_rb(tm, w)], [False], grid=(nb, t // tm), name="merge_fwd")[0]


def _merge_bwd(proj, ys, yh, dm, t, d, gs_off):
    tm = _tile(t, 256, 8)
    w = _tile(d, 1024)
    nb = d // w

    def fn(gs, gh, a, b, g):
        g = g.astype(_F32)
        s1 = _sigmoid(gs)
        s2 = _sigmoid(gh)
        return g * s1, g * s2, g * a * s1 * (1.0 - s1), g * b * s2 * (1.0 - s2)

    return _rowcall(fn, [proj, proj, ys, yh, dm],
                    [_rb(tm, w, gs_off // w), _rb(tm, w, gs_off // w + nb), _rb(tm, w), _rb(tm, w), _rb(tm, w)],
                    [_sds((t, d), _BF)] * 4, [_rb(tm, w)] * 4, [False] * 4, grid=(nb, t // tm), name="merge_bwd")


_HALO = 16


def _prev_rows(up_prev, is_first):
    p1 = jnp.where(is_first, 0.0, up_prev[_HALO - 1:_HALO, :])
    p2 = jnp.where(is_first, 0.0, up_prev[_HALO - 2:_HALO - 1, :])
    return p1, p2


def _causal_taps(cur, p1, p2):
    row = lax.broadcasted_iota(jnp.int32, cur.shape, 0)
    s1 = jnp.where(row == 0, p1, _shift_down(cur, 1))
    s2 = jnp.where(row == 0, p2, jnp.where(row == 1, p1, _shift_down(cur, 2)))
    return s1, s2


def _pair_perm(nbh, per):
    def perm(n_):
        big = n_ // per
        return (2 * (big % nbh) + big // nbh) * per + n_ % per
    return perm


def _conv_specs(tm, w, nb_half, t):
    r8 = tm // _HALO
    cur_g = pl.BlockSpec((tm, w), lambda j, i: (i, 2 * j))
    cur_v = pl.BlockSpec((tm, w), lambda j, i: (i, 2 * j + 1))
    prev_g = pl.BlockSpec((_HALO, w), lambda j, i: (jnp.maximum(i * r8 - 1, 0), 2 * j))
    prev_v = pl.BlockSpec((_HALO, w), lambda j, i: (jnp.maximum(i * r8 - 1, 0), 2 * j + 1))
    w_g = pl.BlockSpec((3, w), lambda j, i: (0, j))
    w_v = pl.BlockSpec((3, w), lambda j, i: (0, nb_half + j))
    b_g = pl.BlockSpec((1, w), lambda j, i: (0, j))
    b_v = pl.BlockSpec((1, w), lambda j, i: (0, nb_half + j))
    return cur_g, cur_v, prev_g, prev_v, w_g, w_v, b_g, b_v


def _conv_of(cur, prev8, wt, bias, is_first):
    cur, prev8 = cur.astype(_F32), prev8.astype(_F32)
    p1, p2 = _prev_rows(prev8, is_first)
    s1, s2 = _causal_taps(cur, p1, p2)
    return bias + wt[0:1, :] * s2 + wt[1:2, :] * s1 + wt[2:3, :] * cur


def _convact_fwd(up, cw, cb, t, dff):
    tm = _tile(t, 512, _HALO)
    w = _tile(dff, 1408)
    nbh = dff // w
    sp = _conv_specs(tm, w, nbh, t)

    def body(ug, uv, pg, pv, wg, wv, bg, bv, o_ref):
        first = pl.program_id(1) == 0
        gate = _conv_of(ug[...], pg[...], wg[...], bg[...], first)
        val = _conv_of(uv[...], pv[...], wv[...], bv[...], first)
        o_ref[...] = (_silu(gate) * val).astype(_BF)

    return _call(body, name="convact_fwd", grid=(nbh, t // tm), in_specs=list(sp),
                 out_specs=pl.BlockSpec((tm, w), lambda j, i: (i, j)), out_shape=_sds((t, dff), _BF),
                 compiler_params=_cparams("parallel", "arbitrary"))(up, up, up, up, cw, cw, cb, cb)


def _convact_bwd(up, dact, cw, cb, t, dff):
    tm = _tile(t, 256, _HALO)
    w = _tile(dff, 1408)
    nbh = dff // w
    r8 = tm // _HALO
    nt = t // tm
    last8 = t // _HALO - 1

    def triple(off):
        return [pl.BlockSpec((tm, w), lambda j, i: (i, 2 * j + off)),
                pl.BlockSpec((_HALO, w), lambda j, i: (jnp.minimum((i + 1) * r8, last8), 2 * j + off)),
                pl.BlockSpec((_HALO, w), lambda j, i: (jnp.maximum(i * r8 - 1, 0), 2 * j + off))]

    def body(ug, ugn, ugp, uv, uvn, uvp, wg_ref, wv_ref, bg_ref, bv_ref, da_ref, dan_ref, du_ref, dw_ref, db_ref):
        i = pl.program_id(1)
        first = i == 0
        is_last = i == nt - 1
        gate = _conv_of(jnp.concatenate([ug[...], ugn[...]], axis=0), ugp[...], wg_ref[...], bg_ref[...], first)
        val = _conv_of(jnp.concatenate([uv[...], uvn[...]], axis=0), uvp[...], wv_ref[...], bv_ref[...], first)
        da = jnp.concatenate([da_ref[...], dan_ref[...]], axis=0).astype(_F32)
        row = lax.broadcasted_iota(jnp.int32, da.shape, 0)
        da = jnp.where(jnp.logical_and(row >= tm, is_last), 0.0, da)
        sg = _sigmoid(gate)
        halves = ((da * val * sg * (1.0 + gate * (1.0 - sg)), wg_ref, ug, ugp),
                  (da * gate * sg, wv_ref, uv, uvp))

        @pl.when(first)
        def _():
            dw_ref[...] = jnp.zeros_like(dw_ref)
            db_ref[...] = jnp.zeros_like(db_ref)

        for h, (dc, w_ref, u_ref, p_ref) in enumerate(halves):
            ln = slice(h * w, (h + 1) * w)
            wt = w_ref[...]
            du = wt[2:3, :] * dc + wt[1:2, :] * _shift_up(dc, 1) + wt[0:1, :] * _shift_up(dc, 2)
            du_ref[:, ln] = du[0:tm, :].astype(_BF)
            dcm = dc[0:tm, :]
            cur = u_ref[...].astype(_F32)
            p1, p2 = _prev_rows(p_ref[...].astype(_F32), first)
            s1, s2 = _causal_taps(cur, p1, p2)
            dw_ref[0:1, ln] += jnp.sum(dcm * s2, axis=0, keepdims=True)
            dw_ref[1:2, ln] += jnp.sum(dcm * s1, axis=0, keepdims=True)
            dw_ref[2:3, ln] += jnp.sum(dcm * cur, axis=0, keepdims=True)
            db_ref[:, ln] += jnp.sum(dcm, axis=0, keepdims=True)

    in_specs = (triple(0) + triple(1)
                + [pl.BlockSpec((3, w), lambda j, i: (0, j)), pl.BlockSpec((3, w), lambda j, i: (0, nbh + j)),
                   pl.BlockSpec((1, w), lambda j, i: (0, j)), pl.BlockSpec((1, w), lambda j, i: (0, nbh + j)),
                   pl.BlockSpec((tm, w), lambda j, i: (i, j)),
                   pl.BlockSpec((_HALO, w), lambda j, i: (jnp.minimum((i + 1) * r8, last8), j))])
    dup, dw_p, db_p = _call(
        body, name="convact_bwd", grid=(nbh, nt), in_specs=in_specs,
        out_specs=[pl.BlockSpec((tm, 2 * w), lambda j, i: (i, j)), pl.BlockSpec((3, 2 * w), lambda j, i: (0, j)),
                   pl.BlockSpec((1, 2 * w), lambda j, i: (0, j))],
        out_shape=[_sds((t, 2 * dff), _BF), _sds((3, 2 * dff), _F32), _sds((1, 2 * dff), _F32)],
        compiler_params=_cparams("parallel", "arbitrary"))(up, up, up, up, up, up, cw, cw, cb, cb, dact, dact)

    def natural(v):
        k = v.shape[0]
        return v.reshape(k, nbh, 2, w).transpose(0, 2, 1, 3).reshape(k, 2 * dff)

    return dup, natural(dw_p), natural(db_p)


def _me():
    return lax.axis_index("x"), lax.axis_index("y"), lax.axis_index("c")


def _other_chips(x, y):
    return [(1 - x, y), (x, 1 - y), (1 - x, 1 - y)]


def _rcopy(src, dst, ssem, rsem, dev):
    return pltpu.make_async_remote_copy(src_ref=src, dst_ref=dst, send_sem=ssem, recv_sem=rsem,
                                        device_id=dev, device_id_type=_MESH)


def _cast_into_slot(w, sel):
    r, c = w.shape
    tm = _tile(r, 256, 16)

    def body(sel_ref, w_ref, o_ref):
        o_ref[...] = w_ref[...].astype(_BF)

    gs = pltpu.PrefetchScalarGridSpec(
        num_scalar_prefetch=1, grid=(r // tm,),
        in_specs=[pl.BlockSpec((tm, c), lambda i, s: (i, 0))],
        out_specs=pl.BlockSpec((None, tm, c), lambda i, s: (s[0], i, 0)))
    return _call(body, name="cast_into_slot", grid_spec=gs, out_shape=_sds((_NCHIP, r, c), _BF),
                 compiler_params=_cparams("parallel"))(sel, w)


class _Plan:
    def __init__(self, ins, outs, aliases, sems, start, finish):
        self.ins, self.outs, self.aliases, self.sems, self.start, self.finish = ins, outs, aliases, sems, start, finish


def _run_plan(plan, name):
    ni, no = len(plan.ins), len(plan.outs)

    def body(*refs):
        rin, rout, sems = refs[:ni], refs[ni:ni + no], refs[ni + no:]
        plan.start(rin, rout, sems)
        plan.finish(rin, rout, sems)

    return _call(body, name=name, in_specs=[_ANY] * ni, out_specs=[_ANY] * no, out_shape=list(plan.outs),
                 input_output_aliases=dict(plan.aliases), scratch_shapes=list(plan.sems))(*plan.ins)


def _call_riding(body, rider, *, name, grid, in_specs, out_specs, out_shape, scratch_shapes, args):
    n_in, n_out, n_scr = len(in_specs), len(out_specs), len(scratch_shapes)
    n_rin, n_rout = len(rider.ins), len(rider.outs)

    def wrapped(*refs):
        ins, rin = refs[:n_in], refs[n_in:n_in + n_rin]
        o0 = n_in + n_rin
        outs, rout = refs[o0:o0 + n_out], refs[o0 + n_out:o0 + n_out + n_rout]
        s0 = o0 + n_out + n_rout
        scratch, sems = refs[s0:s0 + n_scr], refs[s0 + n_scr:]
        first = functools.reduce(jnp.logical_and, [pl.program_id(k) == 0 for k in range(len(grid))])
        last = functools.reduce(jnp.logical_and, [pl.program_id(k) == grid[k] - 1 for k in range(len(grid))])

        @pl.when(first)
        def _():
            rider.start(rin, rout, sems)

        body(*ins, *outs, *scratch)

        @pl.when(last)
        def _():
            rider.finish(rin, rout, sems)

    res = _call(wrapped, name=name, grid=grid, in_specs=list(in_specs) + [_ANY] * n_rin,
                out_specs=list(out_specs) + [_ANY] * n_rout, out_shape=list(out_shape) + list(rider.outs),
                input_output_aliases={n_in + k: n_out + v for k, v in rider.aliases.items()},
                scratch_shapes=list(scratch_shapes) + list(rider.sems),
                compiler_params=_cparams(*(["arbitrary"] * len(grid))))(*args, *rider.ins)
    return list(res[:n_out]), list(res[n_out:])


def _gather_plan(bufs, direct):
    n, nd = len(bufs), len(direct)

    def where():
        x, y, c = _me()
        return c, 2 * x + y, _other_chips(x, y), (x, y, 1 - c)

    def piece(outs, a, chip, h):
        r2 = bufs[a].shape[1] // 2
        return outs[a].at[chip, pl.ds(h * r2, r2)]

    def send(outs, sems, a, j, me, c, chip):
        return _rcopy(piece(outs, a, me, c), piece(outs, a, me, c), sems[0].at[3 * a + j], sems[1].at[3 * a + j],
                      (chip[0], chip[1], c))

    def forward(outs, sems, a, j, pc, c, sib):
        return _rcopy(piece(outs, a, pc, c), piece(outs, a, pc, c), sems[2].at[3 * a + j], sems[3].at[3 * a + j], sib)

    def dsend(dins, douts, sems, a, j, me, c, chip):
        return _rcopy(dins[a], douts[a].at[me], sems[4].at[3 * a + j], sems[5].at[3 * a + j], (chip[0], chip[1], c))

    def start(rin, rout, sems):
        outs, dins, douts = rout[:n], rin[n:], rout[n:]
        c, me, chips, _ = where()
        for a in range(n):
            for j, chip in enumerate(chips):
                send(outs, sems, a, j, me, c, chip).start()
        for a in range(nd):
            pltpu.make_async_copy(dins[a], douts[a].at[me], sems[6].at[a]).start()
            for j, chip in enumerate(chips):
                dsend(dins, douts, sems, a, j, me, c, chip).start()

    def finish(rin, rout, sems):
        outs, dins, douts = rout[:n], rin[n:], rout[n:]
        c, me, chips, sib = where()
        for a in range(n):
            for j, (cx, cy) in enumerate(chips):
                pc = 2 * cx + cy
                _rcopy(piece(outs, a, me, c), piece(outs, a, pc, c), sems[0].at[3 * a + j], sems[1].at[3 * a + j],
                       (cx, cy, c)).wait_recv()
                forward(outs, sems, a, j, pc, c, sib).start()
        for a in range(n):
            for j, (cx, cy) in enumerate(chips):
                pc = 2 * cx + cy
                _rcopy(piece(outs, a, pc, 1 - c), piece(outs, a, pc, 1 - c), sems[2].at[3 * a + j],
                       sems[3].at[3 * a + j], sib).wait_recv()
        for a in range(nd):
            for j, (cx, cy) in enumerate(chips):
                _rcopy(dins[a], douts[a].at[2 * cx + cy], sems[4].at[3 * a + j], sems[5].at[3 * a + j],
                       (cx, cy, c)).wait_recv()
        for a in range(n):
            for j, (cx, cy) in enumerate(chips):
                send(outs, sems, a, j, me, c, (cx, cy)).wait_send()
                forward(outs, sems, a, j, 2 * cx + cy, c, sib).wait_send()
        for a in range(nd):
            pltpu.make_async_copy(dins[a], douts[a].at[me], sems[6].at[a]).wait()
            for j, chip in enumerate(chips):
                dsend(dins, douts, sems, a, j, me, c, chip).wait_send()

    dma = pltpu.SemaphoreType.DMA
    return _Plan(list(bufs) + list(direct),
                 [_sds(b.shape, b.dtype) for b in bufs] + [_sds((_NCHIP,) + s.shape, s.dtype) for s in direct],
                 {a: a for a in range(n)},
                 [dma((3 * max(n, 1),)), dma((3 * max(n, 1),)), dma((3 * max(n, 1),)), dma((3 * max(n, 1),)),
                  dma((3 * max(nd, 1),)), dma((3 * max(nd, 1),)), dma((max(nd, 1),))], start, finish)


def _swap_halves(grads, name):
    n = len(grads)

    def body(*refs):
        ins, outs = refs[:n], refs[n:2 * n]
        ssem, rsem = refs[2 * n:]
        x, y, c = _me()
        sib = (x, y, 1 - c)
        cps = []
        for a in range(n):
            cp = _rcopy(ins[a].at[1 - c], outs[a], ssem.at[a], rsem.at[a], sib)
            cp.start()
            cps.append(cp)
        for cp in cps:
            cp.wait_recv()
        for cp in cps:
            cp.wait_send()

    dma = pltpu.SemaphoreType.DMA
    return _call(body, name=name, in_specs=[_ANY] * n, out_specs=[_ANY] * n,
                 out_shape=[_sds(g.shape[1:], g.dtype) for g in grads],
                 scratch_shapes=[dma((n,)), dma((n,))])(*grads)


def _add_pairs(grads, theirs, sel):
    _, s, r, c2 = grads.shape
    a3 = grads.reshape(2, s * r, c2)
    b2 = theirs.reshape(s * r, c2)
    tm = _tile(s * r, 512, 16)

    def body(sel_ref, a_ref, b_ref, o_ref):
        o_ref[...] = (a_ref[...].astype(_F32) + b_ref[...].astype(_F32)).astype(_BF)

    gs = pltpu.PrefetchScalarGridSpec(
        num_scalar_prefetch=1, grid=(s * r // tm,),
        in_specs=[pl.BlockSpec((None, tm, c2), lambda i, q: (q[1], i, 0)), pl.BlockSpec((tm, c2), lambda i, q: (i, 0))],
        out_specs=pl.BlockSpec((tm, c2), lambda i, q: (i, 0)))
    out = _call(body, name="chip_sum", grid_spec=gs, out_shape=_sds((s * r, c2), _BF),
                compiler_params=_cparams("parallel"))(sel, a3, b2)
    return out.reshape(s, r, c2)


def _exchange_plan(sums, small):
    n = len(sums)
    has_small = small is not None

    def where():
        x, y, c = _me()
        peers = [(1 - x if k & 4 else x, 1 - y if k & 2 else y, 1 - c if k & 1 else c) for k in range(1, _NDEV)]
        return c, 4 * x + 2 * y + c, _other_chips(x, y), peers

    def send(rin, rout, sems, a, j, c, chip):
        return _rcopy(rin[a].at[2 * chip[0] + chip[1]], rout[a].at[j], sems[0].at[3 * a + j], sems[1].at[3 * a + j],
                      (chip[0], chip[1], c))

    def small_send(rin, rout, sems, k, dev, peer):
        return _rcopy(rin[n], rout[n].at[dev], sems[2].at[k], sems[3].at[k], peer)

    def start(rin, rout, sems):
        c, dev, chips, peers = where()
        for a in range(n):
            for j, chip in enumerate(chips):
                send(rin, rout, sems, a, j, c, chip).start()
        if has_small:
            pltpu.make_async_copy(rin[n], rout[n].at[dev], sems[4].at[0]).start()
            for k, peer in enumerate(peers):
                small_send(rin, rout, sems, k, dev, peer).start()

    def finish(rin, rout, sems):
        c, dev, chips, peers = where()
        for a in range(n):
            for j, chip in enumerate(chips):
                send(rin, rout, sems, a, j, c, chip).wait_recv()
        if has_small:
            for k, (px, py, pc_) in enumerate(peers):
                _rcopy(rin[n], rout[n].at[4 * px + 2 * py + pc_], sems[2].at[k], sems[3].at[k], (px, py, pc_)).wait_recv()
        for a in range(n):
            for j, chip in enumerate(chips):
                send(rin, rout, sems, a, j, c, chip).wait_send()
        if has_small:
            pltpu.make_async_copy(rin[n], rout[n].at[dev], sems[4].at[0]).wait()
            for k, peer in enumerate(peers):
                small_send(rin, rout, sems, k, dev, peer).wait_send()

    dma = pltpu.SemaphoreType.DMA
    outs = [_sds((3,) + s.shape[1:], s.dtype) for s in sums]
    if has_small:
        outs.append(_sds((_NDEV,) + small.shape, small.dtype))
    return _Plan(list(sums) + ([small] if has_small else []), outs, {},
                 [dma((3 * max(n, 1),)), dma((3 * max(n, 1),)), dma((_NDEV - 1,)), dma((_NDEV - 1,)), dma((1,))],
                 start, finish)


def _shard_sum(sums, recv, sel):
    s, r, c2 = sums.shape
    tm = _tile(r, 256, 16)

    def body(sel_ref, own_ref, rc_ref, o_ref):
        rc = rc_ref[...]
        o_ref[...] = ((own_ref[...].astype(_F32) + rc[0].astype(_F32)) + rc[1].astype(_F32)) + rc[2].astype(_F32)

    gs = pltpu.PrefetchScalarGridSpec(
        num_scalar_prefetch=1, grid=(r // tm,),
        in_specs=[pl.BlockSpec((None, tm, c2), lambda i, q: (q[0], i, 0)),
                  pl.BlockSpec((3, tm, c2), lambda i, q: (0, i, 0))],
        out_specs=pl.BlockSpec((None, tm, c2), lambda i, q: (q[1], i, 0)))
    return _call(body, name="shard_sum", grid_spec=gs, out_shape=_sds((2, r, c2), _F32),
                 compiler_params=_cparams("parallel"))(sel, sums, recv)


def _sum_slots(stack, name):
    k, r, c = stack.shape
    tm = _tile(r, 256, 16 if stack.dtype == _BF else 8)

    def fn(v):
        out = v[0].astype(_F32)
        for i in range(1, k):
            out = out + v[i].astype(_F32)
        return (out,)

    return _rowcall(fn, [stack], [pl.BlockSpec((k, tm, c), lambda j, i: (0, i, 0))], [_sds((r, c), _F32)],
                    [_rb(tm, c)], [False], grid=(1, r // tm), name=name)[0]


def _share_halves(bufs):
    n = len(bufs)

    def body(*refs):
        outs = refs[n:2 * n]
        ssem, rsem = refs[2 * n:]
        x, y, c = _me()
        sib = (x, y, 1 - c)
        cps = []
        for a in range(n):
            cp = _rcopy(outs[a].at[c], outs[a].at[c], ssem.at[a], rsem.at[a], sib)
            cp.start()
            cps.append(cp)
        for a in range(n):
            _rcopy(outs[a].at[c], outs[a].at[1 - c], ssem.at[a], rsem.at[a], sib).wait_recv()
        for cp in cps:
            cp.wait_send()

    dma = pltpu.SemaphoreType.DMA
    return _call(body, name="share_halves", in_specs=[_ANY] * n, out_specs=[_ANY] * n,
                 out_shape=[_sds(b.shape, b.dtype) for b in bufs], input_output_aliases={a: a for a in range(n)},
                 scratch_shapes=[dma((n,)), dma((n,))])(*bufs)


def _adamw_math(w, g, m, v):
    m = _B1 * m + (1.0 - _B1) * g
    v = _B2 * v + (1.0 - _B2) * jnp.square(g)
    m_hat = m / (1.0 - _B1 ** _STEP)
    v_hat = v / (1.0 - _B2 ** _STEP)
    delta = -_LR * (m_hat / (jnp.sqrt(v_hat) + _ADAM_EPS) + _WD * w)
    return delta, m, v


def _adamw_shard(w, g2, m, v, name):
    r, c = w.shape
    c2 = c // 2
    tm = _tile(r, 256, 8)
    blk = pl.BlockSpec((tm, c2), lambda h, i: (i, h))

    def fn(wb, gb, mb, vb):
        return (gb,) + _adamw_math(wb, gb, mb, vb)

    return _rowcall(fn, [w, g2, m, v], [blk, pl.BlockSpec((None, tm, c2), lambda h, i: (h, i, 0)), blk, blk],
                    [_sds((r, c), _F32)] * 4, [blk] * 4, [False] * 4, grid=(2, r // tm), name=name)


def _adamw_whole(w, g, m, v, name):
    r, c = w.shape
    blk = _full((r, c))
    return _rowcall(lambda *a: _adamw_math(*a), [w, g, m, v], [blk] * 4, [_sds((r, c), _F32)] * 3, [blk] * 3,
                    [False] * 3, grid=(1, 1), name=name)


def _adamw_many(ws, g_pack, ms, vs):
    k = len(ws)
    views, offs, off = [], [], 0
    for w in ws:
        n = w.size
        views.append((n // _LANE, _LANE) if n % _LANE == 0 else (1, n))
        offs.append(off)
        off += (n + (-n) % (8 * _LANE)) // _LANE

    def body(*refs):
        g_ref, w_refs, m_refs, v_refs = refs[0], refs[1:1 + k], refs[1 + k:1 + 2 * k], refs[1 + 2 * k:1 + 3 * k]
        outs = refs[1 + 3 * k:]
        for i in range(k):
            r, c = views[i]
            g = g_ref[offs[i]:offs[i] + r, 0:c]
            res = (g,) + _adamw_math(w_refs[i][...], g, m_refs[i][...], v_refs[i][...])
            for o_ref, val in zip(outs[4 * i:4 * i + 4], res):
                o_ref[...] = val

    args = [g_pack] + [a.reshape(views[i]) for grp in (ws, ms, vs) for i, a in enumerate(grp)]
    res = _call(body, name="adamw_small", out_shape=[_sds(views[i], _F32) for i in range(k) for _ in range(4)])(*args)
    return [[res[4 * i + j].reshape(ws[i].shape) for j in range(4)] for i in range(k)]


def _pack(arrs):
    parts = []
    for a in arrs:
        f = a.reshape(-1).astype(_F32)
        pad = (-f.shape[0]) % (8 * _LANE)
        if pad:
            f = jnp.concatenate([f, jnp.zeros((pad,), _F32)])
        parts.append(f)
    return jnp.concatenate(parts).reshape(-1, _LANE)


def kernel(x, ln_mix_g, w_in, s5_a_re, s5_a_im, s5_log_dt, s5_b_re, s5_b_im, s5_c_re, s5_c_im, s5_d, s5_w_glu, s5_b_glu, w_proj_s5, hgrn_lb_logits, hgrn_norm_g, w_proj_hgrn, w_out, ln_ffn_g, w_up, conv_w, conv_b, w_down, ln_final_g, loss_target, m_ln_mix_g, m_w_in, m_s5_a_re, m_s5_a_im, m_s5_log_dt, m_s5_b_re, m_s5_b_im, m_s5_c_re, m_s5_c_im, m_s5_d, m_s5_w_glu, m_s5_b_glu, m_w_proj_s5, m_hgrn_lb_logits, m_hgrn_norm_g, m_w_proj_hgrn, m_w_out, m_ln_ffn_g, m_w_up, m_conv_w, m_conv_b, m_w_down, m_ln_final_g, v_ln_mix_g, v_w_in, v_s5_a_re, v_s5_a_im, v_s5_log_dt, v_s5_b_re, v_s5_b_im, v_s5_c_re, v_s5_c_im, v_s5_d, v_s5_w_glu, v_s5_b_glu, v_w_proj_s5, v_hgrn_lb_logits, v_hgrn_norm_g, v_w_proj_hgrn, v_w_out, v_ln_ffn_g, v_w_up, v_conv_w, v_conv_b, v_w_down, v_ln_final_g):
    assert x.shape[0] == 1 and w_in.shape[0] == 1, "one example per device, one layer"
    t, d = x.shape[1], x.shape[2]
    w5 = s5_w_glu.shape[2]
    hw = hgrn_norm_g.shape[1]
    ng_, np_, gc = s5_b_re.shape[1], s5_b_re.shape[2], s5_b_re.shape[3]
    dff = w_down.shape[1] * _NCHIP
    assert gc * _S5_SET == _LANE and ng_ * gc == w5 and hw % _LANE == 0
    gs_off = w5 + 4 * hw
    ci = lax.axis_index("c")
    xt = x.reshape(t, d)
    tgt = loss_target.reshape(t, d)

    big_names = ["w_in", "s5_w_glu", "w_proj_s5", "w_proj_hgrn", "w_out", "w_up", "w_down"]
    big_w = dict(w_in=w_in[0], s5_w_glu=s5_w_glu[0], w_proj_s5=w_proj_s5[0], w_proj_hgrn=w_proj_hgrn[0],
                 w_out=w_out[0], w_up=w_up[0], w_down=w_down[0])
    chip = 2 * lax.axis_index("x") + lax.axis_index("y")
    sel_chip = jnp.stack([chip, ci]).astype(jnp.int32)
    slots = {k: _cast_into_slot(big_w[k], sel_chip) for k in big_names}
    g_in, g_cw = _run_plan(_gather_plan([slots["w_in"]], [conv_w[0]]), "gather_w_in")
    cw = g_cw.transpose(1, 0, 2).reshape(3, 2 * dff)
    cb = conv_b

    tm_big = _tile(t, 1024, 8)

    h1 = _rms_fwd(xt, ln_mix_g, "rms1_fwd")
    nin_s = g_in.shape[2]
    proj, (g_glu, g_ps5, g_ph, g_out) = _mm(
        h1, g_in, "nn", _F32, tm=tm_big, tn=_tile(nin_s, 1152), tk=d, name="mm_proj",
        rider=_gather_plan([slots[k] for k in ("s5_w_glu", "w_proj_s5", "w_proj_hgrn", "w_out")], []))
    wglu = g_glu.reshape(w5, w5)
    wout = g_out.reshape(d, d)

    abar_r, abar_i, coef_r, coef_i = _s5_prep(s5_a_re[0], s5_a_im[0], s5_log_dt.reshape(ng_, 1))
    lanes = ng_ * np_
    par = (abar_r.reshape(1, lanes), abar_i.reshape(1, lanes), coef_r.reshape(1, lanes), coef_i.reshape(1, lanes),
           _bd_in(s5_b_re[0]), _bd_in(s5_b_im[0]), _bd_out(s5_c_re[0]), -_bd_out(s5_c_im[0]), s5_d.reshape(1, w5))
    (y_s5, car, cai), (g_up,) = _s5_fwd(proj, par, t, w5, rider=_gather_plan([slots["w_up"]], []))
    z = _s5glu_fwd(y_s5, wglu, s5_b_glu)
    ys = _mm(z, g_ps5, "nn", _F32, tm=tm_big, tn=g_ps5.shape[2], tk=w5, name="mm_proj_s5")

    lb = _lb_prep(hgrn_lb_logits)
    og, st_all = _hgrn_fwd(proj, lb, hgrn_norm_g, t, w5, hw)
    yh = _mm(og, g_ph, "nn", _F32, tm=tm_big, tn=g_ph.shape[2], tk=hw, name="mm_proj_hgrn")

    merged = _merge_fwd(proj, ys, yh, t, d, gs_off)
    x2 = _mm(merged, wout, "nn", _F32, tm=tm_big, tn=_tile(d, 1024), tk=d, res=xt, name="mm_out")
    h2 = _rms_fwd(x2, ln_ffn_g, "rms2_fwd")
    up_s = g_up.shape[2]
    w_conv = _tile(dff, 1408)
    nbh = dff // w_conv
    tn_up = _tile(up_s, 1408)
    tk_dh2 = _tile(up_s, w_conv)
    tn_gwup = _tile(up_s // 2, 1408)
    assert w_conv % tn_up == 0 and w_conv % tk_dh2 == 0 and w_conv % tn_gwup == 0
    up, (g_down,) = _mm(h2, g_up, "nn", _BF, tm=tm_big, tn=tn_up, tk=d, name="mm_up",
                        colperm=_pair_perm(nbh, w_conv // tn_up), rider=_gather_plan([slots["w_down"]], []))
    wdown = g_down.reshape(dff, d)
    act = _convact_fwd(up, cw, cb, t, dff)
    x3 = _mm(act, wdown, "nn", _F32, tm=tm_big, tn=_tile(d, 1024), tk=None, res=x2, name="mm_down")
    loss_part, dx3, dx3b, d_gfin = _loss_head(x3, tgt, ln_final_g.reshape(1, d))

    dact = _mm(dx3b, wdown, "nt", _BF, tm=tm_big, tn=_tile(dff, 1408), tk=d, name="mm_dact")
    r_down = dff // _NCHIP
    gw_down = _mm(act, dx3b, "tn", _BF, tm=_tile(r_down, 1408), tn=_tile(d // 2, 1024), tk=None, halves="rows",
                  name="mm_gw_down")
    def chip_sums(grads, name):
        theirs = _swap_halves(grads, name)
        return [_add_pairs(g, th, sel_chip) for g, th in zip(grads, theirs)]

    (s_down,) = chip_sums([gw_down], "swap_halves_d")
    dup, d_cw, d_cb = _convact_bwd(up, dact, cw, cb, t, dff)
    dh2, (r_down,) = _mm(dup, g_up, "nt", _BF, tm=tm_big, tn=_tile(d, 1024), tk=tk_dh2, name="mm_dh2",
                         colperm=_pair_perm(nbh, w_conv // tk_dh2), rider=_exchange_plan([s_down], None))
    gw_up = _mm(h2, dup, "tn", _BF, tm=_tile(d, 1024), tn=tn_gwup, tk=None, halves="cols", name="mm_gw_up",
                colperm=_pair_perm(nbh, w_conv // tn_gwup))
    dx2, dx2b, d_gffn = _rms_bwd(x2, ln_ffn_g, dh2, dx3, "rms2_bwd")
    dmerged = _mm(dx2b, wout, "nt", _BF, tm=tm_big, tn=_tile(d, 1024), tk=d, name="mm_dmerged")
    gw_out = _mm(merged, dx2b, "tn", _BF, tm=_tile(d // _NCHIP, 1024), tn=_tile(d // 2, 1024), tk=None, halves="rows",
                 name="mm_gw_out")
    dys, dyh, dgs, dgh = _merge_bwd(proj, ys, yh, dmerged, t, d, gs_off)
    ps_s = g_ps5.shape[2]
    dz = _mm(dys, g_ps5, "nt", _BF, tm=tm_big, tn=_tile(w5, 1024), tk=ps_s, name="mm_dz")
    gw_ps5 = _mm(z, dys, "tn", _BF, tm=_tile(w5, 1024), tn=ps_s // 2, tk=None, halves="cols", name="mm_gw_ps5")
    dog = _mm(dyh, g_ph, "nt", _BF, tm=tm_big, tn=_tile(hw, 1024), tk=ps_s, name="mm_dog")
    gw_ph = _mm(og, dyh, "tn", _BF, tm=_tile(hw, 1024), tn=ps_s // 2, tk=None, halves="cols", name="mm_gw_ph")
    dy_s5, gw_glu_full, d_bglu = _s5glu_bwd(y_s5, dz, wglu, s5_b_glu)
    r_glu = w5 // _NCHIP
    gw_glu = gw_glu_full.astype(_BF).reshape(_NCHIP, r_glu, 2, w5 // 2).transpose(2, 0, 1, 3)

    s_glu, s_ps5, s_ph, s_out, s_up = chip_sums([gw_glu, gw_ps5, gw_ph, gw_out, gw_up], "swap_halves_a")
    s5g, (r_glu_, r_ps5, r_ph, r_out, r_up) = _s5_bwd(
        proj, dy_s5, car, cai, par, t, w5, rider=_exchange_plan([s_glu, s_ps5, s_ph, s_out, s_up], None))
    du = s5g[0]
    dq, df, di, dg, d_lb, d_ng = _hgrn_bwd(proj, lb, hgrn_norm_g, st_all, dog, t, w5, hw)
    dproj = jnp.concatenate([du, dq, df, di, dg, dgs, dgh], axis=1)

    d_are, d_aim, d_ldt = _s5_prep_bwd(s5_a_re[0], s5_a_im[0], s5_log_dt.reshape(ng_, 1),
                                       [s5g[k].reshape(ng_, np_) for k in (1, 2, 3, 4)])
    d_bre = _bd_in_grad(s5g[5], np_, gc)
    d_bim = _bd_in_grad(s5g[6], np_, gc)
    d_cre = _bd_out_grad(s5g[7], np_, gc)
    d_cim = -_bd_out_grad(s5g[8], np_, gc)
    d_logits = _lb_prep_bwd(hgrn_lb_logits, d_lb)

    small_names = ["s5_a_re", "s5_a_im", "s5_log_dt", "s5_b_re", "s5_b_im", "s5_c_re", "s5_c_im", "s5_d",
                   "s5_b_glu", "hgrn_lb_logits", "hgrn_norm_g", "ln_ffn_g", "conv_b", "ln_final_g", "ln_mix_g"]
    small_w = dict(ln_mix_g=ln_mix_g, s5_a_re=s5_a_re, s5_a_im=s5_a_im, s5_log_dt=s5_log_dt, s5_b_re=s5_b_re,
                   s5_b_im=s5_b_im, s5_c_re=s5_c_re, s5_c_im=s5_c_im, s5_d=s5_d, s5_b_glu=s5_b_glu,
                   hgrn_lb_logits=hgrn_lb_logits, hgrn_norm_g=hgrn_norm_g, ln_ffn_g=ln_ffn_g, conv_b=conv_b,
                   ln_final_g=ln_final_g)
    small_m = dict(ln_mix_g=m_ln_mix_g, s5_a_re=m_s5_a_re, s5_a_im=m_s5_a_im, s5_log_dt=m_s5_log_dt, s5_b_re=m_s5_b_re,
                   s5_b_im=m_s5_b_im, s5_c_re=m_s5_c_re, s5_c_im=m_s5_c_im, s5_d=m_s5_d, s5_b_glu=m_s5_b_glu,
                   hgrn_lb_logits=m_hgrn_lb_logits, hgrn_norm_g=m_hgrn_norm_g, ln_ffn_g=m_ln_ffn_g, conv_b=m_conv_b,
                   ln_final_g=m_ln_final_g)
    small_v = dict(ln_mix_g=v_ln_mix_g, s5_a_re=v_s5_a_re, s5_a_im=v_s5_a_im, s5_log_dt=v_s5_log_dt, s5_b_re=v_s5_b_re,
                   s5_b_im=v_s5_b_im, s5_c_re=v_s5_c_re, s5_c_im=v_s5_c_im, s5_d=v_s5_d, s5_b_glu=v_s5_b_glu,
                   hgrn_lb_logits=v_hgrn_lb_logits, hgrn_norm_g=v_hgrn_norm_g, ln_ffn_g=v_ln_ffn_g, conv_b=v_conv_b,
                   ln_final_g=v_ln_final_g)
    small_g = dict(s5_a_re=d_are, s5_a_im=d_aim, s5_log_dt=d_ldt, s5_b_re=d_bre, s5_b_im=d_bim,
                   s5_c_re=d_cre, s5_c_im=d_cim, s5_d=s5g[9], s5_b_glu=d_bglu, hgrn_lb_logits=d_logits,
                   hgrn_norm_g=d_ng, ln_ffn_g=d_gffn, conv_b=d_cb, ln_final_g=d_gfin)
    like = [small_w[k] for k in small_names]
    assert small_names[-1] == "ln_mix_g"
    pack_a = _pack([small_g[k] for k in small_names[:-1]] + [d_cw])
    gw_in, (r_small_a,) = _mm(h1, dproj, "tn", _BF, tm=_tile(d, 1024), tn=_tile(nin_s // 2, 1152), tk=None, halves="cols",
                              name="mm_gw_in", rider=_exchange_plan([], pack_a))
    (s_in,) = chip_sums([gw_in], "swap_halves_b")
    dh1, (r_in,) = _mm(dproj, g_in, "nt", _BF, tm=tm_big, tn=_tile(d, 1024), tk=None, name="mm_dh1",
                       rider=_exchange_plan([s_in], None))
    dx, _, d_gmix = _rms_bwd(xt, ln_mix_g, dh1, dx2, "rms1_bwd")
    (r_small_b,) = _run_plan(_exchange_plan([], _pack([d_gmix])), "exchange_gmix")
    sums = [s_in, s_glu, s_ps5, s_ph, s_out, s_up, s_down]
    received = [r_in, r_glu_, r_ps5, r_ph, r_out, r_up, r_down]
    halves = [_shard_sum(sm, rc, sel_chip) for sm, rc in zip(sums, received)]
    g_a = _sum_slots(r_small_a, "small_sum_a")
    g_b = _sum_slots(r_small_b, "small_sum_b")
    full = _share_halves(halves)
    w_pack = _pack(like)
    rows_a = w_pack.shape[0] - g_b.shape[0]
    g_small = jnp.concatenate([g_a[:rows_a], g_b], axis=0)
    cs = conv_w.shape[2]
    g_cw_full = g_a[rows_a:].reshape(-1)[:3 * 2 * dff].reshape(3, 2 * dff)
    g_cw = lax.dynamic_slice_in_dim(g_cw_full, chip * cs, cs, axis=1)

    big_m = dict(w_in=m_w_in, s5_w_glu=m_s5_w_glu, w_proj_s5=m_w_proj_s5, w_proj_hgrn=m_w_proj_hgrn, w_out=m_w_out,
                 w_up=m_w_up, w_down=m_w_down)
    big_v = dict(w_in=v_w_in, s5_w_glu=v_s5_w_glu, w_proj_s5=v_w_proj_s5, w_proj_hgrn=v_w_proj_hgrn, w_out=v_w_out,
                 w_up=v_w_up, w_down=v_w_down)
    res = {}
    for k, g2 in zip(big_names, full):
        w2 = big_w[k]
        shp = (1,) + w2.shape
        outs = _adamw_shard(w2, g2, big_m[k][0], big_v[k][0], "adamw_" + k)
        res[k] = [o.reshape(shp) for o in outs]
    sm_outs = _adamw_many(like, g_small, [small_m[k] for k in small_names], [small_v[k] for k in small_names])
    for k, outs in zip(small_names, sm_outs):
        res[k] = outs

    cw_outs = _adamw_whole(conv_w[0], g_cw, m_conv_w[0], v_conv_w[0], "adamw_conv_w")
    res["conv_w"] = [g_cw.reshape(conv_w.shape)] + [o.reshape(conv_w.shape) for o in cw_outs]

    loss = lax.psum(loss_part[0, 0], ("x", "y", "c"))
    order = ["ln_mix_g", "w_in", "s5_a_re", "s5_a_im", "s5_log_dt", "s5_b_re", "s5_b_im", "s5_c_re", "s5_c_im", "s5_d",
             "s5_w_glu", "s5_b_glu", "w_proj_s5", "hgrn_lb_logits", "hgrn_norm_g", "w_proj_hgrn", "w_out", "ln_ffn_g",
             "w_up", "conv_w", "conv_b", "w_down", "ln_final_g"]
    return (loss, dx.reshape(x.shape), *[res[k][0] for k in order], *[res[k][1] for k in order],
            *[res[k][2] for k in order], *[res[k][3] for k in order])
```

```python
import functools

import jax
import jax.numpy as jnp
from jax import lax
from jax.experimental import pallas as pl
from jax.experimental.pallas import tpu as pltpu

_F32 = jnp.float32
_BF = jnp.bfloat16
_RMS_EPS = 1e-6
_S5_MAX_RE = -1e-4
_LR, _B1, _B2, _ADAM_EPS, _WD, _STEP = 0.001, 0.9, 0.999, 1e-08, 0.01, 10
_MESH = pl.DeviceIdType.MESH
_ANY = pl.BlockSpec(memory_space=pl.ANY)
_LANE = 128
_VMEM_LIMIT = 56 * 1024 * 1024
_CHUNK = 64
_S5_TB = 512
_S5_SET = 8
_HGRN_HP = 8
_HGRN_SEG = 512
_NCHIP = 4
_NDEV = 8


def _call(body, **kw):
    return pl.pallas_call(body, **kw)


def _cparams(*sem):
    return pltpu.CompilerParams(dimension_semantics=sem, vmem_limit_bytes=_VMEM_LIMIT)


def _tile(n, pref, unit=_LANE):
    if n <= pref:
        return n
    t = (pref // unit) * unit
    while t >= unit:
        if n % t == 0:
            return t
        t -= unit
    raise ValueError(f"no tile for {n}")


_OPERAND_BYTES = 12 * 1024 * 1024


def _tk_fit(k, tm, tn):
    best = _LANE
    for tk in range(_LANE, k + 1, _LANE):
        if k % tk == 0 and (tm + tn) * tk * 2 <= _OPERAND_BYTES:
            best = tk
    return best if k % _LANE == 0 else k


_NN = ((1,), (0,))
_NT = ((1,), (1,))
_TN = ((0,), (0,))


def _dg(a, b, dims):
    return lax.dot_general(a.astype(_BF), b.astype(_BF), (dims, ((), ())), preferred_element_type=_F32)


@jax.custom_vjp
def _bdot(a, b):
    return _dg(a, b, _NN)


def _bdot_f(a, b):
    return _dg(a, b, _NN), (a, b)


def _bdot_b(res, g):
    a, b = res
    return _dg(g, b, _NT).astype(a.dtype), _dg(a, g, _TN).astype(b.dtype)


_bdot.defvjp(_bdot_f, _bdot_b)


@jax.custom_vjp
def _bdot_nt(a, b):
    return _dg(a, b, _NT)


def _bdot_nt_f(a, b):
    return _dg(a, b, _NT), (a, b)


def _bdot_nt_b(res, g):
    a, b = res
    return _dg(g, b, _NN).astype(a.dtype), _dg(g, a, _TN).astype(b.dtype)


_bdot_nt.defvjp(_bdot_nt_f, _bdot_nt_b)


@jax.custom_vjp
def _bdot_tn(a, b):
    return _dg(a, b, _TN)


def _bdot_tn_f(a, b):
    return _dg(a, b, _TN), (a, b)


def _bdot_tn_b(res, g):
    a, b = res
    return _dg(b, g, _NT).astype(a.dtype), _dg(a, g, _NN).astype(b.dtype)


_bdot_tn.defvjp(_bdot_tn_f, _bdot_tn_b)


_SUBLANES = 8


def _shift_up(x, n):
    r = x.shape[0]
    if n % _SUBLANES == 0:
        return jnp.concatenate([x[n:], jnp.zeros((n,) + x.shape[1:], x.dtype)], axis=0)
    row = lax.broadcasted_iota(jnp.int32, x.shape, 0)
    return jnp.where(row < r - n, pltpu.roll(x, r - n, 0), 0.0)


@functools.partial(jax.custom_vjp, nondiff_argnums=(1,))
def _shift_down(x, n):
    if n % _SUBLANES == 0:
        return jnp.concatenate([jnp.zeros((n,) + x.shape[1:], x.dtype), x[:x.shape[0] - n]], axis=0)
    row = lax.broadcasted_iota(jnp.int32, x.shape, 0)
    return jnp.where(row >= n, pltpu.roll(x, n, 0), 0.0)


def _shift_down_f(x, n):
    return _shift_down(x, n), None


def _shift_down_b(n, _, g):
    return (_shift_up(g, n),)


_shift_down.defvjp(_shift_down_f, _shift_down_b)


def _rows_apart(x):
    return tuple(x[k:k + _SUBLANES] for k in range(0, x.shape[0], _SUBLANES))


@jax.custom_vjp
def _split_rows(x):
    return _rows_apart(x)


_split_rows.defvjp(lambda x: (_rows_apart(x), None), lambda _, gs: (jnp.concatenate(gs, axis=0),))


@jax.custom_vjp
def _join_rows(pieces):
    return jnp.concatenate(pieces, axis=0)


_join_rows.defvjp(lambda pieces: (jnp.concatenate(pieces, axis=0), None), lambda _, g: (_rows_apart(g),))


@jax.custom_vjp
def _last_row(x):
    return x[_SUBLANES - 1:_SUBLANES]


def _last_row_b(_, g):
    row = lax.broadcasted_iota(jnp.int32, (_SUBLANES, g.shape[1]), 0)
    return (jnp.where(row == _SUBLANES - 1, g, 0.0),)


_last_row.defvjp(lambda x: (x[_SUBLANES - 1:_SUBLANES], None), _last_row_b)


def _sigmoid(x):
    return 1.0 / (1.0 + jnp.exp(-x))


def _silu(x):
    return x * _sigmoid(x)


def _gelu(x):
    return 0.5 * x * (1.0 + jnp.tanh(0.7978845608028654 * (x + 0.044715 * (x * x * x))))


def _rms_core(x, g):
    return x * lax.rsqrt(jnp.mean(x * x, axis=-1, keepdims=True) + _RMS_EPS) * g


def _mm(a, b, mode, out_dtype, *, tm, tn, tk, res=None, halves=None, rider=None, colperm=None, name):
    if colperm is None:
        def colperm(n_):
            return n_
    if tk is None:
        kdim = a.shape[0] if mode == "tn" else (b.shape[2] if (mode == "nt" and b.ndim == 3) else a.shape[1])
        tk = _tk_fit(kdim, tm, tn)
    if mode == "nn":
        m, k = a.shape
        a_spec = pl.BlockSpec((tm, tk), lambda i, j, kk: (i, kk))
        if b.ndim == 3:
            s, _, ns = b.shape
            n = s * ns
            npb = ns // tn
            b_spec = pl.BlockSpec((None, tk, tn), lambda i, j, kk: (j // npb, kk, j % npb))
        else:
            n = b.shape[1]
            b_spec = pl.BlockSpec((tk, tn), lambda i, j, kk: (kk, j))
        dims = _NN
    elif mode == "nt":
        m, k = a.shape
        a_spec = pl.BlockSpec((tm, tk), lambda i, j, kk: (i, colperm(kk)))
        if b.ndim == 3:
            s, n, ks = b.shape
            kpb = ks // tk
            b_spec = pl.BlockSpec((None, tn, tk), lambda i, j, kk: (kk // kpb, j, kk % kpb))
        else:
            n = b.shape[0]
            b_spec = pl.BlockSpec((tn, tk), lambda i, j, kk: (j, kk))
        dims = _NT
    else:
        k, m = a.shape
        n = b.shape[1]
        a_spec = pl.BlockSpec((tk, tm), lambda i, j, kk: (kk, i))
        b_spec = pl.BlockSpec((tk, tn), lambda i, j, kk: (kk, colperm(j)))
        dims = _TN
    nk = k // tk
    if halves is None:
        out_shape = jax.ShapeDtypeStruct((m, n), out_dtype)
        out_spec = pl.BlockSpec((tm, tn), lambda i, j, kk: (i, colperm(j) if mode == "nn" else j))
    elif halves == "cols":
        c2 = n // (2 * _NCHIP)
        tpc = c2 // tn
        out_shape = jax.ShapeDtypeStruct((2, _NCHIP, m, c2), out_dtype)
        out_spec = pl.BlockSpec((None, None, tm, tn),
                                lambda i, j, kk: ((j // tpc) % 2, j // (2 * tpc), i, j % tpc))
    else:
        c2 = n // 2
        tpc = c2 // tn
        r = m // _NCHIP
        tpr = r // tm
        out_shape = jax.ShapeDtypeStruct((2, _NCHIP, r, c2), out_dtype)
        out_spec = pl.BlockSpec((None, None, tm, tn),
                                lambda i, j, kk: (j // tpc, i // tpr, i % tpr, j % tpc))
    has_res = res is not None
    nreg = 3 if has_res else 2
    ni, nj = m // tm, n // tn
    r_ins = list(rider.ins) if rider else []
    r_outs = list(rider.outs) if rider else []

    def body(*refs):
        a_ref, b_ref = refs[0], refs[1]
        r_ref = refs[2] if has_res else None
        rin = refs[nreg:nreg + len(r_ins)]
        o_ref = refs[nreg + len(r_ins)]
        rout = refs[nreg + len(r_ins) + 1:nreg + len(r_ins) + 1 + len(r_outs)]
        acc_ref = refs[nreg + len(r_ins) + 1 + len(r_outs)]
        sems = refs[nreg + len(r_ins) + 2 + len(r_outs):]
        i, j, kk = pl.program_id(0), pl.program_id(1), pl.program_id(2)

        if rider:
            @pl.when(jnp.logical_and(jnp.logical_and(i == 0, j == 0), kk == 0))
            def _():
                rider.start(rin, rout, sems)

        @pl.when(kk == 0)
        def _():
            acc_ref[...] = jnp.zeros_like(acc_ref)

        acc_ref[...] += _dg(a_ref[...], b_ref[...], dims)

        @pl.when(kk == nk - 1)
        def _():
            out = acc_ref[...]
            if has_res:
                out = out + r_ref[...]
            o_ref[...] = out.astype(out_dtype)

        if rider:
            @pl.when(jnp.logical_and(jnp.logical_and(i == ni - 1, j == nj - 1), kk == nk - 1))
            def _():
                rider.finish(rin, rout, sems)

    in_specs = [a_spec, b_spec]
    args = [a, b]
    if has_res:
        in_specs.append(pl.BlockSpec((tm, tn), lambda i, j, kk: (i, j)))
        args.append(res)
    if not rider:
        return _call(body, name=name, grid=(ni, nj, nk), in_specs=in_specs, out_specs=out_spec,
                     out_shape=out_shape, scratch_shapes=[pltpu.VMEM((tm, tn), _F32)],
                     compiler_params=_cparams("parallel", "parallel", "arbitrary"))(*args)
    res_all = _call(body, name=name, grid=(ni, nj, nk), in_specs=in_specs + [_ANY] * len(r_ins),
                    out_specs=[out_spec] + [_ANY] * len(r_outs), out_shape=[out_shape] + r_outs,
                    input_output_aliases={nreg + k: 1 + v for k, v in rider.aliases.items()},
                    scratch_shapes=[pltpu.VMEM((tm, tn), _F32)] + list(rider.sems),
                    compiler_params=_cparams("arbitrary", "arbitrary", "arbitrary"))(*args, *r_ins)
    return res_all[0], list(res_all[1:])


def _rowcall(fn, ins, in_specs, outs, out_specs, acc, *, grid, name):
    nin = len(ins)

    def body(*refs):
        vals = fn(*[r[...] for r in refs[:nin]])
        first = pl.program_id(1) == 0
        for k, (o_ref, v) in enumerate(zip(refs[nin:], vals)):
            if acc[k]:
                @pl.when(first)
                def _(o_ref=o_ref):
                    o_ref[...] = jnp.zeros_like(o_ref)
                o_ref[...] += v.astype(o_ref.dtype)
            else:
                o_ref[...] = v.astype(o_ref.dtype)

    return _call(body, name=name, grid=grid, in_specs=in_specs, out_specs=out_specs, out_shape=outs,
                 compiler_params=_cparams("parallel", "arbitrary"))(*ins)


def _rb(tm, w, cb=0):
    return pl.BlockSpec((tm, w), lambda j, i: (i, cb + j))


def _cb(w, cb=0):
    return pl.BlockSpec((1, w), lambda j, i: (0, cb + j))


def _full(shape):
    nd = len(shape)
    return pl.BlockSpec(shape, lambda j, i: (0,) * nd)


def _sds(shape, dtype):
    return jax.ShapeDtypeStruct(shape, dtype)


def _rms_fwd(x, g, name):
    t, d = x.shape
    tm = _tile(t, 512, 8)
    return _rowcall(lambda xb, gb: (_rms_core(xb, gb),), [x, g], [_rb(tm, d), _full((1, d))],
                    [_sds((t, d), _BF)], [_rb(tm, d)], [False], grid=(1, t // tm), name=name)[0]


def _rms_bwd(x, g, dh, dres, name):
    t, d = x.shape
    tm = _tile(t, 256, 8)

    def fn(xb, gb, dhb, drb):
        _, vjp = jax.vjp(_rms_core, xb, gb)
        dx, dg = vjp(dhb.astype(_F32))
        dx = dx + drb
        return dx, dx, dg

    return _rowcall(fn, [x, g, dh, dres], [_rb(tm, d), _full((1, d)), _rb(tm, d), _rb(tm, d)],
                    [_sds((t, d), _F32), _sds((t, d), _BF), _sds((1, d), _F32)],
                    [_rb(tm, d), _rb(tm, d), _full((1, d))], [False, False, True],
                    grid=(1, t // tm), name=name)


def _loss_head(x3, tgt, g):
    t, d = x3.shape
    tm = _tile(t, 256, 8)

    def fn(xb, tb, gb):
        y, vjp = jax.vjp(_rms_core, xb, gb)
        e = y - tb
        part = 0.5 * jnp.sum(jnp.mean(e * e, axis=-1, keepdims=True), axis=0, keepdims=True)
        dx, dg = vjp(e * (1.0 / d))
        return jnp.broadcast_to(part, (1, _LANE)), dx, dx, dg

    return _rowcall(fn, [x3, tgt, g], [_rb(tm, d), _rb(tm, d), _full((1, d))],
                    [_sds((1, _LANE), _F32), _sds((t, d), _F32), _sds((t, d), _BF), _sds((1, d), _F32)],
                    [_full((1, _LANE)), _rb(tm, d), _rb(tm, d), _full((1, d))], [True, False, False, True],
                    grid=(1, t // tm), name="loss_head")


def _s5_disc(a_re, a_im, log_dt):
    lam_re = jnp.minimum(a_re, _S5_MAX_RE)
    lam_im = a_im
    dt = jnp.exp(log_dt)
    mag = jnp.exp(lam_re * dt)
    abar_re = mag * jnp.cos(lam_im * dt)
    abar_im = mag * jnp.sin(lam_im * dt)
    den = lam_re * lam_re + lam_im * lam_im
    nr = abar_re - 1.0
    ni = abar_im
    coef_re = (nr * lam_re + ni * lam_im) / den
    coef_im = (ni * lam_re - nr * lam_im) / den
    return abar_re, abar_im, coef_re, coef_im


def _s5_prep(a_re, a_im, log_dt):
    g, p = a_re.shape

    def body(ar, ai, ld, o0, o1, o2, o3):
        outs = _s5_disc(ar[...], ai[...], ld[...])
        for o, v in zip((o0, o1, o2, o3), outs):
            o[...] = v

    return _call(body, name="s5_prep", out_shape=[_sds((g, p), _F32)] * 4)(a_re, a_im, log_dt)


def _s5_prep_bwd(a_re, a_im, log_dt, cts):
    g, p = a_re.shape

    def body(ar, ai, ld, c0, c1, c2, c3, d0, d1, d2):
        _, vjp = jax.vjp(_s5_disc, ar[...], ai[...], ld[...])
        outs = vjp((c0[...], c1[...], c2[...], c3[...]))
        for o, v in zip((d0, d1, d2), outs):
            o[...] = v

    return _call(body, name="s5_prep_bwd",
                 out_shape=[_sds((g, p), _F32), _sds((g, p), _F32), _sds((g, 1), _F32)])(a_re, a_im, log_dt, *cts)


def _s5_block(u, car, cai, ar, ai, b_re, b_im, c_re, c_imn, dvec):
    bur = _bdot(u, b_re)
    bui = _bdot(u, b_im)
    shape8 = (_SUBLANES, ar.shape[1])
    pows = [(ar, ai)]
    for _ in range(2):
        pr, pi = pows[-1]
        pows.append((pr * pr - pi * pi, 2.0 * pr * pi))
    pows = [(jnp.broadcast_to(pr, shape8), jnp.broadcast_to(pi, shape8)) for pr, pi in pows]

    def scan8(xr, xi):
        for k, (pr, pi) in enumerate(pows):
            dr = _shift_down(xr, 1 << k)
            di = _shift_down(xi, 1 << k)
            xr, xi = xr + pr * dr - pi * di, xi + pr * di + pi * dr
        return xr, xi

    row8 = lax.broadcasted_iota(jnp.int32, (_SUBLANES, ar.shape[1]), 0)
    tr, ti = scan8(jnp.where(row8 == 0, ar, 0.0), jnp.where(row8 == 0, ai, 0.0))
    outs_r, outs_i = [], []
    for xr, xi in zip(_split_rows(bur), _split_rows(bui)):
        xr, xi = scan8(xr, xi)
        xr, xi = xr + tr * car - ti * cai, xi + tr * cai + ti * car
        car, cai = _last_row(xr), _last_row(xi)
        outs_r.append(xr)
        outs_i.append(xi)
    y = _bdot(_join_rows(tuple(outs_r)), c_re) + _bdot(_join_rows(tuple(outs_i)), c_imn) + dvec * u
    return y, car, cai


def _s5_specs(tb, lw, nt, rev):
    tmap = (lambda t: nt - 1 - t) if rev else (lambda t: t)
    vec = pl.BlockSpec((1, lw), lambda s, t: (0, s))
    return dict(
        u=pl.BlockSpec((tb, _LANE), lambda s, t: (tmap(t), s)),
        car=pl.BlockSpec((None, 1, lw), lambda s, t: (tmap(t), 0, s)),
        vec=vec,
        bmat=pl.BlockSpec((None, _LANE, lw), lambda s, t: (s, 0, 0)),
        cmat=pl.BlockSpec((None, lw, _LANE), lambda s, t: (s, 0, 0)),
        dvec=pl.BlockSpec((1, _LANE), lambda s, t: (0, s)),
    )


def _s5_fwd(proj, par, t, w5, rider=None):
    ar, ai, b_re, b_im, c_re, c_imn, dvec = par
    ns = w5 // _LANE
    lw = ar.shape[1] // ns
    tb = min(_S5_TB, t)
    nt = t // tb
    sp = _s5_specs(tb, lw, nt, False)

    def body(u_ref, ar_r, ai_r, bre_r, bim_r, cre_r, cim_r, d_r, y_ref, car_ref, cai_ref, s_r, s_i):
        @pl.when(pl.program_id(1) == 0)
        def _():
            s_r[...] = jnp.zeros_like(s_r)
            s_i[...] = jnp.zeros_like(s_i)

        car_ref[...] = s_r[...]
        cai_ref[...] = s_i[...]
        y, ncr, nci = _s5_block(u_ref[...], s_r[...], s_i[...], ar_r[...], ai_r[...],
                                bre_r[...], bim_r[...], cre_r[...], cim_r[...], d_r[...])
        y_ref[...] = y
        s_r[...] = ncr
        s_i[...] = nci

    kw = dict(name="s5_fwd", grid=(ns, nt),
              in_specs=[sp["u"], sp["vec"], sp["vec"], sp["bmat"], sp["bmat"],
                        sp["cmat"], sp["cmat"], sp["dvec"]],
              out_specs=[sp["u"], sp["car"], sp["car"]],
              out_shape=[_sds((t, w5), _F32), _sds((nt, 1, ns * lw), _F32), _sds((nt, 1, ns * lw), _F32)],
              scratch_shapes=[pltpu.VMEM((1, lw), _F32), pltpu.VMEM((1, lw), _F32)])
    args = (proj, ar, ai, b_re, b_im, c_re, c_imn, dvec)
    if rider is None:
        return _call(body, compiler_params=_cparams("parallel", "arbitrary"), **kw)(*args), []
    return _call_riding(body, rider, args=args, **kw)


def _s5_bwd(proj, dy, car, cai, par, t, w5, rider=None):
    ar, ai, b_re, b_im, c_re, c_imn, dvec = par
    ns = w5 // _LANE
    lw = ar.shape[1] // ns
    tb = min(_S5_TB, t)
    nt = t // tb
    sp = _s5_specs(tb, lw, nt, True)

    def body(u_ref, dy_ref, car_ref, cai_ref, ar_r, ai_r, bre_r, bim_r, cre_r, cim_r, d_r,
             du_ref, g_ar, g_ai, g_bre, g_bim, g_cre, g_cim, g_d, ds_r, ds_i):
        accs = (g_ar, g_ai, g_bre, g_bim, g_cre, g_cim, g_d)

        @pl.when(pl.program_id(1) == 0)
        def _():
            ds_r[...] = jnp.zeros_like(ds_r)
            ds_i[...] = jnp.zeros_like(ds_i)
            for o in accs:
                o[...] = jnp.zeros_like(o)

        _, vjp = jax.vjp(_s5_block, u_ref[...], car_ref[...], cai_ref[...], ar_r[...], ai_r[...],
                         bre_r[...], bim_r[...], cre_r[...], cim_r[...], d_r[...])
        grads = vjp((dy_ref[...], ds_r[...], ds_i[...]))
        du_ref[...] = grads[0].astype(_BF)
        ds_r[...] = grads[1]
        ds_i[...] = grads[2]
        for o, gval in zip(accs, grads[3:]):
            o[...] += gval

    vec_o = _sds((1, ns * lw), _F32)
    kw = dict(name="s5_bwd", grid=(ns, nt),
              in_specs=[sp["u"], sp["u"], sp["car"], sp["car"], sp["vec"], sp["vec"],
                        sp["bmat"], sp["bmat"], sp["cmat"], sp["cmat"], sp["dvec"]],
              out_specs=[sp["u"], sp["vec"], sp["vec"], sp["bmat"], sp["bmat"],
                         sp["cmat"], sp["cmat"], sp["dvec"]],
              out_shape=[_sds((t, w5), _BF), vec_o, vec_o,
                         _sds(b_re.shape, _F32), _sds(b_re.shape, _F32), _sds(c_re.shape, _F32),
                         _sds(c_re.shape, _F32), _sds((1, w5), _F32)],
              scratch_shapes=[pltpu.VMEM((1, lw), _F32), pltpu.VMEM((1, lw), _F32)])
    args = (proj, dy, car, cai, ar, ai, b_re, b_im, c_re, c_imn, dvec)
    if rider is None:
        return _call(body, compiler_params=_cparams("parallel", "arbitrary"), **kw)(*args), []
    return _call_riding(body, rider, args=args, **kw)


def _bd_in(b):
    g, p, c = b.shape
    s = g // _S5_SET
    b4 = b.reshape(s, _S5_SET, p, c).transpose(0, 1, 3, 2)
    eye = jnp.eye(_S5_SET, dtype=b.dtype)
    return (b4[:, :, :, None, :] * eye[None, :, None, :, None]).reshape(s, _S5_SET * c, _S5_SET * p)


def _bd_in_grad(d, p, c):
    s = d.shape[0]
    eye = jnp.eye(_S5_SET, dtype=d.dtype)
    d5 = d.reshape(s, _S5_SET, c, _S5_SET, p) * eye[None, :, None, :, None]
    return d5.sum(axis=3).transpose(0, 1, 3, 2).reshape(s * _S5_SET, p, c)


def _bd_out(cm):
    g, c, p = cm.shape
    s = g // _S5_SET
    c4 = cm.reshape(s, _S5_SET, c, p).transpose(0, 1, 3, 2)
    eye = jnp.eye(_S5_SET, dtype=cm.dtype)
    return (c4[:, :, :, None, :] * eye[None, :, None, :, None]).reshape(s, _S5_SET * p, _S5_SET * c)


def _bd_out_grad(d, p, c):
    s = d.shape[0]
    eye = jnp.eye(_S5_SET, dtype=d.dtype)
    d5 = d.reshape(s, _S5_SET, p, _S5_SET, c) * eye[None, :, None, :, None]
    return d5.sum(axis=3).transpose(0, 1, 3, 2).reshape(s * _S5_SET, c, p)


def _s5glu_fwd(y, wglu, bglu):
    t, w5 = y.shape
    tm = _tile(t, 256, 8)

    def fn(yb, wb, bb):
        z1 = _gelu(yb)
        a = _dg(z1, wb, _NN) + bb
        return (z1 * _sigmoid(a),)

    return _rowcall(fn, [y, wglu, bglu], [_rb(tm, w5), _full(wglu.shape), _full((1, w5))],
                    [_sds((t, w5), _BF)], [_rb(tm, w5)], [False], grid=(1, t // tm), name="s5glu_fwd")[0]


def _s5glu_bwd(y, dz, wglu, bglu):
    t, w5 = y.shape
    tm = _tile(t, 256, 8)

    def fn(yb, dzb, wb, bb):
        dzb = dzb.astype(_F32)
        z1, gelu_vjp = jax.vjp(_gelu, yb)
        sig = _sigmoid(_dg(z1, wb, _NN) + bb)
        da = dzb * z1 * sig * (1.0 - sig)
        dz1 = dzb * sig + _dg(da, wb, _NT)
        (dy,) = gelu_vjp(dz1)
        return dy, _dg(z1, da, _TN), jnp.sum(da, axis=0, keepdims=True)

    return _rowcall(fn, [y, dz, wglu, bglu], [_rb(tm, w5), _rb(tm, w5), _full(wglu.shape), _full((1, w5))],
                    [_sds((t, w5), _F32), _sds((w5, w5), _F32), _sds((1, w5), _F32)],
                    [_rb(tm, w5), _full((w5, w5)), _full((1, w5))], [False, True, True],
                    grid=(1, t // tm), name="s5glu_bwd")


_LEVELS = (6, 5, 4, 3, 2, 1)


def _tri_stack(c):
    t = jnp.arange(c, dtype=jnp.int32)[:, None]
    j = jnp.arange(c, dtype=jnp.int32)[None, :]
    low = (j <= t).astype(_F32)
    mats = [low]
    for sh in _LEVELS:
        r = ((t >> sh) << sh) + ((1 << (sh - 1)) - 1)
        mats.append(low - (j <= r).astype(_F32))
    stack = jnp.concatenate(mats, axis=0).astype(_BF)
    return stack, stack.T


def _split_dot(mat, x):
    l = x.shape[1]
    hi = x.astype(_BF)
    lo = (x - hi.astype(_F32)).astype(_BF)
    out = jnp.dot(mat, jnp.concatenate([hi, lo], axis=1), preferred_element_type=_F32)
    return out[:, :l] + out[:, l:]


@jax.custom_vjp
def _decay_sums(lf, tri, tri_t):
    c = lf.shape[0]
    out = _split_dot(tri, lf)
    return tuple(out[k * c:(k + 1) * c] for k in range(len(_LEVELS) + 1))


def _decay_sums_f(lf, tri, tri_t):
    return _decay_sums(lf, tri, tri_t), (tri, tri_t)


def _decay_sums_b(res, gs):
    tri, tri_t = res
    return _split_dot(tri_t, jnp.concatenate(gs, axis=0)), jnp.zeros_like(tri), jnp.zeros_like(tri_t)


_decay_sums.defvjp(_decay_sums_f, _decay_sums_b)


def _hgrn_chunk(qi, fi, vi, gi, st, lb, ng, tri, tri_t):
    c = qi.shape[0]
    row = lax.broadcasted_iota(jnp.int32, (c, 1), 0)
    q = _silu(qi)
    lf = jnp.log(lb + (1.0 - lb) * _sigmoid(fi))
    k = (1.0 - lb) * _sigmoid(-fi)
    sums = _decay_sums(lf, tri, tri_t)
    b = sums[0]
    btot = jnp.sum(lf, axis=0, keepdims=True)
    inter = _bdot_nt(q * jnp.exp(b), st)
    tt = lax.broadcasted_iota(jnp.int32, (c, c), 0)
    ss = lax.broadcasted_iota(jnp.int32, (c, c), 1)
    sc = jnp.where(tt == ss, jnp.sum(q * k, axis=1, keepdims=True), 0.0)
    for sh, p in zip(_LEVELS, sums[1:]):
        upper = ((row >> (sh - 1)) & 1) == 1
        qm = jnp.where(upper, q * jnp.exp(jnp.where(upper, p, 0.0)), 0.0)
        km = jnp.where(upper, 0.0, k * jnp.exp(jnp.where(upper, 0.0, -p)))
        sc = sc + jnp.where((tt >> sh) == (ss >> sh), _bdot_nt(qm, km), 0.0)
    o = inter + _bdot(sc, vi)
    st_new = st * jnp.exp(btot) + _bdot_tn(vi, k * jnp.exp(btot - b))
    on = o * lax.rsqrt(jnp.mean(o * o, axis=1, keepdims=True) + _RMS_EPS) * ng
    return on * _silu(gi), st_new


def _hgrn_geom(t, w5, hw):
    hp = _HGRN_HP if (hw // _LANE) % _HGRN_HP == 0 and w5 % (_LANE * _HGRN_HP) == 0 else 1
    seg = min(t, _HGRN_SEG)
    return hp, hp * _LANE, seg, t // seg


def _hgrn_in_specs(seg, wd, w5, hw, tmap):
    nhp = hw // wd
    qb = w5 // wd
    return [pl.BlockSpec((seg, wd), (lambda h, s, k=k: (tmap(s), qb + k * nhp + h))) for k in range(4)]


def _hgrn_fwd(proj, lb, ng, t, w5, hw):
    assert _CHUNK == 64
    hp, wd, seg, nseg = _hgrn_geom(t, w5, hw)
    ncs = seg // _CHUNK
    vec = pl.BlockSpec((1, wd), lambda h, s: (0, h))

    tri, tri_t = _tri_stack(_CHUNK)

    def body(q_ref, f_ref, i_ref, g_ref, lb_ref, ng_ref, tri_ref, trit_ref, og_ref, st_ref, s_scr):
        @pl.when(pl.program_id(1) == 0)
        def _():
            s_scr[...] = jnp.zeros_like(s_scr)

        tri_v, trit_v = tri_ref[...], trit_ref[...]

        def step(ci, carry):
            r = pl.ds(pl.multiple_of(ci * _CHUNK, _CHUNK), _CHUNK)
            for hh in range(hp):
                ln = slice(hh * _LANE, (hh + 1) * _LANE)
                st_ref[hh, ci] = s_scr[hh]
                og, sn = _hgrn_chunk(q_ref[r, ln], f_ref[r, ln], i_ref[r, ln], g_ref[r, ln], s_scr[hh],
                                     lb_ref[:, ln], ng_ref[:, ln], tri_v, trit_v)
                og_ref[r, ln] = og.astype(_BF)
                s_scr[hh] = sn
            return carry

        lax.fori_loop(0, ncs, step, 0)

    return _call(body, name="hgrn_fwd", grid=(hw // wd, nseg),
                 in_specs=_hgrn_in_specs(seg, wd, w5, hw, lambda s: s) + [
                     vec, vec, pl.BlockSpec(tri.shape, lambda h, s: (0, 0)), pl.BlockSpec(tri_t.shape, lambda h, s: (0, 0))],
                 out_specs=[pl.BlockSpec((seg, wd), lambda h, s: (s, h)),
                            pl.BlockSpec((hp, ncs, _LANE, _LANE), lambda h, s: (h, s, 0, 0))],
                 out_shape=[_sds((t, hw), _BF), _sds((hw // _LANE, t // _CHUNK, _LANE, _LANE), _F32)],
                 scratch_shapes=[pltpu.VMEM((hp, _LANE, _LANE), _F32)],
                 compiler_params=_cparams("parallel", "arbitrary"))(proj, proj, proj, proj, lb, ng, tri, tri_t)


def _hgrn_bwd(proj, lb, ng, st_all, dog, t, w5, hw):
    hp, wd, seg, nseg = _hgrn_geom(t, w5, hw)
    ncs = seg // _CHUNK

    def rev(s):
        return nseg - 1 - s

    vec = pl.BlockSpec((1, wd), lambda h, s: (0, h))
    col = pl.BlockSpec((seg, wd), lambda h, s: (rev(s), h))

    tri, tri_t = _tri_stack(_CHUNK)

    def body(q_ref, f_ref, i_ref, g_ref, lb_ref, ng_ref, tri_ref, trit_ref, st_ref, dog_ref,
             dq_ref, df_ref, di_ref, dg_ref, dlb_ref, dng_ref, ds_scr):
        @pl.when(pl.program_id(1) == 0)
        def _():
            ds_scr[...] = jnp.zeros_like(ds_scr)
            dlb_ref[...] = jnp.zeros_like(dlb_ref)
            dng_ref[...] = jnp.zeros_like(dng_ref)

        tri_v, trit_v = tri_ref[...], trit_ref[...]

        def step(kk, carry):
            ci = ncs - 1 - kk
            r = pl.ds(pl.multiple_of(ci * _CHUNK, _CHUNK), _CHUNK)
            for hh in range(hp):
                ln = slice(hh * _LANE, (hh + 1) * _LANE)
                _, vjp = jax.vjp(_hgrn_chunk, q_ref[r, ln], f_ref[r, ln], i_ref[r, ln], g_ref[r, ln], st_ref[hh, ci],
                                 lb_ref[:, ln], ng_ref[:, ln], tri_v, trit_v)
                dq, df, di, dg, ds, dlb, dng = vjp((dog_ref[r, ln].astype(_F32), ds_scr[hh]))[:7]
                dq_ref[r, ln] = dq.astype(_BF)
                df_ref[r, ln] = df.astype(_BF)
                di_ref[r, ln] = di.astype(_BF)
                dg_ref[r, ln] = dg.astype(_BF)
                ds_scr[hh] = ds
                dlb_ref[:, ln] += dlb
                dng_ref[:, ln] += dng
            return carry

        lax.fori_loop(0, ncs, step, 0)

    return _call(body, name="hgrn_bwd", grid=(hw // wd, nseg),
                 in_specs=_hgrn_in_specs(seg, wd, w5, hw, rev) + [
                     vec, vec, pl.BlockSpec(tri.shape, lambda h, s: (0, 0)), pl.BlockSpec(tri_t.shape, lambda h, s: (0, 0)),
                     pl.BlockSpec((hp, ncs, _LANE, _LANE), lambda h, s: (h, rev(s), 0, 0)), col],
                 out_specs=[col, col, col, col, vec, vec],
                 out_shape=[_sds((t, hw), _BF)] * 4 + [_sds((1, hw), _F32)] * 2,
                 scratch_shapes=[pltpu.VMEM((hp, _LANE, _LANE), _F32)],
                 compiler_params=_cparams("parallel", "arbitrary"))(
                     proj, proj, proj, proj, lb, ng, tri, tri_t, st_all, dog)


def _lb_of(logits):
    mx = jnp.max(logits, axis=0, keepdims=True)
    e = jnp.exp(logits - mx)
    sm = e / jnp.sum(e, axis=0, keepdims=True)
    row = lax.broadcasted_iota(jnp.int32, logits.shape, 0)
    return jnp.sum(jnp.where(row == 0, sm, 0.0), axis=0, keepdims=True)


def _lb_prep(logits):
    def body(l_ref, o_ref):
        o_ref[...] = _lb_of(l_ref[...])

    return _call(body, name="lb_prep", out_shape=_sds((1, logits.shape[1]), _F32))(logits)


def _lb_prep_bwd(logits, dlb):
    def body(l_ref, d_ref, o_ref):
        _, vjp = jax.vjp(_lb_of, l_ref[...])
        o_ref[...] = vjp(d_ref[...])[0]

    return _call(body, name="lb_prep_bwd", out_shape=_sds(logits.shape, _F32))(logits, dlb)


def _merge_fwd(proj, ys, yh, t, d, gs_off):
    tm = _tile(t, 256, 8)
    w = _tile(d, 1024)
    nb = d // w

    def fn(gs, gh, a, b):
        return (_sigmoid(gs) * a + _sigmoid(gh) * b,)

    return _rowcall(fn, [proj, proj, ys, yh], [_rb(tm, w, gs_off // w), _rb(tm, w, gs_off // w + nb), _rb(tm, w), _rb(tm, w)],
                    [_sds((t, d), _BF)], [_rb(tm, w)], [False], grid=(nb, t // tm), name="merge_fwd")[0]


def _merge_bwd(proj, ys, yh, dm, t, d, gs_off):
    tm = _tile(t, 256, 8)
    w = _tile(d, 1024)
    nb = d // w

    def fn(gs, gh, a, b, g):
        g = g.astype(_F32)
        s1 = _sigmoid(gs)
        s2 = _sigmoid(gh)
        return g * s1, g * s2, g * a * s1 * (1.0 - s1), g * b * s2 * (1.0 - s2)

    return _rowcall(fn, [proj, proj, ys, yh, dm],
                    [_rb(tm, w, gs_off // w), _rb(tm, w, gs_off // w + nb), _rb(tm, w), _rb(tm, w), _rb(tm, w)],
                    [_sds((t, d), _BF)] * 4, [_rb(tm, w)] * 4, [False] * 4, grid=(nb, t // tm), name="merge_bwd")


_HALO = 16


def _prev_rows(up_prev, is_first):
    p1 = jnp.where(is_first, 0.0, up_prev[_HALO - 1:_HALO, :])
    p2 = jnp.where(is_first, 0.0, up_prev[_HALO - 2:_HALO - 1, :])
    return p1, p2


def _causal_taps(cur, p1, p2):
    row = lax.broadcasted_iota(jnp.int32, cur.shape, 0)
    s1 = jnp.where(row == 0, p1, _shift_down(cur, 1))
    s2 = jnp.where(row == 0, p2, jnp.where(row == 1, p1, _shift_down(cur, 2)))
    return s1, s2


def _pair_perm(nbh, per):
    def perm(n_):
        big = n_ // per
        return (2 * (big % nbh) + big // nbh) * per + n_ % per
    return perm


def _conv_specs(tm, w, nb_half, t):
    r8 = tm // _HALO
    cur_g = pl.BlockSpec((tm, w), lambda j, i: (i, 2 * j))
    cur_v = pl.BlockSpec((tm, w), lambda j, i: (i, 2 * j + 1))
    prev_g = pl.BlockSpec((_HALO, w), lambda j, i: (jnp.maximum(i * r8 - 1, 0), 2 * j))
    prev_v = pl.BlockSpec((_HALO, w), lambda j, i: (jnp.maximum(i * r8 - 1, 0), 2 * j + 1))
    w_g = pl.BlockSpec((3, w), lambda j, i: (0, j))
    w_v = pl.BlockSpec((3, w), lambda j, i: (0, nb_half + j))
    b_g = pl.BlockSpec((1, w), lambda j, i: (0, j))
    b_v = pl.BlockSpec((1, w), lambda j, i: (0, nb_half + j))
    return cur_g, cur_v, prev_g, prev_v, w_g, w_v, b_g, b_v


def _conv_of(cur, prev8, wt, bias, is_first):
    cur, prev8 = cur.astype(_F32), prev8.astype(_F32)
    p1, p2 = _prev_rows(prev8, is_first)
    s1, s2 = _causal_taps(cur, p1, p2)
    return bias + wt[0:1, :] * s2 + wt[1:2, :] * s1 + wt[2:3, :] * cur


def _convact_fwd(up, cw, cb, t, dff):
    tm = _tile(t, 512, _HALO)
    w = _tile(dff, 1408)
    nbh = dff // w
    sp = _conv_specs(tm, w, nbh, t)

    def body(ug, uv, pg, pv, wg, wv, bg, bv, o_ref):
        first = pl.program_id(1) == 0
        gate = _conv_of(ug[...], pg[...], wg[...], bg[...], first)
        val = _conv_of(uv[...], pv[...], wv[...], bv[...], first)
        o_ref[...] = (_silu(gate) * val).astype(_BF)

    return _call(body, name="convact_fwd", grid=(nbh, t // tm), in_specs=list(sp),
                 out_specs=pl.BlockSpec((tm, w), lambda j, i: (i, j)), out_shape=_sds((t, dff), _BF),
                 compiler_params=_cparams("parallel", "arbitrary"))(up, up, up, up, cw, cw, cb, cb)


def _convact_bwd(up, dact, cw, cb, t, dff):
    tm = _tile(t, 256, _HALO)
    w = _tile(dff, 1408)
    nbh = dff // w
    r8 = tm // _HALO
    nt = t // tm
    last8 = t // _HALO - 1

    def triple(off):
        return [pl.BlockSpec((tm, w), lambda j, i: (i, 2 * j + off)),
                pl.BlockSpec((_HALO, w), lambda j, i: (jnp.minimum((i + 1) * r8, last8), 2 * j + off)),
                pl.BlockSpec((_HALO, w), lambda j, i: (jnp.maximum(i * r8 - 1, 0), 2 * j + off))]

    def body(ug, ugn, ugp, uv, uvn, uvp, wg_ref, wv_ref, bg_ref, bv_ref, da_ref, dan_ref, du_ref, dw_ref, db_ref):
        i = pl.program_id(1)
        first = i == 0
        is_last = i == nt - 1
        gate = _conv_of(jnp.concatenate([ug[...], ugn[...]], axis=0), ugp[...], wg_ref[...], bg_ref[...], first)
        val = _conv_of(jnp.concatenate([uv[...], uvn[...]], axis=0), uvp[...], wv_ref[...], bv_ref[...], first)
        da = jnp.concatenate([da_ref[...], dan_ref[...]], axis=0).astype(_F32)
        row = lax.broadcasted_iota(jnp.int32, da.shape, 0)
        da = jnp.where(jnp.logical_and(row >= tm, is_last), 0.0, da)
        sg = _sigmoid(gate)
        halves = ((da * val * sg * (1.0 + gate * (1.0 - sg)), wg_ref, ug, ugp),
                  (da * gate * sg, wv_ref, uv, uvp))

        @pl.when(first)
        def _():
            dw_ref[...] = jnp.zeros_like(dw_ref)
            db_ref[...] = jnp.zeros_like(db_ref)

        for h, (dc, w_ref, u_ref, p_ref) in enumerate(halves):
            ln = slice(h * w, (h + 1) * w)
            wt = w_ref[...]
            du = wt[2:3, :] * dc + wt[1:2, :] * _shift_up(dc, 1) + wt[0:1, :] * _shift_up(dc, 2)
            du_ref[:, ln] = du[0:tm, :].astype(_BF)
            dcm = dc[0:tm, :]
            cur = u_ref[...].astype(_F32)
            p1, p2 = _prev_rows(p_ref[...].astype(_F32), first)
            s1, s2 = _causal_taps(cur, p1, p2)
            dw_ref[0:1, ln] += jnp.sum(dcm * s2, axis=0, keepdims=True)
            dw_ref[1:2, ln] += jnp.sum(dcm * s1, axis=0, keepdims=True)
            dw_ref[2:3, ln] += jnp.sum(dcm * cur, axis=0, keepdims=True)
            db_ref[:, ln] += jnp.sum(dcm, axis=0, keepdims=True)

    in_specs = (triple(0) + triple(1)
                + [pl.BlockSpec((3, w), lambda j, i: (0, j)), pl.BlockSpec((3, w), lambda j, i: (0, nbh + j)),
                   pl.BlockSpec((1, w), lambda j, i: (0, j)), pl.BlockSpec((1, w), lambda j, i: (0, nbh + j)),
                   pl.BlockSpec((tm, w), lambda j, i: (i, j)),
                   pl.BlockSpec((_HALO, w), lambda j, i: (jnp.minimum((i + 1) * r8, last8), j))])
    dup, dw_p, db_p = _call(
        body, name="convact_bwd", grid=(nbh, nt), in_specs=in_specs,
        out_specs=[pl.BlockSpec((tm, 2 * w), lambda j, i: (i, j)), pl.BlockSpec((3, 2 * w), lambda j, i: (0, j)),
                   pl.BlockSpec((1, 2 * w), lambda j, i: (0, j))],
        out_shape=[_sds((t, 2 * dff), _BF), _sds((3, 2 * dff), _F32), _sds((1, 2 * dff), _F32)],
        compiler_params=_cparams("parallel", "arbitrary"))(up, up, up, up, up, up, cw, cw, cb, cb, dact, dact)

    def natural(v):
        k = v.shape[0]
        return v.reshape(k, nbh, 2, w).transpose(0, 2, 1, 3).reshape(k, 2 * dff)

    return dup, natural(dw_p), natural(db_p)


def _me():
    return lax.axis_index("x"), lax.axis_index("y"), lax.axis_index("c")


def _other_chips(x, y):
    return [(1 - x, y), (x, 1 - y), (1 - x, 1 - y)]


def _rcopy(src, dst, ssem, rsem, dev):
    return pltpu.make_async_remote_copy(src_ref=src, dst_ref=dst, send_sem=ssem, recv_sem=rsem,
                                        device_id=dev, device_id_type=_MESH)


def _cast_into_slot(w, sel):
    r, c = w.shape
    tm = _tile(r, 256, 16)

    def body(sel_ref, w_ref, o_ref):
        o_ref[...] = w_ref[...].astype(_BF)

    gs = pltpu.PrefetchScalarGridSpec(
        num_scalar_prefetch=1, grid=(r // tm,),
        in_specs=[pl.BlockSpec((tm, c), lambda i, s: (i, 0))],
        out_specs=pl.BlockSpec((None, tm, c), lambda i, s: (s[0], i, 0)))
    return _call(body, name="cast_into_slot", grid_spec=gs, out_shape=_sds((_NCHIP, r, c), _BF),
                 compiler_params=_cparams("parallel"))(sel, w)


class _Plan:
    def __init__(self, ins, outs, aliases, sems, start, finish):
        self.ins, self.outs, self.aliases, self.sems, self.start, self.finish = ins, outs, aliases, sems, start, finish


def _run_plan(plan, name):
    ni, no = len(plan.ins), len(plan.outs)

    def body(*refs):
        rin, rout, sems = refs[:ni], refs[ni:ni + no], refs[ni + no:]
        plan.start(rin, rout, sems)
        plan.finish(rin, rout, sems)

    return _call(body, name=name, in_specs=[_ANY] * ni, out_specs=[_ANY] * no, out_shape=list(plan.outs),
                 input_output_aliases=dict(plan.aliases), scratch_shapes=list(plan.sems))(*plan.ins)


def _call_riding(body, rider, *, name, grid, in_specs, out_specs, out_shape, scratch_shapes, args):
    n_in, n_out, n_scr = len(in_specs), len(out_specs), len(scratch_shapes)
    n_rin, n_rout = len(rider.ins), len(rider.outs)

    def wrapped(*refs):
        ins, rin = refs[:n_in], refs[n_in:n_in + n_rin]
        o0 = n_in + n_rin
        outs, rout = refs[o0:o0 + n_out], refs[o0 + n_out:o0 + n_out + n_rout]
        s0 = o0 + n_out + n_rout
        scratch, sems = refs[s0:s0 + n_scr], refs[s0 + n_scr:]
        first = functools.reduce(jnp.logical_and, [pl.program_id(k) == 0 for k in range(len(grid))])
        last = functools.reduce(jnp.logical_and, [pl.program_id(k) == grid[k] - 1 for k in range(len(grid))])

        @pl.when(first)
        def _():
            rider.start(rin, rout, sems)

        body(*ins, *outs, *scratch)

        @pl.when(last)
        def _():
            rider.finish(rin, rout, sems)

    res = _call(wrapped, name=name, grid=grid, in_specs=list(in_specs) + [_ANY] * n_rin,
                out_specs=list(out_specs) + [_ANY] * n_rout, out_shape=list(out_shape) + list(rider.outs),
                input_output_aliases={n_in + k: n_out + v for k, v in rider.aliases.items()},
                scratch_shapes=list(scratch_shapes) + list(rider.sems),
                compiler_params=_cparams(*(["arbitrary"] * len(grid))))(*args, *rider.ins)
    return list(res[:n_out]), list(res[n_out:])


def _gather_plan(bufs, direct):
    n, nd = len(bufs), len(direct)

    def where():
        x, y, c = _me()
        return c, 2 * x + y, _other_chips(x, y), (x, y, 1 - c)

    def piece(outs, a, chip, h):
        r2 = bufs[a].shape[1] // 2
        return outs[a].at[chip, pl.ds(h * r2, r2)]

    def send(outs, sems, a, j, me, c, chip):
        return _rcopy(piece(outs, a, me, c), piece(outs, a, me, c), sems[0].at[3 * a + j], sems[1].at[3 * a + j],
                      (chip[0], chip[1], c))

    def forward(outs, sems, a, j, pc, c, sib):
        return _rcopy(piece(outs, a, pc, c), piece(outs, a, pc, c), sems[2].at[3 * a + j], sems[3].at[3 * a + j], sib)

    def dsend(dins, douts, sems, a, j, me, c, chip):
        return _rcopy(dins[a], douts[a].at[me], sems[4].at[3 * a + j], sems[5].at[3 * a + j], (chip[0], chip[1], c))

    def start(rin, rout, sems):
        outs, dins, douts = rout[:n], rin[n:], rout[n:]
        c, me, chips, _ = where()
        for a in range(n):
            for j, chip in enumerate(chips):
                send(outs, sems, a, j, me, c, chip).start()
        for a in range(nd):
            pltpu.make_async_copy(dins[a], douts[a].at[me], sems[6].at[a]).start()
            for j, chip in enumerate(chips):
                dsend(dins, douts, sems, a, j, me, c, chip).start()

    def finish(rin, rout, sems):
        outs, dins, douts = rout[:n], rin[n:], rout[n:]
        c, me, chips, sib = where()
        for a in range(n):
            for j, (cx, cy) in enumerate(chips):
                pc = 2 * cx + cy
                _rcopy(piece(outs, a, me, c), piece(outs, a, pc, c), sems[0].at[3 * a + j], sems[1].at[3 * a + j],
                       (cx, cy, c)).wait_recv()
                forward(outs, sems, a, j, pc, c, sib).start()
        for a in range(n):
            for j, (cx, cy) in enumerate(chips):
                pc = 2 * cx + cy
                _rcopy(piece(outs, a, pc, 1 - c), piece(outs, a, pc, 1 - c), sems[2].at[3 * a + j],
                       sems[3].at[3 * a + j], sib).wait_recv()
        for a in range(nd):
            for j, (cx, cy) in enumerate(chips):
                _rcopy(dins[a], douts[a].at[2 * cx + cy], sems[4].at[3 * a + j], sems[5].at[3 * a + j],
                       (cx, cy, c)).wait_recv()
        for a in range(n):
            for j, (cx, cy) in enumerate(chips):
                send(outs, sems, a, j, me, c, (cx, cy)).wait_send()
                forward(outs, sems, a, j, 2 * cx + cy, c, sib).wait_send()
        for a in range(nd):
            pltpu.make_async_copy(dins[a], douts[a].at[me], sems[6].at[a]).wait()
            for j, chip in enumerate(chips):
                dsend(dins, douts, sems, a, j, me, c, chip).wait_send()

    dma = pltpu.SemaphoreType.DMA
    return _Plan(list(bufs) + list(direct),
                 [_sds(b.shape, b.dtype) for b in bufs] + [_sds((_NCHIP,) + s.shape, s.dtype) for s in direct],
                 {a: a for a in range(n)},
                 [dma((3 * max(n, 1),)), dma((3 * max(n, 1),)), dma((3 * max(n, 1),)), dma((3 * max(n, 1),)),
                  dma((3 * max(nd, 1),)), dma((3 * max(nd, 1),)), dma((max(nd, 1),))], start, finish)


def _swap_halves(grads, name):
    n = len(grads)

    def body(*refs):
        ins, outs = refs[:n], refs[n:2 * n]
        ssem, rsem = refs[2 * n:]
        x, y, c = _me()
        sib = (x, y, 1 - c)
        cps = []
        for a in range(n):
            cp = _rcopy(ins[a].at[1 - c], outs[a], ssem.at[a], rsem.at[a], sib)
            cp.start()
            cps.append(cp)
        for cp in cps:
            cp.wait_recv()
        for cp in cps:
            cp.wait_send()

    dma = pltpu.SemaphoreType.DMA
    return _call(body, name=name, in_specs=[_ANY] * n, out_specs=[_ANY] * n,
                 out_shape=[_sds(g.shape[1:], g.dtype) for g in grads],
                 scratch_shapes=[dma((n,)), dma((n,))])(*grads)


def _add_pairs(grads, theirs, sel):
    _, s, r, c2 = grads.shape
    a3 = grads.reshape(2, s * r, c2)
    b2 = theirs.reshape(s * r, c2)
    tm = _tile(s * r, 512, 16)

    def body(sel_ref, a_ref, b_ref, o_ref):
        o_ref[...] = (a_ref[...].astype(_F32) + b_ref[...].astype(_F32)).astype(_BF)

    gs = pltpu.PrefetchScalarGridSpec(
        num_scalar_prefetch=1, grid=(s * r // tm,),
        in_specs=[pl.BlockSpec((None, tm, c2), lambda i, q: (q[1], i, 0)), pl.BlockSpec((tm, c2), lambda i, q: (i, 0))],
        out_specs=pl.BlockSpec((tm, c2), lambda i, q: (i, 0)))
    out = _call(body, name="chip_sum", grid_spec=gs, out_shape=_sds((s * r, c2), _BF),
                compiler_params=_cparams("parallel"))(sel, a3, b2)
    return out.reshape(s, r, c2)


def _exchange_plan(sums, small):
    n = len(sums)
    has_small = small is not None

    def where():
        x, y, c = _me()
        peers = [(1 - x if k & 4 else x, 1 - y if k & 2 else y, 1 - c if k & 1 else c) for k in range(1, _NDEV)]
        return c, 4 * x + 2 * y + c, _other_chips(x, y), peers

    def send(rin, rout, sems, a, j, c, chip):
        return _rcopy(rin[a].at[2 * chip[0] + chip[1]], rout[a].at[j], sems[0].at[3 * a + j], sems[1].at[3 * a + j],
                      (chip[0], chip[1], c))

    def small_send(rin, rout, sems, k, dev, peer):
        return _rcopy(rin[n], rout[n].at[dev], sems[2].at[k], sems[3].at[k], peer)

    def start(rin, rout, sems):
        c, dev, chips, peers = where()
        for a in range(n):
            for j, chip in enumerate(chips):
                send(rin, rout, sems, a, j, c, chip).start()
        if has_small:
            pltpu.make_async_copy(rin[n], rout[n].at[dev], sems[4].at[0]).start()
            for k, peer in enumerate(peers):
                small_send(rin, rout, sems, k, dev, peer).start()

    def finish(rin, rout, sems):
        c, dev, chips, peers = where()
        for a in range(n):
            for j, chip in enumerate(chips):
                send(rin, rout, sems, a, j, c, chip).wait_recv()
        if has_small:
            for k, (px, py, pc_) in enumerate(peers):
                _rcopy(rin[n], rout[n].at[4 * px + 2 * py + pc_], sems[2].at[k], sems[3].at[k], (px, py, pc_)).wait_recv()
        for a in range(n):
            for j, chip in enumerate(chips):
                send(rin, rout, sems, a, j, c, chip).wait_send()
        if has_small:
            pltpu.make_async_copy(rin[n], rout[n].at[dev], sems[4].at[0]).wait()
            for k, peer in enumerate(peers):
                small_send(rin, rout, sems, k, dev, peer).wait_send()

    dma = pltpu.SemaphoreType.DMA
    outs = [_sds((3,) + s.shape[1:], s.dtype) for s in sums]
    if has_small:
        outs.append(_sds((_NDEV,) + small.shape, small.dtype))
    return _Plan(list(sums) + ([small] if has_small else []), outs, {},
                 [dma((3 * max(n, 1),)), dma((3 * max(n, 1),)), dma((_NDEV - 1,)), dma((_NDEV - 1,)), dma((1,))],
                 start, finish)


def _shard_sum(sums, recv, sel):
    s, r, c2 = sums.shape
    tm = _tile(r, 256, 16)

    def body(sel_ref, own_ref, rc_ref, o_ref):
        rc = rc_ref[...]
        o_ref[...] = ((own_ref[...].astype(_F32) + rc[0].astype(_F32)) + rc[1].astype(_F32)) + rc[2].astype(_F32)

    gs = pltpu.PrefetchScalarGridSpec(
        num_scalar_prefetch=1, grid=(r // tm,),
        in_specs=[pl.BlockSpec((None, tm, c2), lambda i, q: (q[0], i, 0)),
                  pl.BlockSpec((3, tm, c2), lambda i, q: (0, i, 0))],
        out_specs=pl.BlockSpec((None, tm, c2), lambda i, q: (q[1], i, 0)))
    return _call(body, name="shard_sum", grid_spec=gs, out_shape=_sds((2, r, c2), _F32),
                 compiler_params=_cparams("parallel"))(sel, sums, recv)


def _sum_slots(stack, name):
    k, r, c = stack.shape
    tm = _tile(r, 256, 16 if stack.dtype == _BF else 8)

    def fn(v):
        out = v[0].astype(_F32)
        for i in range(1, k):
            out = out + v[i].astype(_F32)
        return (out,)

    return _rowcall(fn, [stack], [pl.BlockSpec((k, tm, c), lambda j, i: (0, i, 0))], [_sds((r, c), _F32)],
                    [_rb(tm, c)], [False], grid=(1, r // tm), name=name)[0]


def _share_halves(bufs):
    n = len(bufs)

    def body(*refs):
        outs = refs[n:2 * n]
        ssem, rsem = refs[2 * n:]
        x, y, c = _me()
        sib = (x, y, 1 - c)
        cps = []
        for a in range(n):
            cp = _rcopy(outs[a].at[c], outs[a].at[c], ssem.at[a], rsem.at[a], sib)
            cp.start()
            cps.append(cp)
        for a in range(n):
            _rcopy(outs[a].at[c], outs[a].at[1 - c], ssem.at[a], rsem.at[a], sib).wait_recv()
        for cp in cps:
            cp.wait_send()

    dma = pltpu.SemaphoreType.DMA
    return _call(body, name="share_halves", in_specs=[_ANY] * n, out_specs=[_ANY] * n,
                 out_shape=[_sds(b.shape, b.dtype) for b in bufs], input_output_aliases={a: a for a in range(n)},
                 scratch_shapes=[dma((n,)), dma((n,))])(*bufs)


def _adamw_math(w, g, m, v):
    m = _B1 * m + (1.0 - _B1) * g
    v = _B2 * v + (1.0 - _B2) * jnp.square(g)
    m_hat = m / (1.0 - _B1 ** _STEP)
    v_hat = v / (1.0 - _B2 ** _STEP)
    delta = -_LR * (m_hat / (jnp.sqrt(v_hat) + _ADAM_EPS) + _WD * w)
    return delta, m, v


def _adamw_shard(w, g2, m, v, name):
    r, c = w.shape
    c2 = c // 2
    tm = _tile(r, 256, 8)
    blk = pl.BlockSpec((tm, c2), lambda h, i: (i, h))

    def fn(wb, gb, mb, vb):
        return (gb,) + _adamw_math(wb, gb, mb, vb)

    return _rowcall(fn, [w, g2, m, v], [blk, pl.BlockSpec((None, tm, c2), lambda h, i: (h, i, 0)), blk, blk],
                    [_sds((r, c), _F32)] * 4, [blk] * 4, [False] * 4, grid=(2, r // tm), name=name)


def _adamw_whole(w, g, m, v, name):
    r, c = w.shape
    blk = _full((r, c))
    return _rowcall(lambda *a: _adamw_math(*a), [w, g, m, v], [blk] * 4, [_sds((r, c), _F32)] * 3, [blk] * 3,
                    [False] * 3, grid=(1, 1), name=name)


def _adamw_many(ws, g_pack, ms, vs):
    k = len(ws)
    views, offs, off = [], [], 0
    for w in ws:
        n = w.size
        views.append((n // _LANE, _LANE) if n % _LANE == 0 else (1, n))
        offs.append(off)
        off += (n + (-n) % (8 * _LANE)) // _LANE

    def body(*refs):
        g_ref, w_refs, m_refs, v_refs = refs[0], refs[1:1 + k], refs[1 + k:1 + 2 * k], refs[1 + 2 * k:1 + 3 * k]
        outs = refs[1 + 3 * k:]
        for i in range(k):
            r, c = views[i]
            g = g_ref[offs[i]:offs[i] + r, 0:c]
            res = (g,) + _adamw_math(w_refs[i][...], g, m_refs[i][...], v_refs[i][...])
            for o_ref, val in zip(outs[4 * i:4 * i + 4], res):
                o_ref[...] = val

    args = [g_pack] + [a.reshape(views[i]) for grp in (ws, ms, vs) for i, a in enumerate(grp)]
    res = _call(body, name="adamw_small", out_shape=[_sds(views[i], _F32) for i in range(k) for _ in range(4)])(*args)
    return [[res[4 * i + j].reshape(ws[i].shape) for j in range(4)] for i in range(k)]


def _pack(arrs):
    parts = []
    for a in arrs:
        f = a.reshape(-1).astype(_F32)
        pad = (-f.shape[0]) % (8 * _LANE)
        if pad:
            f = jnp.concatenate([f, jnp.zeros((pad,), _F32)])
        parts.append(f)
    return jnp.concatenate(parts).reshape(-1, _LANE)


def kernel(x, ln_mix_g, w_in, s5_a_re, s5_a_im, s5_log_dt, s5_b_re, s5_b_im, s5_c_re, s5_c_im, s5_d, s5_w_glu, s5_b_glu, w_proj_s5, hgrn_lb_logits, hgrn_norm_g, w_proj_hgrn, w_out, ln_ffn_g, w_up, conv_w, conv_b, w_down, ln_final_g, loss_target, m_ln_mix_g, m_w_in, m_s5_a_re, m_s5_a_im, m_s5_log_dt, m_s5_b_re, m_s5_b_im, m_s5_c_re, m_s5_c_im, m_s5_d, m_s5_w_glu, m_s5_b_glu, m_w_proj_s5, m_hgrn_lb_logits, m_hgrn_norm_g, m_w_proj_hgrn, m_w_out, m_ln_ffn_g, m_w_up, m_conv_w, m_conv_b, m_w_down, m_ln_final_g, v_ln_mix_g, v_w_in, v_s5_a_re, v_s5_a_im, v_s5_log_dt, v_s5_b_re, v_s5_b_im, v_s5_c_re, v_s5_c_im, v_s5_d, v_s5_w_glu, v_s5_b_glu, v_w_proj_s5, v_hgrn_lb_logits, v_hgrn_norm_g, v_w_proj_hgrn, v_w_out, v_ln_ffn_g, v_w_up, v_conv_w, v_conv_b, v_w_down, v_ln_final_g):
    assert x.shape[0] == 1 and w_in.shape[0] == 1, "one example per device, one layer"
    t, d = x.shape[1], x.shape[2]
    w5 = s5_w_glu.shape[2]
    hw = hgrn_norm_g.shape[1]
    ng_, np_, gc = s5_b_re.shape[1], s5_b_re.shape[2], s5_b_re.shape[3]
    dff = w_down.shape[1] * _NCHIP
    assert gc * _S5_SET == _LANE and ng_ * gc == w5 and hw % _LANE == 0
    gs_off = w5 + 4 * hw
    ci = lax.axis_index("c")
    xt = x.reshape(t, d)
    tgt = loss_target.reshape(t, d)

    big_names = ["w_in", "s5_w_glu", "w_proj_s5", "w_proj_hgrn", "w_out", "w_up", "w_down"]
    big_w = dict(w_in=w_in[0], s5_w_glu=s5_w_glu[0], w_proj_s5=w_proj_s5[0], w_proj_hgrn=w_proj_hgrn[0],
                 w_out=w_out[0], w_up=w_up[0], w_down=w_down[0])
    chip = 2 * lax.axis_index("x") + lax.axis_index("y")
    sel_chip = jnp.stack([chip, ci]).astype(jnp.int32)
    slots = {k: _cast_into_slot(big_w[k], sel_chip) for k in big_names}
    g_in, g_cw = _run_plan(_gather_plan([slots["w_in"]], [conv_w[0]]), "gather_w_in")
    cw = g_cw.transpose(1, 0, 2).reshape(3, 2 * dff)
    cb = conv_b

    tm_big = _tile(t, 1024, 8)

    h1 = _rms_fwd(xt, ln_mix_g, "rms1_fwd")
    nin_s = g_in.shape[2]
    proj, (g_glu, g_ps5, g_ph, g_out) = _mm(
        h1, g_in, "nn", _F32, tm=tm_big, tn=_tile(nin_s, 1152), tk=d, name="mm_proj",
        rider=_gather_plan([slots[k] for k in ("s5_w_glu", "w_proj_s5", "w_proj_hgrn", "w_out")], []))
    wglu = g_glu.reshape(w5, w5)
    wout = g_out.reshape(d, d)

    abar_r, abar_i, coef_r, coef_i = _s5_prep(s5_a_re[0], s5_a_im[0], s5_log_dt.reshape(ng_, 1))
    lanes = ng_ * np_

    def fold_coef(cr_, ci_, bre, bim):
        return cr_[..., None] * bre - ci_[..., None] * bim, cr_[..., None] * bim + ci_[..., None] * bre

    (bf_re, bf_im), fold_vjp = jax.vjp(fold_coef, coef_r, coef_i, s5_b_re[0], s5_b_im[0])
    par = (abar_r.reshape(1, lanes), abar_i.reshape(1, lanes),
           _bd_in(bf_re), _bd_in(bf_im), _bd_out(s5_c_re[0]), -_bd_out(s5_c_im[0]), s5_d.reshape(1, w5))
    (y_s5, car, cai), (g_up,) = _s5_fwd(proj, par, t, w5, rider=_gather_plan([slots["w_up"]], []))
    z = _s5glu_fwd(y_s5, wglu, s5_b_glu)
    ys = _mm(z, g_ps5, "nn", _F32, tm=tm_big, tn=g_ps5.shape[2], tk=w5, name="mm_proj_s5")

    lb = _lb_prep(hgrn_lb_logits)
    og, st_all = _hgrn_fwd(proj, lb, hgrn_norm_g, t, w5, hw)
    yh = _mm(og, g_ph, "nn", _F32, tm=tm_big, tn=g_ph.shape[2], tk=hw, name="mm_proj_hgrn")

    merged = _merge_fwd(proj, ys, yh, t, d, gs_off)
    x2 = _mm(merged, wout, "nn", _F32, tm=tm_big, tn=_tile(d, 1024), tk=d, res=xt, name="mm_out")
    h2 = _rms_fwd(x2, ln_ffn_g, "rms2_fwd")
    up_s = g_up.shape[2]
    w_conv = _tile(dff, 1408)
    nbh = dff // w_conv
    tn_up = _tile(up_s, 1408)
    tk_dh2 = _tile(up_s, w_conv)
    tn_gwup = _tile(up_s // 2, 1408)
    assert w_conv % tn_up == 0 and w_conv % tk_dh2 == 0 and w_conv % tn_gwup == 0
    up, (g_down,) = _mm(h2, g_up, "nn", _BF, tm=tm_big, tn=tn_up, tk=d, name="mm_up",
                        colperm=_pair_perm(nbh, w_conv // tn_up), rider=_gather_plan([slots["w_down"]], []))
    wdown = g_down.reshape(dff, d)
    act = _convact_fwd(up, cw, cb, t, dff)
    x3 = _mm(act, wdown, "nn", _F32, tm=tm_big, tn=_tile(d, 1024), tk=None, res=x2, name="mm_down")
    loss_part, dx3, dx3b, d_gfin = _loss_head(x3, tgt, ln_final_g.reshape(1, d))

    dact = _mm(dx3b, wdown, "nt", _BF, tm=tm_big, tn=_tile(dff, 1408), tk=d, name="mm_dact")
    r_down = dff // _NCHIP
    gw_down = _mm(act, dx3b, "tn", _BF, tm=_tile(r_down, 1408), tn=_tile(d // 2, 1024), tk=None, halves="rows",
                  name="mm_gw_down")
    def chip_sums(grads, name):
        theirs = _swap_halves(grads, name)
        return [_add_pairs(g, th, sel_chip) for g, th in zip(grads, theirs)]

    (s_down,) = chip_sums([gw_down], "swap_halves_d")
    dup, d_cw, d_cb = _convact_bwd(up, dact, cw, cb, t, dff)
    dh2, (r_down,) = _mm(dup, g_up, "nt", _BF, tm=tm_big, tn=_tile(d, 1024), tk=tk_dh2, name="mm_dh2",
                         colperm=_pair_perm(nbh, w_conv // tk_dh2), rider=_exchange_plan([s_down], None))
    gw_up = _mm(h2, dup, "tn", _BF, tm=_tile(d, 1024), tn=tn_gwup, tk=None, halves="cols", name="mm_gw_up",
                colperm=_pair_perm(nbh, w_conv // tn_gwup))
    dx2, dx2b, d_gffn = _rms_bwd(x2, ln_ffn_g, dh2, dx3, "rms2_bwd")
    dmerged = _mm(dx2b, wout, "nt", _BF, tm=tm_big, tn=_tile(d, 1024), tk=d, name="mm_dmerged")
    gw_out = _mm(merged, dx2b, "tn", _BF, tm=_tile(d // _NCHIP, 1024), tn=_tile(d // 2, 1024), tk=None, halves="rows",
                 name="mm_gw_out")
    dys, dyh, dgs, dgh = _merge_bwd(proj, ys, yh, dmerged, t, d, gs_off)
    ps_s = g_ps5.shape[2]
    dz = _mm(dys, g_ps5, "nt", _BF, tm=tm_big, tn=_tile(w5, 1024), tk=ps_s, name="mm_dz")
    gw_ps5 = _mm(z, dys, "tn", _BF, tm=_tile(w5, 1024), tn=ps_s // 2, tk=None, halves="cols", name="mm_gw_ps5")
    dog = _mm(dyh, g_ph, "nt", _BF, tm=tm_big, tn=_tile(hw, 1024), tk=ps_s, name="mm_dog")
    gw_ph = _mm(og, dyh, "tn", _BF, tm=_tile(hw, 1024), tn=ps_s // 2, tk=None, halves="cols", name="mm_gw_ph")
    dy_s5, gw_glu_full, d_bglu = _s5glu_bwd(y_s5, dz, wglu, s5_b_glu)
    r_glu = w5 // _NCHIP
    gw_glu = gw_glu_full.astype(_BF).reshape(_NCHIP, r_glu, 2, w5 // 2).transpose(2, 0, 1, 3)

    s_glu, s_ps5, s_ph, s_out, s_up = chip_sums([gw_glu, gw_ps5, gw_ph, gw_out, gw_up], "swap_halves_a")
    s5g, (r_glu_, r_ps5, r_ph, r_out, r_up) = _s5_bwd(
        proj, dy_s5, car, cai, par, t, w5, rider=_exchange_plan([s_glu, s_ps5, s_ph, s_out, s_up], None))
    du = s5g[0]
    dq, df, di, dg, d_lb, d_ng = _hgrn_bwd(proj, lb, hgrn_norm_g, st_all, dog, t, w5, hw)
    dproj = jnp.concatenate([du, dq, df, di, dg, dgs, dgh], axis=1)

    d_coef_r, d_coef_i, d_bre, d_bim = fold_vjp((_bd_in_grad(s5g[3], np_, gc), _bd_in_grad(s5g[4], np_, gc)))
    d_are, d_aim, d_ldt = _s5_prep_bwd(s5_a_re[0], s5_a_im[0], s5_log_dt.reshape(ng_, 1),
                                       [s5g[1].reshape(ng_, np_), s5g[2].reshape(ng_, np_), d_coef_r, d_coef_i])
    d_cre = _bd_out_grad(s5g[5], np_, gc)
    d_cim = -_bd_out_grad(s5g[6], np_, gc)
    d_logits = _lb_prep_bwd(hgrn_lb_logits, d_lb)

    small_names = ["s5_a_re", "s5_a_im", "s5_log_dt", "s5_b_re", "s5_b_im", "s5_c_re", "s5_c_im", "s5_d",
                   "s5_b_glu", "hgrn_lb_logits", "hgrn_norm_g", "ln_ffn_g", "conv_b", "ln_final_g", "ln_mix_g"]
    small_w = dict(ln_mix_g=ln_mix_g, s5_a_re=s5_a_re, s5_a_im=s5_a_im, s5_log_dt=s5_log_dt, s5_b_re=s5_b_re,
                   s5_b_im=s5_b_im, s5_c_re=s5_c_re, s5_c_im=s5_c_im, s5_d=s5_d, s5_b_glu=s5_b_glu,
                   hgrn_lb_logits=hgrn_lb_logits, hgrn_norm_g=hgrn_norm_g, ln_ffn_g=ln_ffn_g, conv_b=conv_b,
                   ln_final_g=ln_final_g)
    small_m = dict(ln_mix_g=m_ln_mix_g, s5_a_re=m_s5_a_re, s5_a_im=m_s5_a_im, s5_log_dt=m_s5_log_dt, s5_b_re=m_s5_b_re,
                   s5_b_im=m_s5_b_im, s5_c_re=m_s5_c_re, s5_c_im=m_s5_c_im, s5_d=m_s5_d, s5_b_glu=m_s5_b_glu,
                   hgrn_lb_logits=m_hgrn_lb_logits, hgrn_norm_g=m_hgrn_norm_g, ln_ffn_g=m_ln_ffn_g, conv_b=m_conv_b,
                   ln_final_g=m_ln_final_g)
    small_v = dict(ln_mix_g=v_ln_mix_g, s5_a_re=v_s5_a_re, s5_a_im=v_s5_a_im, s5_log_dt=v_s5_log_dt, s5_b_re=v_s5_b_re,
                   s5_b_im=v_s5_b_im, s5_c_re=v_s5_c_re, s5_c_im=v_s5_c_im, s5_d=v_s5_d, s5_b_glu=v_s5_b_glu,
                   hgrn_lb_logits=v_hgrn_lb_logits, hgrn_norm_g=v_hgrn_norm_g, ln_ffn_g=v_ln_ffn_g, conv_b=v_conv_b,
                   ln_final_g=v_ln_final_g)
    small_g = dict(s5_a_re=d_are, s5_a_im=d_aim, s5_log_dt=d_ldt, s5_b_re=d_bre, s5_b_im=d_bim,
                   s5_c_re=d_cre, s5_c_im=d_cim, s5_d=s5g[7], s5_b_glu=d_bglu, hgrn_lb_logits=d_logits,
                   hgrn_norm_g=d_ng, ln_ffn_g=d_gffn, conv_b=d_cb, ln_final_g=d_gfin)
    like = [small_w[k] for k in small_names]
    assert small_names[-1] == "ln_mix_g"
    pack_a = _pack([small_g[k] for k in small_names[:-1]] + [d_cw])
    gw_in, (r_small_a,) = _mm(h1, dproj, "tn", _BF, tm=_tile(d, 1024), tn=_tile(nin_s // 2, 1152), tk=None, halves="cols",
                              name="mm_gw_in", rider=_exchange_plan([], pack_a))
    (s_in,) = chip_sums([gw_in], "swap_halves_b")
    dh1, (r_in,) = _mm(dproj, g_in, "nt", _BF, tm=tm_big, tn=_tile(d, 1024), tk=None, name="mm_dh1",
                       rider=_exchange_plan([s_in], None))
    dx, _, d_gmix = _rms_bwd(xt, ln_mix_g, dh1, dx2, "rms1_bwd")
    (r_small_b,) = _run_plan(_exchange_plan([], _pack([d_gmix])), "exchange_gmix")
    sums = [s_in, s_glu, s_ps5, s_ph, s_out, s_up, s_down]
    received = [r_in, r_glu_, r_ps5, r_ph, r_out, r_up, r_down]
    halves = [_shard_sum(sm, rc, sel_chip) for sm, rc in zip(sums, received)]
    g_a = _sum_slots(r_small_a, "small_sum_a")
    g_b = _sum_slots(r_small_b, "small_sum_b")
    full = _share_halves(halves)
    w_pack = _pack(like)
    rows_a = w_pack.shape[0] - g_b.shape[0]
    g_small = jnp.concatenate([g_a[:rows_a], g_b], axis=0)
    cs = conv_w.shape[2]
    g_cw_full = g_a[rows_a:].reshape(-1)[:3 * 2 * dff].reshape(3, 2 * dff)
    g_cw = lax.dynamic_slice_in_dim(g_cw_full, chip * cs, cs, axis=1)

    big_m = dict(w_in=m_w_in, s5_w_glu=m_s5_w_glu, w_proj_s5=m_w_proj_s5, w_proj_hgrn=m_w_proj_hgrn, w_out=m_w_out,
                 w_up=m_w_up, w_down=m_w_down)
    big_v = dict(w_in=v_w_in, s5_w_glu=v_s5_w_glu, w_proj_s5=v_w_proj_s5, w_proj_hgrn=v_w_proj_hgrn, w_out=v_w_out,
                 w_up=v_w_up, w_down=v_w_down)
    res = {}
    for k, g2 in zip(big_names, full):
        w2 = big_w[k]
        shp = (1,) + w2.shape
        outs = _adamw_shard(w2, g2, big_m[k][0], big_v[k][0], "adamw_" + k)
        res[k] = [o.reshape(shp) for o in outs]
    sm_outs = _adamw_many(like, g_small, [small_m[k] for k in small_names], [small_v[k] for k in small_names])
    for k, outs in zip(small_names, sm_outs):
        res[k] = outs

    cw_outs = _adamw_whole(conv_w[0], g_cw, m_conv_w[0], v_conv_w[0], "adamw_conv_w")
    res["conv_w"] = [g_cw.reshape(conv_w.shape)] + [o.reshape(conv_w.shape) for o in cw_outs]

    loss = lax.psum(loss_part[0, 0], ("x", "y", "c"))
    order = ["ln_mix_g", "w_in", "s5_a_re", "s5_a_im", "s5_log_dt", "s5_b_re", "s5_b_im", "s5_c_re", "s5_c_im", "s5_d",
             "s5_w_glu", "s5_b_glu", "w_proj_s5", "hgrn_lb_logits", "hgrn_norm_g", "w_proj_hgrn", "w_out", "ln_ffn_g",
             "w_up", "conv_w", "conv_b", "w_down", "ln_final_g"]
    return (loss, dx.reshape(x.shape), *[res[k][0] for k in order], *[res[k][1] for k in order],
            *[res[k][2] for k in order], *[res[k][3] for k in order])
```

```python
import functools

import jax
import jax.numpy as jnp
from jax import lax
from jax.experimental import pallas as pl
from jax.experimental.pallas import tpu as pltpu

_F32 = jnp.float32
_BF = jnp.bfloat16
_RMS_EPS = 1e-6
_S5_MAX_RE = -1e-4
_LR, _B1, _B2, _ADAM_EPS, _WD, _STEP = 0.001, 0.9, 0.999, 1e-08, 0.01, 10
_MESH = pl.DeviceIdType.MESH
_ANY = pl.BlockSpec(memory_space=pl.ANY)
_LANE = 128
_VMEM_LIMIT = 56 * 1024 * 1024
_CHUNK = 64
_S5_TB = 512
_S5_SET = 8
_HGRN_HP = 8
_HGRN_SEG = 512
_NCHIP = 4
_NDEV = 8


def _call(body, **kw):
    return pl.pallas_call(body, **kw)


def _cparams(*sem):
    return pltpu.CompilerParams(dimension_semantics=sem, vmem_limit_bytes=_VMEM_LIMIT)


def _tile(n, pref, unit=_LANE):
    if n <= pref:
        return n
    t = (pref // unit) * unit
    while t >= unit:
        if n % t == 0:
            return t
        t -= unit
    raise ValueError(f"no tile for {n}")


_OPERAND_BYTES = 12 * 1024 * 1024


def _tk_fit(k, tm, tn):
    best = _LANE
    for tk in range(_LANE, k + 1, _LANE):
        if k % tk == 0 and (tm + tn) * tk * 2 <= _OPERAND_BYTES:
            best = tk
    return best if k % _LANE == 0 else k


_NN = ((1,), (0,))
_NT = ((1,), (1,))
_TN = ((0,), (0,))


def _dg(a, b, dims):
    return lax.dot_general(a.astype(_BF), b.astype(_BF), (dims, ((), ())), preferred_element_type=_F32)


@jax.custom_vjp
def _bdot(a, b):
    return _dg(a, b, _NN)


def _bdot_f(a, b):
    return _dg(a, b, _NN), (a, b)


def _bdot_b(res, g):
    a, b = res
    return _dg(g, b, _NT).astype(a.dtype), _dg(a, g, _TN).astype(b.dtype)


_bdot.defvjp(_bdot_f, _bdot_b)


@jax.custom_vjp
def _bdot_nt(a, b):
    return _dg(a, b, _NT)


def _bdot_nt_f(a, b):
    return _dg(a, b, _NT), (a, b)


def _bdot_nt_b(res, g):
    a, b = res
    return _dg(g, b, _NN).astype(a.dtype), _dg(g, a, _TN).astype(b.dtype)


_bdot_nt.defvjp(_bdot_nt_f, _bdot_nt_b)


@jax.custom_vjp
def _bdot_tn(a, b):
    return _dg(a, b, _TN)


def _bdot_tn_f(a, b):
    return _dg(a, b, _TN), (a, b)


def _bdot_tn_b(res, g):
    a, b = res
    return _dg(b, g, _NT).astype(a.dtype), _dg(a, g, _NN).astype(b.dtype)


_bdot_tn.defvjp(_bdot_tn_f, _bdot_tn_b)


_SUBLANES = 8


def _shift_up(x, n):
    r = x.shape[0]
    if n % _SUBLANES == 0:
        return jnp.concatenate([x[n:], jnp.zeros((n,) + x.shape[1:], x.dtype)], axis=0)
    row = lax.broadcasted_iota(jnp.int32, x.shape, 0)
    return jnp.where(row < r - n, pltpu.roll(x, r - n, 0), 0.0)


@functools.partial(jax.custom_vjp, nondiff_argnums=(1,))
def _shift_down(x, n):
    if n % _SUBLANES == 0:
        return jnp.concatenate([jnp.zeros((n,) + x.shape[1:], x.dtype), x[:x.shape[0] - n]], axis=0)
    row = lax.broadcasted_iota(jnp.int32, x.shape, 0)
    return jnp.where(row >= n, pltpu.roll(x, n, 0), 0.0)


def _shift_down_f(x, n):
    return _shift_down(x, n), None


def _shift_down_b(n, _, g):
    return (_shift_up(g, n),)


_shift_down.defvjp(_shift_down_f, _shift_down_b)


def _rows_apart(x):
    return tuple(x[k:k + _SUBLANES] for k in range(0, x.shape[0], _SUBLANES))


@jax.custom_vjp
def _split_rows(x):
    return _rows_apart(x)


_split_rows.defvjp(lambda x: (_rows_apart(x), None), lambda _, gs: (jnp.concatenate(gs, axis=0),))


@jax.custom_vjp
def _join_rows(pieces):
    return jnp.concatenate(pieces, axis=0)


_join_rows.defvjp(lambda pieces: (jnp.concatenate(pieces, axis=0), None), lambda _, g: (_rows_apart(g),))


@jax.custom_vjp
def _last_row(x):
    return x[_SUBLANES - 1:_SUBLANES]


def _last_row_b(_, g):
    row = lax.broadcasted_iota(jnp.int32, (_SUBLANES, g.shape[1]), 0)
    return (jnp.where(row == _SUBLANES - 1, g, 0.0),)


_last_row.defvjp(lambda x: (x[_SUBLANES - 1:_SUBLANES], None), _last_row_b)


def _sigmoid(x):
    return 1.0 / (1.0 + jnp.exp(-x))


def _silu(x):
    return x * _sigmoid(x)


def _gelu(x):
    return 0.5 * x * (1.0 + jnp.tanh(0.7978845608028654 * (x + 0.044715 * (x * x * x))))


def _rms_core(x, g):
    return x * lax.rsqrt(jnp.mean(x * x, axis=-1, keepdims=True) + _RMS_EPS) * g


def _mm(a, b, mode, out_dtype, *, tm, tn, tk, res=None, halves=None, rider=None, colperm=None, name):
    if colperm is None:
        def colperm(n_):
            return n_
    if tk is None:
        kdim = a.shape[0] if mode == "tn" else (b.shape[2] if (mode == "nt" and b.ndim == 3) else a.shape[1])
        tk = _tk_fit(kdim, tm, tn)
    if mode == "nn":
        m, k = a.shape
        a_spec = pl.BlockSpec((tm, tk), lambda i, j, kk: (i, kk))
        if b.ndim == 3:
            s, _, ns = b.shape
            n = s * ns
            npb = ns // tn
            b_spec = pl.BlockSpec((None, tk, tn), lambda i, j, kk: (j // npb, kk, j % npb))
        else:
            n = b.shape[1]
            b_spec = pl.BlockSpec((tk, tn), lambda i, j, kk: (kk, j))
        dims = _NN
    elif mode == "nt":
        m, k = a.shape
        a_spec = pl.BlockSpec((tm, tk), lambda i, j, kk: (i, colperm(kk)))
        if b.ndim == 3:
            s, n, ks = b.shape
            kpb = ks // tk
            b_spec = pl.BlockSpec((None, tn, tk), lambda i, j, kk: (kk // kpb, j, kk % kpb))
        else:
            n = b.shape[0]
            b_spec = pl.BlockSpec((tn, tk), lambda i, j, kk: (j, kk))
        dims = _NT
    else:
        k, m = a.shape
        n = b.shape[1]
        a_spec = pl.BlockSpec((tk, tm), lambda i, j, kk: (kk, i))
        b_spec = pl.BlockSpec((tk, tn), lambda i, j, kk: (kk, colperm(j)))
        dims = _TN
    nk = k // tk
    if halves is None:
        out_shape = jax.ShapeDtypeStruct((m, n), out_dtype)
        out_spec = pl.BlockSpec((tm, tn), lambda i, j, kk: (i, colperm(j) if mode == "nn" else j))
    elif halves == "cols":
        c2 = n // (2 * _NCHIP)
        tpc = c2 // tn
        out_shape = jax.ShapeDtypeStruct((2, _NCHIP, m, c2), out_dtype)
        out_spec = pl.BlockSpec((None, None, tm, tn),
                                lambda i, j, kk: ((j // tpc) % 2, j // (2 * tpc), i, j % tpc))
    else:
        c2 = n // 2
        tpc = c2 // tn
        r = m // _NCHIP
        tpr = r // tm
        out_shape = jax.ShapeDtypeStruct((2, _NCHIP, r, c2), out_dtype)
        out_spec = pl.BlockSpec((None, None, tm, tn),
                                lambda i, j, kk: (j // tpc, i // tpr, i % tpr, j % tpc))
    has_res = res is not None
    nreg = 3 if has_res else 2
    ni, nj = m // tm, n // tn
    r_ins = list(rider.ins) if rider else []
    r_outs = list(rider.outs) if rider else []

    def body(*refs):
        a_ref, b_ref = refs[0], refs[1]
        r_ref = refs[2] if has_res else None
        rin = refs[nreg:nreg + len(r_ins)]
        o_ref = refs[nreg + len(r_ins)]
        rout = refs[nreg + len(r_ins) + 1:nreg + len(r_ins) + 1 + len(r_outs)]
        acc_ref = refs[nreg + len(r_ins) + 1 + len(r_outs)]
        sems = refs[nreg + len(r_ins) + 2 + len(r_outs):]
        i, j, kk = pl.program_id(0), pl.program_id(1), pl.program_id(2)

        if rider:
            @pl.when(jnp.logical_and(jnp.logical_and(i == 0, j == 0), kk == 0))
            def _():
                rider.start(rin, rout, sems)

        @pl.when(kk == 0)
        def _():
            acc_ref[...] = jnp.zeros_like(acc_ref)

        acc_ref[...] += _dg(a_ref[...], b_ref[...], dims)

        @pl.when(kk == nk - 1)
        def _():
            out = acc_ref[...]
            if has_res:
                out = out + r_ref[...]
            o_ref[...] = out.astype(out_dtype)

        if rider:
            @pl.when(jnp.logical_and(jnp.logical_and(i == ni - 1, j == nj - 1), kk == nk - 1))
            def _():
                rider.finish(rin, rout, sems)

    in_specs = [a_spec, b_spec]
    args = [a, b]
    if has_res:
        in_specs.append(pl.BlockSpec((tm, tn), lambda i, j, kk: (i, j)))
        args.append(res)
    if not rider:
        return _call(body, name=name, grid=(ni, nj, nk), in_specs=in_specs, out_specs=out_spec,
                     out_shape=out_shape, scratch_shapes=[pltpu.VMEM((tm, tn), _F32)],
                     compiler_params=_cparams("parallel", "parallel", "arbitrary"))(*args)
    res_all = _call(body, name=name, grid=(ni, nj, nk), in_specs=in_specs + [_ANY] * len(r_ins),
                    out_specs=[out_spec] + [_ANY] * len(r_outs), out_shape=[out_shape] + r_outs,
                    input_output_aliases={nreg + k: 1 + v for k, v in rider.aliases.items()},
                    scratch_shapes=[pltpu.VMEM((tm, tn), _F32)] + list(rider.sems),
                    compiler_params=_cparams("arbitrary", "arbitrary", "arbitrary"))(*args, *r_ins)
    return res_all[0], list(res_all[1:])


def _rowcall(fn, ins, in_specs, outs, out_specs, acc, *, grid, name):
    nin = len(ins)

    def body(*refs):
        vals = fn(*[r[...] for r in refs[:nin]])
        first = pl.program_id(1) == 0
        for k, (o_ref, v) in enumerate(zip(refs[nin:], vals)):
            if acc[k]:
                @pl.when(first)
                def _(o_ref=o_ref):
                    o_ref[...] = jnp.zeros_like(o_ref)
                o_ref[...] += v.astype(o_ref.dtype)
            else:
                o_ref[...] = v.astype(o_ref.dtype)

    return _call(body, name=name, grid=grid, in_specs=in_specs, out_specs=out_specs, out_shape=outs,
                 compiler_params=_cparams("parallel", "arbitrary"))(*ins)


def _rb(tm, w, cb=0):
    return pl.BlockSpec((tm, w), lambda j, i: (i, cb + j))


def _cb(w, cb=0):
    return pl.BlockSpec((1, w), lambda j, i: (0, cb + j))


def _full(shape):
    nd = len(shape)
    return pl.BlockSpec(shape, lambda j, i: (0,) * nd)


def _sds(shape, dtype):
    return jax.ShapeDtypeStruct(shape, dtype)


def _rms_fwd(x, g, name):
    t, d = x.shape
    tm = _tile(t, 512, 8)
    return _rowcall(lambda xb, gb: (_rms_core(xb, gb),), [x, g], [_rb(tm, d), _full((1, d))],
                    [_sds((t, d), _BF)], [_rb(tm, d)], [False], grid=(1, t // tm), name=name)[0]


def _rms_bwd(x, g, dh, dres, name):
    t, d = x.shape
    tm = _tile(t, 256, 8)

    def fn(xb, gb, dhb, drb):
        _, vjp = jax.vjp(_rms_core, xb, gb)
        dx, dg = vjp(dhb.astype(_F32))
        dx = dx + drb
        return dx, dx, dg

    return _rowcall(fn, [x, g, dh, dres], [_rb(tm, d), _full((1, d)), _rb(tm, d), _rb(tm, d)],
                    [_sds((t, d), _F32), _sds((t, d), _BF), _sds((1, d), _F32)],
                    [_rb(tm, d), _rb(tm, d), _full((1, d))], [False, False, True],
                    grid=(1, t // tm), name=name)


def _loss_head(x3, tgt, g):
    t, d = x3.shape
    tm = _tile(t, 256, 8)

    def fn(xb, tb, gb):
        y, vjp = jax.vjp(_rms_core, xb, gb)
        e = y - tb
        part = 0.5 * jnp.sum(jnp.mean(e * e, axis=-1, keepdims=True), axis=0, keepdims=True)
        dx, dg = vjp(e * (1.0 / d))
        return jnp.broadcast_to(part, (1, _LANE)), dx, dx, dg

    return _rowcall(fn, [x3, tgt, g], [_rb(tm, d), _rb(tm, d), _full((1, d))],
                    [_sds((1, _LANE), _F32), _sds((t, d), _F32), _sds((t, d), _BF), _sds((1, d), _F32)],
                    [_full((1, _LANE)), _rb(tm, d), _rb(tm, d), _full((1, d))], [True, False, False, True],
                    grid=(1, t // tm), name="loss_head")


def _s5_disc(a_re, a_im, log_dt):
    lam_re = jnp.minimum(a_re, _S5_MAX_RE)
    lam_im = a_im
    dt = jnp.exp(log_dt)
    mag = jnp.exp(lam_re * dt)
    abar_re = mag * jnp.cos(lam_im * dt)
    abar_im = mag * jnp.sin(lam_im * dt)
    den = lam_re * lam_re + lam_im * lam_im
    nr = abar_re - 1.0
    ni = abar_im
    coef_re = (nr * lam_re + ni * lam_im) / den
    coef_im = (ni * lam_re - nr * lam_im) / den
    return abar_re, abar_im, coef_re, coef_im


def _s5_prep(a_re, a_im, log_dt):
    g, p = a_re.shape

    def body(ar, ai, ld, o0, o1, o2, o3):
        outs = _s5_disc(ar[...], ai[...], ld[...])
        for o, v in zip((o0, o1, o2, o3), outs):
            o[...] = v

    return _call(body, name="s5_prep", out_shape=[_sds((g, p), _F32)] * 4)(a_re, a_im, log_dt)


def _s5_prep_bwd(a_re, a_im, log_dt, cts):
    g, p = a_re.shape

    def body(ar, ai, ld, c0, c1, c2, c3, d0, d1, d2):
        _, vjp = jax.vjp(_s5_disc, ar[...], ai[...], ld[...])
        outs = vjp((c0[...], c1[...], c2[...], c3[...]))
        for o, v in zip((d0, d1, d2), outs):
            o[...] = v

    return _call(body, name="s5_prep_bwd",
                 out_shape=[_sds((g, p), _F32), _sds((g, p), _F32), _sds((g, 1), _F32)])(a_re, a_im, log_dt, *cts)


def _s5_block(u, car, cai, ar, ai, b_re, b_im, c_re, c_imn, dvec):
    bur = _bdot(u, b_re)
    bui = _bdot(u, b_im)
    shape8 = (_SUBLANES, ar.shape[1])
    pows = [(ar, ai)]
    for _ in range(2):
        pr, pi = pows[-1]
        pows.append((pr * pr - pi * pi, 2.0 * pr * pi))
    pows = [(jnp.broadcast_to(pr, shape8), jnp.broadcast_to(pi, shape8)) for pr, pi in pows]

    def scan8(xr, xi):
        for k, (pr, pi) in enumerate(pows):
            dr = _shift_down(xr, 1 << k)
            di = _shift_down(xi, 1 << k)
            xr, xi = xr + pr * dr - pi * di, xi + pr * di + pi * dr
        return xr, xi

    row8 = lax.broadcasted_iota(jnp.int32, (_SUBLANES, ar.shape[1]), 0)
    tr, ti = scan8(jnp.where(row8 == 0, ar, 0.0), jnp.where(row8 == 0, ai, 0.0))
    outs_r, outs_i = [], []
    for xr, xi in zip(_split_rows(bur), _split_rows(bui)):
        xr, xi = scan8(xr, xi)
        xr, xi = xr + tr * car - ti * cai, xi + tr * cai + ti * car
        car, cai = _last_row(xr), _last_row(xi)
        outs_r.append(xr)
        outs_i.append(xi)
    y = _bdot(_join_rows(tuple(outs_r)), c_re) + _bdot(_join_rows(tuple(outs_i)), c_imn) + dvec * u
    return y, car, cai


def _s5_specs(tb, lw, nt, rev):
    tmap = (lambda t: nt - 1 - t) if rev else (lambda t: t)
    vec = pl.BlockSpec((1, lw), lambda s, t: (0, s))
    return dict(
        u=pl.BlockSpec((tb, _LANE), lambda s, t: (tmap(t), s)),
        car=pl.BlockSpec((None, 1, lw), lambda s, t: (tmap(t), 0, s)),
        vec=vec,
        bmat=pl.BlockSpec((None, _LANE, lw), lambda s, t: (s, 0, 0)),
        cmat=pl.BlockSpec((None, lw, _LANE), lambda s, t: (s, 0, 0)),
        dvec=pl.BlockSpec((1, _LANE), lambda s, t: (0, s)),
    )


def _s5_fwd(proj, par, t, w5, rider=None):
    ar, ai, b_re, b_im, c_re, c_imn, dvec = par
    ns = w5 // _LANE
    lw = ar.shape[1] // ns
    tb = min(_S5_TB, t)
    nt = t // tb
    sp = _s5_specs(tb, lw, nt, False)

    def body(u_ref, ar_r, ai_r, bre_r, bim_r, cre_r, cim_r, d_r, y_ref, car_ref, cai_ref, s_r, s_i):
        @pl.when(pl.program_id(1) == 0)
        def _():
            s_r[...] = jnp.zeros_like(s_r)
            s_i[...] = jnp.zeros_like(s_i)

        car_ref[...] = s_r[...]
        cai_ref[...] = s_i[...]
        y, ncr, nci = _s5_block(u_ref[...], s_r[...], s_i[...], ar_r[...], ai_r[...],
                                bre_r[...], bim_r[...], cre_r[...], cim_r[...], d_r[...])
        y_ref[...] = y
        s_r[...] = ncr
        s_i[...] = nci

    kw = dict(name="s5_fwd", grid=(ns, nt),
              in_specs=[sp["u"], sp["vec"], sp["vec"], sp["bmat"], sp["bmat"],
                        sp["cmat"], sp["cmat"], sp["dvec"]],
              out_specs=[sp["u"], sp["car"], sp["car"]],
              out_shape=[_sds((t, w5), _F32), _sds((nt, 1, ns * lw), _F32), _sds((nt, 1, ns * lw), _F32)],
              scratch_shapes=[pltpu.VMEM((1, lw), _F32), pltpu.VMEM((1, lw), _F32)])
    args = (proj, ar, ai, b_re, b_im, c_re, c_imn, dvec)
    if rider is None:
        return _call(body, compiler_params=_cparams("parallel", "arbitrary"), **kw)(*args), []
    return _call_riding(body, rider, args=args, **kw)


def _s5_bwd(proj, dy, car, cai, par, t, w5, rider=None):
    ar, ai, b_re, b_im, c_re, c_imn, dvec = par
    ns = w5 // _LANE
    lw = ar.shape[1] // ns
    tb = min(_S5_TB, t)
    nt = t // tb
    sp = _s5_specs(tb, lw, nt, True)

    def body(u_ref, dy_ref, car_ref, cai_ref, ar_r, ai_r, bre_r, bim_r, cre_r, cim_r, d_r,
             du_ref, g_ar, g_ai, g_bre, g_bim, g_cre, g_cim, g_d, ds_r, ds_i):
        accs = (g_ar, g_ai, g_bre, g_bim, g_cre, g_cim, g_d)

        @pl.when(pl.program_id(1) == 0)
        def _():
            ds_r[...] = jnp.zeros_like(ds_r)
            ds_i[...] = jnp.zeros_like(ds_i)
            for o in accs:
                o[...] = jnp.zeros_like(o)

        _, vjp = jax.vjp(_s5_block, u_ref[...], car_ref[...], cai_ref[...], ar_r[...], ai_r[...],
                         bre_r[...], bim_r[...], cre_r[...], cim_r[...], d_r[...])
        grads = vjp((dy_ref[...], ds_r[...], ds_i[...]))
        du_ref[...] = grads[0].astype(_BF)
        ds_r[...] = grads[1]
        ds_i[...] = grads[2]
        for o, gval in zip(accs, grads[3:]):
            o[...] += gval

    vec_o = _sds((1, ns * lw), _F32)
    kw = dict(name="s5_bwd", grid=(ns, nt),
              in_specs=[sp["u"], sp["u"], sp["car"], sp["car"], sp["vec"], sp["vec"],
                        sp["bmat"], sp["bmat"], sp["cmat"], sp["cmat"], sp["dvec"]],
              out_specs=[sp["u"], sp["vec"], sp["vec"], sp["bmat"], sp["bmat"],
                         sp["cmat"], sp["cmat"], sp["dvec"]],
              out_shape=[_sds((t, w5), _BF), vec_o, vec_o,
                         _sds(b_re.shape, _F32), _sds(b_re.shape, _F32), _sds(c_re.shape, _F32),
                         _sds(c_re.shape, _F32), _sds((1, w5), _F32)],
              scratch_shapes=[pltpu.VMEM((1, lw), _F32), pltpu.VMEM((1, lw), _F32)])
    args = (proj, dy, car, cai, ar, ai, b_re, b_im, c_re, c_imn, dvec)
    if rider is None:
        return _call(body, compiler_params=_cparams("parallel", "arbitrary"), **kw)(*args), []
    return _call_riding(body, rider, args=args, **kw)


def _bd_in(b):
    g, p, c = b.shape
    s = g // _S5_SET
    b4 = b.reshape(s, _S5_SET, p, c).transpose(0, 1, 3, 2)
    eye = jnp.eye(_S5_SET, dtype=b.dtype)
    return (b4[:, :, :, None, :] * eye[None, :, None, :, None]).reshape(s, _S5_SET * c, _S5_SET * p)


def _bd_in_grad(d, p, c):
    s = d.shape[0]
    eye = jnp.eye(_S5_SET, dtype=d.dtype)
    d5 = d.reshape(s, _S5_SET, c, _S5_SET, p) * eye[None, :, None, :, None]
    return d5.sum(axis=3).transpose(0, 1, 3, 2).reshape(s * _S5_SET, p, c)


def _bd_out(cm):
    g, c, p = cm.shape
    s = g // _S5_SET
    c4 = cm.reshape(s, _S5_SET, c, p).transpose(0, 1, 3, 2)
    eye = jnp.eye(_S5_SET, dtype=cm.dtype)
    return (c4[:, :, :, None, :] * eye[None, :, None, :, None]).reshape(s, _S5_SET * p, _S5_SET * c)


def _bd_out_grad(d, p, c):
    s = d.shape[0]
    eye = jnp.eye(_S5_SET, dtype=d.dtype)
    d5 = d.reshape(s, _S5_SET, p, _S5_SET, c) * eye[None, :, None, :, None]
    return d5.sum(axis=3).transpose(0, 1, 3, 2).reshape(s * _S5_SET, c, p)


def _s5glu_fwd(y, wglu, bglu):
    t, w5 = y.shape
    tm = _tile(t, 256, 8)

    def fn(yb, wb, bb):
        z1 = _gelu(yb)
        a = _dg(z1, wb, _NN) + bb
        return (z1 * _sigmoid(a),)

    return _rowcall(fn, [y, wglu, bglu], [_rb(tm, w5), _full(wglu.shape), _full((1, w5))],
                    [_sds((t, w5), _BF)], [_rb(tm, w5)], [False], grid=(1, t // tm), name="s5glu_fwd")[0]


def _s5glu_bwd(y, dz, wglu, bglu):
    t, w5 = y.shape
    tm = _tile(t, 256, 8)

    def fn(yb, dzb, wb, bb):
        dzb = dzb.astype(_F32)
        z1, gelu_vjp = jax.vjp(_gelu, yb)
        sig = _sigmoid(_dg(z1, wb, _NN) + bb)
        da = dzb * z1 * sig * (1.0 - sig)
        dz1 = dzb * sig + _dg(da, wb, _NT)
        (dy,) = gelu_vjp(dz1)
        return dy, _dg(z1, da, _TN), jnp.sum(da, axis=0, keepdims=True)

    return _rowcall(fn, [y, dz, wglu, bglu], [_rb(tm, w5), _rb(tm, w5), _full(wglu.shape), _full((1, w5))],
                    [_sds((t, w5), _F32), _sds((w5, w5), _F32), _sds((1, w5), _F32)],
                    [_rb(tm, w5), _full((w5, w5)), _full((1, w5))], [False, True, True],
                    grid=(1, t // tm), name="s5glu_bwd")


_LEVELS = (6, 5, 4, 3, 2, 1)


def _tri_stack(c):
    t = jnp.arange(c, dtype=jnp.int32)[:, None]
    j = jnp.arange(c, dtype=jnp.int32)[None, :]
    low = (j <= t).astype(_F32)
    mats = [low]
    for sh in _LEVELS:
        r = ((t >> sh) << sh) + ((1 << (sh - 1)) - 1)
        mats.append(low - (j <= r).astype(_F32))
    stack = jnp.concatenate(mats, axis=0).astype(_BF)
    return stack, stack.T


def _split_dot(mat, x):
    l = x.shape[1]
    hi = x.astype(_BF)
    lo = (x - hi.astype(_F32)).astype(_BF)
    out = jnp.dot(mat, jnp.concatenate([hi, lo], axis=1), preferred_element_type=_F32)
    return out[:, :l] + out[:, l:]


@jax.custom_vjp
def _decay_sums(lf, tri, tri_t):
    c = lf.shape[0]
    out = _split_dot(tri, lf)
    return tuple(out[k * c:(k + 1) * c] for k in range(len(_LEVELS) + 1))


def _decay_sums_f(lf, tri, tri_t):
    return _decay_sums(lf, tri, tri_t), (tri, tri_t)


def _decay_sums_b(res, gs):
    tri, tri_t = res
    return _split_dot(tri_t, jnp.concatenate(gs, axis=0)), jnp.zeros_like(tri), jnp.zeros_like(tri_t)


_decay_sums.defvjp(_decay_sums_f, _decay_sums_b)


def _hgrn_chunk(qi, fi, vi, gi, st, lb, ng, tri, tri_t):
    c = qi.shape[0]
    row = lax.broadcasted_iota(jnp.int32, (c, 1), 0)
    q = _silu(qi)
    lf = jnp.log(lb + (1.0 - lb) * _sigmoid(fi))
    k = (1.0 - lb) * _sigmoid(-fi)
    sums = _decay_sums(lf, tri, tri_t)
    b = sums[0]
    btot = jnp.sum(lf, axis=0, keepdims=True)
    inter = _bdot_nt(q * jnp.exp(b), st)
    tt = lax.broadcasted_iota(jnp.int32, (c, c), 0)
    ss = lax.broadcasted_iota(jnp.int32, (c, c), 1)
    sc = jnp.where(tt == ss, jnp.sum(q * k, axis=1, keepdims=True), 0.0)
    for sh, p in zip(_LEVELS, sums[1:]):
        upper = ((row >> (sh - 1)) & 1) == 1
        qm = jnp.where(upper, q * jnp.exp(jnp.where(upper, p, 0.0)), 0.0)
        km = jnp.where(upper, 0.0, k * jnp.exp(jnp.where(upper, 0.0, -p)))
        sc = sc + jnp.where((tt >> sh) == (ss >> sh), _bdot_nt(qm, km), 0.0)
    o = inter + _bdot(sc, vi)
    st_new = st * jnp.exp(btot) + _bdot_tn(vi, k * jnp.exp(btot - b))
    on = o * lax.rsqrt(jnp.mean(o * o, axis=1, keepdims=True) + _RMS_EPS) * ng
    return on * _silu(gi), st_new


def _hgrn_geom(t, w5, hw):
    hp = _HGRN_HP if (hw // _LANE) % _HGRN_HP == 0 and w5 % (_LANE * _HGRN_HP) == 0 else 1
    seg = min(t, _HGRN_SEG)
    return hp, hp * _LANE, seg, t // seg


def _hgrn_in_specs(seg, wd, w5, hw, tmap):
    nhp = hw // wd
    qb = w5 // wd
    return [pl.BlockSpec((seg, wd), (lambda h, s, k=k: (tmap(s), qb + k * nhp + h))) for k in range(4)]


def _hgrn_fwd(proj, lb, ng, t, w5, hw, rider=None):
    assert _CHUNK == 64
    hp, wd, seg, nseg = _hgrn_geom(t, w5, hw)
    ncs = seg // _CHUNK
    vec = pl.BlockSpec((1, wd), lambda h, s: (0, h))

    tri, tri_t = _tri_stack(_CHUNK)

    def body(q_ref, f_ref, i_ref, g_ref, lb_ref, ng_ref, tri_ref, trit_ref, og_ref, st_ref, s_scr):
        @pl.when(pl.program_id(1) == 0)
        def _():
            s_scr[...] = jnp.zeros_like(s_scr)

        tri_v, trit_v = tri_ref[...], trit_ref[...]

        def step(ci, carry):
            r = pl.ds(pl.multiple_of(ci * _CHUNK, _CHUNK), _CHUNK)
            for hh in range(hp):
                ln = slice(hh * _LANE, (hh + 1) * _LANE)
                st_ref[hh, ci] = s_scr[hh]
                og, sn = _hgrn_chunk(q_ref[r, ln], f_ref[r, ln], i_ref[r, ln], g_ref[r, ln], s_scr[hh],
                                     lb_ref[:, ln], ng_ref[:, ln], tri_v, trit_v)
                og_ref[r, ln] = og.astype(_BF)
                s_scr[hh] = sn
            return carry

        lax.fori_loop(0, ncs, step, 0)

    kw = dict(name="hgrn_fwd", grid=(hw // wd, nseg),
              in_specs=_hgrn_in_specs(seg, wd, w5, hw, lambda s: s) + [
                  vec, vec, pl.BlockSpec(tri.shape, lambda h, s: (0, 0)), pl.BlockSpec(tri_t.shape, lambda h, s: (0, 0))],
              out_specs=[pl.BlockSpec((seg, wd), lambda h, s: (s, h)),
                         pl.BlockSpec((hp, ncs, _LANE, _LANE), lambda h, s: (h, s, 0, 0))],
              out_shape=[_sds((t, hw), _BF), _sds((hw // _LANE, t // _CHUNK, _LANE, _LANE), _F32)],
              scratch_shapes=[pltpu.VMEM((hp, _LANE, _LANE), _F32)])
    args = (proj, proj, proj, proj, lb, ng, tri, tri_t)
    if rider is None:
        return _call(body, compiler_params=_cparams("parallel", "arbitrary"), **kw)(*args), []
    return _call_riding(body, rider, args=args, **kw)


def _hgrn_bwd(proj, lb, ng, st_all, dog, t, w5, hw):
    hp, wd, seg, nseg = _hgrn_geom(t, w5, hw)
    ncs = seg // _CHUNK

    def rev(s):
        return nseg - 1 - s

    vec = pl.BlockSpec((1, wd), lambda h, s: (0, h))
    col = pl.BlockSpec((seg, wd), lambda h, s: (rev(s), h))

    tri, tri_t = _tri_stack(_CHUNK)

    def body(q_ref, f_ref, i_ref, g_ref, lb_ref, ng_ref, tri_ref, trit_ref, st_ref, dog_ref,
             dq_ref, df_ref, di_ref, dg_ref, dlb_ref, dng_ref, ds_scr):
        @pl.when(pl.program_id(1) == 0)
        def _():
            ds_scr[...] = jnp.zeros_like(ds_scr)
            dlb_ref[...] = jnp.zeros_like(dlb_ref)
            dng_ref[...] = jnp.zeros_like(dng_ref)

        tri_v, trit_v = tri_ref[...], trit_ref[...]

        def step(kk, carry):
            ci = ncs - 1 - kk
            r = pl.ds(pl.multiple_of(ci * _CHUNK, _CHUNK), _CHUNK)
            for hh in range(hp):
                ln = slice(hh * _LANE, (hh + 1) * _LANE)
                _, vjp = jax.vjp(_hgrn_chunk, q_ref[r, ln], f_ref[r, ln], i_ref[r, ln], g_ref[r, ln], st_ref[hh, ci],
                                 lb_ref[:, ln], ng_ref[:, ln], tri_v, trit_v)
                dq, df, di, dg, ds, dlb, dng = vjp((dog_ref[r, ln].astype(_F32), ds_scr[hh]))[:7]
                dq_ref[r, ln] = dq.astype(_BF)
                df_ref[r, ln] = df.astype(_BF)
                di_ref[r, ln] = di.astype(_BF)
                dg_ref[r, ln] = dg.astype(_BF)
                ds_scr[hh] = ds
                dlb_ref[:, ln] += dlb
                dng_ref[:, ln] += dng
            return carry

        lax.fori_loop(0, ncs, step, 0)

    return _call(body, name="hgrn_bwd", grid=(hw // wd, nseg),
                 in_specs=_hgrn_in_specs(seg, wd, w5, hw, rev) + [
                     vec, vec, pl.BlockSpec(tri.shape, lambda h, s: (0, 0)), pl.BlockSpec(tri_t.shape, lambda h, s: (0, 0)),
                     pl.BlockSpec((hp, ncs, _LANE, _LANE), lambda h, s: (h, rev(s), 0, 0)), col],
                 out_specs=[col, col, col, col, vec, vec],
                 out_shape=[_sds((t, hw), _BF)] * 4 + [_sds((1, hw), _F32)] * 2,
                 scratch_shapes=[pltpu.VMEM((hp, _LANE, _LANE), _F32)],
                 compiler_params=_cparams("parallel", "arbitrary"))(
                     proj, proj, proj, proj, lb, ng, tri, tri_t, st_all, dog)


def _lb_of(logits):
    mx = jnp.max(logits, axis=0, keepdims=True)
    e = jnp.exp(logits - mx)
    sm = e / jnp.sum(e, axis=0, keepdims=True)
    row = lax.broadcasted_iota(jnp.int32, logits.shape, 0)
    return jnp.sum(jnp.where(row == 0, sm, 0.0), axis=0, keepdims=True)


def _lb_prep(logits):
    def body(l_ref, o_ref):
        o_ref[...] = _lb_of(l_ref[...])

    return _call(body, name="lb_prep", out_shape=_sds((1, logits.shape[1]), _F32))(logits)


def _lb_prep_bwd(logits, dlb):
    def body(l_ref, d_ref, o_ref):
        _, vjp = jax.vjp(_lb_of, l_ref[...])
        o_ref[...] = vjp(d_ref[...])[0]

    return _call(body, name="lb_prep_bwd", out_shape=_sds(logits.shape, _F32))(logits, dlb)


def _merge_fwd(proj, ys, yh, t, d, gs_off):
    tm = _tile(t, 256, 8)
    w = _tile(d, 1024)
    nb = d // w

    def fn(gs, gh, a, b):
        return (_sigmoid(gs) * a + _sigmoid(gh) * b,)

    return _rowcall(fn, [proj, proj, ys, yh], [_rb(tm, w, gs_off // w), _rb(tm, w, gs_off // w + nb), _rb(tm, w), _rb(tm, w)],
                    [_sds((t, d), _BF)], [_rb(tm, w)], [False], grid=(nb, t // tm), name="merge_fwd")[0]


def _merge_bwd(proj, ys, yh, dm, t, d, gs_off):
    tm = _tile(t, 256, 8)
    w = _tile(d, 1024)
    nb = d // w

    def fn(gs, gh, a, b, g):
        g = g.astype(_F32)
        s1 = _sigmoid(gs)
        s2 = _sigmoid(gh)
        return g * s1, g * s2, g * a * s1 * (1.0 - s1), g * b * s2 * (1.0 - s2)

    return _rowcall(fn, [proj, proj, ys, yh, dm],
                    [_rb(tm, w, gs_off // w), _rb(tm, w, gs_off // w + nb), _rb(tm, w), _rb(tm, w), _rb(tm, w)],
                    [_sds((t, d), _BF)] * 4, [_rb(tm, w)] * 4, [False] * 4, grid=(nb, t // tm), name="merge_bwd")


_HALO = 16


def _prev_rows(up_prev, is_first):
    p1 = jnp.where(is_first, 0.0, up_prev[_HALO - 1:_HALO, :])
    p2 = jnp.where(is_first, 0.0, up_prev[_HALO - 2:_HALO - 1, :])
    return p1, p2


def _causal_taps(cur, p1, p2):
    row = lax.broadcasted_iota(jnp.int32, cur.shape, 0)
    s1 = jnp.where(row == 0, p1, _shift_down(cur, 1))
    s2 = jnp.where(row == 0, p2, jnp.where(row == 1, p1, _shift_down(cur, 2)))
    return s1, s2


def _pair_perm(nbh, per):
    def perm(n_):
        big = n_ // per
        return (2 * (big % nbh) + big // nbh) * per + n_ % per
    return perm


def _conv_specs(tm, w, nb_half, t):
    r8 = tm // _HALO
    cur_g = pl.BlockSpec((tm, w), lambda j, i: (i, 2 * j))
    cur_v = pl.BlockSpec((tm, w), lambda j, i: (i, 2 * j + 1))
    prev_g = pl.BlockSpec((_HALO, w), lambda j, i: (jnp.maximum(i * r8 - 1, 0), 2 * j))
    prev_v = pl.BlockSpec((_HALO, w), lambda j, i: (jnp.maximum(i * r8 - 1, 0), 2 * j + 1))
    w_g = pl.BlockSpec((3, w), lambda j, i: (0, j))
    w_v = pl.BlockSpec((3, w), lambda j, i: (0, nb_half + j))
    b_g = pl.BlockSpec((1, w), lambda j, i: (0, j))
    b_v = pl.BlockSpec((1, w), lambda j, i: (0, nb_half + j))
    return cur_g, cur_v, prev_g, prev_v, w_g, w_v, b_g, b_v


def _conv_of(cur, prev8, wt, bias, is_first):
    cur, prev8 = cur.astype(_F32), prev8.astype(_F32)
    p1, p2 = _prev_rows(prev8, is_first)
    s1, s2 = _causal_taps(cur, p1, p2)
    return bias + wt[0:1, :] * s2 + wt[1:2, :] * s1 + wt[2:3, :] * cur


def _convact_fwd(up, cw, cb, t, dff):
    tm = _tile(t, 512, _HALO)
    w = _tile(dff, 1408)
    nbh = dff // w
    sp = _conv_specs(tm, w, nbh, t)

    def body(ug, uv, pg, pv, wg, wv, bg, bv, o_ref):
        first = pl.program_id(1) == 0
        gate = _conv_of(ug[...], pg[...], wg[...], bg[...], first)
        val = _conv_of(uv[...], pv[...], wv[...], bv[...], first)
        o_ref[...] = (_silu(gate) * val).astype(_BF)

    return _call(body, name="convact_fwd", grid=(nbh, t // tm), in_specs=list(sp),
                 out_specs=pl.BlockSpec((tm, w), lambda j, i: (i, j)), out_shape=_sds((t, dff), _BF),
                 compiler_params=_cparams("parallel", "arbitrary"))(up, up, up, up, cw, cw, cb, cb)


def _convact_bwd(up, dact, cw, cb, t, dff):
    tm = _tile(t, 256, _HALO)
    w = _tile(dff, 1408)
    nbh = dff // w
    r8 = tm // _HALO
    nt = t // tm
    last8 = t // _HALO - 1

    def triple(off):
        return [pl.BlockSpec((tm, w), lambda j, i: (i, 2 * j + off)),
                pl.BlockSpec((_HALO, w), lambda j, i: (jnp.minimum((i + 1) * r8, last8), 2 * j + off)),
                pl.BlockSpec((_HALO, w), lambda j, i: (jnp.maximum(i * r8 - 1, 0), 2 * j + off))]

    def body(ug, ugn, ugp, uv, uvn, uvp, wg_ref, wv_ref, bg_ref, bv_ref, da_ref, dan_ref, du_ref, dw_ref, db_ref):
        i = pl.program_id(1)
        first = i == 0
        is_last = i == nt - 1
        gate = _conv_of(jnp.concatenate([ug[...], ugn[...]], axis=0), ugp[...], wg_ref[...], bg_ref[...], first)
        val = _conv_of(jnp.concatenate([uv[...], uvn[...]], axis=0), uvp[...], wv_ref[...], bv_ref[...], first)
        da = jnp.concatenate([da_ref[...], dan_ref[...]], axis=0).astype(_F32)
        row = lax.broadcasted_iota(jnp.int32, da.shape, 0)
        da = jnp.where(jnp.logical_and(row >= tm, is_last), 0.0, da)
        sg = _sigmoid(gate)
        halves = ((da * val * sg * (1.0 + gate * (1.0 - sg)), wg_ref, ug, ugp),
                  (da * gate * sg, wv_ref, uv, uvp))

        @pl.when(first)
        def _():
            dw_ref[...] = jnp.zeros_like(dw_ref)
            db_ref[...] = jnp.zeros_like(db_ref)

        for h, (dc, w_ref, u_ref, p_ref) in enumerate(halves):
            ln = slice(h * w, (h + 1) * w)
            wt = w_ref[...]
            du = wt[2:3, :] * dc + wt[1:2, :] * _shift_up(dc, 1) + wt[0:1, :] * _shift_up(dc, 2)
            du_ref[:, ln] = du[0:tm, :].astype(_BF)
            dcm = dc[0:tm, :]
            cur = u_ref[...].astype(_F32)
            p1, p2 = _prev_rows(p_ref[...].astype(_F32), first)
            s1, s2 = _causal_taps(cur, p1, p2)
            dw_ref[0:1, ln] += jnp.sum(dcm * s2, axis=0, keepdims=True)
            dw_ref[1:2, ln] += jnp.sum(dcm * s1, axis=0, keepdims=True)
            dw_ref[2:3, ln] += jnp.sum(dcm * cur, axis=0, keepdims=True)
            db_ref[:, ln] += jnp.sum(dcm, axis=0, keepdims=True)

    in_specs = (triple(0) + triple(1)
                + [pl.BlockSpec((3, w), lambda j, i: (0, j)), pl.BlockSpec((3, w), lambda j, i: (0, nbh + j)),
                   pl.BlockSpec((1, w), lambda j, i: (0, j)), pl.BlockSpec((1, w), lambda j, i: (0, nbh + j)),
                   pl.BlockSpec((tm, w), lambda j, i: (i, j)),
                   pl.BlockSpec((_HALO, w), lambda j, i: (jnp.minimum((i + 1) * r8, last8), j))])
    dup, dw_p, db_p = _call(
        body, name="convact_bwd", grid=(nbh, nt), in_specs=in_specs,
        out_specs=[pl.BlockSpec((tm, 2 * w), lambda j, i: (i, j)), pl.BlockSpec((3, 2 * w), lambda j, i: (0, j)),
                   pl.BlockSpec((1, 2 * w), lambda j, i: (0, j))],
        out_shape=[_sds((t, 2 * dff), _BF), _sds((3, 2 * dff), _F32), _sds((1, 2 * dff), _F32)],
        compiler_params=_cparams("parallel", "arbitrary"))(up, up, up, up, up, up, cw, cw, cb, cb, dact, dact)

    def natural(v):
        k = v.shape[0]
        return v.reshape(k, nbh, 2, w).transpose(0, 2, 1, 3).reshape(k, 2 * dff)

    return dup, natural(dw_p), natural(db_p)


def _me():
    return lax.axis_index("x"), lax.axis_index("y"), lax.axis_index("c")


def _other_chips(x, y):
    return [(1 - x, y), (x, 1 - y), (1 - x, 1 - y)]


def _rcopy(src, dst, ssem, rsem, dev):
    return pltpu.make_async_remote_copy(src_ref=src, dst_ref=dst, send_sem=ssem, recv_sem=rsem,
                                        device_id=dev, device_id_type=_MESH)


def _cast_into_slot(w, sel):
    r, c = w.shape
    tm = _tile(r, 256, 16)

    def body(sel_ref, w_ref, o_ref):
        o_ref[...] = w_ref[...].astype(_BF)

    gs = pltpu.PrefetchScalarGridSpec(
        num_scalar_prefetch=1, grid=(r // tm,),
        in_specs=[pl.BlockSpec((tm, c), lambda i, s: (i, 0))],
        out_specs=pl.BlockSpec((None, tm, c), lambda i, s: (s[0], i, 0)))
    return _call(body, name="cast_into_slot", grid_spec=gs, out_shape=_sds((_NCHIP, r, c), _BF),
                 compiler_params=_cparams("parallel"))(sel, w)


class _Plan:
    def __init__(self, ins, outs, aliases, sems, start, finish):
        self.ins, self.outs, self.aliases, self.sems, self.start, self.finish = ins, outs, aliases, sems, start, finish


def _run_plan(plan, name):
    ni, no = len(plan.ins), len(plan.outs)

    def body(*refs):
        rin, rout, sems = refs[:ni], refs[ni:ni + no], refs[ni + no:]
        plan.start(rin, rout, sems)
        plan.finish(rin, rout, sems)

    return _call(body, name=name, in_specs=[_ANY] * ni, out_specs=[_ANY] * no, out_shape=list(plan.outs),
                 input_output_aliases=dict(plan.aliases), scratch_shapes=list(plan.sems))(*plan.ins)


def _call_riding(body, rider, *, name, grid, in_specs, out_specs, out_shape, scratch_shapes, args):
    n_in, n_out, n_scr = len(in_specs), len(out_specs), len(scratch_shapes)
    n_rin, n_rout = len(rider.ins), len(rider.outs)

    def wrapped(*refs):
        ins, rin = refs[:n_in], refs[n_in:n_in + n_rin]
        o0 = n_in + n_rin
        outs, rout = refs[o0:o0 + n_out], refs[o0 + n_out:o0 + n_out + n_rout]
        s0 = o0 + n_out + n_rout
        scratch, sems = refs[s0:s0 + n_scr], refs[s0 + n_scr:]
        first = functools.reduce(jnp.logical_and, [pl.program_id(k) == 0 for k in range(len(grid))])
        last = functools.reduce(jnp.logical_and, [pl.program_id(k) == grid[k] - 1 for k in range(len(grid))])

        @pl.when(first)
        def _():
            rider.start(rin, rout, sems)

        body(*ins, *outs, *scratch)

        @pl.when(last)
        def _():
            rider.finish(rin, rout, sems)

    res = _call(wrapped, name=name, grid=grid, in_specs=list(in_specs) + [_ANY] * n_rin,
                out_specs=list(out_specs) + [_ANY] * n_rout, out_shape=list(out_shape) + list(rider.outs),
                input_output_aliases={n_in + k: n_out + v for k, v in rider.aliases.items()},
                scratch_shapes=list(scratch_shapes) + list(rider.sems),
                compiler_params=_cparams(*(["arbitrary"] * len(grid))))(*args, *rider.ins)
    return list(res[:n_out]), list(res[n_out:])


def _gather_plan(bufs, direct, rel=(0, 1, 2)):
    n, nd = len(bufs), len(direct)

    def where():
        x, y, c = _me()
        return c, 2 * x + y, _other_chips(x, y), (x, y, 1 - c)

    def picked(chips):
        return [(j, chips[j]) for j in rel]

    def piece(outs, a, chip, h):
        r2 = bufs[a].shape[1] // 2
        return outs[a].at[chip, pl.ds(h * r2, r2)]

    def send(outs, sems, a, j, me, c, chip):
        return _rcopy(piece(outs, a, me, c), piece(outs, a, me, c), sems[0].at[3 * a + j], sems[1].at[3 * a + j],
                      (chip[0], chip[1], c))

    def forward(outs, sems, a, j, pc, c, sib):
        return _rcopy(piece(outs, a, pc, c), piece(outs, a, pc, c), sems[2].at[3 * a + j], sems[3].at[3 * a + j], sib)

    def dsend(dins, douts, sems, a, j, me, c, chip):
        return _rcopy(dins[a], douts[a].at[me], sems[4].at[3 * a + j], sems[5].at[3 * a + j], (chip[0], chip[1], c))

    def start(rin, rout, sems):
        outs, dins, douts = rout[:n], rin[n:], rout[n:]
        c, me, chips, _ = where()
        for a in range(n):
            for j, chip in picked(chips):
                send(outs, sems, a, j, me, c, chip).start()
        for a in range(nd):
            pltpu.make_async_copy(dins[a], douts[a].at[me], sems[6].at[a]).start()
            for j, chip in enumerate(chips):
                dsend(dins, douts, sems, a, j, me, c, chip).start()

    def finish(rin, rout, sems):
        outs, dins, douts = rout[:n], rin[n:], rout[n:]
        c, me, chips, sib = where()
        for a in range(n):
            for j, (cx, cy) in picked(chips):
                pc = 2 * cx + cy
                _rcopy(piece(outs, a, me, c), piece(outs, a, pc, c), sems[0].at[3 * a + j], sems[1].at[3 * a + j],
                       (cx, cy, c)).wait_recv()
                forward(outs, sems, a, j, pc, c, sib).start()
        for a in range(n):
            for j, (cx, cy) in picked(chips):
                pc = 2 * cx + cy
                _rcopy(piece(outs, a, pc, 1 - c), piece(outs, a, pc, 1 - c), sems[2].at[3 * a + j],
                       sems[3].at[3 * a + j], sib).wait_recv()
        for a in range(nd):
            for j, (cx, cy) in enumerate(chips):
                _rcopy(dins[a], douts[a].at[2 * cx + cy], sems[4].at[3 * a + j], sems[5].at[3 * a + j],
                       (cx, cy, c)).wait_recv()
        for a in range(n):
            for j, (cx, cy) in picked(chips):
                send(outs, sems, a, j, me, c, (cx, cy)).wait_send()
                forward(outs, sems, a, j, 2 * cx + cy, c, sib).wait_send()
        for a in range(nd):
            pltpu.make_async_copy(dins[a], douts[a].at[me], sems[6].at[a]).wait()
            for j, chip in enumerate(chips):
                dsend(dins, douts, sems, a, j, me, c, chip).wait_send()

    dma = pltpu.SemaphoreType.DMA
    return _Plan(list(bufs) + list(direct),
                 [_sds(b.shape, b.dtype) for b in bufs] + [_sds((_NCHIP,) + s.shape, s.dtype) for s in direct],
                 {a: a for a in range(n)},
                 [dma((3 * max(n, 1),)), dma((3 * max(n, 1),)), dma((3 * max(n, 1),)), dma((3 * max(n, 1),)),
                  dma((3 * max(nd, 1),)), dma((3 * max(nd, 1),)), dma((max(nd, 1),))], start, finish)


def _swap_halves(grads, name):
    n = len(grads)

    def body(*refs):
        ins, outs = refs[:n], refs[n:2 * n]
        ssem, rsem = refs[2 * n:]
        x, y, c = _me()
        sib = (x, y, 1 - c)
        cps = []
        for a in range(n):
            cp = _rcopy(ins[a].at[1 - c], outs[a], ssem.at[a], rsem.at[a], sib)
            cp.start()
            cps.append(cp)
        for cp in cps:
            cp.wait_recv()
        for cp in cps:
            cp.wait_send()

    dma = pltpu.SemaphoreType.DMA
    return _call(body, name=name, in_specs=[_ANY] * n, out_specs=[_ANY] * n,
                 out_shape=[_sds(g.shape[1:], g.dtype) for g in grads],
                 scratch_shapes=[dma((n,)), dma((n,))])(*grads)


def _add_pairs(grads, theirs, sel):
    _, s, r, c2 = grads.shape
    a3 = grads.reshape(2, s * r, c2)
    b2 = theirs.reshape(s * r, c2)
    tm = _tile(s * r, 512, 16)

    def body(sel_ref, a_ref, b_ref, o_ref):
        o_ref[...] = (a_ref[...].astype(_F32) + b_ref[...].astype(_F32)).astype(_BF)

    gs = pltpu.PrefetchScalarGridSpec(
        num_scalar_prefetch=1, grid=(s * r // tm,),
        in_specs=[pl.BlockSpec((None, tm, c2), lambda i, q: (q[1], i, 0)), pl.BlockSpec((tm, c2), lambda i, q: (i, 0))],
        out_specs=pl.BlockSpec((tm, c2), lambda i, q: (i, 0)))
    out = _call(body, name="chip_sum", grid_spec=gs, out_shape=_sds((s * r, c2), _BF),
                compiler_params=_cparams("parallel"))(sel, a3, b2)
    return out.reshape(s, r, c2)


def _exchange_plan(sums, small):
    n = len(sums)
    has_small = small is not None

    def where():
        x, y, c = _me()
        peers = [(1 - x if k & 4 else x, 1 - y if k & 2 else y, 1 - c if k & 1 else c) for k in range(1, _NDEV)]
        return c, 4 * x + 2 * y + c, _other_chips(x, y), peers

    def send(rin, rout, sems, a, j, c, chip):
        return _rcopy(rin[a].at[2 * chip[0] + chip[1]], rout[a].at[j], sems[0].at[3 * a + j], sems[1].at[3 * a + j],
                      (chip[0], chip[1], c))

    def small_send(rin, rout, sems, k, dev, peer):
        return _rcopy(rin[n], rout[n].at[dev], sems[2].at[k], sems[3].at[k], peer)

    def start(rin, rout, sems):
        c, dev, chips, peers = where()
        for a in range(n):
            for j, chip in enumerate(chips):
                send(rin, rout, sems, a, j, c, chip).start()
        if has_small:
            pltpu.make_async_copy(rin[n], rout[n].at[dev], sems[4].at[0]).start()
            for k, peer in enumerate(peers):
                small_send(rin, rout, sems, k, dev, peer).start()

    def finish(rin, rout, sems):
        c, dev, chips, peers = where()
        for a in range(n):
            for j, chip in enumerate(chips):
                send(rin, rout, sems, a, j, c, chip).wait_recv()
        if has_small:
            for k, (px, py, pc_) in enumerate(peers):
                _rcopy(rin[n], rout[n].at[4 * px + 2 * py + pc_], sems[2].at[k], sems[3].at[k], (px, py, pc_)).wait_recv()
        for a in range(n):
            for j, chip in enumerate(chips):
                send(rin, rout, sems, a, j, c, chip).wait_send()
        if has_small:
            pltpu.make_async_copy(rin[n], rout[n].at[dev], sems[4].at[0]).wait()
            for k, peer in enumerate(peers):
                small_send(rin, rout, sems, k, dev, peer).wait_send()

    dma = pltpu.SemaphoreType.DMA
    outs = [_sds((3,) + s.shape[1:], s.dtype) for s in sums]
    if has_small:
        outs.append(_sds((_NDEV,) + small.shape, small.dtype))
    return _Plan(list(sums) + ([small] if has_small else []), outs, {},
                 [dma((3 * max(n, 1),)), dma((3 * max(n, 1),)), dma((_NDEV - 1,)), dma((_NDEV - 1,)), dma((1,))],
                 start, finish)


def _shard_sum(sums, recv, sel):
    s, r, c2 = sums.shape
    tm = _tile(r, 256, 16)

    def body(sel_ref, own_ref, rc_ref, o_ref):
        rc = rc_ref[...]
        o_ref[...] = ((own_ref[...].astype(_F32) + rc[0].astype(_F32)) + rc[1].astype(_F32)) + rc[2].astype(_F32)

    gs = pltpu.PrefetchScalarGridSpec(
        num_scalar_prefetch=1, grid=(r // tm,),
        in_specs=[pl.BlockSpec((None, tm, c2), lambda i, q: (q[0], i, 0)),
                  pl.BlockSpec((3, tm, c2), lambda i, q: (0, i, 0))],
        out_specs=pl.BlockSpec((None, tm, c2), lambda i, q: (q[1], i, 0)))
    return _call(body, name="shard_sum", grid_spec=gs, out_shape=_sds((2, r, c2), _F32),
                 compiler_params=_cparams("parallel"))(sel, sums, recv)


def _sum_slots(stack, name):
    k, r, c = stack.shape
    tm = _tile(r, 256, 16 if stack.dtype == _BF else 8)

    def fn(v):
        out = v[0].astype(_F32)
        for i in range(1, k):
            out = out + v[i].astype(_F32)
        return (out,)

    return _rowcall(fn, [stack], [pl.BlockSpec((k, tm, c), lambda j, i: (0, i, 0))], [_sds((r, c), _F32)],
                    [_rb(tm, c)], [False], grid=(1, r // tm), name=name)[0]


def _share_halves(bufs):
    n = len(bufs)

    def body(*refs):
        outs = refs[n:2 * n]
        ssem, rsem = refs[2 * n:]
        x, y, c = _me()
        sib = (x, y, 1 - c)
        cps = []
        for a in range(n):
            cp = _rcopy(outs[a].at[c], outs[a].at[c], ssem.at[a], rsem.at[a], sib)
            cp.start()
            cps.append(cp)
        for a in range(n):
            _rcopy(outs[a].at[c], outs[a].at[1 - c], ssem.at[a], rsem.at[a], sib).wait_recv()
        for cp in cps:
            cp.wait_send()

    dma = pltpu.SemaphoreType.DMA
    return _call(body, name="share_halves", in_specs=[_ANY] * n, out_specs=[_ANY] * n,
                 out_shape=[_sds(b.shape, b.dtype) for b in bufs], input_output_aliases={a: a for a in range(n)},
                 scratch_shapes=[dma((n,)), dma((n,))])(*bufs)


def _adamw_math(w, g, m, v):
    m = _B1 * m + (1.0 - _B1) * g
    v = _B2 * v + (1.0 - _B2) * jnp.square(g)
    m_hat = m / (1.0 - _B1 ** _STEP)
    v_hat = v / (1.0 - _B2 ** _STEP)
    delta = -_LR * (m_hat / (jnp.sqrt(v_hat) + _ADAM_EPS) + _WD * w)
    return delta, m, v


def _adamw_shard(w, g2, m, v, name):
    r, c = w.shape
    c2 = c // 2
    tm = _tile(r, 256, 8)
    blk = pl.BlockSpec((tm, c2), lambda h, i: (i, h))

    def fn(wb, gb, mb, vb):
        return (gb,) + _adamw_math(wb, gb, mb, vb)

    return _rowcall(fn, [w, g2, m, v], [blk, pl.BlockSpec((None, tm, c2), lambda h, i: (h, i, 0)), blk, blk],
                    [_sds((r, c), _F32)] * 4, [blk] * 4, [False] * 4, grid=(2, r // tm), name=name)


def _adamw_whole(w, g, m, v, name):
    r, c = w.shape
    blk = _full((r, c))
    return _rowcall(lambda *a: _adamw_math(*a), [w, g, m, v], [blk] * 4, [_sds((r, c), _F32)] * 3, [blk] * 3,
                    [False] * 3, grid=(1, 1), name=name)


def _adamw_many(ws, g_pack, ms, vs):
    k = len(ws)
    views, offs, off = [], [], 0
    for w in ws:
        n = w.size
        views.append((n // _LANE, _LANE) if n % _LANE == 0 else (1, n))
        offs.append(off)
        off += (n + (-n) % (8 * _LANE)) // _LANE

    def body(*refs):
        g_ref, w_refs, m_refs, v_refs = refs[0], refs[1:1 + k], refs[1 + k:1 + 2 * k], refs[1 + 2 * k:1 + 3 * k]
        outs = refs[1 + 3 * k:]
        for i in range(k):
            r, c = views[i]
            g = g_ref[offs[i]:offs[i] + r, 0:c]
            res = (g,) + _adamw_math(w_refs[i][...], g, m_refs[i][...], v_refs[i][...])
            for o_ref, val in zip(outs[4 * i:4 * i + 4], res):
                o_ref[...] = val

    args = [g_pack] + [a.reshape(views[i]) for grp in (ws, ms, vs) for i, a in enumerate(grp)]
    res = _call(body, name="adamw_small", out_shape=[_sds(views[i], _F32) for i in range(k) for _ in range(4)])(*args)
    return [[res[4 * i + j].reshape(ws[i].shape) for j in range(4)] for i in range(k)]


def _pack(arrs):
    parts = []
    for a in arrs:
        f = a.reshape(-1).astype(_F32)
        pad = (-f.shape[0]) % (8 * _LANE)
        if pad:
            f = jnp.concatenate([f, jnp.zeros((pad,), _F32)])
        parts.append(f)
    return jnp.concatenate(parts).reshape(-1, _LANE)


def kernel(x, ln_mix_g, w_in, s5_a_re, s5_a_im, s5_log_dt, s5_b_re, s5_b_im, s5_c_re, s5_c_im, s5_d, s5_w_glu, s5_b_glu, w_proj_s5, hgrn_lb_logits, hgrn_norm_g, w_proj_hgrn, w_out, ln_ffn_g, w_up, conv_w, conv_b, w_down, ln_final_g, loss_target, m_ln_mix_g, m_w_in, m_s5_a_re, m_s5_a_im, m_s5_log_dt, m_s5_b_re, m_s5_b_im, m_s5_c_re, m_s5_c_im, m_s5_d, m_s5_w_glu, m_s5_b_glu, m_w_proj_s5, m_hgrn_lb_logits, m_hgrn_norm_g, m_w_proj_hgrn, m_w_out, m_ln_ffn_g, m_w_up, m_conv_w, m_conv_b, m_w_down, m_ln_final_g, v_ln_mix_g, v_w_in, v_s5_a_re, v_s5_a_im, v_s5_log_dt, v_s5_b_re, v_s5_b_im, v_s5_c_re, v_s5_c_im, v_s5_d, v_s5_w_glu, v_s5_b_glu, v_w_proj_s5, v_hgrn_lb_logits, v_hgrn_norm_g, v_w_proj_hgrn, v_w_out, v_ln_ffn_g, v_w_up, v_conv_w, v_conv_b, v_w_down, v_ln_final_g):
    assert x.shape[0] == 1 and w_in.shape[0] == 1, "one example per device, one layer"
    t, d = x.shape[1], x.shape[2]
    w5 = s5_w_glu.shape[2]
    hw = hgrn_norm_g.shape[1]
    ng_, np_, gc = s5_b_re.shape[1], s5_b_re.shape[2], s5_b_re.shape[3]
    dff = w_down.shape[1] * _NCHIP
    assert gc * _S5_SET == _LANE and ng_ * gc == w5 and hw % _LANE == 0
    gs_off = w5 + 4 * hw
    ci = lax.axis_index("c")
    xt = x.reshape(t, d)
    tgt = loss_target.reshape(t, d)

    big_names = ["w_in", "s5_w_glu", "w_proj_s5", "w_proj_hgrn", "w_out", "w_up", "w_down"]
    big_w = dict(w_in=w_in[0], s5_w_glu=s5_w_glu[0], w_proj_s5=w_proj_s5[0], w_proj_hgrn=w_proj_hgrn[0],
                 w_out=w_out[0], w_up=w_up[0], w_down=w_down[0])
    chip = 2 * lax.axis_index("x") + lax.axis_index("y")
    sel_chip = jnp.stack([chip, ci]).astype(jnp.int32)
    slots = {k: _cast_into_slot(big_w[k], sel_chip) for k in big_names}
    g_in, g_cw = _run_plan(_gather_plan([slots["w_in"]], [conv_w[0]]), "gather_w_in")
    cw = g_cw.transpose(1, 0, 2).reshape(3, 2 * dff)
    cb = conv_b

    tm_big = _tile(t, 1024, 8)

    h1 = _rms_fwd(xt, ln_mix_g, "rms1_fwd")
    nin_s = g_in.shape[2]
    proj, (g_glu, g_ps5, g_ph, g_out) = _mm(
        h1, g_in, "nn", _F32, tm=tm_big, tn=_tile(nin_s, 1152), tk=d, name="mm_proj",
        rider=_gather_plan([slots[k] for k in ("s5_w_glu", "w_proj_s5", "w_proj_hgrn", "w_out")], []))
    wglu = g_glu.reshape(w5, w5)
    wout = g_out.reshape(d, d)

    abar_r, abar_i, coef_r, coef_i = _s5_prep(s5_a_re[0], s5_a_im[0], s5_log_dt.reshape(ng_, 1))
    lanes = ng_ * np_

    def fold_coef(cr_, ci_, bre, bim):
        return cr_[..., None] * bre - ci_[..., None] * bim, cr_[..., None] * bim + ci_[..., None] * bre

    (bf_re, bf_im), fold_vjp = jax.vjp(fold_coef, coef_r, coef_i, s5_b_re[0], s5_b_im[0])
    par = (abar_r.reshape(1, lanes), abar_i.reshape(1, lanes),
           _bd_in(bf_re), _bd_in(bf_im), _bd_out(s5_c_re[0]), -_bd_out(s5_c_im[0]), s5_d.reshape(1, w5))
    (y_s5, car, cai), (g_up_part,) = _s5_fwd(proj, par, t, w5, rider=_gather_plan([slots["w_up"]], [], rel=(0, 1)))
    z = _s5glu_fwd(y_s5, wglu, s5_b_glu)
    ys = _mm(z, g_ps5, "nn", _F32, tm=tm_big, tn=g_ps5.shape[2], tk=w5, name="mm_proj_s5")

    lb = _lb_prep(hgrn_lb_logits)
    (og, st_all), (g_up,) = _hgrn_fwd(proj, lb, hgrn_norm_g, t, w5, hw, rider=_gather_plan([g_up_part], [], rel=(2,)))
    yh = _mm(og, g_ph, "nn", _F32, tm=tm_big, tn=g_ph.shape[2], tk=hw, name="mm_proj_hgrn")

    merged = _merge_fwd(proj, ys, yh, t, d, gs_off)
    x2 = _mm(merged, wout, "nn", _F32, tm=tm_big, tn=_tile(d, 1024), tk=d, res=xt, name="mm_out")
    h2 = _rms_fwd(x2, ln_ffn_g, "rms2_fwd")
    up_s = g_up.shape[2]
    w_conv = _tile(dff, 1408)
    nbh = dff // w_conv
    tn_up = _tile(up_s, 1408)
    tk_dh2 = _tile(up_s, w_conv)
    tn_gwup = _tile(up_s // 2, 1408)
    assert w_conv % tn_up == 0 and w_conv % tk_dh2 == 0 and w_conv % tn_gwup == 0
    up, (g_down,) = _mm(h2, g_up, "nn", _BF, tm=tm_big, tn=tn_up, tk=d, name="mm_up",
                        colperm=_pair_perm(nbh, w_conv // tn_up), rider=_gather_plan([slots["w_down"]], []))
    wdown = g_down.reshape(dff, d)
    act = _convact_fwd(up, cw, cb, t, dff)
    x3 = _mm(act, wdown, "nn", _F32, tm=tm_big, tn=_tile(d, 1024), tk=None, res=x2, name="mm_down")
    loss_part, dx3, dx3b, d_gfin = _loss_head(x3, tgt, ln_final_g.reshape(1, d))

    dact = _mm(dx3b, wdown, "nt", _BF, tm=tm_big, tn=_tile(dff, 1408), tk=d, name="mm_dact")
    r_down = dff // _NCHIP
    gw_down = _mm(act, dx3b, "tn", _BF, tm=_tile(r_down, 1408), tn=_tile(d // 2, 1024), tk=None, halves="rows",
                  name="mm_gw_down")
    def chip_sums(grads, name):
        theirs = _swap_halves(grads, name)
        return [_add_pairs(g, th, sel_chip) for g, th in zip(grads, theirs)]

    (s_down,) = chip_sums([gw_down], "swap_halves_d")
    dup, d_cw, d_cb = _convact_bwd(up, dact, cw, cb, t, dff)
    dh2, (r_down,) = _mm(dup, g_up, "nt", _BF, tm=tm_big, tn=_tile(d, 1024), tk=tk_dh2, name="mm_dh2",
                         colperm=_pair_perm(nbh, w_conv // tk_dh2), rider=_exchange_plan([s_down], None))
    gw_up = _mm(h2, dup, "tn", _BF, tm=_tile(d, 1024), tn=tn_gwup, tk=None, halves="cols", name="mm_gw_up",
                colperm=_pair_perm(nbh, w_conv // tn_gwup))
    dx2, dx2b, d_gffn = _rms_bwd(x2, ln_ffn_g, dh2, dx3, "rms2_bwd")
    dmerged = _mm(dx2b, wout, "nt", _BF, tm=tm_big, tn=_tile(d, 1024), tk=d, name="mm_dmerged")
    gw_out = _mm(merged, dx2b, "tn", _BF, tm=_tile(d // _NCHIP, 1024), tn=_tile(d // 2, 1024), tk=None, halves="rows",
                 name="mm_gw_out")
    dys, dyh, dgs, dgh = _merge_bwd(proj, ys, yh, dmerged, t, d, gs_off)
    ps_s = g_ps5.shape[2]
    dz = _mm(dys, g_ps5, "nt", _BF, tm=tm_big, tn=_tile(w5, 1024), tk=ps_s, name="mm_dz")
    gw_ps5 = _mm(z, dys, "tn", _BF, tm=_tile(w5, 1024), tn=ps_s // 2, tk=None, halves="cols", name="mm_gw_ps5")
    dog = _mm(dyh, g_ph, "nt", _BF, tm=tm_big, tn=_tile(hw, 1024), tk=ps_s, name="mm_dog")
    gw_ph = _mm(og, dyh, "tn", _BF, tm=_tile(hw, 1024), tn=ps_s // 2, tk=None, halves="cols", name="mm_gw_ph")
    dy_s5, gw_glu_full, d_bglu = _s5glu_bwd(y_s5, dz, wglu, s5_b_glu)
    r_glu = w5 // _NCHIP
    gw_glu = gw_glu_full.astype(_BF).reshape(_NCHIP, r_glu, 2, w5 // 2).transpose(2, 0, 1, 3)

    s_glu, s_ps5, s_ph, s_out, s_up = chip_sums([gw_glu, gw_ps5, gw_ph, gw_out, gw_up], "swap_halves_a")
    s5g, (r_glu_, r_ps5, r_ph, r_out, r_up) = _s5_bwd(
        proj, dy_s5, car, cai, par, t, w5, rider=_exchange_plan([s_glu, s_ps5, s_ph, s_out, s_up], None))
    du = s5g[0]
    dq, df, di, dg, d_lb, d_ng = _hgrn_bwd(proj, lb, hgrn_norm_g, st_all, dog, t, w5, hw)
    dproj = jnp.concatenate([du, dq, df, di, dg, dgs, dgh], axis=1)

    d_coef_r, d_coef_i, d_bre, d_bim = fold_vjp((_bd_in_grad(s5g[3], np_, gc), _bd_in_grad(s5g[4], np_, gc)))
    d_are, d_aim, d_ldt = _s5_prep_bwd(s5_a_re[0], s5_a_im[0], s5_log_dt.reshape(ng_, 1),
                                       [s5g[1].reshape(ng_, np_), s5g[2].reshape(ng_, np_), d_coef_r, d_coef_i])
    d_cre = _bd_out_grad(s5g[5], np_, gc)
    d_cim = -_bd_out_grad(s5g[6], np_, gc)
    d_logits = _lb_prep_bwd(hgrn_lb_logits, d_lb)

    small_names = ["s5_a_re", "s5_a_im", "s5_log_dt", "s5_b_re", "s5_b_im", "s5_c_re", "s5_c_im", "s5_d",
                   "s5_b_glu", "hgrn_lb_logits", "hgrn_norm_g", "ln_ffn_g", "conv_b", "ln_final_g", "ln_mix_g"]
    small_w = dict(ln_mix_g=ln_mix_g, s5_a_re=s5_a_re, s5_a_im=s5_a_im, s5_log_dt=s5_log_dt, s5_b_re=s5_b_re,
                   s5_b_im=s5_b_im, s5_c_re=s5_c_re, s5_c_im=s5_c_im, s5_d=s5_d, s5_b_glu=s5_b_glu,
                   hgrn_lb_logits=hgrn_lb_logits, hgrn_norm_g=hgrn_norm_g, ln_ffn_g=ln_ffn_g, conv_b=conv_b,
                   ln_final_g=ln_final_g)
    small_m = dict(ln_mix_g=m_ln_mix_g, s5_a_re=m_s5_a_re, s5_a_im=m_s5_a_im, s5_log_dt=m_s5_log_dt, s5_b_re=m_s5_b_re,
                   s5_b_im=m_s5_b_im, s5_c_re=m_s5_c_re, s5_c_im=m_s5_c_im, s5_d=m_s5_d, s5_b_glu=m_s5_b_glu,
                   hgrn_lb_logits=m_hgrn_lb_logits, hgrn_norm_g=m_hgrn_norm_g, ln_ffn_g=m_ln_ffn_g, conv_b=m_conv_b,
                   ln_final_g=m_ln_final_g)
    small_v = dict(ln_mix_g=v_ln_mix_g, s5_a_re=v_s5_a_re, s5_a_im=v_s5_a_im, s5_log_dt=v_s5_log_dt, s5_b_re=v_s5_b_re,
                   s5_b_im=v_s5_b_im, s5_c_re=v_s5_c_re, s5_c_im=v_s5_c_im, s5_d=v_s5_d, s5_b_glu=v_s5_b_glu,
                   hgrn_lb_logits=v_hgrn_lb_logits, hgrn_norm_g=v_hgrn_norm_g, ln_ffn_g=v_ln_ffn_g, conv_b=v_conv_b,
                   ln_final_g=v_ln_final_g)
    small_g = dict(s5_a_re=d_are, s5_a_im=d_aim, s5_log_dt=d_ldt, s5_b_re=d_bre, s5_b_im=d_bim,
                   s5_c_re=d_cre, s5_c_im=d_cim, s5_d=s5g[7], s5_b_glu=d_bglu, hgrn_lb_logits=d_logits,
                   hgrn_norm_g=d_ng, ln_ffn_g=d_gffn, conv_b=d_cb, ln_final_g=d_gfin)
    like = [small_w[k] for k in small_names]
    assert small_names[-1] == "ln_mix_g"
    pack_a = _pack([small_g[k] for k in small_names[:-1]] + [d_cw])
    gw_in, (r_small_a,) = _mm(h1, dproj, "tn", _BF, tm=_tile(d, 1024), tn=_tile(nin_s // 2, 1152), tk=None, halves="cols",
                              name="mm_gw_in", rider=_exchange_plan([], pack_a))
    (s_in,) = chip_sums([gw_in], "swap_halves_b")
    dh1, (r_in,) = _mm(dproj, g_in, "nt", _BF, tm=tm_big, tn=_tile(d, 1024), tk=None, name="mm_dh1",
                       rider=_exchange_plan([s_in], None))
    dx, _, d_gmix = _rms_bwd(xt, ln_mix_g, dh1, dx2, "rms1_bwd")
    (r_small_b,) = _run_plan(_exchange_plan([], _pack([d_gmix])), "exchange_gmix")
    sums = [s_in, s_glu, s_ps5, s_ph, s_out, s_up, s_down]
    received = [r_in, r_glu_, r_ps5, r_ph, r_out, r_up, r_down]
    halves = [_shard_sum(sm, rc, sel_chip) for sm, rc in zip(sums, received)]
    g_a = _sum_slots(r_small_a, "small_sum_a")
    g_b = _sum_slots(r_small_b, "small_sum_b")
    full = _share_halves(halves)
    w_pack = _pack(like)
    rows_a = w_pack.shape[0] - g_b.shape[0]
    g_small = jnp.concatenate([g_a[:rows_a], g_b], axis=0)
    cs = conv_w.shape[2]
    g_cw_full = g_a[rows_a:].reshape(-1)[:3 * 2 * dff].reshape(3, 2 * dff)
    g_cw = lax.dynamic_slice_in_dim(g_cw_full, chip * cs, cs, axis=1)

    big_m = dict(w_in=m_w_in, s5_w_glu=m_s5_w_glu, w_proj_s5=m_w_proj_s5, w_proj_hgrn=m_w_proj_hgrn, w_out=m_w_out,
                 w_up=m_w_up, w_down=m_w_down)
    big_v = dict(w_in=v_w_in, s5_w_glu=v_s5_w_glu, w_proj_s5=v_w_proj_s5, w_proj_hgrn=v_w_proj_hgrn, w_out=v_w_out,
                 w_up=v_w_up, w_down=v_w_down)
    res = {}
    for k, g2 in zip(big_names, full):
        w2 = big_w[k]
        shp = (1,) + w2.shape
        outs = _adamw_shard(w2, g2, big_m[k][0], big_v[k][0], "adamw_" + k)
        res[k] = [o.reshape(shp) for o in outs]
    sm_outs = _adamw_many(like, g_small, [small_m[k] for k in small_names], [small_v[k] for k in small_names])
    for k, outs in zip(small_names, sm_outs):
        res[k] = outs

    cw_outs = _adamw_whole(conv_w[0], g_cw, m_conv_w[0], v_conv_w[0], "adamw_conv_w")
    res["conv_w"] = [g_cw.reshape(conv_w.shape)] + [o.reshape(conv_w.shape) for o in cw_outs]

    loss = lax.psum(loss_part[0, 0], ("x", "y", "c"))
    order = ["ln_mix_g", "w_in", "s5_a_re", "s5_a_im", "s5_log_dt", "s5_b_re", "s5_b_im", "s5_c_re", "s5_c_im", "s5_d",
             "s5_w_glu", "s5_b_glu", "w_proj_s5", "hgrn_lb_logits", "hgrn_norm_g", "w_proj_hgrn", "w_out", "ln_ffn_g",
             "w_up", "conv_w", "conv_b", "w_down", "ln_final_g"]
    return (loss, dx.reshape(x.shape), *[res[k][0] for k in order], *[res[k][1] for k in order],
            *[res[k][2] for k in order], *[res[k][3] for k in order])
```

```python
import functools

import jax
import jax.numpy as jnp
from jax import lax
from jax.experimental import pallas as pl
from jax.experimental.pallas import tpu as pltpu

_F32 = jnp.float32
_BF = jnp.bfloat16
_RMS_EPS = 1e-6
_S5_MAX_RE = -1e-4
_LR, _B1, _B2, _ADAM_EPS, _WD, _STEP = 0.001, 0.9, 0.999, 1e-08, 0.01, 10
_MESH = pl.DeviceIdType.MESH
_ANY = pl.BlockSpec(memory_space=pl.ANY)
_LANE = 128
_VMEM_LIMIT = 56 * 1024 * 1024
_CHUNK = 64
_S5_TB = 512
_S5_SET = 8
_HGRN_HP = 8
_HGRN_SEG = 512
_NCHIP = 4
_NDEV = 8


def _call(body, **kw):
    return pl.pallas_call(body, **kw)


def _cparams(*sem):
    return pltpu.CompilerParams(dimension_semantics=sem, vmem_limit_bytes=_VMEM_LIMIT)


def _tile(n, pref, unit=_LANE):
    if n <= pref:
        return n
    t = (pref // unit) * unit
    while t >= unit:
        if n % t == 0:
            return t
        t -= unit
    raise ValueError(f"no tile for {n}")


_OPERAND_BYTES = 12 * 1024 * 1024


def _tk_fit(k, tm, tn):
    best = _LANE
    for tk in range(_LANE, k + 1, _LANE):
        if k % tk == 0 and (tm + tn) * tk * 2 <= _OPERAND_BYTES:
            best = tk
    return best if k % _LANE == 0 else k


_NN = ((1,), (0,))
_NT = ((1,), (1,))
_TN = ((0,), (0,))


def _dg(a, b, dims):
    return lax.dot_general(a.astype(_BF), b.astype(_BF), (dims, ((), ())), preferred_element_type=_F32)


@jax.custom_vjp
def _bdot(a, b):
    return _dg(a, b, _NN)


def _bdot_f(a, b):
    return _dg(a, b, _NN), (a, b)


def _bdot_b(res, g):
    a, b = res
    return _dg(g, b, _NT).astype(a.dtype), _dg(a, g, _TN).astype(b.dtype)


_bdot.defvjp(_bdot_f, _bdot_b)


@jax.custom_vjp
def _bdot_nt(a, b):
    return _dg(a, b, _NT)


def _bdot_nt_f(a, b):
    return _dg(a, b, _NT), (a, b)


def _bdot_nt_b(res, g):
    a, b = res
    return _dg(g, b, _NN).astype(a.dtype), _dg(g, a, _TN).astype(b.dtype)


_bdot_nt.defvjp(_bdot_nt_f, _bdot_nt_b)


@jax.custom_vjp
def _bdot_tn(a, b):
    return _dg(a, b, _TN)


def _bdot_tn_f(a, b):
    return _dg(a, b, _TN), (a, b)


def _bdot_tn_b(res, g):
    a, b = res
    return _dg(b, g, _NT).astype(a.dtype), _dg(a, g, _NN).astype(b.dtype)


_bdot_tn.defvjp(_bdot_tn_f, _bdot_tn_b)


_SUBLANES = 8


def _shift_up(x, n):
    r = x.shape[0]
    if n % _SUBLANES == 0:
        return jnp.concatenate([x[n:], jnp.zeros((n,) + x.shape[1:], x.dtype)], axis=0)
    row = lax.broadcasted_iota(jnp.int32, x.shape, 0)
    return jnp.where(row < r - n, pltpu.roll(x, r - n, 0), 0.0)


@functools.partial(jax.custom_vjp, nondiff_argnums=(1,))
def _shift_down(x, n):
    if n % _SUBLANES == 0:
        return jnp.concatenate([jnp.zeros((n,) + x.shape[1:], x.dtype), x[:x.shape[0] - n]], axis=0)
    row = lax.broadcasted_iota(jnp.int32, x.shape, 0)
    return jnp.where(row >= n, pltpu.roll(x, n, 0), 0.0)


def _shift_down_f(x, n):
    return _shift_down(x, n), None


def _shift_down_b(n, _, g):
    return (_shift_up(g, n),)


_shift_down.defvjp(_shift_down_f, _shift_down_b)


def _rows_apart(x):
    return tuple(x[k:k + _SUBLANES] for k in range(0, x.shape[0], _SUBLANES))


@jax.custom_vjp
def _split_rows(x):
    return _rows_apart(x)


_split_rows.defvjp(lambda x: (_rows_apart(x), None), lambda _, gs: (jnp.concatenate(gs, axis=0),))


@jax.custom_vjp
def _join_rows(pieces):
    return jnp.concatenate(pieces, axis=0)


_join_rows.defvjp(lambda pieces: (jnp.concatenate(pieces, axis=0), None), lambda _, g: (_rows_apart(g),))


@jax.custom_vjp
def _last_row(x):
    return x[_SUBLANES - 1:_SUBLANES]


def _last_row_b(_, g):
    row = lax.broadcasted_iota(jnp.int32, (_SUBLANES, g.shape[1]), 0)
    return (jnp.where(row == _SUBLANES - 1, g, 0.0),)


_last_row.defvjp(lambda x: (x[_SUBLANES - 1:_SUBLANES], None), _last_row_b)


def _sigmoid(x):
    return 1.0 / (1.0 + jnp.exp(-x))


def _silu(x):
    return x * _sigmoid(x)


def _gelu(x):
    return 0.5 * x * (1.0 + jnp.tanh(0.7978845608028654 * (x + 0.044715 * (x * x * x))))


def _rms_core(x, g):
    return x * lax.rsqrt(jnp.mean(x * x, axis=-1, keepdims=True) + _RMS_EPS) * g


def _mm(a, b, mode, out_dtype, *, tm, tn, tk, res=None, halves=None, rider=None, colperm=None, name):
    if colperm is None:
        def colperm(n_):
            return n_
    if tk is None:
        kdim = a.shape[0] if mode == "tn" else (b.shape[2] if (mode == "nt" and b.ndim == 3) else a.shape[1])
        tk = _tk_fit(kdim, tm, tn)
    if mode == "nn":
        m, k = a.shape
        a_spec = pl.BlockSpec((tm, tk), lambda i, j, kk: (i, kk))
        if b.ndim == 3:
            s, _, ns = b.shape
            n = s * ns
            npb = ns // tn
            b_spec = pl.BlockSpec((None, tk, tn), lambda i, j, kk: (j // npb, kk, j % npb))
        else:
            n = b.shape[1]
            b_spec = pl.BlockSpec((tk, tn), lambda i, j, kk: (kk, j))
        dims = _NN
    elif mode == "nt":
        m, k = a.shape
        a_spec = pl.BlockSpec((tm, tk), lambda i, j, kk: (i, colperm(kk)))
        if b.ndim == 3:
            s, n, ks = b.shape
            kpb = ks // tk
            b_spec = pl.BlockSpec((None, tn, tk), lambda i, j, kk: (kk // kpb, j, kk % kpb))
        else:
            n = b.shape[0]
            b_spec = pl.BlockSpec((tn, tk), lambda i, j, kk: (j, kk))
        dims = _NT
    else:
        k, m = a.shape
        n = b.shape[1]
        a_spec = pl.BlockSpec((tk, tm), lambda i, j, kk: (kk, i))
        b_spec = pl.BlockSpec((tk, tn), lambda i, j, kk: (kk, colperm(j)))
        dims = _TN
    nk = k // tk
    if halves is None:
        out_shape = jax.ShapeDtypeStruct((m, n), out_dtype)
        out_spec = pl.BlockSpec((tm, tn), lambda i, j, kk: (i, colperm(j) if mode == "nn" else j))
    elif halves == "cols":
        c2 = n // (2 * _NCHIP)
        tpc = c2 // tn
        out_shape = jax.ShapeDtypeStruct((2, _NCHIP, m, c2), out_dtype)
        out_spec = pl.BlockSpec((None, None, tm, tn),
                                lambda i, j, kk: ((j // tpc) % 2, j // (2 * tpc), i, j % tpc))
    else:
        c2 = n // 2
        tpc = c2 // tn
        r = m // _NCHIP
        tpr = r // tm
        out_shape = jax.ShapeDtypeStruct((2, _NCHIP, r, c2), out_dtype)
        out_spec = pl.BlockSpec((None, None, tm, tn),
                                lambda i, j, kk: (j // tpc, i // tpr, i % tpr, j % tpc))
    has_res = res is not None
    nreg = 3 if has_res else 2
    ni, nj = m // tm, n // tn
    r_ins = list(rider.ins) if rider else []
    r_outs = list(rider.outs) if rider else []

    def body(*refs):
        a_ref, b_ref = refs[0], refs[1]
        r_ref = refs[2] if has_res else None
        rin = refs[nreg:nreg + len(r_ins)]
        o_ref = refs[nreg + len(r_ins)]
        rout = refs[nreg + len(r_ins) + 1:nreg + len(r_ins) + 1 + len(r_outs)]
        acc_ref = refs[nreg + len(r_ins) + 1 + len(r_outs)]
        sems = refs[nreg + len(r_ins) + 2 + len(r_outs):]
        i, j, kk = pl.program_id(0), pl.program_id(1), pl.program_id(2)

        if rider:
            @pl.when(jnp.logical_and(jnp.logical_and(i == 0, j == 0), kk == 0))
            def _():
                rider.start(rin, rout, sems)

        @pl.when(kk == 0)
        def _():
            acc_ref[...] = jnp.zeros_like(acc_ref)

        acc_ref[...] += _dg(a_ref[...], b_ref[...], dims)

        @pl.when(kk == nk - 1)
        def _():
            out = acc_ref[...]
            if has_res:
                out = out + r_ref[...]
            o_ref[...] = out.astype(out_dtype)

        if rider:
            @pl.when(jnp.logical_and(jnp.logical_and(i == ni - 1, j == nj - 1), kk == nk - 1))
            def _():
                rider.finish(rin, rout, sems)

    in_specs = [a_spec, b_spec]
    args = [a, b]
    if has_res:
        in_specs.append(pl.BlockSpec((tm, tn), lambda i, j, kk: (i, j)))
        args.append(res)
    if not rider:
        return _call(body, name=name, grid=(ni, nj, nk), in_specs=in_specs, out_specs=out_spec,
                     out_shape=out_shape, scratch_shapes=[pltpu.VMEM((tm, tn), _F32)],
                     compiler_params=_cparams("parallel", "parallel", "arbitrary"))(*args)
    res_all = _call(body, name=name, grid=(ni, nj, nk), in_specs=in_specs + [_ANY] * len(r_ins),
                    out_specs=[out_spec] + [_ANY] * len(r_outs), out_shape=[out_shape] + r_outs,
                    input_output_aliases={nreg + k: 1 + v for k, v in rider.aliases.items()},
                    scratch_shapes=[pltpu.VMEM((tm, tn), _F32)] + list(rider.sems),
                    compiler_params=_cparams("arbitrary", "arbitrary", "arbitrary"))(*args, *r_ins)
    return res_all[0], list(res_all[1:])


def _rowcall(fn, ins, in_specs, outs, out_specs, acc, *, grid, name):
    nin = len(ins)

    def body(*refs):
        vals = fn(*[r[...] for r in refs[:nin]])
        first = pl.program_id(1) == 0
        for k, (o_ref, v) in enumerate(zip(refs[nin:], vals)):
            if acc[k]:
                @pl.when(first)
                def _(o_ref=o_ref):
                    o_ref[...] = jnp.zeros_like(o_ref)
                o_ref[...] += v.astype(o_ref.dtype)
            else:
                o_ref[...] = v.astype(o_ref.dtype)

    return _call(body, name=name, grid=grid, in_specs=in_specs, out_specs=out_specs, out_shape=outs,
                 compiler_params=_cparams("parallel", "arbitrary"))(*ins)


def _rb(tm, w, cb=0):
    return pl.BlockSpec((tm, w), lambda j, i: (i, cb + j))


def _cb(w, cb=0):
    return pl.BlockSpec((1, w), lambda j, i: (0, cb + j))


def _full(shape):
    nd = len(shape)
    return pl.BlockSpec(shape, lambda j, i: (0,) * nd)


def _sds(shape, dtype):
    return jax.ShapeDtypeStruct(shape, dtype)


def _rms_fwd(x, g, name):
    t, d = x.shape
    tm = _tile(t, 512, 8)
    return _rowcall(lambda xb, gb: (_rms_core(xb, gb),), [x, g], [_rb(tm, d), _full((1, d))],
                    [_sds((t, d), _BF)], [_rb(tm, d)], [False], grid=(1, t // tm), name=name)[0]


def _rms_bwd(x, g, dh, dres, name):
    t, d = x.shape
    tm = _tile(t, 256, 8)

    def fn(xb, gb, dhb, drb):
        _, vjp = jax.vjp(_rms_core, xb, gb)
        dx, dg = vjp(dhb.astype(_F32))
        dx = dx + drb
        return dx, dx, dg

    return _rowcall(fn, [x, g, dh, dres], [_rb(tm, d), _full((1, d)), _rb(tm, d), _rb(tm, d)],
                    [_sds((t, d), _F32), _sds((t, d), _BF), _sds((1, d), _F32)],
                    [_rb(tm, d), _rb(tm, d), _full((1, d))], [False, False, True],
                    grid=(1, t // tm), name=name)


def _loss_head(x3, tgt, g):
    t, d = x3.shape
    tm = _tile(t, 256, 8)

    def fn(xb, tb, gb):
        y, vjp = jax.vjp(_rms_core, xb, gb)
        e = y - tb
        part = 0.5 * jnp.sum(jnp.mean(e * e, axis=-1, keepdims=True), axis=0, keepdims=True)
        dx, dg = vjp(e * (1.0 / d))
        return jnp.broadcast_to(part, (1, _LANE)), dx, dx, dg

    return _rowcall(fn, [x3, tgt, g], [_rb(tm, d), _rb(tm, d), _full((1, d))],
                    [_sds((1, _LANE), _F32), _sds((t, d), _F32), _sds((t, d), _BF), _sds((1, d), _F32)],
                    [_full((1, _LANE)), _rb(tm, d), _rb(tm, d), _full((1, d))], [True, False, False, True],
                    grid=(1, t // tm), name="loss_head")


def _s5_disc(a_re, a_im, log_dt):
    lam_re = jnp.minimum(a_re, _S5_MAX_RE)
    lam_im = a_im
    dt = jnp.exp(log_dt)
    mag = jnp.exp(lam_re * dt)
    abar_re = mag * jnp.cos(lam_im * dt)
    abar_im = mag * jnp.sin(lam_im * dt)
    den = lam_re * lam_re + lam_im * lam_im
    nr = abar_re - 1.0
    ni = abar_im
    coef_re = (nr * lam_re + ni * lam_im) / den
    coef_im = (ni * lam_re - nr * lam_im) / den
    return abar_re, abar_im, coef_re, coef_im


def _s5_prep(a_re, a_im, log_dt):
    g, p = a_re.shape

    def body(ar, ai, ld, o0, o1, o2, o3):
        outs = _s5_disc(ar[...], ai[...], ld[...])
        for o, v in zip((o0, o1, o2, o3), outs):
            o[...] = v

    return _call(body, name="s5_prep", out_shape=[_sds((g, p), _F32)] * 4)(a_re, a_im, log_dt)


def _s5_prep_bwd(a_re, a_im, log_dt, cts):
    g, p = a_re.shape

    def body(ar, ai, ld, c0, c1, c2, c3, d0, d1, d2):
        _, vjp = jax.vjp(_s5_disc, ar[...], ai[...], ld[...])
        outs = vjp((c0[...], c1[...], c2[...], c3[...]))
        for o, v in zip((d0, d1, d2), outs):
            o[...] = v

    return _call(body, name="s5_prep_bwd",
                 out_shape=[_sds((g, p), _F32), _sds((g, p), _F32), _sds((g, 1), _F32)])(a_re, a_im, log_dt, *cts)


def _s5_block(u, car, cai, ar, ai, b_re, b_im, c_re, c_imn, dvec):
    bur = _bdot(u, b_re)
    bui = _bdot(u, b_im)
    shape8 = (_SUBLANES, ar.shape[1])
    pows = [(ar, ai)]
    for _ in range(2):
        pr, pi = pows[-1]
        pows.append((pr * pr - pi * pi, 2.0 * pr * pi))
    pows = [(jnp.broadcast_to(pr, shape8), jnp.broadcast_to(pi, shape8)) for pr, pi in pows]

    def scan8(xr, xi):
        for k, (pr, pi) in enumerate(pows):
            dr = _shift_down(xr, 1 << k)
            di = _shift_down(xi, 1 << k)
            xr, xi = xr + pr * dr - pi * di, xi + pr * di + pi * dr
        return xr, xi

    row8 = lax.broadcasted_iota(jnp.int32, (_SUBLANES, ar.shape[1]), 0)
    tr, ti = scan8(jnp.where(row8 == 0, ar, 0.0), jnp.where(row8 == 0, ai, 0.0))
    outs_r, outs_i = [], []
    for xr, xi in zip(_split_rows(bur), _split_rows(bui)):
        xr, xi = scan8(xr, xi)
        xr, xi = xr + tr * car - ti * cai, xi + tr * cai + ti * car
        car, cai = _last_row(xr), _last_row(xi)
        outs_r.append(xr)
        outs_i.append(xi)
    y = _bdot(_join_rows(tuple(outs_r)), c_re) + _bdot(_join_rows(tuple(outs_i)), c_imn) + dvec * u
    return y, car, cai


def _s5_specs(tb, lw, nt, rev):
    tmap = (lambda t: nt - 1 - t) if rev else (lambda t: t)
    vec = pl.BlockSpec((1, lw), lambda s, t: (0, s))
    return dict(
        u=pl.BlockSpec((tb, _LANE), lambda s, t: (tmap(t), s)),
        car=pl.BlockSpec((None, 1, lw), lambda s, t: (tmap(t), 0, s)),
        vec=vec,
        bmat=pl.BlockSpec((None, _LANE, lw), lambda s, t: (s, 0, 0)),
        cmat=pl.BlockSpec((None, lw, _LANE), lambda s, t: (s, 0, 0)),
        dvec=pl.BlockSpec((1, _LANE), lambda s, t: (0, s)),
    )


def _s5_fwd(proj, par, t, w5, rider=None):
    ar, ai, b_re, b_im, c_re, c_imn, dvec = par
    ns = w5 // _LANE
    lw = ar.shape[1] // ns
    tb = min(_S5_TB, t)
    nt = t // tb
    sp = _s5_specs(tb, lw, nt, False)

    def body(u_ref, ar_r, ai_r, bre_r, bim_r, cre_r, cim_r, d_r, y_ref, car_ref, cai_ref, s_r, s_i):
        @pl.when(pl.program_id(1) == 0)
        def _():
            s_r[...] = jnp.zeros_like(s_r)
            s_i[...] = jnp.zeros_like(s_i)

        car_ref[...] = s_r[...]
        cai_ref[...] = s_i[...]
        y, ncr, nci = _s5_block(u_ref[...], s_r[...], s_i[...], ar_r[...], ai_r[...],
                                bre_r[...], bim_r[...], cre_r[...], cim_r[...], d_r[...])
        y_ref[...] = y
        s_r[...] = ncr
        s_i[...] = nci

    kw = dict(name="s5_fwd", grid=(ns, nt),
              in_specs=[sp["u"], sp["vec"], sp["vec"], sp["bmat"], sp["bmat"],
                        sp["cmat"], sp["cmat"], sp["dvec"]],
              out_specs=[sp["u"], sp["car"], sp["car"]],
              out_shape=[_sds((t, w5), _F32), _sds((nt, 1, ns * lw), _F32), _sds((nt, 1, ns * lw), _F32)],
              scratch_shapes=[pltpu.VMEM((1, lw), _F32), pltpu.VMEM((1, lw), _F32)])
    args = (proj, ar, ai, b_re, b_im, c_re, c_imn, dvec)
    if rider is None:
        return _call(body, compiler_params=_cparams("parallel", "arbitrary"), **kw)(*args), []
    return _call_riding(body, rider, args=args, **kw)


def _s5_bwd(proj, dy, car, cai, par, t, w5, rider=None):
    ar, ai, b_re, b_im, c_re, c_imn, dvec = par
    ns = w5 // _LANE
    lw = ar.shape[1] // ns
    tb = min(_S5_TB, t)
    nt = t // tb
    sp = _s5_specs(tb, lw, nt, True)

    def body(u_ref, dy_ref, car_ref, cai_ref, ar_r, ai_r, bre_r, bim_r, cre_r, cim_r, d_r,
             du_ref, g_ar, g_ai, g_bre, g_bim, g_cre, g_cim, g_d, ds_r, ds_i):
        accs = (g_ar, g_ai, g_bre, g_bim, g_cre, g_cim, g_d)

        @pl.when(pl.program_id(1) == 0)
        def _():
            ds_r[...] = jnp.zeros_like(ds_r)
            ds_i[...] = jnp.zeros_like(ds_i)
            for o in accs:
                o[...] = jnp.zeros_like(o)

        _, vjp = jax.vjp(_s5_block, u_ref[...], car_ref[...], cai_ref[...], ar_r[...], ai_r[...],
                         bre_r[...], bim_r[...], cre_r[...], cim_r[...], d_r[...])
        grads = vjp((dy_ref[...], ds_r[...], ds_i[...]))
        du_ref[...] = grads[0].astype(_BF)
        ds_r[...] = grads[1]
        ds_i[...] = grads[2]
        for o, gval in zip(accs, grads[3:]):
            o[...] += gval

    vec_o = _sds((1, ns * lw), _F32)
    kw = dict(name="s5_bwd", grid=(ns, nt),
              in_specs=[sp["u"], sp["u"], sp["car"], sp["car"], sp["vec"], sp["vec"],
                        sp["bmat"], sp["bmat"], sp["cmat"], sp["cmat"], sp["dvec"]],
              out_specs=[sp["u"], sp["vec"], sp["vec"], sp["bmat"], sp["bmat"],
                         sp["cmat"], sp["cmat"], sp["dvec"]],
              out_shape=[_sds((t, w5), _BF), vec_o, vec_o,
                         _sds(b_re.shape, _F32), _sds(b_re.shape, _F32), _sds(c_re.shape, _F32),
                         _sds(c_re.shape, _F32), _sds((1, w5), _F32)],
              scratch_shapes=[pltpu.VMEM((1, lw), _F32), pltpu.VMEM((1, lw), _F32)])
    args = (proj, dy, car, cai, ar, ai, b_re, b_im, c_re, c_imn, dvec)
    if rider is None:
        return _call(body, compiler_params=_cparams("parallel", "arbitrary"), **kw)(*args), []
    return _call_riding(body, rider, args=args, **kw)


def _bd_in(b):
    g, p, c = b.shape
    s = g // _S5_SET
    b4 = b.reshape(s, _S5_SET, p, c).transpose(0, 1, 3, 2)
    eye = jnp.eye(_S5_SET, dtype=b.dtype)
    return (b4[:, :, :, None, :] * eye[None, :, None, :, None]).reshape(s, _S5_SET * c, _S5_SET * p)


def _bd_in_grad(d, p, c):
    s = d.shape[0]
    eye = jnp.eye(_S5_SET, dtype=d.dtype)
    d5 = d.reshape(s, _S5_SET, c, _S5_SET, p) * eye[None, :, None, :, None]
    return d5.sum(axis=3).transpose(0, 1, 3, 2).reshape(s * _S5_SET, p, c)


def _bd_out(cm):
    g, c, p = cm.shape
    s = g // _S5_SET
    c4 = cm.reshape(s, _S5_SET, c, p).transpose(0, 1, 3, 2)
    eye = jnp.eye(_S5_SET, dtype=cm.dtype)
    return (c4[:, :, :, None, :] * eye[None, :, None, :, None]).reshape(s, _S5_SET * p, _S5_SET * c)


def _bd_out_grad(d, p, c):
    s = d.shape[0]
    eye = jnp.eye(_S5_SET, dtype=d.dtype)
    d5 = d.reshape(s, _S5_SET, p, _S5_SET, c) * eye[None, :, None, :, None]
    return d5.sum(axis=3).transpose(0, 1, 3, 2).reshape(s * _S5_SET, c, p)


def _s5glu_fwd(y, wglu, bglu):
    t, w5 = y.shape
    tm = _tile(t, 256, 8)

    def fn(yb, wb, bb):
        z1 = _gelu(yb)
        a = _dg(z1, wb, _NN) + bb
        return (z1 * _sigmoid(a),)

    return _rowcall(fn, [y, wglu, bglu], [_rb(tm, w5), _full(wglu.shape), _full((1, w5))],
                    [_sds((t, w5), _BF)], [_rb(tm, w5)], [False], grid=(1, t // tm), name="s5glu_fwd")[0]


def _s5glu_bwd(y, dz, wglu, bglu):
    t, w5 = y.shape
    tm = _tile(t, 256, 8)

    def fn(yb, dzb, wb, bb):
        dzb = dzb.astype(_F32)
        z1, gelu_vjp = jax.vjp(_gelu, yb)
        sig = _sigmoid(_dg(z1, wb, _NN) + bb)
        da = dzb * z1 * sig * (1.0 - sig)
        dz1 = dzb * sig + _dg(da, wb, _NT)
        (dy,) = gelu_vjp(dz1)
        return dy, _dg(z1, da, _TN), jnp.sum(da, axis=0, keepdims=True)

    return _rowcall(fn, [y, dz, wglu, bglu], [_rb(tm, w5), _rb(tm, w5), _full(wglu.shape), _full((1, w5))],
                    [_sds((t, w5), _F32), _sds((w5, w5), _F32), _sds((1, w5), _F32)],
                    [_rb(tm, w5), _full((w5, w5)), _full((1, w5))], [False, True, True],
                    grid=(1, t // tm), name="s5glu_bwd")


_LEVELS = (6, 5, 4, 3, 2, 1)


def _tri_stack(c):
    t = jnp.arange(c, dtype=jnp.int32)[:, None]
    j = jnp.arange(c, dtype=jnp.int32)[None, :]
    low = (j <= t).astype(_F32)
    mats = [low]
    for sh in _LEVELS:
        r = ((t >> sh) << sh) + ((1 << (sh - 1)) - 1)
        mats.append(low - (j <= r).astype(_F32))
    stack = jnp.concatenate(mats, axis=0).astype(_BF)
    return stack, stack.T


def _split_dot(mat, x):
    l = x.shape[1]
    hi = x.astype(_BF)
    lo = (x - hi.astype(_F32)).astype(_BF)
    out = jnp.dot(mat, jnp.concatenate([hi, lo], axis=1), preferred_element_type=_F32)
    return out[:, :l] + out[:, l:]


@jax.custom_vjp
def _decay_sums(lf, tri, tri_t):
    c = lf.shape[0]
    out = _split_dot(tri, lf)
    return tuple(out[k * c:(k + 1) * c] for k in range(len(_LEVELS) + 1))


def _decay_sums_f(lf, tri, tri_t):
    return _decay_sums(lf, tri, tri_t), (tri, tri_t)


def _decay_sums_b(res, gs):
    tri, tri_t = res
    return _split_dot(tri_t, jnp.concatenate(gs, axis=0)), jnp.zeros_like(tri), jnp.zeros_like(tri_t)


_decay_sums.defvjp(_decay_sums_f, _decay_sums_b)


def _hgrn_chunk(qi, fi, vi, gi, st, lb, ng, tri, tri_t):
    c = qi.shape[0]
    row = lax.broadcasted_iota(jnp.int32, (c, 1), 0)
    q = _silu(qi)
    lf = jnp.log(lb + (1.0 - lb) * _sigmoid(fi))
    k = (1.0 - lb) * _sigmoid(-fi)
    sums = _decay_sums(lf, tri, tri_t)
    b = sums[0]
    btot = jnp.sum(lf, axis=0, keepdims=True)
    inter = _bdot_nt(q * jnp.exp(b), st)
    tt = lax.broadcasted_iota(jnp.int32, (c, c), 0)
    ss = lax.broadcasted_iota(jnp.int32, (c, c), 1)
    sc = jnp.where(tt == ss, jnp.sum(q * k, axis=1, keepdims=True), 0.0)
    for sh, p in zip(_LEVELS, sums[1:]):
        upper = ((row >> (sh - 1)) & 1) == 1
        qm = jnp.where(upper, q * jnp.exp(jnp.where(upper, p, 0.0)), 0.0)
        km = jnp.where(upper, 0.0, k * jnp.exp(jnp.where(upper, 0.0, -p)))
        sc = sc + jnp.where((tt >> sh) == (ss >> sh), _bdot_nt(qm, km), 0.0)
    o = inter + _bdot(sc, vi)
    st_new = st * jnp.exp(btot) + _bdot_tn(vi, k * jnp.exp(btot - b))
    on = o * lax.rsqrt(jnp.mean(o * o, axis=1, keepdims=True) + _RMS_EPS) * ng
    return on * _silu(gi), st_new


def _hgrn_geom(t, w5, hw):
    hp = _HGRN_HP if (hw // _LANE) % _HGRN_HP == 0 and w5 % (_LANE * _HGRN_HP) == 0 else 1
    seg = min(t, _HGRN_SEG)
    return hp, hp * _LANE, seg, t // seg


def _hgrn_in_specs(seg, wd, w5, hw, tmap):
    nhp = hw // wd
    qb = w5 // wd
    return [pl.BlockSpec((seg, wd), (lambda h, s, k=k: (tmap(s), qb + k * nhp + h))) for k in range(4)]


def _hgrn_fwd(proj, lb, ng, t, w5, hw, rider=None):
    assert _CHUNK == 64
    hp, wd, seg, nseg = _hgrn_geom(t, w5, hw)
    ncs = seg // _CHUNK
    vec = pl.BlockSpec((1, wd), lambda h, s: (0, h))

    tri, tri_t = _tri_stack(_CHUNK)

    def body(q_ref, f_ref, i_ref, g_ref, lb_ref, ng_ref, tri_ref, trit_ref, og_ref, st_ref, s_scr):
        @pl.when(pl.program_id(1) == 0)
        def _():
            s_scr[...] = jnp.zeros_like(s_scr)

        tri_v, trit_v = tri_ref[...], trit_ref[...]

        def step(ci, carry):
            r = pl.ds(pl.multiple_of(ci * _CHUNK, _CHUNK), _CHUNK)
            for hh in range(hp):
                ln = slice(hh * _LANE, (hh + 1) * _LANE)
                st_ref[hh, ci] = s_scr[hh]
                og, sn = _hgrn_chunk(q_ref[r, ln], f_ref[r, ln], i_ref[r, ln], g_ref[r, ln], s_scr[hh],
                                     lb_ref[:, ln], ng_ref[:, ln], tri_v, trit_v)
                og_ref[r, ln] = og.astype(_BF)
                s_scr[hh] = sn
            return carry

        lax.fori_loop(0, ncs, step, 0)

    kw = dict(name="hgrn_fwd", grid=(hw // wd, nseg),
              in_specs=_hgrn_in_specs(seg, wd, w5, hw, lambda s: s) + [
                  vec, vec, pl.BlockSpec(tri.shape, lambda h, s: (0, 0)), pl.BlockSpec(tri_t.shape, lambda h, s: (0, 0))],
              out_specs=[pl.BlockSpec((seg, wd), lambda h, s: (s, h)),
                         pl.BlockSpec((hp, ncs, _LANE, _LANE), lambda h, s: (h, s, 0, 0))],
              out_shape=[_sds((t, hw), _BF), _sds((hw // _LANE, t // _CHUNK, _LANE, _LANE), _F32)],
              scratch_shapes=[pltpu.VMEM((hp, _LANE, _LANE), _F32)])
    args = (proj, proj, proj, proj, lb, ng, tri, tri_t)
    if rider is None:
        return _call(body, compiler_params=_cparams("parallel", "arbitrary"), **kw)(*args), []
    return _call_riding(body, rider, args=args, **kw)


def _hgrn_bwd(proj, lb, ng, st_all, dog, t, w5, hw):
    hp, wd, seg, nseg = _hgrn_geom(t, w5, hw)
    ncs = seg // _CHUNK

    def rev(s):
        return nseg - 1 - s

    vec = pl.BlockSpec((1, wd), lambda h, s: (0, h))
    col = pl.BlockSpec((seg, wd), lambda h, s: (rev(s), h))

    tri, tri_t = _tri_stack(_CHUNK)

    def body(q_ref, f_ref, i_ref, g_ref, lb_ref, ng_ref, tri_ref, trit_ref, st_ref, dog_ref,
             dq_ref, df_ref, di_ref, dg_ref, dlb_ref, dng_ref, ds_scr):
        @pl.when(pl.program_id(1) == 0)
        def _():
            ds_scr[...] = jnp.zeros_like(ds_scr)
            dlb_ref[...] = jnp.zeros_like(dlb_ref)
            dng_ref[...] = jnp.zeros_like(dng_ref)

        tri_v, trit_v = tri_ref[...], trit_ref[...]

        def step(kk, carry):
            ci = ncs - 1 - kk
            r = pl.ds(pl.multiple_of(ci * _CHUNK, _CHUNK), _CHUNK)
            for hh in range(hp):
                ln = slice(hh * _LANE, (hh + 1) * _LANE)
                _, vjp = jax.vjp(_hgrn_chunk, q_ref[r, ln], f_ref[r, ln], i_ref[r, ln], g_ref[r, ln], st_ref[hh, ci],
                                 lb_ref[:, ln], ng_ref[:, ln], tri_v, trit_v)
                dq, df, di, dg, ds, dlb, dng = vjp((dog_ref[r, ln].astype(_F32), ds_scr[hh]))[:7]
                dq_ref[r, ln] = dq.astype(_BF)
                df_ref[r, ln] = df.astype(_BF)
                di_ref[r, ln] = di.astype(_BF)
                dg_ref[r, ln] = dg.astype(_BF)
                ds_scr[hh] = ds
                dlb_ref[:, ln] += dlb
                dng_ref[:, ln] += dng
            return carry

        lax.fori_loop(0, ncs, step, 0)

    return _call(body, name="hgrn_bwd", grid=(hw // wd, nseg),
                 in_specs=_hgrn_in_specs(seg, wd, w5, hw, rev) + [
                     vec, vec, pl.BlockSpec(tri.shape, lambda h, s: (0, 0)), pl.BlockSpec(tri_t.shape, lambda h, s: (0, 0)),
                     pl.BlockSpec((hp, ncs, _LANE, _LANE), lambda h, s: (h, rev(s), 0, 0)), col],
                 out_specs=[col, col, col, col, vec, vec],
                 out_shape=[_sds((t, hw), _BF)] * 4 + [_sds((1, hw), _F32)] * 2,
                 scratch_shapes=[pltpu.VMEM((hp, _LANE, _LANE), _F32)],
                 compiler_params=_cparams("parallel", "arbitrary"))(
                     proj, proj, proj, proj, lb, ng, tri, tri_t, st_all, dog)


def _lb_of(logits):
    mx = jnp.max(logits, axis=0, keepdims=True)
    e = jnp.exp(logits - mx)
    sm = e / jnp.sum(e, axis=0, keepdims=True)
    row = lax.broadcasted_iota(jnp.int32, logits.shape, 0)
    return jnp.sum(jnp.where(row == 0, sm, 0.0), axis=0, keepdims=True)


def _lb_prep(logits):
    def body(l_ref, o_ref):
        o_ref[...] = _lb_of(l_ref[...])

    return _call(body, name="lb_prep", out_shape=_sds((1, logits.shape[1]), _F32))(logits)


def _lb_prep_bwd(logits, dlb):
    def body(l_ref, d_ref, o_ref):
        _, vjp = jax.vjp(_lb_of, l_ref[...])
        o_ref[...] = vjp(d_ref[...])[0]

    return _call(body, name="lb_prep_bwd", out_shape=_sds(logits.shape, _F32))(logits, dlb)


def _merge_fwd(proj, ys, yh, t, d, gs_off):
    tm = _tile(t, 256, 8)
    w = _tile(d, 1024)
    nb = d // w

    def fn(gs, gh, a, b):
        return (_sigmoid(gs) * a + _sigmoid(gh) * b,)

    return _rowcall(fn, [proj, proj, ys, yh], [_rb(tm, w, gs_off // w), _rb(tm, w, gs_off // w + nb), _rb(tm, w), _rb(tm, w)],
                    [_sds((t, d), _BF)], [_rb(tm, w)], [False], grid=(nb, t // tm), name="merge_fwd")[0]


def _merge_bwd(proj, ys, yh, dm, t, d, gs_off):
    tm = _tile(t, 256, 8)
    w = _tile(d, 1024)
    nb = d // w

    def fn(gs, gh, a, b, g):
        g = g.astype(_F32)
        s1 = _sigmoid(gs)
        s2 = _sigmoid(gh)
        return g * s1, g * s2, g * a * s1 * (1.0 - s1), g * b * s2 * (1.0 - s2)

    return _rowcall(fn, [proj, proj, ys, yh, dm],
                    [_rb(tm, w, gs_off // w), _rb(tm, w, gs_off // w + nb), _rb(tm, w), _rb(tm, w), _rb(tm, w)],
                    [_sds((t, d), _BF)] * 4, [_rb(tm, w)] * 4, [False] * 4, grid=(nb, t // tm), name="merge_bwd")


_HALO = 16


def _prev_rows(up_prev, is_first):
    p1 = jnp.where(is_first, 0.0, up_prev[_HALO - 1:_HALO, :])
    p2 = jnp.where(is_first, 0.0, up_prev[_HALO - 2:_HALO - 1, :])
    return p1, p2


def _causal_taps(cur, p1, p2):
    row = lax.broadcasted_iota(jnp.int32, cur.shape, 0)
    s1 = jnp.where(row == 0, p1, _shift_down(cur, 1))
    s2 = jnp.where(row == 0, p2, jnp.where(row == 1, p1, _shift_down(cur, 2)))
    return s1, s2


def _pair_perm(nbh, per):
    def perm(n_):
        big = n_ // per
        return (2 * (big % nbh) + big // nbh) * per + n_ % per
    return perm


def _conv_specs(tm, w, nb_half, t):
    r8 = tm // _HALO
    cur_g = pl.BlockSpec((tm, w), lambda j, i: (i, 2 * j))
    cur_v = pl.BlockSpec((tm, w), lambda j, i: (i, 2 * j + 1))
    prev_g = pl.BlockSpec((_HALO, w), lambda j, i: (jnp.maximum(i * r8 - 1, 0), 2 * j))
    prev_v = pl.BlockSpec((_HALO, w), lambda j, i: (jnp.maximum(i * r8 - 1, 0), 2 * j + 1))
    w_g = pl.BlockSpec((3, w), lambda j, i: (0, j))
    w_v = pl.BlockSpec((3, w), lambda j, i: (0, nb_half + j))
    b_g = pl.BlockSpec((1, w), lambda j, i: (0, j))
    b_v = pl.BlockSpec((1, w), lambda j, i: (0, nb_half + j))
    return cur_g, cur_v, prev_g, prev_v, w_g, w_v, b_g, b_v


def _conv_of(cur, prev8, wt, bias, is_first):
    cur, prev8 = cur.astype(_F32), prev8.astype(_F32)
    p1, p2 = _prev_rows(prev8, is_first)
    s1, s2 = _causal_taps(cur, p1, p2)
    return bias + wt[0:1, :] * s2 + wt[1:2, :] * s1 + wt[2:3, :] * cur


def _convact_fwd(up, cw, cb, t, dff):
    tm = _tile(t, 512, _HALO)
    w = _tile(dff, 1408)
    nbh = dff // w
    sp = _conv_specs(tm, w, nbh, t)

    def body(ug, uv, pg, pv, wg, wv, bg, bv, o_ref):
        first = pl.program_id(1) == 0
        gate = _conv_of(ug[...], pg[...], wg[...], bg[...], first)
        val = _conv_of(uv[...], pv[...], wv[...], bv[...], first)
        o_ref[...] = (_silu(gate) * val).astype(_BF)

    return _call(body, name="convact_fwd", grid=(nbh, t // tm), in_specs=list(sp),
                 out_specs=pl.BlockSpec((tm, w), lambda j, i: (i, j)), out_shape=_sds((t, dff), _BF),
                 compiler_params=_cparams("parallel", "arbitrary"))(up, up, up, up, cw, cw, cb, cb)


def _convact_bwd(up, dact, cw, cb, t, dff):
    tm = _tile(t, 256, _HALO)
    w = _tile(dff, 1408)
    nbh = dff // w
    r8 = tm // _HALO
    nt = t // tm
    last8 = t // _HALO - 1

    def triple(off):
        return [pl.BlockSpec((tm, w), lambda j, i: (i, 2 * j + off)),
                pl.BlockSpec((_HALO, w), lambda j, i: (jnp.minimum((i + 1) * r8, last8), 2 * j + off)),
                pl.BlockSpec((_HALO, w), lambda j, i: (jnp.maximum(i * r8 - 1, 0), 2 * j + off))]

    def body(ug, ugn, ugp, uv, uvn, uvp, wg_ref, wv_ref, bg_ref, bv_ref, da_ref, dan_ref, du_ref, dw_ref, db_ref):
        i = pl.program_id(1)
        first = i == 0
        is_last = i == nt - 1
        gate = _conv_of(jnp.concatenate([ug[...], ugn[...]], axis=0), ugp[...], wg_ref[...], bg_ref[...], first)
        val = _conv_of(jnp.concatenate([uv[...], uvn[...]], axis=0), uvp[...], wv_ref[...], bv_ref[...], first)
        da = jnp.concatenate([da_ref[...], dan_ref[...]], axis=0).astype(_F32)
        row = lax.broadcasted_iota(jnp.int32, da.shape, 0)
        da = jnp.where(jnp.logical_and(row >= tm, is_last), 0.0, da)
        sg = _sigmoid(gate)
        halves = ((da * val * sg * (1.0 + gate * (1.0 - sg)), wg_ref, ug, ugp),
                  (da * gate * sg, wv_ref, uv, uvp))

        @pl.when(first)
        def _():
            dw_ref[...] = jnp.zeros_like(dw_ref)
            db_ref[...] = jnp.zeros_like(db_ref)

        for h, (dc, w_ref, u_ref, p_ref) in enumerate(halves):
            ln = slice(h * w, (h + 1) * w)
            wt = w_ref[...]
            du = wt[2:3, :] * dc + wt[1:2, :] * _shift_up(dc, 1) + wt[0:1, :] * _shift_up(dc, 2)
            du_ref[:, ln] = du[0:tm, :].astype(_BF)
            dcm = dc[0:tm, :]
            cur = u_ref[...].astype(_F32)
            p1, p2 = _prev_rows(p_ref[...].astype(_F32), first)
            s1, s2 = _causal_taps(cur, p1, p2)
            dw_ref[0:1, ln] += jnp.sum(dcm * s2, axis=0, keepdims=True)
            dw_ref[1:2, ln] += jnp.sum(dcm * s1, axis=0, keepdims=True)
            dw_ref[2:3, ln] += jnp.sum(dcm * cur, axis=0, keepdims=True)
            db_ref[:, ln] += jnp.sum(dcm, axis=0, keepdims=True)

    in_specs = (triple(0) + triple(1)
                + [pl.BlockSpec((3, w), lambda j, i: (0, j)), pl.BlockSpec((3, w), lambda j, i: (0, nbh + j)),
                   pl.BlockSpec((1, w), lambda j, i: (0, j)), pl.BlockSpec((1, w), lambda j, i: (0, nbh + j)),
                   pl.BlockSpec((tm, w), lambda j, i: (i, j)),
                   pl.BlockSpec((_HALO, w), lambda j, i: (jnp.minimum((i + 1) * r8, last8), j))])
    dup, dw_p, db_p = _call(
        body, name="convact_bwd", grid=(nbh, nt), in_specs=in_specs,
        out_specs=[pl.BlockSpec((tm, 2 * w), lambda j, i: (i, j)), pl.BlockSpec((3, 2 * w), lambda j, i: (0, j)),
                   pl.BlockSpec((1, 2 * w), lambda j, i: (0, j))],
        out_shape=[_sds((t, 2 * dff), _BF), _sds((3, 2 * dff), _F32), _sds((1, 2 * dff), _F32)],
        compiler_params=_cparams("parallel", "arbitrary"))(up, up, up, up, up, up, cw, cw, cb, cb, dact, dact)

    def natural(v):
        k = v.shape[0]
        return v.reshape(k, nbh, 2, w).transpose(0, 2, 1, 3).reshape(k, 2 * dff)

    return dup, natural(dw_p), natural(db_p)


def _me():
    return lax.axis_index("x"), lax.axis_index("y"), lax.axis_index("c")


def _other_chips(x, y):
    return [(1 - x, y), (x, 1 - y), (1 - x, 1 - y)]


def _rcopy(src, dst, ssem, rsem, dev):
    return pltpu.make_async_remote_copy(src_ref=src, dst_ref=dst, send_sem=ssem, recv_sem=rsem,
                                        device_id=dev, device_id_type=_MESH)


def _cast_into_slot(w, sel):
    r, c = w.shape
    tm = _tile(r, 256, 16)

    def body(sel_ref, w_ref, o_ref):
        o_ref[...] = w_ref[...].astype(_BF)

    gs = pltpu.PrefetchScalarGridSpec(
        num_scalar_prefetch=1, grid=(r // tm,),
        in_specs=[pl.BlockSpec((tm, c), lambda i, s: (i, 0))],
        out_specs=pl.BlockSpec((None, tm, c), lambda i, s: (s[0], i, 0)))
    return _call(body, name="cast_into_slot", grid_spec=gs, out_shape=_sds((_NCHIP, r, c), _BF),
                 compiler_params=_cparams("parallel"))(sel, w)


class _Plan:
    def __init__(self, ins, outs, aliases, sems, start, finish):
        self.ins, self.outs, self.aliases, self.sems, self.start, self.finish = ins, outs, aliases, sems, start, finish


def _run_plan(plan, name):
    ni, no = len(plan.ins), len(plan.outs)

    def body(*refs):
        rin, rout, sems = refs[:ni], refs[ni:ni + no], refs[ni + no:]
        plan.start(rin, rout, sems)
        plan.finish(rin, rout, sems)

    return _call(body, name=name, in_specs=[_ANY] * ni, out_specs=[_ANY] * no, out_shape=list(plan.outs),
                 input_output_aliases=dict(plan.aliases), scratch_shapes=list(plan.sems))(*plan.ins)


def _call_riding(body, rider, *, name, grid, in_specs, out_specs, out_shape, scratch_shapes, args):
    n_in, n_out, n_scr = len(in_specs), len(out_specs), len(scratch_shapes)
    n_rin, n_rout = len(rider.ins), len(rider.outs)

    def wrapped(*refs):
        ins, rin = refs[:n_in], refs[n_in:n_in + n_rin]
        o0 = n_in + n_rin
        outs, rout = refs[o0:o0 + n_out], refs[o0 + n_out:o0 + n_out + n_rout]
        s0 = o0 + n_out + n_rout
        scratch, sems = refs[s0:s0 + n_scr], refs[s0 + n_scr:]
        first = functools.reduce(jnp.logical_and, [pl.program_id(k) == 0 for k in range(len(grid))])
        last = functools.reduce(jnp.logical_and, [pl.program_id(k) == grid[k] - 1 for k in range(len(grid))])

        @pl.when(first)
        def _():
            rider.start(rin, rout, sems)

        body(*ins, *outs, *scratch)

        @pl.when(last)
        def _():
            rider.finish(rin, rout, sems)

    res = _call(wrapped, name=name, grid=grid, in_specs=list(in_specs) + [_ANY] * n_rin,
                out_specs=list(out_specs) + [_ANY] * n_rout, out_shape=list(out_shape) + list(rider.outs),
                input_output_aliases={n_in + k: n_out + v for k, v in rider.aliases.items()},
                scratch_shapes=list(scratch_shapes) + list(rider.sems),
                compiler_params=_cparams(*(["arbitrary"] * len(grid))))(*args, *rider.ins)
    return list(res[:n_out]), list(res[n_out:])


def _gather_plan(bufs, direct, rel=(0, 1, 2)):
    n, nd = len(bufs), len(direct)

    def where():
        x, y, c = _me()
        return c, 2 * x + y, _other_chips(x, y), (x, y, 1 - c)

    def picked(chips):
        return [(j, chips[j]) for j in rel]

    def piece(outs, a, chip, h):
        r2 = bufs[a].shape[1] // 2
        return outs[a].at[chip, pl.ds(h * r2, r2)]

    def send(outs, sems, a, j, me, c, chip):
        return _rcopy(piece(outs, a, me, c), piece(outs, a, me, c), sems[0].at[3 * a + j], sems[1].at[3 * a + j],
                      (chip[0], chip[1], c))

    def forward(outs, sems, a, j, pc, c, sib):
        return _rcopy(piece(outs, a, pc, c), piece(outs, a, pc, c), sems[2].at[3 * a + j], sems[3].at[3 * a + j], sib)

    def dsend(dins, douts, sems, a, j, me, c, chip):
        return _rcopy(dins[a], douts[a].at[me], sems[4].at[3 * a + j], sems[5].at[3 * a + j], (chip[0], chip[1], c))

    def start(rin, rout, sems):
        outs, dins, douts = rout[:n], rin[n:], rout[n:]
        c, me, chips, _ = where()
        for a in range(n):
            for j, chip in picked(chips):
                send(outs, sems, a, j, me, c, chip).start()
        for a in range(nd):
            pltpu.make_async_copy(dins[a], douts[a].at[me], sems[6].at[a]).start()
            for j, chip in enumerate(chips):
                dsend(dins, douts, sems, a, j, me, c, chip).start()

    def finish(rin, rout, sems):
        outs, dins, douts = rout[:n], rin[n:], rout[n:]
        c, me, chips, sib = where()
        for a in range(n):
            for j, (cx, cy) in picked(chips):
                pc = 2 * cx + cy
                _rcopy(piece(outs, a, me, c), piece(outs, a, pc, c), sems[0].at[3 * a + j], sems[1].at[3 * a + j],
                       (cx, cy, c)).wait_recv()
                forward(outs, sems, a, j, pc, c, sib).start()
        for a in range(n):
            for j, (cx, cy) in picked(chips):
                pc = 2 * cx + cy
                _rcopy(piece(outs, a, pc, 1 - c), piece(outs, a, pc, 1 - c), sems[2].at[3 * a + j],
                       sems[3].at[3 * a + j], sib).wait_recv()
        for a in range(nd):
            for j, (cx, cy) in enumerate(chips):
                _rcopy(dins[a], douts[a].at[2 * cx + cy], sems[4].at[3 * a + j], sems[5].at[3 * a + j],
                       (cx, cy, c)).wait_recv()
        for a in range(n):
            for j, (cx, cy) in picked(chips):
                send(outs, sems, a, j, me, c, (cx, cy)).wait_send()
                forward(outs, sems, a, j, 2 * cx + cy, c, sib).wait_send()
        for a in range(nd):
            pltpu.make_async_copy(dins[a], douts[a].at[me], sems[6].at[a]).wait()
            for j, chip in enumerate(chips):
                dsend(dins, douts, sems, a, j, me, c, chip).wait_send()

    dma = pltpu.SemaphoreType.DMA
    return _Plan(list(bufs) + list(direct),
                 [_sds(b.shape, b.dtype) for b in bufs] + [_sds((_NCHIP,) + s.shape, s.dtype) for s in direct],
                 {a: a for a in range(n)},
                 [dma((3 * max(n, 1),)), dma((3 * max(n, 1),)), dma((3 * max(n, 1),)), dma((3 * max(n, 1),)),
                  dma((3 * max(nd, 1),)), dma((3 * max(nd, 1),)), dma((max(nd, 1),))], start, finish)


def _mm_shard_order(a, mid_plan, end_plan, sel, *, tm, tn, name):
    b = mid_plan.ins[0]
    s, k, ns = b.shape
    m = a.shape[0]
    npb = ns // tn
    nj, ni = s * npb, m // tm
    n_in = (len(mid_plan.ins), len(end_plan.ins))
    n_out = (len(mid_plan.outs), len(end_plan.outs))
    n_sem = (len(mid_plan.sems), len(end_plan.sems))

    def shard_of(jj, q):
        return jnp.bitwise_xor(q[0], jj // npb)

    def body(q_ref, a_ref, *rest):
        rin = (rest[:n_in[0]], rest[n_in[0]:sum(n_in)])
        o_ref = rest[sum(n_in)]
        o0 = sum(n_in) + 1
        rout = (rest[o0:o0 + n_out[0]], rest[o0 + n_out[0]:o0 + sum(n_out)])
        vbuf, bsem = rest[o0 + sum(n_out)], rest[o0 + sum(n_out) + 1]
        s0 = o0 + sum(n_out) + 2
        sems = (rest[s0:s0 + n_sem[0]], rest[s0 + n_sem[0]:s0 + sum(n_sem)])
        b_ref = rout[0][0]
        j, i = pl.program_id(0), pl.program_id(1)

        def tile_copy(jj, slot):
            return pltpu.make_async_copy(b_ref.at[shard_of(jj, q_ref), :, pl.ds((jj % npb) * tn, tn)],
                                         vbuf.at[slot], bsem.at[slot])

        @pl.when(jnp.logical_and(j == 0, i == 0))
        def _():
            mid_plan.start(rin[0], rout[0], sems[0])
            end_plan.start(rin[1], rout[1], sems[1])
            tile_copy(0, 0).start()

        @pl.when(i == 0)
        def _():
            @pl.when(j == (s - 1) * npb - 1)
            def _():
                mid_plan.finish(rin[0], rout[0], sems[0])

            @pl.when(j + 1 < nj)
            def _():
                tile_copy(j + 1, (j + 1) % 2).start()

            tile_copy(j, j % 2).wait()

        o_ref[...] = _dg(a_ref[...], vbuf[j % 2], _NN)

        @pl.when(jnp.logical_and(j == nj - 1, i == ni - 1))
        def _():
            end_plan.finish(rin[1], rout[1], sems[1])

    n_rin, n_rout = sum(n_in), sum(n_out)
    gs = pltpu.PrefetchScalarGridSpec(
        num_scalar_prefetch=1, grid=(nj, ni),
        in_specs=[pl.BlockSpec((tm, k), lambda j, i, q: (i, 0))] + [_ANY] * n_rin,
        out_specs=[pl.BlockSpec((tm, tn), lambda j, i, q: (i, shard_of(j, q) * npb + j % npb))] + [_ANY] * n_rout,
        scratch_shapes=[pltpu.VMEM((2, k, tn), _BF), pltpu.SemaphoreType.DMA((2,))] + list(mid_plan.sems) + list(end_plan.sems))
    aliases = {2 + kk: 1 + v for kk, v in mid_plan.aliases.items()}
    aliases.update({2 + n_in[0] + kk: 1 + n_out[0] + v for kk, v in end_plan.aliases.items()})
    res = _call(body, name=name, grid_spec=gs, out_shape=[_sds((m, s * ns), _F32)] + list(mid_plan.outs) + list(end_plan.outs),
                input_output_aliases=aliases,
                compiler_params=_cparams("arbitrary", "arbitrary"))(sel, a, *mid_plan.ins, *end_plan.ins)
    return res[0], list(res[1:1 + n_out[0]]), list(res[1 + n_out[0]:])


def _swap_halves(grads, name):
    n = len(grads)

    def body(*refs):
        ins, outs = refs[:n], refs[n:2 * n]
        ssem, rsem = refs[2 * n:]
        x, y, c = _me()
        sib = (x, y, 1 - c)
        cps = []
        for a in range(n):
            cp = _rcopy(ins[a].at[1 - c], outs[a], ssem.at[a], rsem.at[a], sib)
            cp.start()
            cps.append(cp)
        for cp in cps:
            cp.wait_recv()
        for cp in cps:
            cp.wait_send()

    dma = pltpu.SemaphoreType.DMA
    return _call(body, name=name, in_specs=[_ANY] * n, out_specs=[_ANY] * n,
                 out_shape=[_sds(g.shape[1:], g.dtype) for g in grads],
                 scratch_shapes=[dma((n,)), dma((n,))])(*grads)


def _add_pairs(grads, theirs, sel):
    _, s, r, c2 = grads.shape
    a3 = grads.reshape(2, s * r, c2)
    b2 = theirs.reshape(s * r, c2)
    tm = _tile(s * r, 512, 16)

    def body(sel_ref, a_ref, b_ref, o_ref):
        o_ref[...] = (a_ref[...].astype(_F32) + b_ref[...].astype(_F32)).astype(_BF)

    gs = pltpu.PrefetchScalarGridSpec(
        num_scalar_prefetch=1, grid=(s * r // tm,),
        in_specs=[pl.BlockSpec((None, tm, c2), lambda i, q: (q[1], i, 0)), pl.BlockSpec((tm, c2), lambda i, q: (i, 0))],
        out_specs=pl.BlockSpec((tm, c2), lambda i, q: (i, 0)))
    out = _call(body, name="chip_sum", grid_spec=gs, out_shape=_sds((s * r, c2), _BF),
                compiler_params=_cparams("parallel"))(sel, a3, b2)
    return out.reshape(s, r, c2)


def _exchange_plan(sums, small):
    n = len(sums)
    has_small = small is not None

    def where():
        x, y, c = _me()
        peers = [(1 - x if k & 4 else x, 1 - y if k & 2 else y, 1 - c if k & 1 else c) for k in range(1, _NDEV)]
        return c, 4 * x + 2 * y + c, _other_chips(x, y), peers

    def send(rin, rout, sems, a, j, c, chip):
        return _rcopy(rin[a].at[2 * chip[0] + chip[1]], rout[a].at[j], sems[0].at[3 * a + j], sems[1].at[3 * a + j],
                      (chip[0], chip[1], c))

    def small_send(rin, rout, sems, k, dev, peer):
        return _rcopy(rin[n], rout[n].at[dev], sems[2].at[k], sems[3].at[k], peer)

    def start(rin, rout, sems):
        c, dev, chips, peers = where()
        for a in range(n):
            for j, chip in enumerate(chips):
                send(rin, rout, sems, a, j, c, chip).start()
        if has_small:
            pltpu.make_async_copy(rin[n], rout[n].at[dev], sems[4].at[0]).start()
            for k, peer in enumerate(peers):
                small_send(rin, rout, sems, k, dev, peer).start()

    def finish(rin, rout, sems):
        c, dev, chips, peers = where()
        for a in range(n):
            for j, chip in enumerate(chips):
                send(rin, rout, sems, a, j, c, chip).wait_recv()
        if has_small:
            for k, (px, py, pc_) in enumerate(peers):
                _rcopy(rin[n], rout[n].at[4 * px + 2 * py + pc_], sems[2].at[k], sems[3].at[k], (px, py, pc_)).wait_recv()
        for a in range(n):
            for j, chip in enumerate(chips):
                send(rin, rout, sems, a, j, c, chip).wait_send()
        if has_small:
            pltpu.make_async_copy(rin[n], rout[n].at[dev], sems[4].at[0]).wait()
            for k, peer in enumerate(peers):
                small_send(rin, rout, sems, k, dev, peer).wait_send()

    dma = pltpu.SemaphoreType.DMA
    outs = [_sds((3,) + s.shape[1:], s.dtype) for s in sums]
    if has_small:
        outs.append(_sds((_NDEV,) + small.shape, small.dtype))
    return _Plan(list(sums) + ([small] if has_small else []), outs, {},
                 [dma((3 * max(n, 1),)), dma((3 * max(n, 1),)), dma((_NDEV - 1,)), dma((_NDEV - 1,)), dma((1,))],
                 start, finish)


def _shard_sum(sums, recv, sel):
    s, r, c2 = sums.shape
    tm = _tile(r, 256, 16)

    def body(sel_ref, own_ref, rc_ref, o_ref):
        rc = rc_ref[...]
        o_ref[...] = ((own_ref[...].astype(_F32) + rc[0].astype(_F32)) + rc[1].astype(_F32)) + rc[2].astype(_F32)

    gs = pltpu.PrefetchScalarGridSpec(
        num_scalar_prefetch=1, grid=(r // tm,),
        in_specs=[pl.BlockSpec((None, tm, c2), lambda i, q: (q[0], i, 0)),
                  pl.BlockSpec((3, tm, c2), lambda i, q: (0, i, 0))],
        out_specs=pl.BlockSpec((None, tm, c2), lambda i, q: (q[1], i, 0)))
    return _call(body, name="shard_sum", grid_spec=gs, out_shape=_sds((2, r, c2), _F32),
                 compiler_params=_cparams("parallel"))(sel, sums, recv)


def _sum_slots(stack, name):
    k, r, c = stack.shape
    tm = _tile(r, 256, 16 if stack.dtype == _BF else 8)

    def fn(v):
        out = v[0].astype(_F32)
        for i in range(1, k):
            out = out + v[i].astype(_F32)
        return (out,)

    return _rowcall(fn, [stack], [pl.BlockSpec((k, tm, c), lambda j, i: (0, i, 0))], [_sds((r, c), _F32)],
                    [_rb(tm, c)], [False], grid=(1, r // tm), name=name)[0]


def _share_halves(bufs):
    n = len(bufs)

    def body(*refs):
        outs = refs[n:2 * n]
        ssem, rsem = refs[2 * n:]
        x, y, c = _me()
        sib = (x, y, 1 - c)
        cps = []
        for a in range(n):
            cp = _rcopy(outs[a].at[c], outs[a].at[c], ssem.at[a], rsem.at[a], sib)
            cp.start()
            cps.append(cp)
        for a in range(n):
            _rcopy(outs[a].at[c], outs[a].at[1 - c], ssem.at[a], rsem.at[a], sib).wait_recv()
        for cp in cps:
            cp.wait_send()

    dma = pltpu.SemaphoreType.DMA
    return _call(body, name="share_halves", in_specs=[_ANY] * n, out_specs=[_ANY] * n,
                 out_shape=[_sds(b.shape, b.dtype) for b in bufs], input_output_aliases={a: a for a in range(n)},
                 scratch_shapes=[dma((n,)), dma((n,))])(*bufs)


def _adamw_math(w, g, m, v):
    m = _B1 * m + (1.0 - _B1) * g
    v = _B2 * v + (1.0 - _B2) * jnp.square(g)
    m_hat = m / (1.0 - _B1 ** _STEP)
    v_hat = v / (1.0 - _B2 ** _STEP)
    delta = -_LR * (m_hat / (jnp.sqrt(v_hat) + _ADAM_EPS) + _WD * w)
    return delta, m, v


def _adamw_shard(w, g2, m, v, name):
    r, c = w.shape
    c2 = c // 2
    tm = _tile(r, 256, 8)
    blk = pl.BlockSpec((tm, c2), lambda h, i: (i, h))

    def fn(wb, gb, mb, vb):
        return (gb,) + _adamw_math(wb, gb, mb, vb)

    return _rowcall(fn, [w, g2, m, v], [blk, pl.BlockSpec((None, tm, c2), lambda h, i: (h, i, 0)), blk, blk],
                    [_sds((r, c), _F32)] * 4, [blk] * 4, [False] * 4, grid=(2, r // tm), name=name)


def _adamw_whole(w, g, m, v, name):
    r, c = w.shape
    blk = _full((r, c))
    return _rowcall(lambda *a: _adamw_math(*a), [w, g, m, v], [blk] * 4, [_sds((r, c), _F32)] * 3, [blk] * 3,
                    [False] * 3, grid=(1, 1), name=name)


def _adamw_many(ws, g_pack, ms, vs):
    k = len(ws)
    views, offs, off = [], [], 0
    for w in ws:
        n = w.size
        views.append((n // _LANE, _LANE) if n % _LANE == 0 else (1, n))
        offs.append(off)
        off += (n + (-n) % (8 * _LANE)) // _LANE

    def body(*refs):
        g_ref, w_refs, m_refs, v_refs = refs[0], refs[1:1 + k], refs[1 + k:1 + 2 * k], refs[1 + 2 * k:1 + 3 * k]
        outs = refs[1 + 3 * k:]
        for i in range(k):
            r, c = views[i]
            g = g_ref[offs[i]:offs[i] + r, 0:c]
            res = (g,) + _adamw_math(w_refs[i][...], g, m_refs[i][...], v_refs[i][...])
            for o_ref, val in zip(outs[4 * i:4 * i + 4], res):
                o_ref[...] = val

    args = [g_pack] + [a.reshape(views[i]) for grp in (ws, ms, vs) for i, a in enumerate(grp)]
    res = _call(body, name="adamw_small", out_shape=[_sds(views[i], _F32) for i in range(k) for _ in range(4)])(*args)
    return [[res[4 * i + j].reshape(ws[i].shape) for j in range(4)] for i in range(k)]


def _pack(arrs):
    parts = []
    for a in arrs:
        f = a.reshape(-1).astype(_F32)
        pad = (-f.shape[0]) % (8 * _LANE)
        if pad:
            f = jnp.concatenate([f, jnp.zeros((pad,), _F32)])
        parts.append(f)
    return jnp.concatenate(parts).reshape(-1, _LANE)


def kernel(x, ln_mix_g, w_in, s5_a_re, s5_a_im, s5_log_dt, s5_b_re, s5_b_im, s5_c_re, s5_c_im, s5_d, s5_w_glu, s5_b_glu, w_proj_s5, hgrn_lb_logits, hgrn_norm_g, w_proj_hgrn, w_out, ln_ffn_g, w_up, conv_w, conv_b, w_down, ln_final_g, loss_target, m_ln_mix_g, m_w_in, m_s5_a_re, m_s5_a_im, m_s5_log_dt, m_s5_b_re, m_s5_b_im, m_s5_c_re, m_s5_c_im, m_s5_d, m_s5_w_glu, m_s5_b_glu, m_w_proj_s5, m_hgrn_lb_logits, m_hgrn_norm_g, m_w_proj_hgrn, m_w_out, m_ln_ffn_g, m_w_up, m_conv_w, m_conv_b, m_w_down, m_ln_final_g, v_ln_mix_g, v_w_in, v_s5_a_re, v_s5_a_im, v_s5_log_dt, v_s5_b_re, v_s5_b_im, v_s5_c_re, v_s5_c_im, v_s5_d, v_s5_w_glu, v_s5_b_glu, v_w_proj_s5, v_hgrn_lb_logits, v_hgrn_norm_g, v_w_proj_hgrn, v_w_out, v_ln_ffn_g, v_w_up, v_conv_w, v_conv_b, v_w_down, v_ln_final_g):
    assert x.shape[0] == 1 and w_in.shape[0] == 1, "one example per device, one layer"
    t, d = x.shape[1], x.shape[2]
    w5 = s5_w_glu.shape[2]
    hw = hgrn_norm_g.shape[1]
    ng_, np_, gc = s5_b_re.shape[1], s5_b_re.shape[2], s5_b_re.shape[3]
    dff = w_down.shape[1] * _NCHIP
    assert gc * _S5_SET == _LANE and ng_ * gc == w5 and hw % _LANE == 0
    gs_off = w5 + 4 * hw
    ci = lax.axis_index("c")
    xt = x.reshape(t, d)
    tgt = loss_target.reshape(t, d)

    big_names = ["w_in", "s5_w_glu", "w_proj_s5", "w_proj_hgrn", "w_out", "w_up", "w_down"]
    big_w = dict(w_in=w_in[0], s5_w_glu=s5_w_glu[0], w_proj_s5=w_proj_s5[0], w_proj_hgrn=w_proj_hgrn[0],
                 w_out=w_out[0], w_up=w_up[0], w_down=w_down[0])
    chip = 2 * lax.axis_index("x") + lax.axis_index("y")
    sel_chip = jnp.stack([chip, ci]).astype(jnp.int32)
    slots = {k: _cast_into_slot(big_w[k], sel_chip) for k in big_names}
    g_in_part, g_cw = _run_plan(_gather_plan([slots["w_in"]], [conv_w[0]], rel=(0, 1)), "gather_w_in")
    cw = g_cw.transpose(1, 0, 2).reshape(3, 2 * dff)
    cb = conv_b

    tm_big = _tile(t, 1024, 8)

    h1 = _rms_fwd(xt, ln_mix_g, "rms1_fwd")
    nin_s = g_in_part.shape[2]
    proj, (g_in,), (g_glu, g_ps5, g_ph, g_out) = _mm_shard_order(
        h1, _gather_plan([g_in_part], [], rel=(2,)),
        _gather_plan([slots[k] for k in ("s5_w_glu", "w_proj_s5", "w_proj_hgrn", "w_out")], []),
        sel_chip, tm=tm_big, tn=_tile(nin_s, 1152), name="mm_proj")
    wglu = g_glu.reshape(w5, w5)
    wout = g_out.reshape(d, d)

    abar_r, abar_i, coef_r, coef_i = _s5_prep(s5_a_re[0], s5_a_im[0], s5_log_dt.reshape(ng_, 1))
    lanes = ng_ * np_

    def fold_coef(cr_, ci_, bre, bim):
        return cr_[..., None] * bre - ci_[..., None] * bim, cr_[..., None] * bim + ci_[..., None] * bre

    (bf_re, bf_im), fold_vjp = jax.vjp(fold_coef, coef_r, coef_i, s5_b_re[0], s5_b_im[0])
    par = (abar_r.reshape(1, lanes), abar_i.reshape(1, lanes),
           _bd_in(bf_re), _bd_in(bf_im), _bd_out(s5_c_re[0]), -_bd_out(s5_c_im[0]), s5_d.reshape(1, w5))
    (y_s5, car, cai), (g_up_part,) = _s5_fwd(proj, par, t, w5, rider=_gather_plan([slots["w_up"]], [], rel=(0, 1)))
    z = _s5glu_fwd(y_s5, wglu, s5_b_glu)
    ys = _mm(z, g_ps5, "nn", _F32, tm=tm_big, tn=g_ps5.shape[2], tk=w5, name="mm_proj_s5")

    lb = _lb_prep(hgrn_lb_logits)
    (og, st_all), (g_up,) = _hgrn_fwd(proj, lb, hgrn_norm_g, t, w5, hw, rider=_gather_plan([g_up_part], [], rel=(2,)))
    yh = _mm(og, g_ph, "nn", _F32, tm=tm_big, tn=g_ph.shape[2], tk=hw, name="mm_proj_hgrn")

    merged = _merge_fwd(proj, ys, yh, t, d, gs_off)
    x2 = _mm(merged, wout, "nn", _F32, tm=tm_big, tn=_tile(d, 1024), tk=d, res=xt, name="mm_out")
    h2 = _rms_fwd(x2, ln_ffn_g, "rms2_fwd")
    up_s = g_up.shape[2]
    w_conv = _tile(dff, 1408)
    nbh = dff // w_conv
    tn_up = _tile(up_s, 1408)
    tk_dh2 = _tile(up_s, w_conv)
    tn_gwup = _tile(up_s // 2, 1408)
    assert w_conv % tn_up == 0 and w_conv % tk_dh2 == 0 and w_conv % tn_gwup == 0
    up, (g_down,) = _mm(h2, g_up, "nn", _BF, tm=tm_big, tn=tn_up, tk=d, name="mm_up",
                        colperm=_pair_perm(nbh, w_conv // tn_up), rider=_gather_plan([slots["w_down"]], []))
    wdown = g_down.reshape(dff, d)
    act = _convact_fwd(up, cw, cb, t, dff)
    x3 = _mm(act, wdown, "nn", _F32, tm=tm_big, tn=_tile(d, 1024), tk=None, res=x2, name="mm_down")
    loss_part, dx3, dx3b, d_gfin = _loss_head(x3, tgt, ln_final_g.reshape(1, d))

    dact = _mm(dx3b, wdown, "nt", _BF, tm=tm_big, tn=_tile(dff, 1408), tk=d, name="mm_dact")
    r_down = dff // _NCHIP
    gw_down = _mm(act, dx3b, "tn", _BF, tm=_tile(r_down, 1408), tn=_tile(d // 2, 1024), tk=None, halves="rows",
                  name="mm_gw_down")
    def chip_sums(grads, name):
        theirs = _swap_halves(grads, name)
        return [_add_pairs(g, th, sel_chip) for g, th in zip(grads, theirs)]

    (s_down,) = chip_sums([gw_down], "swap_halves_d")
    dup, d_cw, d_cb = _convact_bwd(up, dact, cw, cb, t, dff)
    dh2, (r_down,) = _mm(dup, g_up, "nt", _BF, tm=tm_big, tn=_tile(d, 1024), tk=tk_dh2, name="mm_dh2",
                         colperm=_pair_perm(nbh, w_conv // tk_dh2), rider=_exchange_plan([s_down], None))
    gw_up = _mm(h2, dup, "tn", _BF, tm=_tile(d, 1024), tn=tn_gwup, tk=None, halves="cols", name="mm_gw_up",
                colperm=_pair_perm(nbh, w_conv // tn_gwup))
    dx2, dx2b, d_gffn = _rms_bwd(x2, ln_ffn_g, dh2, dx3, "rms2_bwd")
    dmerged = _mm(dx2b, wout, "nt", _BF, tm=tm_big, tn=_tile(d, 1024), tk=d, name="mm_dmerged")
    gw_out = _mm(merged, dx2b, "tn", _BF, tm=_tile(d // _NCHIP, 1024), tn=_tile(d // 2, 1024), tk=None, halves="rows",
                 name="mm_gw_out")
    dys, dyh, dgs, dgh = _merge_bwd(proj, ys, yh, dmerged, t, d, gs_off)
    ps_s = g_ps5.shape[2]
    dz = _mm(dys, g_ps5, "nt", _BF, tm=tm_big, tn=_tile(w5, 1024), tk=ps_s, name="mm_dz")
    gw_ps5 = _mm(z, dys, "tn", _BF, tm=_tile(w5, 1024), tn=ps_s // 2, tk=None, halves="cols", name="mm_gw_ps5")
    dog = _mm(dyh, g_ph, "nt", _BF, tm=tm_big, tn=_tile(hw, 1024), tk=ps_s, name="mm_dog")
    gw_ph = _mm(og, dyh, "tn", _BF, tm=_tile(hw, 1024), tn=ps_s // 2, tk=None, halves="cols", name="mm_gw_ph")
    dy_s5, gw_glu_full, d_bglu = _s5glu_bwd(y_s5, dz, wglu, s5_b_glu)
    r_glu = w5 // _NCHIP
    gw_glu = gw_glu_full.astype(_BF).reshape(_NCHIP, r_glu, 2, w5 // 2).transpose(2, 0, 1, 3)

    s_glu, s_ps5, s_ph, s_out, s_up = chip_sums([gw_glu, gw_ps5, gw_ph, gw_out, gw_up], "swap_halves_a")
    s5g, (r_glu_, r_ps5, r_ph, r_out, r_up) = _s5_bwd(
        proj, dy_s5, car, cai, par, t, w5, rider=_exchange_plan([s_glu, s_ps5, s_ph, s_out, s_up], None))
    du = s5g[0]
    dq, df, di, dg, d_lb, d_ng = _hgrn_bwd(proj, lb, hgrn_norm_g, st_all, dog, t, w5, hw)
    dproj = jnp.concatenate([du, dq, df, di, dg, dgs, dgh], axis=1)

    d_coef_r, d_coef_i, d_bre, d_bim = fold_vjp((_bd_in_grad(s5g[3], np_, gc), _bd_in_grad(s5g[4], np_, gc)))
    d_are, d_aim, d_ldt = _s5_prep_bwd(s5_a_re[0], s5_a_im[0], s5_log_dt.reshape(ng_, 1),
                                       [s5g[1].reshape(ng_, np_), s5g[2].reshape(ng_, np_), d_coef_r, d_coef_i])
    d_cre = _bd_out_grad(s5g[5], np_, gc)
    d_cim = -_bd_out_grad(s5g[6], np_, gc)
    d_logits = _lb_prep_bwd(hgrn_lb_logits, d_lb)

    small_names = ["s5_a_re", "s5_a_im", "s5_log_dt", "s5_b_re", "s5_b_im", "s5_c_re", "s5_c_im", "s5_d",
                   "s5_b_glu", "hgrn_lb_logits", "hgrn_norm_g", "ln_ffn_g", "conv_b", "ln_final_g", "ln_mix_g"]
    small_w = dict(ln_mix_g=ln_mix_g, s5_a_re=s5_a_re, s5_a_im=s5_a_im, s5_log_dt=s5_log_dt, s5_b_re=s5_b_re,
                   s5_b_im=s5_b_im, s5_c_re=s5_c_re, s5_c_im=s5_c_im, s5_d=s5_d, s5_b_glu=s5_b_glu,
                   hgrn_lb_logits=hgrn_lb_logits, hgrn_norm_g=hgrn_norm_g, ln_ffn_g=ln_ffn_g, conv_b=conv_b,
                   ln_final_g=ln_final_g)
    small_m = dict(ln_mix_g=m_ln_mix_g, s5_a_re=m_s5_a_re, s5_a_im=m_s5_a_im, s5_log_dt=m_s5_log_dt, s5_b_re=m_s5_b_re,
                   s5_b_im=m_s5_b_im, s5_c_re=m_s5_c_re, s5_c_im=m_s5_c_im, s5_d=m_s5_d, s5_b_glu=m_s5_b_glu,
                   hgrn_lb_logits=m_hgrn_lb_logits, hgrn_norm_g=m_hgrn_norm_g, ln_ffn_g=m_ln_ffn_g, conv_b=m_conv_b,
                   ln_final_g=m_ln_final_g)
    small_v = dict(ln_mix_g=v_ln_mix_g, s5_a_re=v_s5_a_re, s5_a_im=v_s5_a_im, s5_log_dt=v_s5_log_dt, s5_b_re=v_s5_b_re,
                   s5_b_im=v_s5_b_im, s5_c_re=v_s5_c_re, s5_c_im=v_s5_c_im, s5_d=v_s5_d, s5_b_glu=v_s5_b_glu,
                   hgrn_lb_logits=v_hgrn_lb_logits, hgrn_norm_g=v_hgrn_norm_g, ln_ffn_g=v_ln_ffn_g, conv_b=v_conv_b,
                   ln_final_g=v_ln_final_g)
    small_g = dict(s5_a_re=d_are, s5_a_im=d_aim, s5_log_dt=d_ldt, s5_b_re=d_bre, s5_b_im=d_bim,
                   s5_c_re=d_cre, s5_c_im=d_cim, s5_d=s5g[7], s5_b_glu=d_bglu, hgrn_lb_logits=d_logits,
                   hgrn_norm_g=d_ng, ln_ffn_g=d_gffn, conv_b=d_cb, ln_final_g=d_gfin)
    like = [small_w[k] for k in small_names]
    assert small_names[-1] == "ln_mix_g"
    pack_a = _pack([small_g[k] for k in small_names[:-1]] + [d_cw])
    gw_in, (r_small_a,) = _mm(h1, dproj, "tn", _BF, tm=_tile(d, 1024), tn=_tile(nin_s // 2, 1152), tk=None, halves="cols",
                              name="mm_gw_in", rider=_exchange_plan([], pack_a))
    (s_in,) = chip_sums([gw_in], "swap_halves_b")
    dh1, (r_in,) = _mm(dproj, g_in, "nt", _BF, tm=tm_big, tn=_tile(d, 1024), tk=None, name="mm_dh1",
                       rider=_exchange_plan([s_in], None))
    dx, _, d_gmix = _rms_bwd(xt, ln_mix_g, dh1, dx2, "rms1_bwd")
    (r_small_b,) = _run_plan(_exchange_plan([], _pack([d_gmix])), "exchange_gmix")
    sums = [s_in, s_glu, s_ps5, s_ph, s_out, s_up, s_down]
    received = [r_in, r_glu_, r_ps5, r_ph, r_out, r_up, r_down]
    halves = [_shard_sum(sm, rc, sel_chip) for sm, rc in zip(sums, received)]
    g_a = _sum_slots(r_small_a, "small_sum_a")
    g_b = _sum_slots(r_small_b, "small_sum_b")
    full = _share_halves(halves)
    w_pack = _pack(like)
    rows_a = w_pack.shape[0] - g_b.shape[0]
    g_small = jnp.concatenate([g_a[:rows_a], g_b], axis=0)
    cs = conv_w.shape[2]
    g_cw_full = g_a[rows_a:].reshape(-1)[:3 * 2 * dff].reshape(3, 2 * dff)
    g_cw = lax.dynamic_slice_in_dim(g_cw_full, chip * cs, cs, axis=1)

    big_m = dict(w_in=m_w_in, s5_w_glu=m_s5_w_glu, w_proj_s5=m_w_proj_s5, w_proj_hgrn=m_w_proj_hgrn, w_out=m_w_out,
                 w_up=m_w_up, w_down=m_w_down)
    big_v = dict(w_in=v_w_in, s5_w_glu=v_s5_w_glu, w_proj_s5=v_w_proj_s5, w_proj_hgrn=v_w_proj_hgrn, w_out=v_w_out,
                 w_up=v_w_up, w_down=v_w_down)
    res = {}
    for k, g2 in zip(big_names, full):
        w2 = big_w[k]
        shp = (1,) + w2.shape
        outs = _adamw_shard(w2, g2, big_m[k][0], big_v[k][0], "adamw_" + k)
        res[k] = [o.reshape(shp) for o in outs]
    sm_outs = _adamw_many(like, g_small, [small_m[k] for k in small_names], [small_v[k] for k in small_names])
    for k, outs in zip(small_names, sm_outs):
        res[k] = outs

    cw_outs = _adamw_whole(conv_w[0], g_cw, m_conv_w[0], v_conv_w[0], "adamw_conv_w")
    res["conv_w"] = [g_cw.reshape(conv_w.shape)] + [o.reshape(conv_w.shape) for o in cw_outs]

    loss = lax.psum(loss_part[0, 0], ("x", "y", "c"))
    order = ["ln_mix_g", "w_in", "s5_a_re", "s5_a_im", "s5_log_dt", "s5_b_re", "s5_b_im", "s5_c_re", "s5_c_im", "s5_d",
             "s5_w_glu", "s5_b_glu", "w_proj_s5", "hgrn_lb_logits", "hgrn_norm_g", "w_proj_hgrn", "w_out", "ln_ffn_g",
             "w_up", "conv_w", "conv_b", "w_down", "ln_final_g"]
    return (loss, dx.reshape(x.shape), *[res[k][0] for k in order], *[res[k][1] for k in order],
            *[res[k][2] for k in order], *[res[k][3] for k in order])
```

```python
import functools

import jax
import jax.numpy as jnp
from jax import lax
from jax.experimental import pallas as pl
from jax.experimental.pallas import tpu as pltpu

_F32 = jnp.float32
_BF = jnp.bfloat16
_RMS_EPS = 1e-6
_S5_MAX_RE = -1e-4
_LR, _B1, _B2, _ADAM_EPS, _WD, _STEP = 0.001, 0.9, 0.999, 1e-08, 0.01, 10
_MESH = pl.DeviceIdType.MESH
_ANY = pl.BlockSpec(memory_space=pl.ANY)
_LANE = 128
_VMEM_LIMIT = 56 * 1024 * 1024
_CHUNK = 64
_S5_TB = 512
_S5_SET = 8
_HGRN_HP = 8
_HGRN_SEG = 512
_NCHIP = 4
_NDEV = 8


def _call(body, **kw):
    return pl.pallas_call(body, **kw)


def _cparams(*sem):
    return pltpu.CompilerParams(dimension_semantics=sem, vmem_limit_bytes=_VMEM_LIMIT)


def _tile(n, pref, unit=_LANE):
    if n <= pref:
        return n
    t = (pref // unit) * unit
    while t >= unit:
        if n % t == 0:
            return t
        t -= unit
    raise ValueError(f"no tile for {n}")


_OPERAND_BYTES = 12 * 1024 * 1024


def _tk_fit(k, tm, tn):
    best = _LANE
    for tk in range(_LANE, k + 1, _LANE):
        if k % tk == 0 and (tm + tn) * tk * 2 <= _OPERAND_BYTES:
            best = tk
    return best if k % _LANE == 0 else k


_NN = ((1,), (0,))
_NT = ((1,), (1,))
_TN = ((0,), (0,))


def _dg(a, b, dims):
    return lax.dot_general(a.astype(_BF), b.astype(_BF), (dims, ((), ())), preferred_element_type=_F32)


@jax.custom_vjp
def _bdot(a, b):
    return _dg(a, b, _NN)


def _bdot_f(a, b):
    return _dg(a, b, _NN), (a, b)


def _bdot_b(res, g):
    a, b = res
    return _dg(g, b, _NT).astype(a.dtype), _dg(a, g, _TN).astype(b.dtype)


_bdot.defvjp(_bdot_f, _bdot_b)


@jax.custom_vjp
def _bdot_nt(a, b):
    return _dg(a, b, _NT)


def _bdot_nt_f(a, b):
    return _dg(a, b, _NT), (a, b)


def _bdot_nt_b(res, g):
    a, b = res
    return _dg(g, b, _NN).astype(a.dtype), _dg(g, a, _TN).astype(b.dtype)


_bdot_nt.defvjp(_bdot_nt_f, _bdot_nt_b)


@jax.custom_vjp
def _bdot_tn(a, b):
    return _dg(a, b, _TN)


def _bdot_tn_f(a, b):
    return _dg(a, b, _TN), (a, b)


def _bdot_tn_b(res, g):
    a, b = res
    return _dg(b, g, _NT).astype(a.dtype), _dg(a, g, _NN).astype(b.dtype)


_bdot_tn.defvjp(_bdot_tn_f, _bdot_tn_b)


_SUBLANES = 8


def _shift_up(x, n):
    r = x.shape[0]
    if n % _SUBLANES == 0:
        return jnp.concatenate([x[n:], jnp.zeros((n,) + x.shape[1:], x.dtype)], axis=0)
    row = lax.broadcasted_iota(jnp.int32, x.shape, 0)
    return jnp.where(row < r - n, pltpu.roll(x, r - n, 0), 0.0)


@functools.partial(jax.custom_vjp, nondiff_argnums=(1,))
def _shift_down(x, n):
    if n % _SUBLANES == 0:
        return jnp.concatenate([jnp.zeros((n,) + x.shape[1:], x.dtype), x[:x.shape[0] - n]], axis=0)
    row = lax.broadcasted_iota(jnp.int32, x.shape, 0)
    return jnp.where(row >= n, pltpu.roll(x, n, 0), 0.0)


def _shift_down_f(x, n):
    return _shift_down(x, n), None


def _shift_down_b(n, _, g):
    return (_shift_up(g, n),)


_shift_down.defvjp(_shift_down_f, _shift_down_b)


def _rows_apart(x):
    return tuple(x[k:k + _SUBLANES] for k in range(0, x.shape[0], _SUBLANES))


@jax.custom_vjp
def _split_rows(x):
    return _rows_apart(x)


_split_rows.defvjp(lambda x: (_rows_apart(x), None), lambda _, gs: (jnp.concatenate(gs, axis=0),))


@jax.custom_vjp
def _join_rows(pieces):
    return jnp.concatenate(pieces, axis=0)


_join_rows.defvjp(lambda pieces: (jnp.concatenate(pieces, axis=0), None), lambda _, g: (_rows_apart(g),))


@jax.custom_vjp
def _last_row(x):
    return x[_SUBLANES - 1:_SUBLANES]


def _last_row_b(_, g):
    row = lax.broadcasted_iota(jnp.int32, (_SUBLANES, g.shape[1]), 0)
    return (jnp.where(row == _SUBLANES - 1, g, 0.0),)


_last_row.defvjp(lambda x: (x[_SUBLANES - 1:_SUBLANES], None), _last_row_b)


def _sigmoid(x):
    return 1.0 / (1.0 + jnp.exp(-x))


def _silu(x):
    return x * _sigmoid(x)


def _gelu(x):
    return 0.5 * x * (1.0 + jnp.tanh(0.7978845608028654 * (x + 0.044715 * (x * x * x))))


def _rms_core(x, g):
    return x * lax.rsqrt(jnp.mean(x * x, axis=-1, keepdims=True) + _RMS_EPS) * g


def _mm(a, b, mode, out_dtype, *, tm, tn, tk, res=None, halves=None, rider=None, colperm=None, name):
    if colperm is None:
        def colperm(n_):
            return n_
    if tk is None:
        kdim = a.shape[0] if mode == "tn" else (b.shape[2] if (mode == "nt" and b.ndim == 3) else a.shape[1])
        tk = _tk_fit(kdim, tm, tn)
    if mode == "nn":
        m, k = a.shape
        a_spec = pl.BlockSpec((tm, tk), lambda i, j, kk: (i, kk))
        if b.ndim == 3:
            s, _, ns = b.shape
            n = s * ns
            npb = ns // tn
            b_spec = pl.BlockSpec((None, tk, tn), lambda i, j, kk: (j // npb, kk, j % npb))
        else:
            n = b.shape[1]
            b_spec = pl.BlockSpec((tk, tn), lambda i, j, kk: (kk, j))
        dims = _NN
    elif mode == "nt":
        m, k = a.shape
        a_spec = pl.BlockSpec((tm, tk), lambda i, j, kk: (i, colperm(kk)))
        if b.ndim == 3:
            s, n, ks = b.shape
            kpb = ks // tk
            b_spec = pl.BlockSpec((None, tn, tk), lambda i, j, kk: (kk // kpb, j, kk % kpb))
        else:
            n = b.shape[0]
            b_spec = pl.BlockSpec((tn, tk), lambda i, j, kk: (j, kk))
        dims = _NT
    else:
        k, m = a.shape
        n = b.shape[1]
        a_spec = pl.BlockSpec((tk, tm), lambda i, j, kk: (kk, i))
        b_spec = pl.BlockSpec((tk, tn), lambda i, j, kk: (kk, colperm(j)))
        dims = _TN
    nk = k // tk
    if halves is None:
        out_shape = jax.ShapeDtypeStruct((m, n), out_dtype)
        out_spec = pl.BlockSpec((tm, tn), lambda i, j, kk: (i, colperm(j) if mode == "nn" else j))
    elif halves == "cols":
        c2 = n // (2 * _NCHIP)
        tpc = c2 // tn
        out_shape = jax.ShapeDtypeStruct((2, _NCHIP, m, c2), out_dtype)
        out_spec = pl.BlockSpec((None, None, tm, tn),
                                lambda i, j, kk: ((j // tpc) % 2, j // (2 * tpc), i, j % tpc))
    else:
        c2 = n // 2
        tpc = c2 // tn
        r = m // _NCHIP
        tpr = r // tm
        out_shape = jax.ShapeDtypeStruct((2, _NCHIP, r, c2), out_dtype)
        out_spec = pl.BlockSpec((None, None, tm, tn),
                                lambda i, j, kk: (j // tpc, i // tpr, i % tpr, j % tpc))
    has_res = res is not None
    nreg = 3 if has_res else 2
    ni, nj = m // tm, n // tn
    r_ins = list(rider.ins) if rider else []
    r_outs = list(rider.outs) if rider else []

    def body(*refs):
        a_ref, b_ref = refs[0], refs[1]
        r_ref = refs[2] if has_res else None
        rin = refs[nreg:nreg + len(r_ins)]
        o_ref = refs[nreg + len(r_ins)]
        rout = refs[nreg + len(r_ins) + 1:nreg + len(r_ins) + 1 + len(r_outs)]
        acc_ref = refs[nreg + len(r_ins) + 1 + len(r_outs)]
        sems = refs[nreg + len(r_ins) + 2 + len(r_outs):]
        i, j, kk = pl.program_id(0), pl.program_id(1), pl.program_id(2)

        if rider:
            @pl.when(jnp.logical_and(jnp.logical_and(i == 0, j == 0), kk == 0))
            def _():
                rider.start(rin, rout, sems)

        @pl.when(kk == 0)
        def _():
            acc_ref[...] = jnp.zeros_like(acc_ref)

        acc_ref[...] += _dg(a_ref[...], b_ref[...], dims)

        @pl.when(kk == nk - 1)
        def _():
            out = acc_ref[...]
            if has_res:
                out = out + r_ref[...]
            o_ref[...] = out.astype(out_dtype)

        if rider:
            @pl.when(jnp.logical_and(jnp.logical_and(i == ni - 1, j == nj - 1), kk == nk - 1))
            def _():
                rider.finish(rin, rout, sems)

    in_specs = [a_spec, b_spec]
    args = [a, b]
    if has_res:
        in_specs.append(pl.BlockSpec((tm, tn), lambda i, j, kk: (i, j)))
        args.append(res)
    if not rider:
        return _call(body, name=name, grid=(ni, nj, nk), in_specs=in_specs, out_specs=out_spec,
                     out_shape=out_shape, scratch_shapes=[pltpu.VMEM((tm, tn), _F32)],
                     compiler_params=_cparams("parallel", "parallel", "arbitrary"))(*args)
    res_all = _call(body, name=name, grid=(ni, nj, nk), in_specs=in_specs + [_ANY] * len(r_ins),
                    out_specs=[out_spec] + [_ANY] * len(r_outs), out_shape=[out_shape] + r_outs,
                    input_output_aliases={nreg + k: 1 + v for k, v in rider.aliases.items()},
                    scratch_shapes=[pltpu.VMEM((tm, tn), _F32)] + list(rider.sems),
                    compiler_params=_cparams("arbitrary", "arbitrary", "arbitrary"))(*args, *r_ins)
    return res_all[0], list(res_all[1:])


def _rowcall(fn, ins, in_specs, outs, out_specs, acc, *, grid, name):
    nin = len(ins)

    def body(*refs):
        vals = fn(*[r[...] for r in refs[:nin]])
        first = pl.program_id(1) == 0
        for k, (o_ref, v) in enumerate(zip(refs[nin:], vals)):
            if acc[k]:
                @pl.when(first)
                def _(o_ref=o_ref):
                    o_ref[...] = jnp.zeros_like(o_ref)
                o_ref[...] += v.astype(o_ref.dtype)
            else:
                o_ref[...] = v.astype(o_ref.dtype)

    return _call(body, name=name, grid=grid, in_specs=in_specs, out_specs=out_specs, out_shape=outs,
                 compiler_params=_cparams("parallel", "arbitrary"))(*ins)


def _rb(tm, w, cb=0):
    return pl.BlockSpec((tm, w), lambda j, i: (i, cb + j))


def _cb(w, cb=0):
    return pl.BlockSpec((1, w), lambda j, i: (0, cb + j))


def _full(shape):
    nd = len(shape)
    return pl.BlockSpec(shape, lambda j, i: (0,) * nd)


def _sds(shape, dtype):
    return jax.ShapeDtypeStruct(shape, dtype)


def _rms_fwd(x, g, name):
    t, d = x.shape
    tm = _tile(t, 512, 8)
    return _rowcall(lambda xb, gb: (_rms_core(xb, gb),), [x, g], [_rb(tm, d), _full((1, d))],
                    [_sds((t, d), _BF)], [_rb(tm, d)], [False], grid=(1, t // tm), name=name)[0]


def _rms_bwd(x, g, dh, dres, name):
    t, d = x.shape
    tm = _tile(t, 256, 8)

    def fn(xb, gb, dhb, drb):
        _, vjp = jax.vjp(_rms_core, xb, gb)
        dx, dg = vjp(dhb.astype(_F32))
        dx = dx + drb
        return dx, dx, dg

    return _rowcall(fn, [x, g, dh, dres], [_rb(tm, d), _full((1, d)), _rb(tm, d), _rb(tm, d)],
                    [_sds((t, d), _F32), _sds((t, d), _BF), _sds((1, d), _F32)],
                    [_rb(tm, d), _rb(tm, d), _full((1, d))], [False, False, True],
                    grid=(1, t // tm), name=name)


def _loss_head(x3, tgt, g):
    t, d = x3.shape
    tm = _tile(t, 256, 8)

    def fn(xb, tb, gb):
        y, vjp = jax.vjp(_rms_core, xb, gb)
        e = y - tb
        part = 0.5 * jnp.sum(jnp.mean(e * e, axis=-1, keepdims=True), axis=0, keepdims=True)
        dx, dg = vjp(e * (1.0 / d))
        return jnp.broadcast_to(part, (1, _LANE)), dx, dx, dg

    return _rowcall(fn, [x3, tgt, g], [_rb(tm, d), _rb(tm, d), _full((1, d))],
                    [_sds((1, _LANE), _F32), _sds((t, d), _F32), _sds((t, d), _BF), _sds((1, d), _F32)],
                    [_full((1, _LANE)), _rb(tm, d), _rb(tm, d), _full((1, d))], [True, False, False, True],
                    grid=(1, t // tm), name="loss_head")


def _s5_disc(a_re, a_im, log_dt):
    lam_re = jnp.minimum(a_re, _S5_MAX_RE)
    lam_im = a_im
    dt = jnp.exp(log_dt)
    mag = jnp.exp(lam_re * dt)
    abar_re = mag * jnp.cos(lam_im * dt)
    abar_im = mag * jnp.sin(lam_im * dt)
    den = lam_re * lam_re + lam_im * lam_im
    nr = abar_re - 1.0
    ni = abar_im
    coef_re = (nr * lam_re + ni * lam_im) / den
    coef_im = (ni * lam_re - nr * lam_im) / den
    return abar_re, abar_im, coef_re, coef_im


def _s5_prep(a_re, a_im, log_dt):
    g, p = a_re.shape

    def body(ar, ai, ld, o0, o1, o2, o3):
        outs = _s5_disc(ar[...], ai[...], ld[...])
        for o, v in zip((o0, o1, o2, o3), outs):
            o[...] = v

    return _call(body, name="s5_prep", out_shape=[_sds((g, p), _F32)] * 4)(a_re, a_im, log_dt)


def _s5_prep_bwd(a_re, a_im, log_dt, cts):
    g, p = a_re.shape

    def body(ar, ai, ld, c0, c1, c2, c3, d0, d1, d2):
        _, vjp = jax.vjp(_s5_disc, ar[...], ai[...], ld[...])
        outs = vjp((c0[...], c1[...], c2[...], c3[...]))
        for o, v in zip((d0, d1, d2), outs):
            o[...] = v

    return _call(body, name="s5_prep_bwd",
                 out_shape=[_sds((g, p), _F32), _sds((g, p), _F32), _sds((g, 1), _F32)])(a_re, a_im, log_dt, *cts)


def _s5_block(u, car, cai, ar, ai, b_re, b_im, c_re, c_imn, dvec):
    bur = _bdot(u, b_re)
    bui = _bdot(u, b_im)
    shape8 = (_SUBLANES, ar.shape[1])
    pows = [(ar, ai)]
    for _ in range(2):
        pr, pi = pows[-1]
        pows.append((pr * pr - pi * pi, 2.0 * pr * pi))
    pows = [(jnp.broadcast_to(pr, shape8), jnp.broadcast_to(pi, shape8)) for pr, pi in pows]

    def scan8(xr, xi):
        for k, (pr, pi) in enumerate(pows):
            dr = _shift_down(xr, 1 << k)
            di = _shift_down(xi, 1 << k)
            xr, xi = xr + pr * dr - pi * di, xi + pr * di + pi * dr
        return xr, xi

    row8 = lax.broadcasted_iota(jnp.int32, (_SUBLANES, ar.shape[1]), 0)
    tr, ti = scan8(jnp.where(row8 == 0, ar, 0.0), jnp.where(row8 == 0, ai, 0.0))
    outs_r, outs_i = [], []
    for xr, xi in zip(_split_rows(bur), _split_rows(bui)):
        xr, xi = scan8(xr, xi)
        xr, xi = xr + tr * car - ti * cai, xi + tr * cai + ti * car
        car, cai = _last_row(xr), _last_row(xi)
        outs_r.append(xr)
        outs_i.append(xi)
    y = _bdot(_join_rows(tuple(outs_r)), c_re) + _bdot(_join_rows(tuple(outs_i)), c_imn) + dvec * u
    return y, car, cai


def _s5_specs(tb, lw, nt, rev):
    tmap = (lambda t: nt - 1 - t) if rev else (lambda t: t)
    vec = pl.BlockSpec((1, lw), lambda s, t: (0, s))
    return dict(
        u=pl.BlockSpec((tb, _LANE), lambda s, t: (tmap(t), s)),
        car=pl.BlockSpec((None, 1, lw), lambda s, t: (tmap(t), 0, s)),
        vec=vec,
        bmat=pl.BlockSpec((None, _LANE, lw), lambda s, t: (s, 0, 0)),
        cmat=pl.BlockSpec((None, lw, _LANE), lambda s, t: (s, 0, 0)),
        dvec=pl.BlockSpec((1, _LANE), lambda s, t: (0, s)),
    )


def _s5_fwd(proj, par, t, w5, rider=None):
    ar, ai, b_re, b_im, c_re, c_imn, dvec = par
    ns = w5 // _LANE
    lw = ar.shape[1] // ns
    tb = min(_S5_TB, t)
    nt = t // tb
    sp = _s5_specs(tb, lw, nt, False)

    def body(u_ref, ar_r, ai_r, bre_r, bim_r, cre_r, cim_r, d_r, y_ref, car_ref, cai_ref, s_r, s_i):
        @pl.when(pl.program_id(1) == 0)
        def _():
            s_r[...] = jnp.zeros_like(s_r)
            s_i[...] = jnp.zeros_like(s_i)

        car_ref[...] = s_r[...]
        cai_ref[...] = s_i[...]
        y, ncr, nci = _s5_block(u_ref[...], s_r[...], s_i[...], ar_r[...], ai_r[...],
                                bre_r[...], bim_r[...], cre_r[...], cim_r[...], d_r[...])
        y_ref[...] = y
        s_r[...] = ncr
        s_i[...] = nci

    kw = dict(name="s5_fwd", grid=(ns, nt),
              in_specs=[sp["u"], sp["vec"], sp["vec"], sp["bmat"], sp["bmat"],
                        sp["cmat"], sp["cmat"], sp["dvec"]],
              out_specs=[sp["u"], sp["car"], sp["car"]],
              out_shape=[_sds((t, w5), _F32), _sds((nt, 1, ns * lw), _F32), _sds((nt, 1, ns * lw), _F32)],
              scratch_shapes=[pltpu.VMEM((1, lw), _F32), pltpu.VMEM((1, lw), _F32)])
    args = (proj, ar, ai, b_re, b_im, c_re, c_imn, dvec)
    if rider is None:
        return _call(body, compiler_params=_cparams("parallel", "arbitrary"), **kw)(*args), []
    return _call_riding(body, rider, args=args, **kw)


def _s5_bwd(proj, dy, car, cai, par, t, w5, rider=None):
    ar, ai, b_re, b_im, c_re, c_imn, dvec = par
    ns = w5 // _LANE
    lw = ar.shape[1] // ns
    tb = min(_S5_TB, t)
    nt = t // tb
    sp = _s5_specs(tb, lw, nt, True)

    def body(u_ref, dy_ref, car_ref, cai_ref, ar_r, ai_r, bre_r, bim_r, cre_r, cim_r, d_r,
             du_ref, g_ar, g_ai, g_bre, g_bim, g_cre, g_cim, g_d, ds_r, ds_i):
        accs = (g_ar, g_ai, g_bre, g_bim, g_cre, g_cim, g_d)

        @pl.when(pl.program_id(1) == 0)
        def _():
            ds_r[...] = jnp.zeros_like(ds_r)
            ds_i[...] = jnp.zeros_like(ds_i)
            for o in accs:
                o[...] = jnp.zeros_like(o)

        _, vjp = jax.vjp(_s5_block, u_ref[...], car_ref[...], cai_ref[...], ar_r[...], ai_r[...],
                         bre_r[...], bim_r[...], cre_r[...], cim_r[...], d_r[...])
        grads = vjp((dy_ref[...], ds_r[...], ds_i[...]))
        du_ref[...] = grads[0].astype(_BF)
        ds_r[...] = grads[1]
        ds_i[...] = grads[2]
        for o, gval in zip(accs, grads[3:]):
            o[...] += gval

    vec_o = _sds((1, ns * lw), _F32)
    kw = dict(name="s5_bwd", grid=(ns, nt),
              in_specs=[sp["u"], sp["u"], sp["car"], sp["car"], sp["vec"], sp["vec"],
                        sp["bmat"], sp["bmat"], sp["cmat"], sp["cmat"], sp["dvec"]],
              out_specs=[sp["u"], sp["vec"], sp["vec"], sp["bmat"], sp["bmat"],
                         sp["cmat"], sp["cmat"], sp["dvec"]],
              out_shape=[_sds((t, w5), _BF), vec_o, vec_o,
                         _sds(b_re.shape, _F32), _sds(b_re.shape, _F32), _sds(c_re.shape, _F32),
                         _sds(c_re.shape, _F32), _sds((1, w5), _F32)],
              scratch_shapes=[pltpu.VMEM((1, lw), _F32), pltpu.VMEM((1, lw), _F32)])
    args = (proj, dy, car, cai, ar, ai, b_re, b_im, c_re, c_imn, dvec)
    if rider is None:
        return _call(body, compiler_params=_cparams("parallel", "arbitrary"), **kw)(*args), []
    return _call_riding(body, rider, args=args, **kw)


def _bd_in(b):
    g, p, c = b.shape
    s = g // _S5_SET
    b4 = b.reshape(s, _S5_SET, p, c).transpose(0, 1, 3, 2)
    eye = jnp.eye(_S5_SET, dtype=b.dtype)
    return (b4[:, :, :, None, :] * eye[None, :, None, :, None]).reshape(s, _S5_SET * c, _S5_SET * p)


def _bd_in_grad(d, p, c):
    s = d.shape[0]
    eye = jnp.eye(_S5_SET, dtype=d.dtype)
    d5 = d.reshape(s, _S5_SET, c, _S5_SET, p) * eye[None, :, None, :, None]
    return d5.sum(axis=3).transpose(0, 1, 3, 2).reshape(s * _S5_SET, p, c)


def _bd_out(cm):
    g, c, p = cm.shape
    s = g // _S5_SET
    c4 = cm.reshape(s, _S5_SET, c, p).transpose(0, 1, 3, 2)
    eye = jnp.eye(_S5_SET, dtype=cm.dtype)
    return (c4[:, :, :, None, :] * eye[None, :, None, :, None]).reshape(s, _S5_SET * p, _S5_SET * c)


def _bd_out_grad(d, p, c):
    s = d.shape[0]
    eye = jnp.eye(_S5_SET, dtype=d.dtype)
    d5 = d.reshape(s, _S5_SET, p, _S5_SET, c) * eye[None, :, None, :, None]
    return d5.sum(axis=3).transpose(0, 1, 3, 2).reshape(s * _S5_SET, c, p)


def _s5glu_fwd(y, wglu, bglu):
    t, w5 = y.shape
    tm = _tile(t, 256, 8)

    def fn(yb, wb, bb):
        z1 = _gelu(yb)
        a = _dg(z1, wb, _NN) + bb
        return (z1 * _sigmoid(a),)

    return _rowcall(fn, [y, wglu, bglu], [_rb(tm, w5), _full(wglu.shape), _full((1, w5))],
                    [_sds((t, w5), _BF)], [_rb(tm, w5)], [False], grid=(1, t // tm), name="s5glu_fwd")[0]


def _s5glu_bwd(y, dz, wglu, bglu):
    t, w5 = y.shape
    tm = _tile(t, 256, 8)

    def fn(yb, dzb, wb, bb):
        dzb = dzb.astype(_F32)
        z1, gelu_vjp = jax.vjp(_gelu, yb)
        sig = _sigmoid(_dg(z1, wb, _NN) + bb)
        da = dzb * z1 * sig * (1.0 - sig)
        dz1 = dzb * sig + _dg(da, wb, _NT)
        (dy,) = gelu_vjp(dz1)
        return dy, _dg(z1, da, _TN), jnp.sum(da, axis=0, keepdims=True)

    return _rowcall(fn, [y, dz, wglu, bglu], [_rb(tm, w5), _rb(tm, w5), _full(wglu.shape), _full((1, w5))],
                    [_sds((t, w5), _F32), _sds((w5, w5), _F32), _sds((1, w5), _F32)],
                    [_rb(tm, w5), _full((w5, w5)), _full((1, w5))], [False, True, True],
                    grid=(1, t // tm), name="s5glu_bwd")


_LEVELS = (6, 5, 4, 3, 2, 1)


def _tri_stack(c):
    t = jnp.arange(c, dtype=jnp.int32)[:, None]
    j = jnp.arange(c, dtype=jnp.int32)[None, :]
    low = (j <= t).astype(_F32)
    mats = [low]
    for sh in _LEVELS:
        r = ((t >> sh) << sh) + ((1 << (sh - 1)) - 1)
        mats.append(low - (j <= r).astype(_F32))
    stack = jnp.concatenate(mats, axis=0).astype(_BF)
    return stack, stack.T


def _split_dot(mat, x):
    l = x.shape[1]
    hi = x.astype(_BF)
    lo = (x - hi.astype(_F32)).astype(_BF)
    out = jnp.dot(mat, jnp.concatenate([hi, lo], axis=1), preferred_element_type=_F32)
    return out[:, :l] + out[:, l:]


@jax.custom_vjp
def _decay_sums(lf, tri, tri_t):
    c = lf.shape[0]
    out = _split_dot(tri, lf)
    return tuple(out[k * c:(k + 1) * c] for k in range(len(_LEVELS) + 1))


def _decay_sums_f(lf, tri, tri_t):
    return _decay_sums(lf, tri, tri_t), (tri, tri_t)


def _decay_sums_b(res, gs):
    tri, tri_t = res
    return _split_dot(tri_t, jnp.concatenate(gs, axis=0)), jnp.zeros_like(tri), jnp.zeros_like(tri_t)


_decay_sums.defvjp(_decay_sums_f, _decay_sums_b)


def _hgrn_chunk(qi, fi, vi, gi, st, lb, ng, tri, tri_t):
    c = qi.shape[0]
    row = lax.broadcasted_iota(jnp.int32, (c, 1), 0)
    q = _silu(qi)
    lf = jnp.log(lb + (1.0 - lb) * _sigmoid(fi))
    k = (1.0 - lb) * _sigmoid(-fi)
    sums = _decay_sums(lf, tri, tri_t)
    b = sums[0]
    btot = jnp.sum(lf, axis=0, keepdims=True)
    inter = _bdot_nt(q * jnp.exp(b), st)
    tt = lax.broadcasted_iota(jnp.int32, (c, c), 0)
    ss = lax.broadcasted_iota(jnp.int32, (c, c), 1)
    sc = jnp.where(tt == ss, jnp.sum(q * k, axis=1, keepdims=True), 0.0)
    for sh, p in zip(_LEVELS, sums[1:]):
        upper = ((row >> (sh - 1)) & 1) == 1
        qm = jnp.where(upper, q * jnp.exp(jnp.where(upper, p, 0.0)), 0.0)
        km = jnp.where(upper, 0.0, k * jnp.exp(jnp.where(upper, 0.0, -p)))
        sc = sc + jnp.where((tt >> sh) == (ss >> sh), _bdot_nt(qm, km), 0.0)
    o = inter + _bdot(sc, vi)
    st_new = st * jnp.exp(btot) + _bdot_tn(vi, k * jnp.exp(btot - b))
    on = o * lax.rsqrt(jnp.mean(o * o, axis=1, keepdims=True) + _RMS_EPS) * ng
    return on * _silu(gi), st_new


def _hgrn_geom(t, w5, hw):
    hp = _HGRN_HP if (hw // _LANE) % _HGRN_HP == 0 and w5 % (_LANE * _HGRN_HP) == 0 else 1
    seg = min(t, _HGRN_SEG)
    return hp, hp * _LANE, seg, t // seg


def _hgrn_in_specs(seg, wd, w5, hw, tmap):
    nhp = hw // wd
    qb = w5 // wd
    return [pl.BlockSpec((seg, wd), (lambda h, s, k=k: (tmap(s), qb + k * nhp + h))) for k in range(4)]


def _hgrn_fwd(proj, lb, ng, t, w5, hw, rider=None):
    assert _CHUNK == 64
    hp, wd, seg, nseg = _hgrn_geom(t, w5, hw)
    ncs = seg // _CHUNK
    vec = pl.BlockSpec((1, wd), lambda h, s: (0, h))

    tri, tri_t = _tri_stack(_CHUNK)

    def body(q_ref, f_ref, i_ref, g_ref, lb_ref, ng_ref, tri_ref, trit_ref, og_ref, st_ref, s_scr):
        @pl.when(pl.program_id(1) == 0)
        def _():
            s_scr[...] = jnp.zeros_like(s_scr)

        tri_v, trit_v = tri_ref[...], trit_ref[...]

        def step(ci, carry):
            r = pl.ds(pl.multiple_of(ci * _CHUNK, _CHUNK), _CHUNK)
            for hh in range(hp):
                ln = slice(hh * _LANE, (hh + 1) * _LANE)
                st_ref[hh, ci] = s_scr[hh]
                og, sn = _hgrn_chunk(q_ref[r, ln], f_ref[r, ln], i_ref[r, ln], g_ref[r, ln], s_scr[hh],
                                     lb_ref[:, ln], ng_ref[:, ln], tri_v, trit_v)
                og_ref[r, ln] = og.astype(_BF)
                s_scr[hh] = sn
            return carry

        lax.fori_loop(0, ncs, step, 0)

    kw = dict(name="hgrn_fwd", grid=(hw // wd, nseg),
              in_specs=_hgrn_in_specs(seg, wd, w5, hw, lambda s: s) + [
                  vec, vec, pl.BlockSpec(tri.shape, lambda h, s: (0, 0)), pl.BlockSpec(tri_t.shape, lambda h, s: (0, 0))],
              out_specs=[pl.BlockSpec((seg, wd), lambda h, s: (s, h)),
                         pl.BlockSpec((hp, ncs, _LANE, _LANE), lambda h, s: (h, s, 0, 0))],
              out_shape=[_sds((t, hw), _BF), _sds((hw // _LANE, t // _CHUNK, _LANE, _LANE), _F32)],
              scratch_shapes=[pltpu.VMEM((hp, _LANE, _LANE), _F32)])
    args = (proj, proj, proj, proj, lb, ng, tri, tri_t)
    if rider is None:
        return _call(body, compiler_params=_cparams("parallel", "arbitrary"), **kw)(*args), []
    return _call_riding(body, rider, args=args, **kw)


def _hgrn_bwd(proj, lb, ng, st_all, dog, t, w5, hw):
    hp, wd, seg, nseg = _hgrn_geom(t, w5, hw)
    ncs = seg // _CHUNK

    def rev(s):
        return nseg - 1 - s

    vec = pl.BlockSpec((1, wd), lambda h, s: (0, h))
    col = pl.BlockSpec((seg, wd), lambda h, s: (rev(s), h))

    tri, tri_t = _tri_stack(_CHUNK)

    def body(q_ref, f_ref, i_ref, g_ref, lb_ref, ng_ref, tri_ref, trit_ref, st_ref, dog_ref,
             dq_ref, df_ref, di_ref, dg_ref, dlb_ref, dng_ref, ds_scr):
        @pl.when(pl.program_id(1) == 0)
        def _():
            ds_scr[...] = jnp.zeros_like(ds_scr)
            dlb_ref[...] = jnp.zeros_like(dlb_ref)
            dng_ref[...] = jnp.zeros_like(dng_ref)

        tri_v, trit_v = tri_ref[...], trit_ref[...]

        def step(kk, carry):
            ci = ncs - 1 - kk
            r = pl.ds(pl.multiple_of(ci * _CHUNK, _CHUNK), _CHUNK)
            for hh in range(hp):
                ln = slice(hh * _LANE, (hh + 1) * _LANE)
                _, vjp = jax.vjp(_hgrn_chunk, q_ref[r, ln], f_ref[r, ln], i_ref[r, ln], g_ref[r, ln], st_ref[hh, ci],
                                 lb_ref[:, ln], ng_ref[:, ln], tri_v, trit_v)
                dq, df, di, dg, ds, dlb, dng = vjp((dog_ref[r, ln].astype(_F32), ds_scr[hh]))[:7]
                dq_ref[r, ln] = dq.astype(_BF)
                df_ref[r, ln] = df.astype(_BF)
                di_ref[r, ln] = di.astype(_BF)
                dg_ref[r, ln] = dg.astype(_BF)
                ds_scr[hh] = ds
                dlb_ref[:, ln] += dlb
                dng_ref[:, ln] += dng
            return carry

        lax.fori_loop(0, ncs, step, 0)

    return _call(body, name="hgrn_bwd", grid=(hw // wd, nseg),
                 in_specs=_hgrn_in_specs(seg, wd, w5, hw, rev) + [
                     vec, vec, pl.BlockSpec(tri.shape, lambda h, s: (0, 0)), pl.BlockSpec(tri_t.shape, lambda h, s: (0, 0)),
                     pl.BlockSpec((hp, ncs, _LANE, _LANE), lambda h, s: (h, rev(s), 0, 0)), col],
                 out_specs=[col, col, col, col, vec, vec],
                 out_shape=[_sds((t, hw), _BF)] * 4 + [_sds((1, hw), _F32)] * 2,
                 scratch_shapes=[pltpu.VMEM((hp, _LANE, _LANE), _F32)],
                 compiler_params=_cparams("parallel", "arbitrary"))(
                     proj, proj, proj, proj, lb, ng, tri, tri_t, st_all, dog)


def _lb_of(logits):
    mx = jnp.max(logits, axis=0, keepdims=True)
    e = jnp.exp(logits - mx)
    sm = e / jnp.sum(e, axis=0, keepdims=True)
    row = lax.broadcasted_iota(jnp.int32, logits.shape, 0)
    return jnp.sum(jnp.where(row == 0, sm, 0.0), axis=0, keepdims=True)


def _lb_prep(logits):
    def body(l_ref, o_ref):
        o_ref[...] = _lb_of(l_ref[...])

    return _call(body, name="lb_prep", out_shape=_sds((1, logits.shape[1]), _F32))(logits)


def _lb_prep_bwd(logits, dlb):
    def body(l_ref, d_ref, o_ref):
        _, vjp = jax.vjp(_lb_of, l_ref[...])
        o_ref[...] = vjp(d_ref[...])[0]

    return _call(body, name="lb_prep_bwd", out_shape=_sds(logits.shape, _F32))(logits, dlb)


def _merge_fwd(proj, ys, yh, t, d, gs_off):
    tm = _tile(t, 256, 8)
    w = _tile(d, 1024)
    nb = d // w

    def fn(gs, gh, a, b):
        return (_sigmoid(gs) * a + _sigmoid(gh) * b,)

    return _rowcall(fn, [proj, proj, ys, yh], [_rb(tm, w, gs_off // w), _rb(tm, w, gs_off // w + nb), _rb(tm, w), _rb(tm, w)],
                    [_sds((t, d), _BF)], [_rb(tm, w)], [False], grid=(nb, t // tm), name="merge_fwd")[0]


def _merge_bwd(proj, ys, yh, dm, t, d, gs_off):
    tm = _tile(t, 256, 8)
    w = _tile(d, 1024)
    nb = d // w

    def fn(gs, gh, a, b, g):
        g = g.astype(_F32)
        s1 = _sigmoid(gs)
        s2 = _sigmoid(gh)
        return g * s1, g * s2, g * a * s1 * (1.0 - s1), g * b * s2 * (1.0 - s2)

    return _rowcall(fn, [proj, proj, ys, yh, dm],
                    [_rb(tm, w, gs_off // w), _rb(tm, w, gs_off // w + nb), _rb(tm, w), _rb(tm, w), _rb(tm, w)],
                    [_sds((t, d), _BF)] * 4, [_rb(tm, w)] * 4, [False] * 4, grid=(nb, t // tm), name="merge_bwd")


_HALO = 16


def _prev_rows(up_prev, is_first):
    p1 = jnp.where(is_first, 0.0, up_prev[_HALO - 1:_HALO, :])
    p2 = jnp.where(is_first, 0.0, up_prev[_HALO - 2:_HALO - 1, :])
    return p1, p2


def _causal_taps(cur, p1, p2):
    row = lax.broadcasted_iota(jnp.int32, cur.shape, 0)
    s1 = jnp.where(row == 0, p1, _shift_down(cur, 1))
    s2 = jnp.where(row == 0, p2, jnp.where(row == 1, p1, _shift_down(cur, 2)))
    return s1, s2


def _pair_perm(nbh, per):
    def perm(n_):
        big = n_ // per
        return (2 * (big % nbh) + big // nbh) * per + n_ % per
    return perm


def _conv_specs(tm, w, nb_half, t):
    r8 = tm // _HALO
    cur_g = pl.BlockSpec((tm, w), lambda j, i: (i, 2 * j))
    cur_v = pl.BlockSpec((tm, w), lambda j, i: (i, 2 * j + 1))
    prev_g = pl.BlockSpec((_HALO, w), lambda j, i: (jnp.maximum(i * r8 - 1, 0), 2 * j))
    prev_v = pl.BlockSpec((_HALO, w), lambda j, i: (jnp.maximum(i * r8 - 1, 0), 2 * j + 1))
    w_g = pl.BlockSpec((3, w), lambda j, i: (0, j))
    w_v = pl.BlockSpec((3, w), lambda j, i: (0, nb_half + j))
    b_g = pl.BlockSpec((1, w), lambda j, i: (0, j))
    b_v = pl.BlockSpec((1, w), lambda j, i: (0, nb_half + j))
    return cur_g, cur_v, prev_g, prev_v, w_g, w_v, b_g, b_v


def _conv_of(cur, prev8, wt, bias, is_first):
    cur, prev8 = cur.astype(_F32), prev8.astype(_F32)
    p1, p2 = _prev_rows(prev8, is_first)
    s1, s2 = _causal_taps(cur, p1, p2)
    return bias + wt[0:1, :] * s2 + wt[1:2, :] * s1 + wt[2:3, :] * cur


def _convact_fwd(up, cw, cb, t, dff):
    tm = _tile(t, 512, _HALO)
    w = _tile(dff, 1408)
    nbh = dff // w
    sp = _conv_specs(tm, w, nbh, t)

    def body(ug, uv, pg, pv, wg, wv, bg, bv, o_ref):
        first = pl.program_id(1) == 0
        gate = _conv_of(ug[...], pg[...], wg[...], bg[...], first)
        val = _conv_of(uv[...], pv[...], wv[...], bv[...], first)
        o_ref[...] = (_silu(gate) * val).astype(_BF)

    return _call(body, name="convact_fwd", grid=(nbh, t // tm), in_specs=list(sp),
                 out_specs=pl.BlockSpec((tm, w), lambda j, i: (i, j)), out_shape=_sds((t, dff), _BF),
                 compiler_params=_cparams("parallel", "arbitrary"))(up, up, up, up, cw, cw, cb, cb)


def _convact_bwd(up, dact, cw, cb, t, dff):
    tm = _tile(t, 256, _HALO)
    w = _tile(dff, 1408)
    nbh = dff // w
    r8 = tm // _HALO
    nt = t // tm
    last8 = t // _HALO - 1

    def triple(off):
        return [pl.BlockSpec((tm, w), lambda j, i: (i, 2 * j + off)),
                pl.BlockSpec((_HALO, w), lambda j, i: (jnp.minimum((i + 1) * r8, last8), 2 * j + off)),
                pl.BlockSpec((_HALO, w), lambda j, i: (jnp.maximum(i * r8 - 1, 0), 2 * j + off))]

    def body(ug, ugn, ugp, uv, uvn, uvp, wg_ref, wv_ref, bg_ref, bv_ref, da_ref, dan_ref, du_ref, dw_ref, db_ref):
        i = pl.program_id(1)
        first = i == 0
        is_last = i == nt - 1
        gate = _conv_of(jnp.concatenate([ug[...], ugn[...]], axis=0), ugp[...], wg_ref[...], bg_ref[...], first)
        val = _conv_of(jnp.concatenate([uv[...], uvn[...]], axis=0), uvp[...], wv_ref[...], bv_ref[...], first)
        da = jnp.concatenate([da_ref[...], dan_ref[...]], axis=0).astype(_F32)
        row = lax.broadcasted_iota(jnp.int32, da.shape, 0)
        da = jnp.where(jnp.logical_and(row >= tm, is_last), 0.0, da)
        sg = _sigmoid(gate)
        halves = ((da * val * sg * (1.0 + gate * (1.0 - sg)), wg_ref, ug, ugp),
                  (da * gate * sg, wv_ref, uv, uvp))

        @pl.when(first)
        def _():
            dw_ref[...] = jnp.zeros_like(dw_ref)
            db_ref[...] = jnp.zeros_like(db_ref)

        for h, (dc, w_ref, u_ref, p_ref) in enumerate(halves):
            ln = slice(h * w, (h + 1) * w)
            wt = w_ref[...]
            du = wt[2:3, :] * dc + wt[1:2, :] * _shift_up(dc, 1) + wt[0:1, :] * _shift_up(dc, 2)
            du_ref[:, ln] = du[0:tm, :].astype(_BF)
            dcm = dc[0:tm, :]
            cur = u_ref[...].astype(_F32)
            p1, p2 = _prev_rows(p_ref[...].astype(_F32), first)
            s1, s2 = _causal_taps(cur, p1, p2)
            dw_ref[0:1, ln] += jnp.sum(dcm * s2, axis=0, keepdims=True)
            dw_ref[1:2, ln] += jnp.sum(dcm * s1, axis=0, keepdims=True)
            dw_ref[2:3, ln] += jnp.sum(dcm * cur, axis=0, keepdims=True)
            db_ref[:, ln] += jnp.sum(dcm, axis=0, keepdims=True)

    in_specs = (triple(0) + triple(1)
                + [pl.BlockSpec((3, w), lambda j, i: (0, j)), pl.BlockSpec((3, w), lambda j, i: (0, nbh + j)),
                   pl.BlockSpec((1, w), lambda j, i: (0, j)), pl.BlockSpec((1, w), lambda j, i: (0, nbh + j)),
                   pl.BlockSpec((tm, w), lambda j, i: (i, j)),
                   pl.BlockSpec((_HALO, w), lambda j, i: (jnp.minimum((i + 1) * r8, last8), j))])
    dup, dw_p, db_p = _call(
        body, name="convact_bwd", grid=(nbh, nt), in_specs=in_specs,
        out_specs=[pl.BlockSpec((tm, 2 * w), lambda j, i: (i, j)), pl.BlockSpec((3, 2 * w), lambda j, i: (0, j)),
                   pl.BlockSpec((1, 2 * w), lambda j, i: (0, j))],
        out_shape=[_sds((t, 2 * dff), _BF), _sds((3, 2 * dff), _F32), _sds((1, 2 * dff), _F32)],
        compiler_params=_cparams("parallel", "arbitrary"))(up, up, up, up, up, up, cw, cw, cb, cb, dact, dact)

    def natural(v):
        k = v.shape[0]
        return v.reshape(k, nbh, 2, w).transpose(0, 2, 1, 3).reshape(k, 2 * dff)

    return dup, natural(dw_p), natural(db_p)


def _me():
    return lax.axis_index("x"), lax.axis_index("y"), lax.axis_index("c")


def _other_chips(x, y):
    return [(1 - x, y), (x, 1 - y), (1 - x, 1 - y)]


def _rcopy(src, dst, ssem, rsem, dev):
    return pltpu.make_async_remote_copy(src_ref=src, dst_ref=dst, send_sem=ssem, recv_sem=rsem,
                                        device_id=dev, device_id_type=_MESH)


def _cast_into_slot(w, sel):
    r, c = w.shape
    tm = _tile(r, 256, 16)

    def body(sel_ref, w_ref, o_ref):
        o_ref[...] = w_ref[...].astype(_BF)

    gs = pltpu.PrefetchScalarGridSpec(
        num_scalar_prefetch=1, grid=(r // tm,),
        in_specs=[pl.BlockSpec((tm, c), lambda i, s: (i, 0))],
        out_specs=pl.BlockSpec((None, tm, c), lambda i, s: (s[0], i, 0)))
    return _call(body, name="cast_into_slot", grid_spec=gs, out_shape=_sds((_NCHIP, r, c), _BF),
                 compiler_params=_cparams("parallel"))(sel, w)


class _Plan:
    def __init__(self, ins, outs, aliases, sems, start, finish):
        self.ins, self.outs, self.aliases, self.sems, self.start, self.finish = ins, outs, aliases, sems, start, finish


def _run_plan(plan, name):
    ni, no = len(plan.ins), len(plan.outs)

    def body(*refs):
        rin, rout, sems = refs[:ni], refs[ni:ni + no], refs[ni + no:]
        plan.start(rin, rout, sems)
        plan.finish(rin, rout, sems)

    return _call(body, name=name, in_specs=[_ANY] * ni, out_specs=[_ANY] * no, out_shape=list(plan.outs),
                 input_output_aliases=dict(plan.aliases), scratch_shapes=list(plan.sems))(*plan.ins)


def _call_riding(body, rider, *, name, grid, in_specs, out_specs, out_shape, scratch_shapes, args):
    n_in, n_out, n_scr = len(in_specs), len(out_specs), len(scratch_shapes)
    n_rin, n_rout = len(rider.ins), len(rider.outs)

    def wrapped(*refs):
        ins, rin = refs[:n_in], refs[n_in:n_in + n_rin]
        o0 = n_in + n_rin
        outs, rout = refs[o0:o0 + n_out], refs[o0 + n_out:o0 + n_out + n_rout]
        s0 = o0 + n_out + n_rout
        scratch, sems = refs[s0:s0 + n_scr], refs[s0 + n_scr:]
        first = functools.reduce(jnp.logical_and, [pl.program_id(k) == 0 for k in range(len(grid))])
        last = functools.reduce(jnp.logical_and, [pl.program_id(k) == grid[k] - 1 for k in range(len(grid))])

        @pl.when(first)
        def _():
            rider.start(rin, rout, sems)

        body(*ins, *outs, *scratch)

        @pl.when(last)
        def _():
            rider.finish(rin, rout, sems)

    res = _call(wrapped, name=name, grid=grid, in_specs=list(in_specs) + [_ANY] * n_rin,
                out_specs=list(out_specs) + [_ANY] * n_rout, out_shape=list(out_shape) + list(rider.outs),
                input_output_aliases={n_in + k: n_out + v for k, v in rider.aliases.items()},
                scratch_shapes=list(scratch_shapes) + list(rider.sems),
                compiler_params=_cparams(*(["arbitrary"] * len(grid))))(*args, *rider.ins)
    return list(res[:n_out]), list(res[n_out:])


def _gather_plan(bufs, direct, rel=(0, 1, 2)):
    n, nd = len(bufs), len(direct)

    def where():
        x, y, c = _me()
        return c, 2 * x + y, _other_chips(x, y), (x, y, 1 - c)

    def picked(chips):
        return [(j, chips[j]) for j in rel]

    def piece(outs, a, chip, h):
        r2 = bufs[a].shape[1] // 2
        return outs[a].at[chip, pl.ds(h * r2, r2)]

    def send(outs, sems, a, j, me, c, chip):
        return _rcopy(piece(outs, a, me, c), piece(outs, a, me, c), sems[0].at[3 * a + j], sems[1].at[3 * a + j],
                      (chip[0], chip[1], c))

    def forward(outs, sems, a, j, pc, c, sib):
        return _rcopy(piece(outs, a, pc, c), piece(outs, a, pc, c), sems[2].at[3 * a + j], sems[3].at[3 * a + j], sib)

    def dsend(dins, douts, sems, a, j, me, c, chip):
        return _rcopy(dins[a], douts[a].at[me], sems[4].at[3 * a + j], sems[5].at[3 * a + j], (chip[0], chip[1], c))

    def start(rin, rout, sems):
        outs, dins, douts = rout[:n], rin[n:], rout[n:]
        c, me, chips, _ = where()
        for a in range(n):
            for j, chip in picked(chips):
                send(outs, sems, a, j, me, c, chip).start()
        for a in range(nd):
            pltpu.make_async_copy(dins[a], douts[a].at[me], sems[6].at[a]).start()
            for j, chip in enumerate(chips):
                dsend(dins, douts, sems, a, j, me, c, chip).start()

    def finish(rin, rout, sems):
        outs, dins, douts = rout[:n], rin[n:], rout[n:]
        c, me, chips, sib = where()
        for a in range(n):
            for j, (cx, cy) in picked(chips):
                pc = 2 * cx + cy
                _rcopy(piece(outs, a, me, c), piece(outs, a, pc, c), sems[0].at[3 * a + j], sems[1].at[3 * a + j],
                       (cx, cy, c)).wait_recv()
                forward(outs, sems, a, j, pc, c, sib).start()
        for a in range(n):
            for j, (cx, cy) in picked(chips):
                pc = 2 * cx + cy
                _rcopy(piece(outs, a, pc, 1 - c), piece(outs, a, pc, 1 - c), sems[2].at[3 * a + j],
                       sems[3].at[3 * a + j], sib).wait_recv()
        for a in range(nd):
            for j, (cx, cy) in enumerate(chips):
                _rcopy(dins[a], douts[a].at[2 * cx + cy], sems[4].at[3 * a + j], sems[5].at[3 * a + j],
                       (cx, cy, c)).wait_recv()
        for a in range(n):
            for j, (cx, cy) in picked(chips):
                send(outs, sems, a, j, me, c, (cx, cy)).wait_send()
                forward(outs, sems, a, j, 2 * cx + cy, c, sib).wait_send()
        for a in range(nd):
            pltpu.make_async_copy(dins[a], douts[a].at[me], sems[6].at[a]).wait()
            for j, chip in enumerate(chips):
                dsend(dins, douts, sems, a, j, me, c, chip).wait_send()

    def arrive(rin, rout, sems, j):
        outs = rout[:n]
        c, me, chips, sib = where()
        cx, cy = chips[j]
        pc = 2 * cx + cy
        for a in range(n):
            _rcopy(piece(outs, a, me, c), piece(outs, a, pc, c), sems[0].at[3 * a + j], sems[1].at[3 * a + j],
                   (cx, cy, c)).wait_recv()
            forward(outs, sems, a, j, pc, c, sib).start()
        for a in range(n):
            _rcopy(piece(outs, a, pc, 1 - c), piece(outs, a, pc, 1 - c), sems[2].at[3 * a + j],
                   sems[3].at[3 * a + j], sib).wait_recv()

    def drain(rin, rout, sems):
        outs, dins, douts = rout[:n], rin[n:], rout[n:]
        c, me, chips, sib = where()
        for a in range(nd):
            for j, (cx, cy) in enumerate(chips):
                _rcopy(dins[a], douts[a].at[2 * cx + cy], sems[4].at[3 * a + j], sems[5].at[3 * a + j],
                       (cx, cy, c)).wait_recv()
        for a in range(n):
            for j, (cx, cy) in picked(chips):
                send(outs, sems, a, j, me, c, (cx, cy)).wait_send()
                forward(outs, sems, a, j, 2 * cx + cy, c, sib).wait_send()
        for a in range(nd):
            pltpu.make_async_copy(dins[a], douts[a].at[me], sems[6].at[a]).wait()
            for j, chip in enumerate(chips):
                dsend(dins, douts, sems, a, j, me, c, chip).wait_send()

    dma = pltpu.SemaphoreType.DMA
    plan = _Plan(list(bufs) + list(direct),
                 [_sds(b.shape, b.dtype) for b in bufs] + [_sds((_NCHIP,) + s.shape, s.dtype) for s in direct],
                 {a: a for a in range(n)},
                 [dma((3 * max(n, 1),)), dma((3 * max(n, 1),)), dma((3 * max(n, 1),)), dma((3 * max(n, 1),)),
                  dma((3 * max(nd, 1),)), dma((3 * max(nd, 1),)), dma((max(nd, 1),))], start, finish)
    plan.arrive, plan.drain = arrive, drain
    return plan


def _mm_shard_order(a, mid_plan, end_plan, sel, *, tm, tn, name):
    b = mid_plan.ins[0]
    s, k, ns = b.shape
    m = a.shape[0]
    npb = ns // tn
    nj, ni = s * npb, m // tm
    n_in = (len(mid_plan.ins), len(end_plan.ins))
    n_out = (len(mid_plan.outs), len(end_plan.outs))
    n_sem = (len(mid_plan.sems), len(end_plan.sems))

    def shard_of(jj, q):
        return jnp.bitwise_xor(q[0], jj // npb)

    def body(q_ref, a_ref, *rest):
        rin = (rest[:n_in[0]], rest[n_in[0]:sum(n_in)])
        o_ref = rest[sum(n_in)]
        o0 = sum(n_in) + 1
        rout = (rest[o0:o0 + n_out[0]], rest[o0 + n_out[0]:o0 + sum(n_out)])
        vbuf, bsem = rest[o0 + sum(n_out)], rest[o0 + sum(n_out) + 1]
        s0 = o0 + sum(n_out) + 2
        sems = (rest[s0:s0 + n_sem[0]], rest[s0 + n_sem[0]:s0 + sum(n_sem)])
        b_ref = rout[0][0]
        j, i = pl.program_id(0), pl.program_id(1)

        def tile_copy(jj, slot):
            return pltpu.make_async_copy(b_ref.at[shard_of(jj, q_ref), :, pl.ds((jj % npb) * tn, tn)],
                                         vbuf.at[slot], bsem.at[slot])

        @pl.when(jnp.logical_and(j == 0, i == 0))
        def _():
            mid_plan.start(rin[0], rout[0], sems[0])
            end_plan.start(rin[1], rout[1], sems[1])
            tile_copy(0, 0).start()

        @pl.when(i == 0)
        def _():
            for kk, rel_j in ((1, 1), (2, 0), (3, 2)):
                @pl.when(j == kk * npb - 1)
                def _(rel_j=rel_j):
                    mid_plan.arrive(rin[0], rout[0], sems[0], rel_j)

            @pl.when(j + 1 < nj)
            def _():
                tile_copy(j + 1, (j + 1) % 2).start()

            tile_copy(j, j % 2).wait()

        o_ref[...] = _dg(a_ref[...], vbuf[j % 2], _NN)

        @pl.when(jnp.logical_and(j == nj - 1, i == ni - 1))
        def _():
            mid_plan.drain(rin[0], rout[0], sems[0])
            end_plan.finish(rin[1], rout[1], sems[1])

    assert s == _NCHIP and nj >= 2
    n_rin, n_rout = sum(n_in), sum(n_out)
    gs = pltpu.PrefetchScalarGridSpec(
        num_scalar_prefetch=1, grid=(nj, ni),
        in_specs=[pl.BlockSpec((tm, k), lambda j, i, q: (i, 0))] + [_ANY] * n_rin,
        out_specs=[pl.BlockSpec((tm, tn), lambda j, i, q: (i, shard_of(j, q) * npb + j % npb))] + [_ANY] * n_rout,
        scratch_shapes=[pltpu.VMEM((2, k, tn), _BF), pltpu.SemaphoreType.DMA((2,))] + list(mid_plan.sems) + list(end_plan.sems))
    aliases = {2 + kk: 1 + v for kk, v in mid_plan.aliases.items()}
    aliases.update({2 + n_in[0] + kk: 1 + n_out[0] + v for kk, v in end_plan.aliases.items()})
    res = _call(body, name=name, grid_spec=gs, out_shape=[_sds((m, s * ns), _F32)] + list(mid_plan.outs) + list(end_plan.outs),
                input_output_aliases=aliases,
                compiler_params=_cparams("arbitrary", "arbitrary"))(sel, a, *mid_plan.ins, *end_plan.ins)
    return res[0], list(res[1:1 + n_out[0]]), list(res[1 + n_out[0]:])


def _swap_halves(grads, name):
    n = len(grads)

    def body(*refs):
        ins, outs = refs[:n], refs[n:2 * n]
        ssem, rsem = refs[2 * n:]
        x, y, c = _me()
        sib = (x, y, 1 - c)
        cps = []
        for a in range(n):
            cp = _rcopy(ins[a].at[1 - c], outs[a], ssem.at[a], rsem.at[a], sib)
            cp.start()
            cps.append(cp)
        for cp in cps:
            cp.wait_recv()
        for cp in cps:
            cp.wait_send()

    dma = pltpu.SemaphoreType.DMA
    return _call(body, name=name, in_specs=[_ANY] * n, out_specs=[_ANY] * n,
                 out_shape=[_sds(g.shape[1:], g.dtype) for g in grads],
                 scratch_shapes=[dma((n,)), dma((n,))])(*grads)


def _add_pairs(grads, theirs, sel):
    _, s, r, c2 = grads.shape
    a3 = grads.reshape(2, s * r, c2)
    b2 = theirs.reshape(s * r, c2)
    tm = _tile(s * r, 512, 16)

    def body(sel_ref, a_ref, b_ref, o_ref):
        o_ref[...] = (a_ref[...].astype(_F32) + b_ref[...].astype(_F32)).astype(_BF)

    gs = pltpu.PrefetchScalarGridSpec(
        num_scalar_prefetch=1, grid=(s * r // tm,),
        in_specs=[pl.BlockSpec((None, tm, c2), lambda i, q: (q[1], i, 0)), pl.BlockSpec((tm, c2), lambda i, q: (i, 0))],
        out_specs=pl.BlockSpec((tm, c2), lambda i, q: (i, 0)))
    out = _call(body, name="chip_sum", grid_spec=gs, out_shape=_sds((s * r, c2), _BF),
                compiler_params=_cparams("parallel"))(sel, a3, b2)
    return out.reshape(s, r, c2)


def _exchange_plan(sums, small):
    n = len(sums)
    has_small = small is not None

    def where():
        x, y, c = _me()
        peers = [(1 - x if k & 4 else x, 1 - y if k & 2 else y, 1 - c if k & 1 else c) for k in range(1, _NDEV)]
        return c, 4 * x + 2 * y + c, _other_chips(x, y), peers

    def send(rin, rout, sems, a, j, c, chip):
        return _rcopy(rin[a].at[2 * chip[0] + chip[1]], rout[a].at[j], sems[0].at[3 * a + j], sems[1].at[3 * a + j],
                      (chip[0], chip[1], c))

    def small_send(rin, rout, sems, k, dev, peer):
        return _rcopy(rin[n], rout[n].at[dev], sems[2].at[k], sems[3].at[k], peer)

    def start(rin, rout, sems):
        c, dev, chips, peers = where()
        for a in range(n):
            for j, chip in enumerate(chips):
                send(rin, rout, sems, a, j, c, chip).start()
        if has_small:
            pltpu.make_async_copy(rin[n], rout[n].at[dev], sems[4].at[0]).start()
            for k, peer in enumerate(peers):
                small_send(rin, rout, sems, k, dev, peer).start()

    def finish(rin, rout, sems):
        c, dev, chips, peers = where()
        for a in range(n):
            for j, chip in enumerate(chips):
                send(rin, rout, sems, a, j, c, chip).wait_recv()
        if has_small:
            for k, (px, py, pc_) in enumerate(peers):
                _rcopy(rin[n], rout[n].at[4 * px + 2 * py + pc_], sems[2].at[k], sems[3].at[k], (px, py, pc_)).wait_recv()
        for a in range(n):
            for j, chip in enumerate(chips):
                send(rin, rout, sems, a, j, c, chip).wait_send()
        if has_small:
            pltpu.make_async_copy(rin[n], rout[n].at[dev], sems[4].at[0]).wait()
            for k, peer in enumerate(peers):
                small_send(rin, rout, sems, k, dev, peer).wait_send()

    dma = pltpu.SemaphoreType.DMA
    outs = [_sds((3,) + s.shape[1:], s.dtype) for s in sums]
    if has_small:
        outs.append(_sds((_NDEV,) + small.shape, small.dtype))
    return _Plan(list(sums) + ([small] if has_small else []), outs, {},
                 [dma((3 * max(n, 1),)), dma((3 * max(n, 1),)), dma((_NDEV - 1,)), dma((_NDEV - 1,)), dma((1,))],
                 start, finish)


def _shard_sum(sums, recv, sel):
    s, r, c2 = sums.shape
    tm = _tile(r, 256, 16)

    def body(sel_ref, own_ref, rc_ref, o_ref):
        rc = rc_ref[...]
        o_ref[...] = ((own_ref[...].astype(_F32) + rc[0].astype(_F32)) + rc[1].astype(_F32)) + rc[2].astype(_F32)

    gs = pltpu.PrefetchScalarGridSpec(
        num_scalar_prefetch=1, grid=(r // tm,),
        in_specs=[pl.BlockSpec((None, tm, c2), lambda i, q: (q[0], i, 0)),
                  pl.BlockSpec((3, tm, c2), lambda i, q: (0, i, 0))],
        out_specs=pl.BlockSpec((None, tm, c2), lambda i, q: (q[1], i, 0)))
    return _call(body, name="shard_sum", grid_spec=gs, out_shape=_sds((2, r, c2), _F32),
                 compiler_params=_cparams("parallel"))(sel, sums, recv)


def _sum_slots(stack, name):
    k, r, c = stack.shape
    tm = _tile(r, 256, 16 if stack.dtype == _BF else 8)

    def fn(v):
        out = v[0].astype(_F32)
        for i in range(1, k):
            out = out + v[i].astype(_F32)
        return (out,)

    return _rowcall(fn, [stack], [pl.BlockSpec((k, tm, c), lambda j, i: (0, i, 0))], [_sds((r, c), _F32)],
                    [_rb(tm, c)], [False], grid=(1, r // tm), name=name)[0]


def _share_halves(bufs):
    n = len(bufs)

    def body(*refs):
        outs = refs[n:2 * n]
        ssem, rsem = refs[2 * n:]
        x, y, c = _me()
        sib = (x, y, 1 - c)
        cps = []
        for a in range(n):
            cp = _rcopy(outs[a].at[c], outs[a].at[c], ssem.at[a], rsem.at[a], sib)
            cp.start()
            cps.append(cp)
        for a in range(n):
            _rcopy(outs[a].at[c], outs[a].at[1 - c], ssem.at[a], rsem.at[a], sib).wait_recv()
        for cp in cps:
            cp.wait_send()

    dma = pltpu.SemaphoreType.DMA
    return _call(body, name="share_halves", in_specs=[_ANY] * n, out_specs=[_ANY] * n,
                 out_shape=[_sds(b.shape, b.dtype) for b in bufs], input_output_aliases={a: a for a in range(n)},
                 scratch_shapes=[dma((n,)), dma((n,))])(*bufs)


def _adamw_math(w, g, m, v):
    m = _B1 * m + (1.0 - _B1) * g
    v = _B2 * v + (1.0 - _B2) * jnp.square(g)
    m_hat = m / (1.0 - _B1 ** _STEP)
    v_hat = v / (1.0 - _B2 ** _STEP)
    delta = -_LR * (m_hat / (jnp.sqrt(v_hat) + _ADAM_EPS) + _WD * w)
    return delta, m, v


def _adamw_shard(w, g2, m, v, name):
    r, c = w.shape
    c2 = c // 2
    tm = _tile(r, 256, 8)
    blk = pl.BlockSpec((tm, c2), lambda h, i: (i, h))

    def fn(wb, gb, mb, vb):
        return (gb,) + _adamw_math(wb, gb, mb, vb)

    return _rowcall(fn, [w, g2, m, v], [blk, pl.BlockSpec((None, tm, c2), lambda h, i: (h, i, 0)), blk, blk],
                    [_sds((r, c), _F32)] * 4, [blk] * 4, [False] * 4, grid=(2, r // tm), name=name)


def _adamw_whole(w, g, m, v, name):
    r, c = w.shape
    blk = _full((r, c))
    return _rowcall(lambda *a: _adamw_math(*a), [w, g, m, v], [blk] * 4, [_sds((r, c), _F32)] * 3, [blk] * 3,
                    [False] * 3, grid=(1, 1), name=name)


def _adamw_many(ws, g_pack, ms, vs):
    k = len(ws)
    views, offs, off = [], [], 0
    for w in ws:
        n = w.size
        views.append((n // _LANE, _LANE) if n % _LANE == 0 else (1, n))
        offs.append(off)
        off += (n + (-n) % (8 * _LANE)) // _LANE

    def body(*refs):
        g_ref, w_refs, m_refs, v_refs = refs[0], refs[1:1 + k], refs[1 + k:1 + 2 * k], refs[1 + 2 * k:1 + 3 * k]
        outs = refs[1 + 3 * k:]
        for i in range(k):
            r, c = views[i]
            g = g_ref[offs[i]:offs[i] + r, 0:c]
            res = (g,) + _adamw_math(w_refs[i][...], g, m_refs[i][...], v_refs[i][...])
            for o_ref, val in zip(outs[4 * i:4 * i + 4], res):
                o_ref[...] = val

    args = [g_pack] + [a.reshape(views[i]) for grp in (ws, ms, vs) for i, a in enumerate(grp)]
    res = _call(body, name="adamw_small", out_shape=[_sds(views[i], _F32) for i in range(k) for _ in range(4)])(*args)
    return [[res[4 * i + j].reshape(ws[i].shape) for j in range(4)] for i in range(k)]


def _pack(arrs):
    parts = []
    for a in arrs:
        f = a.reshape(-1).astype(_F32)
        pad = (-f.shape[0]) % (8 * _LANE)
        if pad:
            f = jnp.concatenate([f, jnp.zeros((pad,), _F32)])
        parts.append(f)
    return jnp.concatenate(parts).reshape(-1, _LANE)


def kernel(x, ln_mix_g, w_in, s5_a_re, s5_a_im, s5_log_dt, s5_b_re, s5_b_im, s5_c_re, s5_c_im, s5_d, s5_w_glu, s5_b_glu, w_proj_s5, hgrn_lb_logits, hgrn_norm_g, w_proj_hgrn, w_out, ln_ffn_g, w_up, conv_w, conv_b, w_down, ln_final_g, loss_target, m_ln_mix_g, m_w_in, m_s5_a_re, m_s5_a_im, m_s5_log_dt, m_s5_b_re, m_s5_b_im, m_s5_c_re, m_s5_c_im, m_s5_d, m_s5_w_glu, m_s5_b_glu, m_w_proj_s5, m_hgrn_lb_logits, m_hgrn_norm_g, m_w_proj_hgrn, m_w_out, m_ln_ffn_g, m_w_up, m_conv_w, m_conv_b, m_w_down, m_ln_final_g, v_ln_mix_g, v_w_in, v_s5_a_re, v_s5_a_im, v_s5_log_dt, v_s5_b_re, v_s5_b_im, v_s5_c_re, v_s5_c_im, v_s5_d, v_s5_w_glu, v_s5_b_glu, v_w_proj_s5, v_hgrn_lb_logits, v_hgrn_norm_g, v_w_proj_hgrn, v_w_out, v_ln_ffn_g, v_w_up, v_conv_w, v_conv_b, v_w_down, v_ln_final_g):
    assert x.shape[0] == 1 and w_in.shape[0] == 1, "one example per device, one layer"
    t, d = x.shape[1], x.shape[2]
    w5 = s5_w_glu.shape[2]
    hw = hgrn_norm_g.shape[1]
    ng_, np_, gc = s5_b_re.shape[1], s5_b_re.shape[2], s5_b_re.shape[3]
    dff = w_down.shape[1] * _NCHIP
    assert gc * _S5_SET == _LANE and ng_ * gc == w5 and hw % _LANE == 0
    gs_off = w5 + 4 * hw
    ci = lax.axis_index("c")
    xt = x.reshape(t, d)
    tgt = loss_target.reshape(t, d)

    big_names = ["w_in", "s5_w_glu", "w_proj_s5", "w_proj_hgrn", "w_out", "w_up", "w_down"]
    big_w = dict(w_in=w_in[0], s5_w_glu=s5_w_glu[0], w_proj_s5=w_proj_s5[0], w_proj_hgrn=w_proj_hgrn[0],
                 w_out=w_out[0], w_up=w_up[0], w_down=w_down[0])
    chip = 2 * lax.axis_index("x") + lax.axis_index("y")
    sel_chip = jnp.stack([chip, ci]).astype(jnp.int32)
    slots = {k: _cast_into_slot(big_w[k], sel_chip) for k in big_names}
    cb = conv_b

    tm_big = _tile(t, 1024, 8)

    h1 = _rms_fwd(xt, ln_mix_g, "rms1_fwd")
    nin_s = slots["w_in"].shape[2]
    proj, (g_in, g_cw), (g_glu, g_ps5, g_ph, g_out) = _mm_shard_order(
        h1, _gather_plan([slots["w_in"]], [conv_w[0]], rel=(1, 0, 2)),
        _gather_plan([slots[k] for k in ("s5_w_glu", "w_proj_s5", "w_proj_hgrn", "w_out")], []),
        sel_chip, tm=tm_big, tn=_tile(nin_s, 1152), name="mm_proj")
    wglu = g_glu.reshape(w5, w5)
    wout = g_out.reshape(d, d)
    cw = g_cw.transpose(1, 0, 2).reshape(3, 2 * dff)

    abar_r, abar_i, coef_r, coef_i = _s5_prep(s5_a_re[0], s5_a_im[0], s5_log_dt.reshape(ng_, 1))
    lanes = ng_ * np_

    def fold_coef(cr_, ci_, bre, bim):
        return cr_[..., None] * bre - ci_[..., None] * bim, cr_[..., None] * bim + ci_[..., None] * bre

    (bf_re, bf_im), fold_vjp = jax.vjp(fold_coef, coef_r, coef_i, s5_b_re[0], s5_b_im[0])
    par = (abar_r.reshape(1, lanes), abar_i.reshape(1, lanes),
           _bd_in(bf_re), _bd_in(bf_im), _bd_out(s5_c_re[0]), -_bd_out(s5_c_im[0]), s5_d.reshape(1, w5))
    (y_s5, car, cai), (g_up_part,) = _s5_fwd(proj, par, t, w5, rider=_gather_plan([slots["w_up"]], [], rel=(0, 1)))
    z = _s5glu_fwd(y_s5, wglu, s5_b_glu)
    ys = _mm(z, g_ps5, "nn", _F32, tm=tm_big, tn=g_ps5.shape[2], tk=w5, name="mm_proj_s5")

    lb = _lb_prep(hgrn_lb_logits)
    (og, st_all), (g_up,) = _hgrn_fwd(proj, lb, hgrn_norm_g, t, w5, hw, rider=_gather_plan([g_up_part], [], rel=(2,)))
    yh = _mm(og, g_ph, "nn", _F32, tm=tm_big, tn=g_ph.shape[2], tk=hw, name="mm_proj_hgrn")

    merged = _merge_fwd(proj, ys, yh, t, d, gs_off)
    x2 = _mm(merged, wout, "nn", _F32, tm=tm_big, tn=_tile(d, 1024), tk=d, res=xt, name="mm_out")
    h2 = _rms_fwd(x2, ln_ffn_g, "rms2_fwd")
    up_s = g_up.shape[2]
    w_conv = _tile(dff, 1408)
    nbh = dff // w_conv
    tn_up = _tile(up_s, 1408)
    tk_dh2 = _tile(up_s, w_conv)
    tn_gwup = _tile(up_s // 2, 1408)
    assert w_conv % tn_up == 0 and w_conv % tk_dh2 == 0 and w_conv % tn_gwup == 0
    up, (g_down,) = _mm(h2, g_up, "nn", _BF, tm=tm_big, tn=tn_up, tk=d, name="mm_up",
                        colperm=_pair_perm(nbh, w_conv // tn_up), rider=_gather_plan([slots["w_down"]], []))
    wdown = g_down.reshape(dff, d)
    act = _convact_fwd(up, cw, cb, t, dff)
    x3 = _mm(act, wdown, "nn", _F32, tm=tm_big, tn=_tile(d, 1024), tk=None, res=x2, name="mm_down")
    loss_part, dx3, dx3b, d_gfin = _loss_head(x3, tgt, ln_final_g.reshape(1, d))

    dact = _mm(dx3b, wdown, "nt", _BF, tm=tm_big, tn=_tile(dff, 1408), tk=d, name="mm_dact")
    r_down = dff // _NCHIP
    gw_down = _mm(act, dx3b, "tn", _BF, tm=_tile(r_down, 1408), tn=_tile(d // 2, 1024), tk=None, halves="rows",
                  name="mm_gw_down")
    def chip_sums(grads, name):
        theirs = _swap_halves(grads, name)
        return [_add_pairs(g, th, sel_chip) for g, th in zip(grads, theirs)]

    (s_down,) = chip_sums([gw_down], "swap_halves_d")
    dup, d_cw, d_cb = _convact_bwd(up, dact, cw, cb, t, dff)
    dh2, (r_down,) = _mm(dup, g_up, "nt", _BF, tm=tm_big, tn=_tile(d, 1024), tk=tk_dh2, name="mm_dh2",
                         colperm=_pair_perm(nbh, w_conv // tk_dh2), rider=_exchange_plan([s_down], None))
    gw_up = _mm(h2, dup, "tn", _BF, tm=_tile(d, 1024), tn=tn_gwup, tk=None, halves="cols", name="mm_gw_up",
                colperm=_pair_perm(nbh, w_conv // tn_gwup))
    dx2, dx2b, d_gffn = _rms_bwd(x2, ln_ffn_g, dh2, dx3, "rms2_bwd")
    dmerged = _mm(dx2b, wout, "nt", _BF, tm=tm_big, tn=_tile(d, 1024), tk=d, name="mm_dmerged")
    gw_out = _mm(merged, dx2b, "tn", _BF, tm=_tile(d // _NCHIP, 1024), tn=_tile(d // 2, 1024), tk=None, halves="rows",
                 name="mm_gw_out")
    dys, dyh, dgs, dgh = _merge_bwd(proj, ys, yh, dmerged, t, d, gs_off)
    ps_s = g_ps5.shape[2]
    dz = _mm(dys, g_ps5, "nt", _BF, tm=tm_big, tn=_tile(w5, 1024), tk=ps_s, name="mm_dz")
    gw_ps5 = _mm(z, dys, "tn", _BF, tm=_tile(w5, 1024), tn=ps_s // 2, tk=None, halves="cols", name="mm_gw_ps5")
    dog = _mm(dyh, g_ph, "nt", _BF, tm=tm_big, tn=_tile(hw, 1024), tk=ps_s, name="mm_dog")
    gw_ph = _mm(og, dyh, "tn", _BF, tm=_tile(hw, 1024), tn=ps_s // 2, tk=None, halves="cols", name="mm_gw_ph")
    dy_s5, gw_glu_full, d_bglu = _s5glu_bwd(y_s5, dz, wglu, s5_b_glu)
    r_glu = w5 // _NCHIP
    gw_glu = gw_glu_full.astype(_BF).reshape(_NCHIP, r_glu, 2, w5 // 2).transpose(2, 0, 1, 3)

    s_glu, s_ps5, s_ph, s_out, s_up = chip_sums([gw_glu, gw_ps5, gw_ph, gw_out, gw_up], "swap_halves_a")
    s5g, (r_glu_, r_ps5, r_ph, r_out, r_up) = _s5_bwd(
        proj, dy_s5, car, cai, par, t, w5, rider=_exchange_plan([s_glu, s_ps5, s_ph, s_out, s_up], None))
    du = s5g[0]
    dq, df, di, dg, d_lb, d_ng = _hgrn_bwd(proj, lb, hgrn_norm_g, st_all, dog, t, w5, hw)
    dproj = jnp.concatenate([du, dq, df, di, dg, dgs, dgh], axis=1)

    d_coef_r, d_coef_i, d_bre, d_bim = fold_vjp((_bd_in_grad(s5g[3], np_, gc), _bd_in_grad(s5g[4], np_, gc)))
    d_are, d_aim, d_ldt = _s5_prep_bwd(s5_a_re[0], s5_a_im[0], s5_log_dt.reshape(ng_, 1),
                                       [s5g[1].reshape(ng_, np_), s5g[2].reshape(ng_, np_), d_coef_r, d_coef_i])
    d_cre = _bd_out_grad(s5g[5], np_, gc)
    d_cim = -_bd_out_grad(s5g[6], np_, gc)
    d_logits = _lb_prep_bwd(hgrn_lb_logits, d_lb)

    small_names = ["s5_a_re", "s5_a_im", "s5_log_dt", "s5_b_re", "s5_b_im", "s5_c_re", "s5_c_im", "s5_d",
                   "s5_b_glu", "hgrn_lb_logits", "hgrn_norm_g", "ln_ffn_g", "conv_b", "ln_final_g", "ln_mix_g"]
    small_w = dict(ln_mix_g=ln_mix_g, s5_a_re=s5_a_re, s5_a_im=s5_a_im, s5_log_dt=s5_log_dt, s5_b_re=s5_b_re,
                   s5_b_im=s5_b_im, s5_c_re=s5_c_re, s5_c_im=s5_c_im, s5_d=s5_d, s5_b_glu=s5_b_glu,
                   hgrn_lb_logits=hgrn_lb_logits, hgrn_norm_g=hgrn_norm_g, ln_ffn_g=ln_ffn_g, conv_b=conv_b,
                   ln_final_g=ln_final_g)
    small_m = dict(ln_mix_g=m_ln_mix_g, s5_a_re=m_s5_a_re, s5_a_im=m_s5_a_im, s5_log_dt=m_s5_log_dt, s5_b_re=m_s5_b_re,
                   s5_b_im=m_s5_b_im, s5_c_re=m_s5_c_re, s5_c_im=m_s5_c_im, s5_d=m_s5_d, s5_b_glu=m_s5_b_glu,
                   hgrn_lb_logits=m_hgrn_lb_logits, hgrn_norm_g=m_hgrn_norm_g, ln_ffn_g=m_ln_ffn_g, conv_b=m_conv_b,
                   ln_final_g=m_ln_final_g)
    small_v = dict(ln_mix_g=v_ln_mix_g, s5_a_re=v_s5_a_re, s5_a_im=v_s5_a_im, s5_log_dt=v_s5_log_dt, s5_b_re=v_s5_b_re,
                   s5_b_im=v_s5_b_im, s5_c_re=v_s5_c_re, s5_c_im=v_s5_c_im, s5_d=v_s5_d, s5_b_glu=v_s5_b_glu,
                   hgrn_lb_logits=v_hgrn_lb_logits, hgrn_norm_g=v_hgrn_norm_g, ln_ffn_g=v_ln_ffn_g, conv_b=v_conv_b,
                   ln_final_g=v_ln_final_g)
    small_g = dict(s5_a_re=d_are, s5_a_im=d_aim, s5_log_dt=d_ldt, s5_b_re=d_bre, s5_b_im=d_bim,
                   s5_c_re=d_cre, s5_c_im=d_cim, s5_d=s5g[7], s5_b_glu=d_bglu, hgrn_lb_logits=d_logits,
                   hgrn_norm_g=d_ng, ln_ffn_g=d_gffn, conv_b=d_cb, ln_final_g=d_gfin)
    like = [small_w[k] for k in small_names]
    assert small_names[-1] == "ln_mix_g"
    pack_a = _pack([small_g[k] for k in small_names[:-1]] + [d_cw])
    gw_in, (r_small_a,) = _mm(h1, dproj, "tn", _BF, tm=_tile(d, 1024), tn=_tile(nin_s // 2, 1152), tk=None, halves="cols",
                              name="mm_gw_in", rider=_exchange_plan([], pack_a))
    (s_in,) = chip_sums([gw_in], "swap_halves_b")
    dh1, (r_in,) = _mm(dproj, g_in, "nt", _BF, tm=tm_big, tn=_tile(d, 1024), tk=None, name="mm_dh1",
                       rider=_exchange_plan([s_in], None))
    dx, _, d_gmix = _rms_bwd(xt, ln_mix_g, dh1, dx2, "rms1_bwd")
    (r_small_b,) = _run_plan(_exchange_plan([], _pack([d_gmix])), "exchange_gmix")
    sums = [s_in, s_glu, s_ps5, s_ph, s_out, s_up, s_down]
    received = [r_in, r_glu_, r_ps5, r_ph, r_out, r_up, r_down]
    halves = [_shard_sum(sm, rc, sel_chip) for sm, rc in zip(sums, received)]
    g_a = _sum_slots(r_small_a, "small_sum_a")
    g_b = _sum_slots(r_small_b, "small_sum_b")
    full = _share_halves(halves)
    w_pack = _pack(like)
    rows_a = w_pack.shape[0] - g_b.shape[0]
    g_small = jnp.concatenate([g_a[:rows_a], g_b], axis=0)
    cs = conv_w.shape[2]
    g_cw_full = g_a[rows_a:].reshape(-1)[:3 * 2 * dff].reshape(3, 2 * dff)
    g_cw = lax.dynamic_slice_in_dim(g_cw_full, chip * cs, cs, axis=1)

    big_m = dict(w_in=m_w_in, s5_w_glu=m_s5_w_glu, w_proj_s5=m_w_proj_s5, w_proj_hgrn=m_w_proj_hgrn, w_out=m_w_out,
                 w_up=m_w_up, w_down=m_w_down)
    big_v = dict(w_in=v_w_in, s5_w_glu=v_s5_w_glu, w_proj_s5=v_w_proj_s5, w_proj_hgrn=v_w_proj_hgrn, w_out=v_w_out,
                 w_up=v_w_up, w_down=v_w_down)
    res = {}
    for k, g2 in zip(big_names, full):
        w2 = big_w[k]
        shp = (1,) + w2.shape
        outs = _adamw_shard(w2, g2, big_m[k][0], big_v[k][0], "adamw_" + k)
        res[k] = [o.reshape(shp) for o in outs]
    sm_outs = _adamw_many(like, g_small, [small_m[k] for k in small_names], [small_v[k] for k in small_names])
    for k, outs in zip(small_names, sm_outs):
        res[k] = outs

    cw_outs = _adamw_whole(conv_w[0], g_cw, m_conv_w[0], v_conv_w[0], "adamw_conv_w")
    res["conv_w"] = [g_cw.reshape(conv_w.shape)] + [o.reshape(conv_w.shape) for o in cw_outs]

    loss = lax.psum(loss_part[0, 0], ("x", "y", "c"))
    order = ["ln_mix_g", "w_in", "s5_a_re", "s5_a_im", "s5_log_dt", "s5_b_re", "s5_b_im", "s5_c_re", "s5_c_im", "s5_d",
             "s5_w_glu", "s5_b_glu", "w_proj_s5", "hgrn_lb_logits", "hgrn_norm_g", "w_proj_hgrn", "w_out", "ln_ffn_g",
             "w_up", "conv_w", "conv_b", "w_down", "ln_final_g"]
    return (loss, dx.reshape(x.shape), *[res[k][0] for k in order], *[res[k][1] for k in order],
            *[res[k][2] for k in order], *[res[k][3] for k in order])
```

```python
import functools

import jax
import jax.numpy as jnp
from jax import lax
from jax.experimental import pallas as pl
from jax.experimental.pallas import tpu as pltpu

_F32 = jnp.float32
_BF = jnp.bfloat16
_RMS_EPS = 1e-6
_S5_MAX_RE = -1e-4
_LR, _B1, _B2, _ADAM_EPS, _WD, _STEP = 0.001, 0.9, 0.999, 1e-08, 0.01, 10
_MESH = pl.DeviceIdType.MESH
_ANY = pl.BlockSpec(memory_space=pl.ANY)
_LANE = 128
_VMEM_LIMIT = 56 * 1024 * 1024
_CHUNK = 64
_S5_TB = 512
_S5_SET = 8
_HGRN_HP = 8
_HGRN_SEG = 512
_HGRN_GROUP = 4
_NCHIP = 4
_NDEV = 8


def _call(body, **kw):
    return pl.pallas_call(body, **kw)


def _cparams(*sem):
    return pltpu.CompilerParams(dimension_semantics=sem, vmem_limit_bytes=_VMEM_LIMIT)


def _tile(n, pref, unit=_LANE):
    if n <= pref:
        return n
    t = (pref // unit) * unit
    while t >= unit:
        if n % t == 0:
            return t
        t -= unit
    raise ValueError(f"no tile for {n}")


_OPERAND_BYTES = 12 * 1024 * 1024


def _tk_fit(k, tm, tn):
    best = _LANE
    for tk in range(_LANE, k + 1, _LANE):
        if k % tk == 0 and (tm + tn) * tk * 2 <= _OPERAND_BYTES:
            best = tk
    return best if k % _LANE == 0 else k


_NN = ((1,), (0,))
_NT = ((1,), (1,))
_TN = ((0,), (0,))


def _dg(a, b, dims):
    return lax.dot_general(a.astype(_BF), b.astype(_BF), (dims, ((), ())), preferred_element_type=_F32)


@jax.custom_vjp
def _bdot(a, b):
    return _dg(a, b, _NN)


def _bdot_f(a, b):
    return _dg(a, b, _NN), (a, b)


def _bdot_b(res, g):
    a, b = res
    return _dg(g, b, _NT).astype(a.dtype), _dg(a, g, _TN).astype(b.dtype)


_bdot.defvjp(_bdot_f, _bdot_b)


@jax.custom_vjp
def _bdot_nt(a, b):
    return _dg(a, b, _NT)


def _bdot_nt_f(a, b):
    return _dg(a, b, _NT), (a, b)


def _bdot_nt_b(res, g):
    a, b = res
    return _dg(g, b, _NN).astype(a.dtype), _dg(g, a, _TN).astype(b.dtype)


_bdot_nt.defvjp(_bdot_nt_f, _bdot_nt_b)


@jax.custom_vjp
def _bdot_tn(a, b):
    return _dg(a, b, _TN)


def _bdot_tn_f(a, b):
    return _dg(a, b, _TN), (a, b)


def _bdot_tn_b(res, g):
    a, b = res
    return _dg(b, g, _NT).astype(a.dtype), _dg(a, g, _NN).astype(b.dtype)


_bdot_tn.defvjp(_bdot_tn_f, _bdot_tn_b)


_SUBLANES = 8


def _shift_up(x, n):
    r = x.shape[0]
    if n % _SUBLANES == 0:
        return jnp.concatenate([x[n:], jnp.zeros((n,) + x.shape[1:], x.dtype)], axis=0)
    row = lax.broadcasted_iota(jnp.int32, x.shape, 0)
    return jnp.where(row < r - n, pltpu.roll(x, r - n, 0), 0.0)


@functools.partial(jax.custom_vjp, nondiff_argnums=(1,))
def _shift_down(x, n):
    if n % _SUBLANES == 0:
        return jnp.concatenate([jnp.zeros((n,) + x.shape[1:], x.dtype), x[:x.shape[0] - n]], axis=0)
    row = lax.broadcasted_iota(jnp.int32, x.shape, 0)
    return jnp.where(row >= n, pltpu.roll(x, n, 0), 0.0)


def _shift_down_f(x, n):
    return _shift_down(x, n), None


def _shift_down_b(n, _, g):
    return (_shift_up(g, n),)


_shift_down.defvjp(_shift_down_f, _shift_down_b)


def _rows_apart(x):
    return tuple(x[k:k + _SUBLANES] for k in range(0, x.shape[0], _SUBLANES))


@jax.custom_vjp
def _split_rows(x):
    return _rows_apart(x)


_split_rows.defvjp(lambda x: (_rows_apart(x), None), lambda _, gs: (jnp.concatenate(gs, axis=0),))


@jax.custom_vjp
def _join_rows(pieces):
    return jnp.concatenate(pieces, axis=0)


_join_rows.defvjp(lambda pieces: (jnp.concatenate(pieces, axis=0), None), lambda _, g: (_rows_apart(g),))


@jax.custom_vjp
def _last_row(x):
    return x[_SUBLANES - 1:_SUBLANES]


def _last_row_b(_, g):
    row = lax.broadcasted_iota(jnp.int32, (_SUBLANES, g.shape[1]), 0)
    return (jnp.where(row == _SUBLANES - 1, g, 0.0),)


_last_row.defvjp(lambda x: (x[_SUBLANES - 1:_SUBLANES], None), _last_row_b)


def _sigmoid(x):
    return 1.0 / (1.0 + jnp.exp(-x))


def _silu(x):
    return x * _sigmoid(x)


def _gelu(x):
    return 0.5 * x * (1.0 + jnp.tanh(0.7978845608028654 * (x + 0.044715 * (x * x * x))))


def _rms_core(x, g):
    return x * lax.rsqrt(jnp.mean(x * x, axis=-1, keepdims=True) + _RMS_EPS) * g


def _mm(a, b, mode, out_dtype, *, tm, tn, tk, res=None, halves=None, rider=None, colperm=None, name):
    if colperm is None:
        def colperm(n_):
            return n_
    if tk is None:
        kdim = a.shape[0] if mode == "tn" else (b.shape[2] if (mode == "nt" and b.ndim == 3) else a.shape[1])
        tk = _tk_fit(kdim, tm, tn)
    if mode == "nn":
        m, k = a.shape
        a_spec = pl.BlockSpec((tm, tk), lambda i, j, kk: (i, kk))
        if b.ndim == 3:
            s, _, ns = b.shape
            n = s * ns
            npb = ns // tn
            b_spec = pl.BlockSpec((None, tk, tn), lambda i, j, kk: (j // npb, kk, j % npb))
        else:
            n = b.shape[1]
            b_spec = pl.BlockSpec((tk, tn), lambda i, j, kk: (kk, j))
        dims = _NN
    elif mode == "nt":
        m, k = a.shape
        a_spec = pl.BlockSpec((tm, tk), lambda i, j, kk: (i, colperm(kk)))
        if b.ndim == 3:
            s, n, ks = b.shape
            kpb = ks // tk
            b_spec = pl.BlockSpec((None, tn, tk), lambda i, j, kk: (kk // kpb, j, kk % kpb))
        else:
            n = b.shape[0]
            b_spec = pl.BlockSpec((tn, tk), lambda i, j, kk: (j, kk))
        dims = _NT
    else:
        k, m = a.shape
        n = b.shape[1]
        a_spec = pl.BlockSpec((tk, tm), lambda i, j, kk: (kk, i))
        b_spec = pl.BlockSpec((tk, tn), lambda i, j, kk: (kk, colperm(j)))
        dims = _TN
    nk = k // tk
    if halves is None:
        out_shape = jax.ShapeDtypeStruct((m, n), out_dtype)
        out_spec = pl.BlockSpec((tm, tn), lambda i, j, kk: (i, colperm(j) if mode == "nn" else j))
    elif halves == "cols":
        c2 = n // (2 * _NCHIP)
        tpc = c2 // tn
        out_shape = jax.ShapeDtypeStruct((2, _NCHIP, m, c2), out_dtype)
        out_spec = pl.BlockSpec((None, None, tm, tn),
                                lambda i, j, kk: ((j // tpc) % 2, j // (2 * tpc), i, j % tpc))
    else:
        c2 = n // 2
        tpc = c2 // tn
        r = m // _NCHIP
        tpr = r // tm
        out_shape = jax.ShapeDtypeStruct((2, _NCHIP, r, c2), out_dtype)
        out_spec = pl.BlockSpec((None, None, tm, tn),
                                lambda i, j, kk: (j // tpc, i // tpr, i % tpr, j % tpc))
    has_res = res is not None
    nreg = 3 if has_res else 2
    ni, nj = m // tm, n // tn
    r_ins = list(rider.ins) if rider else []
    r_outs = list(rider.outs) if rider else []

    def body(*refs):
        a_ref, b_ref = refs[0], refs[1]
        r_ref = refs[2] if has_res else None
        rin = refs[nreg:nreg + len(r_ins)]
        o_ref = refs[nreg + len(r_ins)]
        rout = refs[nreg + len(r_ins) + 1:nreg + len(r_ins) + 1 + len(r_outs)]
        acc_ref = refs[nreg + len(r_ins) + 1 + len(r_outs)]
        sems = refs[nreg + len(r_ins) + 2 + len(r_outs):]
        i, j, kk = pl.program_id(0), pl.program_id(1), pl.program_id(2)

        if rider:
            @pl.when(jnp.logical_and(jnp.logical_and(i == 0, j == 0), kk == 0))
            def _():
                rider.start(rin, rout, sems)

        @pl.when(kk == 0)
        def _():
            acc_ref[...] = jnp.zeros_like(acc_ref)

        acc_ref[...] += _dg(a_ref[...], b_ref[...], dims)

        @pl.when(kk == nk - 1)
        def _():
            out = acc_ref[...]
            if has_res:
                out = out + r_ref[...]
            o_ref[...] = out.astype(out_dtype)

        if rider:
            @pl.when(jnp.logical_and(jnp.logical_and(i == ni - 1, j == nj - 1), kk == nk - 1))
            def _():
                rider.finish(rin, rout, sems)

    in_specs = [a_spec, b_spec]
    args = [a, b]
    if has_res:
        in_specs.append(pl.BlockSpec((tm, tn), lambda i, j, kk: (i, j)))
        args.append(res)
    if not rider:
        return _call(body, name=name, grid=(ni, nj, nk), in_specs=in_specs, out_specs=out_spec,
                     out_shape=out_shape, scratch_shapes=[pltpu.VMEM((tm, tn), _F32)],
                     compiler_params=_cparams("parallel", "parallel", "arbitrary"))(*args)
    res_all = _call(body, name=name, grid=(ni, nj, nk), in_specs=in_specs + [_ANY] * len(r_ins),
                    out_specs=[out_spec] + [_ANY] * len(r_outs), out_shape=[out_shape] + r_outs,
                    input_output_aliases={nreg + k: 1 + v for k, v in rider.aliases.items()},
                    scratch_shapes=[pltpu.VMEM((tm, tn), _F32)] + list(rider.sems),
                    compiler_params=_cparams("arbitrary", "arbitrary", "arbitrary"))(*args, *r_ins)
    return res_all[0], list(res_all[1:])


def _rowcall(fn, ins, in_specs, outs, out_specs, acc, *, grid, name):
    nin = len(ins)

    def body(*refs):
        vals = fn(*[r[...] for r in refs[:nin]])
        first = pl.program_id(1) == 0
        for k, (o_ref, v) in enumerate(zip(refs[nin:], vals)):
            if acc[k]:
                @pl.when(first)
                def _(o_ref=o_ref):
                    o_ref[...] = jnp.zeros_like(o_ref)
                o_ref[...] += v.astype(o_ref.dtype)
            else:
                o_ref[...] = v.astype(o_ref.dtype)

    return _call(body, name=name, grid=grid, in_specs=in_specs, out_specs=out_specs, out_shape=outs,
                 compiler_params=_cparams("parallel", "arbitrary"))(*ins)


def _rb(tm, w, cb=0):
    return pl.BlockSpec((tm, w), lambda j, i: (i, cb + j))


def _cb(w, cb=0):
    return pl.BlockSpec((1, w), lambda j, i: (0, cb + j))


def _full(shape):
    nd = len(shape)
    return pl.BlockSpec(shape, lambda j, i: (0,) * nd)


def _sds(shape, dtype):
    return jax.ShapeDtypeStruct(shape, dtype)


def _rms_fwd(x, g, name):
    t, d = x.shape
    tm = _tile(t, 512, 8)
    return _rowcall(lambda xb, gb: (_rms_core(xb, gb),), [x, g], [_rb(tm, d), _full((1, d))],
                    [_sds((t, d), _BF)], [_rb(tm, d)], [False], grid=(1, t // tm), name=name)[0]


def _rms_bwd(x, g, dh, dres, name):
    t, d = x.shape
    tm = _tile(t, 256, 8)

    def fn(xb, gb, dhb, drb):
        _, vjp = jax.vjp(_rms_core, xb, gb)
        dx, dg = vjp(dhb.astype(_F32))
        dx = dx + drb
        return dx, dx, dg

    return _rowcall(fn, [x, g, dh, dres], [_rb(tm, d), _full((1, d)), _rb(tm, d), _rb(tm, d)],
                    [_sds((t, d), _F32), _sds((t, d), _BF), _sds((1, d), _F32)],
                    [_rb(tm, d), _rb(tm, d), _full((1, d))], [False, False, True],
                    grid=(1, t // tm), name=name)


def _loss_head(x3, tgt, g):
    t, d = x3.shape
    tm = _tile(t, 256, 8)

    def fn(xb, tb, gb):
        y, vjp = jax.vjp(_rms_core, xb, gb)
        e = y - tb
        part = 0.5 * jnp.sum(jnp.mean(e * e, axis=-1, keepdims=True), axis=0, keepdims=True)
        dx, dg = vjp(e * (1.0 / d))
        return jnp.broadcast_to(part, (1, _LANE)), dx, dx, dg

    return _rowcall(fn, [x3, tgt, g], [_rb(tm, d), _rb(tm, d), _full((1, d))],
                    [_sds((1, _LANE), _F32), _sds((t, d), _F32), _sds((t, d), _BF), _sds((1, d), _F32)],
                    [_full((1, _LANE)), _rb(tm, d), _rb(tm, d), _full((1, d))], [True, False, False, True],
                    grid=(1, t // tm), name="loss_head")


def _s5_disc(a_re, a_im, log_dt):
    lam_re = jnp.minimum(a_re, _S5_MAX_RE)
    lam_im = a_im
    dt = jnp.exp(log_dt)
    mag = jnp.exp(lam_re * dt)
    abar_re = mag * jnp.cos(lam_im * dt)
    abar_im = mag * jnp.sin(lam_im * dt)
    den = lam_re * lam_re + lam_im * lam_im
    nr = abar_re - 1.0
    ni = abar_im
    coef_re = (nr * lam_re + ni * lam_im) / den
    coef_im = (ni * lam_re - nr * lam_im) / den
    return abar_re, abar_im, coef_re, coef_im


def _s5_prep(a_re, a_im, log_dt):
    g, p = a_re.shape

    def body(ar, ai, ld, o0, o1, o2, o3):
        outs = _s5_disc(ar[...], ai[...], ld[...])
        for o, v in zip((o0, o1, o2, o3), outs):
            o[...] = v

    return _call(body, name="s5_prep", out_shape=[_sds((g, p), _F32)] * 4)(a_re, a_im, log_dt)


def _s5_prep_bwd(a_re, a_im, log_dt, cts):
    g, p = a_re.shape

    def body(ar, ai, ld, c0, c1, c2, c3, d0, d1, d2):
        _, vjp = jax.vjp(_s5_disc, ar[...], ai[...], ld[...])
        outs = vjp((c0[...], c1[...], c2[...], c3[...]))
        for o, v in zip((d0, d1, d2), outs):
            o[...] = v

    return _call(body, name="s5_prep_bwd",
                 out_shape=[_sds((g, p), _F32), _sds((g, p), _F32), _sds((g, 1), _F32)])(a_re, a_im, log_dt, *cts)


def _s5_block(u, car, cai, ar, ai, b_re, b_im, c_re, c_imn, dvec):
    bur = _bdot(u, b_re)
    bui = _bdot(u, b_im)
    shape8 = (_SUBLANES, ar.shape[1])
    pows = [(ar, ai)]
    for _ in range(2):
        pr, pi = pows[-1]
        pows.append((pr * pr - pi * pi, 2.0 * pr * pi))
    pows = [(jnp.broadcast_to(pr, shape8), jnp.broadcast_to(pi, shape8)) for pr, pi in pows]

    def scan8(xr, xi):
        for k, (pr, pi) in enumerate(pows):
            dr = _shift_down(xr, 1 << k)
            di = _shift_down(xi, 1 << k)
            xr, xi = xr + pr * dr - pi * di, xi + pr * di + pi * dr
        return xr, xi

    row8 = lax.broadcasted_iota(jnp.int32, (_SUBLANES, ar.shape[1]), 0)
    tr, ti = scan8(jnp.where(row8 == 0, ar, 0.0), jnp.where(row8 == 0, ai, 0.0))
    outs_r, outs_i = [], []
    for xr, xi in zip(_split_rows(bur), _split_rows(bui)):
        xr, xi = scan8(xr, xi)
        xr, xi = xr + tr * car - ti * cai, xi + tr * cai + ti * car
        car, cai = _last_row(xr), _last_row(xi)
        outs_r.append(xr)
        outs_i.append(xi)
    y = _bdot(_join_rows(tuple(outs_r)), c_re) + _bdot(_join_rows(tuple(outs_i)), c_imn) + dvec * u
    return y, car, cai


def _s5_specs(tb, lw, nt, rev):
    tmap = (lambda t: nt - 1 - t) if rev else (lambda t: t)
    vec = pl.BlockSpec((1, lw), lambda s, t: (0, s))
    return dict(
        u=pl.BlockSpec((tb, _LANE), lambda s, t: (tmap(t), s)),
        car=pl.BlockSpec((None, 1, lw), lambda s, t: (tmap(t), 0, s)),
        vec=vec,
        bmat=pl.BlockSpec((None, _LANE, lw), lambda s, t: (s, 0, 0)),
        cmat=pl.BlockSpec((None, lw, _LANE), lambda s, t: (s, 0, 0)),
        dvec=pl.BlockSpec((1, _LANE), lambda s, t: (0, s)),
    )


def _s5_fwd(proj, par, t, w5, rider=None):
    ar, ai, b_re, b_im, c_re, c_imn, dvec = par
    ns = w5 // _LANE
    lw = ar.shape[1] // ns
    tb = min(_S5_TB, t)
    nt = t // tb
    sp = _s5_specs(tb, lw, nt, False)

    def body(u_ref, ar_r, ai_r, bre_r, bim_r, cre_r, cim_r, d_r, y_ref, car_ref, cai_ref, s_r, s_i):
        @pl.when(pl.program_id(1) == 0)
        def _():
            s_r[...] = jnp.zeros_like(s_r)
            s_i[...] = jnp.zeros_like(s_i)

        car_ref[...] = s_r[...]
        cai_ref[...] = s_i[...]
        y, ncr, nci = _s5_block(u_ref[...], s_r[...], s_i[...], ar_r[...], ai_r[...],
                                bre_r[...], bim_r[...], cre_r[...], cim_r[...], d_r[...])
        y_ref[...] = y
        s_r[...] = ncr
        s_i[...] = nci

    kw = dict(name="s5_fwd", grid=(ns, nt),
              in_specs=[sp["u"], sp["vec"], sp["vec"], sp["bmat"], sp["bmat"],
                        sp["cmat"], sp["cmat"], sp["dvec"]],
              out_specs=[sp["u"], sp["car"], sp["car"]],
              out_shape=[_sds((t, w5), _F32), _sds((nt, 1, ns * lw), _F32), _sds((nt, 1, ns * lw), _F32)],
              scratch_shapes=[pltpu.VMEM((1, lw), _F32), pltpu.VMEM((1, lw), _F32)])
    args = (proj, ar, ai, b_re, b_im, c_re, c_imn, dvec)
    if rider is None:
        return _call(body, compiler_params=_cparams("parallel", "arbitrary"), **kw)(*args), []
    return _call_riding(body, rider, args=args, **kw)


def _s5_bwd(proj, dy, car, cai, par, t, w5, rider=None):
    ar, ai, b_re, b_im, c_re, c_imn, dvec = par
    ns = w5 // _LANE
    lw = ar.shape[1] // ns
    tb = min(_S5_TB, t)
    nt = t // tb
    sp = _s5_specs(tb, lw, nt, True)

    def body(u_ref, dy_ref, car_ref, cai_ref, ar_r, ai_r, bre_r, bim_r, cre_r, cim_r, d_r,
             du_ref, g_ar, g_ai, g_bre, g_bim, g_cre, g_cim, g_d, ds_r, ds_i):
        accs = (g_ar, g_ai, g_bre, g_bim, g_cre, g_cim, g_d)

        @pl.when(pl.program_id(1) == 0)
        def _():
            ds_r[...] = jnp.zeros_like(ds_r)
            ds_i[...] = jnp.zeros_like(ds_i)
            for o in accs:
                o[...] = jnp.zeros_like(o)

        _, vjp = jax.vjp(_s5_block, u_ref[...], car_ref[...], cai_ref[...], ar_r[...], ai_r[...],
                         bre_r[...], bim_r[...], cre_r[...], cim_r[...], d_r[...])
        grads = vjp((dy_ref[...], ds_r[...], ds_i[...]))
        du_ref[...] = grads[0].astype(_BF)
        ds_r[...] = grads[1]
        ds_i[...] = grads[2]
        for o, gval in zip(accs, grads[3:]):
            o[...] += gval

    vec_o = _sds((1, ns * lw), _F32)
    kw = dict(name="s5_bwd", grid=(ns, nt),
              in_specs=[sp["u"], sp["u"], sp["car"], sp["car"], sp["vec"], sp["vec"],
                        sp["bmat"], sp["bmat"], sp["cmat"], sp["cmat"], sp["dvec"]],
              out_specs=[sp["u"], sp["vec"], sp["vec"], sp["bmat"], sp["bmat"],
                         sp["cmat"], sp["cmat"], sp["dvec"]],
              out_shape=[_sds((t, w5), _BF), vec_o, vec_o,
                         _sds(b_re.shape, _F32), _sds(b_re.shape, _F32), _sds(c_re.shape, _F32),
                         _sds(c_re.shape, _F32), _sds((1, w5), _F32)],
              scratch_shapes=[pltpu.VMEM((1, lw), _F32), pltpu.VMEM((1, lw), _F32)])
    args = (proj, dy, car, cai, ar, ai, b_re, b_im, c_re, c_imn, dvec)
    if rider is None:
        return _call(body, compiler_params=_cparams("parallel", "arbitrary"), **kw)(*args), []
    return _call_riding(body, rider, args=args, **kw)


def _bd_in(b):
    g, p, c = b.shape
    s = g // _S5_SET
    b4 = b.reshape(s, _S5_SET, p, c).transpose(0, 1, 3, 2)
    eye = jnp.eye(_S5_SET, dtype=b.dtype)
    return (b4[:, :, :, None, :] * eye[None, :, None, :, None]).reshape(s, _S5_SET * c, _S5_SET * p)


def _bd_in_grad(d, p, c):
    s = d.shape[0]
    eye = jnp.eye(_S5_SET, dtype=d.dtype)
    d5 = d.reshape(s, _S5_SET, c, _S5_SET, p) * eye[None, :, None, :, None]
    return d5.sum(axis=3).transpose(0, 1, 3, 2).reshape(s * _S5_SET, p, c)


def _bd_out(cm):
    g, c, p = cm.shape
    s = g // _S5_SET
    c4 = cm.reshape(s, _S5_SET, c, p).transpose(0, 1, 3, 2)
    eye = jnp.eye(_S5_SET, dtype=cm.dtype)
    return (c4[:, :, :, None, :] * eye[None, :, None, :, None]).reshape(s, _S5_SET * p, _S5_SET * c)


def _bd_out_grad(d, p, c):
    s = d.shape[0]
    eye = jnp.eye(_S5_SET, dtype=d.dtype)
    d5 = d.reshape(s, _S5_SET, p, _S5_SET, c) * eye[None, :, None, :, None]
    return d5.sum(axis=3).transpose(0, 1, 3, 2).reshape(s * _S5_SET, c, p)


def _s5glu_fwd(y, wglu, bglu):
    t, w5 = y.shape
    tm = _tile(t, 256, 8)

    def fn(yb, wb, bb):
        z1 = _gelu(yb)
        a = _dg(z1, wb, _NN) + bb
        return (z1 * _sigmoid(a),)

    return _rowcall(fn, [y, wglu, bglu], [_rb(tm, w5), _full(wglu.shape), _full((1, w5))],
                    [_sds((t, w5), _BF)], [_rb(tm, w5)], [False], grid=(1, t // tm), name="s5glu_fwd")[0]


def _s5glu_bwd(y, dz, wglu, bglu):
    t, w5 = y.shape
    tm = _tile(t, 256, 8)

    def fn(yb, dzb, wb, bb):
        dzb = dzb.astype(_F32)
        z1, gelu_vjp = jax.vjp(_gelu, yb)
        sig = _sigmoid(_dg(z1, wb, _NN) + bb)
        da = dzb * z1 * sig * (1.0 - sig)
        dz1 = dzb * sig + _dg(da, wb, _NT)
        (dy,) = gelu_vjp(dz1)
        return dy, _dg(z1, da, _TN), jnp.sum(da, axis=0, keepdims=True)

    return _rowcall(fn, [y, dz, wglu, bglu], [_rb(tm, w5), _rb(tm, w5), _full(wglu.shape), _full((1, w5))],
                    [_sds((t, w5), _F32), _sds((w5, w5), _F32), _sds((1, w5), _F32)],
                    [_rb(tm, w5), _full((w5, w5)), _full((1, w5))], [False, True, True],
                    grid=(1, t // tm), name="s5glu_bwd")


_LEVELS = (6, 5, 4, 3, 2, 1)


def _tri_stack(c):
    t = jnp.arange(c, dtype=jnp.int32)[:, None]
    j = jnp.arange(c, dtype=jnp.int32)[None, :]
    low = (j <= t).astype(_F32)
    mats = [low]
    for sh in _LEVELS:
        r = ((t >> sh) << sh) + ((1 << (sh - 1)) - 1)
        mats.append(low - (j <= r).astype(_F32))
    stack = jnp.concatenate(mats, axis=0).astype(_BF)
    return stack, stack.T


def _split_dot(mat, x):
    l = x.shape[1]
    hi = x.astype(_BF)
    lo = (x - hi.astype(_F32)).astype(_BF)
    out = jnp.dot(mat, jnp.concatenate([hi, lo], axis=1), preferred_element_type=_F32)
    return out[:, :l] + out[:, l:]


@jax.custom_vjp
def _decay_sums(lf, tri, tri_t):
    c = lf.shape[0]
    out = _split_dot(tri, lf)
    return tuple(out[k * c:(k + 1) * c] for k in range(len(_LEVELS) + 1))


def _decay_sums_f(lf, tri, tri_t):
    return _decay_sums(lf, tri, tri_t), (tri, tri_t)


def _decay_sums_b(res, gs):
    tri, tri_t = res
    return _split_dot(tri_t, jnp.concatenate(gs, axis=0)), jnp.zeros_like(tri), jnp.zeros_like(tri_t)


_decay_sums.defvjp(_decay_sums_f, _decay_sums_b)


def _hgrn_chunk(qi, fi, vi, gi, st, lb, ng, tri, tri_t):
    heads = range(len(qi))
    c = qi[0].shape[0]
    row = lax.broadcasted_iota(jnp.int32, (c, 1), 0)
    tt = lax.broadcasted_iota(jnp.int32, (c, c), 0)
    ss = lax.broadcasted_iota(jnp.int32, (c, c), 1)
    q = [_silu(qi[h]) for h in heads]
    lf = [jnp.log(lb[h] + (1.0 - lb[h]) * _sigmoid(fi[h])) for h in heads]
    k = [(1.0 - lb[h]) * _sigmoid(-fi[h]) for h in heads]
    sums = [_decay_sums(lf[h], tri, tri_t) for h in heads]
    btot = [jnp.sum(lf[h], axis=0, keepdims=True) for h in heads]
    inter = [_bdot_nt(q[h] * jnp.exp(sums[h][0]), st[h]) for h in heads]
    sc = [jnp.where(tt == ss, jnp.sum(q[h] * k[h], axis=1, keepdims=True), 0.0) for h in heads]
    for li, sh in enumerate(_LEVELS):
        upper = ((row >> (sh - 1)) & 1) == 1
        same = (tt >> sh) == (ss >> sh)
        qm = [jnp.where(upper, q[h] * jnp.exp(jnp.where(upper, sums[h][li + 1], 0.0)), 0.0) for h in heads]
        km = [jnp.where(upper, 0.0, k[h] * jnp.exp(jnp.where(upper, 0.0, -sums[h][li + 1]))) for h in heads]
        prod = [_bdot_nt(qm[h], km[h]) for h in heads]
        sc = [sc[h] + jnp.where(same, prod[h], 0.0) for h in heads]
    o = [inter[h] + _bdot(sc[h], vi[h]) for h in heads]
    st_new = [st[h] * jnp.exp(btot[h]) + _bdot_tn(vi[h], k[h] * jnp.exp(btot[h] - sums[h][0])) for h in heads]
    og = [o[h] * lax.rsqrt(jnp.mean(o[h] * o[h], axis=1, keepdims=True) + _RMS_EPS) * ng[h] * _silu(gi[h])
          for h in heads]
    return tuple(og), tuple(st_new)


def _head_groups(hp):
    g = min(_HGRN_GROUP, hp)
    return [list(range(k, min(k + g, hp))) for k in range(0, hp, g)]


def _hgrn_geom(t, w5, hw):
    hp = _HGRN_HP if (hw // _LANE) % _HGRN_HP == 0 and w5 % (_LANE * _HGRN_HP) == 0 else 1
    seg = min(t, _HGRN_SEG)
    return hp, hp * _LANE, seg, t // seg


def _hgrn_in_specs(seg, wd, w5, hw, tmap):
    nhp = hw // wd
    qb = w5 // wd
    return [pl.BlockSpec((seg, wd), (lambda h, s, k=k: (tmap(s), qb + k * nhp + h))) for k in range(4)]


def _hgrn_fwd(proj, lb, ng, t, w5, hw, rider=None):
    assert _CHUNK == 64
    hp, wd, seg, nseg = _hgrn_geom(t, w5, hw)
    ncs = seg // _CHUNK
    vec = pl.BlockSpec((1, wd), lambda h, s: (0, h))

    tri, tri_t = _tri_stack(_CHUNK)

    def body(q_ref, f_ref, i_ref, g_ref, lb_ref, ng_ref, tri_ref, trit_ref, og_ref, st_ref, s_scr):
        @pl.when(pl.program_id(1) == 0)
        def _():
            s_scr[...] = jnp.zeros_like(s_scr)

        tri_v, trit_v = tri_ref[...], trit_ref[...]

        def step(ci, carry):
            r = pl.ds(pl.multiple_of(ci * _CHUNK, _CHUNK), _CHUNK)
            for grp in _head_groups(hp):
                lns = [slice(hh * _LANE, (hh + 1) * _LANE) for hh in grp]
                for hh in grp:
                    st_ref[hh, ci] = s_scr[hh]
                ogs, sns = _hgrn_chunk(tuple(q_ref[r, ln] for ln in lns), tuple(f_ref[r, ln] for ln in lns),
                                       tuple(i_ref[r, ln] for ln in lns), tuple(g_ref[r, ln] for ln in lns),
                                       tuple(s_scr[hh] for hh in grp), tuple(lb_ref[:, ln] for ln in lns),
                                       tuple(ng_ref[:, ln] for ln in lns), tri_v, trit_v)
                for hh, ln, og, sn in zip(grp, lns, ogs, sns):
                    og_ref[r, ln] = og.astype(_BF)
                    s_scr[hh] = sn
            return carry

        lax.fori_loop(0, ncs, step, 0)

    kw = dict(name="hgrn_fwd", grid=(hw // wd, nseg),
              in_specs=_hgrn_in_specs(seg, wd, w5, hw, lambda s: s) + [
                  vec, vec, pl.BlockSpec(tri.shape, lambda h, s: (0, 0)), pl.BlockSpec(tri_t.shape, lambda h, s: (0, 0))],
              out_specs=[pl.BlockSpec((seg, wd), lambda h, s: (s, h)),
                         pl.BlockSpec((hp, ncs, _LANE, _LANE), lambda h, s: (h, s, 0, 0))],
              out_shape=[_sds((t, hw), _BF), _sds((hw // _LANE, t // _CHUNK, _LANE, _LANE), _F32)],
              scratch_shapes=[pltpu.VMEM((hp, _LANE, _LANE), _F32)])
    args = (proj, proj, proj, proj, lb, ng, tri, tri_t)
    if rider is None:
        return _call(body, compiler_params=_cparams("parallel", "arbitrary"), **kw)(*args), []
    return _call_riding(body, rider, args=args, **kw)


def _hgrn_bwd(proj, lb, ng, st_all, dog, t, w5, hw):
    hp, wd, seg, nseg = _hgrn_geom(t, w5, hw)
    ncs = seg // _CHUNK

    def rev(s):
        return nseg - 1 - s

    vec = pl.BlockSpec((1, wd), lambda h, s: (0, h))
    col = pl.BlockSpec((seg, wd), lambda h, s: (rev(s), h))

    tri, tri_t = _tri_stack(_CHUNK)

    def body(q_ref, f_ref, i_ref, g_ref, lb_ref, ng_ref, tri_ref, trit_ref, st_ref, dog_ref,
             dq_ref, df_ref, di_ref, dg_ref, dlb_ref, dng_ref, ds_scr):
        @pl.when(pl.program_id(1) == 0)
        def _():
            ds_scr[...] = jnp.zeros_like(ds_scr)
            dlb_ref[...] = jnp.zeros_like(dlb_ref)
            dng_ref[...] = jnp.zeros_like(dng_ref)

        tri_v, trit_v = tri_ref[...], trit_ref[...]

        def step(kk, carry):
            ci = ncs - 1 - kk
            r = pl.ds(pl.multiple_of(ci * _CHUNK, _CHUNK), _CHUNK)
            for grp in _head_groups(hp):
                lns = [slice(hh * _LANE, (hh + 1) * _LANE) for hh in grp]
                _, vjp = jax.vjp(_hgrn_chunk, tuple(q_ref[r, ln] for ln in lns), tuple(f_ref[r, ln] for ln in lns),
                                 tuple(i_ref[r, ln] for ln in lns), tuple(g_ref[r, ln] for ln in lns),
                                 tuple(st_ref[hh, ci] for hh in grp), tuple(lb_ref[:, ln] for ln in lns),
                                 tuple(ng_ref[:, ln] for ln in lns), tri_v, trit_v)
                dq, df, di, dg, ds, dlb, dng = vjp((tuple(dog_ref[r, ln].astype(_F32) for ln in lns),
                                                    tuple(ds_scr[hh] for hh in grp)))[:7]
                for n_, (hh, ln) in enumerate(zip(grp, lns)):
                    dq_ref[r, ln] = dq[n_].astype(_BF)
                    df_ref[r, ln] = df[n_].astype(_BF)
                    di_ref[r, ln] = di[n_].astype(_BF)
                    dg_ref[r, ln] = dg[n_].astype(_BF)
                    ds_scr[hh] = ds[n_]
                    dlb_ref[:, ln] += dlb[n_]
                    dng_ref[:, ln] += dng[n_]
            return carry

        lax.fori_loop(0, ncs, step, 0)

    return _call(body, name="hgrn_bwd", grid=(hw // wd, nseg),
                 in_specs=_hgrn_in_specs(seg, wd, w5, hw, rev) + [
                     vec, vec, pl.BlockSpec(tri.shape, lambda h, s: (0, 0)), pl.BlockSpec(tri_t.shape, lambda h, s: (0, 0)),
                     pl.BlockSpec((hp, ncs, _LANE, _LANE), lambda h, s: (h, rev(s), 0, 0)), col],
                 out_specs=[col, col, col, col, vec, vec],
                 out_shape=[_sds((t, hw), _BF)] * 4 + [_sds((1, hw), _F32)] * 2,
                 scratch_shapes=[pltpu.VMEM((hp, _LANE, _LANE), _F32)],
                 compiler_params=_cparams("parallel", "arbitrary"))(
                     proj, proj, proj, proj, lb, ng, tri, tri_t, st_all, dog)


def _lb_of(logits):
    mx = jnp.max(logits, axis=0, keepdims=True)
    e = jnp.exp(logits - mx)
    sm = e / jnp.sum(e, axis=0, keepdims=True)
    row = lax.broadcasted_iota(jnp.int32, logits.shape, 0)
    return jnp.sum(jnp.where(row == 0, sm, 0.0), axis=0, keepdims=True)


def _lb_prep(logits):
    def body(l_ref, o_ref):
        o_ref[...] = _lb_of(l_ref[...])

    return _call(body, name="lb_prep", out_shape=_sds((1, logits.shape[1]), _F32))(logits)


def _lb_prep_bwd(logits, dlb):
    def body(l_ref, d_ref, o_ref):
        _, vjp = jax.vjp(_lb_of, l_ref[...])
        o_ref[...] = vjp(d_ref[...])[0]

    return _call(body, name="lb_prep_bwd", out_shape=_sds(logits.shape, _F32))(logits, dlb)


def _merge_fwd(proj, ys, yh, t, d, gs_off):
    tm = _tile(t, 256, 8)
    w = _tile(d, 1024)
    nb = d // w

    def fn(gs, gh, a, b):
        return (_sigmoid(gs) * a + _sigmoid(gh) * b,)

    return _rowcall(fn, [proj, proj, ys, yh], [_rb(tm, w, gs_off // w), _rb(tm, w, gs_off // w + nb), _rb(tm, w), _rb(tm, w)],
                    [_sds((t, d), _BF)], [_rb(tm, w)], [False], grid=(nb, t // tm), name="merge_fwd")[0]


def _merge_bwd(proj, ys, yh, dm, t, d, gs_off):
    tm = _tile(t, 256, 8)
    w = _tile(d, 1024)
    nb = d // w

    def fn(gs, gh, a, b, g):
        g = g.astype(_F32)
        s1 = _sigmoid(gs)
        s2 = _sigmoid(gh)
        return g * s1, g * s2, g * a * s1 * (1.0 - s1), g * b * s2 * (1.0 - s2)

    return _rowcall(fn, [proj, proj, ys, yh, dm],
                    [_rb(tm, w, gs_off // w), _rb(tm, w, gs_off // w + nb), _rb(tm, w), _rb(tm, w), _rb(tm, w)],
                    [_sds((t, d), _BF)] * 4, [_rb(tm, w)] * 4, [False] * 4, grid=(nb, t // tm), name="merge_bwd")


_HALO = 16


def _prev_rows(up_prev, is_first):
    p1 = jnp.where(is_first, 0.0, up_prev[_HALO - 1:_HALO, :])
    p2 = jnp.where(is_first, 0.0, up_prev[_HALO - 2:_HALO - 1, :])
    return p1, p2


def _causal_taps(cur, p1, p2):
    row = lax.broadcasted_iota(jnp.int32, cur.shape, 0)
    s1 = jnp.where(row == 0, p1, _shift_down(cur, 1))
    s2 = jnp.where(row == 0, p2, jnp.where(row == 1, p1, _shift_down(cur, 2)))
    return s1, s2


def _pair_perm(nbh, per):
    def perm(n_):
        big = n_ // per
        return (2 * (big % nbh) + big // nbh) * per + n_ % per
    return perm


def _conv_specs(tm, w, nb_half, t):
    r8 = tm // _HALO
    cur_g = pl.BlockSpec((tm, w), lambda j, i: (i, 2 * j))
    cur_v = pl.BlockSpec((tm, w), lambda j, i: (i, 2 * j + 1))
    prev_g = pl.BlockSpec((_HALO, w), lambda j, i: (jnp.maximum(i * r8 - 1, 0), 2 * j))
    prev_v = pl.BlockSpec((_HALO, w), lambda j, i: (jnp.maximum(i * r8 - 1, 0), 2 * j + 1))
    w_g = pl.BlockSpec((3, w), lambda j, i: (0, j))
    w_v = pl.BlockSpec((3, w), lambda j, i: (0, nb_half + j))
    b_g = pl.BlockSpec((1, w), lambda j, i: (0, j))
    b_v = pl.BlockSpec((1, w), lambda j, i: (0, nb_half + j))
    return cur_g, cur_v, prev_g, prev_v, w_g, w_v, b_g, b_v


def _conv_of(cur, prev8, wt, bias, is_first):
    cur, prev8 = cur.astype(_F32), prev8.astype(_F32)
    p1, p2 = _prev_rows(prev8, is_first)
    s1, s2 = _causal_taps(cur, p1, p2)
    return bias + wt[0:1, :] * s2 + wt[1:2, :] * s1 + wt[2:3, :] * cur


def _convact_fwd(up, cw, cb, t, dff):
    tm = _tile(t, 512, _HALO)
    w = _tile(dff, 1408)
    nbh = dff // w
    sp = _conv_specs(tm, w, nbh, t)

    def body(ug, uv, pg, pv, wg, wv, bg, bv, o_ref):
        first = pl.program_id(1) == 0
        gate = _conv_of(ug[...], pg[...], wg[...], bg[...], first)
        val = _conv_of(uv[...], pv[...], wv[...], bv[...], first)
        o_ref[...] = (_silu(gate) * val).astype(_BF)

    return _call(body, name="convact_fwd", grid=(nbh, t // tm), in_specs=list(sp),
                 out_specs=pl.BlockSpec((tm, w), lambda j, i: (i, j)), out_shape=_sds((t, dff), _BF),
                 compiler_params=_cparams("parallel", "arbitrary"))(up, up, up, up, cw, cw, cb, cb)


def _convact_bwd(up, dact, cw, cb, t, dff):
    tm = _tile(t, 256, _HALO)
    w = _tile(dff, 1408)
    nbh = dff // w
    r8 = tm // _HALO
    nt = t // tm
    last8 = t // _HALO - 1

    def triple(off):
        return [pl.BlockSpec((tm, w), lambda j, i: (i, 2 * j + off)),
                pl.BlockSpec((_HALO, w), lambda j, i: (jnp.minimum((i + 1) * r8, last8), 2 * j + off)),
                pl.BlockSpec((_HALO, w), lambda j, i: (jnp.maximum(i * r8 - 1, 0), 2 * j + off))]

    def body(ug, ugn, ugp, uv, uvn, uvp, wg_ref, wv_ref, bg_ref, bv_ref, da_ref, dan_ref, du_ref, dw_ref, db_ref):
        i = pl.program_id(1)
        first = i == 0
        is_last = i == nt - 1
        gate = _conv_of(jnp.concatenate([ug[...], ugn[...]], axis=0), ugp[...], wg_ref[...], bg_ref[...], first)
        val = _conv_of(jnp.concatenate([uv[...], uvn[...]], axis=0), uvp[...], wv_ref[...], bv_ref[...], first)
        da = jnp.concatenate([da_ref[...], dan_ref[...]], axis=0).astype(_F32)
        row = lax.broadcasted_iota(jnp.int32, da.shape, 0)
        da = jnp.where(jnp.logical_and(row >= tm, is_last), 0.0, da)
        sg = _sigmoid(gate)
        halves = ((da * val * sg * (1.0 + gate * (1.0 - sg)), wg_ref, ug, ugp),
                  (da * gate * sg, wv_ref, uv, uvp))

        @pl.when(first)
        def _():
            dw_ref[...] = jnp.zeros_like(dw_ref)
            db_ref[...] = jnp.zeros_like(db_ref)

        for h, (dc, w_ref, u_ref, p_ref) in enumerate(halves):
            ln = slice(h * w, (h + 1) * w)
            wt = w_ref[...]
            du = wt[2:3, :] * dc + wt[1:2, :] * _shift_up(dc, 1) + wt[0:1, :] * _shift_up(dc, 2)
            du_ref[:, ln] = du[0:tm, :].astype(_BF)
            dcm = dc[0:tm, :]
            cur = u_ref[...].astype(_F32)
            p1, p2 = _prev_rows(p_ref[...].astype(_F32), first)
            s1, s2 = _causal_taps(cur, p1, p2)
            dw_ref[0:1, ln] += jnp.sum(dcm * s2, axis=0, keepdims=True)
            dw_ref[1:2, ln] += jnp.sum(dcm * s1, axis=0, keepdims=True)
            dw_ref[2:3, ln] += jnp.sum(dcm * cur, axis=0, keepdims=True)
            db_ref[:, ln] += jnp.sum(dcm, axis=0, keepdims=True)

    in_specs = (triple(0) + triple(1)
                + [pl.BlockSpec((3, w), lambda j, i: (0, j)), pl.BlockSpec((3, w), lambda j, i: (0, nbh + j)),
                   pl.BlockSpec((1, w), lambda j, i: (0, j)), pl.BlockSpec((1, w), lambda j, i: (0, nbh + j)),
                   pl.BlockSpec((tm, w), lambda j, i: (i, j)),
                   pl.BlockSpec((_HALO, w), lambda j, i: (jnp.minimum((i + 1) * r8, last8), j))])
    dup, dw_p, db_p = _call(
        body, name="convact_bwd", grid=(nbh, nt), in_specs=in_specs,
        out_specs=[pl.BlockSpec((tm, 2 * w), lambda j, i: (i, j)), pl.BlockSpec((3, 2 * w), lambda j, i: (0, j)),
                   pl.BlockSpec((1, 2 * w), lambda j, i: (0, j))],
        out_shape=[_sds((t, 2 * dff), _BF), _sds((3, 2 * dff), _F32), _sds((1, 2 * dff), _F32)],
        compiler_params=_cparams("parallel", "arbitrary"))(up, up, up, up, up, up, cw, cw, cb, cb, dact, dact)

    def natural(v):
        k = v.shape[0]
        return v.reshape(k, nbh, 2, w).transpose(0, 2, 1, 3).reshape(k, 2 * dff)

    return dup, natural(dw_p), natural(db_p)


def _me():
    return lax.axis_index("x"), lax.axis_index("y"), lax.axis_index("c")


def _other_chips(x, y):
    return [(1 - x, y), (x, 1 - y), (1 - x, 1 - y)]


def _rcopy(src, dst, ssem, rsem, dev):
    return pltpu.make_async_remote_copy(src_ref=src, dst_ref=dst, send_sem=ssem, recv_sem=rsem,
                                        device_id=dev, device_id_type=_MESH)


def _cast_into_slot(w, sel):
    r, c = w.shape
    tm = _tile(r, 256, 16)

    def body(sel_ref, w_ref, o_ref):
        o_ref[...] = w_ref[...].astype(_BF)

    gs = pltpu.PrefetchScalarGridSpec(
        num_scalar_prefetch=1, grid=(r // tm,),
        in_specs=[pl.BlockSpec((tm, c), lambda i, s: (i, 0))],
        out_specs=pl.BlockSpec((None, tm, c), lambda i, s: (s[0], i, 0)))
    return _call(body, name="cast_into_slot", grid_spec=gs, out_shape=_sds((_NCHIP, r, c), _BF),
                 compiler_params=_cparams("parallel"))(sel, w)


class _Plan:
    def __init__(self, ins, outs, aliases, sems, start, finish):
        self.ins, self.outs, self.aliases, self.sems, self.start, self.finish = ins, outs, aliases, sems, start, finish


def _run_plan(plan, name):
    ni, no = len(plan.ins), len(plan.outs)

    def body(*refs):
        rin, rout, sems = refs[:ni], refs[ni:ni + no], refs[ni + no:]
        plan.start(rin, rout, sems)
        plan.finish(rin, rout, sems)

    return _call(body, name=name, in_specs=[_ANY] * ni, out_specs=[_ANY] * no, out_shape=list(plan.outs),
                 input_output_aliases=dict(plan.aliases), scratch_shapes=list(plan.sems))(*plan.ins)


def _call_riding(body, rider, *, name, grid, in_specs, out_specs, out_shape, scratch_shapes, args):
    n_in, n_out, n_scr = len(in_specs), len(out_specs), len(scratch_shapes)
    n_rin, n_rout = len(rider.ins), len(rider.outs)

    def wrapped(*refs):
        ins, rin = refs[:n_in], refs[n_in:n_in + n_rin]
        o0 = n_in + n_rin
        outs, rout = refs[o0:o0 + n_out], refs[o0 + n_out:o0 + n_out + n_rout]
        s0 = o0 + n_out + n_rout
        scratch, sems = refs[s0:s0 + n_scr], refs[s0 + n_scr:]
        first = functools.reduce(jnp.logical_and, [pl.program_id(k) == 0 for k in range(len(grid))])
        last = functools.reduce(jnp.logical_and, [pl.program_id(k) == grid[k] - 1 for k in range(len(grid))])

        @pl.when(first)
        def _():
            rider.start(rin, rout, sems)

        body(*ins, *outs, *scratch)

        @pl.when(last)
        def _():
            rider.finish(rin, rout, sems)

    res = _call(wrapped, name=name, grid=grid, in_specs=list(in_specs) + [_ANY] * n_rin,
                out_specs=list(out_specs) + [_ANY] * n_rout, out_shape=list(out_shape) + list(rider.outs),
                input_output_aliases={n_in + k: n_out + v for k, v in rider.aliases.items()},
                scratch_shapes=list(scratch_shapes) + list(rider.sems),
                compiler_params=_cparams(*(["arbitrary"] * len(grid))))(*args, *rider.ins)
    return list(res[:n_out]), list(res[n_out:])


def _gather_plan(bufs, direct, rel=(0, 1, 2)):
    n, nd = len(bufs), len(direct)

    def where():
        x, y, c = _me()
        return c, 2 * x + y, _other_chips(x, y), (x, y, 1 - c)

    def picked(chips):
        return [(j, chips[j]) for j in rel]

    def piece(outs, a, chip, h):
        r2 = bufs[a].shape[1] // 2
        return outs[a].at[chip, pl.ds(h * r2, r2)]

    def send(outs, sems, a, j, me, c, chip):
        return _rcopy(piece(outs, a, me, c), piece(outs, a, me, c), sems[0].at[3 * a + j], sems[1].at[3 * a + j],
                      (chip[0], chip[1], c))

    def forward(outs, sems, a, j, pc, c, sib):
        return _rcopy(piece(outs, a, pc, c), piece(outs, a, pc, c), sems[2].at[3 * a + j], sems[3].at[3 * a + j], sib)

    def dsend(dins, douts, sems, a, j, me, c, chip):
        return _rcopy(dins[a], douts[a].at[me], sems[4].at[3 * a + j], sems[5].at[3 * a + j], (chip[0], chip[1], c))

    def start(rin, rout, sems):
        outs, dins, douts = rout[:n], rin[n:], rout[n:]
        c, me, chips, _ = where()
        for a in range(n):
            for j, chip in picked(chips):
                send(outs, sems, a, j, me, c, chip).start()
        for a in range(nd):
            pltpu.make_async_copy(dins[a], douts[a].at[me], sems[6].at[a]).start()
            for j, chip in enumerate(chips):
                dsend(dins, douts, sems, a, j, me, c, chip).start()

    def finish(rin, rout, sems):
        outs, dins, douts = rout[:n], rin[n:], rout[n:]
        c, me, chips, sib = where()
        for a in range(n):
            for j, (cx, cy) in picked(chips):
                pc = 2 * cx + cy
                _rcopy(piece(outs, a, me, c), piece(outs, a, pc, c), sems[0].at[3 * a + j], sems[1].at[3 * a + j],
                       (cx, cy, c)).wait_recv()
                forward(outs, sems, a, j, pc, c, sib).start()
        for a in range(n):
            for j, (cx, cy) in picked(chips):
                pc = 2 * cx + cy
                _rcopy(piece(outs, a, pc, 1 - c), piece(outs, a, pc, 1 - c), sems[2].at[3 * a + j],
                       sems[3].at[3 * a + j], sib).wait_recv()
        for a in range(nd):
            for j, (cx, cy) in enumerate(chips):
                _rcopy(dins[a], douts[a].at[2 * cx + cy], sems[4].at[3 * a + j], sems[5].at[3 * a + j],
                       (cx, cy, c)).wait_recv()
        for a in range(n):
            for j, (cx, cy) in picked(chips):
                send(outs, sems, a, j, me, c, (cx, cy)).wait_send()
                forward(outs, sems, a, j, 2 * cx + cy, c, sib).wait_send()
        for a in range(nd):
            pltpu.make_async_copy(dins[a], douts[a].at[me], sems[6].at[a]).wait()
            for j, chip in enumerate(chips):
                dsend(dins, douts, sems, a, j, me, c, chip).wait_send()

    dma = pltpu.SemaphoreType.DMA
    return _Plan(list(bufs) + list(direct),
                 [_sds(b.shape, b.dtype) for b in bufs] + [_sds((_NCHIP,) + s.shape, s.dtype) for s in direct],
                 {a: a for a in range(n)},
                 [dma((3 * max(n, 1),)), dma((3 * max(n, 1),)), dma((3 * max(n, 1),)), dma((3 * max(n, 1),)),
                  dma((3 * max(nd, 1),)), dma((3 * max(nd, 1),)), dma((max(nd, 1),))], start, finish)


def _mm_shard_order(a, mid_plan, end_plan, sel, *, tm, tn, name):
    b = mid_plan.ins[0]
    s, k, ns = b.shape
    m = a.shape[0]
    npb = ns // tn
    nj, ni = s * npb, m // tm
    n_in = (len(mid_plan.ins), len(end_plan.ins))
    n_out = (len(mid_plan.outs), len(end_plan.outs))
    n_sem = (len(mid_plan.sems), len(end_plan.sems))

    def shard_of(jj, q):
        return jnp.bitwise_xor(q[0], jj // npb)

    def body(q_ref, a_ref, *rest):
        rin = (rest[:n_in[0]], rest[n_in[0]:sum(n_in)])
        o_ref = rest[sum(n_in)]
        o0 = sum(n_in) + 1
        rout = (rest[o0:o0 + n_out[0]], rest[o0 + n_out[0]:o0 + sum(n_out)])
        vbuf, bsem = rest[o0 + sum(n_out)], rest[o0 + sum(n_out) + 1]
        s0 = o0 + sum(n_out) + 2
        sems = (rest[s0:s0 + n_sem[0]], rest[s0 + n_sem[0]:s0 + sum(n_sem)])
        b_ref = rout[0][0]
        j, i = pl.program_id(0), pl.program_id(1)

        def tile_copy(jj, slot):
            return pltpu.make_async_copy(b_ref.at[shard_of(jj, q_ref), :, pl.ds((jj % npb) * tn, tn)],
                                         vbuf.at[slot], bsem.at[slot])

        @pl.when(jnp.logical_and(j == 0, i == 0))
        def _():
            mid_plan.start(rin[0], rout[0], sems[0])
            end_plan.start(rin[1], rout[1], sems[1])
            tile_copy(0, 0).start()

        @pl.when(i == 0)
        def _():
            @pl.when(j == (s - 1) * npb - 1)
            def _():
                mid_plan.finish(rin[0], rout[0], sems[0])

            @pl.when(j + 1 < nj)
            def _():
                tile_copy(j + 1, (j + 1) % 2).start()

            tile_copy(j, j % 2).wait()

        o_ref[...] = _dg(a_ref[...], vbuf[j % 2], _NN)

        @pl.when(jnp.logical_and(j == nj - 1, i == ni - 1))
        def _():
            end_plan.finish(rin[1], rout[1], sems[1])

    assert s == _NCHIP and nj >= 2
    n_rin, n_rout = sum(n_in), sum(n_out)
    gs = pltpu.PrefetchScalarGridSpec(
        num_scalar_prefetch=1, grid=(nj, ni),
        in_specs=[pl.BlockSpec((tm, k), lambda j, i, q: (i, 0))] + [_ANY] * n_rin,
        out_specs=[pl.BlockSpec((tm, tn), lambda j, i, q: (i, shard_of(j, q) * npb + j % npb))] + [_ANY] * n_rout,
        scratch_shapes=[pltpu.VMEM((2, k, tn), _BF), pltpu.SemaphoreType.DMA((2,))] + list(mid_plan.sems) + list(end_plan.sems))
    aliases = {2 + kk: 1 + v for kk, v in mid_plan.aliases.items()}
    aliases.update({2 + n_in[0] + kk: 1 + n_out[0] + v for kk, v in end_plan.aliases.items()})
    res = _call(body, name=name, grid_spec=gs, out_shape=[_sds((m, s * ns), _F32)] + list(mid_plan.outs) + list(end_plan.outs),
                input_output_aliases=aliases,
                compiler_params=_cparams("arbitrary", "arbitrary"))(sel, a, *mid_plan.ins, *end_plan.ins)
    return res[0], list(res[1:1 + n_out[0]]), list(res[1 + n_out[0]:])


def _swap_halves(grads, name):
    n = len(grads)

    def body(*refs):
        ins, outs = refs[:n], refs[n:2 * n]
        ssem, rsem = refs[2 * n:]
        x, y, c = _me()
        sib = (x, y, 1 - c)
        cps = []
        for a in range(n):
            cp = _rcopy(ins[a].at[1 - c], outs[a], ssem.at[a], rsem.at[a], sib)
            cp.start()
            cps.append(cp)
        for cp in cps:
            cp.wait_recv()
        for cp in cps:
            cp.wait_send()

    dma = pltpu.SemaphoreType.DMA
    return _call(body, name=name, in_specs=[_ANY] * n, out_specs=[_ANY] * n,
                 out_shape=[_sds(g.shape[1:], g.dtype) for g in grads],
                 scratch_shapes=[dma((n,)), dma((n,))])(*grads)


def _add_pairs(grads, theirs, sel):
    _, s, r, c2 = grads.shape
    a3 = grads.reshape(2, s * r, c2)
    b2 = theirs.reshape(s * r, c2)
    tm = _tile(s * r, 512, 16)

    def body(sel_ref, a_ref, b_ref, o_ref):
        o_ref[...] = (a_ref[...].astype(_F32) + b_ref[...].astype(_F32)).astype(_BF)

    gs = pltpu.PrefetchScalarGridSpec(
        num_scalar_prefetch=1, grid=(s * r // tm,),
        in_specs=[pl.BlockSpec((None, tm, c2), lambda i, q: (q[1], i, 0)), pl.BlockSpec((tm, c2), lambda i, q: (i, 0))],
        out_specs=pl.BlockSpec((tm, c2), lambda i, q: (i, 0)))
    out = _call(body, name="chip_sum", grid_spec=gs, out_shape=_sds((s * r, c2), _BF),
                compiler_params=_cparams("parallel"))(sel, a3, b2)
    return out.reshape(s, r, c2)


def _exchange_plan(sums, small):
    n = len(sums)
    has_small = small is not None

    def where():
        x, y, c = _me()
        peers = [(1 - x if k & 4 else x, 1 - y if k & 2 else y, 1 - c if k & 1 else c) for k in range(1, _NDEV)]
        return c, 4 * x + 2 * y + c, _other_chips(x, y), peers

    def send(rin, rout, sems, a, j, c, chip):
        return _rcopy(rin[a].at[2 * chip[0] + chip[1]], rout[a].at[j], sems[0].at[3 * a + j], sems[1].at[3 * a + j],
                      (chip[0], chip[1], c))

    def small_send(rin, rout, sems, k, dev, peer):
        return _rcopy(rin[n], rout[n].at[dev], sems[2].at[k], sems[3].at[k], peer)

    def start(rin, rout, sems):
        c, dev, chips, peers = where()
        for a in range(n):
            for j, chip in enumerate(chips):
                send(rin, rout, sems, a, j, c, chip).start()
        if has_small:
            pltpu.make_async_copy(rin[n], rout[n].at[dev], sems[4].at[0]).start()
            for k, peer in enumerate(peers):
                small_send(rin, rout, sems, k, dev, peer).start()

    def finish(rin, rout, sems):
        c, dev, chips, peers = where()
        for a in range(n):
            for j, chip in enumerate(chips):
                send(rin, rout, sems, a, j, c, chip).wait_recv()
        if has_small:
            for k, (px, py, pc_) in enumerate(peers):
                _rcopy(rin[n], rout[n].at[4 * px + 2 * py + pc_], sems[2].at[k], sems[3].at[k], (px, py, pc_)).wait_recv()
        for a in range(n):
            for j, chip in enumerate(chips):
                send(rin, rout, sems, a, j, c, chip).wait_send()
        if has_small:
            pltpu.make_async_copy(rin[n], rout[n].at[dev], sems[4].at[0]).wait()
            for k, peer in enumerate(peers):
                small_send(rin, rout, sems, k, dev, peer).wait_send()

    dma = pltpu.SemaphoreType.DMA
    outs = [_sds((3,) + s.shape[1:], s.dtype) for s in sums]
    if has_small:
        outs.append(_sds((_NDEV,) + small.shape, small.dtype))
    return _Plan(list(sums) + ([small] if has_small else []), outs, {},
                 [dma((3 * max(n, 1),)), dma((3 * max(n, 1),)), dma((_NDEV - 1,)), dma((_NDEV - 1,)), dma((1,))],
                 start, finish)


def _shard_sum(sums, recv, sel):
    s, r, c2 = sums.shape
    tm = _tile(r, 256, 16)

    def body(sel_ref, own_ref, rc_ref, o_ref):
        rc = rc_ref[...]
        o_ref[...] = ((own_ref[...].astype(_F32) + rc[0].astype(_F32)) + rc[1].astype(_F32)) + rc[2].astype(_F32)

    gs = pltpu.PrefetchScalarGridSpec(
        num_scalar_prefetch=1, grid=(r // tm,),
        in_specs=[pl.BlockSpec((None, tm, c2), lambda i, q: (q[0], i, 0)),
                  pl.BlockSpec((3, tm, c2), lambda i, q: (0, i, 0))],
        out_specs=pl.BlockSpec((None, tm, c2), lambda i, q: (q[1], i, 0)))
    return _call(body, name="shard_sum", grid_spec=gs, out_shape=_sds((2, r, c2), _F32),
                 compiler_params=_cparams("parallel"))(sel, sums, recv)


def _sum_slots(stack, name):
    k, r, c = stack.shape
    tm = _tile(r, 256, 16 if stack.dtype == _BF else 8)

    def fn(v):
        out = v[0].astype(_F32)
        for i in range(1, k):
            out = out + v[i].astype(_F32)
        return (out,)

    return _rowcall(fn, [stack], [pl.BlockSpec((k, tm, c), lambda j, i: (0, i, 0))], [_sds((r, c), _F32)],
                    [_rb(tm, c)], [False], grid=(1, r // tm), name=name)[0]


def _share_halves(bufs):
    n = len(bufs)

    def body(*refs):
        outs = refs[n:2 * n]
        ssem, rsem = refs[2 * n:]
        x, y, c = _me()
        sib = (x, y, 1 - c)
        cps = []
        for a in range(n):
            cp = _rcopy(outs[a].at[c], outs[a].at[c], ssem.at[a], rsem.at[a], sib)
            cp.start()
            cps.append(cp)
        for a in range(n):
            _rcopy(outs[a].at[c], outs[a].at[1 - c], ssem.at[a], rsem.at[a], sib).wait_recv()
        for cp in cps:
            cp.wait_send()

    dma = pltpu.SemaphoreType.DMA
    return _call(body, name="share_halves", in_specs=[_ANY] * n, out_specs=[_ANY] * n,
                 out_shape=[_sds(b.shape, b.dtype) for b in bufs], input_output_aliases={a: a for a in range(n)},
                 scratch_shapes=[dma((n,)), dma((n,))])(*bufs)


def _adamw_math(w, g, m, v):
    m = _B1 * m + (1.0 - _B1) * g
    v = _B2 * v + (1.0 - _B2) * jnp.square(g)
    m_hat = m / (1.0 - _B1 ** _STEP)
    v_hat = v / (1.0 - _B2 ** _STEP)
    delta = -_LR * (m_hat / (jnp.sqrt(v_hat) + _ADAM_EPS) + _WD * w)
    return delta, m, v


def _adamw_shard(w, g2, m, v, name):
    r, c = w.shape
    c2 = c // 2
    tm = _tile(r, 256, 8)
    blk = pl.BlockSpec((tm, c2), lambda h, i: (i, h))

    def fn(wb, gb, mb, vb):
        return (gb,) + _adamw_math(wb, gb, mb, vb)

    return _rowcall(fn, [w, g2, m, v], [blk, pl.BlockSpec((None, tm, c2), lambda h, i: (h, i, 0)), blk, blk],
                    [_sds((r, c), _F32)] * 4, [blk] * 4, [False] * 4, grid=(2, r // tm), name=name)


def _adamw_whole(w, g, m, v, name):
    r, c = w.shape
    blk = _full((r, c))
    return _rowcall(lambda *a: _adamw_math(*a), [w, g, m, v], [blk] * 4, [_sds((r, c), _F32)] * 3, [blk] * 3,
                    [False] * 3, grid=(1, 1), name=name)


def _adamw_many(ws, g_pack, ms, vs):
    k = len(ws)
    views, offs, off = [], [], 0
    for w in ws:
        n = w.size
        views.append((n // _LANE, _LANE) if n % _LANE == 0 else (1, n))
        offs.append(off)
        off += (n + (-n) % (8 * _LANE)) // _LANE

    def body(*refs):
        g_ref, w_refs, m_refs, v_refs = refs[0], refs[1:1 + k], refs[1 + k:1 + 2 * k], refs[1 + 2 * k:1 + 3 * k]
        outs = refs[1 + 3 * k:]
        for i in range(k):
            r, c = views[i]
            g = g_ref[offs[i]:offs[i] + r, 0:c]
            res = (g,) + _adamw_math(w_refs[i][...], g, m_refs[i][...], v_refs[i][...])
            for o_ref, val in zip(outs[4 * i:4 * i + 4], res):
                o_ref[...] = val

    args = [g_pack] + [a.reshape(views[i]) for grp in (ws, ms, vs) for i, a in enumerate(grp)]
    res = _call(body, name="adamw_small", out_shape=[_sds(views[i], _F32) for i in range(k) for _ in range(4)])(*args)
    return [[res[4 * i + j].reshape(ws[i].shape) for j in range(4)] for i in range(k)]


def _pack(arrs):
    parts = []
    for a in arrs:
        f = a.reshape(-1).astype(_F32)
        pad = (-f.shape[0]) % (8 * _LANE)
        if pad:
            f = jnp.concatenate([f, jnp.zeros((pad,), _F32)])
        parts.append(f)
    return jnp.concatenate(parts).reshape(-1, _LANE)


def kernel(x, ln_mix_g, w_in, s5_a_re, s5_a_im, s5_log_dt, s5_b_re, s5_b_im, s5_c_re, s5_c_im, s5_d, s5_w_glu, s5_b_glu, w_proj_s5, hgrn_lb_logits, hgrn_norm_g, w_proj_hgrn, w_out, ln_ffn_g, w_up, conv_w, conv_b, w_down, ln_final_g, loss_target, m_ln_mix_g, m_w_in, m_s5_a_re, m_s5_a_im, m_s5_log_dt, m_s5_b_re, m_s5_b_im, m_s5_c_re, m_s5_c_im, m_s5_d, m_s5_w_glu, m_s5_b_glu, m_w_proj_s5, m_hgrn_lb_logits, m_hgrn_norm_g, m_w_proj_hgrn, m_w_out, m_ln_ffn_g, m_w_up, m_conv_w, m_conv_b, m_w_down, m_ln_final_g, v_ln_mix_g, v_w_in, v_s5_a_re, v_s5_a_im, v_s5_log_dt, v_s5_b_re, v_s5_b_im, v_s5_c_re, v_s5_c_im, v_s5_d, v_s5_w_glu, v_s5_b_glu, v_w_proj_s5, v_hgrn_lb_logits, v_hgrn_norm_g, v_w_proj_hgrn, v_w_out, v_ln_ffn_g, v_w_up, v_conv_w, v_conv_b, v_w_down, v_ln_final_g):
    assert x.shape[0] == 1 and w_in.shape[0] == 1, "one example per device, one layer"
    t, d = x.shape[1], x.shape[2]
    w5 = s5_w_glu.shape[2]
    hw = hgrn_norm_g.shape[1]
    ng_, np_, gc = s5_b_re.shape[1], s5_b_re.shape[2], s5_b_re.shape[3]
    dff = w_down.shape[1] * _NCHIP
    assert gc * _S5_SET == _LANE and ng_ * gc == w5 and hw % _LANE == 0
    gs_off = w5 + 4 * hw
    ci = lax.axis_index("c")
    xt = x.reshape(t, d)
    tgt = loss_target.reshape(t, d)

    big_names = ["w_in", "s5_w_glu", "w_proj_s5", "w_proj_hgrn", "w_out", "w_up", "w_down"]
    big_w = dict(w_in=w_in[0], s5_w_glu=s5_w_glu[0], w_proj_s5=w_proj_s5[0], w_proj_hgrn=w_proj_hgrn[0],
                 w_out=w_out[0], w_up=w_up[0], w_down=w_down[0])
    chip = 2 * lax.axis_index("x") + lax.axis_index("y")
    sel_chip = jnp.stack([chip, ci]).astype(jnp.int32)
    slots = {k: _cast_into_slot(big_w[k], sel_chip) for k in big_names}
    g_in_part, g_cw = _run_plan(_gather_plan([slots["w_in"]], [conv_w[0]], rel=(0, 1)), "gather_w_in")
    cw = g_cw.transpose(1, 0, 2).reshape(3, 2 * dff)
    cb = conv_b

    tm_big = _tile(t, 1024, 8)

    h1 = _rms_fwd(xt, ln_mix_g, "rms1_fwd")
    nin_s = g_in_part.shape[2]
    proj, (g_in,), (g_glu, g_ps5, g_ph, g_out) = _mm_shard_order(
        h1, _gather_plan([g_in_part], [], rel=(2,)),
        _gather_plan([slots[k] for k in ("s5_w_glu", "w_proj_s5", "w_proj_hgrn", "w_out")], []),
        sel_chip, tm=tm_big, tn=_tile(nin_s, 1152), name="mm_proj")
    wglu = g_glu.reshape(w5, w5)
    wout = g_out.reshape(d, d)

    abar_r, abar_i, coef_r, coef_i = _s5_prep(s5_a_re[0], s5_a_im[0], s5_log_dt.reshape(ng_, 1))
    lanes = ng_ * np_

    def fold_coef(cr_, ci_, bre, bim):
        return cr_[..., None] * bre - ci_[..., None] * bim, cr_[..., None] * bim + ci_[..., None] * bre

    (bf_re, bf_im), fold_vjp = jax.vjp(fold_coef, coef_r, coef_i, s5_b_re[0], s5_b_im[0])
    par = (abar_r.reshape(1, lanes), abar_i.reshape(1, lanes),
           _bd_in(bf_re), _bd_in(bf_im), _bd_out(s5_c_re[0]), -_bd_out(s5_c_im[0]), s5_d.reshape(1, w5))
    (y_s5, car, cai), (g_up_part,) = _s5_fwd(proj, par, t, w5, rider=_gather_plan([slots["w_up"]], [], rel=(0, 1)))
    z = _s5glu_fwd(y_s5, wglu, s5_b_glu)
    ys = _mm(z, g_ps5, "nn", _F32, tm=tm_big, tn=g_ps5.shape[2], tk=w5, name="mm_proj_s5")

    lb = _lb_prep(hgrn_lb_logits)
    (og, st_all), (g_up,) = _hgrn_fwd(proj, lb, hgrn_norm_g, t, w5, hw, rider=_gather_plan([g_up_part], [], rel=(2,)))
    yh = _mm(og, g_ph, "nn", _F32, tm=tm_big, tn=g_ph.shape[2], tk=hw, name="mm_proj_hgrn")

    merged = _merge_fwd(proj, ys, yh, t, d, gs_off)
    x2 = _mm(merged, wout, "nn", _F32, tm=tm_big, tn=_tile(d, 1024), tk=d, res=xt, name="mm_out")
    h2 = _rms_fwd(x2, ln_ffn_g, "rms2_fwd")
    up_s = g_up.shape[2]
    w_conv = _tile(dff, 1408)
    nbh = dff // w_conv
    tn_up = _tile(up_s, 1408)
    tk_dh2 = _tile(up_s, w_conv)
    tn_gwup = _tile(up_s // 2, 1408)
    assert w_conv % tn_up == 0 and w_conv % tk_dh2 == 0 and w_conv % tn_gwup == 0
    up, (g_down,) = _mm(h2, g_up, "nn", _BF, tm=tm_big, tn=tn_up, tk=d, name="mm_up",
                        colperm=_pair_perm(nbh, w_conv // tn_up), rider=_gather_plan([slots["w_down"]], []))
    wdown = g_down.reshape(dff, d)
    act = _convact_fwd(up, cw, cb, t, dff)
    x3 = _mm(act, wdown, "nn", _F32, tm=tm_big, tn=_tile(d, 1024), tk=None, res=x2, name="mm_down")
    loss_part, dx3, dx3b, d_gfin = _loss_head(x3, tgt, ln_final_g.reshape(1, d))

    dact = _mm(dx3b, wdown, "nt", _BF, tm=tm_big, tn=_tile(dff, 1408), tk=d, name="mm_dact")
    r_down = dff // _NCHIP
    gw_down = _mm(act, dx3b, "tn", _BF, tm=_tile(r_down, 1408), tn=_tile(d // 2, 1024), tk=None, halves="rows",
                  name="mm_gw_down")
    def chip_sums(grads, name):
        theirs = _swap_halves(grads, name)
        return [_add_pairs(g, th, sel_chip) for g, th in zip(grads, theirs)]

    (s_down,) = chip_sums([gw_down], "swap_halves_d")
    dup, d_cw, d_cb = _convact_bwd(up, dact, cw, cb, t, dff)
    dh2, (r_down,) = _mm(dup, g_up, "nt", _BF, tm=tm_big, tn=_tile(d, 1024), tk=tk_dh2, name="mm_dh2",
                         colperm=_pair_perm(nbh, w_conv // tk_dh2), rider=_exchange_plan([s_down], None))
    gw_up = _mm(h2, dup, "tn", _BF, tm=_tile(d, 1024), tn=tn_gwup, tk=None, halves="cols", name="mm_gw_up",
                colperm=_pair_perm(nbh, w_conv // tn_gwup))
    dx2, dx2b, d_gffn = _rms_bwd(x2, ln_ffn_g, dh2, dx3, "rms2_bwd")
    dmerged = _mm(dx2b, wout, "nt", _BF, tm=tm_big, tn=_tile(d, 1024), tk=d, name="mm_dmerged")
    gw_out = _mm(merged, dx2b, "tn", _BF, tm=_tile(d // _NCHIP, 1024), tn=_tile(d // 2, 1024), tk=None, halves="rows",
                 name="mm_gw_out")
    dys, dyh, dgs, dgh = _merge_bwd(proj, ys, yh, dmerged, t, d, gs_off)
    ps_s = g_ps5.shape[2]
    dz = _mm(dys, g_ps5, "nt", _BF, tm=tm_big, tn=_tile(w5, 1024), tk=ps_s, name="mm_dz")
    gw_ps5 = _mm(z, dys, "tn", _BF, tm=_tile(w5, 1024), tn=ps_s // 2, tk=None, halves="cols", name="mm_gw_ps5")
    dog = _mm(dyh, g_ph, "nt", _BF, tm=tm_big, tn=_tile(hw, 1024), tk=ps_s, name="mm_dog")
    gw_ph = _mm(og, dyh, "tn", _BF, tm=_tile(hw, 1024), tn=ps_s // 2, tk=None, halves="cols", name="mm_gw_ph")
    dy_s5, gw_glu_full, d_bglu = _s5glu_bwd(y_s5, dz, wglu, s5_b_glu)
    r_glu = w5 // _NCHIP
    gw_glu = gw_glu_full.astype(_BF).reshape(_NCHIP, r_glu, 2, w5 // 2).transpose(2, 0, 1, 3)

    s_glu, s_ps5, s_ph, s_out, s_up = chip_sums([gw_glu, gw_ps5, gw_ph, gw_out, gw_up], "swap_halves_a")
    s5g, (r_glu_, r_ps5, r_ph, r_out, r_up) = _s5_bwd(
        proj, dy_s5, car, cai, par, t, w5, rider=_exchange_plan([s_glu, s_ps5, s_ph, s_out, s_up], None))
    du = s5g[0]
    dq, df, di, dg, d_lb, d_ng = _hgrn_bwd(proj, lb, hgrn_norm_g, st_all, dog, t, w5, hw)
    dproj = jnp.concatenate([du, dq, df, di, dg, dgs, dgh], axis=1)

    d_coef_r, d_coef_i, d_bre, d_bim = fold_vjp((_bd_in_grad(s5g[3], np_, gc), _bd_in_grad(s5g[4], np_, gc)))
    d_are, d_aim, d_ldt = _s5_prep_bwd(s5_a_re[0], s5_a_im[0], s5_log_dt.reshape(ng_, 1),
                                       [s5g[1].reshape(ng_, np_), s5g[2].reshape(ng_, np_), d_coef_r, d_coef_i])
    d_cre = _bd_out_grad(s5g[5], np_, gc)
    d_cim = -_bd_out_grad(s5g[6], np_, gc)
    d_logits = _lb_prep_bwd(hgrn_lb_logits, d_lb)

    small_names = ["s5_a_re", "s5_a_im", "s5_log_dt", "s5_b_re", "s5_b_im", "s5_c_re", "s5_c_im", "s5_d",
                   "s5_b_glu", "hgrn_lb_logits", "hgrn_norm_g", "ln_ffn_g", "conv_b", "ln_final_g", "ln_mix_g"]
    small_w = dict(ln_mix_g=ln_mix_g, s5_a_re=s5_a_re, s5_a_im=s5_a_im, s5_log_dt=s5_log_dt, s5_b_re=s5_b_re,
                   s5_b_im=s5_b_im, s5_c_re=s5_c_re, s5_c_im=s5_c_im, s5_d=s5_d, s5_b_glu=s5_b_glu,
                   hgrn_lb_logits=hgrn_lb_logits, hgrn_norm_g=hgrn_norm_g, ln_ffn_g=ln_ffn_g, conv_b=conv_b,
                   ln_final_g=ln_final_g)
    small_m = dict(ln_mix_g=m_ln_mix_g, s5_a_re=m_s5_a_re, s5_a_im=m_s5_a_im, s5_log_dt=m_s5_log_dt, s5_b_re=m_s5_b_re,
                   s5_b_im=m_s5_b_im, s5_c_re=m_s5_c_re, s5_c_im=m_s5_c_im, s5_d=m_s5_d, s5_b_glu=m_s5_b_glu,
                   hgrn_lb_logits=m_hgrn_lb_logits, hgrn_norm_g=m_hgrn_norm_g, ln_ffn_g=m_ln_ffn_g, conv_b=m_conv_b,
                   ln_final_g=m_ln_final_g)
    small_v = dict(ln_mix_g=v_ln_mix_g, s5_a_re=v_s5_a_re, s5_a_im=v_s5_a_im, s5_log_dt=v_s5_log_dt, s5_b_re=v_s5_b_re,
                   s5_b_im=v_s5_b_im, s5_c_re=v_s5_c_re, s5_c_im=v_s5_c_im, s5_d=v_s5_d, s5_b_glu=v_s5_b_glu,
                   hgrn_lb_logits=v_hgrn_lb_logits, hgrn_norm_g=v_hgrn_norm_g, ln_ffn_g=v_ln_ffn_g, conv_b=v_conv_b,
                   ln_final_g=v_ln_final_g)
    small_g = dict(s5_a_re=d_are, s5_a_im=d_aim, s5_log_dt=d_ldt, s5_b_re=d_bre, s5_b_im=d_bim,
                   s5_c_re=d_cre, s5_c_im=d_cim, s5_d=s5g[7], s5_b_glu=d_bglu, hgrn_lb_logits=d_logits,
                   hgrn_norm_g=d_ng, ln_ffn_g=d_gffn, conv_b=d_cb, ln_final_g=d_gfin)
    like = [small_w[k] for k in small_names]
    assert small_names[-1] == "ln_mix_g"
    pack_a = _pack([small_g[k] for k in small_names[:-1]] + [d_cw])
    gw_in, (r_small_a,) = _mm(h1, dproj, "tn", _BF, tm=_tile(d, 1024), tn=_tile(nin_s // 2, 1152), tk=None, halves="cols",
                              name="mm_gw_in", rider=_exchange_plan([], pack_a))
    (s_in,) = chip_sums([gw_in], "swap_halves_b")
    dh1, (r_in,) = _mm(dproj, g_in, "nt", _BF, tm=tm_big, tn=_tile(d, 1024), tk=None, name="mm_dh1",
                       rider=_exchange_plan([s_in], None))
    dx, _, d_gmix = _rms_bwd(xt, ln_mix_g, dh1, dx2, "rms1_bwd")
    (r_small_b,) = _run_plan(_exchange_plan([], _pack([d_gmix])), "exchange_gmix")
    sums = [s_in, s_glu, s_ps5, s_ph, s_out, s_up, s_down]
    received = [r_in, r_glu_, r_ps5, r_ph, r_out, r_up, r_down]
    halves = [_shard_sum(sm, rc, sel_chip) for sm, rc in zip(sums, received)]
    g_a = _sum_slots(r_small_a, "small_sum_a")
    g_b = _sum_slots(r_small_b, "small_sum_b")
    full = _share_halves(halves)
    w_pack = _pack(like)
    rows_a = w_pack.shape[0] - g_b.shape[0]
    g_small = jnp.concatenate([g_a[:rows_a], g_b], axis=0)
    cs = conv_w.shape[2]
    g_cw_full = g_a[rows_a:].reshape(-1)[:3 * 2 * dff].reshape(3, 2 * dff)
    g_cw = lax.dynamic_slice_in_dim(g_cw_full, chip * cs, cs, axis=1)

    big_m = dict(w_in=m_w_in, s5_w_glu=m_s5_w_glu, w_proj_s5=m_w_proj_s5, w_proj_hgrn=m_w_proj_hgrn, w_out=m_w_out,
                 w_up=m_w_up, w_down=m_w_down)
    big_v = dict(w_in=v_w_in, s5_w_glu=v_s5_w_glu, w_proj_s5=v_w_proj_s5, w_proj_hgrn=v_w_proj_hgrn, w_out=v_w_out,
                 w_up=v_w_up, w_down=v_w_down)
    res = {}
    for k, g2 in zip(big_names, full):
        w2 = big_w[k]
        shp = (1,) + w2.shape
        outs = _adamw_shard(w2, g2, big_m[k][0], big_v[k][0], "adamw_" + k)
        res[k] = [o.reshape(shp) for o in outs]
    sm_outs = _adamw_many(like, g_small, [small_m[k] for k in small_names], [small_v[k] for k in small_names])
    for k, outs in zip(small_names, sm_outs):
        res[k] = outs

    cw_outs = _adamw_whole(conv_w[0], g_cw, m_conv_w[0], v_conv_w[0], "adamw_conv_w")
    res["conv_w"] = [g_cw.reshape(conv_w.shape)] + [o.reshape(conv_w.shape) for o in cw_outs]

    loss = lax.psum(loss_part[0, 0], ("x", "y", "c"))
    order = ["ln_mix_g", "w_in", "s5_a_re", "s5_a_im", "s5_log_dt", "s5_b_re", "s5_b_im", "s5_c_re", "s5_c_im", "s5_d",
             "s5_w_glu", "s5_b_glu", "w_proj_s5", "hgrn_lb_logits", "hgrn_norm_g", "w_proj_hgrn", "w_out", "ln_ffn_g",
             "w_up", "conv_w", "conv_b", "w_down", "ln_final_g"]
    return (loss, dx.reshape(x.shape), *[res[k][0] for k in order], *[res[k][1] for k in order],
            *[res[k][2] for k in order], *[res[k][3] for k in order])
```

```python
import functools

import jax
import jax.numpy as jnp
from jax import lax
from jax.experimental import pallas as pl
from jax.experimental.pallas import tpu as pltpu

_F32 = jnp.float32
_BF = jnp.bfloat16
_RMS_EPS = 1e-6
_S5_MAX_RE = -1e-4
_LR, _B1, _B2, _ADAM_EPS, _WD, _STEP = 0.001, 0.9, 0.999, 1e-08, 0.01, 10
_MESH = pl.DeviceIdType.MESH
_ANY = pl.BlockSpec(memory_space=pl.ANY)
_LANE = 128
_VMEM_LIMIT = 56 * 1024 * 1024
_CHUNK = 64
_S5_TB = 1024
_S5_SET = 8
_HGRN_HP = 8
_HGRN_SEG = 512
_HGRN_GROUP = 8
_NCHIP = 4
_NDEV = 8


def _call(body, **kw):
    return pl.pallas_call(body, **kw)


def _cparams(*sem):
    return pltpu.CompilerParams(dimension_semantics=sem, vmem_limit_bytes=_VMEM_LIMIT)


def _tile(n, pref, unit=_LANE):
    if n <= pref:
        return n
    t = (pref // unit) * unit
    while t >= unit:
        if n % t == 0:
            return t
        t -= unit
    raise ValueError(f"no tile for {n}")


_OPERAND_BYTES = 12 * 1024 * 1024


def _tk_fit(k, tm, tn):
    best = _LANE
    for tk in range(_LANE, k + 1, _LANE):
        if k % tk == 0 and (tm + tn) * tk * 2 <= _OPERAND_BYTES:
            best = tk
    return best if k % _LANE == 0 else k


_NN = ((1,), (0,))
_NT = ((1,), (1,))
_TN = ((0,), (0,))


def _dg(a, b, dims):
    return lax.dot_general(a.astype(_BF), b.astype(_BF), (dims, ((), ())), preferred_element_type=_F32)


@jax.custom_vjp
def _bdot(a, b):
    return _dg(a, b, _NN)


def _bdot_f(a, b):
    return _dg(a, b, _NN), (a, b)


def _bdot_b(res, g):
    a, b = res
    return _dg(g, b, _NT).astype(a.dtype), _dg(a, g, _TN).astype(b.dtype)


_bdot.defvjp(_bdot_f, _bdot_b)


@jax.custom_vjp
def _bdot_nt(a, b):
    return _dg(a, b, _NT)


def _bdot_nt_f(a, b):
    return _dg(a, b, _NT), (a, b)


def _bdot_nt_b(res, g):
    a, b = res
    return _dg(g, b, _NN).astype(a.dtype), _dg(g, a, _TN).astype(b.dtype)


_bdot_nt.defvjp(_bdot_nt_f, _bdot_nt_b)


@jax.custom_vjp
def _bdot_tn(a, b):
    return _dg(a, b, _TN)


def _bdot_tn_f(a, b):
    return _dg(a, b, _TN), (a, b)


def _bdot_tn_b(res, g):
    a, b = res
    return _dg(b, g, _NT).astype(a.dtype), _dg(a, g, _NN).astype(b.dtype)


_bdot_tn.defvjp(_bdot_tn_f, _bdot_tn_b)


_SUBLANES = 8


def _shift_up(x, n):
    r = x.shape[0]
    if n % _SUBLANES == 0:
        return jnp.concatenate([x[n:], jnp.zeros((n,) + x.shape[1:], x.dtype)], axis=0)
    row = lax.broadcasted_iota(jnp.int32, x.shape, 0)
    return jnp.where(row < r - n, pltpu.roll(x, r - n, 0), 0.0)


@functools.partial(jax.custom_vjp, nondiff_argnums=(1,))
def _shift_down(x, n):
    if n % _SUBLANES == 0:
        return jnp.concatenate([jnp.zeros((n,) + x.shape[1:], x.dtype), x[:x.shape[0] - n]], axis=0)
    row = lax.broadcasted_iota(jnp.int32, x.shape, 0)
    return jnp.where(row >= n, pltpu.roll(x, n, 0), 0.0)


def _shift_down_f(x, n):
    return _shift_down(x, n), None


def _shift_down_b(n, _, g):
    return (_shift_up(g, n),)


_shift_down.defvjp(_shift_down_f, _shift_down_b)


def _rows_apart(x):
    return tuple(x[k:k + _SUBLANES] for k in range(0, x.shape[0], _SUBLANES))


@jax.custom_vjp
def _split_rows(x):
    return _rows_apart(x)


_split_rows.defvjp(lambda x: (_rows_apart(x), None), lambda _, gs: (jnp.concatenate(gs, axis=0),))


@jax.custom_vjp
def _join_rows(pieces):
    return jnp.concatenate(pieces, axis=0)


_join_rows.defvjp(lambda pieces: (jnp.concatenate(pieces, axis=0), None), lambda _, g: (_rows_apart(g),))


@jax.custom_vjp
def _last_row(x):
    return x[_SUBLANES - 1:_SUBLANES]


def _last_row_b(_, g):
    row = lax.broadcasted_iota(jnp.int32, (_SUBLANES, g.shape[1]), 0)
    return (jnp.where(row == _SUBLANES - 1, g, 0.0),)


_last_row.defvjp(lambda x: (x[_SUBLANES - 1:_SUBLANES], None), _last_row_b)


def _sigmoid(x):
    return 1.0 / (1.0 + jnp.exp(-x))


def _silu(x):
    return x * _sigmoid(x)


def _gelu(x):
    return 0.5 * x * (1.0 + jnp.tanh(0.7978845608028654 * (x + 0.044715 * (x * x * x))))


def _rms_core(x, g):
    return x * lax.rsqrt(jnp.mean(x * x, axis=-1, keepdims=True) + _RMS_EPS) * g


def _mm(a, b, mode, out_dtype, *, tm, tn, tk, res=None, halves=None, rider=None, colperm=None, name):
    if colperm is None:
        def colperm(n_):
            return n_
    if tk is None:
        kdim = a.shape[0] if mode == "tn" else (b.shape[2] if (mode == "nt" and b.ndim == 3) else a.shape[1])
        tk = _tk_fit(kdim, tm, tn)
    if mode == "nn":
        m, k = a.shape
        a_spec = pl.BlockSpec((tm, tk), lambda i, j, kk: (i, kk))
        if b.ndim == 3:
            s, _, ns = b.shape
            n = s * ns
            npb = ns // tn
            b_spec = pl.BlockSpec((None, tk, tn), lambda i, j, kk: (j // npb, kk, j % npb))
        else:
            n = b.shape[1]
            b_spec = pl.BlockSpec((tk, tn), lambda i, j, kk: (kk, j))
        dims = _NN
    elif mode == "nt":
        m, k = a.shape
        a_spec = pl.BlockSpec((tm, tk), lambda i, j, kk: (i, colperm(kk)))
        if b.ndim == 3:
            s, n, ks = b.shape
            kpb = ks // tk
            b_spec = pl.BlockSpec((None, tn, tk), lambda i, j, kk: (kk // kpb, j, kk % kpb))
        else:
            n = b.shape[0]
            b_spec = pl.BlockSpec((tn, tk), lambda i, j, kk: (j, kk))
        dims = _NT
    else:
        k, m = a.shape
        n = b.shape[1]
        a_spec = pl.BlockSpec((tk, tm), lambda i, j, kk: (kk, i))
        b_spec = pl.BlockSpec((tk, tn), lambda i, j, kk: (kk, colperm(j)))
        dims = _TN
    nk = k // tk
    if halves is None:
        out_shape = jax.ShapeDtypeStruct((m, n), out_dtype)
        out_spec = pl.BlockSpec((tm, tn), lambda i, j, kk: (i, colperm(j) if mode == "nn" else j))
    elif halves == "cols":
        c2 = n // (2 * _NCHIP)
        tpc = c2 // tn
        out_shape = jax.ShapeDtypeStruct((2, _NCHIP, m, c2), out_dtype)
        out_spec = pl.BlockSpec((None, None, tm, tn),
                                lambda i, j, kk: ((j // tpc) % 2, j // (2 * tpc), i, j % tpc))
    else:
        c2 = n // 2
        tpc = c2 // tn
        r = m // _NCHIP
        tpr = r // tm
        out_shape = jax.ShapeDtypeStruct((2, _NCHIP, r, c2), out_dtype)
        out_spec = pl.BlockSpec((None, None, tm, tn),
                                lambda i, j, kk: (j // tpc, i // tpr, i % tpr, j % tpc))
    has_res = res is not None
    nreg = 3 if has_res else 2
    ni, nj = m // tm, n // tn
    r_ins = list(rider.ins) if rider else []
    r_outs = list(rider.outs) if rider else []

    def body(*refs):
        a_ref, b_ref = refs[0], refs[1]
        r_ref = refs[2] if has_res else None
        rin = refs[nreg:nreg + len(r_ins)]
        o_ref = refs[nreg + len(r_ins)]
        rout = refs[nreg + len(r_ins) + 1:nreg + len(r_ins) + 1 + len(r_outs)]
        acc_ref = refs[nreg + len(r_ins) + 1 + len(r_outs)]
        sems = refs[nreg + len(r_ins) + 2 + len(r_outs):]
        i, j, kk = pl.program_id(0), pl.program_id(1), pl.program_id(2)

        if rider:
            @pl.when(jnp.logical_and(jnp.logical_and(i == 0, j == 0), kk == 0))
            def _():
                rider.start(rin, rout, sems)

        @pl.when(kk == 0)
        def _():
            acc_ref[...] = jnp.zeros_like(acc_ref)

        acc_ref[...] += _dg(a_ref[...], b_ref[...], dims)

        @pl.when(kk == nk - 1)
        def _():
            out = acc_ref[...]
            if has_res:
                out = out + r_ref[...]
            o_ref[...] = out.astype(out_dtype)

        if rider:
            @pl.when(jnp.logical_and(jnp.logical_and(i == ni - 1, j == nj - 1), kk == nk - 1))
            def _():
                rider.finish(rin, rout, sems)

    in_specs = [a_spec, b_spec]
    args = [a, b]
    if has_res:
        in_specs.append(pl.BlockSpec((tm, tn), lambda i, j, kk: (i, j)))
        args.append(res)
    if not rider:
        return _call(body, name=name, grid=(ni, nj, nk), in_specs=in_specs, out_specs=out_spec,
                     out_shape=out_shape, scratch_shapes=[pltpu.VMEM((tm, tn), _F32)],
                     compiler_params=_cparams("parallel", "parallel", "arbitrary"))(*args)
    res_all = _call(body, name=name, grid=(ni, nj, nk), in_specs=in_specs + [_ANY] * len(r_ins),
                    out_specs=[out_spec] + [_ANY] * len(r_outs), out_shape=[out_shape] + r_outs,
                    input_output_aliases={nreg + k: 1 + v for k, v in rider.aliases.items()},
                    scratch_shapes=[pltpu.VMEM((tm, tn), _F32)] + list(rider.sems),
                    compiler_params=_cparams("arbitrary", "arbitrary", "arbitrary"))(*args, *r_ins)
    return res_all[0], list(res_all[1:])


def _rowcall(fn, ins, in_specs, outs, out_specs, acc, *, grid, name):
    nin = len(ins)

    def body(*refs):
        vals = fn(*[r[...] for r in refs[:nin]])
        first = pl.program_id(1) == 0
        for k, (o_ref, v) in enumerate(zip(refs[nin:], vals)):
            if acc[k]:
                @pl.when(first)
                def _(o_ref=o_ref):
                    o_ref[...] = jnp.zeros_like(o_ref)
                o_ref[...] += v.astype(o_ref.dtype)
            else:
                o_ref[...] = v.astype(o_ref.dtype)

    return _call(body, name=name, grid=grid, in_specs=in_specs, out_specs=out_specs, out_shape=outs,
                 compiler_params=_cparams("parallel", "arbitrary"))(*ins)


def _rb(tm, w, cb=0):
    return pl.BlockSpec((tm, w), lambda j, i: (i, cb + j))


def _cb(w, cb=0):
    return pl.BlockSpec((1, w), lambda j, i: (0, cb + j))


def _full(shape):
    nd = len(shape)
    return pl.BlockSpec(shape, lambda j, i: (0,) * nd)


def _sds(shape, dtype):
    return jax.ShapeDtypeStruct(shape, dtype)


def _rms_fwd(x, g, name):
    t, d = x.shape
    tm = _tile(t, 512, 8)
    return _rowcall(lambda xb, gb: (_rms_core(xb, gb),), [x, g], [_rb(tm, d), _full((1, d))],
                    [_sds((t, d), _BF)], [_rb(tm, d)], [False], grid=(1, t // tm), name=name)[0]


def _rms_bwd(x, g, dh, dres, name):
    t, d = x.shape
    tm = _tile(t, 256, 8)

    def fn(xb, gb, dhb, drb):
        _, vjp = jax.vjp(_rms_core, xb, gb)
        dx, dg = vjp(dhb.astype(_F32))
        dx = dx + drb
        return dx, dx, dg

    return _rowcall(fn, [x, g, dh, dres], [_rb(tm, d), _full((1, d)), _rb(tm, d), _rb(tm, d)],
                    [_sds((t, d), _F32), _sds((t, d), _BF), _sds((1, d), _F32)],
                    [_rb(tm, d), _rb(tm, d), _full((1, d))], [False, False, True],
                    grid=(1, t // tm), name=name)


def _loss_head(x3, tgt, g):
    t, d = x3.shape
    tm = _tile(t, 256, 8)

    def fn(xb, tb, gb):
        y, vjp = jax.vjp(_rms_core, xb, gb)
        e = y - tb
        part = 0.5 * jnp.sum(jnp.mean(e * e, axis=-1, keepdims=True), axis=0, keepdims=True)
        dx, dg = vjp(e * (1.0 / d))
        return jnp.broadcast_to(part, (1, _LANE)), dx, dx, dg

    return _rowcall(fn, [x3, tgt, g], [_rb(tm, d), _rb(tm, d), _full((1, d))],
                    [_sds((1, _LANE), _F32), _sds((t, d), _F32), _sds((t, d), _BF), _sds((1, d), _F32)],
                    [_full((1, _LANE)), _rb(tm, d), _rb(tm, d), _full((1, d))], [True, False, False, True],
                    grid=(1, t // tm), name="loss_head")


def _s5_disc(a_re, a_im, log_dt):
    lam_re = jnp.minimum(a_re, _S5_MAX_RE)
    lam_im = a_im
    dt = jnp.exp(log_dt)
    mag = jnp.exp(lam_re * dt)
    abar_re = mag * jnp.cos(lam_im * dt)
    abar_im = mag * jnp.sin(lam_im * dt)
    den = lam_re * lam_re + lam_im * lam_im
    nr = abar_re - 1.0
    ni = abar_im
    coef_re = (nr * lam_re + ni * lam_im) / den
    coef_im = (ni * lam_re - nr * lam_im) / den
    return abar_re, abar_im, coef_re, coef_im


def _s5_prep(a_re, a_im, log_dt):
    g, p = a_re.shape

    def body(ar, ai, ld, o0, o1, o2, o3):
        outs = _s5_disc(ar[...], ai[...], ld[...])
        for o, v in zip((o0, o1, o2, o3), outs):
            o[...] = v

    return _call(body, name="s5_prep", out_shape=[_sds((g, p), _F32)] * 4)(a_re, a_im, log_dt)


def _s5_prep_bwd(a_re, a_im, log_dt, cts):
    g, p = a_re.shape

    def body(ar, ai, ld, c0, c1, c2, c3, d0, d1, d2):
        _, vjp = jax.vjp(_s5_disc, ar[...], ai[...], ld[...])
        outs = vjp((c0[...], c1[...], c2[...], c3[...]))
        for o, v in zip((d0, d1, d2), outs):
            o[...] = v

    return _call(body, name="s5_prep_bwd",
                 out_shape=[_sds((g, p), _F32), _sds((g, p), _F32), _sds((g, 1), _F32)])(a_re, a_im, log_dt, *cts)


def _s5_block(u, car, cai, ar, ai, b_re, b_im, c_re, c_imn, dvec):
    bur = _bdot(u, b_re)
    bui = _bdot(u, b_im)
    shape8 = (_SUBLANES, ar.shape[1])
    pows = [(ar, ai)]
    for _ in range(2):
        pr, pi = pows[-1]
        pows.append((pr * pr - pi * pi, 2.0 * pr * pi))
    pows = [(jnp.broadcast_to(pr, shape8), jnp.broadcast_to(pi, shape8)) for pr, pi in pows]

    def scan8(xr, xi):
        for k, (pr, pi) in enumerate(pows):
            dr = _shift_down(xr, 1 << k)
            di = _shift_down(xi, 1 << k)
            xr, xi = xr + pr * dr - pi * di, xi + pr * di + pi * dr
        return xr, xi

    row8 = lax.broadcasted_iota(jnp.int32, (_SUBLANES, ar.shape[1]), 0)
    tr, ti = scan8(jnp.where(row8 == 0, ar, 0.0), jnp.where(row8 == 0, ai, 0.0))
    outs_r, outs_i = [], []
    for xr, xi in zip(_split_rows(bur), _split_rows(bui)):
        xr, xi = scan8(xr, xi)
        xr, xi = xr + tr * car - ti * cai, xi + tr * cai + ti * car
        car, cai = _last_row(xr), _last_row(xi)
        outs_r.append(xr)
        outs_i.append(xi)
    y = _bdot(_join_rows(tuple(outs_r)), c_re) + _bdot(_join_rows(tuple(outs_i)), c_imn) + dvec * u
    return y, car, cai


def _s5_specs(tb, lw, nt, rev):
    tmap = (lambda t: nt - 1 - t) if rev else (lambda t: t)
    vec = pl.BlockSpec((1, lw), lambda s, t: (0, s))
    return dict(
        u=pl.BlockSpec((tb, _LANE), lambda s, t: (tmap(t), s)),
        car=pl.BlockSpec((None, 1, lw), lambda s, t: (tmap(t), 0, s)),
        vec=vec,
        bmat=pl.BlockSpec((None, _LANE, lw), lambda s, t: (s, 0, 0)),
        cmat=pl.BlockSpec((None, lw, _LANE), lambda s, t: (s, 0, 0)),
        dvec=pl.BlockSpec((1, _LANE), lambda s, t: (0, s)),
    )


def _s5_fwd(proj, par, t, w5, rider=None):
    ar, ai, b_re, b_im, c_re, c_imn, dvec = par
    ns = w5 // _LANE
    lw = ar.shape[1] // ns
    tb = min(_S5_TB, t)
    nt = t // tb
    sp = _s5_specs(tb, lw, nt, False)

    def body(u_ref, ar_r, ai_r, bre_r, bim_r, cre_r, cim_r, d_r, y_ref, car_ref, cai_ref, s_r, s_i):
        @pl.when(pl.program_id(1) == 0)
        def _():
            s_r[...] = jnp.zeros_like(s_r)
            s_i[...] = jnp.zeros_like(s_i)

        car_ref[...] = s_r[...]
        cai_ref[...] = s_i[...]
        y, ncr, nci = _s5_block(u_ref[...], s_r[...], s_i[...], ar_r[...], ai_r[...],
                                bre_r[...], bim_r[...], cre_r[...], cim_r[...], d_r[...])
        y_ref[...] = y
        s_r[...] = ncr
        s_i[...] = nci

    kw = dict(name="s5_fwd", grid=(ns, nt),
              in_specs=[sp["u"], sp["vec"], sp["vec"], sp["bmat"], sp["bmat"],
                        sp["cmat"], sp["cmat"], sp["dvec"]],
              out_specs=[sp["u"], sp["car"], sp["car"]],
              out_shape=[_sds((t, w5), _F32), _sds((nt, 1, ns * lw), _F32), _sds((nt, 1, ns * lw), _F32)],
              scratch_shapes=[pltpu.VMEM((1, lw), _F32), pltpu.VMEM((1, lw), _F32)])
    args = (proj, ar, ai, b_re, b_im, c_re, c_imn, dvec)
    if rider is None:
        return _call(body, compiler_params=_cparams("parallel", "arbitrary"), **kw)(*args), []
    return _call_riding(body, rider, args=args, **kw)


def _s5_bwd(proj, dy, car, cai, par, t, w5, rider=None):
    ar, ai, b_re, b_im, c_re, c_imn, dvec = par
    ns = w5 // _LANE
    lw = ar.shape[1] // ns
    tb = min(_S5_TB, t)
    nt = t // tb
    sp = _s5_specs(tb, lw, nt, True)

    def body(u_ref, dy_ref, car_ref, cai_ref, ar_r, ai_r, bre_r, bim_r, cre_r, cim_r, d_r,
             du_ref, g_ar, g_ai, g_bre, g_bim, g_cre, g_cim, g_d, ds_r, ds_i):
        accs = (g_ar, g_ai, g_bre, g_bim, g_cre, g_cim, g_d)

        @pl.when(pl.program_id(1) == 0)
        def _():
            ds_r[...] = jnp.zeros_like(ds_r)
            ds_i[...] = jnp.zeros_like(ds_i)
            for o in accs:
                o[...] = jnp.zeros_like(o)

        _, vjp = jax.vjp(_s5_block, u_ref[...], car_ref[...], cai_ref[...], ar_r[...], ai_r[...],
                         bre_r[...], bim_r[...], cre_r[...], cim_r[...], d_r[...])
        grads = vjp((dy_ref[...], ds_r[...], ds_i[...]))
        du_ref[...] = grads[0].astype(_BF)
        ds_r[...] = grads[1]
        ds_i[...] = grads[2]
        for o, gval in zip(accs, grads[3:]):
            o[...] += gval

    vec_o = _sds((1, ns * lw), _F32)
    kw = dict(name="s5_bwd", grid=(ns, nt),
              in_specs=[sp["u"], sp["u"], sp["car"], sp["car"], sp["vec"], sp["vec"],
                        sp["bmat"], sp["bmat"], sp["cmat"], sp["cmat"], sp["dvec"]],
              out_specs=[sp["u"], sp["vec"], sp["vec"], sp["bmat"], sp["bmat"],
                         sp["cmat"], sp["cmat"], sp["dvec"]],
              out_shape=[_sds((t, w5), _BF), vec_o, vec_o,
                         _sds(b_re.shape, _F32), _sds(b_re.shape, _F32), _sds(c_re.shape, _F32),
                         _sds(c_re.shape, _F32), _sds((1, w5), _F32)],
              scratch_shapes=[pltpu.VMEM((1, lw), _F32), pltpu.VMEM((1, lw), _F32)])
    args = (proj, dy, car, cai, ar, ai, b_re, b_im, c_re, c_imn, dvec)
    if rider is None:
        return _call(body, compiler_params=_cparams("parallel", "arbitrary"), **kw)(*args), []
    return _call_riding(body, rider, args=args, **kw)


def _bd_in(b):
    g, p, c = b.shape
    s = g // _S5_SET
    b4 = b.reshape(s, _S5_SET, p, c).transpose(0, 1, 3, 2)
    eye = jnp.eye(_S5_SET, dtype=b.dtype)
    return (b4[:, :, :, None, :] * eye[None, :, None, :, None]).reshape(s, _S5_SET * c, _S5_SET * p)


def _bd_in_grad(d, p, c):
    s = d.shape[0]
    eye = jnp.eye(_S5_SET, dtype=d.dtype)
    d5 = d.reshape(s, _S5_SET, c, _S5_SET, p) * eye[None, :, None, :, None]
    return d5.sum(axis=3).transpose(0, 1, 3, 2).reshape(s * _S5_SET, p, c)


def _bd_out(cm):
    g, c, p = cm.shape
    s = g // _S5_SET
    c4 = cm.reshape(s, _S5_SET, c, p).transpose(0, 1, 3, 2)
    eye = jnp.eye(_S5_SET, dtype=cm.dtype)
    return (c4[:, :, :, None, :] * eye[None, :, None, :, None]).reshape(s, _S5_SET * p, _S5_SET * c)


def _bd_out_grad(d, p, c):
    s = d.shape[0]
    eye = jnp.eye(_S5_SET, dtype=d.dtype)
    d5 = d.reshape(s, _S5_SET, p, _S5_SET, c) * eye[None, :, None, :, None]
    return d5.sum(axis=3).transpose(0, 1, 3, 2).reshape(s * _S5_SET, c, p)


def _s5glu_fwd(y, wglu, bglu):
    t, w5 = y.shape
    tm = _tile(t, 256, 8)

    def fn(yb, wb, bb):
        z1 = _gelu(yb)
        a = _dg(z1, wb, _NN) + bb
        return (z1 * _sigmoid(a),)

    return _rowcall(fn, [y, wglu, bglu], [_rb(tm, w5), _full(wglu.shape), _full((1, w5))],
                    [_sds((t, w5), _BF)], [_rb(tm, w5)], [False], grid=(1, t // tm), name="s5glu_fwd")[0]


def _s5glu_bwd(y, dz, wglu, bglu):
    t, w5 = y.shape
    tm = _tile(t, 256, 8)

    def fn(yb, dzb, wb, bb):
        dzb = dzb.astype(_F32)
        z1, gelu_vjp = jax.vjp(_gelu, yb)
        sig = _sigmoid(_dg(z1, wb, _NN) + bb)
        da = dzb * z1 * sig * (1.0 - sig)
        dz1 = dzb * sig + _dg(da, wb, _NT)
        (dy,) = gelu_vjp(dz1)
        return dy, _dg(z1, da, _TN), jnp.sum(da, axis=0, keepdims=True)

    return _rowcall(fn, [y, dz, wglu, bglu], [_rb(tm, w5), _rb(tm, w5), _full(wglu.shape), _full((1, w5))],
                    [_sds((t, w5), _F32), _sds((w5, w5), _F32), _sds((1, w5), _F32)],
                    [_rb(tm, w5), _full((w5, w5)), _full((1, w5))], [False, True, True],
                    grid=(1, t // tm), name="s5glu_bwd")


_LEVELS = (6, 5, 4, 3, 2, 1)


def _tri_stack(c):
    t = jnp.arange(c, dtype=jnp.int32)[:, None]
    j = jnp.arange(c, dtype=jnp.int32)[None, :]
    low = (j <= t).astype(_F32)
    mats = [low]
    for sh in _LEVELS:
        r = ((t >> sh) << sh) + ((1 << (sh - 1)) - 1)
        mats.append(low - (j <= r).astype(_F32))
    stack = jnp.concatenate(mats, axis=0).astype(_BF)
    return stack, stack.T


def _split_dot(mat, x):
    l = x.shape[1]
    hi = x.astype(_BF)
    lo = (x - hi.astype(_F32)).astype(_BF)
    out = jnp.dot(mat, jnp.concatenate([hi, lo], axis=1), preferred_element_type=_F32)
    return out[:, :l] + out[:, l:]


@jax.custom_vjp
def _decay_sums(lf, tri, tri_t):
    c = lf.shape[0]
    out = _split_dot(tri, lf)
    return tuple(out[k * c:(k + 1) * c] for k in range(len(_LEVELS) + 1))


def _decay_sums_f(lf, tri, tri_t):
    return _decay_sums(lf, tri, tri_t), (tri, tri_t)


def _decay_sums_b(res, gs):
    tri, tri_t = res
    return _split_dot(tri_t, jnp.concatenate(gs, axis=0)), jnp.zeros_like(tri), jnp.zeros_like(tri_t)


_decay_sums.defvjp(_decay_sums_f, _decay_sums_b)


def _hgrn_chunk(qi, fi, vi, gi, st, lb, ng, tri, tri_t):
    heads = range(len(qi))
    c = qi[0].shape[0]
    row = lax.broadcasted_iota(jnp.int32, (c, 1), 0)
    tt = lax.broadcasted_iota(jnp.int32, (c, c), 0)
    ss = lax.broadcasted_iota(jnp.int32, (c, c), 1)
    q = [_silu(qi[h]) for h in heads]
    lf = [jnp.log(lb[h] + (1.0 - lb[h]) * _sigmoid(fi[h])) for h in heads]
    k = [(1.0 - lb[h]) * _sigmoid(-fi[h]) for h in heads]
    sums = [_decay_sums(lf[h], tri, tri_t) for h in heads]
    btot = [jnp.sum(lf[h], axis=0, keepdims=True) for h in heads]
    inter = [_bdot_nt(q[h] * jnp.exp(sums[h][0]), st[h]) for h in heads]
    sc = [jnp.where(tt == ss, jnp.sum(q[h] * k[h], axis=1, keepdims=True), 0.0) for h in heads]
    for li, sh in enumerate(_LEVELS):
        upper = ((row >> (sh - 1)) & 1) == 1
        same = (tt >> sh) == (ss >> sh)
        qm = [jnp.where(upper, q[h] * jnp.exp(jnp.where(upper, sums[h][li + 1], 0.0)), 0.0) for h in heads]
        km = [jnp.where(upper, 0.0, k[h] * jnp.exp(jnp.where(upper, 0.0, -sums[h][li + 1]))) for h in heads]
        prod = [_bdot_nt(qm[h], km[h]) for h in heads]
        sc = [sc[h] + jnp.where(same, prod[h], 0.0) for h in heads]
    o = [inter[h] + _bdot(sc[h], vi[h]) for h in heads]
    st_new = [st[h] * jnp.exp(btot[h]) + _bdot_tn(vi[h], k[h] * jnp.exp(btot[h] - sums[h][0])) for h in heads]
    og = [o[h] * lax.rsqrt(jnp.mean(o[h] * o[h], axis=1, keepdims=True) + _RMS_EPS) * ng[h] * _silu(gi[h])
          for h in heads]
    return tuple(og), tuple(st_new)


def _head_groups(hp):
    g = min(_HGRN_GROUP, hp)
    return [list(range(k, min(k + g, hp))) for k in range(0, hp, g)]


def _hgrn_geom(t, w5, hw):
    hp = _HGRN_HP if (hw // _LANE) % _HGRN_HP == 0 and w5 % (_LANE * _HGRN_HP) == 0 else 1
    seg = min(t, _HGRN_SEG)
    return hp, hp * _LANE, seg, t // seg


def _hgrn_in_specs(seg, wd, w5, hw, tmap):
    nhp = hw // wd
    qb = w5 // wd
    return [pl.BlockSpec((seg, wd), (lambda h, s, k=k: (tmap(s), qb + k * nhp + h))) for k in range(4)]


def _hgrn_fwd(proj, lb, ng, t, w5, hw, rider=None):
    assert _CHUNK == 64
    hp, wd, seg, nseg = _hgrn_geom(t, w5, hw)
    ncs = seg // _CHUNK
    vec = pl.BlockSpec((1, wd), lambda h, s: (0, h))

    tri, tri_t = _tri_stack(_CHUNK)

    def body(q_ref, f_ref, i_ref, g_ref, lb_ref, ng_ref, tri_ref, trit_ref, og_ref, st_ref, s_scr):
        @pl.when(pl.program_id(1) == 0)
        def _():
            s_scr[...] = jnp.zeros_like(s_scr)

        tri_v, trit_v = tri_ref[...], trit_ref[...]

        def step(ci, carry):
            r = pl.ds(pl.multiple_of(ci * _CHUNK, _CHUNK), _CHUNK)
            for grp in _head_groups(hp):
                lns = [slice(hh * _LANE, (hh + 1) * _LANE) for hh in grp]
                for hh in grp:
                    st_ref[hh, ci] = s_scr[hh]
                ogs, sns = _hgrn_chunk(tuple(q_ref[r, ln] for ln in lns), tuple(f_ref[r, ln] for ln in lns),
                                       tuple(i_ref[r, ln] for ln in lns), tuple(g_ref[r, ln] for ln in lns),
                                       tuple(s_scr[hh] for hh in grp), tuple(lb_ref[:, ln] for ln in lns),
                                       tuple(ng_ref[:, ln] for ln in lns), tri_v, trit_v)
                for hh, ln, og, sn in zip(grp, lns, ogs, sns):
                    og_ref[r, ln] = og.astype(_BF)
                    s_scr[hh] = sn
            return carry

        lax.fori_loop(0, ncs, step, 0)

    kw = dict(name="hgrn_fwd", grid=(hw // wd, nseg),
              in_specs=_hgrn_in_specs(seg, wd, w5, hw, lambda s: s) + [
                  vec, vec, pl.BlockSpec(tri.shape, lambda h, s: (0, 0)), pl.BlockSpec(tri_t.shape, lambda h, s: (0, 0))],
              out_specs=[pl.BlockSpec((seg, wd), lambda h, s: (s, h)),
                         pl.BlockSpec((hp, ncs, _LANE, _LANE), lambda h, s: (h, s, 0, 0))],
              out_shape=[_sds((t, hw), _BF), _sds((hw // _LANE, t // _CHUNK, _LANE, _LANE), _F32)],
              scratch_shapes=[pltpu.VMEM((hp, _LANE, _LANE), _F32)])
    args = (proj, proj, proj, proj, lb, ng, tri, tri_t)
    if rider is None:
        return _call(body, compiler_params=_cparams("parallel", "arbitrary"), **kw)(*args), []
    return _call_riding(body, rider, args=args, **kw)


def _hgrn_bwd(proj, lb, ng, st_all, dog, t, w5, hw):
    hp, wd, seg, nseg = _hgrn_geom(t, w5, hw)
    ncs = seg // _CHUNK

    def rev(s):
        return nseg - 1 - s

    vec = pl.BlockSpec((1, wd), lambda h, s: (0, h))
    col = pl.BlockSpec((seg, wd), lambda h, s: (rev(s), h))

    tri, tri_t = _tri_stack(_CHUNK)

    def body(q_ref, f_ref, i_ref, g_ref, lb_ref, ng_ref, tri_ref, trit_ref, st_ref, dog_ref,
             dq_ref, df_ref, di_ref, dg_ref, dlb_ref, dng_ref, ds_scr):
        @pl.when(pl.program_id(1) == 0)
        def _():
            ds_scr[...] = jnp.zeros_like(ds_scr)
            dlb_ref[...] = jnp.zeros_like(dlb_ref)
            dng_ref[...] = jnp.zeros_like(dng_ref)

        tri_v, trit_v = tri_ref[...], trit_ref[...]

        def step(kk, carry):
            ci = ncs - 1 - kk
            r = pl.ds(pl.multiple_of(ci * _CHUNK, _CHUNK), _CHUNK)
            for grp in _head_groups(hp):
                lns = [slice(hh * _LANE, (hh + 1) * _LANE) for hh in grp]
                _, vjp = jax.vjp(_hgrn_chunk, tuple(q_ref[r, ln] for ln in lns), tuple(f_ref[r, ln] for ln in lns),
                                 tuple(i_ref[r, ln] for ln in lns), tuple(g_ref[r, ln] for ln in lns),
                                 tuple(st_ref[hh, ci] for hh in grp), tuple(lb_ref[:, ln] for ln in lns),
                                 tuple(ng_ref[:, ln] for ln in lns), tri_v, trit_v)
                dq, df, di, dg, ds, dlb, dng = vjp((tuple(dog_ref[r, ln].astype(_F32) for ln in lns),
                                                    tuple(ds_scr[hh] for hh in grp)))[:7]
                for n_, (hh, ln) in enumerate(zip(grp, lns)):
                    dq_ref[r, ln] = dq[n_].astype(_BF)
                    df_ref[r, ln] = df[n_].astype(_BF)
                    di_ref[r, ln] = di[n_].astype(_BF)
                    dg_ref[r, ln] = dg[n_].astype(_BF)
                    ds_scr[hh] = ds[n_]
                    dlb_ref[:, ln] += dlb[n_]
                    dng_ref[:, ln] += dng[n_]
            return carry

        lax.fori_loop(0, ncs, step, 0)

    return _call(body, name="hgrn_bwd", grid=(hw // wd, nseg),
                 in_specs=_hgrn_in_specs(seg, wd, w5, hw, rev) + [
                     vec, vec, pl.BlockSpec(tri.shape, lambda h, s: (0, 0)), pl.BlockSpec(tri_t.shape, lambda h, s: (0, 0)),
                     pl.BlockSpec((hp, ncs, _LANE, _LANE), lambda h, s: (h, rev(s), 0, 0)), col],
                 out_specs=[col, col, col, col, vec, vec],
                 out_shape=[_sds((t, hw), _BF)] * 4 + [_sds((1, hw), _F32)] * 2,
                 scratch_shapes=[pltpu.VMEM((hp, _LANE, _LANE), _F32)],
                 compiler_params=_cparams("parallel", "arbitrary"))(
                     proj, proj, proj, proj, lb, ng, tri, tri_t, st_all, dog)


def _lb_of(logits):
    mx = jnp.max(logits, axis=0, keepdims=True)
    e = jnp.exp(logits - mx)
    sm = e / jnp.sum(e, axis=0, keepdims=True)
    row = lax.broadcasted_iota(jnp.int32, logits.shape, 0)
    return jnp.sum(jnp.where(row == 0, sm, 0.0), axis=0, keepdims=True)


def _lb_prep(logits):
    def body(l_ref, o_ref):
        o_ref[...] = _lb_of(l_ref[...])

    return _call(body, name="lb_prep", out_shape=_sds((1, logits.shape[1]), _F32))(logits)


def _lb_prep_bwd(logits, dlb):
    def body(l_ref, d_ref, o_ref):
        _, vjp = jax.vjp(_lb_of, l_ref[...])
        o_ref[...] = vjp(d_ref[...])[0]

    return _call(body, name="lb_prep_bwd", out_shape=_sds(logits.shape, _F32))(logits, dlb)


def _merge_fwd(proj, ys, yh, t, d, gs_off):
    tm = _tile(t, 256, 8)
    w = _tile(d, 1024)
    nb = d // w

    def fn(gs, gh, a, b):
        return (_sigmoid(gs) * a + _sigmoid(gh) * b,)

    return _rowcall(fn, [proj, proj, ys, yh], [_rb(tm, w, gs_off // w), _rb(tm, w, gs_off // w + nb), _rb(tm, w), _rb(tm, w)],
                    [_sds((t, d), _BF)], [_rb(tm, w)], [False], grid=(nb, t // tm), name="merge_fwd")[0]


def _merge_bwd(proj, ys, yh, dm, t, d, gs_off):
    tm = _tile(t, 256, 8)
    w = _tile(d, 1024)
    nb = d // w

    def fn(gs, gh, a, b, g):
        g = g.astype(_F32)
        s1 = _sigmoid(gs)
        s2 = _sigmoid(gh)
        return g * s1, g * s2, g * a * s1 * (1.0 - s1), g * b * s2 * (1.0 - s2)

    return _rowcall(fn, [proj, proj, ys, yh, dm],
                    [_rb(tm, w, gs_off // w), _rb(tm, w, gs_off // w + nb), _rb(tm, w), _rb(tm, w), _rb(tm, w)],
                    [_sds((t, d), _BF)] * 4, [_rb(tm, w)] * 4, [False] * 4, grid=(nb, t // tm), name="merge_bwd")


_HALO = 16


def _prev_rows(up_prev, is_first):
    p1 = jnp.where(is_first, 0.0, up_prev[_HALO - 1:_HALO, :])
    p2 = jnp.where(is_first, 0.0, up_prev[_HALO - 2:_HALO - 1, :])
    return p1, p2


def _causal_taps(cur, p1, p2):
    row = lax.broadcasted_iota(jnp.int32, cur.shape, 0)
    s1 = jnp.where(row == 0, p1, _shift_down(cur, 1))
    s2 = jnp.where(row == 0, p2, jnp.where(row == 1, p1, _shift_down(cur, 2)))
    return s1, s2


def _pair_perm(nbh, per):
    def perm(n_):
        big = n_ // per
        return (2 * (big % nbh) + big // nbh) * per + n_ % per
    return perm


def _conv_specs(tm, w, nb_half, t):
    r8 = tm // _HALO
    cur_g = pl.BlockSpec((tm, w), lambda j, i: (i, 2 * j))
    cur_v = pl.BlockSpec((tm, w), lambda j, i: (i, 2 * j + 1))
    prev_g = pl.BlockSpec((_HALO, w), lambda j, i: (jnp.maximum(i * r8 - 1, 0), 2 * j))
    prev_v = pl.BlockSpec((_HALO, w), lambda j, i: (jnp.maximum(i * r8 - 1, 0), 2 * j + 1))
    w_g = pl.BlockSpec((3, w), lambda j, i: (0, j))
    w_v = pl.BlockSpec((3, w), lambda j, i: (0, nb_half + j))
    b_g = pl.BlockSpec((1, w), lambda j, i: (0, j))
    b_v = pl.BlockSpec((1, w), lambda j, i: (0, nb_half + j))
    return cur_g, cur_v, prev_g, prev_v, w_g, w_v, b_g, b_v


def _conv_of(cur, prev8, wt, bias, is_first):
    cur, prev8 = cur.astype(_F32), prev8.astype(_F32)
    p1, p2 = _prev_rows(prev8, is_first)
    s1, s2 = _causal_taps(cur, p1, p2)
    return bias + wt[0:1, :] * s2 + wt[1:2, :] * s1 + wt[2:3, :] * cur


def _convact_fwd(up, cw, cb, t, dff):
    tm = _tile(t, 512, _HALO)
    w = _tile(dff, 1408)
    nbh = dff // w
    sp = _conv_specs(tm, w, nbh, t)

    def body(ug, uv, pg, pv, wg, wv, bg, bv, o_ref):
        first = pl.program_id(1) == 0
        gate = _conv_of(ug[...], pg[...], wg[...], bg[...], first)
        val = _conv_of(uv[...], pv[...], wv[...], bv[...], first)
        o_ref[...] = (_silu(gate) * val).astype(_BF)

    return _call(body, name="convact_fwd", grid=(nbh, t // tm), in_specs=list(sp),
                 out_specs=pl.BlockSpec((tm, w), lambda j, i: (i, j)), out_shape=_sds((t, dff), _BF),
                 compiler_params=_cparams("parallel", "arbitrary"))(up, up, up, up, cw, cw, cb, cb)


def _convact_bwd(up, dact, cw, cb, t, dff):
    tm = _tile(t, 256, _HALO)
    w = _tile(dff, 1408)
    nbh = dff // w
    r8 = tm // _HALO
    nt = t // tm
    last8 = t // _HALO - 1

    def triple(off):
        return [pl.BlockSpec((tm, w), lambda j, i: (i, 2 * j + off)),
                pl.BlockSpec((_HALO, w), lambda j, i: (jnp.minimum((i + 1) * r8, last8), 2 * j + off)),
                pl.BlockSpec((_HALO, w), lambda j, i: (jnp.maximum(i * r8 - 1, 0), 2 * j + off))]

    def body(ug, ugn, ugp, uv, uvn, uvp, wg_ref, wv_ref, bg_ref, bv_ref, da_ref, dan_ref, du_ref, dw_ref, db_ref):
        i = pl.program_id(1)
        first = i == 0
        is_last = i == nt - 1
        gate = _conv_of(jnp.concatenate([ug[...], ugn[...]], axis=0), ugp[...], wg_ref[...], bg_ref[...], first)
        val = _conv_of(jnp.concatenate([uv[...], uvn[...]], axis=0), uvp[...], wv_ref[...], bv_ref[...], first)
        da = jnp.concatenate([da_ref[...], dan_ref[...]], axis=0).astype(_F32)
        row = lax.broadcasted_iota(jnp.int32, da.shape, 0)
        da = jnp.where(jnp.logical_and(row >= tm, is_last), 0.0, da)
        sg = _sigmoid(gate)
        halves = ((da * val * sg * (1.0 + gate * (1.0 - sg)), wg_ref, ug, ugp),
                  (da * gate * sg, wv_ref, uv, uvp))

        @pl.when(first)
        def _():
            dw_ref[...] = jnp.zeros_like(dw_ref)
            db_ref[...] = jnp.zeros_like(db_ref)

        for h, (dc, w_ref, u_ref, p_ref) in enumerate(halves):
            ln = slice(h * w, (h + 1) * w)
            wt = w_ref[...]
            du = wt[2:3, :] * dc + wt[1:2, :] * _shift_up(dc, 1) + wt[0:1, :] * _shift_up(dc, 2)
            du_ref[:, ln] = du[0:tm, :].astype(_BF)
            dcm = dc[0:tm, :]
            cur = u_ref[...].astype(_F32)
            p1, p2 = _prev_rows(p_ref[...].astype(_F32), first)
            s1, s2 = _causal_taps(cur, p1, p2)
            dw_ref[0:1, ln] += jnp.sum(dcm * s2, axis=0, keepdims=True)
            dw_ref[1:2, ln] += jnp.sum(dcm * s1, axis=0, keepdims=True)
            dw_ref[2:3, ln] += jnp.sum(dcm * cur, axis=0, keepdims=True)
            db_ref[:, ln] += jnp.sum(dcm, axis=0, keepdims=True)

    in_specs = (triple(0) + triple(1)
                + [pl.BlockSpec((3, w), lambda j, i: (0, j)), pl.BlockSpec((3, w), lambda j, i: (0, nbh + j)),
                   pl.BlockSpec((1, w), lambda j, i: (0, j)), pl.BlockSpec((1, w), lambda j, i: (0, nbh + j)),
                   pl.BlockSpec((tm, w), lambda j, i: (i, j)),
                   pl.BlockSpec((_HALO, w), lambda j, i: (jnp.minimum((i + 1) * r8, last8), j))])
    dup, dw_p, db_p = _call(
        body, name="convact_bwd", grid=(nbh, nt), in_specs=in_specs,
        out_specs=[pl.BlockSpec((tm, 2 * w), lambda j, i: (i, j)), pl.BlockSpec((3, 2 * w), lambda j, i: (0, j)),
                   pl.BlockSpec((1, 2 * w), lambda j, i: (0, j))],
        out_shape=[_sds((t, 2 * dff), _BF), _sds((3, 2 * dff), _F32), _sds((1, 2 * dff), _F32)],
        compiler_params=_cparams("parallel", "arbitrary"))(up, up, up, up, up, up, cw, cw, cb, cb, dact, dact)

    def natural(v):
        k = v.shape[0]
        return v.reshape(k, nbh, 2, w).transpose(0, 2, 1, 3).reshape(k, 2 * dff)

    return dup, natural(dw_p), natural(db_p)


def _me():
    return lax.axis_index("x"), lax.axis_index("y"), lax.axis_index("c")


def _other_chips(x, y):
    return [(1 - x, y), (x, 1 - y), (1 - x, 1 - y)]


def _rcopy(src, dst, ssem, rsem, dev):
    return pltpu.make_async_remote_copy(src_ref=src, dst_ref=dst, send_sem=ssem, recv_sem=rsem,
                                        device_id=dev, device_id_type=_MESH)


def _cast_into_slot(w, sel):
    r, c = w.shape
    tm = _tile(r, 256, 16)

    def body(sel_ref, w_ref, o_ref):
        o_ref[...] = w_ref[...].astype(_BF)

    gs = pltpu.PrefetchScalarGridSpec(
        num_scalar_prefetch=1, grid=(r // tm,),
        in_specs=[pl.BlockSpec((tm, c), lambda i, s: (i, 0))],
        out_specs=pl.BlockSpec((None, tm, c), lambda i, s: (s[0], i, 0)))
    return _call(body, name="cast_into_slot", grid_spec=gs, out_shape=_sds((_NCHIP, r, c), _BF),
                 compiler_params=_cparams("parallel"))(sel, w)


class _Plan:
    def __init__(self, ins, outs, aliases, sems, start, finish):
        self.ins, self.outs, self.aliases, self.sems, self.start, self.finish = ins, outs, aliases, sems, start, finish


def _run_plan(plan, name):
    ni, no = len(plan.ins), len(plan.outs)

    def body(*refs):
        rin, rout, sems = refs[:ni], refs[ni:ni + no], refs[ni + no:]
        plan.start(rin, rout, sems)
        plan.finish(rin, rout, sems)

    return _call(body, name=name, in_specs=[_ANY] * ni, out_specs=[_ANY] * no, out_shape=list(plan.outs),
                 input_output_aliases=dict(plan.aliases), scratch_shapes=list(plan.sems))(*plan.ins)


def _call_riding(body, rider, *, name, grid, in_specs, out_specs, out_shape, scratch_shapes, args):
    n_in, n_out, n_scr = len(in_specs), len(out_specs), len(scratch_shapes)
    n_rin, n_rout = len(rider.ins), len(rider.outs)

    def wrapped(*refs):
        ins, rin = refs[:n_in], refs[n_in:n_in + n_rin]
        o0 = n_in + n_rin
        outs, rout = refs[o0:o0 + n_out], refs[o0 + n_out:o0 + n_out + n_rout]
        s0 = o0 + n_out + n_rout
        scratch, sems = refs[s0:s0 + n_scr], refs[s0 + n_scr:]
        first = functools.reduce(jnp.logical_and, [pl.program_id(k) == 0 for k in range(len(grid))])
        last = functools.reduce(jnp.logical_and, [pl.program_id(k) == grid[k] - 1 for k in range(len(grid))])

        @pl.when(first)
        def _():
            rider.start(rin, rout, sems)

        body(*ins, *outs, *scratch)

        @pl.when(last)
        def _():
            rider.finish(rin, rout, sems)

    res = _call(wrapped, name=name, grid=grid, in_specs=list(in_specs) + [_ANY] * n_rin,
                out_specs=list(out_specs) + [_ANY] * n_rout, out_shape=list(out_shape) + list(rider.outs),
                input_output_aliases={n_in + k: n_out + v for k, v in rider.aliases.items()},
                scratch_shapes=list(scratch_shapes) + list(rider.sems),
                compiler_params=_cparams(*(["arbitrary"] * len(grid))))(*args, *rider.ins)
    return list(res[:n_out]), list(res[n_out:])


def _gather_plan(bufs, direct, rel=(0, 1, 2)):
    n, nd = len(bufs), len(direct)

    def where():
        x, y, c = _me()
        return c, 2 * x + y, _other_chips(x, y), (x, y, 1 - c)

    def picked(chips):
        return [(j, chips[j]) for j in rel]

    def piece(outs, a, chip, h):
        r2 = bufs[a].shape[1] // 2
        return outs[a].at[chip, pl.ds(h * r2, r2)]

    def send(outs, sems, a, j, me, c, chip):
        return _rcopy(piece(outs, a, me, c), piece(outs, a, me, c), sems[0].at[3 * a + j], sems[1].at[3 * a + j],
                      (chip[0], chip[1], c))

    def forward(outs, sems, a, j, pc, c, sib):
        return _rcopy(piece(outs, a, pc, c), piece(outs, a, pc, c), sems[2].at[3 * a + j], sems[3].at[3 * a + j], sib)

    def dsend(dins, douts, sems, a, j, me, c, chip):
        return _rcopy(dins[a], douts[a].at[me], sems[4].at[3 * a + j], sems[5].at[3 * a + j], (chip[0], chip[1], c))

    def start(rin, rout, sems):
        outs, dins, douts = rout[:n], rin[n:], rout[n:]
        c, me, chips, _ = where()
        for a in range(n):
            for j, chip in picked(chips):
                send(outs, sems, a, j, me, c, chip).start()
        for a in range(nd):
            pltpu.make_async_copy(dins[a], douts[a].at[me], sems[6].at[a]).start()
            for j, chip in enumerate(chips):
                dsend(dins, douts, sems, a, j, me, c, chip).start()

    def finish(rin, rout, sems):
        outs, dins, douts = rout[:n], rin[n:], rout[n:]
        c, me, chips, sib = where()
        for a in range(n):
            for j, (cx, cy) in picked(chips):
                pc = 2 * cx + cy
                _rcopy(piece(outs, a, me, c), piece(outs, a, pc, c), sems[0].at[3 * a + j], sems[1].at[3 * a + j],
                       (cx, cy, c)).wait_recv()
                forward(outs, sems, a, j, pc, c, sib).start()
        for a in range(n):
            for j, (cx, cy) in picked(chips):
                pc = 2 * cx + cy
                _rcopy(piece(outs, a, pc, 1 - c), piece(outs, a, pc, 1 - c), sems[2].at[3 * a + j],
                       sems[3].at[3 * a + j], sib).wait_recv()
        for a in range(nd):
            for j, (cx, cy) in enumerate(chips):
                _rcopy(dins[a], douts[a].at[2 * cx + cy], sems[4].at[3 * a + j], sems[5].at[3 * a + j],
                       (cx, cy, c)).wait_recv()
        for a in range(n):
            for j, (cx, cy) in picked(chips):
                send(outs, sems, a, j, me, c, (cx, cy)).wait_send()
                forward(outs, sems, a, j, 2 * cx + cy, c, sib).wait_send()
        for a in range(nd):
            pltpu.make_async_copy(dins[a], douts[a].at[me], sems[6].at[a]).wait()
            for j, chip in enumerate(chips):
                dsend(dins, douts, sems, a, j, me, c, chip).wait_send()

    dma = pltpu.SemaphoreType.DMA
    return _Plan(list(bufs) + list(direct),
                 [_sds(b.shape, b.dtype) for b in bufs] + [_sds((_NCHIP,) + s.shape, s.dtype) for s in direct],
                 {a: a for a in range(n)},
                 [dma((3 * max(n, 1),)), dma((3 * max(n, 1),)), dma((3 * max(n, 1),)), dma((3 * max(n, 1),)),
                  dma((3 * max(nd, 1),)), dma((3 * max(nd, 1),)), dma((max(nd, 1),))], start, finish)


def _mm_shard_order(a, mid_plan, end_plan, sel, *, tm, tn, name):
    b = mid_plan.ins[0]
    s, k, ns = b.shape
    m = a.shape[0]
    npb = ns // tn
    nj, ni = s * npb, m // tm
    n_in = (len(mid_plan.ins), len(end_plan.ins))
    n_out = (len(mid_plan.outs), len(end_plan.outs))
    n_sem = (len(mid_plan.sems), len(end_plan.sems))

    def shard_of(jj, q):
        return jnp.bitwise_xor(q[0], jj // npb)

    def body(q_ref, a_ref, *rest):
        rin = (rest[:n_in[0]], rest[n_in[0]:sum(n_in)])
        o_ref = rest[sum(n_in)]
        o0 = sum(n_in) + 1
        rout = (rest[o0:o0 + n_out[0]], rest[o0 + n_out[0]:o0 + sum(n_out)])
        vbuf, bsem = rest[o0 + sum(n_out)], rest[o0 + sum(n_out) + 1]
        s0 = o0 + sum(n_out) + 2
        sems = (rest[s0:s0 + n_sem[0]], rest[s0 + n_sem[0]:s0 + sum(n_sem)])
        b_ref = rout[0][0]
        j, i = pl.program_id(0), pl.program_id(1)

        def tile_copy(jj, slot):
            return pltpu.make_async_copy(b_ref.at[shard_of(jj, q_ref), :, pl.ds((jj % npb) * tn, tn)],
                                         vbuf.at[slot], bsem.at[slot])

        @pl.when(jnp.logical_and(j == 0, i == 0))
        def _():
            mid_plan.start(rin[0], rout[0], sems[0])
            end_plan.start(rin[1], rout[1], sems[1])
            tile_copy(0, 0).start()

        @pl.when(i == 0)
        def _():
            @pl.when(j == (s - 1) * npb - 1)
            def _():
                mid_plan.finish(rin[0], rout[0], sems[0])

            @pl.when(j + 1 < nj)
            def _():
                tile_copy(j + 1, (j + 1) % 2).start()

            tile_copy(j, j % 2).wait()

        o_ref[...] = _dg(a_ref[...], vbuf[j % 2], _NN)

        @pl.when(jnp.logical_and(j == nj - 1, i == ni - 1))
        def _():
            end_plan.finish(rin[1], rout[1], sems[1])

    assert s == _NCHIP and nj >= 2
    n_rin, n_rout = sum(n_in), sum(n_out)
    gs = pltpu.PrefetchScalarGridSpec(
        num_scalar_prefetch=1, grid=(nj, ni),
        in_specs=[pl.BlockSpec((tm, k), lambda j, i, q: (i, 0))] + [_ANY] * n_rin,
        out_specs=[pl.BlockSpec((tm, tn), lambda j, i, q: (i, shard_of(j, q) * npb + j % npb))] + [_ANY] * n_rout,
        scratch_shapes=[pltpu.VMEM((2, k, tn), _BF), pltpu.SemaphoreType.DMA((2,))] + list(mid_plan.sems) + list(end_plan.sems))
    aliases = {2 + kk: 1 + v for kk, v in mid_plan.aliases.items()}
    aliases.update({2 + n_in[0] + kk: 1 + n_out[0] + v for kk, v in end_plan.aliases.items()})
    res = _call(body, name=name, grid_spec=gs, out_shape=[_sds((m, s * ns), _F32)] + list(mid_plan.outs) + list(end_plan.outs),
                input_output_aliases=aliases,
                compiler_params=_cparams("arbitrary", "arbitrary"))(sel, a, *mid_plan.ins, *end_plan.ins)
    return res[0], list(res[1:1 + n_out[0]]), list(res[1 + n_out[0]:])


def _swap_halves(grads, name):
    n = len(grads)

    def body(*refs):
        ins, outs = refs[:n], refs[n:2 * n]
        ssem, rsem = refs[2 * n:]
        x, y, c = _me()
        sib = (x, y, 1 - c)
        cps = []
        for a in range(n):
            cp = _rcopy(ins[a].at[1 - c], outs[a], ssem.at[a], rsem.at[a], sib)
            cp.start()
            cps.append(cp)
        for cp in cps:
            cp.wait_recv()
        for cp in cps:
            cp.wait_send()

    dma = pltpu.SemaphoreType.DMA
    return _call(body, name=name, in_specs=[_ANY] * n, out_specs=[_ANY] * n,
                 out_shape=[_sds(g.shape[1:], g.dtype) for g in grads],
                 scratch_shapes=[dma((n,)), dma((n,))])(*grads)


def _add_pairs(grads, theirs, sel):
    _, s, r, c2 = grads.shape
    a3 = grads.reshape(2, s * r, c2)
    b2 = theirs.reshape(s * r, c2)
    tm = _tile(s * r, 512, 16)

    def body(sel_ref, a_ref, b_ref, o_ref):
        o_ref[...] = (a_ref[...].astype(_F32) + b_ref[...].astype(_F32)).astype(_BF)

    gs = pltpu.PrefetchScalarGridSpec(
        num_scalar_prefetch=1, grid=(s * r // tm,),
        in_specs=[pl.BlockSpec((None, tm, c2), lambda i, q: (q[1], i, 0)), pl.BlockSpec((tm, c2), lambda i, q: (i, 0))],
        out_specs=pl.BlockSpec((tm, c2), lambda i, q: (i, 0)))
    out = _call(body, name="chip_sum", grid_spec=gs, out_shape=_sds((s * r, c2), _BF),
                compiler_params=_cparams("parallel"))(sel, a3, b2)
    return out.reshape(s, r, c2)


def _exchange_plan(sums, small):
    n = len(sums)
    has_small = small is not None

    def where():
        x, y, c = _me()
        peers = [(1 - x if k & 4 else x, 1 - y if k & 2 else y, 1 - c if k & 1 else c) for k in range(1, _NDEV)]
        return c, 4 * x + 2 * y + c, _other_chips(x, y), peers

    def send(rin, rout, sems, a, j, c, chip):
        return _rcopy(rin[a].at[2 * chip[0] + chip[1]], rout[a].at[j], sems[0].at[3 * a + j], sems[1].at[3 * a + j],
                      (chip[0], chip[1], c))

    def small_send(rin, rout, sems, k, dev, peer):
        return _rcopy(rin[n], rout[n].at[dev], sems[2].at[k], sems[3].at[k], peer)

    def start(rin, rout, sems):
        c, dev, chips, peers = where()
        for a in range(n):
            for j, chip in enumerate(chips):
                send(rin, rout, sems, a, j, c, chip).start()
        if has_small:
            pltpu.make_async_copy(rin[n], rout[n].at[dev], sems[4].at[0]).start()
            for k, peer in enumerate(peers):
                small_send(rin, rout, sems, k, dev, peer).start()

    def finish(rin, rout, sems):
        c, dev, chips, peers = where()
        for a in range(n):
            for j, chip in enumerate(chips):
                send(rin, rout, sems, a, j, c, chip).wait_recv()
        if has_small:
            for k, (px, py, pc_) in enumerate(peers):
                _rcopy(rin[n], rout[n].at[4 * px + 2 * py + pc_], sems[2].at[k], sems[3].at[k], (px, py, pc_)).wait_recv()
        for a in range(n):
            for j, chip in enumerate(chips):
                send(rin, rout, sems, a, j, c, chip).wait_send()
        if has_small:
            pltpu.make_async_copy(rin[n], rout[n].at[dev], sems[4].at[0]).wait()
            for k, peer in enumerate(peers):
                small_send(rin, rout, sems, k, dev, peer).wait_send()

    dma = pltpu.SemaphoreType.DMA
    outs = [_sds((3,) + s.shape[1:], s.dtype) for s in sums]
    if has_small:
        outs.append(_sds((_NDEV,) + small.shape, small.dtype))
    return _Plan(list(sums) + ([small] if has_small else []), outs, {},
                 [dma((3 * max(n, 1),)), dma((3 * max(n, 1),)), dma((_NDEV - 1,)), dma((_NDEV - 1,)), dma((1,))],
                 start, finish)


def _shard_sum(sums, recv, sel):
    s, r, c2 = sums.shape
    tm = _tile(r, 256, 16)

    def body(sel_ref, own_ref, rc_ref, o_ref):
        rc = rc_ref[...]
        o_ref[...] = ((own_ref[...].astype(_F32) + rc[0].astype(_F32)) + rc[1].astype(_F32)) + rc[2].astype(_F32)

    gs = pltpu.PrefetchScalarGridSpec(
        num_scalar_prefetch=1, grid=(r // tm,),
        in_specs=[pl.BlockSpec((None, tm, c2), lambda i, q: (q[0], i, 0)),
                  pl.BlockSpec((3, tm, c2), lambda i, q: (0, i, 0))],
        out_specs=pl.BlockSpec((None, tm, c2), lambda i, q: (q[1], i, 0)))
    return _call(body, name="shard_sum", grid_spec=gs, out_shape=_sds((2, r, c2), _F32),
                 compiler_params=_cparams("parallel"))(sel, sums, recv)


def _sum_slots(stack, name):
    k, r, c = stack.shape
    tm = _tile(r, 256, 16 if stack.dtype == _BF else 8)

    def fn(v):
        out = v[0].astype(_F32)
        for i in range(1, k):
            out = out + v[i].astype(_F32)
        return (out,)

    return _rowcall(fn, [stack], [pl.BlockSpec((k, tm, c), lambda j, i: (0, i, 0))], [_sds((r, c), _F32)],
                    [_rb(tm, c)], [False], grid=(1, r // tm), name=name)[0]


def _share_halves(bufs):
    n = len(bufs)

    def body(*refs):
        outs = refs[n:2 * n]
        ssem, rsem = refs[2 * n:]
        x, y, c = _me()
        sib = (x, y, 1 - c)
        cps = []
        for a in range(n):
            cp = _rcopy(outs[a].at[c], outs[a].at[c], ssem.at[a], rsem.at[a], sib)
            cp.start()
            cps.append(cp)
        for a in range(n):
            _rcopy(outs[a].at[c], outs[a].at[1 - c], ssem.at[a], rsem.at[a], sib).wait_recv()
        for cp in cps:
            cp.wait_send()

    dma = pltpu.SemaphoreType.DMA
    return _call(body, name="share_halves", in_specs=[_ANY] * n, out_specs=[_ANY] * n,
                 out_shape=[_sds(b.shape, b.dtype) for b in bufs], input_output_aliases={a: a for a in range(n)},
                 scratch_shapes=[dma((n,)), dma((n,))])(*bufs)


def _adamw_math(w, g, m, v):
    m = _B1 * m + (1.0 - _B1) * g
    v = _B2 * v + (1.0 - _B2) * jnp.square(g)
    m_hat = m / (1.0 - _B1 ** _STEP)
    v_hat = v / (1.0 - _B2 ** _STEP)
    delta = -_LR * (m_hat / (jnp.sqrt(v_hat) + _ADAM_EPS) + _WD * w)
    return delta, m, v


def _adamw_shard(w, g2, m, v, name):
    r, c = w.shape
    c2 = c // 2
    tm = _tile(r, 256, 8)
    blk = pl.BlockSpec((tm, c2), lambda h, i: (i, h))

    def fn(wb, gb, mb, vb):
        return (gb,) + _adamw_math(wb, gb, mb, vb)

    return _rowcall(fn, [w, g2, m, v], [blk, pl.BlockSpec((None, tm, c2), lambda h, i: (h, i, 0)), blk, blk],
                    [_sds((r, c), _F32)] * 4, [blk] * 4, [False] * 4, grid=(2, r // tm), name=name)


def _adamw_whole(w, g, m, v, name):
    r, c = w.shape
    blk = _full((r, c))
    return _rowcall(lambda *a: _adamw_math(*a), [w, g, m, v], [blk] * 4, [_sds((r, c), _F32)] * 3, [blk] * 3,
                    [False] * 3, grid=(1, 1), name=name)


def _adamw_many(ws, g_pack, ms, vs):
    k = len(ws)
    views, offs, off = [], [], 0
    for w in ws:
        n = w.size
        views.append((n // _LANE, _LANE) if n % _LANE == 0 else (1, n))
        offs.append(off)
        off += (n + (-n) % (8 * _LANE)) // _LANE

    def body(*refs):
        g_ref, w_refs, m_refs, v_refs = refs[0], refs[1:1 + k], refs[1 + k:1 + 2 * k], refs[1 + 2 * k:1 + 3 * k]
        outs = refs[1 + 3 * k:]
        for i in range(k):
            r, c = views[i]
            g = g_ref[offs[i]:offs[i] + r, 0:c]
            res = (g,) + _adamw_math(w_refs[i][...], g, m_refs[i][...], v_refs[i][...])
            for o_ref, val in zip(outs[4 * i:4 * i + 4], res):
                o_ref[...] = val

    args = [g_pack] + [a.reshape(views[i]) for grp in (ws, ms, vs) for i, a in enumerate(grp)]
    res = _call(body, name="adamw_small", out_shape=[_sds(views[i], _F32) for i in range(k) for _ in range(4)])(*args)
    return [[res[4 * i + j].reshape(ws[i].shape) for j in range(4)] for i in range(k)]


def _pack(arrs):
    parts = []
    for a in arrs:
        f = a.reshape(-1).astype(_F32)
        pad = (-f.shape[0]) % (8 * _LANE)
        if pad:
            f = jnp.concatenate([f, jnp.zeros((pad,), _F32)])
        parts.append(f)
    return jnp.concatenate(parts).reshape(-1, _LANE)


def kernel(x, ln_mix_g, w_in, s5_a_re, s5_a_im, s5_log_dt, s5_b_re, s5_b_im, s5_c_re, s5_c_im, s5_d, s5_w_glu, s5_b_glu, w_proj_s5, hgrn_lb_logits, hgrn_norm_g, w_proj_hgrn, w_out, ln_ffn_g, w_up, conv_w, conv_b, w_down, ln_final_g, loss_target, m_ln_mix_g, m_w_in, m_s5_a_re, m_s5_a_im, m_s5_log_dt, m_s5_b_re, m_s5_b_im, m_s5_c_re, m_s5_c_im, m_s5_d, m_s5_w_glu, m_s5_b_glu, m_w_proj_s5, m_hgrn_lb_logits, m_hgrn_norm_g, m_w_proj_hgrn, m_w_out, m_ln_ffn_g, m_w_up, m_conv_w, m_conv_b, m_w_down, m_ln_final_g, v_ln_mix_g, v_w_in, v_s5_a_re, v_s5_a_im, v_s5_log_dt, v_s5_b_re, v_s5_b_im, v_s5_c_re, v_s5_c_im, v_s5_d, v_s5_w_glu, v_s5_b_glu, v_w_proj_s5, v_hgrn_lb_logits, v_hgrn_norm_g, v_w_proj_hgrn, v_w_out, v_ln_ffn_g, v_w_up, v_conv_w, v_conv_b, v_w_down, v_ln_final_g):
    assert x.shape[0] == 1 and w_in.shape[0] == 1, "one example per device, one layer"
    t, d = x.shape[1], x.shape[2]
    w5 = s5_w_glu.shape[2]
    hw = hgrn_norm_g.shape[1]
    ng_, np_, gc = s5_b_re.shape[1], s5_b_re.shape[2], s5_b_re.shape[3]
    dff = w_down.shape[1] * _NCHIP
    assert gc * _S5_SET == _LANE and ng_ * gc == w5 and hw % _LANE == 0
    gs_off = w5 + 4 * hw
    ci = lax.axis_index("c")
    xt = x.reshape(t, d)
    tgt = loss_target.reshape(t, d)

    big_names = ["w_in", "s5_w_glu", "w_proj_s5", "w_proj_hgrn", "w_out", "w_up", "w_down"]
    big_w = dict(w_in=w_in[0], s5_w_glu=s5_w_glu[0], w_proj_s5=w_proj_s5[0], w_proj_hgrn=w_proj_hgrn[0],
                 w_out=w_out[0], w_up=w_up[0], w_down=w_down[0])
    chip = 2 * lax.axis_index("x") + lax.axis_index("y")
    sel_chip = jnp.stack([chip, ci]).astype(jnp.int32)
    slots = {k: _cast_into_slot(big_w[k], sel_chip) for k in big_names}
    g_in_part, g_cw = _run_plan(_gather_plan([slots["w_in"]], [conv_w[0]], rel=(0, 1)), "gather_w_in")
    cw = g_cw.transpose(1, 0, 2).reshape(3, 2 * dff)
    cb = conv_b

    tm_big = _tile(t, 1024, 8)

    h1 = _rms_fwd(xt, ln_mix_g, "rms1_fwd")
    nin_s = g_in_part.shape[2]
    proj, (g_in,), (g_glu, g_ps5, g_ph, g_out) = _mm_shard_order(
        h1, _gather_plan([g_in_part], [], rel=(2,)),
        _gather_plan([slots[k] for k in ("s5_w_glu", "w_proj_s5", "w_proj_hgrn", "w_out")], []),
        sel_chip, tm=tm_big, tn=_tile(nin_s, 1152), name="mm_proj")
    wglu = g_glu.reshape(w5, w5)
    wout = g_out.reshape(d, d)

    abar_r, abar_i, coef_r, coef_i = _s5_prep(s5_a_re[0], s5_a_im[0], s5_log_dt.reshape(ng_, 1))
    lanes = ng_ * np_

    def fold_coef(cr_, ci_, bre, bim):
        return cr_[..., None] * bre - ci_[..., None] * bim, cr_[..., None] * bim + ci_[..., None] * bre

    (bf_re, bf_im), fold_vjp = jax.vjp(fold_coef, coef_r, coef_i, s5_b_re[0], s5_b_im[0])
    par = (abar_r.reshape(1, lanes), abar_i.reshape(1, lanes),
           _bd_in(bf_re), _bd_in(bf_im), _bd_out(s5_c_re[0]), -_bd_out(s5_c_im[0]), s5_d.reshape(1, w5))
    (y_s5, car, cai), (g_up_part,) = _s5_fwd(proj, par, t, w5, rider=_gather_plan([slots["w_up"]], [], rel=(0, 1)))
    z = _s5glu_fwd(y_s5, wglu, s5_b_glu)
    ys = _mm(z, g_ps5, "nn", _F32, tm=tm_big, tn=g_ps5.shape[2], tk=w5, name="mm_proj_s5")

    lb = _lb_prep(hgrn_lb_logits)
    (og, st_all), (g_up,) = _hgrn_fwd(proj, lb, hgrn_norm_g, t, w5, hw, rider=_gather_plan([g_up_part], [], rel=(2,)))
    yh = _mm(og, g_ph, "nn", _F32, tm=tm_big, tn=g_ph.shape[2], tk=hw, name="mm_proj_hgrn")

    merged = _merge_fwd(proj, ys, yh, t, d, gs_off)
    x2 = _mm(merged, wout, "nn", _F32, tm=tm_big, tn=_tile(d, 1024), tk=d, res=xt, name="mm_out")
    h2 = _rms_fwd(x2, ln_ffn_g, "rms2_fwd")
    up_s = g_up.shape[2]
    w_conv = _tile(dff, 1408)
    nbh = dff // w_conv
    tn_up = _tile(up_s, 1408)
    tk_dh2 = _tile(up_s, w_conv)
    tn_gwup = _tile(up_s // 2, 1408)
    assert w_conv % tn_up == 0 and w_conv % tk_dh2 == 0 and w_conv % tn_gwup == 0
    up, (g_down,) = _mm(h2, g_up, "nn", _BF, tm=tm_big, tn=tn_up, tk=d, name="mm_up",
                        colperm=_pair_perm(nbh, w_conv // tn_up), rider=_gather_plan([slots["w_down"]], []))
    wdown = g_down.reshape(dff, d)
    act = _convact_fwd(up, cw, cb, t, dff)
    x3 = _mm(act, wdown, "nn", _F32, tm=tm_big, tn=_tile(d, 1024), tk=None, res=x2, name="mm_down")
    loss_part, dx3, dx3b, d_gfin = _loss_head(x3, tgt, ln_final_g.reshape(1, d))

    dact = _mm(dx3b, wdown, "nt", _BF, tm=tm_big, tn=_tile(dff, 1408), tk=d, name="mm_dact")
    r_down = dff // _NCHIP
    gw_down = _mm(act, dx3b, "tn", _BF, tm=_tile(r_down, 1408), tn=_tile(d // 2, 1024), tk=None, halves="rows",
                  name="mm_gw_down")
    def chip_sums(grads, name):
        theirs = _swap_halves(grads, name)
        return [_add_pairs(g, th, sel_chip) for g, th in zip(grads, theirs)]

    (s_down,) = chip_sums([gw_down], "swap_halves_d")
    dup, d_cw, d_cb = _convact_bwd(up, dact, cw, cb, t, dff)
    dh2, (r_down,) = _mm(dup, g_up, "nt", _BF, tm=tm_big, tn=_tile(d, 1024), tk=tk_dh2, name="mm_dh2",
                         colperm=_pair_perm(nbh, w_conv // tk_dh2), rider=_exchange_plan([s_down], None))
    gw_up = _mm(h2, dup, "tn", _BF, tm=_tile(d, 1024), tn=tn_gwup, tk=None, halves="cols", name="mm_gw_up",
                colperm=_pair_perm(nbh, w_conv // tn_gwup))
    dx2, dx2b, d_gffn = _rms_bwd(x2, ln_ffn_g, dh2, dx3, "rms2_bwd")
    dmerged = _mm(dx2b, wout, "nt", _BF, tm=tm_big, tn=_tile(d, 1024), tk=d, name="mm_dmerged")
    gw_out = _mm(merged, dx2b, "tn", _BF, tm=_tile(d // _NCHIP, 1024), tn=_tile(d // 2, 1024), tk=None, halves="rows",
                 name="mm_gw_out")
    dys, dyh, dgs, dgh = _merge_bwd(proj, ys, yh, dmerged, t, d, gs_off)
    ps_s = g_ps5.shape[2]
    dz = _mm(dys, g_ps5, "nt", _BF, tm=tm_big, tn=_tile(w5, 1024), tk=ps_s, name="mm_dz")
    gw_ps5 = _mm(z, dys, "tn", _BF, tm=_tile(w5, 1024), tn=ps_s // 2, tk=None, halves="cols", name="mm_gw_ps5")
    dog = _mm(dyh, g_ph, "nt", _BF, tm=tm_big, tn=_tile(hw, 1024), tk=ps_s, name="mm_dog")
    gw_ph = _mm(og, dyh, "tn", _BF, tm=_tile(hw, 1024), tn=ps_s // 2, tk=None, halves="cols", name="mm_gw_ph")
    dy_s5, gw_glu_full, d_bglu = _s5glu_bwd(y_s5, dz, wglu, s5_b_glu)
    r_glu = w5 // _NCHIP
    gw_glu = gw_glu_full.astype(_BF).reshape(_NCHIP, r_glu, 2, w5 // 2).transpose(2, 0, 1, 3)

    s_glu, s_ps5, s_ph, s_out, s_up = chip_sums([gw_glu, gw_ps5, gw_ph, gw_out, gw_up], "swap_halves_a")
    s5g, (r_glu_, r_ps5, r_ph, r_out, r_up) = _s5_bwd(
        proj, dy_s5, car, cai, par, t, w5, rider=_exchange_plan([s_glu, s_ps5, s_ph, s_out, s_up], None))
    du = s5g[0]
    dq, df, di, dg, d_lb, d_ng = _hgrn_bwd(proj, lb, hgrn_norm_g, st_all, dog, t, w5, hw)
    dproj = jnp.concatenate([du, dq, df, di, dg, dgs, dgh], axis=1)

    d_coef_r, d_coef_i, d_bre, d_bim = fold_vjp((_bd_in_grad(s5g[3], np_, gc), _bd_in_grad(s5g[4], np_, gc)))
    d_are, d_aim, d_ldt = _s5_prep_bwd(s5_a_re[0], s5_a_im[0], s5_log_dt.reshape(ng_, 1),
                                       [s5g[1].reshape(ng_, np_), s5g[2].reshape(ng_, np_), d_coef_r, d_coef_i])
    d_cre = _bd_out_grad(s5g[5], np_, gc)
    d_cim = -_bd_out_grad(s5g[6], np_, gc)
    d_logits = _lb_prep_bwd(hgrn_lb_logits, d_lb)

    small_names = ["s5_a_re", "s5_a_im", "s5_log_dt", "s5_b_re", "s5_b_im", "s5_c_re", "s5_c_im", "s5_d",
                   "s5_b_glu", "hgrn_lb_logits", "hgrn_norm_g", "ln_ffn_g", "conv_b", "ln_final_g", "ln_mix_g"]
    small_w = dict(ln_mix_g=ln_mix_g, s5_a_re=s5_a_re, s5_a_im=s5_a_im, s5_log_dt=s5_log_dt, s5_b_re=s5_b_re,
                   s5_b_im=s5_b_im, s5_c_re=s5_c_re, s5_c_im=s5_c_im, s5_d=s5_d, s5_b_glu=s5_b_glu,
                   hgrn_lb_logits=hgrn_lb_logits, hgrn_norm_g=hgrn_norm_g, ln_ffn_g=ln_ffn_g, conv_b=conv_b,
                   ln_final_g=ln_final_g)
    small_m = dict(ln_mix_g=m_ln_mix_g, s5_a_re=m_s5_a_re, s5_a_im=m_s5_a_im, s5_log_dt=m_s5_log_dt, s5_b_re=m_s5_b_re,
                   s5_b_im=m_s5_b_im, s5_c_re=m_s5_c_re, s5_c_im=m_s5_c_im, s5_d=m_s5_d, s5_b_glu=m_s5_b_glu,
                   hgrn_lb_logits=m_hgrn_lb_logits, hgrn_norm_g=m_hgrn_norm_g, ln_ffn_g=m_ln_ffn_g, conv_b=m_conv_b,
                   ln_final_g=m_ln_final_g)
    small_v = dict(ln_mix_g=v_ln_mix_g, s5_a_re=v_s5_a_re, s5_a_im=v_s5_a_im, s5_log_dt=v_s5_log_dt, s5_b_re=v_s5_b_re,
                   s5_b_im=v_s5_b_im, s5_c_re=v_s5_c_re, s5_c_im=v_s5_c_im, s5_d=v_s5_d, s5_b_glu=v_s5_b_glu,
                   hgrn_lb_logits=v_hgrn_lb_logits, hgrn_norm_g=v_hgrn_norm_g, ln_ffn_g=v_ln_ffn_g, conv_b=v_conv_b,
                   ln_final_g=v_ln_final_g)
    small_g = dict(s5_a_re=d_are, s5_a_im=d_aim, s5_log_dt=d_ldt, s5_b_re=d_bre, s5_b_im=d_bim,
                   s5_c_re=d_cre, s5_c_im=d_cim, s5_d=s5g[7], s5_b_glu=d_bglu, hgrn_lb_logits=d_logits,
                   hgrn_norm_g=d_ng, ln_ffn_g=d_gffn, conv_b=d_cb, ln_final_g=d_gfin)
    like = [small_w[k] for k in small_names]
    assert small_names[-1] == "ln_mix_g"
    pack_a = _pack([small_g[k] for k in small_names[:-1]] + [d_cw])
    gw_in, (r_small_a,) = _mm(h1, dproj, "tn", _BF, tm=_tile(d, 1024), tn=_tile(nin_s // 2, 1152), tk=None, halves="cols",
                              name="mm_gw_in", rider=_exchange_plan([], pack_a))
    (s_in,) = chip_sums([gw_in], "swap_halves_b")
    dh1, (r_in,) = _mm(dproj, g_in, "nt", _BF, tm=tm_big, tn=_tile(d, 1024), tk=None, name="mm_dh1",
                       rider=_exchange_plan([s_in], None))
    dx, _, d_gmix = _rms_bwd(xt, ln_mix_g, dh1, dx2, "rms1_bwd")
    (r_small_b,) = _run_plan(_exchange_plan([], _pack([d_gmix])), "exchange_gmix")
    sums = [s_in, s_glu, s_ps5, s_ph, s_out, s_up, s_down]
    received = [r_in, r_glu_, r_ps5, r_ph, r_out, r_up, r_down]
    halves = [_shard_sum(sm, rc, sel_chip) for sm, rc in zip(sums, received)]
    g_a = _sum_slots(r_small_a, "small_sum_a")
    g_b = _sum_slots(r_small_b, "small_sum_b")
    full = _share_halves(halves)
    w_pack = _pack(like)
    rows_a = w_pack.shape[0] - g_b.shape[0]
    g_small = jnp.concatenate([g_a[:rows_a], g_b], axis=0)
    cs = conv_w.shape[2]
    g_cw_full = g_a[rows_a:].reshape(-1)[:3 * 2 * dff].reshape(3, 2 * dff)
    g_cw = lax.dynamic_slice_in_dim(g_cw_full, chip * cs, cs, axis=1)

    big_m = dict(w_in=m_w_in, s5_w_glu=m_s5_w_glu, w_proj_s5=m_w_proj_s5, w_proj_hgrn=m_w_proj_hgrn, w_out=m_w_out,
                 w_up=m_w_up, w_down=m_w_down)
    big_v = dict(w_in=v_w_in, s5_w_glu=v_s5_w_glu, w_proj_s5=v_w_proj_s5, w_proj_hgrn=v_w_proj_hgrn, w_out=v_w_out,
                 w_up=v_w_up, w_down=v_w_down)
    res = {}
    for k, g2 in zip(big_names, full):
        w2 = big_w[k]
        shp = (1,) + w2.shape
        outs = _adamw_shard(w2, g2, big_m[k][0], big_v[k][0], "adamw_" + k)
        res[k] = [o.reshape(shp) for o in outs]
    sm_outs = _adamw_many(like, g_small, [small_m[k] for k in small_names], [small_v[k] for k in small_names])
    for k, outs in zip(small_names, sm_outs):
        res[k] = outs

    cw_outs = _adamw_whole(conv_w[0], g_cw, m_conv_w[0], v_conv_w[0], "adamw_conv_w")
    res["conv_w"] = [g_cw.reshape(conv_w.shape)] + [o.reshape(conv_w.shape) for o in cw_outs]

    loss = lax.psum(loss_part[0, 0], ("x", "y", "c"))
    order = ["ln_mix_g", "w_in", "s5_a_re", "s5_a_im", "s5_log_dt", "s5_b_re", "s5_b_im", "s5_c_re", "s5_c_im", "s5_d",
             "s5_w_glu", "s5_b_glu", "w_proj_s5", "hgrn_lb_logits", "hgrn_norm_g", "w_proj_hgrn", "w_out", "ln_ffn_g",
             "w_up", "conv_w", "conv_b", "w_down", "ln_final_g"]
    return (loss, dx.reshape(x.shape), *[res[k][0] for k in order], *[res[k][1] for k in order],
            *[res[k][2] for k in order], *[res[k][3] for k in order])
```

```python
import functools

import jax
import jax.numpy as jnp
from jax import lax
from jax.experimental import pallas as pl
from jax.experimental.pallas import tpu as pltpu

_F32 = jnp.float32
_BF = jnp.bfloat16
_RMS_EPS = 1e-6
_S5_MAX_RE = -1e-4
_LR, _B1, _B2, _ADAM_EPS, _WD, _STEP = 0.001, 0.9, 0.999, 1e-08, 0.01, 10
_MESH = pl.DeviceIdType.MESH
_ANY = pl.BlockSpec(memory_space=pl.ANY)
_LANE = 128
_VMEM_LIMIT = 56 * 1024 * 1024
_CHUNK = 64
_S5_TB = 1024
_S5_SET = 8
_HGRN_HP = 8
_HGRN_SEG = 512
_HGRN_GROUP = 8
_NCHIP = 4
_NDEV = 8


def _call(body, **kw):
    return pl.pallas_call(body, **kw)


def _cparams(*sem):
    return pltpu.CompilerParams(dimension_semantics=sem, vmem_limit_bytes=_VMEM_LIMIT)


def _tile(n, pref, unit=_LANE):
    if n <= pref:
        return n
    t = (pref // unit) * unit
    while t >= unit:
        if n % t == 0:
            return t
        t -= unit
    raise ValueError(f"no tile for {n}")


_OPERAND_BYTES = 12 * 1024 * 1024


def _tk_fit(k, tm, tn):
    best = _LANE
    for tk in range(_LANE, k + 1, _LANE):
        if k % tk == 0 and (tm + tn) * tk * 2 <= _OPERAND_BYTES:
            best = tk
    return best if k % _LANE == 0 else k


_NN = ((1,), (0,))
_NT = ((1,), (1,))
_TN = ((0,), (0,))


def _dg(a, b, dims):
    return lax.dot_general(a.astype(_BF), b.astype(_BF), (dims, ((), ())), preferred_element_type=_F32)


@jax.custom_vjp
def _bdot(a, b):
    return _dg(a, b, _NN)


def _bdot_f(a, b):
    return _dg(a, b, _NN), (a, b)


def _bdot_b(res, g):
    a, b = res
    return _dg(g, b, _NT).astype(a.dtype), _dg(a, g, _TN).astype(b.dtype)


_bdot.defvjp(_bdot_f, _bdot_b)


@jax.custom_vjp
def _bdot_nt(a, b):
    return _dg(a, b, _NT)


def _bdot_nt_f(a, b):
    return _dg(a, b, _NT), (a, b)


def _bdot_nt_b(res, g):
    a, b = res
    return _dg(g, b, _NN).astype(a.dtype), _dg(g, a, _TN).astype(b.dtype)


_bdot_nt.defvjp(_bdot_nt_f, _bdot_nt_b)


@jax.custom_vjp
def _bdot_tn(a, b):
    return _dg(a, b, _TN)


def _bdot_tn_f(a, b):
    return _dg(a, b, _TN), (a, b)


def _bdot_tn_b(res, g):
    a, b = res
    return _dg(b, g, _NT).astype(a.dtype), _dg(a, g, _NN).astype(b.dtype)


_bdot_tn.defvjp(_bdot_tn_f, _bdot_tn_b)


_SUBLANES = 8


def _shift_up(x, n):
    r = x.shape[0]
    if n % _SUBLANES == 0:
        return jnp.concatenate([x[n:], jnp.zeros((n,) + x.shape[1:], x.dtype)], axis=0)
    row = lax.broadcasted_iota(jnp.int32, x.shape, 0)
    return jnp.where(row < r - n, pltpu.roll(x, r - n, 0), 0.0)


@functools.partial(jax.custom_vjp, nondiff_argnums=(1,))
def _shift_down(x, n):
    if n % _SUBLANES == 0:
        return jnp.concatenate([jnp.zeros((n,) + x.shape[1:], x.dtype), x[:x.shape[0] - n]], axis=0)
    row = lax.broadcasted_iota(jnp.int32, x.shape, 0)
    return jnp.where(row >= n, pltpu.roll(x, n, 0), 0.0)


def _shift_down_f(x, n):
    return _shift_down(x, n), None


def _shift_down_b(n, _, g):
    return (_shift_up(g, n),)


_shift_down.defvjp(_shift_down_f, _shift_down_b)


def _rows_apart(x):
    return tuple(x[k:k + _SUBLANES] for k in range(0, x.shape[0], _SUBLANES))


@jax.custom_vjp
def _split_rows(x):
    return _rows_apart(x)


_split_rows.defvjp(lambda x: (_rows_apart(x), None), lambda _, gs: (jnp.concatenate(gs, axis=0),))


@jax.custom_vjp
def _join_rows(pieces):
    return jnp.concatenate(pieces, axis=0)


_join_rows.defvjp(lambda pieces: (jnp.concatenate(pieces, axis=0), None), lambda _, g: (_rows_apart(g),))


@jax.custom_vjp
def _last_row(x):
    return x[_SUBLANES - 1:_SUBLANES]


def _last_row_b(_, g):
    row = lax.broadcasted_iota(jnp.int32, (_SUBLANES, g.shape[1]), 0)
    return (jnp.where(row == _SUBLANES - 1, g, 0.0),)


_last_row.defvjp(lambda x: (x[_SUBLANES - 1:_SUBLANES], None), _last_row_b)


def _sigmoid(x):
    return 1.0 / (1.0 + jnp.exp(-x))


def _silu(x):
    return x * _sigmoid(x)


def _gelu(x):
    return 0.5 * x * (1.0 + jnp.tanh(0.7978845608028654 * (x + 0.044715 * (x * x * x))))


def _rms_core(x, g):
    return x * lax.rsqrt(jnp.mean(x * x, axis=-1, keepdims=True) + _RMS_EPS) * g


def _mm(a, b, mode, out_dtype, *, tm, tn, tk, res=None, halves=None, rider=None, colperm=None, name):
    if colperm is None:
        def colperm(n_):
            return n_
    if tk is None:
        kdim = a.shape[0] if mode == "tn" else (b.shape[2] if (mode == "nt" and b.ndim == 3) else a.shape[1])
        tk = _tk_fit(kdim, tm, tn)
    if mode == "nn":
        m, k = a.shape
        a_spec = pl.BlockSpec((tm, tk), lambda i, j, kk: (i, kk))
        if b.ndim == 3:
            s, _, ns = b.shape
            n = s * ns
            npb = ns // tn
            b_spec = pl.BlockSpec((None, tk, tn), lambda i, j, kk: (j // npb, kk, j % npb))
        else:
            n = b.shape[1]
            b_spec = pl.BlockSpec((tk, tn), lambda i, j, kk: (kk, j))
        dims = _NN
    elif mode == "nt":
        m, k = a.shape
        a_spec = pl.BlockSpec((tm, tk), lambda i, j, kk: (i, colperm(kk)))
        if b.ndim == 3:
            s, n, ks = b.shape
            kpb = ks // tk
            b_spec = pl.BlockSpec((None, tn, tk), lambda i, j, kk: (kk // kpb, j, kk % kpb))
        else:
            n = b.shape[0]
            b_spec = pl.BlockSpec((tn, tk), lambda i, j, kk: (j, kk))
        dims = _NT
    else:
        k, m = a.shape
        n = b.shape[1]
        a_spec = pl.BlockSpec((tk, tm), lambda i, j, kk: (kk, i))
        b_spec = pl.BlockSpec((tk, tn), lambda i, j, kk: (kk, colperm(j)))
        dims = _TN
    nk = k // tk
    if halves is None:
        out_shape = jax.ShapeDtypeStruct((m, n), out_dtype)
        out_spec = pl.BlockSpec((tm, tn), lambda i, j, kk: (i, colperm(j) if mode == "nn" else j))
    elif halves == "cols":
        c2 = n // (2 * _NCHIP)
        tpc = c2 // tn
        out_shape = jax.ShapeDtypeStruct((2, _NCHIP, m, c2), out_dtype)
        out_spec = pl.BlockSpec((None, None, tm, tn),
                                lambda i, j, kk: ((j // tpc) % 2, j // (2 * tpc), i, j % tpc))
    else:
        c2 = n // 2
        tpc = c2 // tn
        r = m // _NCHIP
        tpr = r // tm
        out_shape = jax.ShapeDtypeStruct((2, _NCHIP, r, c2), out_dtype)
        out_spec = pl.BlockSpec((None, None, tm, tn),
                                lambda i, j, kk: (j // tpc, i // tpr, i % tpr, j % tpc))
    has_res = res is not None
    nreg = 3 if has_res else 2
    ni, nj = m // tm, n // tn
    r_ins = list(rider.ins) if rider else []
    r_outs = list(rider.outs) if rider else []

    def body(*refs):
        a_ref, b_ref = refs[0], refs[1]
        r_ref = refs[2] if has_res else None
        rin = refs[nreg:nreg + len(r_ins)]
        o_ref = refs[nreg + len(r_ins)]
        rout = refs[nreg + len(r_ins) + 1:nreg + len(r_ins) + 1 + len(r_outs)]
        acc_ref = refs[nreg + len(r_ins) + 1 + len(r_outs)]
        sems = refs[nreg + len(r_ins) + 2 + len(r_outs):]
        i, j, kk = pl.program_id(0), pl.program_id(1), pl.program_id(2)

        if rider:
            @pl.when(jnp.logical_and(jnp.logical_and(i == 0, j == 0), kk == 0))
            def _():
                rider.start(rin, rout, sems)

        @pl.when(kk == 0)
        def _():
            acc_ref[...] = jnp.zeros_like(acc_ref)

        acc_ref[...] += _dg(a_ref[...], b_ref[...], dims)

        @pl.when(kk == nk - 1)
        def _():
            out = acc_ref[...]
            if has_res:
                out = out + r_ref[...]
            o_ref[...] = out.astype(out_dtype)

        if rider:
            @pl.when(jnp.logical_and(jnp.logical_and(i == ni - 1, j == nj - 1), kk == nk - 1))
            def _():
                rider.finish(rin, rout, sems)

    in_specs = [a_spec, b_spec]
    args = [a, b]
    if has_res:
        in_specs.append(pl.BlockSpec((tm, tn), lambda i, j, kk: (i, j)))
        args.append(res)
    if not rider:
        return _call(body, name=name, grid=(ni, nj, nk), in_specs=in_specs, out_specs=out_spec,
                     out_shape=out_shape, scratch_shapes=[pltpu.VMEM((tm, tn), _F32)],
                     compiler_params=_cparams("parallel", "parallel", "arbitrary"))(*args)
    res_all = _call(body, name=name, grid=(ni, nj, nk), in_specs=in_specs + [_ANY] * len(r_ins),
                    out_specs=[out_spec] + [_ANY] * len(r_outs), out_shape=[out_shape] + r_outs,
                    input_output_aliases={nreg + k: 1 + v for k, v in rider.aliases.items()},
                    scratch_shapes=[pltpu.VMEM((tm, tn), _F32)] + list(rider.sems),
                    compiler_params=_cparams("arbitrary", "arbitrary", "arbitrary"))(*args, *r_ins)
    return res_all[0], list(res_all[1:])


def _rowcall(fn, ins, in_specs, outs, out_specs, acc, *, grid, name):
    nin = len(ins)

    def body(*refs):
        vals = fn(*[r[...] for r in refs[:nin]])
        first = pl.program_id(1) == 0
        for k, (o_ref, v) in enumerate(zip(refs[nin:], vals)):
            if acc[k]:
                @pl.when(first)
                def _(o_ref=o_ref):
                    o_ref[...] = jnp.zeros_like(o_ref)
                o_ref[...] += v.astype(o_ref.dtype)
            else:
                o_ref[...] = v.astype(o_ref.dtype)

    return _call(body, name=name, grid=grid, in_specs=in_specs, out_specs=out_specs, out_shape=outs,
                 compiler_params=_cparams("parallel", "arbitrary"))(*ins)


def _rb(tm, w, cb=0):
    return pl.BlockSpec((tm, w), lambda j, i: (i, cb + j))


def _cb(w, cb=0):
    return pl.BlockSpec((1, w), lambda j, i: (0, cb + j))


def _full(shape):
    nd = len(shape)
    return pl.BlockSpec(shape, lambda j, i: (0,) * nd)


def _sds(shape, dtype):
    return jax.ShapeDtypeStruct(shape, dtype)


def _rms_fwd(x, g, name):
    t, d = x.shape
    tm = _tile(t, 512, 8)
    return _rowcall(lambda xb, gb: (_rms_core(xb, gb),), [x, g], [_rb(tm, d), _full((1, d))],
                    [_sds((t, d), _BF)], [_rb(tm, d)], [False], grid=(1, t // tm), name=name)[0]


def _rms_bwd(x, g, dh, dres, name):
    t, d = x.shape
    tm = _tile(t, 256, 8)

    def fn(xb, gb, dhb, drb):
        _, vjp = jax.vjp(_rms_core, xb, gb)
        dx, dg = vjp(dhb.astype(_F32))
        dx = dx + drb
        return dx, dx, dg

    return _rowcall(fn, [x, g, dh, dres], [_rb(tm, d), _full((1, d)), _rb(tm, d), _rb(tm, d)],
                    [_sds((t, d), _F32), _sds((t, d), _BF), _sds((1, d), _F32)],
                    [_rb(tm, d), _rb(tm, d), _full((1, d))], [False, False, True],
                    grid=(1, t // tm), name=name)


def _loss_head(x3, tgt, g):
    t, d = x3.shape
    tm = _tile(t, 256, 8)

    def fn(xb, tb, gb):
        y, vjp = jax.vjp(_rms_core, xb, gb)
        e = y - tb
        part = 0.5 * jnp.sum(jnp.mean(e * e, axis=-1, keepdims=True), axis=0, keepdims=True)
        dx, dg = vjp(e * (1.0 / d))
        return jnp.broadcast_to(part, (1, _LANE)), dx, dx, dg

    return _rowcall(fn, [x3, tgt, g], [_rb(tm, d), _rb(tm, d), _full((1, d))],
                    [_sds((1, _LANE), _F32), _sds((t, d), _F32), _sds((t, d), _BF), _sds((1, d), _F32)],
                    [_full((1, _LANE)), _rb(tm, d), _rb(tm, d), _full((1, d))], [True, False, False, True],
                    grid=(1, t // tm), name="loss_head")


def _s5_disc(a_re, a_im, log_dt):
    lam_re = jnp.minimum(a_re, _S5_MAX_RE)
    lam_im = a_im
    dt = jnp.exp(log_dt)
    mag = jnp.exp(lam_re * dt)
    abar_re = mag * jnp.cos(lam_im * dt)
    abar_im = mag * jnp.sin(lam_im * dt)
    den = lam_re * lam_re + lam_im * lam_im
    nr = abar_re - 1.0
    ni = abar_im
    coef_re = (nr * lam_re + ni * lam_im) / den
    coef_im = (ni * lam_re - nr * lam_im) / den
    return abar_re, abar_im, coef_re, coef_im


def _s5_prep(a_re, a_im, log_dt):
    g, p = a_re.shape

    def body(ar, ai, ld, o0, o1, o2, o3):
        outs = _s5_disc(ar[...], ai[...], ld[...])
        for o, v in zip((o0, o1, o2, o3), outs):
            o[...] = v

    return _call(body, name="s5_prep", out_shape=[_sds((g, p), _F32)] * 4)(a_re, a_im, log_dt)


def _s5_prep_bwd(a_re, a_im, log_dt, cts):
    g, p = a_re.shape

    def body(ar, ai, ld, c0, c1, c2, c3, d0, d1, d2):
        _, vjp = jax.vjp(_s5_disc, ar[...], ai[...], ld[...])
        outs = vjp((c0[...], c1[...], c2[...], c3[...]))
        for o, v in zip((d0, d1, d2), outs):
            o[...] = v

    return _call(body, name="s5_prep_bwd",
                 out_shape=[_sds((g, p), _F32), _sds((g, p), _F32), _sds((g, 1), _F32)])(a_re, a_im, log_dt, *cts)


def _s5_block(u, car, cai, ar, ai, b_re, b_im, c_re, c_imn, dvec):
    bur = _bdot(u, b_re)
    bui = _bdot(u, b_im)
    shape8 = (_SUBLANES, ar.shape[1])
    pows = [(ar, ai)]
    for _ in range(2):
        pr, pi = pows[-1]
        pows.append((pr * pr - pi * pi, 2.0 * pr * pi))
    pows = [(jnp.broadcast_to(pr, shape8), jnp.broadcast_to(pi, shape8)) for pr, pi in pows]

    def scan8(xr, xi):
        for k, (pr, pi) in enumerate(pows):
            dr = _shift_down(xr, 1 << k)
            di = _shift_down(xi, 1 << k)
            xr, xi = xr + pr * dr - pi * di, xi + pr * di + pi * dr
        return xr, xi

    row8 = lax.broadcasted_iota(jnp.int32, (_SUBLANES, ar.shape[1]), 0)
    tr, ti = scan8(jnp.where(row8 == 0, ar, 0.0), jnp.where(row8 == 0, ai, 0.0))
    outs_r, outs_i = [], []
    for xr, xi in zip(_split_rows(bur), _split_rows(bui)):
        xr, xi = scan8(xr, xi)
        xr, xi = xr + tr * car - ti * cai, xi + tr * cai + ti * car
        car, cai = _last_row(xr), _last_row(xi)
        outs_r.append(xr)
        outs_i.append(xi)
    y = _bdot(_join_rows(tuple(outs_r)), c_re) + _bdot(_join_rows(tuple(outs_i)), c_imn) + dvec * u
    return y, car, cai


def _s5_specs(tb, lw, nt, rev):
    tmap = (lambda t: nt - 1 - t) if rev else (lambda t: t)
    vec = pl.BlockSpec((1, lw), lambda s, t: (0, s))
    return dict(
        u=pl.BlockSpec((tb, _LANE), lambda s, t: (tmap(t), s)),
        car=pl.BlockSpec((None, 1, lw), lambda s, t: (tmap(t), 0, s)),
        vec=vec,
        bmat=pl.BlockSpec((None, _LANE, lw), lambda s, t: (s, 0, 0)),
        cmat=pl.BlockSpec((None, lw, _LANE), lambda s, t: (s, 0, 0)),
        dvec=pl.BlockSpec((1, _LANE), lambda s, t: (0, s)),
    )


def _s5_fwd(proj, par, t, w5, rider=None):
    ar, ai, b_re, b_im, c_re, c_imn, dvec = par
    ns = w5 // _LANE
    lw = ar.shape[1] // ns
    tb = min(_S5_TB, t)
    nt = t // tb
    sp = _s5_specs(tb, lw, nt, False)

    def body(u_ref, ar_r, ai_r, bre_r, bim_r, cre_r, cim_r, d_r, y_ref, car_ref, cai_ref, s_r, s_i):
        @pl.when(pl.program_id(1) == 0)
        def _():
            s_r[...] = jnp.zeros_like(s_r)
            s_i[...] = jnp.zeros_like(s_i)

        car_ref[...] = s_r[...]
        cai_ref[...] = s_i[...]
        y, ncr, nci = _s5_block(u_ref[...], s_r[...], s_i[...], ar_r[...], ai_r[...],
                                bre_r[...], bim_r[...], cre_r[...], cim_r[...], d_r[...])
        y_ref[...] = y
        s_r[...] = ncr
        s_i[...] = nci

    kw = dict(name="s5_fwd", grid=(ns, nt),
              in_specs=[sp["u"], sp["vec"], sp["vec"], sp["bmat"], sp["bmat"],
                        sp["cmat"], sp["cmat"], sp["dvec"]],
              out_specs=[sp["u"], sp["car"], sp["car"]],
              out_shape=[_sds((t, w5), _F32), _sds((nt, 1, ns * lw), _F32), _sds((nt, 1, ns * lw), _F32)],
              scratch_shapes=[pltpu.VMEM((1, lw), _F32), pltpu.VMEM((1, lw), _F32)])
    args = (proj, ar, ai, b_re, b_im, c_re, c_imn, dvec)
    if rider is None:
        return _call(body, compiler_params=_cparams("parallel", "arbitrary"), **kw)(*args), []
    return _call_riding(body, rider, args=args, **kw)


def _s5_bwd(proj, dy, car, cai, par, t, w5, rider=None):
    ar, ai, b_re, b_im, c_re, c_imn, dvec = par
    ns = w5 // _LANE
    lw = ar.shape[1] // ns
    tb = min(_S5_TB, t)
    nt = t // tb
    sp = _s5_specs(tb, lw, nt, True)

    def body(u_ref, dy_ref, car_ref, cai_ref, ar_r, ai_r, bre_r, bim_r, cre_r, cim_r, d_r,
             du_ref, g_ar, g_ai, g_bre, g_bim, g_cre, g_cim, g_d, ds_r, ds_i):
        accs = (g_ar, g_ai, g_bre, g_bim, g_cre, g_cim, g_d)

        @pl.when(pl.program_id(1) == 0)
        def _():
            ds_r[...] = jnp.zeros_like(ds_r)
            ds_i[...] = jnp.zeros_like(ds_i)
            for o in accs:
                o[...] = jnp.zeros_like(o)

        _, vjp = jax.vjp(_s5_block, u_ref[...], car_ref[...], cai_ref[...], ar_r[...], ai_r[...],
                         bre_r[...], bim_r[...], cre_r[...], cim_r[...], d_r[...])
        grads = vjp((dy_ref[...], ds_r[...], ds_i[...]))
        du_ref[...] = grads[0].astype(_BF)
        ds_r[...] = grads[1]
        ds_i[...] = grads[2]
        for o, gval in zip(accs, grads[3:]):
            o[...] += gval

    vec_o = _sds((1, ns * lw), _F32)
    kw = dict(name="s5_bwd", grid=(ns, nt),
              in_specs=[sp["u"], sp["u"], sp["car"], sp["car"], sp["vec"], sp["vec"],
                        sp["bmat"], sp["bmat"], sp["cmat"], sp["cmat"], sp["dvec"]],
              out_specs=[sp["u"], sp["vec"], sp["vec"], sp["bmat"], sp["bmat"],
                         sp["cmat"], sp["cmat"], sp["dvec"]],
              out_shape=[_sds((t, w5), _BF), vec_o, vec_o,
                         _sds(b_re.shape, _F32), _sds(b_re.shape, _F32), _sds(c_re.shape, _F32),
                         _sds(c_re.shape, _F32), _sds((1, w5), _F32)],
              scratch_shapes=[pltpu.VMEM((1, lw), _F32), pltpu.VMEM((1, lw), _F32)])
    args = (proj, dy, car, cai, ar, ai, b_re, b_im, c_re, c_imn, dvec)
    if rider is None:
        return _call(body, compiler_params=_cparams("parallel", "arbitrary"), **kw)(*args), []
    return _call_riding(body, rider, args=args, **kw)


def _bd_in(b):
    g, p, c = b.shape
    s = g // _S5_SET
    b4 = b.reshape(s, _S5_SET, p, c).transpose(0, 1, 3, 2)
    eye = jnp.eye(_S5_SET, dtype=b.dtype)
    return (b4[:, :, :, None, :] * eye[None, :, None, :, None]).reshape(s, _S5_SET * c, _S5_SET * p)


def _bd_in_grad(d, p, c):
    s = d.shape[0]
    eye = jnp.eye(_S5_SET, dtype=d.dtype)
    d5 = d.reshape(s, _S5_SET, c, _S5_SET, p) * eye[None, :, None, :, None]
    return d5.sum(axis=3).transpose(0, 1, 3, 2).reshape(s * _S5_SET, p, c)


def _bd_out(cm):
    g, c, p = cm.shape
    s = g // _S5_SET
    c4 = cm.reshape(s, _S5_SET, c, p).transpose(0, 1, 3, 2)
    eye = jnp.eye(_S5_SET, dtype=cm.dtype)
    return (c4[:, :, :, None, :] * eye[None, :, None, :, None]).reshape(s, _S5_SET * p, _S5_SET * c)


def _bd_out_grad(d, p, c):
    s = d.shape[0]
    eye = jnp.eye(_S5_SET, dtype=d.dtype)
    d5 = d.reshape(s, _S5_SET, p, _S5_SET, c) * eye[None, :, None, :, None]
    return d5.sum(axis=3).transpose(0, 1, 3, 2).reshape(s * _S5_SET, c, p)


def _s5glu_fwd(y, wglu, bglu):
    t, w5 = y.shape
    tm = _tile(t, 256, 8)

    def fn(yb, wb, bb):
        z1 = _gelu(yb)
        a = _dg(z1, wb, _NN) + bb
        return (z1 * _sigmoid(a),)

    return _rowcall(fn, [y, wglu, bglu], [_rb(tm, w5), _full(wglu.shape), _full((1, w5))],
                    [_sds((t, w5), _BF)], [_rb(tm, w5)], [False], grid=(1, t // tm), name="s5glu_fwd")[0]


def _s5glu_bwd(y, dz, wglu, bglu):
    t, w5 = y.shape
    tm = _tile(t, 256, 8)

    def fn(yb, dzb, wb, bb):
        dzb = dzb.astype(_F32)
        z1, gelu_vjp = jax.vjp(_gelu, yb)
        sig = _sigmoid(_dg(z1, wb, _NN) + bb)
        da = dzb * z1 * sig * (1.0 - sig)
        dz1 = dzb * sig + _dg(da, wb, _NT)
        (dy,) = gelu_vjp(dz1)
        return dy, _dg(z1, da, _TN), jnp.sum(da, axis=0, keepdims=True)

    return _rowcall(fn, [y, dz, wglu, bglu], [_rb(tm, w5), _rb(tm, w5), _full(wglu.shape), _full((1, w5))],
                    [_sds((t, w5), _F32), _sds((w5, w5), _F32), _sds((1, w5), _F32)],
                    [_rb(tm, w5), _full((w5, w5)), _full((1, w5))], [False, True, True],
                    grid=(1, t // tm), name="s5glu_bwd")


_LEVELS = (6, 5, 4, 3, 2, 1)


def _tri_stack(c):
    t = jnp.arange(c, dtype=jnp.int32)[:, None]
    j = jnp.arange(c, dtype=jnp.int32)[None, :]
    low = (j <= t).astype(_F32)
    mats = [low]
    for sh in _LEVELS:
        r = ((t >> sh) << sh) + ((1 << (sh - 1)) - 1)
        mats.append(low - (j <= r).astype(_F32))
    stack = jnp.concatenate(mats, axis=0).astype(_BF)
    return stack, stack.T


def _split_dot(mat, x):
    l = x.shape[1]
    hi = x.astype(_BF)
    lo = (x - hi.astype(_F32)).astype(_BF)
    out = jnp.dot(mat, jnp.concatenate([hi, lo], axis=1), preferred_element_type=_F32)
    return out[:, :l] + out[:, l:]


@jax.custom_vjp
def _decay_sums(lf, tri, tri_t):
    c = lf.shape[0]
    out = _split_dot(tri, lf)
    return tuple(out[k * c:(k + 1) * c] for k in range(len(_LEVELS) + 1))


def _decay_sums_f(lf, tri, tri_t):
    return _decay_sums(lf, tri, tri_t), (tri, tri_t)


def _decay_sums_b(res, gs):
    tri, tri_t = res
    return _split_dot(tri_t, jnp.concatenate(gs, axis=0)), jnp.zeros_like(tri), jnp.zeros_like(tri_t)


_decay_sums.defvjp(_decay_sums_f, _decay_sums_b)


def _hgrn_chunk(qi, fi, vi, gi, st, lb, ng, tri, tri_t):
    heads = range(len(qi))
    c = qi[0].shape[0]
    row = lax.broadcasted_iota(jnp.int32, (c, 1), 0)
    tt = lax.broadcasted_iota(jnp.int32, (c, c), 0)
    ss = lax.broadcasted_iota(jnp.int32, (c, c), 1)
    q = [_silu(qi[h]) for h in heads]
    lf = [jnp.log(lb[h] + (1.0 - lb[h]) * _sigmoid(fi[h])) for h in heads]
    k = [(1.0 - lb[h]) * _sigmoid(-fi[h]) for h in heads]
    sums = [_decay_sums(lf[h], tri, tri_t) for h in heads]
    btot = [jnp.sum(lf[h], axis=0, keepdims=True) for h in heads]
    inter = [_bdot_nt(q[h] * jnp.exp(sums[h][0]), st[h]) for h in heads]
    sc = [jnp.where(tt == ss, jnp.sum(q[h] * k[h], axis=1, keepdims=True), 0.0) for h in heads]
    for li, sh in enumerate(_LEVELS):
        upper = ((row >> (sh - 1)) & 1) == 1
        same = (tt >> sh) == (ss >> sh)
        qm = [jnp.where(upper, q[h] * jnp.exp(jnp.where(upper, sums[h][li + 1], 0.0)), 0.0) for h in heads]
        km = [jnp.where(upper, 0.0, k[h] * jnp.exp(jnp.where(upper, 0.0, -sums[h][li + 1]))) for h in heads]
        prod = [_bdot_nt(qm[h], km[h]) for h in heads]
        sc = [sc[h] + jnp.where(same, prod[h], 0.0) for h in heads]
    o = [inter[h] + _bdot(sc[h], vi[h]) for h in heads]
    st_new = [st[h] * jnp.exp(btot[h]) + _bdot_tn(vi[h], k[h] * jnp.exp(btot[h] - sums[h][0])) for h in heads]
    og = [o[h] * lax.rsqrt(jnp.mean(o[h] * o[h], axis=1, keepdims=True) + _RMS_EPS) * ng[h] * _silu(gi[h])
          for h in heads]
    return tuple(og), tuple(st_new)


def _head_groups(hp):
    g = min(_HGRN_GROUP, hp)
    return [list(range(k, min(k + g, hp))) for k in range(0, hp, g)]


def _hgrn_geom(t, w5, hw):
    hp = _HGRN_HP if (hw // _LANE) % _HGRN_HP == 0 and w5 % (_LANE * _HGRN_HP) == 0 else 1
    seg = min(t, _HGRN_SEG)
    return hp, hp * _LANE, seg, t // seg


def _hgrn_in_specs(seg, wd, w5, hw, tmap):
    nhp = hw // wd
    qb = w5 // wd
    return [pl.BlockSpec((seg, wd), (lambda h, s, k=k: (tmap(s), qb + k * nhp + h))) for k in range(4)]


def _hgrn_fwd(proj, lb, ng, t, w5, hw, rider=None):
    assert _CHUNK == 64
    hp, wd, seg, nseg = _hgrn_geom(t, w5, hw)
    ncs = seg // _CHUNK
    vec = pl.BlockSpec((1, wd), lambda h, s: (0, h))

    tri, tri_t = _tri_stack(_CHUNK)

    def body(q_ref, f_ref, i_ref, g_ref, lb_ref, ng_ref, tri_ref, trit_ref, og_ref, st_ref, s_scr):
        @pl.when(pl.program_id(1) == 0)
        def _():
            s_scr[...] = jnp.zeros_like(s_scr)

        tri_v, trit_v = tri_ref[...], trit_ref[...]

        def step(ci, carry):
            r = pl.ds(pl.multiple_of(ci * _CHUNK, _CHUNK), _CHUNK)
            for grp in _head_groups(hp):
                lns = [slice(hh * _LANE, (hh + 1) * _LANE) for hh in grp]
                for hh in grp:
                    st_ref[hh, ci] = s_scr[hh]
                ogs, sns = _hgrn_chunk(tuple(q_ref[r, ln] for ln in lns), tuple(f_ref[r, ln] for ln in lns),
                                       tuple(i_ref[r, ln] for ln in lns), tuple(g_ref[r, ln] for ln in lns),
                                       tuple(s_scr[hh] for hh in grp), tuple(lb_ref[:, ln] for ln in lns),
                                       tuple(ng_ref[:, ln] for ln in lns), tri_v, trit_v)
                for hh, ln, og, sn in zip(grp, lns, ogs, sns):
                    og_ref[r, ln] = og.astype(_BF)
                    s_scr[hh] = sn
            return carry

        lax.fori_loop(0, ncs, step, 0)

    kw = dict(name="hgrn_fwd", grid=(hw // wd, nseg),
              in_specs=_hgrn_in_specs(seg, wd, w5, hw, lambda s: s) + [
                  vec, vec, pl.BlockSpec(tri.shape, lambda h, s: (0, 0)), pl.BlockSpec(tri_t.shape, lambda h, s: (0, 0))],
              out_specs=[pl.BlockSpec((seg, wd), lambda h, s: (s, h)),
                         pl.BlockSpec((hp, ncs, _LANE, _LANE), lambda h, s: (h, s, 0, 0))],
              out_shape=[_sds((t, hw), _BF), _sds((hw // _LANE, t // _CHUNK, _LANE, _LANE), _F32)],
              scratch_shapes=[pltpu.VMEM((hp, _LANE, _LANE), _F32)])
    args = (proj, proj, proj, proj, lb, ng, tri, tri_t)
    if rider is None:
        return _call(body, compiler_params=_cparams("parallel", "arbitrary"), **kw)(*args), []
    return _call_riding(body, rider, args=args, **kw)


def _hgrn_bwd(proj, lb, ng, st_all, dog, t, w5, hw):
    hp, wd, seg, nseg = _hgrn_geom(t, w5, hw)
    ncs = seg // _CHUNK

    def rev(s):
        return nseg - 1 - s

    vec = pl.BlockSpec((1, wd), lambda h, s: (0, h))
    col = pl.BlockSpec((seg, wd), lambda h, s: (rev(s), h))

    tri, tri_t = _tri_stack(_CHUNK)

    def body(q_ref, f_ref, i_ref, g_ref, lb_ref, ng_ref, tri_ref, trit_ref, st_ref, dog_ref,
             dq_ref, df_ref, di_ref, dg_ref, dlb_ref, dng_ref, ds_scr):
        @pl.when(pl.program_id(1) == 0)
        def _():
            ds_scr[...] = jnp.zeros_like(ds_scr)
            dlb_ref[...] = jnp.zeros_like(dlb_ref)
            dng_ref[...] = jnp.zeros_like(dng_ref)

        tri_v, trit_v = tri_ref[...], trit_ref[...]

        def step(kk, carry):
            ci = ncs - 1 - kk
            r = pl.ds(pl.multiple_of(ci * _CHUNK, _CHUNK), _CHUNK)
            for grp in _head_groups(hp):
                lns = [slice(hh * _LANE, (hh + 1) * _LANE) for hh in grp]
                _, vjp = jax.vjp(_hgrn_chunk, tuple(q_ref[r, ln] for ln in lns), tuple(f_ref[r, ln] for ln in lns),
                                 tuple(i_ref[r, ln] for ln in lns), tuple(g_ref[r, ln] for ln in lns),
                                 tuple(st_ref[hh, ci] for hh in grp), tuple(lb_ref[:, ln] for ln in lns),
                                 tuple(ng_ref[:, ln] for ln in lns), tri_v, trit_v)
                dq, df, di, dg, ds, dlb, dng = vjp((tuple(dog_ref[r, ln].astype(_F32) for ln in lns),
                                                    tuple(ds_scr[hh] for hh in grp)))[:7]
                for n_, (hh, ln) in enumerate(zip(grp, lns)):
                    dq_ref[r, ln] = dq[n_].astype(_BF)
                    df_ref[r, ln] = df[n_].astype(_BF)
                    di_ref[r, ln] = di[n_].astype(_BF)
                    dg_ref[r, ln] = dg[n_].astype(_BF)
                    ds_scr[hh] = ds[n_]
                    dlb_ref[:, ln] += dlb[n_]
                    dng_ref[:, ln] += dng[n_]
            return carry

        lax.fori_loop(0, ncs, step, 0)

    return _call(body, name="hgrn_bwd", grid=(hw // wd, nseg),
                 in_specs=_hgrn_in_specs(seg, wd, w5, hw, rev) + [
                     vec, vec, pl.BlockSpec(tri.shape, lambda h, s: (0, 0)), pl.BlockSpec(tri_t.shape, lambda h, s: (0, 0)),
                     pl.BlockSpec((hp, ncs, _LANE, _LANE), lambda h, s: (h, rev(s), 0, 0)), col],
                 out_specs=[col, col, col, col, vec, vec],
                 out_shape=[_sds((t, hw), _BF)] * 4 + [_sds((1, hw), _F32)] * 2,
                 scratch_shapes=[pltpu.VMEM((hp, _LANE, _LANE), _F32)],
                 compiler_params=_cparams("parallel", "arbitrary"))(
                     proj, proj, proj, proj, lb, ng, tri, tri_t, st_all, dog)


def _lb_of(logits):
    mx = jnp.max(logits, axis=0, keepdims=True)
    e = jnp.exp(logits - mx)
    sm = e / jnp.sum(e, axis=0, keepdims=True)
    row = lax.broadcasted_iota(jnp.int32, logits.shape, 0)
    return jnp.sum(jnp.where(row == 0, sm, 0.0), axis=0, keepdims=True)


def _lb_prep(logits):
    def body(l_ref, o_ref):
        o_ref[...] = _lb_of(l_ref[...])

    return _call(body, name="lb_prep", out_shape=_sds((1, logits.shape[1]), _F32))(logits)


def _lb_prep_bwd(logits, dlb):
    def body(l_ref, d_ref, o_ref):
        _, vjp = jax.vjp(_lb_of, l_ref[...])
        o_ref[...] = vjp(d_ref[...])[0]

    return _call(body, name="lb_prep_bwd", out_shape=_sds(logits.shape, _F32))(logits, dlb)


def _merge_fwd(proj, ys, yh, t, d, gs_off):
    tm = _tile(t, 256, 8)
    w = _tile(d, 1024)
    nb = d // w

    def fn(gs, gh, a, b):
        return (_sigmoid(gs) * a + _sigmoid(gh) * b,)

    return _rowcall(fn, [proj, proj, ys, yh], [_rb(tm, w, gs_off // w), _rb(tm, w, gs_off // w + nb), _rb(tm, w), _rb(tm, w)],
                    [_sds((t, d), _BF)], [_rb(tm, w)], [False], grid=(nb, t // tm), name="merge_fwd")[0]


def _merge_bwd(proj, ys, yh, dm, t, d, gs_off):
    tm = _tile(t, 256, 8)
    w = _tile(d, 1024)
    nb = d // w

    def fn(gs, gh, a, b, g):
        g = g.astype(_F32)
        s1 = _sigmoid(gs)
        s2 = _sigmoid(gh)
        return g * s1, g * s2, g * a * s1 * (1.0 - s1), g * b * s2 * (1.0 - s2)

    return _rowcall(fn, [proj, proj, ys, yh, dm],
                    [_rb(tm, w, gs_off // w), _rb(tm, w, gs_off // w + nb), _rb(tm, w), _rb(tm, w), _rb(tm, w)],
                    [_sds((t, d), _BF)] * 4, [_rb(tm, w)] * 4, [False] * 4, grid=(nb, t // tm), name="merge_bwd")


_HALO = 16


def _prev_rows(up_prev, is_first):
    p1 = jnp.where(is_first, 0.0, up_prev[_HALO - 1:_HALO, :])
    p2 = jnp.where(is_first, 0.0, up_prev[_HALO - 2:_HALO - 1, :])
    return p1, p2


def _causal_taps(cur, p1, p2):
    row = lax.broadcasted_iota(jnp.int32, cur.shape, 0)
    s1 = jnp.where(row == 0, p1, _shift_down(cur, 1))
    s2 = jnp.where(row == 0, p2, jnp.where(row == 1, p1, _shift_down(cur, 2)))
    return s1, s2


def _pair_perm(nbh, per):
    def perm(n_):
        big = n_ // per
        return (2 * (big % nbh) + big // nbh) * per + n_ % per
    return perm


def _conv_specs(tm, w, nb_half, t):
    r8 = tm // _HALO
    cur_g = pl.BlockSpec((tm, w), lambda j, i: (i, 2 * j))
    cur_v = pl.BlockSpec((tm, w), lambda j, i: (i, 2 * j + 1))
    prev_g = pl.BlockSpec((_HALO, w), lambda j, i: (jnp.maximum(i * r8 - 1, 0), 2 * j))
    prev_v = pl.BlockSpec((_HALO, w), lambda j, i: (jnp.maximum(i * r8 - 1, 0), 2 * j + 1))
    w_g = pl.BlockSpec((3, w), lambda j, i: (0, j))
    w_v = pl.BlockSpec((3, w), lambda j, i: (0, nb_half + j))
    b_g = pl.BlockSpec((1, w), lambda j, i: (0, j))
    b_v = pl.BlockSpec((1, w), lambda j, i: (0, nb_half + j))
    return cur_g, cur_v, prev_g, prev_v, w_g, w_v, b_g, b_v


def _conv_of(cur, prev8, wt, bias, is_first):
    cur, prev8 = cur.astype(_F32), prev8.astype(_F32)
    p1, p2 = _prev_rows(prev8, is_first)
    s1, s2 = _causal_taps(cur, p1, p2)
    return bias + wt[0:1, :] * s2 + wt[1:2, :] * s1 + wt[2:3, :] * cur


def _convact_fwd(up, cw, cb, t, dff):
    tm = _tile(t, 512, _HALO)
    w = _tile(dff, 1408)
    nbh = dff // w
    sp = _conv_specs(tm, w, nbh, t)

    def body(ug, uv, pg, pv, wg, wv, bg, bv, o_ref):
        first = pl.program_id(1) == 0
        gate = _conv_of(ug[...], pg[...], wg[...], bg[...], first)
        val = _conv_of(uv[...], pv[...], wv[...], bv[...], first)
        o_ref[...] = (_silu(gate) * val).astype(_BF)

    return _call(body, name="convact_fwd", grid=(nbh, t // tm), in_specs=list(sp),
                 out_specs=pl.BlockSpec((tm, w), lambda j, i: (i, j)), out_shape=_sds((t, dff), _BF),
                 compiler_params=_cparams("parallel", "arbitrary"))(up, up, up, up, cw, cw, cb, cb)


def _convact_bwd(up, dact, cw, cb, t, dff):
    tm = _tile(t, 256, _HALO)
    w = _tile(dff, 1408)
    nbh = dff // w
    r8 = tm // _HALO
    nt = t // tm
    last8 = t // _HALO - 1

    def triple(off):
        return [pl.BlockSpec((tm, w), lambda j, i: (i, 2 * j + off)),
                pl.BlockSpec((_HALO, w), lambda j, i: (jnp.minimum((i + 1) * r8, last8), 2 * j + off)),
                pl.BlockSpec((_HALO, w), lambda j, i: (jnp.maximum(i * r8 - 1, 0), 2 * j + off))]

    def body(ug, ugn, ugp, uv, uvn, uvp, wg_ref, wv_ref, bg_ref, bv_ref, da_ref, dan_ref, du_ref, dw_ref, db_ref):
        i = pl.program_id(1)
        first = i == 0
        is_last = i == nt - 1
        gate = _conv_of(jnp.concatenate([ug[...], ugn[...]], axis=0), ugp[...], wg_ref[...], bg_ref[...], first)
        val = _conv_of(jnp.concatenate([uv[...], uvn[...]], axis=0), uvp[...], wv_ref[...], bv_ref[...], first)
        da = jnp.concatenate([da_ref[...], dan_ref[...]], axis=0).astype(_F32)
        row = lax.broadcasted_iota(jnp.int32, da.shape, 0)
        da = jnp.where(jnp.logical_and(row >= tm, is_last), 0.0, da)
        sg = _sigmoid(gate)
        halves = ((da * val * sg * (1.0 + gate * (1.0 - sg)), wg_ref, ug, ugp),
                  (da * gate * sg, wv_ref, uv, uvp))

        @pl.when(first)
        def _():
            dw_ref[...] = jnp.zeros_like(dw_ref)
            db_ref[...] = jnp.zeros_like(db_ref)

        for h, (dc, w_ref, u_ref, p_ref) in enumerate(halves):
            ln = slice(h * w, (h + 1) * w)
            wt = w_ref[...]
            du = wt[2:3, :] * dc + wt[1:2, :] * _shift_up(dc, 1) + wt[0:1, :] * _shift_up(dc, 2)
            du_ref[:, ln] = du[0:tm, :].astype(_BF)
            dcm = dc[0:tm, :]
            cur = u_ref[...].astype(_F32)
            p1, p2 = _prev_rows(p_ref[...].astype(_F32), first)
            s1, s2 = _causal_taps(cur, p1, p2)
            dw_ref[0:1, ln] += jnp.sum(dcm * s2, axis=0, keepdims=True)
            dw_ref[1:2, ln] += jnp.sum(dcm * s1, axis=0, keepdims=True)
            dw_ref[2:3, ln] += jnp.sum(dcm * cur, axis=0, keepdims=True)
            db_ref[:, ln] += jnp.sum(dcm, axis=0, keepdims=True)

    in_specs = (triple(0) + triple(1)
                + [pl.BlockSpec((3, w), lambda j, i: (0, j)), pl.BlockSpec((3, w), lambda j, i: (0, nbh + j)),
                   pl.BlockSpec((1, w), lambda j, i: (0, j)), pl.BlockSpec((1, w), lambda j, i: (0, nbh + j)),
                   pl.BlockSpec((tm, w), lambda j, i: (i, j)),
                   pl.BlockSpec((_HALO, w), lambda j, i: (jnp.minimum((i + 1) * r8, last8), j))])
    dup, dw_p, db_p = _call(
        body, name="convact_bwd", grid=(nbh, nt), in_specs=in_specs,
        out_specs=[pl.BlockSpec((tm, 2 * w), lambda j, i: (i, j)), pl.BlockSpec((3, 2 * w), lambda j, i: (0, j)),
                   pl.BlockSpec((1, 2 * w), lambda j, i: (0, j))],
        out_shape=[_sds((t, 2 * dff), _BF), _sds((3, 2 * dff), _F32), _sds((1, 2 * dff), _F32)],
        compiler_params=_cparams("parallel", "arbitrary"))(up, up, up, up, up, up, cw, cw, cb, cb, dact, dact)

    def natural(v):
        k = v.shape[0]
        return v.reshape(k, nbh, 2, w).transpose(0, 2, 1, 3).reshape(k, 2 * dff)

    return dup, natural(dw_p), natural(db_p)


def _me():
    return lax.axis_index("x"), lax.axis_index("y"), lax.axis_index("c")


def _other_chips(x, y):
    return [(1 - x, y), (x, 1 - y), (1 - x, 1 - y)]


def _rcopy(src, dst, ssem, rsem, dev):
    return pltpu.make_async_remote_copy(src_ref=src, dst_ref=dst, send_sem=ssem, recv_sem=rsem,
                                        device_id=dev, device_id_type=_MESH)


def _cast_into_slot(w, sel):
    r, c = w.shape
    tm = _tile(r, 256, 16)

    def body(sel_ref, w_ref, o_ref):
        o_ref[...] = w_ref[...].astype(_BF)

    gs = pltpu.PrefetchScalarGridSpec(
        num_scalar_prefetch=1, grid=(r // tm,),
        in_specs=[pl.BlockSpec((tm, c), lambda i, s: (i, 0))],
        out_specs=pl.BlockSpec((None, tm, c), lambda i, s: (s[0], i, 0)))
    return _call(body, name="cast_into_slot", grid_spec=gs, out_shape=_sds((_NCHIP, r, c), _BF),
                 compiler_params=_cparams("parallel"))(sel, w)


class _Plan:
    def __init__(self, ins, outs, aliases, sems, start, finish):
        self.ins, self.outs, self.aliases, self.sems, self.start, self.finish = ins, outs, aliases, sems, start, finish


def _run_plan(plan, name):
    ni, no = len(plan.ins), len(plan.outs)

    def body(*refs):
        rin, rout, sems = refs[:ni], refs[ni:ni + no], refs[ni + no:]
        plan.start(rin, rout, sems)
        plan.finish(rin, rout, sems)

    return _call(body, name=name, in_specs=[_ANY] * ni, out_specs=[_ANY] * no, out_shape=list(plan.outs),
                 input_output_aliases=dict(plan.aliases), scratch_shapes=list(plan.sems))(*plan.ins)


def _call_riding(body, rider, *, name, grid, in_specs, out_specs, out_shape, scratch_shapes, args):
    n_in, n_out, n_scr = len(in_specs), len(out_specs), len(scratch_shapes)
    n_rin, n_rout = len(rider.ins), len(rider.outs)

    def wrapped(*refs):
        ins, rin = refs[:n_in], refs[n_in:n_in + n_rin]
        o0 = n_in + n_rin
        outs, rout = refs[o0:o0 + n_out], refs[o0 + n_out:o0 + n_out + n_rout]
        s0 = o0 + n_out + n_rout
        scratch, sems = refs[s0:s0 + n_scr], refs[s0 + n_scr:]
        first = functools.reduce(jnp.logical_and, [pl.program_id(k) == 0 for k in range(len(grid))])
        last = functools.reduce(jnp.logical_and, [pl.program_id(k) == grid[k] - 1 for k in range(len(grid))])

        @pl.when(first)
        def _():
            rider.start(rin, rout, sems)

        body(*ins, *outs, *scratch)

        @pl.when(last)
        def _():
            rider.finish(rin, rout, sems)

    res = _call(wrapped, name=name, grid=grid, in_specs=list(in_specs) + [_ANY] * n_rin,
                out_specs=list(out_specs) + [_ANY] * n_rout, out_shape=list(out_shape) + list(rider.outs),
                input_output_aliases={n_in + k: n_out + v for k, v in rider.aliases.items()},
                scratch_shapes=list(scratch_shapes) + list(rider.sems),
                compiler_params=_cparams(*(["arbitrary"] * len(grid))))(*args, *rider.ins)
    return list(res[:n_out]), list(res[n_out:])


def _gather_plan(bufs, direct, rel=(0, 1, 2)):
    n, nd = len(bufs), len(direct)

    def where():
        x, y, c = _me()
        return c, 2 * x + y, _other_chips(x, y), (x, y, 1 - c)

    def picked(chips):
        return [(j, chips[j]) for j in rel]

    def piece(outs, a, chip, h):
        r2 = bufs[a].shape[1] // 2
        return outs[a].at[chip, pl.ds(h * r2, r2)]

    def send(outs, sems, a, j, me, c, chip):
        return _rcopy(piece(outs, a, me, c), piece(outs, a, me, c), sems[0].at[3 * a + j], sems[1].at[3 * a + j],
                      (chip[0], chip[1], c))

    def forward(outs, sems, a, j, pc, c, sib):
        return _rcopy(piece(outs, a, pc, c), piece(outs, a, pc, c), sems[2].at[3 * a + j], sems[3].at[3 * a + j], sib)

    def dsend(dins, douts, sems, a, j, me, c, chip):
        return _rcopy(dins[a], douts[a].at[me], sems[4].at[3 * a + j], sems[5].at[3 * a + j], (chip[0], chip[1], c))

    def start(rin, rout, sems):
        outs, dins, douts = rout[:n], rin[n:], rout[n:]
        c, me, chips, _ = where()
        for a in range(n):
            for j, chip in picked(chips):
                send(outs, sems, a, j, me, c, chip).start()
        for a in range(nd):
            pltpu.make_async_copy(dins[a], douts[a].at[me], sems[6].at[a]).start()
            for j, chip in enumerate(chips):
                dsend(dins, douts, sems, a, j, me, c, chip).start()

    def finish(rin, rout, sems):
        outs, dins, douts = rout[:n], rin[n:], rout[n:]
        c, me, chips, sib = where()
        for a in range(n):
            for j, (cx, cy) in picked(chips):
                pc = 2 * cx + cy
                _rcopy(piece(outs, a, me, c), piece(outs, a, pc, c), sems[0].at[3 * a + j], sems[1].at[3 * a + j],
                       (cx, cy, c)).wait_recv()
                forward(outs, sems, a, j, pc, c, sib).start()
        for a in range(n):
            for j, (cx, cy) in picked(chips):
                pc = 2 * cx + cy
                _rcopy(piece(outs, a, pc, 1 - c), piece(outs, a, pc, 1 - c), sems[2].at[3 * a + j],
                       sems[3].at[3 * a + j], sib).wait_recv()
        for a in range(nd):
            for j, (cx, cy) in enumerate(chips):
                _rcopy(dins[a], douts[a].at[2 * cx + cy], sems[4].at[3 * a + j], sems[5].at[3 * a + j],
                       (cx, cy, c)).wait_recv()
        for a in range(n):
            for j, (cx, cy) in picked(chips):
                send(outs, sems, a, j, me, c, (cx, cy)).wait_send()
                forward(outs, sems, a, j, 2 * cx + cy, c, sib).wait_send()
        for a in range(nd):
            pltpu.make_async_copy(dins[a], douts[a].at[me], sems[6].at[a]).wait()
            for j, chip in enumerate(chips):
                dsend(dins, douts, sems, a, j, me, c, chip).wait_send()

    dma = pltpu.SemaphoreType.DMA
    return _Plan(list(bufs) + list(direct),
                 [_sds(b.shape, b.dtype) for b in bufs] + [_sds((_NCHIP,) + s.shape, s.dtype) for s in direct],
                 {a: a for a in range(n)},
                 [dma((3 * max(n, 1),)), dma((3 * max(n, 1),)), dma((3 * max(n, 1),)), dma((3 * max(n, 1),)),
                  dma((3 * max(nd, 1),)), dma((3 * max(nd, 1),)), dma((max(nd, 1),))], start, finish)


def _mm_shard_order(a, mid_plan, end_plan, sel, *, tm, tn, name):
    b = mid_plan.ins[0]
    s, k, ns = b.shape
    m = a.shape[0]
    npb = ns // tn
    nj, ni = s * npb, m // tm
    n_in = (len(mid_plan.ins), len(end_plan.ins))
    n_out = (len(mid_plan.outs), len(end_plan.outs))
    n_sem = (len(mid_plan.sems), len(end_plan.sems))

    def shard_of(jj, q):
        return jnp.bitwise_xor(q[0], jj // npb)

    def body(q_ref, a_ref, *rest):
        rin = (rest[:n_in[0]], rest[n_in[0]:sum(n_in)])
        o_ref = rest[sum(n_in)]
        o0 = sum(n_in) + 1
        rout = (rest[o0:o0 + n_out[0]], rest[o0 + n_out[0]:o0 + sum(n_out)])
        vbuf, bsem = rest[o0 + sum(n_out)], rest[o0 + sum(n_out) + 1]
        s0 = o0 + sum(n_out) + 2
        sems = (rest[s0:s0 + n_sem[0]], rest[s0 + n_sem[0]:s0 + sum(n_sem)])
        b_ref = rout[0][0]
        j, i = pl.program_id(0), pl.program_id(1)

        def tile_copy(jj, slot):
            return pltpu.make_async_copy(b_ref.at[shard_of(jj, q_ref), :, pl.ds((jj % npb) * tn, tn)],
                                         vbuf.at[slot], bsem.at[slot])

        @pl.when(jnp.logical_and(j == 0, i == 0))
        def _():
            mid_plan.start(rin[0], rout[0], sems[0])
            end_plan.start(rin[1], rout[1], sems[1])
            tile_copy(0, 0).start()

        @pl.when(i == 0)
        def _():
            @pl.when(j == (s - 1) * npb - 1)
            def _():
                mid_plan.finish(rin[0], rout[0], sems[0])

            @pl.when(j + 1 < nj)
            def _():
                tile_copy(j + 1, (j + 1) % 2).start()

            tile_copy(j, j % 2).wait()

        o_ref[...] = _dg(a_ref[...], vbuf[j % 2], _NN)

        @pl.when(jnp.logical_and(j == nj - 1, i == ni - 1))
        def _():
            end_plan.finish(rin[1], rout[1], sems[1])

    assert s == _NCHIP and nj >= 2
    n_rin, n_rout = sum(n_in), sum(n_out)
    gs = pltpu.PrefetchScalarGridSpec(
        num_scalar_prefetch=1, grid=(nj, ni),
        in_specs=[pl.BlockSpec((tm, k), lambda j, i, q: (i, 0))] + [_ANY] * n_rin,
        out_specs=[pl.BlockSpec((tm, tn), lambda j, i, q: (i, shard_of(j, q) * npb + j % npb))] + [_ANY] * n_rout,
        scratch_shapes=[pltpu.VMEM((2, k, tn), _BF), pltpu.SemaphoreType.DMA((2,))] + list(mid_plan.sems) + list(end_plan.sems))
    aliases = {2 + kk: 1 + v for kk, v in mid_plan.aliases.items()}
    aliases.update({2 + n_in[0] + kk: 1 + n_out[0] + v for kk, v in end_plan.aliases.items()})
    res = _call(body, name=name, grid_spec=gs, out_shape=[_sds((m, s * ns), _F32)] + list(mid_plan.outs) + list(end_plan.outs),
                input_output_aliases=aliases,
                compiler_params=_cparams("arbitrary", "arbitrary"))(sel, a, *mid_plan.ins, *end_plan.ins)
    return res[0], list(res[1:1 + n_out[0]]), list(res[1 + n_out[0]:])


def _swap_halves(grads, name):
    n = len(grads)

    def body(*refs):
        ins, outs = refs[:n], refs[n:2 * n]
        ssem, rsem = refs[2 * n:]
        x, y, c = _me()
        sib = (x, y, 1 - c)
        cps = []
        for a in range(n):
            cp = _rcopy(ins[a].at[1 - c], outs[a], ssem.at[a], rsem.at[a], sib)
            cp.start()
            cps.append(cp)
        for cp in cps:
            cp.wait_recv()
        for cp in cps:
            cp.wait_send()

    dma = pltpu.SemaphoreType.DMA
    return _call(body, name=name, in_specs=[_ANY] * n, out_specs=[_ANY] * n,
                 out_shape=[_sds(g.shape[1:], g.dtype) for g in grads],
                 scratch_shapes=[dma((n,)), dma((n,))])(*grads)


def _add_pairs(grads, theirs, sel):
    _, s, r, c2 = grads.shape
    a3 = grads.reshape(2, s * r, c2)
    b2 = theirs.reshape(s * r, c2)
    tm = _tile(s * r, 512, 16)

    def body(sel_ref, a_ref, b_ref, o_ref):
        o_ref[...] = (a_ref[...].astype(_F32) + b_ref[...].astype(_F32)).astype(_BF)

    gs = pltpu.PrefetchScalarGridSpec(
        num_scalar_prefetch=1, grid=(s * r // tm,),
        in_specs=[pl.BlockSpec((None, tm, c2), lambda i, q: (q[1], i, 0)), pl.BlockSpec((tm, c2), lambda i, q: (i, 0))],
        out_specs=pl.BlockSpec((tm, c2), lambda i, q: (i, 0)))
    out = _call(body, name="chip_sum", grid_spec=gs, out_shape=_sds((s * r, c2), _BF),
                compiler_params=_cparams("parallel"))(sel, a3, b2)
    return out.reshape(s, r, c2)


def _exchange_plan(sums, small):
    n = len(sums)
    has_small = small is not None

    def where():
        x, y, c = _me()
        peers = [(1 - x if k & 4 else x, 1 - y if k & 2 else y, 1 - c if k & 1 else c) for k in range(1, _NDEV)]
        return c, 4 * x + 2 * y + c, _other_chips(x, y), peers

    def send(rin, rout, sems, a, j, c, chip):
        return _rcopy(rin[a].at[2 * chip[0] + chip[1]], rout[a].at[j], sems[0].at[3 * a + j], sems[1].at[3 * a + j],
                      (chip[0], chip[1], c))

    def small_send(rin, rout, sems, k, dev, peer):
        return _rcopy(rin[n], rout[n].at[dev], sems[2].at[k], sems[3].at[k], peer)

    def start(rin, rout, sems):
        c, dev, chips, peers = where()
        for a in range(n):
            for j, chip in enumerate(chips):
                send(rin, rout, sems, a, j, c, chip).start()
        if has_small:
            pltpu.make_async_copy(rin[n], rout[n].at[dev], sems[4].at[0]).start()
            for k, peer in enumerate(peers):
                small_send(rin, rout, sems, k, dev, peer).start()

    def finish(rin, rout, sems):
        c, dev, chips, peers = where()
        for a in range(n):
            for j, chip in enumerate(chips):
                send(rin, rout, sems, a, j, c, chip).wait_recv()
        if has_small:
            for k, (px, py, pc_) in enumerate(peers):
                _rcopy(rin[n], rout[n].at[4 * px + 2 * py + pc_], sems[2].at[k], sems[3].at[k], (px, py, pc_)).wait_recv()
        for a in range(n):
            for j, chip in enumerate(chips):
                send(rin, rout, sems, a, j, c, chip).wait_send()
        if has_small:
            pltpu.make_async_copy(rin[n], rout[n].at[dev], sems[4].at[0]).wait()
            for k, peer in enumerate(peers):
                small_send(rin, rout, sems, k, dev, peer).wait_send()

    dma = pltpu.SemaphoreType.DMA
    outs = [_sds((3,) + s.shape[1:], s.dtype) for s in sums]
    if has_small:
        outs.append(_sds((_NDEV,) + small.shape, small.dtype))
    return _Plan(list(sums) + ([small] if has_small else []), outs, {},
                 [dma((3 * max(n, 1),)), dma((3 * max(n, 1),)), dma((_NDEV - 1,)), dma((_NDEV - 1,)), dma((1,))],
                 start, finish)


def _shard_sum(sums, recv, sel):
    s, r, c2 = sums.shape
    tm = _tile(r, 256, 16)

    def body(sel_ref, own_ref, rc_ref, o_ref):
        rc = rc_ref[...]
        o_ref[...] = ((own_ref[...].astype(_F32) + rc[0].astype(_F32)) + rc[1].astype(_F32)) + rc[2].astype(_F32)

    gs = pltpu.PrefetchScalarGridSpec(
        num_scalar_prefetch=1, grid=(r // tm,),
        in_specs=[pl.BlockSpec((None, tm, c2), lambda i, q: (q[0], i, 0)),
                  pl.BlockSpec((3, tm, c2), lambda i, q: (0, i, 0))],
        out_specs=pl.BlockSpec((None, tm, c2), lambda i, q: (q[1], i, 0)))
    return _call(body, name="shard_sum", grid_spec=gs, out_shape=_sds((2, r, c2), _F32),
                 compiler_params=_cparams("parallel"))(sel, sums, recv)


def _sum_slots(stack, name):
    k, r, c = stack.shape
    tm = _tile(r, 256, 16 if stack.dtype == _BF else 8)

    def fn(v):
        out = v[0].astype(_F32)
        for i in range(1, k):
            out = out + v[i].astype(_F32)
        return (out,)

    return _rowcall(fn, [stack], [pl.BlockSpec((k, tm, c), lambda j, i: (0, i, 0))], [_sds((r, c), _F32)],
                    [_rb(tm, c)], [False], grid=(1, r // tm), name=name)[0]


def _share_halves(bufs):
    n = len(bufs)

    def body(*refs):
        outs = refs[n:2 * n]
        ssem, rsem = refs[2 * n:]
        x, y, c = _me()
        sib = (x, y, 1 - c)
        cps = []
        for a in range(n):
            cp = _rcopy(outs[a].at[c], outs[a].at[c], ssem.at[a], rsem.at[a], sib)
            cp.start()
            cps.append(cp)
        for a in range(n):
            _rcopy(outs[a].at[c], outs[a].at[1 - c], ssem.at[a], rsem.at[a], sib).wait_recv()
        for cp in cps:
            cp.wait_send()

    dma = pltpu.SemaphoreType.DMA
    return _call(body, name="share_halves", in_specs=[_ANY] * n, out_specs=[_ANY] * n,
                 out_shape=[_sds(b.shape, b.dtype) for b in bufs], input_output_aliases={a: a for a in range(n)},
                 scratch_shapes=[dma((n,)), dma((n,))])(*bufs)


def _adamw_math(w, g, m, v):
    m = _B1 * m + (1.0 - _B1) * g
    v = _B2 * v + (1.0 - _B2) * jnp.square(g)
    m_hat = m / (1.0 - _B1 ** _STEP)
    v_hat = v / (1.0 - _B2 ** _STEP)
    delta = -_LR * (m_hat / (jnp.sqrt(v_hat) + _ADAM_EPS) + _WD * w)
    return delta, m, v


def _adamw_shard(w, g2, m, v, name):
    r, c = w.shape
    c2 = c // 2
    tm = _tile(r, 256, 8)
    blk = pl.BlockSpec((tm, c2), lambda h, i: (i, h))

    def fn(wb, gb, mb, vb):
        return (gb,) + _adamw_math(wb, gb, mb, vb)

    return _rowcall(fn, [w, g2, m, v], [blk, pl.BlockSpec((None, tm, c2), lambda h, i: (h, i, 0)), blk, blk],
                    [_sds((r, c), _F32)] * 4, [blk] * 4, [False] * 4, grid=(2, r // tm), name=name)


def _adamw_whole(w, g, m, v, name):
    r, c = w.shape
    blk = _full((r, c))
    return _rowcall(lambda *a: _adamw_math(*a), [w, g, m, v], [blk] * 4, [_sds((r, c), _F32)] * 3, [blk] * 3,
                    [False] * 3, grid=(1, 1), name=name)


def _adamw_many(ws, g_pack, ms, vs):
    k = len(ws)
    views, offs, off = [], [], 0
    for w in ws:
        n = w.size
        views.append((n // _LANE, _LANE) if n % _LANE == 0 else (1, n))
        offs.append(off)
        off += (n + (-n) % (8 * _LANE)) // _LANE

    def body(*refs):
        g_ref, w_refs, m_refs, v_refs = refs[0], refs[1:1 + k], refs[1 + k:1 + 2 * k], refs[1 + 2 * k:1 + 3 * k]
        outs = refs[1 + 3 * k:]
        for i in range(k):
            r, c = views[i]
            g = g_ref[offs[i]:offs[i] + r, 0:c]
            res = (g,) + _adamw_math(w_refs[i][...], g, m_refs[i][...], v_refs[i][...])
            for o_ref, val in zip(outs[4 * i:4 * i + 4], res):
                o_ref[...] = val

    args = [g_pack] + [a.reshape(views[i]) for grp in (ws, ms, vs) for i, a in enumerate(grp)]
    res = _call(body, name="adamw_small", out_shape=[_sds(views[i], _F32) for i in range(k) for _ in range(4)])(*args)
    return [[res[4 * i + j].reshape(ws[i].shape) for j in range(4)] for i in range(k)]


def _pack(arrs):
    parts = []
    for a in arrs:
        f = a.reshape(-1).astype(_F32)
        pad = (-f.shape[0]) % (8 * _LANE)
        if pad:
            f = jnp.concatenate([f, jnp.zeros((pad,), _F32)])
        parts.append(f)
    return jnp.concatenate(parts).reshape(-1, _LANE)


def kernel(x, ln_mix_g, w_in, s5_a_re, s5_a_im, s5_log_dt, s5_b_re, s5_b_im, s5_c_re, s5_c_im, s5_d, s5_w_glu, s5_b_glu, w_proj_s5, hgrn_lb_logits, hgrn_norm_g, w_proj_hgrn, w_out, ln_ffn_g, w_up, conv_w, conv_b, w_down, ln_final_g, loss_target, m_ln_mix_g, m_w_in, m_s5_a_re, m_s5_a_im, m_s5_log_dt, m_s5_b_re, m_s5_b_im, m_s5_c_re, m_s5_c_im, m_s5_d, m_s5_w_glu, m_s5_b_glu, m_w_proj_s5, m_hgrn_lb_logits, m_hgrn_norm_g, m_w_proj_hgrn, m_w_out, m_ln_ffn_g, m_w_up, m_conv_w, m_conv_b, m_w_down, m_ln_final_g, v_ln_mix_g, v_w_in, v_s5_a_re, v_s5_a_im, v_s5_log_dt, v_s5_b_re, v_s5_b_im, v_s5_c_re, v_s5_c_im, v_s5_d, v_s5_w_glu, v_s5_b_glu, v_w_proj_s5, v_hgrn_lb_logits, v_hgrn_norm_g, v_w_proj_hgrn, v_w_out, v_ln_ffn_g, v_w_up, v_conv_w, v_conv_b, v_w_down, v_ln_final_g):
    assert x.shape[0] == 1 and w_in.shape[0] == 1, "one example per device, one layer"
    t, d = x.shape[1], x.shape[2]
    w5 = s5_w_glu.shape[2]
    hw = hgrn_norm_g.shape[1]
    ng_, np_, gc = s5_b_re.shape[1], s5_b_re.shape[2], s5_b_re.shape[3]
    dff = w_down.shape[1] * _NCHIP
    assert gc * _S5_SET == _LANE and ng_ * gc == w5 and hw % _LANE == 0
    gs_off = w5 + 4 * hw
    ci = lax.axis_index("c")
    xt = x.reshape(t, d)
    tgt = loss_target.reshape(t, d)

    big_names = ["w_in", "s5_w_glu", "w_proj_s5", "w_proj_hgrn", "w_out", "w_up", "w_down"]
    big_w = dict(w_in=w_in[0], s5_w_glu=s5_w_glu[0], w_proj_s5=w_proj_s5[0], w_proj_hgrn=w_proj_hgrn[0],
                 w_out=w_out[0], w_up=w_up[0], w_down=w_down[0])
    chip = 2 * lax.axis_index("x") + lax.axis_index("y")
    sel_chip = jnp.stack([chip, ci]).astype(jnp.int32)
    slots = {k: _cast_into_slot(big_w[k], sel_chip) for k in big_names}
    g_in_part, g_cw = _run_plan(_gather_plan([slots["w_in"]], [conv_w[0]], rel=(0, 1)), "gather_w_in")
    cw = g_cw.transpose(1, 0, 2).reshape(3, 2 * dff)
    cb = conv_b

    tm_big = _tile(t, 1024, 8)

    h1 = _rms_fwd(xt, ln_mix_g, "rms1_fwd")
    nin_s = g_in_part.shape[2]
    proj, (g_in,), (g_glu, g_ps5, g_ph, g_out) = _mm_shard_order(
        h1, _gather_plan([g_in_part], [], rel=(2,)),
        _gather_plan([slots[k] for k in ("s5_w_glu", "w_proj_s5", "w_proj_hgrn", "w_out")], []),
        sel_chip, tm=_tile(t, 512, 8), tn=nin_s, name="mm_proj")
    wglu = g_glu.reshape(w5, w5)
    wout = g_out.reshape(d, d)

    abar_r, abar_i, coef_r, coef_i = _s5_prep(s5_a_re[0], s5_a_im[0], s5_log_dt.reshape(ng_, 1))
    lanes = ng_ * np_

    def fold_coef(cr_, ci_, bre, bim):
        return cr_[..., None] * bre - ci_[..., None] * bim, cr_[..., None] * bim + ci_[..., None] * bre

    (bf_re, bf_im), fold_vjp = jax.vjp(fold_coef, coef_r, coef_i, s5_b_re[0], s5_b_im[0])
    par = (abar_r.reshape(1, lanes), abar_i.reshape(1, lanes),
           _bd_in(bf_re), _bd_in(bf_im), _bd_out(s5_c_re[0]), -_bd_out(s5_c_im[0]), s5_d.reshape(1, w5))
    (y_s5, car, cai), (g_up_part,) = _s5_fwd(proj, par, t, w5, rider=_gather_plan([slots["w_up"]], [], rel=(0, 1)))
    z = _s5glu_fwd(y_s5, wglu, s5_b_glu)
    ys = _mm(z, g_ps5, "nn", _F32, tm=tm_big, tn=g_ps5.shape[2], tk=w5, name="mm_proj_s5")

    lb = _lb_prep(hgrn_lb_logits)
    (og, st_all), (g_up,) = _hgrn_fwd(proj, lb, hgrn_norm_g, t, w5, hw, rider=_gather_plan([g_up_part], [], rel=(2,)))
    yh = _mm(og, g_ph, "nn", _F32, tm=tm_big, tn=g_ph.shape[2], tk=hw, name="mm_proj_hgrn")

    merged = _merge_fwd(proj, ys, yh, t, d, gs_off)
    x2 = _mm(merged, wout, "nn", _F32, tm=tm_big, tn=_tile(d, 1024), tk=d, res=xt, name="mm_out")
    h2 = _rms_fwd(x2, ln_ffn_g, "rms2_fwd")
    up_s = g_up.shape[2]
    w_conv = _tile(dff, 1408)
    nbh = dff // w_conv
    tn_up = _tile(up_s, 1408)
    tk_dh2 = _tile(up_s, w_conv)
    tn_gwup = _tile(up_s // 2, 1408)
    assert w_conv % tn_up == 0 and w_conv % tk_dh2 == 0 and w_conv % tn_gwup == 0
    up, (g_down,) = _mm(h2, g_up, "nn", _BF, tm=tm_big, tn=tn_up, tk=d, name="mm_up",
                        colperm=_pair_perm(nbh, w_conv // tn_up), rider=_gather_plan([slots["w_down"]], []))
    wdown = g_down.reshape(dff, d)
    act = _convact_fwd(up, cw, cb, t, dff)
    x3 = _mm(act, wdown, "nn", _F32, tm=tm_big, tn=_tile(d, 1024), tk=None, res=x2, name="mm_down")
    loss_part, dx3, dx3b, d_gfin = _loss_head(x3, tgt, ln_final_g.reshape(1, d))

    dact = _mm(dx3b, wdown, "nt", _BF, tm=tm_big, tn=_tile(dff, 1408), tk=d, name="mm_dact")
    r_down = dff // _NCHIP
    gw_down = _mm(act, dx3b, "tn", _BF, tm=_tile(r_down, 1408), tn=_tile(d // 2, 1024), tk=None, halves="rows",
                  name="mm_gw_down")
    def chip_sums(grads, name):
        theirs = _swap_halves(grads, name)
        return [_add_pairs(g, th, sel_chip) for g, th in zip(grads, theirs)]

    (s_down,) = chip_sums([gw_down], "swap_halves_d")
    dup, d_cw, d_cb = _convact_bwd(up, dact, cw, cb, t, dff)
    dh2, (r_down,) = _mm(dup, g_up, "nt", _BF, tm=tm_big, tn=_tile(d, 1024), tk=tk_dh2, name="mm_dh2",
                         colperm=_pair_perm(nbh, w_conv // tk_dh2), rider=_exchange_plan([s_down], None))
    gw_up = _mm(h2, dup, "tn", _BF, tm=_tile(d, 1024), tn=tn_gwup, tk=None, halves="cols", name="mm_gw_up",
                colperm=_pair_perm(nbh, w_conv // tn_gwup))
    dx2, dx2b, d_gffn = _rms_bwd(x2, ln_ffn_g, dh2, dx3, "rms2_bwd")
    dmerged = _mm(dx2b, wout, "nt", _BF, tm=tm_big, tn=_tile(d, 1024), tk=d, name="mm_dmerged")
    gw_out = _mm(merged, dx2b, "tn", _BF, tm=_tile(d // _NCHIP, 1024), tn=_tile(d // 2, 1024), tk=None, halves="rows",
                 name="mm_gw_out")
    dys, dyh, dgs, dgh = _merge_bwd(proj, ys, yh, dmerged, t, d, gs_off)
    ps_s = g_ps5.shape[2]
    dz = _mm(dys, g_ps5, "nt", _BF, tm=tm_big, tn=_tile(w5, 1024), tk=ps_s, name="mm_dz")
    gw_ps5 = _mm(z, dys, "tn", _BF, tm=_tile(w5, 1024), tn=ps_s // 2, tk=None, halves="cols", name="mm_gw_ps5")
    dog = _mm(dyh, g_ph, "nt", _BF, tm=tm_big, tn=_tile(hw, 1024), tk=ps_s, name="mm_dog")
    gw_ph = _mm(og, dyh, "tn", _BF, tm=_tile(hw, 1024), tn=ps_s // 2, tk=None, halves="cols", name="mm_gw_ph")
    dy_s5, gw_glu_full, d_bglu = _s5glu_bwd(y_s5, dz, wglu, s5_b_glu)
    r_glu = w5 // _NCHIP
    gw_glu = gw_glu_full.astype(_BF).reshape(_NCHIP, r_glu, 2, w5 // 2).transpose(2, 0, 1, 3)

    s_glu, s_ps5, s_ph, s_out, s_up = chip_sums([gw_glu, gw_ps5, gw_ph, gw_out, gw_up], "swap_halves_a")
    s5g, (r_glu_, r_ps5, r_ph, r_out, r_up) = _s5_bwd(
        proj, dy_s5, car, cai, par, t, w5, rider=_exchange_plan([s_glu, s_ps5, s_ph, s_out, s_up], None))
    du = s5g[0]
    dq, df, di, dg, d_lb, d_ng = _hgrn_bwd(proj, lb, hgrn_norm_g, st_all, dog, t, w5, hw)
    dproj = jnp.concatenate([du, dq, df, di, dg, dgs, dgh], axis=1)

    d_coef_r, d_coef_i, d_bre, d_bim = fold_vjp((_bd_in_grad(s5g[3], np_, gc), _bd_in_grad(s5g[4], np_, gc)))
    d_are, d_aim, d_ldt = _s5_prep_bwd(s5_a_re[0], s5_a_im[0], s5_log_dt.reshape(ng_, 1),
                                       [s5g[1].reshape(ng_, np_), s5g[2].reshape(ng_, np_), d_coef_r, d_coef_i])
    d_cre = _bd_out_grad(s5g[5], np_, gc)
    d_cim = -_bd_out_grad(s5g[6], np_, gc)
    d_logits = _lb_prep_bwd(hgrn_lb_logits, d_lb)

    small_names = ["s5_a_re", "s5_a_im", "s5_log_dt", "s5_b_re", "s5_b_im", "s5_c_re", "s5_c_im", "s5_d",
                   "s5_b_glu", "hgrn_lb_logits", "hgrn_norm_g", "ln_ffn_g", "conv_b", "ln_final_g", "ln_mix_g"]
    small_w = dict(ln_mix_g=ln_mix_g, s5_a_re=s5_a_re, s5_a_im=s5_a_im, s5_log_dt=s5_log_dt, s5_b_re=s5_b_re,
                   s5_b_im=s5_b_im, s5_c_re=s5_c_re, s5_c_im=s5_c_im, s5_d=s5_d, s5_b_glu=s5_b_glu,
                   hgrn_lb_logits=hgrn_lb_logits, hgrn_norm_g=hgrn_norm_g, ln_ffn_g=ln_ffn_g, conv_b=conv_b,
                   ln_final_g=ln_final_g)
    small_m = dict(ln_mix_g=m_ln_mix_g, s5_a_re=m_s5_a_re, s5_a_im=m_s5_a_im, s5_log_dt=m_s5_log_dt, s5_b_re=m_s5_b_re,
                   s5_b_im=m_s5_b_im, s5_c_re=m_s5_c_re, s5_c_im=m_s5_c_im, s5_d=m_s5_d, s5_b_glu=m_s5_b_glu,
                   hgrn_lb_logits=m_hgrn_lb_logits, hgrn_norm_g=m_hgrn_norm_g, ln_ffn_g=m_ln_ffn_g, conv_b=m_conv_b,
                   ln_final_g=m_ln_final_g)
    small_v = dict(ln_mix_g=v_ln_mix_g, s5_a_re=v_s5_a_re, s5_a_im=v_s5_a_im, s5_log_dt=v_s5_log_dt, s5_b_re=v_s5_b_re,
                   s5_b_im=v_s5_b_im, s5_c_re=v_s5_c_re, s5_c_im=v_s5_c_im, s5_d=v_s5_d, s5_b_glu=v_s5_b_glu,
                   hgrn_lb_logits=v_hgrn_lb_logits, hgrn_norm_g=v_hgrn_norm_g, ln_ffn_g=v_ln_ffn_g, conv_b=v_conv_b,
                   ln_final_g=v_ln_final_g)
    small_g = dict(s5_a_re=d_are, s5_a_im=d_aim, s5_log_dt=d_ldt, s5_b_re=d_bre, s5_b_im=d_bim,
                   s5_c_re=d_cre, s5_c_im=d_cim, s5_d=s5g[7], s5_b_glu=d_bglu, hgrn_lb_logits=d_logits,
                   hgrn_norm_g=d_ng, ln_ffn_g=d_gffn, conv_b=d_cb, ln_final_g=d_gfin)
    like = [small_w[k] for k in small_names]
    assert small_names[-1] == "ln_mix_g"
    pack_a = _pack([small_g[k] for k in small_names[:-1]] + [d_cw])
    gw_in, (r_small_a,) = _mm(h1, dproj, "tn", _BF, tm=_tile(d, 1024), tn=_tile(nin_s // 2, 1152), tk=None, halves="cols",
                              name="mm_gw_in", rider=_exchange_plan([], pack_a))
    (s_in,) = chip_sums([gw_in], "swap_halves_b")
    dh1, (r_in,) = _mm(dproj, g_in, "nt", _BF, tm=tm_big, tn=_tile(d, 1024), tk=None, name="mm_dh1",
                       rider=_exchange_plan([s_in], None))
    dx, _, d_gmix = _rms_bwd(xt, ln_mix_g, dh1, dx2, "rms1_bwd")
    (r_small_b,) = _run_plan(_exchange_plan([], _pack([d_gmix])), "exchange_gmix")
    sums = [s_in, s_glu, s_ps5, s_ph, s_out, s_up, s_down]
    received = [r_in, r_glu_, r_ps5, r_ph, r_out, r_up, r_down]
    halves = [_shard_sum(sm, rc, sel_chip) for sm, rc in zip(sums, received)]
    g_a = _sum_slots(r_small_a, "small_sum_a")
    g_b = _sum_slots(r_small_b, "small_sum_b")
    full = _share_halves(halves)
    w_pack = _pack(like)
    rows_a = w_pack.shape[0] - g_b.shape[0]
    g_small = jnp.concatenate([g_a[:rows_a], g_b], axis=0)
    cs = conv_w.shape[2]
    g_cw_full = g_a[rows_a:].reshape(-1)[:3 * 2 * dff].reshape(3, 2 * dff)
    g_cw = lax.dynamic_slice_in_dim(g_cw_full, chip * cs, cs, axis=1)

    big_m = dict(w_in=m_w_in, s5_w_glu=m_s5_w_glu, w_proj_s5=m_w_proj_s5, w_proj_hgrn=m_w_proj_hgrn, w_out=m_w_out,
                 w_up=m_w_up, w_down=m_w_down)
    big_v = dict(w_in=v_w_in, s5_w_glu=v_s5_w_glu, w_proj_s5=v_w_proj_s5, w_proj_hgrn=v_w_proj_hgrn, w_out=v_w_out,
                 w_up=v_w_up, w_down=v_w_down)
    res = {}
    for k, g2 in zip(big_names, full):
        w2 = big_w[k]
        shp = (1,) + w2.shape
        outs = _adamw_shard(w2, g2, big_m[k][0], big_v[k][0], "adamw_" + k)
        res[k] = [o.reshape(shp) for o in outs]
    sm_outs = _adamw_many(like, g_small, [small_m[k] for k in small_names], [small_v[k] for k in small_names])
    for k, outs in zip(small_names, sm_outs):
        res[k] = outs

    cw_outs = _adamw_whole(conv_w[0], g_cw, m_conv_w[0], v_conv_w[0], "adamw_conv_w")
    res["conv_w"] = [g_cw.reshape(conv_w.shape)] + [o.reshape(conv_w.shape) for o in cw_outs]

    loss = lax.psum(loss_part[0, 0], ("x", "y", "c"))
    order = ["ln_mix_g", "w_in", "s5_a_re", "s5_a_im", "s5_log_dt", "s5_b_re", "s5_b_im", "s5_c_re", "s5_c_im", "s5_d",
             "s5_w_glu", "s5_b_glu", "w_proj_s5", "hgrn_lb_logits", "hgrn_norm_g", "w_proj_hgrn", "w_out", "ln_ffn_g",
             "w_up", "conv_w", "conv_b", "w_down", "ln_final_g"]
    return (loss, dx.reshape(x.shape), *[res[k][0] for k in order], *[res[k][1] for k in order],
            *[res[k][2] for k in order], *[res[k][3] for k in order])
```

```python
import functools

import jax
import jax.numpy as jnp
from jax import lax
from jax.experimental import pallas as pl
from jax.experimental.pallas import tpu as pltpu

_F32 = jnp.float32
_BF = jnp.bfloat16
_RMS_EPS = 1e-6
_S5_MAX_RE = -1e-4
_LR, _B1, _B2, _ADAM_EPS, _WD, _STEP = 0.001, 0.9, 0.999, 1e-08, 0.01, 10
_MESH = pl.DeviceIdType.MESH
_ANY = pl.BlockSpec(memory_space=pl.ANY)
_LANE = 128
_VMEM_LIMIT = 56 * 1024 * 1024
_CHUNK = 64
_S5_TB = 1024
_S5_SET = 8
_HGRN_HP = 8
_HGRN_SEG = 512
_HGRN_GROUP = 8
_NCHIP = 4
_NDEV = 8


def _call(body, **kw):
    return pl.pallas_call(body, **kw)


def _cparams(*sem):
    return pltpu.CompilerParams(dimension_semantics=sem, vmem_limit_bytes=_VMEM_LIMIT)


def _tile(n, pref, unit=_LANE):
    if n <= pref:
        return n
    t = (pref // unit) * unit
    while t >= unit:
        if n % t == 0:
            return t
        t -= unit
    raise ValueError(f"no tile for {n}")


_OPERAND_BYTES = 12 * 1024 * 1024


def _tk_fit(k, tm, tn):
    best = _LANE
    for tk in range(_LANE, k + 1, _LANE):
        if k % tk == 0 and (tm + tn) * tk * 2 <= _OPERAND_BYTES:
            best = tk
    return best if k % _LANE == 0 else k


_NN = ((1,), (0,))
_NT = ((1,), (1,))
_TN = ((0,), (0,))


def _dg(a, b, dims):
    return lax.dot_general(a.astype(_BF), b.astype(_BF), (dims, ((), ())), preferred_element_type=_F32)


@jax.custom_vjp
def _bdot(a, b):
    return _dg(a, b, _NN)


def _bdot_f(a, b):
    return _dg(a, b, _NN), (a, b)


def _bdot_b(res, g):
    a, b = res
    return _dg(g, b, _NT).astype(a.dtype), _dg(a, g, _TN).astype(b.dtype)


_bdot.defvjp(_bdot_f, _bdot_b)


@jax.custom_vjp
def _bdot_nt(a, b):
    return _dg(a, b, _NT)


def _bdot_nt_f(a, b):
    return _dg(a, b, _NT), (a, b)


def _bdot_nt_b(res, g):
    a, b = res
    return _dg(g, b, _NN).astype(a.dtype), _dg(g, a, _TN).astype(b.dtype)


_bdot_nt.defvjp(_bdot_nt_f, _bdot_nt_b)


@jax.custom_vjp
def _bdot_tn(a, b):
    return _dg(a, b, _TN)


def _bdot_tn_f(a, b):
    return _dg(a, b, _TN), (a, b)


def _bdot_tn_b(res, g):
    a, b = res
    return _dg(b, g, _NT).astype(a.dtype), _dg(a, g, _NN).astype(b.dtype)


_bdot_tn.defvjp(_bdot_tn_f, _bdot_tn_b)


_SUBLANES = 8


def _shift_up(x, n):
    r = x.shape[0]
    if n % _SUBLANES == 0:
        return jnp.concatenate([x[n:], jnp.zeros((n,) + x.shape[1:], x.dtype)], axis=0)
    row = lax.broadcasted_iota(jnp.int32, x.shape, 0)
    return jnp.where(row < r - n, pltpu.roll(x, r - n, 0), 0.0)


@functools.partial(jax.custom_vjp, nondiff_argnums=(1,))
def _shift_down(x, n):
    if n % _SUBLANES == 0:
        return jnp.concatenate([jnp.zeros((n,) + x.shape[1:], x.dtype), x[:x.shape[0] - n]], axis=0)
    row = lax.broadcasted_iota(jnp.int32, x.shape, 0)
    return jnp.where(row >= n, pltpu.roll(x, n, 0), 0.0)


def _shift_down_f(x, n):
    return _shift_down(x, n), None


def _shift_down_b(n, _, g):
    return (_shift_up(g, n),)


_shift_down.defvjp(_shift_down_f, _shift_down_b)


def _rows_apart(x):
    return tuple(x[k:k + _SUBLANES] for k in range(0, x.shape[0], _SUBLANES))


@jax.custom_vjp
def _split_rows(x):
    return _rows_apart(x)


_split_rows.defvjp(lambda x: (_rows_apart(x), None), lambda _, gs: (jnp.concatenate(gs, axis=0),))


@jax.custom_vjp
def _join_rows(pieces):
    return jnp.concatenate(pieces, axis=0)


_join_rows.defvjp(lambda pieces: (jnp.concatenate(pieces, axis=0), None), lambda _, g: (_rows_apart(g),))


@jax.custom_vjp
def _last_row(x):
    return x[_SUBLANES - 1:_SUBLANES]


def _last_row_b(_, g):
    row = lax.broadcasted_iota(jnp.int32, (_SUBLANES, g.shape[1]), 0)
    return (jnp.where(row == _SUBLANES - 1, g, 0.0),)


_last_row.defvjp(lambda x: (x[_SUBLANES - 1:_SUBLANES], None), _last_row_b)


def _sigmoid(x):
    return 1.0 / (1.0 + jnp.exp(-x))


def _silu(x):
    return x * _sigmoid(x)


def _gelu(x):
    return 0.5 * x * (1.0 + jnp.tanh(0.7978845608028654 * (x + 0.044715 * (x * x * x))))


def _rms_core(x, g):
    return x * lax.rsqrt(jnp.mean(x * x, axis=-1, keepdims=True) + _RMS_EPS) * g


def _mm(a, b, mode, out_dtype, *, tm, tn, tk, res=None, halves=None, rider=None, colperm=None, name):
    if colperm is None:
        def colperm(n_):
            return n_
    if tk is None:
        kdim = a.shape[0] if mode == "tn" else (b.shape[2] if (mode == "nt" and b.ndim == 3) else a.shape[1])
        tk = _tk_fit(kdim, tm, tn)
    if mode == "nn":
        m, k = a.shape
        a_spec = pl.BlockSpec((tm, tk), lambda i, j, kk: (i, kk))
        if b.ndim == 3:
            s, _, ns = b.shape
            n = s * ns
            npb = ns // tn
            b_spec = pl.BlockSpec((None, tk, tn), lambda i, j, kk: (j // npb, kk, j % npb))
        else:
            n = b.shape[1]
            b_spec = pl.BlockSpec((tk, tn), lambda i, j, kk: (kk, j))
        dims = _NN
    elif mode == "nt":
        m, k = a.shape
        a_spec = pl.BlockSpec((tm, tk), lambda i, j, kk: (i, colperm(kk)))
        if b.ndim == 3:
            s, n, ks = b.shape
            kpb = ks // tk
            b_spec = pl.BlockSpec((None, tn, tk), lambda i, j, kk: (kk // kpb, j, kk % kpb))
        else:
            n = b.shape[0]
            b_spec = pl.BlockSpec((tn, tk), lambda i, j, kk: (j, kk))
        dims = _NT
    else:
        k, m = a.shape
        n = b.shape[1]
        a_spec = pl.BlockSpec((tk, tm), lambda i, j, kk: (kk, i))
        b_spec = pl.BlockSpec((tk, tn), lambda i, j, kk: (kk, colperm(j)))
        dims = _TN
    nk = k // tk
    if halves is None:
        out_shape = jax.ShapeDtypeStruct((m, n), out_dtype)
        out_spec = pl.BlockSpec((tm, tn), lambda i, j, kk: (i, colperm(j) if mode == "nn" else j))
    elif halves == "cols":
        c2 = n // (2 * _NCHIP)
        tpc = c2 // tn
        out_shape = jax.ShapeDtypeStruct((2, _NCHIP, m, c2), out_dtype)
        out_spec = pl.BlockSpec((None, None, tm, tn),
                                lambda i, j, kk: ((j // tpc) % 2, j // (2 * tpc), i, j % tpc))
    else:
        c2 = n // 2
        tpc = c2 // tn
        r = m // _NCHIP
        tpr = r // tm
        out_shape = jax.ShapeDtypeStruct((2, _NCHIP, r, c2), out_dtype)
        out_spec = pl.BlockSpec((None, None, tm, tn),
                                lambda i, j, kk: (j // tpc, i // tpr, i % tpr, j % tpc))
    has_res = res is not None
    nreg = 3 if has_res else 2
    ni, nj = m // tm, n // tn
    r_ins = list(rider.ins) if rider else []
    r_outs = list(rider.outs) if rider else []

    def body(*refs):
        a_ref, b_ref = refs[0], refs[1]
        r_ref = refs[2] if has_res else None
        rin = refs[nreg:nreg + len(r_ins)]
        o_ref = refs[nreg + len(r_ins)]
        rout = refs[nreg + len(r_ins) + 1:nreg + len(r_ins) + 1 + len(r_outs)]
        acc_ref = refs[nreg + len(r_ins) + 1 + len(r_outs)]
        sems = refs[nreg + len(r_ins) + 2 + len(r_outs):]
        i, j, kk = pl.program_id(0), pl.program_id(1), pl.program_id(2)

        if rider:
            @pl.when(jnp.logical_and(jnp.logical_and(i == 0, j == 0), kk == 0))
            def _():
                rider.start(rin, rout, sems)

        @pl.when(kk == 0)
        def _():
            acc_ref[...] = jnp.zeros_like(acc_ref)

        acc_ref[...] += _dg(a_ref[...], b_ref[...], dims)

        @pl.when(kk == nk - 1)
        def _():
            out = acc_ref[...]
            if has_res:
                out = out + r_ref[...]
            o_ref[...] = out.astype(out_dtype)

        if rider:
            @pl.when(jnp.logical_and(jnp.logical_and(i == ni - 1, j == nj - 1), kk == nk - 1))
            def _():
                rider.finish(rin, rout, sems)

    in_specs = [a_spec, b_spec]
    args = [a, b]
    if has_res:
        in_specs.append(pl.BlockSpec((tm, tn), lambda i, j, kk: (i, j)))
        args.append(res)
    if not rider:
        return _call(body, name=name, grid=(ni, nj, nk), in_specs=in_specs, out_specs=out_spec,
                     out_shape=out_shape, scratch_shapes=[pltpu.VMEM((tm, tn), _F32)],
                     compiler_params=_cparams("parallel", "parallel", "arbitrary"))(*args)
    res_all = _call(body, name=name, grid=(ni, nj, nk), in_specs=in_specs + [_ANY] * len(r_ins),
                    out_specs=[out_spec] + [_ANY] * len(r_outs), out_shape=[out_shape] + r_outs,
                    input_output_aliases={nreg + k: 1 + v for k, v in rider.aliases.items()},
                    scratch_shapes=[pltpu.VMEM((tm, tn), _F32)] + list(rider.sems),
                    compiler_params=_cparams("arbitrary", "arbitrary", "arbitrary"))(*args, *r_ins)
    return res_all[0], list(res_all[1:])


def _rowcall(fn, ins, in_specs, outs, out_specs, acc, *, grid, name):
    nin = len(ins)

    def body(*refs):
        vals = fn(*[r[...] for r in refs[:nin]])
        first = pl.program_id(1) == 0
        for k, (o_ref, v) in enumerate(zip(refs[nin:], vals)):
            if acc[k]:
                @pl.when(first)
                def _(o_ref=o_ref):
                    o_ref[...] = jnp.zeros_like(o_ref)
                o_ref[...] += v.astype(o_ref.dtype)
            else:
                o_ref[...] = v.astype(o_ref.dtype)

    return _call(body, name=name, grid=grid, in_specs=in_specs, out_specs=out_specs, out_shape=outs,
                 compiler_params=_cparams("parallel", "arbitrary"))(*ins)


def _rb(tm, w, cb=0):
    return pl.BlockSpec((tm, w), lambda j, i: (i, cb + j))


def _cb(w, cb=0):
    return pl.BlockSpec((1, w), lambda j, i: (0, cb + j))


def _full(shape):
    nd = len(shape)
    return pl.BlockSpec(shape, lambda j, i: (0,) * nd)


def _sds(shape, dtype):
    return jax.ShapeDtypeStruct(shape, dtype)


def _rms_fwd(x, g, name):
    t, d = x.shape
    tm = _tile(t, 512, 8)
    return _rowcall(lambda xb, gb: (_rms_core(xb, gb),), [x, g], [_rb(tm, d), _full((1, d))],
                    [_sds((t, d), _BF)], [_rb(tm, d)], [False], grid=(1, t // tm), name=name)[0]


def _rms_bwd(x, g, dh, dres, name):
    t, d = x.shape
    tm = _tile(t, 256, 8)

    def fn(xb, gb, dhb, drb):
        _, vjp = jax.vjp(_rms_core, xb, gb)
        dx, dg = vjp(dhb.astype(_F32))
        dx = dx + drb
        return dx, dx, dg

    return _rowcall(fn, [x, g, dh, dres], [_rb(tm, d), _full((1, d)), _rb(tm, d), _rb(tm, d)],
                    [_sds((t, d), _F32), _sds((t, d), _BF), _sds((1, d), _F32)],
                    [_rb(tm, d), _rb(tm, d), _full((1, d))], [False, False, True],
                    grid=(1, t // tm), name=name)


def _loss_head(x3, tgt, g):
    t, d = x3.shape
    tm = _tile(t, 256, 8)

    def fn(xb, tb, gb):
        y, vjp = jax.vjp(_rms_core, xb, gb)
        e = y - tb
        part = 0.5 * jnp.sum(jnp.mean(e * e, axis=-1, keepdims=True), axis=0, keepdims=True)
        dx, dg = vjp(e * (1.0 / d))
        return jnp.broadcast_to(part, (1, _LANE)), dx, dx, dg

    return _rowcall(fn, [x3, tgt, g], [_rb(tm, d), _rb(tm, d), _full((1, d))],
                    [_sds((1, _LANE), _F32), _sds((t, d), _F32), _sds((t, d), _BF), _sds((1, d), _F32)],
                    [_full((1, _LANE)), _rb(tm, d), _rb(tm, d), _full((1, d))], [True, False, False, True],
                    grid=(1, t // tm), name="loss_head")


def _s5_disc(a_re, a_im, log_dt):
    lam_re = jnp.minimum(a_re, _S5_MAX_RE)
    lam_im = a_im
    dt = jnp.exp(log_dt)
    mag = jnp.exp(lam_re * dt)
    abar_re = mag * jnp.cos(lam_im * dt)
    abar_im = mag * jnp.sin(lam_im * dt)
    den = lam_re * lam_re + lam_im * lam_im
    nr = abar_re - 1.0
    ni = abar_im
    coef_re = (nr * lam_re + ni * lam_im) / den
    coef_im = (ni * lam_re - nr * lam_im) / den
    return abar_re, abar_im, coef_re, coef_im


def _s5_prep(a_re, a_im, log_dt):
    g, p = a_re.shape

    def body(ar, ai, ld, o0, o1, o2, o3):
        outs = _s5_disc(ar[...], ai[...], ld[...])
        for o, v in zip((o0, o1, o2, o3), outs):
            o[...] = v

    return _call(body, name="s5_prep", out_shape=[_sds((g, p), _F32)] * 4)(a_re, a_im, log_dt)


def _s5_prep_bwd(a_re, a_im, log_dt, cts):
    g, p = a_re.shape

    def body(ar, ai, ld, c0, c1, c2, c3, d0, d1, d2):
        _, vjp = jax.vjp(_s5_disc, ar[...], ai[...], ld[...])
        outs = vjp((c0[...], c1[...], c2[...], c3[...]))
        for o, v in zip((d0, d1, d2), outs):
            o[...] = v

    return _call(body, name="s5_prep_bwd",
                 out_shape=[_sds((g, p), _F32), _sds((g, p), _F32), _sds((g, 1), _F32)])(a_re, a_im, log_dt, *cts)


def _s5_block(u, car, cai, ar, ai, b_re, b_im, c_re, c_imn, dvec):
    bur = _bdot(u, b_re)
    bui = _bdot(u, b_im)
    shape8 = (_SUBLANES, ar.shape[1])
    pows = [(ar, ai)]
    for _ in range(2):
        pr, pi = pows[-1]
        pows.append((pr * pr - pi * pi, 2.0 * pr * pi))
    pows = [(jnp.broadcast_to(pr, shape8), jnp.broadcast_to(pi, shape8)) for pr, pi in pows]

    def scan8(xr, xi):
        for k, (pr, pi) in enumerate(pows):
            dr = _shift_down(xr, 1 << k)
            di = _shift_down(xi, 1 << k)
            xr, xi = xr + pr * dr - pi * di, xi + pr * di + pi * dr
        return xr, xi

    row8 = lax.broadcasted_iota(jnp.int32, (_SUBLANES, ar.shape[1]), 0)
    tr, ti = scan8(jnp.where(row8 == 0, ar, 0.0), jnp.where(row8 == 0, ai, 0.0))
    outs_r, outs_i = [], []
    for xr, xi in zip(_split_rows(bur), _split_rows(bui)):
        xr, xi = scan8(xr, xi)
        xr, xi = xr + tr * car - ti * cai, xi + tr * cai + ti * car
        car, cai = _last_row(xr), _last_row(xi)
        outs_r.append(xr)
        outs_i.append(xi)
    y = _bdot(_join_rows(tuple(outs_r)), c_re) + _bdot(_join_rows(tuple(outs_i)), c_imn) + dvec * u
    return y, car, cai


def _s5_specs(tb, lw, nt, rev):
    tmap = (lambda t: nt - 1 - t) if rev else (lambda t: t)
    vec = pl.BlockSpec((1, lw), lambda s, t: (0, s))
    return dict(
        u=pl.BlockSpec((tb, _LANE), lambda s, t: (tmap(t), s)),
        car=pl.BlockSpec((None, 1, lw), lambda s, t: (tmap(t), 0, s)),
        vec=vec,
        bmat=pl.BlockSpec((None, _LANE, lw), lambda s, t: (s, 0, 0)),
        cmat=pl.BlockSpec((None, lw, _LANE), lambda s, t: (s, 0, 0)),
        dvec=pl.BlockSpec((1, _LANE), lambda s, t: (0, s)),
    )


def _s5_fwd(proj, par, t, w5, rider=None):
    ar, ai, b_re, b_im, c_re, c_imn, dvec = par
    ns = w5 // _LANE
    lw = ar.shape[1] // ns
    tb = min(_S5_TB, t)
    nt = t // tb
    sp = _s5_specs(tb, lw, nt, False)

    def body(u_ref, ar_r, ai_r, bre_r, bim_r, cre_r, cim_r, d_r, y_ref, car_ref, cai_ref, s_r, s_i):
        @pl.when(pl.program_id(1) == 0)
        def _():
            s_r[...] = jnp.zeros_like(s_r)
            s_i[...] = jnp.zeros_like(s_i)

        car_ref[...] = s_r[...]
        cai_ref[...] = s_i[...]
        y, ncr, nci = _s5_block(u_ref[...], s_r[...], s_i[...], ar_r[...], ai_r[...],
                                bre_r[...], bim_r[...], cre_r[...], cim_r[...], d_r[...])
        y_ref[...] = y
        s_r[...] = ncr
        s_i[...] = nci

    kw = dict(name="s5_fwd", grid=(ns, nt),
              in_specs=[sp["u"], sp["vec"], sp["vec"], sp["bmat"], sp["bmat"],
                        sp["cmat"], sp["cmat"], sp["dvec"]],
              out_specs=[sp["u"], sp["car"], sp["car"]],
              out_shape=[_sds((t, w5), _F32), _sds((nt, 1, ns * lw), _F32), _sds((nt, 1, ns * lw), _F32)],
              scratch_shapes=[pltpu.VMEM((1, lw), _F32), pltpu.VMEM((1, lw), _F32)])
    args = (proj, ar, ai, b_re, b_im, c_re, c_imn, dvec)
    if rider is None:
        return _call(body, compiler_params=_cparams("parallel", "arbitrary"), **kw)(*args), []
    return _call_riding(body, rider, args=args, **kw)


def _s5_bwd(proj, dy, car, cai, par, t, w5, rider=None):
    ar, ai, b_re, b_im, c_re, c_imn, dvec = par
    ns = w5 // _LANE
    lw = ar.shape[1] // ns
    tb = min(_S5_TB, t)
    nt = t // tb
    sp = _s5_specs(tb, lw, nt, True)

    def body(u_ref, dy_ref, car_ref, cai_ref, ar_r, ai_r, bre_r, bim_r, cre_r, cim_r, d_r,
             du_ref, g_ar, g_ai, g_bre, g_bim, g_cre, g_cim, g_d, ds_r, ds_i):
        accs = (g_ar, g_ai, g_bre, g_bim, g_cre, g_cim, g_d)

        @pl.when(pl.program_id(1) == 0)
        def _():
            ds_r[...] = jnp.zeros_like(ds_r)
            ds_i[...] = jnp.zeros_like(ds_i)
            for o in accs:
                o[...] = jnp.zeros_like(o)

        _, vjp = jax.vjp(_s5_block, u_ref[...], car_ref[...], cai_ref[...], ar_r[...], ai_r[...],
                         bre_r[...], bim_r[...], cre_r[...], cim_r[...], d_r[...])
        grads = vjp((dy_ref[...], ds_r[...], ds_i[...]))
        du_ref[...] = grads[0].astype(_BF)
        ds_r[...] = grads[1]
        ds_i[...] = grads[2]
        for o, gval in zip(accs, grads[3:]):
            o[...] += gval

    vec_o = _sds((1, ns * lw), _F32)
    kw = dict(name="s5_bwd", grid=(ns, nt),
              in_specs=[sp["u"], sp["u"], sp["car"], sp["car"], sp["vec"], sp["vec"],
                        sp["bmat"], sp["bmat"], sp["cmat"], sp["cmat"], sp["dvec"]],
              out_specs=[sp["u"], sp["vec"], sp["vec"], sp["bmat"], sp["bmat"],
                         sp["cmat"], sp["cmat"], sp["dvec"]],
              out_shape=[_sds((t, w5), _BF), vec_o, vec_o,
                         _sds(b_re.shape, _F32), _sds(b_re.shape, _F32), _sds(c_re.shape, _F32),
                         _sds(c_re.shape, _F32), _sds((1, w5), _F32)],
              scratch_shapes=[pltpu.VMEM((1, lw), _F32), pltpu.VMEM((1, lw), _F32)])
    args = (proj, dy, car, cai, ar, ai, b_re, b_im, c_re, c_imn, dvec)
    if rider is None:
        return _call(body, compiler_params=_cparams("parallel", "arbitrary"), **kw)(*args), []
    return _call_riding(body, rider, args=args, **kw)


def _bd_in(b):
    g, p, c = b.shape
    s = g // _S5_SET
    b4 = b.reshape(s, _S5_SET, p, c).transpose(0, 1, 3, 2)
    eye = jnp.eye(_S5_SET, dtype=b.dtype)
    return (b4[:, :, :, None, :] * eye[None, :, None, :, None]).reshape(s, _S5_SET * c, _S5_SET * p)


def _bd_in_grad(d, p, c):
    s = d.shape[0]
    eye = jnp.eye(_S5_SET, dtype=d.dtype)
    d5 = d.reshape(s, _S5_SET, c, _S5_SET, p) * eye[None, :, None, :, None]
    return d5.sum(axis=3).transpose(0, 1, 3, 2).reshape(s * _S5_SET, p, c)


def _bd_out(cm):
    g, c, p = cm.shape
    s = g // _S5_SET
    c4 = cm.reshape(s, _S5_SET, c, p).transpose(0, 1, 3, 2)
    eye = jnp.eye(_S5_SET, dtype=cm.dtype)
    return (c4[:, :, :, None, :] * eye[None, :, None, :, None]).reshape(s, _S5_SET * p, _S5_SET * c)


def _bd_out_grad(d, p, c):
    s = d.shape[0]
    eye = jnp.eye(_S5_SET, dtype=d.dtype)
    d5 = d.reshape(s, _S5_SET, p, _S5_SET, c) * eye[None, :, None, :, None]
    return d5.sum(axis=3).transpose(0, 1, 3, 2).reshape(s * _S5_SET, c, p)


def _s5glu_fwd(y, wglu, bglu):
    t, w5 = y.shape
    tm = _tile(t, 256, 8)

    def fn(yb, wb, bb):
        z1 = _gelu(yb)
        a = _dg(z1, wb, _NN) + bb
        return (z1 * _sigmoid(a),)

    return _rowcall(fn, [y, wglu, bglu], [_rb(tm, w5), _full(wglu.shape), _full((1, w5))],
                    [_sds((t, w5), _BF)], [_rb(tm, w5)], [False], grid=(1, t // tm), name="s5glu_fwd")[0]


def _s5glu_bwd(y, dz, wglu, bglu):
    t, w5 = y.shape
    tm = _tile(t, 256, 8)

    def fn(yb, dzb, wb, bb):
        dzb = dzb.astype(_F32)
        z1, gelu_vjp = jax.vjp(_gelu, yb)
        sig = _sigmoid(_dg(z1, wb, _NN) + bb)
        da = dzb * z1 * sig * (1.0 - sig)
        dz1 = dzb * sig + _dg(da, wb, _NT)
        (dy,) = gelu_vjp(dz1)
        return dy, _dg(z1, da, _TN), jnp.sum(da, axis=0, keepdims=True)

    return _rowcall(fn, [y, dz, wglu, bglu], [_rb(tm, w5), _rb(tm, w5), _full(wglu.shape), _full((1, w5))],
                    [_sds((t, w5), _F32), _sds((w5, w5), _F32), _sds((1, w5), _F32)],
                    [_rb(tm, w5), _full((w5, w5)), _full((1, w5))], [False, True, True],
                    grid=(1, t // tm), name="s5glu_bwd")


_LEVELS = (6, 5, 4, 3, 2, 1)


def _tri_stack(c):
    t = jnp.arange(c, dtype=jnp.int32)[:, None]
    j = jnp.arange(c, dtype=jnp.int32)[None, :]
    low = (j <= t).astype(_F32)
    mats = [low]
    for sh in _LEVELS:
        r = ((t >> sh) << sh) + ((1 << (sh - 1)) - 1)
        mats.append(low - (j <= r).astype(_F32))
    stack = jnp.concatenate(mats, axis=0).astype(_BF)
    return stack, stack.T


def _split_dot(mat, x):
    l = x.shape[1]
    hi = x.astype(_BF)
    lo = (x - hi.astype(_F32)).astype(_BF)
    out = jnp.dot(mat, jnp.concatenate([hi, lo], axis=1), preferred_element_type=_F32)
    return out[:, :l] + out[:, l:]


@jax.custom_vjp
def _decay_sums(lf, tri, tri_t):
    c = lf.shape[0]
    out = _split_dot(tri, lf)
    return tuple(out[k * c:(k + 1) * c] for k in range(len(_LEVELS) + 1))


def _decay_sums_f(lf, tri, tri_t):
    return _decay_sums(lf, tri, tri_t), (tri, tri_t)


def _decay_sums_b(res, gs):
    tri, tri_t = res
    return _split_dot(tri_t, jnp.concatenate(gs, axis=0)), jnp.zeros_like(tri), jnp.zeros_like(tri_t)


_decay_sums.defvjp(_decay_sums_f, _decay_sums_b)


def _hgrn_chunk(qi, fi, vi, gi, st, lb, ng, tri, tri_t):
    heads = range(len(qi))
    c = qi[0].shape[0]
    row = lax.broadcasted_iota(jnp.int32, (c, 1), 0)
    tt = lax.broadcasted_iota(jnp.int32, (c, c), 0)
    ss = lax.broadcasted_iota(jnp.int32, (c, c), 1)
    q = [_silu(qi[h]) for h in heads]
    lf = [jnp.log(lb[h] + (1.0 - lb[h]) * _sigmoid(fi[h])) for h in heads]
    k = [(1.0 - lb[h]) * _sigmoid(-fi[h]) for h in heads]
    sums = [_decay_sums(lf[h], tri, tri_t) for h in heads]
    btot = [jnp.sum(lf[h], axis=0, keepdims=True) for h in heads]
    inter = [_bdot_nt(q[h] * jnp.exp(sums[h][0]), st[h]) for h in heads]
    sc = [jnp.where(tt == ss, jnp.sum(q[h] * k[h], axis=1, keepdims=True), 0.0) for h in heads]
    for li, sh in enumerate(_LEVELS):
        upper = ((row >> (sh - 1)) & 1) == 1
        same = (tt >> sh) == (ss >> sh)
        qm = [jnp.where(upper, q[h] * jnp.exp(jnp.where(upper, sums[h][li + 1], 0.0)), 0.0) for h in heads]
        km = [jnp.where(upper, 0.0, k[h] * jnp.exp(jnp.where(upper, 0.0, -sums[h][li + 1]))) for h in heads]
        prod = [_bdot_nt(qm[h], km[h]) for h in heads]
        sc = [sc[h] + jnp.where(same, prod[h], 0.0) for h in heads]
    o = [inter[h] + _bdot(sc[h], vi[h]) for h in heads]
    st_new = [st[h] * jnp.exp(btot[h]) + _bdot_tn(vi[h], k[h] * jnp.exp(btot[h] - sums[h][0])) for h in heads]
    og = [o[h] * lax.rsqrt(jnp.mean(o[h] * o[h], axis=1, keepdims=True) + _RMS_EPS) * ng[h] * _silu(gi[h])
          for h in heads]
    return tuple(og), tuple(st_new)


def _head_groups(hp):
    g = min(_HGRN_GROUP, hp)
    return [list(range(k, min(k + g, hp))) for k in range(0, hp, g)]


def _hgrn_geom(t, w5, hw):
    hp = _HGRN_HP if (hw // _LANE) % _HGRN_HP == 0 and w5 % (_LANE * _HGRN_HP) == 0 else 1
    seg = min(t, _HGRN_SEG)
    return hp, hp * _LANE, seg, t // seg


def _hgrn_in_specs(seg, wd, w5, hw, tmap):
    nhp = hw // wd
    qb = w5 // wd
    return [pl.BlockSpec((seg, wd), (lambda h, s, k=k: (tmap(s), qb + k * nhp + h))) for k in range(4)]


def _hgrn_fwd(proj, lb, ng, t, w5, hw, rider=None):
    assert _CHUNK == 64
    hp, wd, seg, nseg = _hgrn_geom(t, w5, hw)
    ncs = seg // _CHUNK
    vec = pl.BlockSpec((1, wd), lambda h, s: (0, h))

    tri, tri_t = _tri_stack(_CHUNK)

    def body(q_ref, f_ref, i_ref, g_ref, lb_ref, ng_ref, tri_ref, trit_ref, og_ref, st_ref, s_scr):
        @pl.when(pl.program_id(1) == 0)
        def _():
            s_scr[...] = jnp.zeros_like(s_scr)

        tri_v, trit_v = tri_ref[...], trit_ref[...]

        def step(ci, carry):
            r = pl.ds(pl.multiple_of(ci * _CHUNK, _CHUNK), _CHUNK)
            for grp in _head_groups(hp):
                lns = [slice(hh * _LANE, (hh + 1) * _LANE) for hh in grp]
                for hh in grp:
                    st_ref[hh, ci] = s_scr[hh]
                ogs, sns = _hgrn_chunk(tuple(q_ref[r, ln] for ln in lns), tuple(f_ref[r, ln] for ln in lns),
                                       tuple(i_ref[r, ln] for ln in lns), tuple(g_ref[r, ln] for ln in lns),
                                       tuple(s_scr[hh] for hh in grp), tuple(lb_ref[:, ln] for ln in lns),
                                       tuple(ng_ref[:, ln] for ln in lns), tri_v, trit_v)
                for hh, ln, og, sn in zip(grp, lns, ogs, sns):
                    og_ref[r, ln] = og.astype(_BF)
                    s_scr[hh] = sn
            return carry

        lax.fori_loop(0, ncs, step, 0)

    kw = dict(name="hgrn_fwd", grid=(hw // wd, nseg),
              in_specs=_hgrn_in_specs(seg, wd, w5, hw, lambda s: s) + [
                  vec, vec, pl.BlockSpec(tri.shape, lambda h, s: (0, 0)), pl.BlockSpec(tri_t.shape, lambda h, s: (0, 0))],
              out_specs=[pl.BlockSpec((seg, wd), lambda h, s: (s, h)),
                         pl.BlockSpec((hp, ncs, _LANE, _LANE), lambda h, s: (h, s, 0, 0))],
              out_shape=[_sds((t, hw), _BF), _sds((hw // _LANE, t // _CHUNK, _LANE, _LANE), _F32)],
              scratch_shapes=[pltpu.VMEM((hp, _LANE, _LANE), _F32)])
    args = (proj, proj, proj, proj, lb, ng, tri, tri_t)
    if rider is None:
        return _call(body, compiler_params=_cparams("parallel", "arbitrary"), **kw)(*args), []
    return _call_riding(body, rider, args=args, **kw)


def _hgrn_bwd(proj, lb, ng, st_all, dog, t, w5, hw, rider=None):
    hp, wd, seg, nseg = _hgrn_geom(t, w5, hw)
    ncs = seg // _CHUNK

    def rev(s):
        return nseg - 1 - s

    vec = pl.BlockSpec((1, wd), lambda h, s: (0, h))
    col = pl.BlockSpec((seg, wd), lambda h, s: (rev(s), h))

    tri, tri_t = _tri_stack(_CHUNK)

    def body(q_ref, f_ref, i_ref, g_ref, lb_ref, ng_ref, tri_ref, trit_ref, st_ref, dog_ref,
             dq_ref, df_ref, di_ref, dg_ref, dlb_ref, dng_ref, ds_scr):
        @pl.when(pl.program_id(1) == 0)
        def _():
            ds_scr[...] = jnp.zeros_like(ds_scr)
            dlb_ref[...] = jnp.zeros_like(dlb_ref)
            dng_ref[...] = jnp.zeros_like(dng_ref)

        tri_v, trit_v = tri_ref[...], trit_ref[...]

        def step(kk, carry):
            ci = ncs - 1 - kk
            r = pl.ds(pl.multiple_of(ci * _CHUNK, _CHUNK), _CHUNK)
            for grp in _head_groups(hp):
                lns = [slice(hh * _LANE, (hh + 1) * _LANE) for hh in grp]
                _, vjp = jax.vjp(_hgrn_chunk, tuple(q_ref[r, ln] for ln in lns), tuple(f_ref[r, ln] for ln in lns),
                                 tuple(i_ref[r, ln] for ln in lns), tuple(g_ref[r, ln] for ln in lns),
                                 tuple(st_ref[hh, ci] for hh in grp), tuple(lb_ref[:, ln] for ln in lns),
                                 tuple(ng_ref[:, ln] for ln in lns), tri_v, trit_v)
                dq, df, di, dg, ds, dlb, dng = vjp((tuple(dog_ref[r, ln].astype(_F32) for ln in lns),
                                                    tuple(ds_scr[hh] for hh in grp)))[:7]
                for n_, (hh, ln) in enumerate(zip(grp, lns)):
                    dq_ref[r, ln] = dq[n_].astype(_BF)
                    df_ref[r, ln] = df[n_].astype(_BF)
                    di_ref[r, ln] = di[n_].astype(_BF)
                    dg_ref[r, ln] = dg[n_].astype(_BF)
                    ds_scr[hh] = ds[n_]
                    dlb_ref[:, ln] += dlb[n_]
                    dng_ref[:, ln] += dng[n_]
            return carry

        lax.fori_loop(0, ncs, step, 0)

    kw = dict(name="hgrn_bwd", grid=(hw // wd, nseg),
              in_specs=_hgrn_in_specs(seg, wd, w5, hw, rev) + [
                  vec, vec, pl.BlockSpec(tri.shape, lambda h, s: (0, 0)), pl.BlockSpec(tri_t.shape, lambda h, s: (0, 0)),
                  pl.BlockSpec((hp, ncs, _LANE, _LANE), lambda h, s: (h, rev(s), 0, 0)), col],
              out_specs=[col, col, col, col, vec, vec],
              out_shape=[_sds((t, hw), _BF)] * 4 + [_sds((1, hw), _F32)] * 2,
              scratch_shapes=[pltpu.VMEM((hp, _LANE, _LANE), _F32)])
    args = (proj, proj, proj, proj, lb, ng, tri, tri_t, st_all, dog)
    if rider is None:
        return _call(body, compiler_params=_cparams("parallel", "arbitrary"), **kw)(*args), []
    return _call_riding(body, rider, args=args, **kw)


def _lb_of(logits):
    mx = jnp.max(logits, axis=0, keepdims=True)
    e = jnp.exp(logits - mx)
    sm = e / jnp.sum(e, axis=0, keepdims=True)
    row = lax.broadcasted_iota(jnp.int32, logits.shape, 0)
    return jnp.sum(jnp.where(row == 0, sm, 0.0), axis=0, keepdims=True)


def _lb_prep(logits):
    def body(l_ref, o_ref):
        o_ref[...] = _lb_of(l_ref[...])

    return _call(body, name="lb_prep", out_shape=_sds((1, logits.shape[1]), _F32))(logits)


def _lb_prep_bwd(logits, dlb):
    def body(l_ref, d_ref, o_ref):
        _, vjp = jax.vjp(_lb_of, l_ref[...])
        o_ref[...] = vjp(d_ref[...])[0]

    return _call(body, name="lb_prep_bwd", out_shape=_sds(logits.shape, _F32))(logits, dlb)


def _merge_fwd(proj, ys, yh, t, d, gs_off):
    tm = _tile(t, 256, 8)
    w = _tile(d, 1024)
    nb = d // w

    def fn(gs, gh, a, b):
        return (_sigmoid(gs) * a + _sigmoid(gh) * b,)

    return _rowcall(fn, [proj, proj, ys, yh], [_rb(tm, w, gs_off // w), _rb(tm, w, gs_off // w + nb), _rb(tm, w), _rb(tm, w)],
                    [_sds((t, d), _BF)], [_rb(tm, w)], [False], grid=(nb, t // tm), name="merge_fwd")[0]


def _merge_bwd(proj, ys, yh, dm, t, d, gs_off):
    tm = _tile(t, 256, 8)
    w = _tile(d, 1024)
    nb = d // w

    def fn(gs, gh, a, b, g):
        g = g.astype(_F32)
        s1 = _sigmoid(gs)
        s2 = _sigmoid(gh)
        return g * s1, g * s2, g * a * s1 * (1.0 - s1), g * b * s2 * (1.0 - s2)

    return _rowcall(fn, [proj, proj, ys, yh, dm],
                    [_rb(tm, w, gs_off // w), _rb(tm, w, gs_off // w + nb), _rb(tm, w), _rb(tm, w), _rb(tm, w)],
                    [_sds((t, d), _BF)] * 4, [_rb(tm, w)] * 4, [False] * 4, grid=(nb, t // tm), name="merge_bwd")


_HALO = 16


def _prev_rows(up_prev, is_first):
    p1 = jnp.where(is_first, 0.0, up_prev[_HALO - 1:_HALO, :])
    p2 = jnp.where(is_first, 0.0, up_prev[_HALO - 2:_HALO - 1, :])
    return p1, p2


def _causal_taps(cur, p1, p2):
    row = lax.broadcasted_iota(jnp.int32, cur.shape, 0)
    s1 = jnp.where(row == 0, p1, _shift_down(cur, 1))
    s2 = jnp.where(row == 0, p2, jnp.where(row == 1, p1, _shift_down(cur, 2)))
    return s1, s2


def _pair_perm(nbh, per):
    def perm(n_):
        big = n_ // per
        return (2 * (big % nbh) + big // nbh) * per + n_ % per
    return perm


def _conv_specs(tm, w, nb_half, t):
    r8 = tm // _HALO
    cur_g = pl.BlockSpec((tm, w), lambda j, i: (i, 2 * j))
    cur_v = pl.BlockSpec((tm, w), lambda j, i: (i, 2 * j + 1))
    prev_g = pl.BlockSpec((_HALO, w), lambda j, i: (jnp.maximum(i * r8 - 1, 0), 2 * j))
    prev_v = pl.BlockSpec((_HALO, w), lambda j, i: (jnp.maximum(i * r8 - 1, 0), 2 * j + 1))
    w_g = pl.BlockSpec((3, w), lambda j, i: (0, j))
    w_v = pl.BlockSpec((3, w), lambda j, i: (0, nb_half + j))
    b_g = pl.BlockSpec((1, w), lambda j, i: (0, j))
    b_v = pl.BlockSpec((1, w), lambda j, i: (0, nb_half + j))
    return cur_g, cur_v, prev_g, prev_v, w_g, w_v, b_g, b_v


def _conv_of(cur, prev8, wt, bias, is_first):
    cur, prev8 = cur.astype(_F32), prev8.astype(_F32)
    p1, p2 = _prev_rows(prev8, is_first)
    s1, s2 = _causal_taps(cur, p1, p2)
    return bias + wt[0:1, :] * s2 + wt[1:2, :] * s1 + wt[2:3, :] * cur


def _convact_fwd(up, cw, cb, t, dff):
    tm = _tile(t, 512, _HALO)
    w = _tile(dff, 1408)
    nbh = dff // w
    sp = _conv_specs(tm, w, nbh, t)

    def body(ug, uv, pg, pv, wg, wv, bg, bv, o_ref):
        first = pl.program_id(1) == 0
        gate = _conv_of(ug[...], pg[...], wg[...], bg[...], first)
        val = _conv_of(uv[...], pv[...], wv[...], bv[...], first)
        o_ref[...] = (_silu(gate) * val).astype(_BF)

    return _call(body, name="convact_fwd", grid=(nbh, t // tm), in_specs=list(sp),
                 out_specs=pl.BlockSpec((tm, w), lambda j, i: (i, j)), out_shape=_sds((t, dff), _BF),
                 compiler_params=_cparams("parallel", "arbitrary"))(up, up, up, up, cw, cw, cb, cb)


def _convact_bwd(up, dact, cw, cb, t, dff):
    tm = _tile(t, 256, _HALO)
    w = _tile(dff, 1408)
    nbh = dff // w
    r8 = tm // _HALO
    nt = t // tm
    last8 = t // _HALO - 1

    def triple(off):
        return [pl.BlockSpec((tm, w), lambda j, i: (i, 2 * j + off)),
                pl.BlockSpec((_HALO, w), lambda j, i: (jnp.minimum((i + 1) * r8, last8), 2 * j + off)),
                pl.BlockSpec((_HALO, w), lambda j, i: (jnp.maximum(i * r8 - 1, 0), 2 * j + off))]

    def body(ug, ugn, ugp, uv, uvn, uvp, wg_ref, wv_ref, bg_ref, bv_ref, da_ref, dan_ref, du_ref, dw_ref, db_ref):
        i = pl.program_id(1)
        first = i == 0
        is_last = i == nt - 1
        gate = _conv_of(jnp.concatenate([ug[...], ugn[...]], axis=0), ugp[...], wg_ref[...], bg_ref[...], first)
        val = _conv_of(jnp.concatenate([uv[...], uvn[...]], axis=0), uvp[...], wv_ref[...], bv_ref[...], first)
        da = jnp.concatenate([da_ref[...], dan_ref[...]], axis=0).astype(_F32)
        row = lax.broadcasted_iota(jnp.int32, da.shape, 0)
        da = jnp.where(jnp.logical_and(row >= tm, is_last), 0.0, da)
        sg = _sigmoid(gate)
        halves = ((da * val * sg * (1.0 + gate * (1.0 - sg)), wg_ref, ug, ugp),
                  (da * gate * sg, wv_ref, uv, uvp))

        @pl.when(first)
        def _():
            dw_ref[...] = jnp.zeros_like(dw_ref)
            db_ref[...] = jnp.zeros_like(db_ref)

        for h, (dc, w_ref, u_ref, p_ref) in enumerate(halves):
            ln = slice(h * w, (h + 1) * w)
            wt = w_ref[...]
            du = wt[2:3, :] * dc + wt[1:2, :] * _shift_up(dc, 1) + wt[0:1, :] * _shift_up(dc, 2)
            du_ref[:, ln] = du[0:tm, :].astype(_BF)
            dcm = dc[0:tm, :]
            cur = u_ref[...].astype(_F32)
            p1, p2 = _prev_rows(p_ref[...].astype(_F32), first)
            s1, s2 = _causal_taps(cur, p1, p2)
            dw_ref[0:1, ln] += jnp.sum(dcm * s2, axis=0, keepdims=True)
            dw_ref[1:2, ln] += jnp.sum(dcm * s1, axis=0, keepdims=True)
            dw_ref[2:3, ln] += jnp.sum(dcm * cur, axis=0, keepdims=True)
            db_ref[:, ln] += jnp.sum(dcm, axis=0, keepdims=True)

    in_specs = (triple(0) + triple(1)
                + [pl.BlockSpec((3, w), lambda j, i: (0, j)), pl.BlockSpec((3, w), lambda j, i: (0, nbh + j)),
                   pl.BlockSpec((1, w), lambda j, i: (0, j)), pl.BlockSpec((1, w), lambda j, i: (0, nbh + j)),
                   pl.BlockSpec((tm, w), lambda j, i: (i, j)),
                   pl.BlockSpec((_HALO, w), lambda j, i: (jnp.minimum((i + 1) * r8, last8), j))])
    dup, dw_p, db_p = _call(
        body, name="convact_bwd", grid=(nbh, nt), in_specs=in_specs,
        out_specs=[pl.BlockSpec((tm, 2 * w), lambda j, i: (i, j)), pl.BlockSpec((3, 2 * w), lambda j, i: (0, j)),
                   pl.BlockSpec((1, 2 * w), lambda j, i: (0, j))],
        out_shape=[_sds((t, 2 * dff), _BF), _sds((3, 2 * dff), _F32), _sds((1, 2 * dff), _F32)],
        compiler_params=_cparams("parallel", "arbitrary"))(up, up, up, up, up, up, cw, cw, cb, cb, dact, dact)

    def natural(v):
        k = v.shape[0]
        return v.reshape(k, nbh, 2, w).transpose(0, 2, 1, 3).reshape(k, 2 * dff)

    return dup, natural(dw_p), natural(db_p)


def _me():
    return lax.axis_index("x"), lax.axis_index("y"), lax.axis_index("c")


def _other_chips(x, y):
    return [(1 - x, y), (x, 1 - y), (1 - x, 1 - y)]


def _rcopy(src, dst, ssem, rsem, dev):
    return pltpu.make_async_remote_copy(src_ref=src, dst_ref=dst, send_sem=ssem, recv_sem=rsem,
                                        device_id=dev, device_id_type=_MESH)


def _cast_into_slot(w, sel):
    r, c = w.shape
    tm = _tile(r, 256, 16)

    def body(sel_ref, w_ref, o_ref):
        o_ref[...] = w_ref[...].astype(_BF)

    gs = pltpu.PrefetchScalarGridSpec(
        num_scalar_prefetch=1, grid=(r // tm,),
        in_specs=[pl.BlockSpec((tm, c), lambda i, s: (i, 0))],
        out_specs=pl.BlockSpec((None, tm, c), lambda i, s: (s[0], i, 0)))
    return _call(body, name="cast_into_slot", grid_spec=gs, out_shape=_sds((_NCHIP, r, c), _BF),
                 compiler_params=_cparams("parallel"))(sel, w)


class _Plan:
    def __init__(self, ins, outs, aliases, sems, start, finish):
        self.ins, self.outs, self.aliases, self.sems, self.start, self.finish = ins, outs, aliases, sems, start, finish


def _run_plan(plan, name):
    ni, no = len(plan.ins), len(plan.outs)

    def body(*refs):
        rin, rout, sems = refs[:ni], refs[ni:ni + no], refs[ni + no:]
        plan.start(rin, rout, sems)
        plan.finish(rin, rout, sems)

    return _call(body, name=name, in_specs=[_ANY] * ni, out_specs=[_ANY] * no, out_shape=list(plan.outs),
                 input_output_aliases=dict(plan.aliases), scratch_shapes=list(plan.sems))(*plan.ins)


def _call_riding(body, rider, *, name, grid, in_specs, out_specs, out_shape, scratch_shapes, args):
    n_in, n_out, n_scr = len(in_specs), len(out_specs), len(scratch_shapes)
    n_rin, n_rout = len(rider.ins), len(rider.outs)

    def wrapped(*refs):
        ins, rin = refs[:n_in], refs[n_in:n_in + n_rin]
        o0 = n_in + n_rin
        outs, rout = refs[o0:o0 + n_out], refs[o0 + n_out:o0 + n_out + n_rout]
        s0 = o0 + n_out + n_rout
        scratch, sems = refs[s0:s0 + n_scr], refs[s0 + n_scr:]
        first = functools.reduce(jnp.logical_and, [pl.program_id(k) == 0 for k in range(len(grid))])
        last = functools.reduce(jnp.logical_and, [pl.program_id(k) == grid[k] - 1 for k in range(len(grid))])

        @pl.when(first)
        def _():
            rider.start(rin, rout, sems)

        body(*ins, *outs, *scratch)

        @pl.when(last)
        def _():
            rider.finish(rin, rout, sems)

    res = _call(wrapped, name=name, grid=grid, in_specs=list(in_specs) + [_ANY] * n_rin,
                out_specs=list(out_specs) + [_ANY] * n_rout, out_shape=list(out_shape) + list(rider.outs),
                input_output_aliases={n_in + k: n_out + v for k, v in rider.aliases.items()},
                scratch_shapes=list(scratch_shapes) + list(rider.sems),
                compiler_params=_cparams(*(["arbitrary"] * len(grid))))(*args, *rider.ins)
    return list(res[:n_out]), list(res[n_out:])


def _gather_plan(bufs, direct, rel=(0, 1, 2)):
    n, nd = len(bufs), len(direct)

    def where():
        x, y, c = _me()
        return c, 2 * x + y, _other_chips(x, y), (x, y, 1 - c)

    def picked(chips):
        return [(j, chips[j]) for j in rel]

    def piece(outs, a, chip, h):
        r2 = bufs[a].shape[1] // 2
        return outs[a].at[chip, pl.ds(h * r2, r2)]

    def send(outs, sems, a, j, me, c, chip):
        return _rcopy(piece(outs, a, me, c), piece(outs, a, me, c), sems[0].at[3 * a + j], sems[1].at[3 * a + j],
                      (chip[0], chip[1], c))

    def forward(outs, sems, a, j, pc, c, sib):
        return _rcopy(piece(outs, a, pc, c), piece(outs, a, pc, c), sems[2].at[3 * a + j], sems[3].at[3 * a + j], sib)

    def dsend(dins, douts, sems, a, j, me, c, chip):
        return _rcopy(dins[a], douts[a].at[me], sems[4].at[3 * a + j], sems[5].at[3 * a + j], (chip[0], chip[1], c))

    def start(rin, rout, sems):
        outs, dins, douts = rout[:n], rin[n:], rout[n:]
        c, me, chips, _ = where()
        for a in range(n):
            for j, chip in picked(chips):
                send(outs, sems, a, j, me, c, chip).start()
        for a in range(nd):
            pltpu.make_async_copy(dins[a], douts[a].at[me], sems[6].at[a]).start()
            for j, chip in enumerate(chips):
                dsend(dins, douts, sems, a, j, me, c, chip).start()

    def finish(rin, rout, sems):
        outs, dins, douts = rout[:n], rin[n:], rout[n:]
        c, me, chips, sib = where()
        for a in range(n):
            for j, (cx, cy) in picked(chips):
                pc = 2 * cx + cy
                _rcopy(piece(outs, a, me, c), piece(outs, a, pc, c), sems[0].at[3 * a + j], sems[1].at[3 * a + j],
                       (cx, cy, c)).wait_recv()
                forward(outs, sems, a, j, pc, c, sib).start()
        for a in range(n):
            for j, (cx, cy) in picked(chips):
                pc = 2 * cx + cy
                _rcopy(piece(outs, a, pc, 1 - c), piece(outs, a, pc, 1 - c), sems[2].at[3 * a + j],
                       sems[3].at[3 * a + j], sib).wait_recv()
        for a in range(nd):
            for j, (cx, cy) in enumerate(chips):
                _rcopy(dins[a], douts[a].at[2 * cx + cy], sems[4].at[3 * a + j], sems[5].at[3 * a + j],
                       (cx, cy, c)).wait_recv()
        for a in range(n):
            for j, (cx, cy) in picked(chips):
                send(outs, sems, a, j, me, c, (cx, cy)).wait_send()
                forward(outs, sems, a, j, 2 * cx + cy, c, sib).wait_send()
        for a in range(nd):
            pltpu.make_async_copy(dins[a], douts[a].at[me], sems[6].at[a]).wait()
            for j, chip in enumerate(chips):
                dsend(dins, douts, sems, a, j, me, c, chip).wait_send()

    dma = pltpu.SemaphoreType.DMA
    return _Plan(list(bufs) + list(direct),
                 [_sds(b.shape, b.dtype) for b in bufs] + [_sds((_NCHIP,) + s.shape, s.dtype) for s in direct],
                 {a: a for a in range(n)},
                 [dma((3 * max(n, 1),)), dma((3 * max(n, 1),)), dma((3 * max(n, 1),)), dma((3 * max(n, 1),)),
                  dma((3 * max(nd, 1),)), dma((3 * max(nd, 1),)), dma((max(nd, 1),))], start, finish)


def _mm_shard_order(a, mid_plan, end_plan, sel, *, tm, tn, name):
    b = mid_plan.ins[0]
    s, k, ns = b.shape
    m = a.shape[0]
    npb = ns // tn
    nj, ni = s * npb, m // tm
    n_in = (len(mid_plan.ins), len(end_plan.ins))
    n_out = (len(mid_plan.outs), len(end_plan.outs))
    n_sem = (len(mid_plan.sems), len(end_plan.sems))

    def shard_of(jj, q):
        return jnp.bitwise_xor(q[0], jj // npb)

    def body(q_ref, a_ref, *rest):
        rin = (rest[:n_in[0]], rest[n_in[0]:sum(n_in)])
        o_ref = rest[sum(n_in)]
        o0 = sum(n_in) + 1
        rout = (rest[o0:o0 + n_out[0]], rest[o0 + n_out[0]:o0 + sum(n_out)])
        vbuf, bsem = rest[o0 + sum(n_out)], rest[o0 + sum(n_out) + 1]
        s0 = o0 + sum(n_out) + 2
        sems = (rest[s0:s0 + n_sem[0]], rest[s0 + n_sem[0]:s0 + sum(n_sem)])
        b_ref = rout[0][0]
        j, i = pl.program_id(0), pl.program_id(1)

        def tile_copy(jj, slot):
            return pltpu.make_async_copy(b_ref.at[shard_of(jj, q_ref), :, pl.ds((jj % npb) * tn, tn)],
                                         vbuf.at[slot], bsem.at[slot])

        @pl.when(jnp.logical_and(j == 0, i == 0))
        def _():
            mid_plan.start(rin[0], rout[0], sems[0])
            end_plan.start(rin[1], rout[1], sems[1])
            tile_copy(0, 0).start()

        @pl.when(i == 0)
        def _():
            @pl.when(j == (s - 1) * npb - 1)
            def _():
                mid_plan.finish(rin[0], rout[0], sems[0])

            @pl.when(j + 1 < nj)
            def _():
                tile_copy(j + 1, (j + 1) % 2).start()

            tile_copy(j, j % 2).wait()

        o_ref[...] = _dg(a_ref[...], vbuf[j % 2], _NN)

        @pl.when(jnp.logical_and(j == nj - 1, i == ni - 1))
        def _():
            end_plan.finish(rin[1], rout[1], sems[1])

    assert s == _NCHIP and nj >= 2
    n_rin, n_rout = sum(n_in), sum(n_out)
    gs = pltpu.PrefetchScalarGridSpec(
        num_scalar_prefetch=1, grid=(nj, ni),
        in_specs=[pl.BlockSpec((tm, k), lambda j, i, q: (i, 0))] + [_ANY] * n_rin,
        out_specs=[pl.BlockSpec((tm, tn), lambda j, i, q: (i, shard_of(j, q) * npb + j % npb))] + [_ANY] * n_rout,
        scratch_shapes=[pltpu.VMEM((2, k, tn), _BF), pltpu.SemaphoreType.DMA((2,))] + list(mid_plan.sems) + list(end_plan.sems))
    aliases = {2 + kk: 1 + v for kk, v in mid_plan.aliases.items()}
    aliases.update({2 + n_in[0] + kk: 1 + n_out[0] + v for kk, v in end_plan.aliases.items()})
    res = _call(body, name=name, grid_spec=gs, out_shape=[_sds((m, s * ns), _F32)] + list(mid_plan.outs) + list(end_plan.outs),
                input_output_aliases=aliases,
                compiler_params=_cparams("arbitrary", "arbitrary"))(sel, a, *mid_plan.ins, *end_plan.ins)
    return res[0], list(res[1:1 + n_out[0]]), list(res[1 + n_out[0]:])


def _swap_plan(grads):
    n = len(grads)

    def copy(rin, rout, sems, a):
        x, y, c = _me()
        return _rcopy(rin[a].at[1 - c], rout[a], sems[0].at[a], sems[1].at[a], (x, y, 1 - c))

    def start(rin, rout, sems):
        for a in range(n):
            copy(rin, rout, sems, a).start()

    def finish(rin, rout, sems):
        for a in range(n):
            copy(rin, rout, sems, a).wait_recv()
        for a in range(n):
            copy(rin, rout, sems, a).wait_send()

    dma = pltpu.SemaphoreType.DMA
    return _Plan(list(grads), [_sds(g.shape[1:], g.dtype) for g in grads], {}, [dma((n,)), dma((n,))], start, finish)


def _swap_halves(grads, name):
    n = len(grads)

    def body(*refs):
        ins, outs = refs[:n], refs[n:2 * n]
        ssem, rsem = refs[2 * n:]
        x, y, c = _me()
        sib = (x, y, 1 - c)
        cps = []
        for a in range(n):
            cp = _rcopy(ins[a].at[1 - c], outs[a], ssem.at[a], rsem.at[a], sib)
            cp.start()
            cps.append(cp)
        for cp in cps:
            cp.wait_recv()
        for cp in cps:
            cp.wait_send()

    dma = pltpu.SemaphoreType.DMA
    return _call(body, name=name, in_specs=[_ANY] * n, out_specs=[_ANY] * n,
                 out_shape=[_sds(g.shape[1:], g.dtype) for g in grads],
                 scratch_shapes=[dma((n,)), dma((n,))])(*grads)


def _add_pairs(grads, theirs, sel):
    _, s, r, c2 = grads.shape
    a3 = grads.reshape(2, s * r, c2)
    b2 = theirs.reshape(s * r, c2)
    tm = _tile(s * r, 512, 16)

    def body(sel_ref, a_ref, b_ref, o_ref):
        o_ref[...] = (a_ref[...].astype(_F32) + b_ref[...].astype(_F32)).astype(_BF)

    gs = pltpu.PrefetchScalarGridSpec(
        num_scalar_prefetch=1, grid=(s * r // tm,),
        in_specs=[pl.BlockSpec((None, tm, c2), lambda i, q: (q[1], i, 0)), pl.BlockSpec((tm, c2), lambda i, q: (i, 0))],
        out_specs=pl.BlockSpec((tm, c2), lambda i, q: (i, 0)))
    out = _call(body, name="chip_sum", grid_spec=gs, out_shape=_sds((s * r, c2), _BF),
                compiler_params=_cparams("parallel"))(sel, a3, b2)
    return out.reshape(s, r, c2)


def _exchange_plan(sums, small):
    n = len(sums)
    has_small = small is not None

    def where():
        x, y, c = _me()
        peers = [(1 - x if k & 4 else x, 1 - y if k & 2 else y, 1 - c if k & 1 else c) for k in range(1, _NDEV)]
        return c, 4 * x + 2 * y + c, _other_chips(x, y), peers

    def send(rin, rout, sems, a, j, c, chip):
        return _rcopy(rin[a].at[2 * chip[0] + chip[1]], rout[a].at[j], sems[0].at[3 * a + j], sems[1].at[3 * a + j],
                      (chip[0], chip[1], c))

    def small_send(rin, rout, sems, k, dev, peer):
        return _rcopy(rin[n], rout[n].at[dev], sems[2].at[k], sems[3].at[k], peer)

    def start(rin, rout, sems):
        c, dev, chips, peers = where()
        for a in range(n):
            for j, chip in enumerate(chips):
                send(rin, rout, sems, a, j, c, chip).start()
        if has_small:
            pltpu.make_async_copy(rin[n], rout[n].at[dev], sems[4].at[0]).start()
            for k, peer in enumerate(peers):
                small_send(rin, rout, sems, k, dev, peer).start()

    def finish(rin, rout, sems):
        c, dev, chips, peers = where()
        for a in range(n):
            for j, chip in enumerate(chips):
                send(rin, rout, sems, a, j, c, chip).wait_recv()
        if has_small:
            for k, (px, py, pc_) in enumerate(peers):
                _rcopy(rin[n], rout[n].at[4 * px + 2 * py + pc_], sems[2].at[k], sems[3].at[k], (px, py, pc_)).wait_recv()
        for a in range(n):
            for j, chip in enumerate(chips):
                send(rin, rout, sems, a, j, c, chip).wait_send()
        if has_small:
            pltpu.make_async_copy(rin[n], rout[n].at[dev], sems[4].at[0]).wait()
            for k, peer in enumerate(peers):
                small_send(rin, rout, sems, k, dev, peer).wait_send()

    dma = pltpu.SemaphoreType.DMA
    outs = [_sds((3,) + s.shape[1:], s.dtype) for s in sums]
    if has_small:
        outs.append(_sds((_NDEV,) + small.shape, small.dtype))
    return _Plan(list(sums) + ([small] if has_small else []), outs, {},
                 [dma((3 * max(n, 1),)), dma((3 * max(n, 1),)), dma((_NDEV - 1,)), dma((_NDEV - 1,)), dma((1,))],
                 start, finish)


def _shard_sum(sums, recv, sel):
    s, r, c2 = sums.shape
    tm = _tile(r, 256, 16)

    def body(sel_ref, own_ref, rc_ref, o_ref):
        rc = rc_ref[...]
        o_ref[...] = ((own_ref[...].astype(_F32) + rc[0].astype(_F32)) + rc[1].astype(_F32)) + rc[2].astype(_F32)

    gs = pltpu.PrefetchScalarGridSpec(
        num_scalar_prefetch=1, grid=(r // tm,),
        in_specs=[pl.BlockSpec((None, tm, c2), lambda i, q: (q[0], i, 0)),
                  pl.BlockSpec((3, tm, c2), lambda i, q: (0, i, 0))],
        out_specs=pl.BlockSpec((None, tm, c2), lambda i, q: (q[1], i, 0)))
    return _call(body, name="shard_sum", grid_spec=gs, out_shape=_sds((2, r, c2), _F32),
                 compiler_params=_cparams("parallel"))(sel, sums, recv)


def _sum_slots(stack, name):
    k, r, c = stack.shape
    tm = _tile(r, 256, 16 if stack.dtype == _BF else 8)

    def fn(v):
        out = v[0].astype(_F32)
        for i in range(1, k):
            out = out + v[i].astype(_F32)
        return (out,)

    return _rowcall(fn, [stack], [pl.BlockSpec((k, tm, c), lambda j, i: (0, i, 0))], [_sds((r, c), _F32)],
                    [_rb(tm, c)], [False], grid=(1, r // tm), name=name)[0]


def _share_halves(bufs):
    n = len(bufs)

    def body(*refs):
        outs = refs[n:2 * n]
        ssem, rsem = refs[2 * n:]
        x, y, c = _me()
        sib = (x, y, 1 - c)
        cps = []
        for a in range(n):
            cp = _rcopy(outs[a].at[c], outs[a].at[c], ssem.at[a], rsem.at[a], sib)
            cp.start()
            cps.append(cp)
        for a in range(n):
            _rcopy(outs[a].at[c], outs[a].at[1 - c], ssem.at[a], rsem.at[a], sib).wait_recv()
        for cp in cps:
            cp.wait_send()

    dma = pltpu.SemaphoreType.DMA
    return _call(body, name="share_halves", in_specs=[_ANY] * n, out_specs=[_ANY] * n,
                 out_shape=[_sds(b.shape, b.dtype) for b in bufs], input_output_aliases={a: a for a in range(n)},
                 scratch_shapes=[dma((n,)), dma((n,))])(*bufs)


def _adamw_math(w, g, m, v):
    m = _B1 * m + (1.0 - _B1) * g
    v = _B2 * v + (1.0 - _B2) * jnp.square(g)
    m_hat = m / (1.0 - _B1 ** _STEP)
    v_hat = v / (1.0 - _B2 ** _STEP)
    delta = -_LR * (m_hat / (jnp.sqrt(v_hat) + _ADAM_EPS) + _WD * w)
    return delta, m, v


def _adamw_shard(w, g2, m, v, name):
    r, c = w.shape
    c2 = c // 2
    tm = _tile(r, 256, 8)
    blk = pl.BlockSpec((tm, c2), lambda h, i: (i, h))

    def fn(wb, gb, mb, vb):
        return (gb,) + _adamw_math(wb, gb, mb, vb)

    return _rowcall(fn, [w, g2, m, v], [blk, pl.BlockSpec((None, tm, c2), lambda h, i: (h, i, 0)), blk, blk],
                    [_sds((r, c), _F32)] * 4, [blk] * 4, [False] * 4, grid=(2, r // tm), name=name)


def _adamw_whole(w, g, m, v, name):
    r, c = w.shape
    blk = _full((r, c))
    return _rowcall(lambda *a: _adamw_math(*a), [w, g, m, v], [blk] * 4, [_sds((r, c), _F32)] * 3, [blk] * 3,
                    [False] * 3, grid=(1, 1), name=name)


def _adamw_many(ws, g_pack, ms, vs):
    k = len(ws)
    views, offs, off = [], [], 0
    for w in ws:
        n = w.size
        views.append((n // _LANE, _LANE) if n % _LANE == 0 else (1, n))
        offs.append(off)
        off += (n + (-n) % (8 * _LANE)) // _LANE

    def body(*refs):
        g_ref, w_refs, m_refs, v_refs = refs[0], refs[1:1 + k], refs[1 + k:1 + 2 * k], refs[1 + 2 * k:1 + 3 * k]
        outs = refs[1 + 3 * k:]
        for i in range(k):
            r, c = views[i]
            g = g_ref[offs[i]:offs[i] + r, 0:c]
            res = (g,) + _adamw_math(w_refs[i][...], g, m_refs[i][...], v_refs[i][...])
            for o_ref, val in zip(outs[4 * i:4 * i + 4], res):
                o_ref[...] = val

    args = [g_pack] + [a.reshape(views[i]) for grp in (ws, ms, vs) for i, a in enumerate(grp)]
    res = _call(body, name="adamw_small", out_shape=[_sds(views[i], _F32) for i in range(k) for _ in range(4)])(*args)
    return [[res[4 * i + j].reshape(ws[i].shape) for j in range(4)] for i in range(k)]


def _pack(arrs):
    parts = []
    for a in arrs:
        f = a.reshape(-1).astype(_F32)
        pad = (-f.shape[0]) % (8 * _LANE)
        if pad:
            f = jnp.concatenate([f, jnp.zeros((pad,), _F32)])
        parts.append(f)
    return jnp.concatenate(parts).reshape(-1, _LANE)


def kernel(x, ln_mix_g, w_in, s5_a_re, s5_a_im, s5_log_dt, s5_b_re, s5_b_im, s5_c_re, s5_c_im, s5_d, s5_w_glu, s5_b_glu, w_proj_s5, hgrn_lb_logits, hgrn_norm_g, w_proj_hgrn, w_out, ln_ffn_g, w_up, conv_w, conv_b, w_down, ln_final_g, loss_target, m_ln_mix_g, m_w_in, m_s5_a_re, m_s5_a_im, m_s5_log_dt, m_s5_b_re, m_s5_b_im, m_s5_c_re, m_s5_c_im, m_s5_d, m_s5_w_glu, m_s5_b_glu, m_w_proj_s5, m_hgrn_lb_logits, m_hgrn_norm_g, m_w_proj_hgrn, m_w_out, m_ln_ffn_g, m_w_up, m_conv_w, m_conv_b, m_w_down, m_ln_final_g, v_ln_mix_g, v_w_in, v_s5_a_re, v_s5_a_im, v_s5_log_dt, v_s5_b_re, v_s5_b_im, v_s5_c_re, v_s5_c_im, v_s5_d, v_s5_w_glu, v_s5_b_glu, v_w_proj_s5, v_hgrn_lb_logits, v_hgrn_norm_g, v_w_proj_hgrn, v_w_out, v_ln_ffn_g, v_w_up, v_conv_w, v_conv_b, v_w_down, v_ln_final_g):
    assert x.shape[0] == 1 and w_in.shape[0] == 1, "one example per device, one layer"
    t, d = x.shape[1], x.shape[2]
    w5 = s5_w_glu.shape[2]
    hw = hgrn_norm_g.shape[1]
    ng_, np_, gc = s5_b_re.shape[1], s5_b_re.shape[2], s5_b_re.shape[3]
    dff = w_down.shape[1] * _NCHIP
    assert gc * _S5_SET == _LANE and ng_ * gc == w5 and hw % _LANE == 0
    gs_off = w5 + 4 * hw
    ci = lax.axis_index("c")
    xt = x.reshape(t, d)
    tgt = loss_target.reshape(t, d)

    big_names = ["w_in", "s5_w_glu", "w_proj_s5", "w_proj_hgrn", "w_out", "w_up", "w_down"]
    big_w = dict(w_in=w_in[0], s5_w_glu=s5_w_glu[0], w_proj_s5=w_proj_s5[0], w_proj_hgrn=w_proj_hgrn[0],
                 w_out=w_out[0], w_up=w_up[0], w_down=w_down[0])
    chip = 2 * lax.axis_index("x") + lax.axis_index("y")
    sel_chip = jnp.stack([chip, ci]).astype(jnp.int32)
    slots = {k: _cast_into_slot(big_w[k], sel_chip) for k in big_names}
    g_in_part, g_cw = _run_plan(_gather_plan([slots["w_in"]], [conv_w[0]], rel=(0, 1)), "gather_w_in")
    cw = g_cw.transpose(1, 0, 2).reshape(3, 2 * dff)
    cb = conv_b

    tm_big = _tile(t, 1024, 8)

    h1 = _rms_fwd(xt, ln_mix_g, "rms1_fwd")
    nin_s = g_in_part.shape[2]
    proj, (g_in,), (g_glu, g_ps5, g_ph, g_out) = _mm_shard_order(
        h1, _gather_plan([g_in_part], [], rel=(2,)),
        _gather_plan([slots[k] for k in ("s5_w_glu", "w_proj_s5", "w_proj_hgrn", "w_out")], []),
        sel_chip, tm=_tile(t, 512, 8), tn=nin_s, name="mm_proj")
    wglu = g_glu.reshape(w5, w5)
    wout = g_out.reshape(d, d)

    abar_r, abar_i, coef_r, coef_i = _s5_prep(s5_a_re[0], s5_a_im[0], s5_log_dt.reshape(ng_, 1))
    lanes = ng_ * np_

    def fold_coef(cr_, ci_, bre, bim):
        return cr_[..., None] * bre - ci_[..., None] * bim, cr_[..., None] * bim + ci_[..., None] * bre

    (bf_re, bf_im), fold_vjp = jax.vjp(fold_coef, coef_r, coef_i, s5_b_re[0], s5_b_im[0])
    par = (abar_r.reshape(1, lanes), abar_i.reshape(1, lanes),
           _bd_in(bf_re), _bd_in(bf_im), _bd_out(s5_c_re[0]), -_bd_out(s5_c_im[0]), s5_d.reshape(1, w5))
    (y_s5, car, cai), (g_up_part,) = _s5_fwd(proj, par, t, w5, rider=_gather_plan([slots["w_up"]], [], rel=(0, 1)))
    z = _s5glu_fwd(y_s5, wglu, s5_b_glu)
    ys = _mm(z, g_ps5, "nn", _F32, tm=tm_big, tn=g_ps5.shape[2], tk=w5, name="mm_proj_s5")

    lb = _lb_prep(hgrn_lb_logits)
    (og, st_all), (g_up,) = _hgrn_fwd(proj, lb, hgrn_norm_g, t, w5, hw, rider=_gather_plan([g_up_part], [], rel=(2,)))
    yh = _mm(og, g_ph, "nn", _F32, tm=tm_big, tn=g_ph.shape[2], tk=hw, name="mm_proj_hgrn")

    merged = _merge_fwd(proj, ys, yh, t, d, gs_off)
    x2 = _mm(merged, wout, "nn", _F32, tm=tm_big, tn=_tile(d, 1024), tk=d, res=xt, name="mm_out")
    h2 = _rms_fwd(x2, ln_ffn_g, "rms2_fwd")
    up_s = g_up.shape[2]
    w_conv = _tile(dff, 1408)
    nbh = dff // w_conv
    tn_up = _tile(up_s, 1408)
    tk_dh2 = _tile(up_s, w_conv)
    tn_gwup = _tile(up_s // 2, 1408)
    assert w_conv % tn_up == 0 and w_conv % tk_dh2 == 0 and w_conv % tn_gwup == 0
    up, (g_down,) = _mm(h2, g_up, "nn", _BF, tm=tm_big, tn=tn_up, tk=d, name="mm_up",
                        colperm=_pair_perm(nbh, w_conv // tn_up), rider=_gather_plan([slots["w_down"]], []))
    wdown = g_down.reshape(dff, d)
    act = _convact_fwd(up, cw, cb, t, dff)
    x3 = _mm(act, wdown, "nn", _F32, tm=tm_big, tn=_tile(d, 1024), tk=None, res=x2, name="mm_down")
    loss_part, dx3, dx3b, d_gfin = _loss_head(x3, tgt, ln_final_g.reshape(1, d))

    dact = _mm(dx3b, wdown, "nt", _BF, tm=tm_big, tn=_tile(dff, 1408), tk=d, name="mm_dact")
    r_down = dff // _NCHIP
    gw_down = _mm(act, dx3b, "tn", _BF, tm=_tile(r_down, 1408), tn=_tile(d // 2, 1024), tk=None, halves="rows",
                  name="mm_gw_down")
    def chip_sums(grads, name):
        theirs = _swap_halves(grads, name)
        return [_add_pairs(g, th, sel_chip) for g, th in zip(grads, theirs)]

    (s_down,) = chip_sums([gw_down], "swap_halves_d")
    dup, d_cw, d_cb = _convact_bwd(up, dact, cw, cb, t, dff)
    dh2, (r_down,) = _mm(dup, g_up, "nt", _BF, tm=tm_big, tn=_tile(d, 1024), tk=tk_dh2, name="mm_dh2",
                         colperm=_pair_perm(nbh, w_conv // tk_dh2), rider=_exchange_plan([s_down], None))
    gw_up = _mm(h2, dup, "tn", _BF, tm=_tile(d, 1024), tn=tn_gwup, tk=None, halves="cols", name="mm_gw_up",
                colperm=_pair_perm(nbh, w_conv // tn_gwup))
    dx2, dx2b, d_gffn = _rms_bwd(x2, ln_ffn_g, dh2, dx3, "rms2_bwd")
    dmerged = _mm(dx2b, wout, "nt", _BF, tm=tm_big, tn=_tile(d, 1024), tk=d, name="mm_dmerged")
    gw_out = _mm(merged, dx2b, "tn", _BF, tm=_tile(d // _NCHIP, 1024), tn=_tile(d // 2, 1024), tk=None, halves="rows",
                 name="mm_gw_out")
    dys, dyh, dgs, dgh = _merge_bwd(proj, ys, yh, dmerged, t, d, gs_off)
    ps_s = g_ps5.shape[2]
    dz = _mm(dys, g_ps5, "nt", _BF, tm=tm_big, tn=_tile(w5, 1024), tk=ps_s, name="mm_dz")
    gw_ps5 = _mm(z, dys, "tn", _BF, tm=_tile(w5, 1024), tn=ps_s // 2, tk=None, halves="cols", name="mm_gw_ps5")
    dog = _mm(dyh, g_ph, "nt", _BF, tm=tm_big, tn=_tile(hw, 1024), tk=ps_s, name="mm_dog")
    gw_ph = _mm(og, dyh, "tn", _BF, tm=_tile(hw, 1024), tn=ps_s // 2, tk=None, halves="cols", name="mm_gw_ph")
    dy_s5, gw_glu_full, d_bglu = _s5glu_bwd(y_s5, dz, wglu, s5_b_glu)
    r_glu = w5 // _NCHIP
    gw_glu = gw_glu_full.astype(_BF).reshape(_NCHIP, r_glu, 2, w5 // 2).transpose(2, 0, 1, 3)

    early = [gw_glu, gw_ps5, gw_ph, gw_out, gw_up]
    (dq, df, di, dg, d_lb, d_ng), theirs = _hgrn_bwd(proj, lb, hgrn_norm_g, st_all, dog, t, w5, hw,
                                                       rider=_swap_plan(early))
    s_glu, s_ps5, s_ph, s_out, s_up = [_add_pairs(g, th, sel_chip) for g, th in zip(early, theirs)]
    s5g, (r_glu_, r_ps5, r_ph, r_out, r_up) = _s5_bwd(
        proj, dy_s5, car, cai, par, t, w5, rider=_exchange_plan([s_glu, s_ps5, s_ph, s_out, s_up], None))
    du = s5g[0]
    dproj = jnp.concatenate([du, dq, df, di, dg, dgs, dgh], axis=1)

    d_coef_r, d_coef_i, d_bre, d_bim = fold_vjp((_bd_in_grad(s5g[3], np_, gc), _bd_in_grad(s5g[4], np_, gc)))
    d_are, d_aim, d_ldt = _s5_prep_bwd(s5_a_re[0], s5_a_im[0], s5_log_dt.reshape(ng_, 1),
                                       [s5g[1].reshape(ng_, np_), s5g[2].reshape(ng_, np_), d_coef_r, d_coef_i])
    d_cre = _bd_out_grad(s5g[5], np_, gc)
    d_cim = -_bd_out_grad(s5g[6], np_, gc)
    d_logits = _lb_prep_bwd(hgrn_lb_logits, d_lb)

    small_names = ["s5_a_re", "s5_a_im", "s5_log_dt", "s5_b_re", "s5_b_im", "s5_c_re", "s5_c_im", "s5_d",
                   "s5_b_glu", "hgrn_lb_logits", "hgrn_norm_g", "ln_ffn_g", "conv_b", "ln_final_g", "ln_mix_g"]
    small_w = dict(ln_mix_g=ln_mix_g, s5_a_re=s5_a_re, s5_a_im=s5_a_im, s5_log_dt=s5_log_dt, s5_b_re=s5_b_re,
                   s5_b_im=s5_b_im, s5_c_re=s5_c_re, s5_c_im=s5_c_im, s5_d=s5_d, s5_b_glu=s5_b_glu,
                   hgrn_lb_logits=hgrn_lb_logits, hgrn_norm_g=hgrn_norm_g, ln_ffn_g=ln_ffn_g, conv_b=conv_b,
                   ln_final_g=ln_final_g)
    small_m = dict(ln_mix_g=m_ln_mix_g, s5_a_re=m_s5_a_re, s5_a_im=m_s5_a_im, s5_log_dt=m_s5_log_dt, s5_b_re=m_s5_b_re,
                   s5_b_im=m_s5_b_im, s5_c_re=m_s5_c_re, s5_c_im=m_s5_c_im, s5_d=m_s5_d, s5_b_glu=m_s5_b_glu,
                   hgrn_lb_logits=m_hgrn_lb_logits, hgrn_norm_g=m_hgrn_norm_g, ln_ffn_g=m_ln_ffn_g, conv_b=m_conv_b,
                   ln_final_g=m_ln_final_g)
    small_v = dict(ln_mix_g=v_ln_mix_g, s5_a_re=v_s5_a_re, s5_a_im=v_s5_a_im, s5_log_dt=v_s5_log_dt, s5_b_re=v_s5_b_re,
                   s5_b_im=v_s5_b_im, s5_c_re=v_s5_c_re, s5_c_im=v_s5_c_im, s5_d=v_s5_d, s5_b_glu=v_s5_b_glu,
                   hgrn_lb_logits=v_hgrn_lb_logits, hgrn_norm_g=v_hgrn_norm_g, ln_ffn_g=v_ln_ffn_g, conv_b=v_conv_b,
                   ln_final_g=v_ln_final_g)
    small_g = dict(s5_a_re=d_are, s5_a_im=d_aim, s5_log_dt=d_ldt, s5_b_re=d_bre, s5_b_im=d_bim,
                   s5_c_re=d_cre, s5_c_im=d_cim, s5_d=s5g[7], s5_b_glu=d_bglu, hgrn_lb_logits=d_logits,
                   hgrn_norm_g=d_ng, ln_ffn_g=d_gffn, conv_b=d_cb, ln_final_g=d_gfin)
    like = [small_w[k] for k in small_names]
    assert small_names[-1] == "ln_mix_g"
    pack_a = _pack([small_g[k] for k in small_names[:-1]] + [d_cw])
    gw_in, (r_small_a,) = _mm(h1, dproj, "tn", _BF, tm=_tile(d, 1024), tn=_tile(nin_s // 2, 1152), tk=None, halves="cols",
                              name="mm_gw_in", rider=_exchange_plan([], pack_a))
    (s_in,) = chip_sums([gw_in], "swap_halves_b")
    dh1, (r_in,) = _mm(dproj, g_in, "nt", _BF, tm=tm_big, tn=_tile(d, 1024), tk=None, name="mm_dh1",
                       rider=_exchange_plan([s_in], None))
    dx, _, d_gmix = _rms_bwd(xt, ln_mix_g, dh1, dx2, "rms1_bwd")
    (r_small_b,) = _run_plan(_exchange_plan([], _pack([d_gmix])), "exchange_gmix")
    sums = [s_in, s_glu, s_ps5, s_ph, s_out, s_up, s_down]
    received = [r_in, r_glu_, r_ps5, r_ph, r_out, r_up, r_down]
    halves = [_shard_sum(sm, rc, sel_chip) for sm, rc in zip(sums, received)]
    g_a = _sum_slots(r_small_a, "small_sum_a")
    g_b = _sum_slots(r_small_b, "small_sum_b")
    full = _share_halves(halves)
    w_pack = _pack(like)
    rows_a = w_pack.shape[0] - g_b.shape[0]
    g_small = jnp.concatenate([g_a[:rows_a], g_b], axis=0)
    cs = conv_w.shape[2]
    g_cw_full = g_a[rows_a:].reshape(-1)[:3 * 2 * dff].reshape(3, 2 * dff)
    g_cw = lax.dynamic_slice_in_dim(g_cw_full, chip * cs, cs, axis=1)

    big_m = dict(w_in=m_w_in, s5_w_glu=m_s5_w_glu, w_proj_s5=m_w_proj_s5, w_proj_hgrn=m_w_proj_hgrn, w_out=m_w_out,
                 w_up=m_w_up, w_down=m_w_down)
    big_v = dict(w_in=v_w_in, s5_w_glu=v_s5_w_glu, w_proj_s5=v_w_proj_s5, w_proj_hgrn=v_w_proj_hgrn, w_out=v_w_out,
                 w_up=v_w_up, w_down=v_w_down)
    res = {}
    for k, g2 in zip(big_names, full):
        w2 = big_w[k]
        shp = (1,) + w2.shape
        outs = _adamw_shard(w2, g2, big_m[k][0], big_v[k][0], "adamw_" + k)
        res[k] = [o.reshape(shp) for o in outs]
    sm_outs = _adamw_many(like, g_small, [small_m[k] for k in small_names], [small_v[k] for k in small_names])
    for k, outs in zip(small_names, sm_outs):
        res[k] = outs

    cw_outs = _adamw_whole(conv_w[0], g_cw, m_conv_w[0], v_conv_w[0], "adamw_conv_w")
    res["conv_w"] = [g_cw.reshape(conv_w.shape)] + [o.reshape(conv_w.shape) for o in cw_outs]

    loss = lax.psum(loss_part[0, 0], ("x", "y", "c"))
    order = ["ln_mix_g", "w_in", "s5_a_re", "s5_a_im", "s5_log_dt", "s5_b_re", "s5_b_im", "s5_c_re", "s5_c_im", "s5_d",
             "s5_w_glu", "s5_b_glu", "w_proj_s5", "hgrn_lb_logits", "hgrn_norm_g", "w_proj_hgrn", "w_out", "ln_ffn_g",
             "w_up", "conv_w", "conv_b", "w_down", "ln_final_g"]
    return (loss, dx.reshape(x.shape), *[res[k][0] for k in order], *[res[k][1] for k in order],
            *[res[k][2] for k in order], *[res[k][3] for k in order])
```
